```python
import jax, jax.numpy as jnp
from jax import lax
import numpy as np

D_MODEL = 1024
BATCH = 8
SEQ = 2048
DEPTH = 2

PLE_DIM = 256
HEAD_DIM = 64
SB_WIDTH = D_MODEL // 4
SB_HEADS = SB_WIDTH // HEAD_DIM
MLA_WIDTH = D_MODEL // 2
MLA_HEADS = MLA_WIDTH // HEAD_DIM
MLA_NOPE_DIM = 64
MLA_ROPE_DIM = 32
MLA_V_DIM = HEAD_DIM
MLA_Q_RANK = 384
MLA_KV_RANK = 256
CONV_WIDTH = D_MODEL // 4
CONV_K = 3
MIX_WIDTH = SB_WIDTH + MLA_WIDTH + CONV_WIDTH
IN_SIZES = (SB_WIDTH, SB_WIDTH, SB_WIDTH,
            MLA_Q_RANK, MLA_KV_RANK, MLA_ROPE_DIM,
            CONV_WIDTH, CONV_WIDTH, CONV_WIDTH)
IN_WIDTH = sum(IN_SIZES)
D_FF = 2816
Q_BLOCK = 128
ROPE_BASE = 10000.0
EPS = 1e-6
NEG_INF = -1e30

kernel_name = "hybrid_sb_mla_shortconv_macaron"


def rmsnorm(x, g):
    xf = x.astype(jnp.float32)
    y = xf * lax.rsqrt(jnp.mean(xf * xf, axis=-1, keepdims=True) + EPS)
    return (y * g.astype(jnp.float32)).astype(x.dtype)


def swiglu(x, w_gate, w_up, w_down):
    return (jax.nn.silu(x @ w_gate) * (x @ w_up)) @ w_down


def rope(x, pos):
    half = x.shape[-1] // 2
    inv = ROPE_BASE ** (-jnp.arange(half, dtype=jnp.float32) / half)
    ang = pos.astype(jnp.float32)[..., None] * inv
    cos = jnp.cos(ang)[:, :, None, :]
    sin = jnp.sin(ang)[:, :, None, :]
    xf = x.astype(jnp.float32)
    x1, x2 = xf[..., :half], xf[..., half:]
    return jnp.concatenate([x1 * cos - x2 * sin, x2 * cos + x1 * sin], axis=-1).astype(x.dtype)


def stick_breaking_attention(q, k, v):
    S = q.shape[1]
    scale = q.shape[-1] ** -0.5
    outs = []
    for start in range(0, S, Q_BLOCK):
        end = start + Q_BLOCK
        z = jnp.einsum("bqhd,bkhd->bhqk", q[:, start:end], k[:, :end],
                       preferred_element_type=jnp.float32) * scale
        t_idx = start + jnp.arange(Q_BLOCK)[:, None]
        s_idx = jnp.arange(end)[None, :]
        mask = s_idx < t_idx
        log_not = jnp.where(mask, -jax.nn.softplus(z), 0.0)
        after = lax.cumsum(log_not, axis=3, reverse=True) - log_not
        w = jnp.where(mask, jnp.exp(jax.nn.log_sigmoid(z) + after), 0.0)
        outs.append(jnp.einsum("bhqk,bkhd->bqhd", w.astype(v.dtype), v[:, :end]))
    return jnp.concatenate(outs, axis=1)


def latent_attention(c_q, c_kv, k_rope_raw, pos, g_q, g_kv, w_uq, w_ukv):
    B, S = c_q.shape[0], c_q.shape[1]
    q = (rmsnorm(c_q, g_q) @ w_uq).reshape(B, S, MLA_HEADS, MLA_NOPE_DIM + MLA_ROPE_DIM)
    q_nope, q_rope = q[..., :MLA_NOPE_DIM], rope(q[..., MLA_NOPE_DIM:], pos)
    kv = (rmsnorm(c_kv, g_kv) @ w_ukv).reshape(B, S, MLA_HEADS, MLA_NOPE_DIM + MLA_V_DIM)
    k_nope, v = kv[..., :MLA_NOPE_DIM], kv[..., MLA_NOPE_DIM:]
    k_rope = rope(k_rope_raw[:, :, None, :], pos)[:, :, 0, :]
    scale = (MLA_NOPE_DIM + MLA_ROPE_DIM) ** -0.5
    outs = []
    for start in range(0, S, Q_BLOCK):
        end = start + Q_BLOCK
        sc = (jnp.einsum("bqhd,bkhd->bhqk", q_nope[:, start:end], k_nope[:, :end],
                         preferred_element_type=jnp.float32)
              + jnp.einsum("bqhr,bkr->bhqk", q_rope[:, start:end], k_rope[:, :end],
                           preferred_element_type=jnp.float32)) * scale
        mask = jnp.arange(end)[None, :] <= (start + jnp.arange(Q_BLOCK)[:, None])
        w = jax.nn.softmax(jnp.where(mask, sc, NEG_INF), axis=-1)
        outs.append(jnp.einsum("bhqk,bkhd->bqhd", w.astype(v.dtype), v[:, :end]))
    return jnp.concatenate(outs, axis=1)


def short_gated_conv(b_gate, c_gate, h, w_conv):
    u = c_gate * h
    y = lax.conv_general_dilated(u, w_conv[:, None, :].astype(u.dtype), window_strides=(1,),
                                 padding=[(CONV_K - 1, 0)],
                                 dimension_numbers=("NWC", "WIO", "NWC"),
                                 feature_group_count=u.shape[-1])
    return b_gate * y


def _fwd_setup_inputs(seed: int = 0) -> dict:
    key = jax.random.key(seed)
    ks = iter(jax.random.split(key, 32))

    def dense(shape, fan_in):
        return jax.random.normal(next(ks), shape, jnp.float32) * fan_in ** -0.5

    def gain(dim):
        return 1.0 + 0.05 * jax.random.normal(next(ks), (DEPTH, dim), jnp.float32)

    x = jax.random.normal(next(ks), (BATCH, SEQ, D_MODEL), jnp.float32)
    p = jax.random.normal(next(ks), (DEPTH, BATCH, SEQ, PLE_DIM), jnp.float32)
    offsets = jax.random.randint(next(ks), (BATCH, 1), 0, 1024, dtype=jnp.int32)
    positions = offsets + jnp.arange(SEQ, dtype=jnp.int32)[None, :]
    return {
        "x": x, "p": p, "positions": positions,
        "g_ffn1_pre": gain(D_MODEL),
        "w1_gate": dense((DEPTH, D_MODEL, D_FF), D_MODEL),
        "w1_up": dense((DEPTH, D_MODEL, D_FF), D_MODEL),
        "w1_down": dense((DEPTH, D_FF, D_MODEL), D_FF),
        "g_ffn1_post": gain(D_MODEL),
        "g_mix_pre": gain(D_MODEL),
        "w_in": dense((DEPTH, D_MODEL, IN_WIDTH), D_MODEL),
        "g_mla_q": gain(MLA_Q_RANK),
        "w_mla_uq": dense((DEPTH, MLA_Q_RANK, MLA_HEADS * (MLA_NOPE_DIM + MLA_ROPE_DIM)), MLA_Q_RANK),
        "g_mla_kv": gain(MLA_KV_RANK),
        "w_mla_ukv": dense((DEPTH, MLA_KV_RANK, MLA_HEADS * (MLA_NOPE_DIM + MLA_V_DIM)), MLA_KV_RANK),
        "w_conv": dense((DEPTH, CONV_K, CONV_WIDTH), CONV_K),
        "w_out": dense((DEPTH, MIX_WIDTH, D_MODEL), MIX_WIDTH),
        "g_mix_post": gain(D_MODEL),
        "g_ffn2_pre": gain(D_MODEL),
        "w2_gate": dense((DEPTH, D_MODEL, D_FF), D_MODEL),
        "w2_up": dense((DEPTH, D_MODEL, D_FF), D_MODEL),
        "w2_down": dense((DEPTH, D_FF, D_MODEL), D_FF),
        "g_ffn2_post": gain(D_MODEL),
        "g_ple_pre": gain(D_MODEL),
        "w_ple_gate": dense((DEPTH, D_MODEL, D_MODEL), D_MODEL),
        "w_ple_proj": dense((DEPTH, PLE_DIM, D_MODEL), PLE_DIM),
        "g_ple_post": gain(D_MODEL),
    }


def _fwd_reference(x, p, positions, g_ffn1_pre, w1_gate, w1_up, w1_down, g_ffn1_post,
              g_mix_pre, w_in, g_mla_q, w_mla_uq, g_mla_kv, w_mla_ukv, w_conv, w_out,
              g_mix_post, g_ffn2_pre, w2_gate, w2_up, w2_down, g_ffn2_post,
              g_ple_pre, w_ple_gate, w_ple_proj, g_ple_post):
    B, S = x.shape[0], x.shape[1]
    split_at = [int(i) for i in np.cumsum(IN_SIZES)[:-1]]
    for i in range(DEPTH):
        h = rmsnorm(x, g_ffn1_pre[i])
        x = x + 0.5 * rmsnorm(swiglu(h, w1_gate[i], w1_up[i], w1_down[i]), g_ffn1_post[i])

        h = rmsnorm(x, g_mix_pre[i])
        (sb_q, sb_k, sb_v, c_q, c_kv, k_rope_raw,
         cv_b, cv_c, cv_h) = jnp.split(h @ w_in[i], split_at, axis=-1)
        heads = lambda t: t.reshape(B, S, SB_HEADS, HEAD_DIM)
        y_sb = stick_breaking_attention(heads(sb_q), heads(sb_k), heads(sb_v))
        y_mla = latent_attention(c_q, c_kv, k_rope_raw, positions, g_mla_q[i], g_mla_kv[i],
                                 w_mla_uq[i], w_mla_ukv[i])
        y_cv = short_gated_conv(cv_b, cv_c, cv_h, w_conv[i])
        mixed = jnp.concatenate([y_sb.reshape(B, S, SB_WIDTH),
                                 y_mla.reshape(B, S, MLA_WIDTH), y_cv], axis=-1) @ w_out[i]
        x = x + rmsnorm(mixed, g_mix_post[i])

        h = rmsnorm(x, g_ffn2_pre[i])
        x = x + 0.5 * rmsnorm(swiglu(h, w2_gate[i], w2_up[i], w2_down[i]), g_ffn2_post[i])

        h = rmsnorm(x, g_ple_pre[i])
        e = jax.nn.sigmoid(h @ w_ple_gate[i]) * (p[i].astype(x.dtype) @ w_ple_proj[i])
        x = x + rmsnorm(e, g_ple_post[i])
    return x


import jax as _jax
import jax.numpy as _jnp

TWIN_FORMAT = 'train_step'
FWD_PARAMS = ['x', 'p', 'positions', 'g_ffn1_pre', 'w1_gate', 'w1_up', 'w1_down', 'g_ffn1_post', 'g_mix_pre', 'w_in', 'g_mla_q', 'w_mla_uq', 'g_mla_kv', 'w_mla_ukv', 'w_conv', 'w_out', 'g_mix_post', 'g_ffn2_pre', 'w2_gate', 'w2_up', 'w2_down', 'g_ffn2_post', 'g_ple_pre', 'w_ple_gate', 'w_ple_proj', 'g_ple_post']
TWIN_WEIGHTS = ['g_ffn1_pre', 'w1_gate', 'w1_up', 'w1_down', 'g_ffn1_post', 'g_mix_pre', 'w_in', 'g_mla_q', 'w_mla_uq', 'g_mla_kv', 'w_mla_ukv', 'w_conv', 'w_out', 'g_mix_post', 'g_ffn2_pre', 'w2_gate', 'w2_up', 'w2_down', 'g_ffn2_post', 'g_ple_pre', 'w_ple_gate', 'w_ple_proj', 'g_ple_post']
TWIN_DIFF_INPUT = 'x'
TWIN_INPUTS = ['x', 'p', 'positions', 'g_ffn1_pre', 'w1_gate', 'w1_up', 'w1_down', 'g_ffn1_post', 'g_mix_pre', 'w_in', 'g_mla_q', 'w_mla_uq', 'g_mla_kv', 'w_mla_ukv', 'w_conv', 'w_out', 'g_mix_post', 'g_ffn2_pre', 'w2_gate', 'w2_up', 'w2_down', 'g_ffn2_post', 'g_ple_pre', 'w_ple_gate', 'w_ple_proj', 'g_ple_post', 'loss_target', 'm_g_ffn1_pre', 'm_w1_gate', 'm_w1_up', 'm_w1_down', 'm_g_ffn1_post', 'm_g_mix_pre', 'm_w_in', 'm_g_mla_q', 'm_w_mla_uq', 'm_g_mla_kv', 'm_w_mla_ukv', 'm_w_conv', 'm_w_out', 'm_g_mix_post', 'm_g_ffn2_pre', 'm_w2_gate', 'm_w2_up', 'm_w2_down', 'm_g_ffn2_post', 'm_g_ple_pre', 'm_w_ple_gate', 'm_w_ple_proj', 'm_g_ple_post', 'v_g_ffn1_pre', 'v_w1_gate', 'v_w1_up', 'v_w1_down', 'v_g_ffn1_post', 'v_g_mix_pre', 'v_w_in', 'v_g_mla_q', 'v_w_mla_uq', 'v_g_mla_kv', 'v_w_mla_ukv', 'v_w_conv', 'v_w_out', 'v_g_mix_post', 'v_g_ffn2_pre', 'v_w2_gate', 'v_w2_up', 'v_w2_down', 'v_g_ffn2_post', 'v_g_ple_pre', 'v_w_ple_gate', 'v_w_ple_proj', 'v_g_ple_post']
TWIN_OUTPUTS = ['loss', 'grad_x', 'grad_g_ffn1_pre', 'grad_w1_gate', 'grad_w1_up', 'grad_w1_down', 'grad_g_ffn1_post', 'grad_g_mix_pre', 'grad_w_in', 'grad_g_mla_q', 'grad_w_mla_uq', 'grad_g_mla_kv', 'grad_w_mla_ukv', 'grad_w_conv', 'grad_w_out', 'grad_g_mix_post', 'grad_g_ffn2_pre', 'grad_w2_gate', 'grad_w2_up', 'grad_w2_down', 'grad_g_ffn2_post', 'grad_g_ple_pre', 'grad_w_ple_gate', 'grad_w_ple_proj', 'grad_g_ple_post', 'delta_g_ffn1_pre', 'delta_w1_gate', 'delta_w1_up', 'delta_w1_down', 'delta_g_ffn1_post', 'delta_g_mix_pre', 'delta_w_in', 'delta_g_mla_q', 'delta_w_mla_uq', 'delta_g_mla_kv', 'delta_w_mla_ukv', 'delta_w_conv', 'delta_w_out', 'delta_g_mix_post', 'delta_g_ffn2_pre', 'delta_w2_gate', 'delta_w2_up', 'delta_w2_down', 'delta_g_ffn2_post', 'delta_g_ple_pre', 'delta_w_ple_gate', 'delta_w_ple_proj', 'delta_g_ple_post', 'new_m_g_ffn1_pre', 'new_m_w1_gate', 'new_m_w1_up', 'new_m_w1_down', 'new_m_g_ffn1_post', 'new_m_g_mix_pre', 'new_m_w_in', 'new_m_g_mla_q', 'new_m_w_mla_uq', 'new_m_g_mla_kv', 'new_m_w_mla_ukv', 'new_m_w_conv', 'new_m_w_out', 'new_m_g_mix_post', 'new_m_g_ffn2_pre', 'new_m_w2_gate', 'new_m_w2_up', 'new_m_w2_down', 'new_m_g_ffn2_post', 'new_m_g_ple_pre', 'new_m_w_ple_gate', 'new_m_w_ple_proj', 'new_m_g_ple_post', 'new_v_g_ffn1_pre', 'new_v_w1_gate', 'new_v_w1_up', 'new_v_w1_down', 'new_v_g_ffn1_post', 'new_v_g_mix_pre', 'new_v_w_in', 'new_v_g_mla_q', 'new_v_w_mla_uq', 'new_v_g_mla_kv', 'new_v_w_mla_ukv', 'new_v_w_conv', 'new_v_w_out', 'new_v_g_mix_post', 'new_v_g_ffn2_pre', 'new_v_w2_gate', 'new_v_w2_up', 'new_v_w2_down', 'new_v_g_ffn2_post', 'new_v_g_ple_pre', 'new_v_w_ple_gate', 'new_v_w_ple_proj', 'new_v_g_ple_post']
TWIN_LEAF_KINDS = {'loss': 'loss', 'grad_x': 'grad_x', 'grad_g_ffn1_pre': 'grad_w', 'grad_w1_gate': 'grad_w', 'grad_w1_up': 'grad_w', 'grad_w1_down': 'grad_w', 'grad_g_ffn1_post': 'grad_w', 'grad_g_mix_pre': 'grad_w', 'grad_w_in': 'grad_w', 'grad_g_mla_q': 'grad_w', 'grad_w_mla_uq': 'grad_w', 'grad_g_mla_kv': 'grad_w', 'grad_w_mla_ukv': 'grad_w', 'grad_w_conv': 'grad_w', 'grad_w_out': 'grad_w', 'grad_g_mix_post': 'grad_w', 'grad_g_ffn2_pre': 'grad_w', 'grad_w2_gate': 'grad_w', 'grad_w2_up': 'grad_w', 'grad_w2_down': 'grad_w', 'grad_g_ffn2_post': 'grad_w', 'grad_g_ple_pre': 'grad_w', 'grad_w_ple_gate': 'grad_w', 'grad_w_ple_proj': 'grad_w', 'grad_g_ple_post': 'grad_w', 'delta_g_ffn1_pre': 'delta_w', 'delta_w1_gate': 'delta_w', 'delta_w1_up': 'delta_w', 'delta_w1_down': 'delta_w', 'delta_g_ffn1_post': 'delta_w', 'delta_g_mix_pre': 'delta_w', 'delta_w_in': 'delta_w', 'delta_g_mla_q': 'delta_w', 'delta_w_mla_uq': 'delta_w', 'delta_g_mla_kv': 'delta_w', 'delta_w_mla_ukv': 'delta_w', 'delta_w_conv': 'delta_w', 'delta_w_out': 'delta_w', 'delta_g_mix_post': 'delta_w', 'delta_g_ffn2_pre': 'delta_w', 'delta_w2_gate': 'delta_w', 'delta_w2_up': 'delta_w', 'delta_w2_down': 'delta_w', 'delta_g_ffn2_post': 'delta_w', 'delta_g_ple_pre': 'delta_w', 'delta_w_ple_gate': 'delta_w', 'delta_w_ple_proj': 'delta_w', 'delta_g_ple_post': 'delta_w', 'new_m_g_ffn1_pre': 'new_m', 'new_m_w1_gate': 'new_m', 'new_m_w1_up': 'new_m', 'new_m_w1_down': 'new_m', 'new_m_g_ffn1_post': 'new_m', 'new_m_g_mix_pre': 'new_m', 'new_m_w_in': 'new_m', 'new_m_g_mla_q': 'new_m', 'new_m_w_mla_uq': 'new_m', 'new_m_g_mla_kv': 'new_m', 'new_m_w_mla_ukv': 'new_m', 'new_m_w_conv': 'new_m', 'new_m_w_out': 'new_m', 'new_m_g_mix_post': 'new_m', 'new_m_g_ffn2_pre': 'new_m', 'new_m_w2_gate': 'new_m', 'new_m_w2_up': 'new_m', 'new_m_w2_down': 'new_m', 'new_m_g_ffn2_post': 'new_m', 'new_m_g_ple_pre': 'new_m', 'new_m_w_ple_gate': 'new_m', 'new_m_w_ple_proj': 'new_m', 'new_m_g_ple_post': 'new_m', 'new_v_g_ffn1_pre': 'new_v', 'new_v_w1_gate': 'new_v', 'new_v_w1_up': 'new_v', 'new_v_w1_down': 'new_v', 'new_v_g_ffn1_post': 'new_v', 'new_v_g_mix_pre': 'new_v', 'new_v_w_in': 'new_v', 'new_v_g_mla_q': 'new_v', 'new_v_w_mla_uq': 'new_v', 'new_v_g_mla_kv': 'new_v', 'new_v_w_mla_ukv': 'new_v', 'new_v_w_conv': 'new_v', 'new_v_w_out': 'new_v', 'new_v_g_mix_post': 'new_v', 'new_v_g_ffn2_pre': 'new_v', 'new_v_w2_gate': 'new_v', 'new_v_w2_up': 'new_v', 'new_v_w2_down': 'new_v', 'new_v_g_ffn2_post': 'new_v', 'new_v_g_ple_pre': 'new_v', 'new_v_w_ple_gate': 'new_v', 'new_v_w_ple_proj': 'new_v', 'new_v_g_ple_post': 'new_v'}


def _forward(args):
    return _fwd_reference(*[args[k] for k in FWD_PARAMS])


def _output_shape():
    out = _jax.eval_shape(lambda: _forward(_fwd_setup_inputs(0)))
    return out.shape, out.dtype

N_MICROBATCH = 1
ADAM_LR = 0.001
ADAM_B1 = 0.9
ADAM_B2 = 0.999
ADAM_EPS = 1e-08
ADAM_WD = 0.01
ADAM_STEP = 10
PER_EXAMPLE_BATCH_AXIS = {'x': 0, 'p': 1, 'positions': 0, 'loss_target': 0}
SHARED_INPUTS = []
_WEIGHT_DTYPES = {'g_ffn1_pre': _jnp.float32, 'w1_gate': _jnp.float32, 'w1_up': _jnp.float32, 'w1_down': _jnp.float32, 'g_ffn1_post': _jnp.float32, 'g_mix_pre': _jnp.float32, 'w_in': _jnp.float32, 'g_mla_q': _jnp.float32, 'w_mla_uq': _jnp.float32, 'g_mla_kv': _jnp.float32, 'w_mla_ukv': _jnp.float32, 'w_conv': _jnp.float32, 'w_out': _jnp.float32, 'g_mix_post': _jnp.float32, 'g_ffn2_pre': _jnp.float32, 'w2_gate': _jnp.float32, 'w2_up': _jnp.float32, 'w2_down': _jnp.float32, 'g_ffn2_post': _jnp.float32, 'g_ple_pre': _jnp.float32, 'w_ple_gate': _jnp.float32, 'w_ple_proj': _jnp.float32, 'g_ple_post': _jnp.float32}
MOMENT_SCALE = {'g_ffn1_pre': 5.466902e-01, 'w1_gate': 2.283222e-01, 'w1_up': 2.298160e-01, 'w1_down': 3.807493e-01, 'g_ffn1_post': 3.793902e+00, 'g_mix_pre': 7.124608e-01, 'w_in': 4.860233e-01, 'g_mla_q': 1.551843e-01, 'w_mla_uq': 1.117606e-01, 'g_mla_kv': 3.068075e-01, 'w_mla_ukv': 1.493579e-01, 'w_conv': 7.425405e-01, 'w_out': 4.687202e-01, 'g_mix_post': 1.603115e+01, 'g_ffn2_pre': 3.299732e-01, 'w2_gate': 1.307318e-01, 'w2_up': 1.515947e-01, 'w2_down': 2.521386e-01, 'g_ffn2_post': 3.997430e+00, 'g_ple_pre': 1.583756e-01, 'w_ple_gate': 1.536839e-01, 'w_ple_proj': 4.040693e-01, 'g_ple_post': 1.601228e+01}


def _to_microbatches(a, axis):
    t = _jnp.moveaxis(a, axis, 0)
    t = t.reshape((N_MICROBATCH, t.shape[0] // N_MICROBATCH) + t.shape[1:])
    return _jnp.moveaxis(t, 1, axis + 1)


def setup_inputs(seed: int = 0) -> dict:
    inp = _fwd_setup_inputs(seed)
    key = _jax.random.fold_in(_jax.random.key(seed), 7919)
    shape, _ = _output_shape()
    out = dict(inp)
    out["loss_target"] = _jax.random.normal(_jax.random.fold_in(key, 0), shape, _jnp.float32)
    for i, name in enumerate(TWIN_WEIGHTS):
        w = inp[name].astype(_jnp.float32)
        if MOMENT_SCALE is None:
            s = _jnp.sqrt(_jnp.mean(_jnp.square(w)) + 1e-30)
        else:
            s = MOMENT_SCALE[name]
        km, kv = _jax.random.split(_jax.random.fold_in(key, i + 1))
        out[name] = w
        out["m_" + name] = s * _jax.random.normal(km, w.shape, _jnp.float32)
        out["v_" + name] = (s * s) * _jax.random.uniform(kv, w.shape, _jnp.float32, 0.5, 1.5)
    if N_MICROBATCH > 1:
        for name, axis in PER_EXAMPLE_BATCH_AXIS.items():
            out[name] = _to_microbatches(out[name], axis)
    return {'x': out['x'], 'p': out['p'], 'positions': out['positions'], 'g_ffn1_pre': out['g_ffn1_pre'], 'w1_gate': out['w1_gate'], 'w1_up': out['w1_up'], 'w1_down': out['w1_down'], 'g_ffn1_post': out['g_ffn1_post'], 'g_mix_pre': out['g_mix_pre'], 'w_in': out['w_in'], 'g_mla_q': out['g_mla_q'], 'w_mla_uq': out['w_mla_uq'], 'g_mla_kv': out['g_mla_kv'], 'w_mla_ukv': out['w_mla_ukv'], 'w_conv': out['w_conv'], 'w_out': out['w_out'], 'g_mix_post': out['g_mix_post'], 'g_ffn2_pre': out['g_ffn2_pre'], 'w2_gate': out['w2_gate'], 'w2_up': out['w2_up'], 'w2_down': out['w2_down'], 'g_ffn2_post': out['g_ffn2_post'], 'g_ple_pre': out['g_ple_pre'], 'w_ple_gate': out['w_ple_gate'], 'w_ple_proj': out['w_ple_proj'], 'g_ple_post': out['g_ple_post'], 'loss_target': out['loss_target'], 'm_g_ffn1_pre': out['m_g_ffn1_pre'], 'm_w1_gate': out['m_w1_gate'], 'm_w1_up': out['m_w1_up'], 'm_w1_down': out['m_w1_down'], 'm_g_ffn1_post': out['m_g_ffn1_post'], 'm_g_mix_pre': out['m_g_mix_pre'], 'm_w_in': out['m_w_in'], 'm_g_mla_q': out['m_g_mla_q'], 'm_w_mla_uq': out['m_w_mla_uq'], 'm_g_mla_kv': out['m_g_mla_kv'], 'm_w_mla_ukv': out['m_w_mla_ukv'], 'm_w_conv': out['m_w_conv'], 'm_w_out': out['m_w_out'], 'm_g_mix_post': out['m_g_mix_post'], 'm_g_ffn2_pre': out['m_g_ffn2_pre'], 'm_w2_gate': out['m_w2_gate'], 'm_w2_up': out['m_w2_up'], 'm_w2_down': out['m_w2_down'], 'm_g_ffn2_post': out['m_g_ffn2_post'], 'm_g_ple_pre': out['m_g_ple_pre'], 'm_w_ple_gate': out['m_w_ple_gate'], 'm_w_ple_proj': out['m_w_ple_proj'], 'm_g_ple_post': out['m_g_ple_post'], 'v_g_ffn1_pre': out['v_g_ffn1_pre'], 'v_w1_gate': out['v_w1_gate'], 'v_w1_up': out['v_w1_up'], 'v_w1_down': out['v_w1_down'], 'v_g_ffn1_post': out['v_g_ffn1_post'], 'v_g_mix_pre': out['v_g_mix_pre'], 'v_w_in': out['v_w_in'], 'v_g_mla_q': out['v_g_mla_q'], 'v_w_mla_uq': out['v_w_mla_uq'], 'v_g_mla_kv': out['v_g_mla_kv'], 'v_w_mla_ukv': out['v_w_mla_ukv'], 'v_w_conv': out['v_w_conv'], 'v_w_out': out['v_w_out'], 'v_g_mix_post': out['v_g_mix_post'], 'v_g_ffn2_pre': out['v_g_ffn2_pre'], 'v_w2_gate': out['v_w2_gate'], 'v_w2_up': out['v_w2_up'], 'v_w2_down': out['v_w2_down'], 'v_g_ffn2_post': out['v_g_ffn2_post'], 'v_g_ple_pre': out['v_g_ple_pre'], 'v_w_ple_gate': out['v_w_ple_gate'], 'v_w_ple_proj': out['v_w_ple_proj'], 'v_g_ple_post': out['v_g_ple_post']}


def _loss(weights, diff, rest, loss_target):
    with _jax.named_scope("forward"):
        args = {**rest, TWIN_DIFF_INPUT: diff, **{k: w.astype(_WEIGHT_DTYPES[k]) for k, w in weights.items()}}
        y = _forward(args)
    with _jax.named_scope("loss_head"):
        err = _jnp.square(y.astype(_jnp.float32) - loss_target)
        return 0.5 * _jnp.sum(_jnp.mean(err, axis=-1)) if err.ndim else 0.5 * err


def _adamw(w, g, m, v):
    m = ADAM_B1 * m + (1.0 - ADAM_B1) * g
    v = ADAM_B2 * v + (1.0 - ADAM_B2) * _jnp.square(g)
    m_hat = m / (1.0 - ADAM_B1 ** ADAM_STEP)
    v_hat = v / (1.0 - ADAM_B2 ** ADAM_STEP)
    delta = -ADAM_LR * (m_hat / (_jnp.sqrt(v_hat) + ADAM_EPS) + ADAM_WD * w)
    return delta, m, v


def reference(x, p, positions, g_ffn1_pre, w1_gate, w1_up, w1_down, g_ffn1_post, g_mix_pre, w_in, g_mla_q, w_mla_uq, g_mla_kv, w_mla_ukv, w_conv, w_out, g_mix_post, g_ffn2_pre, w2_gate, w2_up, w2_down, g_ffn2_post, g_ple_pre, w_ple_gate, w_ple_proj, g_ple_post, loss_target, m_g_ffn1_pre, m_w1_gate, m_w1_up, m_w1_down, m_g_ffn1_post, m_g_mix_pre, m_w_in, m_g_mla_q, m_w_mla_uq, m_g_mla_kv, m_w_mla_ukv, m_w_conv, m_w_out, m_g_mix_post, m_g_ffn2_pre, m_w2_gate, m_w2_up, m_w2_down, m_g_ffn2_post, m_g_ple_pre, m_w_ple_gate, m_w_ple_proj, m_g_ple_post, v_g_ffn1_pre, v_w1_gate, v_w1_up, v_w1_down, v_g_ffn1_post, v_g_mix_pre, v_w_in, v_g_mla_q, v_w_mla_uq, v_g_mla_kv, v_w_mla_ukv, v_w_conv, v_w_out, v_g_mix_post, v_g_ffn2_pre, v_w2_gate, v_w2_up, v_w2_down, v_g_ffn2_post, v_g_ple_pre, v_w_ple_gate, v_w_ple_proj, v_g_ple_post):
    given = dict(x=x, p=p, positions=positions, g_ffn1_pre=g_ffn1_pre, w1_gate=w1_gate, w1_up=w1_up, w1_down=w1_down, g_ffn1_post=g_ffn1_post, g_mix_pre=g_mix_pre, w_in=w_in, g_mla_q=g_mla_q, w_mla_uq=w_mla_uq, g_mla_kv=g_mla_kv, w_mla_ukv=w_mla_ukv, w_conv=w_conv, w_out=w_out, g_mix_post=g_mix_post, g_ffn2_pre=g_ffn2_pre, w2_gate=w2_gate, w2_up=w2_up, w2_down=w2_down, g_ffn2_post=g_ffn2_post, g_ple_pre=g_ple_pre, w_ple_gate=w_ple_gate, w_ple_proj=w_ple_proj, g_ple_post=g_ple_post, loss_target=loss_target, m_g_ffn1_pre=m_g_ffn1_pre, m_w1_gate=m_w1_gate, m_w1_up=m_w1_up, m_w1_down=m_w1_down, m_g_ffn1_post=m_g_ffn1_post, m_g_mix_pre=m_g_mix_pre, m_w_in=m_w_in, m_g_mla_q=m_g_mla_q, m_w_mla_uq=m_w_mla_uq, m_g_mla_kv=m_g_mla_kv, m_w_mla_ukv=m_w_mla_ukv, m_w_conv=m_w_conv, m_w_out=m_w_out, m_g_mix_post=m_g_mix_post, m_g_ffn2_pre=m_g_ffn2_pre, m_w2_gate=m_w2_gate, m_w2_up=m_w2_up, m_w2_down=m_w2_down, m_g_ffn2_post=m_g_ffn2_post, m_g_ple_pre=m_g_ple_pre, m_w_ple_gate=m_w_ple_gate, m_w_ple_proj=m_w_ple_proj, m_g_ple_post=m_g_ple_post, v_g_ffn1_pre=v_g_ffn1_pre, v_w1_gate=v_w1_gate, v_w1_up=v_w1_up, v_w1_down=v_w1_down, v_g_ffn1_post=v_g_ffn1_post, v_g_mix_pre=v_g_mix_pre, v_w_in=v_w_in, v_g_mla_q=v_g_mla_q, v_w_mla_uq=v_w_mla_uq, v_g_mla_kv=v_g_mla_kv, v_w_mla_ukv=v_w_mla_ukv, v_w_conv=v_w_conv, v_w_out=v_w_out, v_g_mix_post=v_g_mix_post, v_g_ffn2_pre=v_g_ffn2_pre, v_w2_gate=v_w2_gate, v_w2_up=v_w2_up, v_w2_down=v_w2_down, v_g_ffn2_post=v_g_ffn2_post, v_g_ple_pre=v_g_ple_pre, v_w_ple_gate=v_w_ple_gate, v_w_ple_proj=v_w_ple_proj, v_g_ple_post=v_g_ple_post)
    weights = {n: given[n] for n in TWIN_WEIGHTS}
    shared = {n: given[n] for n in SHARED_INPUTS}
    per_example = {n: given[n] for n in ['x', 'p', 'positions']}
    grad_fn = _jax.value_and_grad(_loss, argnums=(0, 1))

    def one_microbatch(ex, loss_target):
        ex = dict(ex)
        diff = ex.pop(TWIN_DIFF_INPUT)
        return grad_fn(weights, diff, {**shared, **ex}, loss_target)

    if N_MICROBATCH == 1:
        loss, (grad_w, grad_x) = one_microbatch(per_example, given["loss_target"])
    else:
        def body(carry, xs):
            loss_sum, grad_sum = carry
            l_k, (gw_k, gx_k) = one_microbatch(xs[0], xs[1])
            with _jax.named_scope("update"):
                return (loss_sum + l_k, _jax.tree.map(_jnp.add, grad_sum, gw_k)), gx_k

        init = (_jnp.zeros((), _jnp.float32), _jax.tree.map(_jnp.zeros_like, weights))
        (loss, grad_w), grad_x = _jax.lax.scan(body, init, (per_example, given["loss_target"]))
    with _jax.named_scope("update"):
        delta_w, new_m, new_v = {}, {}, {}
        for n in TWIN_WEIGHTS:
            delta_w[n], new_m[n], new_v[n] = _adamw(weights[n], grad_w[n], given["m_" + n], given["v_" + n])
    return (loss, grad_x, *[grad_w[n] for n in TWIN_WEIGHTS], *[delta_w[n] for n in TWIN_WEIGHTS],
            *[new_m[n] for n in TWIN_WEIGHTS], *[new_v[n] for n in TWIN_WEIGHTS])
```

```python
import functools

import jax
import jax.numpy as jnp
from jax import lax
from jax.experimental import pallas as pl
from jax.experimental.pallas import tpu as pltpu

F32 = jnp.float32
BF16 = jnp.bfloat16
MESH = pl.DeviceIdType.MESH
AXES = ("x", "y", "c")

D_MODEL = 1024
D_FF = 2816
N_CHIPS = 4
EPS = 1e-6
NEG_INF = -1e30
ROPE_BASE = 10000.0
ROPE_HALF = 16
SB_HEADS, MLA_HEADS, HEAD_DIM = 4, 8, 64
MLA_QK_PAD = 128
MLA_SCALE = 96.0 ** -0.5
SB_SCALE = 64.0 ** -0.5
IN_PAD = 2304
ADAM_LR, ADAM_B1, ADAM_B2, ADAM_EPS, ADAM_WD, ADAM_STEP = 0.001, 0.9, 0.999, 1e-08, 0.01, 10
PACK_W = 1024
VMEM_LIMIT = 48 * 2 ** 20

NT = (((1,), (1,)), ((), ()))
TN = (((0,), (0,)), ((), ()))

BIG = (("w1_gate", D_MODEL, D_FF, 1), ("w1_up", D_MODEL, D_FF, 1), ("w1_down", D_FF, D_MODEL, 0),
       ("w_in", D_MODEL, 2208, 1), ("w_mla_uq", 384, 768, 1), ("w_mla_ukv", 256, 1024, 1),
       ("w_out", D_MODEL, D_MODEL, 0),
       ("w2_gate", D_MODEL, D_FF, 1), ("w2_up", D_MODEL, D_FF, 1), ("w2_down", D_FF, D_MODEL, 0),
       ("w_ple_gate", D_MODEL, D_MODEL, 0), ("w_ple_proj", 256, D_MODEL, 1))
GAINS = (("g_ffn1_pre", 1024), ("g_ffn1_post", 1024), ("g_mix_pre", 1024), ("g_mla_q", 384),
         ("g_mla_kv", 256), ("g_mix_post", 1024), ("g_ffn2_pre", 1024), ("g_ffn2_post", 1024),
         ("g_ple_pre", 1024), ("g_ple_post", 1024))
WEIGHT_ORDER = ("g_ffn1_pre", "w1_gate", "w1_up", "w1_down", "g_ffn1_post", "g_mix_pre", "w_in", "g_mla_q",
                "w_mla_uq", "g_mla_kv", "w_mla_ukv", "w_conv", "w_out", "g_mix_post", "g_ffn2_pre", "w2_gate",
                "w2_up", "w2_down", "g_ffn2_post", "g_ple_pre", "w_ple_gate", "w_ple_proj", "g_ple_post")

_pcall = pl.pallas_call


def _params(sem=None):
    return pltpu.CompilerParams(dimension_semantics=sem, vmem_limit_bytes=VMEM_LIMIT)


def _dot(a, b, dims=None):
    if dims is None:
        return jnp.dot(a, b, preferred_element_type=F32)
    return lax.dot_general(a, b, dims, preferred_element_type=F32)


def _rstd(v):
    return lax.rsqrt(jnp.mean(v * v, axis=-1, keepdims=True) + EPS)


def _sigmoid(v):
    return 1.0 / (1.0 + jnp.exp(-v))


def _row_block(n, want):
    b = min(n, want)
    while n % b:
        b //= 2
    return b


def _mm(pairs, mode, out_dtype, name, bm=512, bn=512):
    a0, b0 = pairs[0]
    if mode == "nn":
        M, N = a0.shape[0], b0.shape[1]
    elif mode == "nt":
        M, N = a0.shape[0], b0.shape[0]
    else:
        M, N = a0.shape[1], b0.shape[1]
    bm, bn = _row_block(M, bm), _row_block(N, bn)
    n_pairs = len(pairs)

    def body(*refs):
        o_ref = refs[-1]
        acc = None
        for t in range(n_pairs):
            a, b = refs[2 * t][...], refs[2 * t + 1][...]
            part = _dot(a, b, {"nn": None, "nt": NT, "tn": TN}[mode])
            acc = part if acc is None else acc + part
        o_ref[...] = acc.astype(o_ref.dtype)

    in_specs, ops = [], []
    for a, b in pairs:
        if mode == "nn":
            in_specs += [pl.BlockSpec((bm, a.shape[1]), lambda j, i: (i, 0)),
                         pl.BlockSpec((b.shape[0], bn), lambda j, i: (0, j))]
        elif mode == "nt":
            in_specs += [pl.BlockSpec((bm, a.shape[1]), lambda j, i: (i, 0)),
                         pl.BlockSpec((bn, b.shape[1]), lambda j, i: (j, 0))]
        else:
            in_specs += [pl.BlockSpec((a.shape[0], bm), lambda j, i: (0, i)),
                         pl.BlockSpec((b.shape[0], bn), lambda j, i: (0, j))]
        ops += [a, b]
    return _pcall(body, name=name, grid=(N // bn, M // bm), in_specs=in_specs,
                  out_specs=pl.BlockSpec((bm, bn), lambda j, i: (i, j)),
                  out_shape=jax.ShapeDtypeStruct((M, N), out_dtype),
                  compiler_params=_params(("parallel", "parallel")))(*ops)


def _ew(fn, ins, outs, name, br=256):
    S = max(a.shape[0] for a in ins)
    br = _row_block(S, br)
    n_in = len(ins)

    def body(*refs):
        res = fn(*[r[...] for r in refs[:n_in]])
        if not isinstance(res, tuple):
            res = (res,)
        for r, v in zip(refs[n_in:], res):
            r[...] = v.astype(r.dtype)

    in_specs = [pl.BlockSpec((br, a.shape[1]), lambda i: (i, 0)) if a.shape[0] == S and S > 1
                else pl.BlockSpec(a.shape, lambda i: (0, 0)) for a in ins]
    out = _pcall(body, name=name, grid=(S // br,), in_specs=in_specs,
                 out_specs=tuple(pl.BlockSpec((br, w), lambda i: (i, 0)) for w, _ in outs),
                 out_shape=tuple(jax.ShapeDtypeStruct((S, w), dt) for w, dt in outs),
                 compiler_params=_params(("parallel",)))(*ins)
    return out if len(outs) > 1 else out[0]


def _norm_fwd(x, g, name):
    return _ew(lambda xv, gv: xv * _rstd(xv) * gv, [x, g], [(x.shape[1], BF16)], name, br=512)


def _resid_norm(x, y, g, alpha, name):
    return _ew(lambda xv, yv, gv: xv + alpha * (yv * _rstd(yv) * gv), [x, y, g], [(x.shape[1], F32)], name)


def _norm_bwd(xin, g, dy, alpha, resid, out_dtype, name):
    S, W = xin.shape
    br = _row_block(S, 256)
    has_res = resid is not None

    def body(*refs):
        x_ref, g_ref, dy_ref = refs[:3]
        dx_ref, dg_ref = refs[-2:]
        xv, dyv = x_ref[...], dy_ref[...] * alpha
        r = _rstd(xv)
        xh = xv * r
        u = dyv * g_ref[...]
        dx = r * (u - xh * jnp.mean(u * xh, axis=-1, keepdims=True))
        if has_res:
            dx = dx + refs[3][...]
        dx_ref[...] = dx.astype(dx_ref.dtype)
        part = jnp.sum(dyv * xh, axis=0, keepdims=True)

        @pl.when(pl.program_id(0) == 0)
        def _():
            dg_ref[...] = part

        @pl.when(pl.program_id(0) > 0)
        def _():
            dg_ref[...] += part

    row = pl.BlockSpec((br, W), lambda i: (i, 0))
    vec = pl.BlockSpec((1, W), lambda i: (0, 0))
    ops = [xin, g, dy] + ([resid] if has_res else [])
    return _pcall(body, name=name, grid=(S // br,), in_specs=[row, vec, row] + ([row] if has_res else []),
                  out_specs=(row, vec),
                  out_shape=(jax.ShapeDtypeStruct((S, W), out_dtype), jax.ShapeDtypeStruct((1, W), F32)),
                  compiler_params=_params(("arbitrary",)))(*ops)


def _ffn_gate_up(h, wg, wu, name):
    S, K = h.shape
    F = wg.shape[1]
    bm, bn = _row_block(S, 1024), 256

    def body(h_ref, wg_ref, wu_ref, g_ref, u_ref, a_ref):
        hv = h_ref[...]
        g = _dot(hv, wg_ref[...])
        u = _dot(hv, wu_ref[...])
        g_ref[...] = g.astype(BF16)
        u_ref[...] = u.astype(BF16)
        a_ref[...] = (g * _sigmoid(g) * u).astype(BF16)

    blk = pl.BlockSpec((bm, bn), lambda j, i: (i, j))
    wsp = pl.BlockSpec((K, bn), lambda j, i: (0, j))
    return _pcall(body, name=name, grid=(F // bn, S // bm),
                  in_specs=[pl.BlockSpec((bm, K), lambda j, i: (i, 0)), wsp, wsp],
                  out_specs=(blk, blk, blk), out_shape=(jax.ShapeDtypeStruct((S, F), BF16),) * 3,
                  compiler_params=_params(("parallel", "parallel")))(h, wg, wu)


def _ffn_bwd_mid(dy, wd, g, u, name):
    S, D = dy.shape
    F = wd.shape[0]
    bm, bn = _row_block(S, 1024), 256

    def body(dy_ref, wd_ref, g_ref, u_ref, dg_ref, du_ref):
        da = _dot(dy_ref[...], wd_ref[...], NT)
        gv, uv = g_ref[...].astype(F32), u_ref[...].astype(F32)
        s = _sigmoid(gv)
        dg_ref[...] = (da * uv * (s * (1.0 + gv * (1.0 - s)))).astype(BF16)
        du_ref[...] = (da * (gv * s)).astype(BF16)

    blk = pl.BlockSpec((bm, bn), lambda j, i: (i, j))
    return _pcall(body, name=name, grid=(F // bn, S // bm),
                  in_specs=[pl.BlockSpec((bm, D), lambda j, i: (i, 0)),
                            pl.BlockSpec((bn, D), lambda j, i: (j, 0)), blk, blk],
                  out_specs=(blk, blk), out_shape=(jax.ShapeDtypeStruct((S, F), BF16),) * 2,
                  compiler_params=_params(("parallel", "parallel")))(dy, wd, g, u)


def _loss_and_grad(xf, tgt):
    S, W = xf.shape
    br = _row_block(S, 256)

    def body(x_ref, t_ref, l_ref, d_ref):
        d = x_ref[...] - t_ref[...]
        d_ref[...] = d * (1.0 / W)
        part = 0.5 * jnp.sum(jnp.sum(d * d, axis=-1, keepdims=True) * (1.0 / W), axis=0, keepdims=True)

        @pl.when(pl.program_id(0) == 0)
        def _():
            l_ref[...] = part

        @pl.when(pl.program_id(0) > 0)
        def _():
            l_ref[...] += part

    row = pl.BlockSpec((br, W), lambda i: (i, 0))
    return _pcall(body, name="loss", grid=(S // br,), in_specs=[row, row],
                  out_specs=(pl.BlockSpec((1, 1), lambda i: (0, 0)), row),
                  out_shape=(jax.ShapeDtypeStruct((1, 1), F32), jax.ShapeDtypeStruct((S, W), F32)),
                  compiler_params=_params(("arbitrary",)))(xf, tgt)


def _adamw(w, g, m, v, name):
    R, C = w.shape
    br = _row_block(R, 256) if R % 8 == 0 else R

    def body(w_ref, g_ref, m_ref, v_ref, d_ref, nm_ref, nv_ref):
        gv = g_ref[...]
        nm = ADAM_B1 * m_ref[...] + (1.0 - ADAM_B1) * gv
        nv = ADAM_B2 * v_ref[...] + (1.0 - ADAM_B2) * (gv * gv)
        m_hat = nm / (1.0 - ADAM_B1 ** ADAM_STEP)
        v_hat = nv / (1.0 - ADAM_B2 ** ADAM_STEP)
        d_ref[...] = -ADAM_LR * (m_hat / (jnp.sqrt(v_hat) + ADAM_EPS) + ADAM_WD * w_ref[...])
        nm_ref[...] = nm
        nv_ref[...] = nv

    blk = pl.BlockSpec((br, C), lambda i: (i, 0))
    return _pcall(body, name=name, grid=(R // br,), in_specs=[blk] * 4, out_specs=(blk,) * 3,
                  out_shape=(jax.ShapeDtypeStruct((R, C), F32),) * 3,
                  compiler_params=_params(("parallel",)))(w, g, m, v)


def _rope_tables(pos, inv):
    return _ew(lambda p, iv: (jnp.cos(p.astype(F32) * iv), jnp.sin(p.astype(F32) * iv)), [pos, inv],
               [(ROPE_HALF, F32)] * 2, "rope_tables")


def _rope(x1, x2, cos, sin, name):
    w = x1.shape[1]
    return _ew(lambda a, b, c, s: (a * c - b * s, b * c + a * s), [x1, x2, cos, sin], [(w, F32)] * 2, name)


def _rope_bwd(d1, d2, cos, sin, name):
    w = d1.shape[1]
    return _ew(lambda a, b, c, s: (a * c + b * s, b * c - a * s), [d1, d2, cos, sin], [(w, F32)] * 2, name)


def _rope_bwd_headsum(d1, d2, cos, sin):
    H, S, w = d1.shape
    br = _row_block(S, 256)

    def body(a_ref, b_ref, c_ref, s_ref, o1_ref, o2_ref):
        a, b = jnp.sum(a_ref[...], axis=0), jnp.sum(b_ref[...], axis=0)
        c, s = c_ref[...], s_ref[...]
        o1_ref[...] = a * c + b * s
        o2_ref[...] = b * c - a * s

    hs = pl.BlockSpec((H, br, w), lambda i: (0, i, 0))
    row = pl.BlockSpec((br, w), lambda i: (i, 0))
    return _pcall(body, name="rope_bwd_k", grid=(S // br,), in_specs=[hs, hs, row, row], out_specs=(row, row),
                  out_shape=(jax.ShapeDtypeStruct((S, w), F32),) * 2,
                  compiler_params=_params(("parallel",)))(d1, d2, cos, sin)


def _shift_down(v, k, row):
    return jnp.where(row >= k, pltpu.roll(v, k, 0), 0.0)


def _shift_up(v, k, row):
    n = v.shape[0]
    return jnp.where(row < n - k, pltpu.roll(v, n - k, 0), 0.0)


def _conv_fwd(b, c, h, w):
    S, C = b.shape
    bc = 128

    def body(b_ref, c_ref, h_ref, w_ref, y_ref):
        u = c_ref[...] * h_ref[...]
        row = lax.broadcasted_iota(jnp.int32, u.shape, 0)
        wv = w_ref[...]
        conv = wv[0:1] * _shift_down(u, 2, row) + wv[1:2] * _shift_down(u, 1, row) + wv[2:3] * u
        y_ref[...] = b_ref[...] * conv

    col = pl.BlockSpec((S, bc), lambda j: (0, j))
    return _pcall(body, name="conv_fwd", grid=(C // bc,),
                  in_specs=[col, col, col, pl.BlockSpec((3, bc), lambda j: (0, j))], out_specs=col,
                  out_shape=jax.ShapeDtypeStruct((S, C), F32), compiler_params=_params(("parallel",)))(b, c, h, w)


def _conv_bwd(b, c, h, w, dy):
    S, C = b.shape
    bc = 128

    def body(b_ref, c_ref, h_ref, w_ref, dy_ref, db_ref, dc_ref, dh_ref, dw_ref):
        cv, hv, bv, dyv = c_ref[...], h_ref[...], b_ref[...], dy_ref[...]
        u = cv * hv
        row = lax.broadcasted_iota(jnp.int32, u.shape, 0)
        wv = w_ref[...]
        u1, u2 = _shift_down(u, 1, row), _shift_down(u, 2, row)
        conv = wv[0:1] * u2 + wv[1:2] * u1 + wv[2:3] * u
        db_ref[...] = dyv * conv
        dconv = dyv * bv
        du = wv[2:3] * dconv + wv[1:2] * _shift_up(dconv, 1, row) + wv[0:1] * _shift_up(dconv, 2, row)
        dc_ref[...] = du * hv
        dh_ref[...] = du * cv
        dw_ref[0:1, :] = jnp.sum(dconv * u2, axis=0, keepdims=True)
        dw_ref[1:2, :] = jnp.sum(dconv * u1, axis=0, keepdims=True)
        dw_ref[2:3, :] = jnp.sum(dconv * u, axis=0, keepdims=True)

    col = pl.BlockSpec((S, bc), lambda j: (0, j))
    wsp = pl.BlockSpec((3, bc), lambda j: (0, j))
    return _pcall(body, name="conv_bwd", grid=(C // bc,), in_specs=[col, col, col, wsp, col],
                  out_specs=(col, col, col, wsp),
                  out_shape=(jax.ShapeDtypeStruct((S, C), F32),) * 3 + (jax.ShapeDtypeStruct((3, C), F32),),
                  compiler_params=_params(("parallel",)))(b, c, h, w, dy)


def _mla_fwd(q, k, v):
    H, S, Dk = q.shape
    Dv = v.shape[2]
    bq = bk = _row_block(S, 256)

    def body(q_ref, k_ref, v_ref, o_ref, lse_ref):
        i = pl.program_id(1)
        qb = q_ref[...]
        row = i * bq + lax.broadcasted_iota(jnp.int32, (bq, bk), 0)
        col = lax.broadcasted_iota(jnp.int32, (bq, bk), 1)

        def step(j, carry):
            m, l, acc = carry
            off = pl.multiple_of(j * bk, bk)
            kb, vb = k_ref[pl.ds(off, bk), :], v_ref[pl.ds(off, bk), :]
            s = _dot(qb, kb, NT) * MLA_SCALE
            s = jnp.where(col + j * bk <= row, s, NEG_INF)
            m_new = jnp.maximum(m, jnp.max(s, axis=-1, keepdims=True))
            p = jnp.exp(s - m_new)
            a = jnp.exp(m - m_new)
            l = a * l + jnp.sum(p, axis=-1, keepdims=True)
            acc = a * acc + _dot(p.astype(BF16), vb)
            return m_new, l, acc

        init = (jnp.full((bq, 1), NEG_INF, F32), jnp.zeros((bq, 1), F32), jnp.zeros((bq, Dv), F32))
        m, l, acc = lax.fori_loop(0, i + 1, step, init)
        o_ref[...] = acc / l
        lse_ref[...] = m + jnp.log(l)

    return _pcall(body, name="mla_fwd", grid=(H, S // bq),
                  in_specs=[pl.BlockSpec((None, bq, Dk), lambda h, i: (h, i, 0)),
                            pl.BlockSpec((None, S, Dk), lambda h, i: (h, 0, 0)),
                            pl.BlockSpec((None, S, Dv), lambda h, i: (h, 0, 0))],
                  out_specs=(pl.BlockSpec((None, bq, Dv), lambda h, i: (h, i, 0)),
                             pl.BlockSpec((None, bq, 1), lambda h, i: (h, i, 0))),
                  out_shape=(jax.ShapeDtypeStruct((H, S, Dv), F32), jax.ShapeDtypeStruct((H, S, 1), F32)),
                  compiler_params=_params(("parallel", "parallel")))(q, k, v)


def _mla_bwd(q, k, v, o, lse, do):
    H, S, Dk = q.shape
    Dv = v.shape[2]
    bq = bk = _row_block(S, 256)

    def body(q_ref, k_ref, v_ref, o_ref, lse_ref, do_ref, dq_ref, dk_ref, dv_ref):
        i = pl.program_id(1)

        @pl.when(i == 0)
        def _():
            dk_ref[...] = jnp.zeros_like(dk_ref)
            dv_ref[...] = jnp.zeros_like(dv_ref)

        qb, dob = q_ref[...], do_ref[...]
        delta = jnp.sum(dob * o_ref[...], axis=-1, keepdims=True)
        do16 = dob.astype(BF16)
        lse_v = lse_ref[...]
        row = i * bq + lax.broadcasted_iota(jnp.int32, (bq, bk), 0)
        col = lax.broadcasted_iota(jnp.int32, (bq, bk), 1)

        def step(j, dq):
            off = pl.multiple_of(j * bk, bk)
            kb, vb = k_ref[pl.ds(off, bk), :], v_ref[pl.ds(off, bk), :]
            s = _dot(qb, kb, NT) * MLA_SCALE
            p = jnp.where(col + j * bk <= row, jnp.exp(s - lse_v), 0.0)
            dp = _dot(do16, vb, NT)
            ds16 = (p * (dp - delta) * MLA_SCALE).astype(BF16)
            dk_ref[pl.ds(off, bk), :] += _dot(ds16, qb, TN)
            dv_ref[pl.ds(off, bk), :] += _dot(p.astype(BF16), do16, TN)
            return dq + _dot(ds16, kb)

        dq_ref[...] = lax.fori_loop(0, i + 1, step, jnp.zeros((bq, Dk), F32))

    qsp = pl.BlockSpec((None, bq, Dk), lambda h, i: (h, i, 0))
    ksp = pl.BlockSpec((None, S, Dk), lambda h, i: (h, 0, 0))
    vsp = pl.BlockSpec((None, S, Dv), lambda h, i: (h, 0, 0))
    osp = pl.BlockSpec((None, bq, Dv), lambda h, i: (h, i, 0))
    return _pcall(body, name="mla_bwd", grid=(H, S // bq),
                  in_specs=[qsp, ksp, vsp, osp, pl.BlockSpec((None, bq, 1), lambda h, i: (h, i, 0)), osp],
                  out_specs=(qsp, ksp, vsp),
                  out_shape=(jax.ShapeDtypeStruct((H, S, Dk), F32), jax.ShapeDtypeStruct((H, S, Dk), F32),
                             jax.ShapeDtypeStruct((H, S, Dv), F32)),
                  compiler_params=_params(("parallel", "arbitrary")))(q, k, v, o, lse, do)


def _dot_exact(x, tri):
    h1 = x.astype(BF16)
    r1 = x - h1.astype(F32)
    h2 = r1.astype(BF16)
    h3 = (r1 - h2.astype(F32)).astype(BF16)
    return _dot(h1, tri) + _dot(h2, tri) + _dot(h3, tri)


def _sb_terms(qb, kb, mask):
    z = _dot(qb, kb, NT) * SB_SCALE
    sp = jnp.maximum(z, 0.0) + jnp.log(1.0 + jnp.exp(-jnp.abs(z)))
    return jnp.where(mask, -sp, 0.0), z - sp


def _sb_fwd(q, k, v):
    H, S, Dh = q.shape
    bq = bk = _row_block(S, 128)

    def body(q_ref, k_ref, v_ref, o_ref, t_ref):
        i = pl.program_id(1)
        qb = q_ref[...]
        row = i * bq + lax.broadcasted_iota(jnp.int32, (bq, bk), 0)
        col = lax.broadcasted_iota(jnp.int32, (bq, bk), 1)
        rr = lax.broadcasted_iota(jnp.int32, (bk, bk), 0)
        cc = lax.broadcasted_iota(jnp.int32, (bk, bk), 1)
        later = (rr > cc).astype(BF16)

        def step(t, carry):
            tail, acc = carry
            j = i - t
            off = pl.multiple_of(j * bk, bk)
            kb, vb = k_ref[pl.ds(off, bk), :], v_ref[pl.ds(off, bk), :]
            mask = col + j * bk < row
            lnot, lsig = _sb_terms(qb, kb, mask)
            after = tail + _dot_exact(lnot, later)
            w = jnp.where(mask, jnp.exp(lsig + after), 0.0)
            return tail + jnp.sum(lnot, axis=-1, keepdims=True), acc + _dot(w.astype(BF16), vb)

        tail, acc = lax.fori_loop(0, i + 1, step, (jnp.zeros((bq, 1), F32), jnp.zeros((bq, Dh), F32)))
        o_ref[...] = acc
        t_ref[...] = tail

    qsp = pl.BlockSpec((None, bq, Dh), lambda h, i: (h, i, 0))
    ksp = pl.BlockSpec((None, S, Dh), lambda h, i: (h, 0, 0))
    return _pcall(body, name="sb_fwd", grid=(H, S // bq), in_specs=[qsp, ksp, ksp],
                  out_specs=(qsp, pl.BlockSpec((None, bq, 1), lambda h, i: (h, i, 0))),
                  out_shape=(jax.ShapeDtypeStruct((H, S, Dh), F32), jax.ShapeDtypeStruct((H, S, 1), F32)),
                  compiler_params=_params(("parallel", "parallel")))(q, k, v)


def _sb_bwd(q, k, v, tot, do):
    H, S, Dh = q.shape
    bq = bk = _row_block(S, 128)

    def body(q_ref, k_ref, v_ref, t_ref, do_ref, dq_ref, dk_ref, dv_ref):
        i = pl.program_id(1)

        @pl.when(i == 0)
        def _():
            dk_ref[...] = jnp.zeros_like(dk_ref)
            dv_ref[...] = jnp.zeros_like(dv_ref)

        qb = q_ref[...]
        do16 = do_ref[...].astype(BF16)
        tot_v = t_ref[...]
        row = i * bq + lax.broadcasted_iota(jnp.int32, (bq, bk), 0)
        col = lax.broadcasted_iota(jnp.int32, (bq, bk), 1)
        rr = lax.broadcasted_iota(jnp.int32, (bk, bk), 0)
        cc = lax.broadcasted_iota(jnp.int32, (bk, bk), 1)
        upto = (rr <= cc).astype(BF16)
        before = (rr < cc).astype(BF16)

        def step(j, carry):
            head, ehead, dq = carry
            off = pl.multiple_of(j * bk, bk)
            kb, vb = k_ref[pl.ds(off, bk), :], v_ref[pl.ds(off, bk), :]
            mask = col + j * bk < row
            lnot, lsig = _sb_terms(qb, kb, mask)
            after = tot_v - (head + _dot_exact(lnot, upto))
            w = jnp.where(mask, jnp.exp(lsig + after), 0.0)
            e = w * _dot(do16, vb, NT)
            esum = ehead + _dot_exact(e, before)
            sig = jnp.exp(lsig)
            dz16 = (jnp.where(mask, e * (1.0 - sig) - sig * esum, 0.0) * SB_SCALE).astype(BF16)
            dk_ref[pl.ds(off, bk), :] += _dot(dz16, qb, TN)
            dv_ref[pl.ds(off, bk), :] += _dot(w.astype(BF16), do16, TN)
            return (head + jnp.sum(lnot, axis=-1, keepdims=True), ehead + jnp.sum(e, axis=-1, keepdims=True),
                    dq + _dot(dz16, kb))

        zero = jnp.zeros((bq, 1), F32)
        _, _, dq = lax.fori_loop(0, i + 1, step, (zero, zero, jnp.zeros((bq, Dh), F32)))
        dq_ref[...] = dq

    qsp = pl.BlockSpec((None, bq, Dh), lambda h, i: (h, i, 0))
    ksp = pl.BlockSpec((None, S, Dh), lambda h, i: (h, 0, 0))
    return _pcall(body, name="sb_bwd", grid=(H, S // bq),
                  in_specs=[qsp, ksp, ksp, pl.BlockSpec((None, bq, 1), lambda h, i: (h, i, 0)), qsp],
                  out_specs=(qsp, ksp, ksp), out_shape=(jax.ShapeDtypeStruct((H, S, Dh), F32),) * 3,
                  compiler_params=_params(("parallel", "arbitrary")))(q, k, v, tot, do)


_HBM = pl.BlockSpec(memory_space=pltpu.HBM)


def _place():
    x, y, c = lax.axis_index("x"), lax.axis_index("y"), lax.axis_index("c")
    other_chips = ((1 - x, y), (x, 1 - y), (1 - x, 1 - y))
    return x, y, c, other_chips


def _remote(src, dst, send_sems, recv_sems, k, dev):
    return pltpu.make_async_remote_copy(src_ref=src, dst_ref=dst, send_sem=send_sems.at[k], recv_sem=recv_sems.at[k],
                                        device_id=dev, device_id_type=MESH)


def _gather_weights(pk):
    _, Hn, W = pk.shape

    def body(pk_ref, g_ref, send_sems, recv_sems, local_sem):
        x, y, c, chips = _place()
        me, sib = 2 * x + y, (x, y, 1 - c)
        own = pltpu.make_async_copy(pk_ref, g_ref.at[me], local_sem)
        own.start()
        first = [_remote(pk_ref.at[c], g_ref.at[me, c], send_sems, recv_sems, k, (cx, cy, c))
                 for k, (cx, cy) in enumerate(chips)]
        for cp in first:
            cp.start()
        passed = []
        for k, (cx, cy) in enumerate(chips):
            blk = g_ref.at[2 * cx + cy, c]
            _remote(blk, blk, send_sems, recv_sems, k, (cx, cy, c)).wait_recv()
            fwd = _remote(blk, blk, send_sems, recv_sems, 3 + k, sib)
            fwd.start()
            passed.append(fwd)
        for k, (cx, cy) in enumerate(chips):
            blk = g_ref.at[2 * cx + cy, 1 - c]
            _remote(blk, blk, send_sems, recv_sems, 3 + k, sib).wait_recv()
        for cp in first + passed:
            cp.wait_send()
        own.wait()

    return _pcall(body, name="gather_weights", in_specs=[_HBM], out_specs=_HBM,
                  out_shape=jax.ShapeDtypeStruct((N_CHIPS, 2, Hn, W), pk.dtype),
                  scratch_shapes=[pltpu.SemaphoreType.DMA((6,)), pltpu.SemaphoreType.DMA((6,)),
                                  pltpu.SemaphoreType.DMA])(pk)


def _swap_halves(gp):
    _, R, W = gp.shape

    def body(gp_ref, got_ref, send_sems, recv_sems):
        x, y, c, _ = _place()
        cp = _remote(gp_ref.at[1 - c], got_ref, send_sems, recv_sems, 0, (x, y, 1 - c))
        cp.start()
        cp.wait()

    return _pcall(body, name="swap_halves", in_specs=[_HBM], out_specs=_HBM,
                  out_shape=jax.ShapeDtypeStruct((R, W), gp.dtype),
                  scratch_shapes=[pltpu.SemaphoreType.DMA((1,)), pltpu.SemaphoreType.DMA((1,))])(gp)


def _scatter_chip_sums(s1):
    _, Hn, W = s1.shape

    def body(s1_ref, got_ref, send_sems, recv_sems):
        x, y, c, chips = _place()
        cps = [_remote(s1_ref.at[2 * cx + cy], got_ref.at[k], send_sems, recv_sems, k, (cx, cy, c))
               for k, (cx, cy) in enumerate(chips)]
        for cp in cps:
            cp.start()
        for cp in cps:
            cp.wait()

    return _pcall(body, name="scatter_chip_sums", in_specs=[_HBM], out_specs=_HBM,
                  out_shape=jax.ShapeDtypeStruct((3, Hn, W), s1.dtype),
                  scratch_shapes=[pltpu.SemaphoreType.DMA((3,)), pltpu.SemaphoreType.DMA((3,))])(s1)


def _join_halves(fin):
    Hn, W = fin.shape

    def body(fin_ref, out_ref, send_sems, recv_sems, local_sem):
        x, y, c, _ = _place()
        own = pltpu.make_async_copy(fin_ref, out_ref.at[c], local_sem)
        own.start()
        cp = _remote(fin_ref, out_ref.at[c], send_sems, recv_sems, 0, (x, y, 1 - c))
        cp.start()
        _remote(fin_ref, out_ref.at[1 - c], send_sems, recv_sems, 0, (x, y, 1 - c)).wait_recv()
        cp.wait_send()
        own.wait()

    return _pcall(body, name="join_halves", in_specs=[_HBM], out_specs=_HBM,
                  out_shape=jax.ShapeDtypeStruct((2, Hn, W), fin.dtype),
                  scratch_shapes=[pltpu.SemaphoreType.DMA((1,)), pltpu.SemaphoreType.DMA((1,)),
                                  pltpu.SemaphoreType.DMA])(fin)


def _add_halves(gp, got, c_idx):
    _, R, W = gp.shape
    br = 784

    def body(c_ref, a_ref, b_ref, o_ref):
        o_ref[...] = (a_ref[...].astype(F32) + b_ref[...].astype(F32)).astype(o_ref.dtype)

    grid_spec = pltpu.PrefetchScalarGridSpec(
        num_scalar_prefetch=1, grid=(R // br,),
        in_specs=[pl.BlockSpec((None, br, W), lambda i, c_ref: (c_ref[0], i, 0)),
                  pl.BlockSpec((br, W), lambda i, c_ref: (i, 0))],
        out_specs=pl.BlockSpec((br, W), lambda i, c_ref: (i, 0)))
    return _pcall(body, name="add_halves", grid_spec=grid_spec, out_shape=jax.ShapeDtypeStruct((R, W), BF16),
                  compiler_params=_params(("parallel",)))(c_idx, gp, got)


def _add_chip_sums(s1, got, me_idx):
    _, Hn, W = s1.shape
    br = 784

    def body(me_ref, a_ref, b_ref, o_ref):
        acc = a_ref[...].astype(F32)
        for k in range(3):
            acc = acc + b_ref[k].astype(F32)
        o_ref[...] = acc

    grid_spec = pltpu.PrefetchScalarGridSpec(
        num_scalar_prefetch=1, grid=(Hn // br,),
        in_specs=[pl.BlockSpec((None, br, W), lambda i, me_ref: (me_ref[0], i, 0)),
                  pl.BlockSpec((3, br, W), lambda i, me_ref: (0, i, 0))],
        out_specs=pl.BlockSpec((br, W), lambda i, me_ref: (i, 0)))
    return _pcall(body, name="add_chip_sums", grid_spec=grid_spec, out_shape=jax.ShapeDtypeStruct((Hn, W), F32),
                  compiler_params=_params(("parallel",)))(me_idx, s1, got)


def _allreduce_small(v):
    R, W = v.shape

    def body(v_ref, o_ref, buf, send_sems, recv_sems):
        x, y, c, _ = _place()
        me = 4 * x + 2 * y + c
        buf[0] = v_ref[...]
        cps = []
        for r in range(1, 8):
            peer = (x if not r & 4 else 1 - x, y if not r & 2 else 1 - y, c if not r & 1 else 1 - c)
            cp = _remote(v_ref, buf.at[r], send_sems, recv_sems, r - 1, peer)
            cp.start()
            cps.append(cp)
        for cp in cps:
            cp.wait()
        acc = buf[me]
        for d in range(1, 8):
            acc = acc + buf[jnp.bitwise_xor(me, d)]
        o_ref[...] = acc

    return _pcall(body, name="allreduce_small", out_shape=jax.ShapeDtypeStruct((R, W), F32),
                  in_specs=[pl.BlockSpec(memory_space=pltpu.VMEM)], out_specs=pl.BlockSpec(memory_space=pltpu.VMEM),
                  scratch_shapes=[pltpu.VMEM((8, R, W), F32), pltpu.SemaphoreType.DMA((7,)),
                                  pltpu.SemaphoreType.DMA((7,))])(v)


def _pack_rows(name_k_n_axis):
    _, K, N, _ = name_k_n_axis
    return 2 * K * N // N_CHIPS // PACK_W


def _pack_shards(shards):
    return jnp.concatenate([shards[spec[0]].astype(BF16).reshape(_pack_rows(spec), PACK_W) for spec in BIG], axis=0)


def _unpack_gathered(g):
    out, r0 = {}, 0
    for spec in BIG:
        name, K, N, axis = spec
        r1 = r0 + _pack_rows(spec)
        blk = g[:, r0:r1]
        if axis == 1:
            out[name] = blk.reshape(N_CHIPS, 2, K, N // N_CHIPS).transpose(1, 2, 0, 3).reshape(2, K, N)
        else:
            out[name] = blk.reshape(N_CHIPS, 2, K // N_CHIPS, N).transpose(1, 0, 2, 3).reshape(2, K, N)
        r0 = r1
    return out


def _pack_full_grads(grads):
    parts = []
    for spec in BIG:
        name, K, N, axis = spec
        gw = grads[name]
        if axis == 1:
            gw = gw.reshape(2, K, N_CHIPS, N // N_CHIPS).transpose(2, 0, 1, 3)
        else:
            gw = gw.reshape(2, N_CHIPS, K // N_CHIPS, N).transpose(1, 0, 2, 3)
        parts.append(gw.reshape(N_CHIPS, _pack_rows(spec), PACK_W))
    return jnp.concatenate(parts, axis=1)


def _unpack_shard_grads(flat):
    out, r0 = {}, 0
    for spec in BIG:
        name, K, N, axis = spec
        r1 = r0 + _pack_rows(spec)
        shape = (2, K, N // N_CHIPS) if axis == 1 else (2, K // N_CHIPS, N)
        out[name] = flat[r0:r1].reshape(shape)
        r0 = r1
    return out


def _regroup_w_in(w):
    return jnp.concatenate([w[:, :1408], w[:, 1440:], w[:, 1408:1440], jnp.zeros((w.shape[0], 96), w.dtype)], axis=1)


def _ungroup_w_in(w):
    return jnp.concatenate([w[:, :1408], w[:, 2176:2208], w[:, 1408:2176]], axis=1)


def _regroup_w_uq(w):
    t = w.reshape(w.shape[0], MLA_HEADS, 96)
    return jnp.concatenate([t[:, :, :64].reshape(-1, 512), t[:, :, 64:80].reshape(-1, 128),
                            t[:, :, 80:].reshape(-1, 128)], axis=1)


def _ungroup_w_uq(w):
    K = w.shape[0]
    return jnp.concatenate([w[:, :512].reshape(K, MLA_HEADS, 64), w[:, 512:640].reshape(K, MLA_HEADS, 16),
                            w[:, 640:].reshape(K, MLA_HEADS, 16)], axis=2).reshape(K, 768)


def _regroup_w_ukv(w):
    t = w.reshape(w.shape[0], MLA_HEADS, 128)
    return jnp.concatenate([t[:, :, :64].reshape(-1, 512), t[:, :, 64:].reshape(-1, 512)], axis=1)


def _ungroup_w_ukv(w):
    K = w.shape[0]
    return jnp.concatenate([w[:, :512].reshape(K, MLA_HEADS, 64), w[:, 512:].reshape(K, MLA_HEADS, 64)],
                           axis=2).reshape(K, 1024)


def _to_heads(t, heads):
    S = t.shape[0]
    return t.reshape(S, heads, -1).transpose(1, 0, 2)


def _from_heads(t):
    return t.transpose(1, 0, 2).reshape(t.shape[1], -1)


def _ffn_fwd(x, g_pre, g_post, wg, wu, wd, tag):
    h = _norm_fwd(x, g_pre, "norm_fwd")
    G, U, A = _ffn_gate_up(h, wg, wu, "ffn_gate_up")
    Y = _mm([(A, wd)], "nn", F32, "ffn_down")
    return _resid_norm(x, Y, g_post, 0.5, "resid_norm"), dict(x=x, h=h, G=G, U=U, A=A, Y=Y)


def _ffn_bwd(dxo, sv, g_pre, g_post, wg, wu, wd):
    dY, dg_post = _norm_bwd(sv["Y"], g_post, dxo, 0.5, None, BF16, "norm_bwd_post")
    dG, dU = _ffn_bwd_mid(dY, wd, sv["G"], sv["U"], "ffn_bwd_mid")
    dwd = _mm([(sv["A"], dY)], "tn", BF16, "ffn_dw_down")
    dwg = _mm([(sv["h"], dG)], "tn", BF16, "ffn_dw_in")
    dwu = _mm([(sv["h"], dU)], "tn", BF16, "ffn_dw_in")
    dh = _mm([(dG, wg), (dU, wu)], "nt", F32, "ffn_dh")
    dx, dg_pre = _norm_bwd(sv["x"], g_pre, dh, 1.0, dxo, F32, "norm_bwd_pre")
    return dx, dg_pre, dg_post, dwg, dwu, dwd


def _layer_fwd(x, p_l, W, G, w_conv, cos, sin, cos8, sin8):
    S = x.shape[0]
    sv = {}
    x1, sv["ffn1"] = _ffn_fwd(x, G["g_ffn1_pre"], G["g_ffn1_post"], W["w1_gate"], W["w1_up"], W["w1_down"], 1)

    h2 = _norm_fwd(x1, G["g_mix_pre"], "norm_fwd")
    Z = _mm([(h2, W["w_in"])], "nn", F32, "mix_in")
    sbq, sbk, sbv = Z[:, 0:256], Z[:, 256:512], Z[:, 512:768]
    cq, ckv = Z[:, 768:1152], Z[:, 1152:1408]
    cvb, cvc, cvh = Z[:, 1408:1664], Z[:, 1664:1920], Z[:, 1920:2176]
    kr = Z[:, 2176:2208]

    q4, k4, v4 = (_to_heads(t.astype(BF16), SB_HEADS) for t in (sbq, sbk, sbv))
    o_sb, tot_sb = _sb_fwd(q4, k4, v4)

    nq = _norm_fwd(cq, G["g_mla_q"], "norm_fwd_q")
    Qf = _mm([(nq, W["w_mla_uq"])], "nn", F32, "mla_uq")
    nkv = _norm_fwd(ckv, G["g_mla_kv"], "norm_fwd_kv")
    KV = _mm([(nkv, W["w_mla_ukv"])], "nn", F32, "mla_ukv")
    qr1, qr2 = _rope(Qf[:, 512:640], Qf[:, 640:768], cos8, sin8, "rope_q")
    kr1, kr2 = _rope(kr[:, :16], kr[:, 16:], cos, sin, "rope_k")
    zpad = jnp.zeros((S, MLA_HEADS, 32), F32)
    qcat = jnp.concatenate([Qf[:, :512].reshape(S, MLA_HEADS, 64), qr1.reshape(S, MLA_HEADS, 16),
                            qr2.reshape(S, MLA_HEADS, 16), zpad], axis=2).transpose(1, 0, 2).astype(BF16)
    kcat = jnp.concatenate([KV[:, :512].reshape(S, MLA_HEADS, 64),
                            jnp.broadcast_to(kr1[:, None, :], (S, MLA_HEADS, 16)),
                            jnp.broadcast_to(kr2[:, None, :], (S, MLA_HEADS, 16)), zpad],
                           axis=2).transpose(1, 0, 2).astype(BF16)
    v8 = _to_heads(KV[:, 512:].astype(BF16), MLA_HEADS)
    o_mla, lse = _mla_fwd(qcat, kcat, v8)

    y_cv = _conv_fwd(cvb, cvc, cvh, w_conv)
    C = jnp.concatenate([_from_heads(o_sb), _from_heads(o_mla), y_cv], axis=1).astype(BF16)
    Mx = _mm([(C, W["w_out"])], "nn", F32, "mix_out")
    x2 = _resid_norm(x1, Mx, G["g_mix_post"], 1.0, "resid_norm")
    sv["mix"] = dict(x=x1, h=h2, q4=q4, k4=k4, v4=v4, tot_sb=tot_sb, cq=cq, ckv=ckv, nq=nq, nkv=nkv,
                     qcat=qcat, kcat=kcat, v8=v8, o_mla=o_mla, lse=lse, cvb=cvb, cvc=cvc, cvh=cvh, C=C, Mx=Mx)

    x3, sv["ffn2"] = _ffn_fwd(x2, G["g_ffn2_pre"], G["g_ffn2_post"], W["w2_gate"], W["w2_up"], W["w2_down"], 2)

    h4 = _norm_fwd(x3, G["g_ple_pre"], "norm_fwd")
    p16 = p_l.astype(BF16)
    Qg = _mm([(h4, W["w_ple_gate"])], "nn", F32, "ple_gate")
    Pp = _mm([(p16, W["w_ple_proj"])], "nn", F32, "ple_proj")
    e = _ew(lambda q, pp: _sigmoid(q) * pp, [Qg, Pp], [(D_MODEL, F32)], "ple_mul")
    x4 = _resid_norm(x3, e, G["g_ple_post"], 1.0, "resid_norm")
    sv["ple"] = dict(x=x3, h=h4, p16=p16, Qg=Qg, Pp=Pp, e=e)
    return x4, sv


def _layer_bwd(dx4, sv, W, G, w_conv, cos, sin, cos8, sin8):
    S = dx4.shape[0]
    gw, gg = {}, {}

    s = sv["ple"]
    de, gg["g_ple_post"] = _norm_bwd(s["e"], G["g_ple_post"], dx4, 1.0, None, F32, "norm_bwd_e")

    def ple_bwd(dev, q, pp):
        sg = _sigmoid(q)
        return dev * pp * sg * (1.0 - sg), dev * sg

    dQg, dPp = _ew(ple_bwd, [de, s["Qg"], s["Pp"]], [(D_MODEL, BF16)] * 2, "ple_mul_bwd")
    gw["w_ple_proj"] = _mm([(s["p16"], dPp)], "tn", BF16, "ple_dw_proj")
    gw["w_ple_gate"] = _mm([(s["h"], dQg)], "tn", BF16, "ple_dw_gate")
    dh4 = _mm([(dQg, W["w_ple_gate"])], "nt", F32, "ple_dh")
    dx3, gg["g_ple_pre"] = _norm_bwd(s["x"], G["g_ple_pre"], dh4, 1.0, dx4, F32, "norm_bwd_pre")

    dx2, gg["g_ffn2_pre"], gg["g_ffn2_post"], gw["w2_gate"], gw["w2_up"], gw["w2_down"] = _ffn_bwd(
        dx3, sv["ffn2"], G["g_ffn2_pre"], G["g_ffn2_post"], W["w2_gate"], W["w2_up"], W["w2_down"])

    s = sv["mix"]
    dM, gg["g_mix_post"] = _norm_bwd(s["Mx"], G["g_mix_post"], dx2, 1.0, None, BF16, "norm_bwd_post")
    dC = _mm([(dM, W["w_out"])], "nt", F32, "mix_out_dx")
    gw["w_out"] = _mm([(s["C"], dM)], "tn", BF16, "mix_out_dw")

    db, dc, dhh, gg["w_conv"] = _conv_bwd(s["cvb"], s["cvc"], s["cvh"], w_conv, dC[:, 768:])

    dqc, dkc, dv8 = _mla_bwd(s["qcat"], s["kcat"], s["v8"], s["o_mla"], s["lse"], _to_heads(dC[:, 256:768], MLA_HEADS))
    dqs = dqc.transpose(1, 0, 2)
    dqx1, dqx2 = _rope_bwd(dqs[:, :, 64:80].reshape(S, 128), dqs[:, :, 80:96].reshape(S, 128), cos8, sin8, "rope_q_bwd")
    dQf = jnp.concatenate([dqs[:, :, :64].reshape(S, 512), dqx1, dqx2], axis=1).astype(BF16)
    gw["w_mla_uq"] = _mm([(s["nq"], dQf)], "tn", BF16, "mla_uq_dw")
    dnq = _mm([(dQf, W["w_mla_uq"])], "nt", F32, "mla_uq_dx")
    dcq, gg["g_mla_q"] = _norm_bwd(s["cq"], G["g_mla_q"], dnq, 1.0, None, F32, "norm_bwd_q")
    dkr1, dkr2 = _rope_bwd_headsum(dkc[:, :, 64:80], dkc[:, :, 80:96], cos, sin)
    dkv = jnp.concatenate([_from_heads(dkc[:, :, :64]), _from_heads(dv8)], axis=1).astype(BF16)
    gw["w_mla_ukv"] = _mm([(s["nkv"], dkv)], "tn", BF16, "mla_ukv_dw")
    dnkv = _mm([(dkv, W["w_mla_ukv"])], "nt", F32, "mla_ukv_dx")
    dckv, gg["g_mla_kv"] = _norm_bwd(s["ckv"], G["g_mla_kv"], dnkv, 1.0, None, F32, "norm_bwd_kv")

    dq4, dk4, dv4 = _sb_bwd(s["q4"], s["k4"], s["v4"], s["tot_sb"], _to_heads(dC[:, :256], SB_HEADS))
    dZ = jnp.concatenate([_from_heads(dq4), _from_heads(dk4), _from_heads(dv4), dcq, dckv, db, dc, dhh,
                          dkr1, dkr2, jnp.zeros((S, 96), F32)], axis=1).astype(BF16)
    gw["w_in"] = _mm([(s["h"], dZ)], "tn", BF16, "mix_in_dw")
    dh2 = _mm([(dZ, W["w_in"])], "nt", F32, "mix_in_dx")
    dx1, gg["g_mix_pre"] = _norm_bwd(s["x"], G["g_mix_pre"], dh2, 1.0, dx2, F32, "norm_bwd_pre")

    dx0, gg["g_ffn1_pre"], gg["g_ffn1_post"], gw["w1_gate"], gw["w1_up"], gw["w1_down"] = _ffn_bwd(
        dx1, sv["ffn1"], G["g_ffn1_pre"], G["g_ffn1_post"], W["w1_gate"], W["w1_up"], W["w1_down"])
    return dx0, gw, gg


def _small_rows(n):
    return -(-n // 128)


def _pack_small(vecs):
    flat = jnp.concatenate([v.reshape(-1) for v in vecs])
    rows = -(-_small_rows(flat.shape[0]) // 8) * 8
    return jnp.pad(flat, (0, rows * 128 - flat.shape[0])).reshape(rows, 128)


def kernel(x, p, positions, g_ffn1_pre, w1_gate, w1_up, w1_down, g_ffn1_post, g_mix_pre, w_in, g_mla_q, w_mla_uq, g_mla_kv, w_mla_ukv, w_conv, w_out, g_mix_post, g_ffn2_pre, w2_gate, w2_up, w2_down, g_ffn2_post, g_ple_pre, w_ple_gate, w_ple_proj, g_ple_post, loss_target, m_g_ffn1_pre, m_w1_gate, m_w1_up, m_w1_down, m_g_ffn1_post, m_g_mix_pre, m_w_in, m_g_mla_q, m_w_mla_uq, m_g_mla_kv, m_w_mla_ukv, m_w_conv, m_w_out, m_g_mix_post, m_g_ffn2_pre, m_w2_gate, m_w2_up, m_w2_down, m_g_ffn2_post, m_g_ple_pre, m_w_ple_gate, m_w_ple_proj, m_g_ple_post, v_g_ffn1_pre, v_w1_gate, v_w1_up, v_w1_down, v_g_ffn1_post, v_g_mix_pre, v_w_in, v_g_mla_q, v_w_mla_uq, v_g_mla_kv, v_w_mla_ukv, v_w_conv, v_w_out, v_g_mix_post, v_g_ffn2_pre, v_w2_gate, v_w2_up, v_w2_down, v_g_ffn2_post, v_g_ple_pre, v_w_ple_gate, v_w_ple_proj, v_g_ple_post):
    w = dict(g_ffn1_pre=g_ffn1_pre, w1_gate=w1_gate, w1_up=w1_up, w1_down=w1_down, g_ffn1_post=g_ffn1_post,
             g_mix_pre=g_mix_pre, w_in=w_in, g_mla_q=g_mla_q, w_mla_uq=w_mla_uq, g_mla_kv=g_mla_kv,
             w_mla_ukv=w_mla_ukv, w_conv=w_conv, w_out=w_out, g_mix_post=g_mix_post, g_ffn2_pre=g_ffn2_pre,
             w2_gate=w2_gate, w2_up=w2_up, w2_down=w2_down, g_ffn2_post=g_ffn2_post, g_ple_pre=g_ple_pre,
             w_ple_gate=w_ple_gate, w_ple_proj=w_ple_proj, g_ple_post=g_ple_post)
    m = dict(g_ffn1_pre=m_g_ffn1_pre, w1_gate=m_w1_gate, w1_up=m_w1_up, w1_down=m_w1_down, g_ffn1_post=m_g_ffn1_post,
             g_mix_pre=m_g_mix_pre, w_in=m_w_in, g_mla_q=m_g_mla_q, w_mla_uq=m_w_mla_uq, g_mla_kv=m_g_mla_kv,
             w_mla_ukv=m_w_mla_ukv, w_conv=m_w_conv, w_out=m_w_out, g_mix_post=m_g_mix_post, g_ffn2_pre=m_g_ffn2_pre,
             w2_gate=m_w2_gate, w2_up=m_w2_up, w2_down=m_w2_down, g_ffn2_post=m_g_ffn2_post, g_ple_pre=m_g_ple_pre,
             w_ple_gate=m_w_ple_gate, w_ple_proj=m_w_ple_proj, g_ple_post=m_g_ple_post)
    v = dict(g_ffn1_pre=v_g_ffn1_pre, w1_gate=v_w1_gate, w1_up=v_w1_up, w1_down=v_w1_down, g_ffn1_post=v_g_ffn1_post,
             g_mix_pre=v_g_mix_pre, w_in=v_w_in, g_mla_q=v_g_mla_q, w_mla_uq=v_w_mla_uq, g_mla_kv=v_g_mla_kv,
             w_mla_ukv=v_w_mla_ukv, w_conv=v_w_conv, w_out=v_w_out, g_mix_post=v_g_mix_post, g_ffn2_pre=v_g_ffn2_pre,
             w2_gate=v_w2_gate, w2_up=v_w2_up, w2_down=v_w2_down, g_ffn2_post=v_g_ffn2_post, g_ple_pre=v_g_ple_pre,
             w_ple_gate=v_w_ple_gate, w_ple_proj=v_w_ple_proj, g_ple_post=v_g_ple_post)

    depth = g_ffn1_pre.shape[0]
    S = x.shape[1]
    cx, cy, cc = lax.axis_index("x"), lax.axis_index("y"), lax.axis_index("c")
    chip = 2 * cx + cy

    pk = _pack_shards(w)
    n_rows = pk.shape[0]
    gathered = _gather_weights(pk.reshape(2, n_rows // 2, PACK_W)).reshape(N_CHIPS, n_rows, PACK_W)
    Wfull = _unpack_gathered(gathered)
    conv_slot = lax.dynamic_update_slice(jnp.zeros((depth, 3, 256), F32),
                                         w_conv * (cc == 0).astype(F32), (0, 0, 64 * chip))
    conv_rows = depth * 3 * 256 // 128
    w_conv_full = _allreduce_small(_pack_small([conv_slot]))[:conv_rows].reshape(depth, 3, 256)

    inv = (ROPE_BASE ** (-jnp.arange(ROPE_HALF, dtype=F32) / ROPE_HALF)).reshape(1, ROPE_HALF)
    cos, sin = _rope_tables(positions.reshape(S, 1), inv)
    cos8, sin8 = jnp.tile(cos, (1, MLA_HEADS)), jnp.tile(sin, (1, MLA_HEADS))

    layers = []
    for i in range(depth):
        Wl = {name: Wfull[name][i] for name, _, _, _ in BIG}
        Wl["w_in"] = _regroup_w_in(Wl["w_in"])
        Wl["w_mla_uq"] = _regroup_w_uq(Wl["w_mla_uq"])
        Wl["w_mla_ukv"] = _regroup_w_ukv(Wl["w_mla_ukv"])
        Gl = {name: w[name][i].reshape(1, n) for name, n in GAINS}
        layers.append((Wl, Gl))

    xs, saved = x[0], []
    for i in range(depth):
        xs, sv = _layer_fwd(xs, p[i, 0], layers[i][0], layers[i][1], w_conv_full[i], cos, sin, cos8, sin8)
        saved.append(sv)

    loss_part, dx = _loss_and_grad(xs, loss_target[0])
    loss = lax.psum(loss_part[0, 0], AXES)

    gws, ggs = [None] * depth, [None] * depth
    for i in reversed(range(depth)):
        dx, gws[i], ggs[i] = _layer_bwd(dx, saved[i], layers[i][0], layers[i][1], w_conv_full[i], cos, sin, cos8, sin8)
        gws[i]["w_in"] = _ungroup_w_in(gws[i]["w_in"])
        gws[i]["w_mla_uq"] = _ungroup_w_uq(gws[i]["w_mla_uq"])
        gws[i]["w_mla_ukv"] = _ungroup_w_ukv(gws[i]["w_mla_ukv"])

    full = {name: jnp.stack([gws[i][name] for i in range(depth)]) for name, _, _, _ in BIG}
    gp = _pack_full_grads(full).reshape(N_CHIPS, 2, n_rows // 2, PACK_W).transpose(1, 0, 2, 3)
    gp = gp.reshape(2, N_CHIPS * (n_rows // 2), PACK_W)
    c_idx, chip_idx = cc.reshape(1).astype(jnp.int32), chip.reshape(1).astype(jnp.int32)
    s1 = _add_halves(gp, _swap_halves(gp), c_idx).reshape(N_CHIPS, n_rows // 2, PACK_W)
    fin = _add_chip_sums(s1, _scatter_chip_sums(s1), chip_idx)
    shard_grads = _unpack_shard_grads(_join_halves(fin).reshape(n_rows, PACK_W))

    small_names = [name for name, _ in GAINS] + ["w_conv"]
    small = _allreduce_small(_pack_small([jnp.stack([ggs[i][name] for i in range(depth)]) for name in small_names]))
    flat, off, grads = small.reshape(-1), 0, dict(shard_grads)
    for name, n in GAINS:
        grads[name] = flat[off:off + depth * n].reshape(depth, n)
        off += depth * n
    conv_full = flat[off:off + depth * 3 * 256].reshape(depth, 3, 256)
    grads["w_conv"] = lax.dynamic_slice(conv_full, (0, 0, 64 * chip), (depth, 3, 64))

    deltas, new_m, new_v = {}, {}, {}
    for name in WEIGHT_ORDER:
        shape = w[name].shape
        two_d = (shape[0] * shape[1], shape[2]) if len(shape) == 3 else shape
        d_, m_, v_ = _adamw(w[name].reshape(two_d), grads[name].reshape(two_d), m[name].reshape(two_d),
                            v[name].reshape(two_d), "adamw")
        deltas[name], new_m[name], new_v[name] = d_.reshape(shape), m_.reshape(shape), v_.reshape(shape)

    return (loss, dx[None], *[grads[n] for n in WEIGHT_ORDER], *[deltas[n] for n in WEIGHT_ORDER],
            *[new_m[n] for n in WEIGHT_ORDER], *[new_v[n] for n in WEIGHT_ORDER])
```

```python
import functools

import jax
import jax.numpy as jnp
from jax import lax
from jax.experimental import pallas as pl
from jax.experimental.pallas import tpu as pltpu

F32 = jnp.float32
BF16 = jnp.bfloat16
MESH = pl.DeviceIdType.MESH
AXES = ("x", "y", "c")

D_MODEL = 1024
N_CHIPS = 4
FF_SHARD = 704
FF_PAD = 768
D_FF_PAD = N_CHIPS * FF_PAD
EPS = 1e-6
NEG_INF = -1e30
ROPE_BASE = 10000.0
ROPE_HALF = 16
LANES = 128
SB_HEADS, MLA_HEADS, HEAD_DIM = 4, 8, 64
MLA_SCALE = 96.0 ** -0.5
SB_SCALE = 64.0 ** -0.5
ADAM_LR, ADAM_B1, ADAM_B2, ADAM_EPS, ADAM_WD, ADAM_STEP = 0.001, 0.9, 0.999, 1e-08, 0.01, 10
VMEM_LIMIT = 48 * 2 ** 20

Z_SB, Z_KV, Z_KR, Z_Q, Z_CV, Z_W = 0, 768, 1024, 1152, 1536, 2304

NT = (((1,), (1,)), ((), ()))
TN = (((0,), (0,)), ((), ()))

BIG = {"w1_gate": ((1024, FF_SHARD), (1024, FF_PAD)), "w1_up": ((1024, FF_SHARD), (1024, FF_PAD)),
       "w1_down": ((FF_SHARD, 1024), (FF_PAD, 1024)), "w_in": ((1024, 552), (1024, 552)),
       "w_mla_uq": ((384, 192), (384, 192)), "w_mla_ukv": ((256, 256), (256, 256)),
       "w_out": ((256, 1024), (256, 1024)),
       "w2_gate": ((1024, FF_SHARD), (1024, FF_PAD)), "w2_up": ((1024, FF_SHARD), (1024, FF_PAD)),
       "w2_down": ((FF_SHARD, 1024), (FF_PAD, 1024)), "w_ple_gate": ((256, 1024), (256, 1024)),
       "w_ple_proj": ((256, 256), (256, 256))}
BIG_NAMES = tuple(BIG)
GAINS = (("g_ffn1_pre", 1024), ("g_ffn1_post", 1024), ("g_mix_pre", 1024), ("g_mla_q", 384),
         ("g_mla_kv", 256), ("g_mix_post", 1024), ("g_ffn2_pre", 1024), ("g_ffn2_post", 1024),
         ("g_ple_pre", 1024), ("g_ple_post", 1024))
WEIGHT_ORDER = ("g_ffn1_pre", "w1_gate", "w1_up", "w1_down", "g_ffn1_post", "g_mix_pre", "w_in", "g_mla_q",
                "w_mla_uq", "g_mla_kv", "w_mla_ukv", "w_conv", "w_out", "g_mix_post", "g_ffn2_pre", "w2_gate",
                "w2_up", "w2_down", "g_ffn2_post", "g_ple_pre", "w_ple_gate", "w_ple_proj", "g_ple_post")

_pcall = pl.pallas_call


def _params(sem=None):
    return pltpu.CompilerParams(dimension_semantics=sem, vmem_limit_bytes=VMEM_LIMIT)


def _dot(a, b, dims=None):
    a, b = a.astype(BF16), b.astype(BF16)
    if dims is None:
        return jnp.dot(a, b, preferred_element_type=F32)
    return lax.dot_general(a, b, dims, preferred_element_type=F32)


def _rstd(v):
    return lax.rsqrt(jnp.mean(v * v, axis=-1, keepdims=True) + EPS)


def _sigmoid(v):
    return 1.0 / (1.0 + jnp.exp(-v))


def _row_block(n, want):
    b = min(n, want)
    while n % b:
        b //= 2
    return b


def _spec(block, index, lead=None):
    if not lead:
        return pl.BlockSpec(block, index)
    lead = tuple(lead)
    return pl.BlockSpec((None,) * len(lead) + tuple(block), lambda *g: lead + tuple(index(*g)))


def _arr(op):
    return op[0] if isinstance(op, tuple) else op


def _lead(op):
    return tuple(op[1:]) if isinstance(op, tuple) else ()


def _mm(pairs, mode, out_dtype, name, bm=512, bn=512):
    a0, b0 = _arr(pairs[0][0]), _arr(pairs[0][1])
    if mode == "nn":
        M, N = a0.shape[-2], b0.shape[-1]
    elif mode == "nt":
        M, N = a0.shape[-2], b0.shape[-2]
    else:
        M, N = a0.shape[-1], b0.shape[-1]
    bm, bn = _row_block(M, bm), _row_block(N, bn)
    n_pairs = len(pairs)
    dims = {"nn": None, "nt": NT, "tn": TN}[mode]

    def body(*refs):
        acc = None
        for t in range(n_pairs):
            part = _dot(refs[2 * t][...], refs[2 * t + 1][...], dims)
            acc = part if acc is None else acc + part
        refs[-1][...] = acc.astype(refs[-1].dtype)

    in_specs, ops = [], []
    for a, b in pairs:
        sa, sb = _arr(a).shape, _arr(b).shape
        if mode == "nn":
            in_specs += [_spec((bm, sa[-1]), lambda j, i: (i, 0), _lead(a)),
                         _spec((sb[-2], bn), lambda j, i: (0, j), _lead(b))]
        elif mode == "nt":
            in_specs += [_spec((bm, sa[-1]), lambda j, i: (i, 0), _lead(a)),
                         _spec((bn, sb[-1]), lambda j, i: (j, 0), _lead(b))]
        else:
            in_specs += [_spec((sa[-2], bm), lambda j, i: (0, i), _lead(a)),
                         _spec((sb[-2], bn), lambda j, i: (0, j), _lead(b))]
        ops += [_arr(a), _arr(b)]
    return _pcall(body, name=name, grid=(N // bn, M // bm), in_specs=in_specs,
                  out_specs=pl.BlockSpec((bm, bn), lambda j, i: (i, j)),
                  out_shape=jax.ShapeDtypeStruct((M, N), out_dtype),
                  compiler_params=_params(("parallel", "parallel")))(*ops)


def _dw_shards(a, b, axis, layer, into, name):
    S = a.shape[0]
    if axis == 1:
        Kp, Np = a.shape[1], b.shape[1] // N_CHIPS
        bm, bn = _row_block(Kp, 512), Np
        grid = (N_CHIPS, Kp // bm)
        a_spec = pl.BlockSpec((S, bm), lambda j, i: (0, i))
        b_spec = pl.BlockSpec((S, Np), lambda j, i: (0, j))
        o_spec = pl.BlockSpec((None, None, bm, Np), lambda j, i: (layer, j, i, 0))
    else:
        Kp, Np = a.shape[1] // N_CHIPS, b.shape[1]
        bn = _row_block(Np, 512)
        grid = (N_CHIPS, Np // bn)
        a_spec = pl.BlockSpec((S, Kp), lambda j, i: (0, j))
        b_spec = pl.BlockSpec((S, bn), lambda j, i: (0, i))
        o_spec = pl.BlockSpec((None, None, Kp, bn), lambda j, i: (layer, j, 0, i))
    shape = jax.ShapeDtypeStruct((2, N_CHIPS, Kp, Np), BF16)

    if into is None:
        def body(a_ref, b_ref, o_ref):
            o_ref[...] = _dot(a_ref[...], b_ref[...], TN).astype(BF16)

        return _pcall(body, name=name, grid=grid, in_specs=[a_spec, b_spec], out_specs=o_spec, out_shape=shape,
                      compiler_params=_params(("parallel", "parallel")))(a, b)

    def body_into(a_ref, b_ref, buf_ref, o_ref):
        o_ref[...] = _dot(a_ref[...], b_ref[...], TN).astype(BF16)

    return _pcall(body_into, name=name + "_into", grid=grid,
                  in_specs=[a_spec, b_spec, pl.BlockSpec(memory_space=pl.ANY)], out_specs=o_spec, out_shape=shape,
                  input_output_aliases={2: 0}, compiler_params=_params(("parallel", "parallel")))(a, b, into)


def _ew(fn, ins, outs, name, br=256):
    S = max(a.shape[0] for a in ins)
    br = _row_block(S, br)
    n_in = len(ins)

    def body(*refs):
        res = fn(*[r[...] for r in refs[:n_in]])
        if not isinstance(res, tuple):
            res = (res,)
        for r, v in zip(refs[n_in:], res):
            r[...] = v.astype(r.dtype)

    in_specs = [pl.BlockSpec((br, a.shape[1]), lambda i: (i, 0)) if a.shape[0] == S and S > 1
                else pl.BlockSpec(a.shape, lambda i: (0, 0)) for a in ins]
    out = _pcall(body, name=name, grid=(S // br,), in_specs=in_specs,
                 out_specs=tuple(pl.BlockSpec((br, w), lambda i: (i, 0)) for w, _ in outs),
                 out_shape=tuple(jax.ShapeDtypeStruct((S, w), dt) for w, dt in outs),
                 compiler_params=_params(("parallel",)))(*ins)
    return out if len(outs) > 1 else out[0]


def _norm_fwd(x, g, name):
    return _ew(lambda xv, gv: xv * _rstd(xv) * gv, [x, g], [(x.shape[1], BF16)], name, br=512)


def _resid_norm(x, y, g, alpha, name):
    return _ew(lambda xv, yv, gv: xv + alpha * (yv * _rstd(yv) * gv), [x, y, g], [(x.shape[1], F32)], name)


def _norm_bwd(xin, g, dy, alpha, resid, out_dtype, name):
    S, W = xin.shape
    br = _row_block(S, 256)
    has_res = resid is not None

    def body(*refs):
        x_ref, g_ref, dy_ref = refs[:3]
        dx_ref, dg_ref = refs[-2:]
        xv, dyv = x_ref[...], dy_ref[...] * alpha
        r = _rstd(xv)
        xh = xv * r
        u = dyv * g_ref[...]
        dx = r * (u - xh * jnp.mean(u * xh, axis=-1, keepdims=True))
        if has_res:
            dx = dx + refs[3][...]
        dx_ref[...] = dx.astype(dx_ref.dtype)
        part = jnp.sum(dyv * xh, axis=0, keepdims=True)

        @pl.when(pl.program_id(0) == 0)
        def _():
            dg_ref[...] = part

        @pl.when(pl.program_id(0) > 0)
        def _():
            dg_ref[...] += part

    row = pl.BlockSpec((br, W), lambda i: (i, 0))
    vec = pl.BlockSpec((1, W), lambda i: (0, 0))
    ops = [xin, g, dy] + ([resid] if has_res else [])
    return _pcall(body, name=name, grid=(S // br,), in_specs=[row, vec, row] + ([row] if has_res else []),
                  out_specs=(row, vec),
                  out_shape=(jax.ShapeDtypeStruct((S, W), out_dtype), jax.ShapeDtypeStruct((1, W), F32)),
                  compiler_params=_params(("arbitrary",)))(*ops)


def _ffn_gate_up(h, wg, wu, layer, name):
    S, K = h.shape
    bm, bn = _row_block(S, 1024), 256
    per = FF_PAD // bn

    def body(h_ref, wg_ref, wu_ref, g_ref, u_ref, a_ref):
        hv = h_ref[...]
        g = _dot(hv, wg_ref[...])
        u = _dot(hv, wu_ref[...])
        g_ref[...] = g.astype(BF16)
        u_ref[...] = u.astype(BF16)
        a_ref[...] = (g * _sigmoid(g) * u).astype(BF16)

    blk = pl.BlockSpec((bm, bn), lambda n, i: (i, n))
    wsp = pl.BlockSpec((None, None, K, bn), lambda n, i: (layer, n // per, 0, n % per))
    return _pcall(body, name=name, grid=(D_FF_PAD // bn, S // bm),
                  in_specs=[pl.BlockSpec((bm, K), lambda n, i: (i, 0)), wsp, wsp],
                  out_specs=(blk, blk, blk), out_shape=(jax.ShapeDtypeStruct((S, D_FF_PAD), BF16),) * 3,
                  compiler_params=_params(("parallel", "parallel")))(h, wg, wu)


def _ffn_bwd_mid(dy, wd, g, u, name):
    S, D = dy.shape
    F = g.shape[1]
    bm, bn = _row_block(S, 1024), 256

    def body(dy_ref, wd_ref, g_ref, u_ref, dg_ref, du_ref):
        da = _dot(dy_ref[...], wd_ref[...], NT)
        gv, uv = g_ref[...].astype(F32), u_ref[...].astype(F32)
        s = _sigmoid(gv)
        dg_ref[...] = (da * uv * (s * (1.0 + gv * (1.0 - s)))).astype(BF16)
        du_ref[...] = (da * (gv * s)).astype(BF16)

    blk = pl.BlockSpec((bm, bn), lambda j, i: (i, j))
    return _pcall(body, name=name, grid=(F // bn, S // bm),
                  in_specs=[pl.BlockSpec((bm, D), lambda j, i: (i, 0)),
                            _spec((bn, D), lambda j, i: (j, 0), _lead(wd)), blk, blk],
                  out_specs=(blk, blk), out_shape=(jax.ShapeDtypeStruct((S, F), BF16),) * 2,
                  compiler_params=_params(("parallel", "parallel")))(dy, _arr(wd), g, u)


def _ffn_dh(dg, du, wg, wu, layer, name):
    S = dg.shape[0]
    D = wg.shape[2]
    bm, bn = _row_block(S, 512), 512

    def body(dg_ref, du_ref, wg_ref, wu_ref, o_ref, acc):
        j = pl.program_id(2)
        part = _dot(dg_ref[...], wg_ref[...], NT) + _dot(du_ref[...], wu_ref[...], NT)

        @pl.when(j == 0)
        def _():
            acc[...] = part

        @pl.when(j > 0)
        def _():
            acc[...] += part

        @pl.when(j == N_CHIPS - 1)
        def _():
            o_ref[...] = acc[...]

    asp = pl.BlockSpec((bm, FF_PAD), lambda n, i, j: (i, j))
    wsp = pl.BlockSpec((None, None, bn, FF_PAD), lambda n, i, j: (layer, j, n, 0))
    return _pcall(body, name=name, grid=(D // bn, S // bm, N_CHIPS), in_specs=[asp, asp, wsp, wsp],
                  out_specs=pl.BlockSpec((bm, bn), lambda n, i, j: (i, n)),
                  out_shape=jax.ShapeDtypeStruct((S, D), F32), scratch_shapes=[pltpu.VMEM((bm, bn), F32)],
                  compiler_params=_params(("parallel", "parallel", "arbitrary")))(dg, du, wg, wu)


def _ple_proj(p16, w, layer):
    S, K = p16.shape
    bm = _row_block(S, 1024)

    def body(p_ref, w_ref, o_ref):
        o_ref[...] = _dot(p_ref[...], w_ref[...])

    return _pcall(body, name="ple_proj", grid=(N_CHIPS, S // bm),
                  in_specs=[pl.BlockSpec((bm, K), lambda j, i: (i, 0)),
                            pl.BlockSpec((None, None, K, 256), lambda j, i: (layer, j, 0, 0))],
                  out_specs=pl.BlockSpec((bm, 256), lambda j, i: (i, j)),
                  out_shape=jax.ShapeDtypeStruct((S, N_CHIPS * 256), F32),
                  compiler_params=_params(("parallel", "parallel")))(p16, w)


def _loss_and_grad(xf, tgt):
    S, W = xf.shape
    br = _row_block(S, 256)

    def body(x_ref, t_ref, l_ref, d_ref):
        d = x_ref[...] - t_ref[...]
        d_ref[...] = d * (1.0 / W)
        part = 0.5 * jnp.sum(jnp.sum(d * d, axis=-1, keepdims=True) * (1.0 / W), axis=0, keepdims=True)

        @pl.when(pl.program_id(0) == 0)
        def _():
            l_ref[...] = part

        @pl.when(pl.program_id(0) > 0)
        def _():
            l_ref[...] += part

    row = pl.BlockSpec((br, W), lambda i: (i, 0))
    return _pcall(body, name="loss", grid=(S // br,), in_specs=[row, row],
                  out_specs=(pl.BlockSpec((1, 1), lambda i: (0, 0)), row),
                  out_shape=(jax.ShapeDtypeStruct((1, 1), F32), jax.ShapeDtypeStruct((S, W), F32)),
                  compiler_params=_params(("arbitrary",)))(xf, tgt)


def _adamw(w, g, m, v, name):
    R, C = w.shape
    br = _row_block(R, 256) if R % 8 == 0 else R

    def body(w_ref, g_ref, m_ref, v_ref, d_ref, nm_ref, nv_ref):
        gv = g_ref[...]
        nm = ADAM_B1 * m_ref[...] + (1.0 - ADAM_B1) * gv
        nv = ADAM_B2 * v_ref[...] + (1.0 - ADAM_B2) * (gv * gv)
        m_hat = nm / (1.0 - ADAM_B1 ** ADAM_STEP)
        v_hat = nv / (1.0 - ADAM_B2 ** ADAM_STEP)
        d_ref[...] = -ADAM_LR * (m_hat / (jnp.sqrt(v_hat) + ADAM_EPS) + ADAM_WD * w_ref[...])
        nm_ref[...] = nm
        nv_ref[...] = nv

    blk = pl.BlockSpec((br, C), lambda i: (i, 0))
    return _pcall(body, name=name, grid=(R // br,), in_specs=[blk] * 4, out_specs=(blk,) * 3,
                  out_shape=(jax.ShapeDtypeStruct((R, C), F32),) * 3,
                  compiler_params=_params(("parallel",)))(w, g, m, v)


def _rope_tables(pos, inv_pat, sign_pat):
    def fn(p, iv, sg):
        ang = p.astype(F32) * iv
        return jnp.cos(ang), jnp.sin(ang) * sg

    return _ew(fn, [pos, inv_pat, sign_pat], [(LANES, F32)] * 2, "rope_tables")


def _swap_halves_of_rope(v):
    W = v.shape[1]
    lane = lax.broadcasted_iota(jnp.int32, (1, W), 1) % LANES
    return jnp.where((lane >= 64) & (lane < 80), pltpu.roll(v, W - ROPE_HALF, 1),
                     jnp.where((lane >= 80) & (lane < 96), pltpu.roll(v, ROPE_HALF, 1), 0.0))


def _tile_lanes(v, n):
    return jnp.concatenate([v] * n, axis=1)


def _mla_prep(qf, kv, z, cp, sp):
    S = qf.shape[0]
    br = _row_block(S, 256)
    W = MLA_HEADS * LANES

    def body(q_ref, k_ref, r_ref, c_ref, s_ref, qo_ref, ko_ref):
        c, s = c_ref[...], s_ref[...]
        q = q_ref[...]
        qo_ref[...] = (q * _tile_lanes(c, MLA_HEADS) + _swap_halves_of_rope(q) * _tile_lanes(s, MLA_HEADS)).astype(BF16)
        r = r_ref[...]
        lane = lax.broadcasted_iota(jnp.int32, (1, LANES), 1)
        kr = jnp.where((lane >= 64) & (lane < 96), r * c + _swap_halves_of_rope(r) * s, 0.0)
        ko_ref[...] = (k_ref[...] + _tile_lanes(kr, MLA_HEADS)).astype(BF16)

    wide = pl.BlockSpec((br, W), lambda i: (i, 0))
    one = pl.BlockSpec((br, LANES), lambda i: (i, 0))
    return _pcall(body, name="mla_prep", grid=(S // br,),
                  in_specs=[wide, wide, pl.BlockSpec((br, LANES), lambda i: (i, Z_KR // LANES)), one, one],
                  out_specs=(wide, wide), out_shape=(jax.ShapeDtypeStruct((S, W), BF16),) * 2,
                  compiler_params=_params(("parallel",)))(qf, kv, z, cp, sp)


def _mla_prep_bwd(dq, dk, cp, sp):
    S, W = dq.shape
    br = _row_block(S, 256)

    def body(dq_ref, dk_ref, c_ref, s_ref, dqo_ref, dr_ref):
        c, s = c_ref[...], s_ref[...]
        d = dq_ref[...]
        dqo_ref[...] = (d * _tile_lanes(c, MLA_HEADS) + _swap_halves_of_rope(d * _tile_lanes(s, MLA_HEADS))).astype(BF16)
        dkv = dk_ref[...]
        tot = dkv[:, 0:LANES]
        for h in range(1, MLA_HEADS):
            tot = tot + dkv[:, h * LANES:(h + 1) * LANES]
        lane = lax.broadcasted_iota(jnp.int32, (1, LANES), 1)
        tot = jnp.where((lane >= 64) & (lane < 96), tot, 0.0)
        dr_ref[...] = tot * c + _swap_halves_of_rope(tot * s)

    wide = pl.BlockSpec((br, W), lambda i: (i, 0))
    one = pl.BlockSpec((br, LANES), lambda i: (i, 0))
    return _pcall(body, name="mla_prep_bwd", grid=(S // br,), in_specs=[wide, wide, one, one], out_specs=(wide, one),
                  out_shape=(jax.ShapeDtypeStruct((S, W), BF16), jax.ShapeDtypeStruct((S, LANES), F32)),
                  compiler_params=_params(("parallel",)))(dq, dk, cp, sp)


def _shift_down(v, k, row):
    return jnp.where(row >= k, pltpu.roll(v, k, 0), 0.0)


def _shift_up(v, k, row):
    n = v.shape[0]
    return jnp.where(row < n - k, pltpu.roll(v, n - k, 0), 0.0)


def _conv_fwd(z, w):
    S = z.shape[0]
    c0 = Z_CV // LANES

    def body(b_ref, c_ref, h_ref, w_ref, y_ref):
        u = c_ref[...] * h_ref[...]
        row = lax.broadcasted_iota(jnp.int32, u.shape, 0)
        wv = w_ref[...]
        conv = wv[0:1] * _shift_down(u, 2, row) + wv[1:2] * _shift_down(u, 1, row) + wv[2:3] * u
        y_ref[...] = b_ref[...] * conv

    def col(k):
        return pl.BlockSpec((S, LANES), lambda j: (0, c0 + 2 * k + j))

    return _pcall(body, name="conv_fwd", grid=(2,),
                  in_specs=[col(0), col(1), col(2), pl.BlockSpec((3, LANES), lambda j: (0, j))],
                  out_specs=pl.BlockSpec((S, LANES), lambda j: (0, j)),
                  out_shape=jax.ShapeDtypeStruct((S, 256), F32), compiler_params=_params(("parallel",)))(z, z, z, w)


def _conv_bwd(z, w, dc_all):
    S = z.shape[0]
    c0 = Z_CV // LANES

    def body(b_ref, c_ref, h_ref, w_ref, dy_ref, db_ref, dc_ref, dh_ref, dw_ref):
        cv, hv, bv, dyv = c_ref[...], h_ref[...], b_ref[...], dy_ref[...]
        u = cv * hv
        row = lax.broadcasted_iota(jnp.int32, u.shape, 0)
        wv = w_ref[...]
        u1, u2 = _shift_down(u, 1, row), _shift_down(u, 2, row)
        conv = wv[0:1] * u2 + wv[1:2] * u1 + wv[2:3] * u
        db_ref[...] = dyv * conv
        dconv = dyv * bv
        du = wv[2:3] * dconv + wv[1:2] * _shift_up(dconv, 1, row) + wv[0:1] * _shift_up(dconv, 2, row)
        dc_ref[...] = du * hv
        dh_ref[...] = du * cv
        dw_ref[0:1, :] = jnp.sum(dconv * u2, axis=0, keepdims=True)
        dw_ref[1:2, :] = jnp.sum(dconv * u1, axis=0, keepdims=True)
        dw_ref[2:3, :] = jnp.sum(dconv * u, axis=0, keepdims=True)

    def col(k):
        return pl.BlockSpec((S, LANES), lambda j: (0, c0 + 2 * k + j))

    wsp = pl.BlockSpec((3, LANES), lambda j: (0, j))
    osp = pl.BlockSpec((S, LANES), lambda j: (0, j))
    db, dc, dh, dw = _pcall(
        body, name="conv_bwd", grid=(2,),
        in_specs=[col(0), col(1), col(2), wsp, pl.BlockSpec((S, LANES), lambda j: (0, 6 + j))],
        out_specs=(osp, osp, osp, wsp),
        out_shape=(jax.ShapeDtypeStruct((S, 256), F32),) * 3 + (jax.ShapeDtypeStruct((3, 256), F32),),
        compiler_params=_params(("parallel",)))(z, z, z, w, dc_all)
    return db, dc, dh, dw


def _half_masks():
    lane = lax.broadcasted_iota(jnp.int32, (1, LANES), 1)
    return lane < HEAD_DIM, lane >= HEAD_DIM


def _pair(v, p):
    return v[:, p * LANES:(p + 1) * LANES]


def _keep(mask, v):
    return jnp.where(mask, v, jnp.zeros_like(v))


def _lane_col(vals):
    lane = lax.broadcasted_iota(jnp.int32, (1, LANES), 1)
    out = jnp.zeros((vals[0].shape[0], LANES), F32)
    for h, v in enumerate(vals):
        out = jnp.where(lane == h, v, out)
    return out


def _mla_fwd(q, k, kv):
    S = q.shape[0]
    H = MLA_HEADS
    bq = bk = _row_block(S, 256)

    def body(q_ref, k_ref, v_ref, o_ref, lse_ref):
        i = pl.program_id(0)
        lo, hi = _half_masks()
        qb = q_ref[...]
        row = i * bq + lax.broadcasted_iota(jnp.int32, (bq, bk), 0)
        col = lax.broadcasted_iota(jnp.int32, (bq, bk), 1)

        def step(j, carry):
            ms, ls, accs = carry
            off = pl.multiple_of(j * bk, bk)
            kb = k_ref[pl.ds(off, bk), :]
            vb = v_ref[pl.ds(off, bk), :].astype(BF16)
            ok = col + j * bk <= row
            ms2, ls2, accs2 = [], [], list(accs)
            for h in range(H):
                s = jnp.where(ok, _dot(_pair(qb, h), _pair(kb, h), NT) * MLA_SCALE, NEG_INF)
                m_new = jnp.maximum(ms[h], jnp.max(s, axis=-1, keepdims=True))
                p = jnp.exp(s - m_new)
                a = jnp.exp(ms[h] - m_new)
                ls2.append(a * ls[h] + jnp.sum(p, axis=-1, keepdims=True))
                ms2.append(m_new)
                pv = _dot(p, _pair(vb, h // 2))
                accs2[h // 2] = jnp.where(lo if h % 2 == 0 else hi, a * accs2[h // 2] + pv, accs2[h // 2])
            return tuple(ms2), tuple(ls2), tuple(accs2)

        init = (tuple(jnp.full((bq, 1), NEG_INF, F32) for _ in range(H)),
                tuple(jnp.zeros((bq, 1), F32) for _ in range(H)),
                tuple(jnp.zeros((bq, LANES), F32) for _ in range(H // 2)))
        ms, ls, accs = lax.fori_loop(0, i + 1, step, init)
        o_ref[...] = jnp.concatenate(
            [accs[p] / jnp.where(lo, ls[2 * p], ls[2 * p + 1]) for p in range(H // 2)], axis=1)
        lse_ref[...] = _lane_col([ms[h] + jnp.log(ls[h]) for h in range(H)])

    return _pcall(body, name="mla_fwd", grid=(S // bq,),
                  in_specs=[pl.BlockSpec((bq, H * LANES), lambda i: (i, 0)),
                            pl.BlockSpec((S, H * LANES), lambda i: (0, 0)),
                            pl.BlockSpec((S, H * HEAD_DIM), lambda i: (0, 2))],
                  out_specs=(pl.BlockSpec((bq, H * HEAD_DIM), lambda i: (i, 0)),
                             pl.BlockSpec((bq, LANES), lambda i: (i, 0))),
                  out_shape=(jax.ShapeDtypeStruct((S, H * HEAD_DIM), F32), jax.ShapeDtypeStruct((S, LANES), F32)),
                  compiler_params=_params(("parallel",)))(q, k, kv)


def _mla_bwd(q, k, kv, o, lse, dc_all):
    S = q.shape[0]
    H = MLA_HEADS
    bq = bk = _row_block(S, 256)

    def body(q_ref, k_ref, v_ref, o_ref, lse_ref, doa_ref, dob_ref, dq_ref, dk_ref, dv_ref):
        i = pl.program_id(0)

        @pl.when(i == 0)
        def _():
            dk_ref[...] = jnp.zeros_like(dk_ref)
            dv_ref[...] = jnp.zeros_like(dv_ref)

        lo, hi = _half_masks()
        qb = q_ref[...]
        dob = jnp.concatenate([doa_ref[...], dob_ref[...]], axis=1)
        prod = dob * o_ref[...]
        lse_v = lse_ref[...]
        do16 = dob.astype(BF16)
        dom, deltas, lses = [], [], []
        for h in range(H):
            mk = lo if h % 2 == 0 else hi
            dom.append(_keep(mk, _pair(do16, h // 2)))
            deltas.append(jnp.sum(_keep(mk, _pair(prod, h // 2)), axis=-1, keepdims=True))
            lses.append(lse_v[:, h:h + 1])
        row = i * bq + lax.broadcasted_iota(jnp.int32, (bq, bk), 0)
        col = lax.broadcasted_iota(jnp.int32, (bq, bk), 1)

        def step(j, dqs):
            off = pl.multiple_of(j * bk, bk)
            kb = k_ref[pl.ds(off, bk), :]
            vb = v_ref[pl.ds(off, bk), :].astype(BF16)
            ok = col + j * bk <= row
            dqs2, dks, dvs = [], [], [None] * (H // 2)
            for h in range(H):
                s = _dot(_pair(qb, h), _pair(kb, h), NT) * MLA_SCALE
                p = jnp.where(ok, jnp.exp(s - lses[h]), 0.0)
                dp = _dot(dom[h], _pair(vb, h // 2), NT)
                ds16 = (p * (dp - deltas[h]) * MLA_SCALE).astype(BF16)
                dks.append(_dot(ds16, _pair(qb, h), TN))
                dvp = _dot(p, dom[h], TN)
                dvs[h // 2] = dvp if dvs[h // 2] is None else dvs[h // 2] + dvp
                dqs2.append(dqs[h] + _dot(ds16, _pair(kb, h)))
            dk_ref[pl.ds(off, bk), :] += jnp.concatenate(dks, axis=1)
            dv_ref[pl.ds(off, bk), :] += jnp.concatenate(dvs, axis=1)
            return tuple(dqs2)

        dqs = lax.fori_loop(0, i + 1, step, tuple(jnp.zeros((bq, LANES), F32) for _ in range(H)))
        dq_ref[...] = jnp.concatenate(dqs, axis=1)

    wide = pl.BlockSpec((bq, H * LANES), lambda i: (i, 0))
    full = pl.BlockSpec((S, H * LANES), lambda i: (0, 0))
    return _pcall(body, name="mla_bwd", grid=(S // bq,),
                  in_specs=[wide, full, pl.BlockSpec((S, H * HEAD_DIM), lambda i: (0, 2)),
                            pl.BlockSpec((bq, H * HEAD_DIM), lambda i: (i, 0)),
                            pl.BlockSpec((bq, LANES), lambda i: (i, 0)),
                            pl.BlockSpec((bq, 256), lambda i: (i, 1)), pl.BlockSpec((bq, 256), lambda i: (i, 2))],
                  out_specs=(wide, full, pl.BlockSpec((S, H * HEAD_DIM), lambda i: (0, 0))),
                  out_shape=(jax.ShapeDtypeStruct((S, H * LANES), F32), jax.ShapeDtypeStruct((S, H * LANES), F32),
                             jax.ShapeDtypeStruct((S, H * HEAD_DIM), F32)),
                  compiler_params=_params(("arbitrary",)))(q, k, kv, o, lse, dc_all, dc_all)


def _dot_exact(x, tri):
    h1 = x.astype(BF16)
    r1 = x - h1.astype(F32)
    h2 = r1.astype(BF16)
    h3 = (r1 - h2.astype(F32)).astype(BF16)
    return _dot(h1, tri) + _dot(h2, tri) + _dot(h3, tri)


def _sb_terms(qm, kb, mask):
    z = _dot(qm, kb, NT) * SB_SCALE
    sp = jnp.maximum(z, 0.0) + jnp.log(1.0 + jnp.exp(-jnp.abs(z)))
    return jnp.where(mask, -sp, 0.0), z - sp


def _sb_fwd(z):
    S = z.shape[0]
    H = SB_HEADS
    bq, bk = _row_block(S, 256), _row_block(S, 128)

    def body(q_ref, k_ref, v_ref, o_ref, t_ref):
        i = pl.program_id(0)
        lo, hi = _half_masks()
        q16 = q_ref[...].astype(BF16)
        qm = [_keep(lo if h % 2 == 0 else hi, _pair(q16, h // 2)) for h in range(H)]
        row = i * bq + lax.broadcasted_iota(jnp.int32, (bq, bk), 0)
        col = lax.broadcasted_iota(jnp.int32, (bq, bk), 1)
        rr = lax.broadcasted_iota(jnp.int32, (bk, bk), 0)
        cc = lax.broadcasted_iota(jnp.int32, (bk, bk), 1)
        later = (rr > cc).astype(BF16)
        n_blocks = (i + 1) * (bq // bk)

        def step(t, carry):
            tails, accs = carry
            j = n_blocks - 1 - t
            off = pl.multiple_of(j * bk, bk)
            kb = k_ref[pl.ds(off, bk), :].astype(BF16)
            vb = v_ref[pl.ds(off, bk), :].astype(BF16)
            mask = col + j * bk < row
            tails2, accs2 = [], list(accs)
            for h in range(H):
                lnot, lsig = _sb_terms(qm[h], _pair(kb, h // 2), mask)
                after = tails[h] + _dot_exact(lnot, later)
                w = jnp.where(mask, jnp.exp(lsig + after), 0.0)
                pv = _dot(w, _pair(vb, h // 2))
                accs2[h // 2] = accs2[h // 2] + _keep(lo if h % 2 == 0 else hi, pv)
                tails2.append(tails[h] + jnp.sum(lnot, axis=-1, keepdims=True))
            return tuple(tails2), tuple(accs2)

        init = (tuple(jnp.zeros((bq, 1), F32) for _ in range(H)),
                tuple(jnp.zeros((bq, LANES), F32) for _ in range(H // 2)))
        tails, accs = lax.fori_loop(0, n_blocks, step, init)
        o_ref[...] = jnp.concatenate(accs, axis=1)
        t_ref[...] = _lane_col(tails)

    return _pcall(body, name="sb_fwd", grid=(S // bq,),
                  in_specs=[pl.BlockSpec((bq, 256), lambda i: (i, 0)), pl.BlockSpec((S, 256), lambda i: (0, 1)),
                            pl.BlockSpec((S, 256), lambda i: (0, 2))],
                  out_specs=(pl.BlockSpec((bq, 256), lambda i: (i, 0)), pl.BlockSpec((bq, LANES), lambda i: (i, 0))),
                  out_shape=(jax.ShapeDtypeStruct((S, 256), F32), jax.ShapeDtypeStruct((S, LANES), F32)),
                  compiler_params=_params(("parallel",)))(z, z, z)


def _sb_bwd(z, tot, dc_all):
    S = z.shape[0]
    H = SB_HEADS
    bq, bk = _row_block(S, 256), _row_block(S, 128)

    def body(q_ref, k_ref, v_ref, t_ref, do_ref, dq_ref, dk_ref, dv_ref):
        i = pl.program_id(0)

        @pl.when(i == 0)
        def _():
            dk_ref[...] = jnp.zeros_like(dk_ref)
            dv_ref[...] = jnp.zeros_like(dv_ref)

        lo, hi = _half_masks()
        q16 = q_ref[...].astype(BF16)
        do16 = do_ref[...].astype(BF16)
        tot_v = t_ref[...]
        masks = [lo if h % 2 == 0 else hi for h in range(H)]
        qm = [_keep(masks[h], _pair(q16, h // 2)) for h in range(H)]
        dom = [_keep(masks[h], _pair(do16, h // 2)) for h in range(H)]
        tots = [tot_v[:, h:h + 1] for h in range(H)]
        row = i * bq + lax.broadcasted_iota(jnp.int32, (bq, bk), 0)
        col = lax.broadcasted_iota(jnp.int32, (bq, bk), 1)
        rr = lax.broadcasted_iota(jnp.int32, (bk, bk), 0)
        cc = lax.broadcasted_iota(jnp.int32, (bk, bk), 1)
        upto = (rr <= cc).astype(BF16)
        before = (rr < cc).astype(BF16)

        def step(j, carry):
            heads, eheads, dqs = carry
            off = pl.multiple_of(j * bk, bk)
            kb = k_ref[pl.ds(off, bk), :].astype(BF16)
            vb = v_ref[pl.ds(off, bk), :].astype(BF16)
            mask = col + j * bk < row
            heads2, eheads2, dqs2 = [], [], list(dqs)
            dks, dvs = [None] * (H // 2), [None] * (H // 2)
            for h in range(H):
                pr = h // 2
                lnot, lsig = _sb_terms(qm[h], _pair(kb, pr), mask)
                after = tots[h] - (heads[h] + _dot_exact(lnot, upto))
                w = jnp.where(mask, jnp.exp(lsig + after), 0.0)
                e = w * _dot(dom[h], _pair(vb, pr), NT)
                esum = eheads[h] + _dot_exact(e, before)
                sig = jnp.exp(lsig)
                dz16 = (jnp.where(mask, e * (1.0 - sig) - sig * esum, 0.0) * SB_SCALE).astype(BF16)
                dkp, dvp = _dot(dz16, qm[h], TN), _dot(w, dom[h], TN)
                dks[pr] = dkp if dks[pr] is None else dks[pr] + dkp
                dvs[pr] = dvp if dvs[pr] is None else dvs[pr] + dvp
                dqs2[pr] = dqs2[pr] + _keep(masks[h], _dot(dz16, _pair(kb, pr)))
                heads2.append(heads[h] + jnp.sum(lnot, axis=-1, keepdims=True))
                eheads2.append(eheads[h] + jnp.sum(e, axis=-1, keepdims=True))
            dk_ref[pl.ds(off, bk), :] += jnp.concatenate(dks, axis=1)
            dv_ref[pl.ds(off, bk), :] += jnp.concatenate(dvs, axis=1)
            return tuple(heads2), tuple(eheads2), tuple(dqs2)

        zeros = tuple(jnp.zeros((bq, 1), F32) for _ in range(H))
        init = (zeros, zeros, tuple(jnp.zeros((bq, LANES), F32) for _ in range(H // 2)))
        _, _, dqs = lax.fori_loop(0, (i + 1) * (bq // bk), step, init)
        dq_ref[...] = jnp.concatenate(dqs, axis=1)

    blk = pl.BlockSpec((bq, 256), lambda i: (i, 0))
    full = pl.BlockSpec((S, 256), lambda i: (0, 0))
    return _pcall(body, name="sb_bwd", grid=(S // bq,),
                  in_specs=[blk, pl.BlockSpec((S, 256), lambda i: (0, 1)), pl.BlockSpec((S, 256), lambda i: (0, 2)),
                            pl.BlockSpec((bq, LANES), lambda i: (i, 0)), blk],
                  out_specs=(blk, full, full), out_shape=(jax.ShapeDtypeStruct((S, 256), F32),) * 3,
                  compiler_params=_params(("arbitrary",)))(z, z, z, tot, dc_all)


_HBM = pl.BlockSpec(memory_space=pltpu.HBM)


def _place():
    x, y, c = lax.axis_index("x"), lax.axis_index("y"), lax.axis_index("c")
    other_chips = ((1 - x, y), (x, 1 - y), (1 - x, 1 - y))
    return x, y, c, other_chips


def _remote(src, dst, send_sems, recv_sems, k, dev):
    return pltpu.make_async_remote_copy(src_ref=src, dst_ref=dst, send_sem=send_sems.at[k], recv_sem=recv_sems.at[k],
                                        device_id=dev, device_id_type=MESH)


def _gather_weights(shards):
    n = len(shards)

    def body(*refs):
        src, dst = refs[:n], refs[n:2 * n]
        send_sems, recv_sems = refs[2 * n:]
        x, y, c, chips = _place()
        me, sib = 2 * x + y, (x, y, 1 - c)
        first = [_remote(src[w].at[c], dst[w].at[c, me], send_sems, recv_sems, 6 * w + k, (cx, cy, c))
                 for k, (cx, cy) in enumerate(chips) for w in range(n)]
        for cp in first:
            cp.start()
        passed = []
        for k, (cx, cy) in enumerate(chips):
            for w in range(n):
                blk = dst[w].at[c, 2 * cx + cy]
                _remote(blk, blk, send_sems, recv_sems, 6 * w + k, (cx, cy, c)).wait_recv()
                fwd = _remote(blk, blk, send_sems, recv_sems, 6 * w + 3 + k, sib)
                fwd.start()
                passed.append(fwd)
        for k, (cx, cy) in enumerate(chips):
            for w in range(n):
                blk = dst[w].at[1 - c, 2 * cx + cy]
                _remote(blk, blk, send_sems, recv_sems, 6 * w + 3 + k, sib).wait_recv()
        for cp in first + passed:
            cp.wait_send()

    return _pcall(body, name="gather_weights", in_specs=[_HBM] * n, out_specs=(_HBM,) * n,
                  out_shape=tuple(jax.ShapeDtypeStruct((2, N_CHIPS) + s.shape[1:], s.dtype) for s in shards),
                  scratch_shapes=[pltpu.SemaphoreType.DMA((6 * n,)), pltpu.SemaphoreType.DMA((6 * n,))])(*shards)


def _swap_layers(bufs):
    n = len(bufs)

    def body(*refs):
        src, dst = refs[:n], refs[n:2 * n]
        send_sems, recv_sems = refs[2 * n:]
        x, y, c, _ = _place()
        cps = [_remote(src[w].at[1 - c], dst[w], send_sems, recv_sems, w, (x, y, 1 - c)) for w in range(n)]
        for cp in cps:
            cp.start()
        for cp in cps:
            cp.wait()

    return _pcall(body, name="swap_layers", in_specs=[_HBM] * n, out_specs=(_HBM,) * n,
                  out_shape=tuple(jax.ShapeDtypeStruct(b.shape[1:], b.dtype) for b in bufs),
                  scratch_shapes=[pltpu.SemaphoreType.DMA((n,)), pltpu.SemaphoreType.DMA((n,))])(*bufs)


def _scatter_chip_sums(sums):
    n = len(sums)

    def body(*refs):
        src, dst = refs[:n], refs[n:2 * n]
        send_sems, recv_sems = refs[2 * n:]
        x, y, c, chips = _place()
        cps = [_remote(src[w].at[2 * cx + cy], dst[w].at[k], send_sems, recv_sems, 3 * w + k, (cx, cy, c))
               for k, (cx, cy) in enumerate(chips) for w in range(n)]
        for cp in cps:
            cp.start()
        for cp in cps:
            cp.wait()

    return _pcall(body, name="scatter_chip_sums", in_specs=[_HBM] * n, out_specs=(_HBM,) * n,
                  out_shape=tuple(jax.ShapeDtypeStruct((3,) + s.shape[1:], s.dtype) for s in sums),
                  scratch_shapes=[pltpu.SemaphoreType.DMA((3 * n,)), pltpu.SemaphoreType.DMA((3 * n,))])(*sums)


def _join_layers(fins):
    n = len(fins)

    def body(*refs):
        src, dst = refs[:n], refs[n:2 * n]
        send_sems, recv_sems = refs[2 * n:]
        x, y, c, _ = _place()
        cps = [_remote(src[w].at[c], dst[w].at[c], send_sems, recv_sems, w, (x, y, 1 - c)) for w in range(n)]
        for cp in cps:
            cp.start()
        for w in range(n):
            blk = dst[w].at[1 - c]
            _remote(blk, blk, send_sems, recv_sems, w, (x, y, 1 - c)).wait_recv()
        for cp in cps:
            cp.wait_send()

    return _pcall(body, name="join_layers", in_specs=[_HBM] * n, out_specs=(_HBM,) * n,
                  out_shape=tuple(jax.ShapeDtypeStruct(f.shape, f.dtype) for f in fins),
                  input_output_aliases={w: w for w in range(n)},
                  scratch_shapes=[pltpu.SemaphoreType.DMA((n,)), pltpu.SemaphoreType.DMA((n,))])(*fins)


def _add_layers(buf, got, c_idx):
    _, _, K, N = buf.shape
    bk = _row_block(K, 256)

    def body(c_ref, a_ref, b_ref, o_ref):
        o_ref[...] = (a_ref[...].astype(F32) + b_ref[...].astype(F32)).astype(o_ref.dtype)

    blk = pl.BlockSpec((None, bk, N), lambda j, i, c_ref: (j, i, 0))
    grid_spec = pltpu.PrefetchScalarGridSpec(
        num_scalar_prefetch=1, grid=(N_CHIPS, K // bk),
        in_specs=[pl.BlockSpec((None, None, bk, N), lambda j, i, c_ref: (c_ref[0], j, i, 0)), blk], out_specs=blk)
    return _pcall(body, name="add_layers", grid_spec=grid_spec, out_shape=jax.ShapeDtypeStruct((N_CHIPS, K, N), BF16),
                  compiler_params=_params(("parallel", "parallel")))(c_idx, buf, got)


def _add_chip_sums(s1, got, idx):
    _, K, N = s1.shape
    bk = _row_block(K, 256)

    def body(idx_ref, a_ref, b_ref, o_ref):
        acc = a_ref[...].astype(F32)
        for k in range(3):
            acc = acc + b_ref[k].astype(F32)
        o_ref[...] = acc

    grid_spec = pltpu.PrefetchScalarGridSpec(
        num_scalar_prefetch=1, grid=(K // bk,),
        in_specs=[pl.BlockSpec((None, bk, N), lambda i, idx_ref: (idx_ref[0], i, 0)),
                  pl.BlockSpec((3, bk, N), lambda i, idx_ref: (0, i, 0))],
        out_specs=pl.BlockSpec((None, bk, N), lambda i, idx_ref: (idx_ref[1], i, 0)))
    return _pcall(body, name="add_chip_sums", grid_spec=grid_spec, out_shape=jax.ShapeDtypeStruct((2, K, N), F32),
                  compiler_params=_params(("parallel",)))(idx, s1, got)


def _allreduce_small(v):
    R, W = v.shape

    def body(v_ref, o_ref, buf, send_sems, recv_sems):
        x, y, c, _ = _place()
        me = 4 * x + 2 * y + c
        buf[0] = v_ref[...]
        cps = []
        for r in range(1, 8):
            peer = (x if not r & 4 else 1 - x, y if not r & 2 else 1 - y, c if not r & 1 else 1 - c)
            cp = _remote(v_ref, buf.at[r], send_sems, recv_sems, r - 1, peer)
            cp.start()
            cps.append(cp)
        for cp in cps:
            cp.wait()
        acc = buf[me]
        for d in range(1, 8):
            acc = acc + buf[jnp.bitwise_xor(me, d)]
        o_ref[...] = acc

    return _pcall(body, name="allreduce_small", out_shape=jax.ShapeDtypeStruct((R, W), F32),
                  in_specs=[pl.BlockSpec(memory_space=pltpu.VMEM)], out_specs=pl.BlockSpec(memory_space=pltpu.VMEM),
                  scratch_shapes=[pltpu.VMEM((8, R, W), F32), pltpu.SemaphoreType.DMA((7,)),
                                  pltpu.SemaphoreType.DMA((7,))])(v)


def _cat_cols(g):
    return g.transpose(1, 0, 2).reshape(g.shape[1], -1)


def _cut_cols(w):
    K, N = w.shape
    return w.reshape(K, N_CHIPS, N // N_CHIPS).transpose(1, 0, 2)


def _regroup_w_in(w):
    zeros = lambda n: jnp.zeros((w.shape[0], n), w.dtype)
    return jnp.concatenate([w[:, 0:768], w[:, 1152:1408], zeros(64), w[:, 1408:1440], zeros(32),
                            w[:, 768:1152], w[:, 1440:2208]], axis=1)


def _ungroup_w_in(w):
    return jnp.concatenate([w[:, 0:768], w[:, Z_Q:Z_Q + 384], w[:, Z_KV:Z_KV + 256],
                            w[:, Z_KR + 64:Z_KR + 96], w[:, Z_CV:Z_W]], axis=1)


def _regroup_w_uq(w):
    K = w.shape[0]
    return jnp.pad(w.reshape(K, MLA_HEADS, 96), ((0, 0), (0, 0), (0, 32))).reshape(K, MLA_HEADS * LANES)


def _ungroup_w_uq(w):
    K = w.shape[0]
    return w.reshape(K, MLA_HEADS, LANES)[:, :, :96].reshape(K, MLA_HEADS * 96)


def _regroup_w_ukv(w):
    K = w.shape[0]
    t = w.reshape(K, MLA_HEADS, 128)
    return jnp.concatenate([jnp.pad(t[:, :, :64], ((0, 0), (0, 0), (0, 64))).reshape(K, MLA_HEADS * LANES),
                            t[:, :, 64:].reshape(K, MLA_HEADS * HEAD_DIM)], axis=1)


def _ungroup_w_ukv(w):
    K = w.shape[0]
    return jnp.concatenate([w[:, :MLA_HEADS * LANES].reshape(K, MLA_HEADS, LANES)[:, :, :64],
                            w[:, MLA_HEADS * LANES:].reshape(K, MLA_HEADS, 64)], axis=2).reshape(K, 1024)


def _ffn_fwd(x, g_pre, g_post, wg, wu, wd, layer):
    h = _norm_fwd(x, g_pre, "norm_fwd")
    G, U, A = _ffn_gate_up(h, wg, wu, layer, "ffn_gate_up")
    Y = _mm([(A, (wd, layer))], "nn", F32, "ffn_down")
    return _resid_norm(x, Y, g_post, 0.5, "resid_norm"), dict(x=x, h=h, G=G, U=U, A=A, Y=Y)


def _ffn_bwd(dxo, sv, g_pre, g_post, wg, wu, wd, layer, bufs):
    dY, dg_post = _norm_bwd(sv["Y"], g_post, dxo, 0.5, None, BF16, "norm_bwd_post")
    dG, dU = _ffn_bwd_mid(dY, (wd, layer), sv["G"], sv["U"], "ffn_bwd_mid")
    bufs = (_dw_shards(sv["h"], dG, 1, layer, bufs[0], "ffn_dw_in"),
            _dw_shards(sv["h"], dU, 1, layer, bufs[1], "ffn_dw_in"),
            _dw_shards(sv["A"], dY, 0, layer, bufs[2], "ffn_dw_down"))
    dh = _ffn_dh(dG, dU, wg, wu, layer, "ffn_dh")
    dx, dg_pre = _norm_bwd(sv["x"], g_pre, dh, 1.0, dxo, F32, "norm_bwd_pre")
    return dx, dg_pre, dg_post, bufs


def _layer_fwd(x, p_l, layer, Gw, Wl, G, w_conv, cp, sp):
    sv = {}
    x1, sv["ffn1"] = _ffn_fwd(x, G["g_ffn1_pre"], G["g_ffn1_post"], Gw["w1_gate"], Gw["w1_up"], Gw["w1_down_rows"], layer)

    h2 = _norm_fwd(x1, G["g_mix_pre"], "norm_fwd")
    Z = _mm([(h2, Wl["w_in"])], "nn", F32, "mix_in")
    o_sb, tot_sb = _sb_fwd(Z)
    cq, ckv = Z[:, Z_Q:Z_Q + 384], Z[:, Z_KV:Z_KV + 256]
    nq = _norm_fwd(cq, G["g_mla_q"], "norm_fwd_q")
    Qf = _mm([(nq, Wl["w_mla_uq"])], "nn", F32, "mla_uq")
    nkv = _norm_fwd(ckv, G["g_mla_kv"], "norm_fwd_kv")
    KV = _mm([(nkv, Wl["w_mla_ukv"])], "nn", F32, "mla_ukv")
    qcat, kcat = _mla_prep(Qf, KV, Z, cp, sp)
    o_mla, lse = _mla_fwd(qcat, kcat, KV)
    y_cv = _conv_fwd(Z, w_conv)
    C = jnp.concatenate([o_sb, o_mla, y_cv], axis=1).astype(BF16)
    Mx = _mm([(C, (Gw["w_out_rows"], layer))], "nn", F32, "mix_out")
    x2 = _resid_norm(x1, Mx, G["g_mix_post"], 1.0, "resid_norm")
    sv["mix"] = dict(x=x1, h=h2, Z=Z, tot_sb=tot_sb, cq=cq, ckv=ckv, nq=nq, nkv=nkv, qcat=qcat, kcat=kcat, KV=KV,
                     o_mla=o_mla, lse=lse, C=C, Mx=Mx)

    x3, sv["ffn2"] = _ffn_fwd(x2, G["g_ffn2_pre"], G["g_ffn2_post"], Gw["w2_gate"], Gw["w2_up"], Gw["w2_down_rows"], layer)

    h4 = _norm_fwd(x3, G["g_ple_pre"], "norm_fwd")
    p16 = p_l.astype(BF16)
    Qg = _mm([(h4, (Gw["w_ple_gate_rows"], layer))], "nn", F32, "ple_gate")
    Pp = _ple_proj(p16, Gw["w_ple_proj"], layer)
    e = _ew(lambda q, pp: _sigmoid(q) * pp, [Qg, Pp], [(D_MODEL, F32)], "ple_mul")
    x4 = _resid_norm(x3, e, G["g_ple_post"], 1.0, "resid_norm")
    sv["ple"] = dict(x=x3, h=h4, p16=p16, Qg=Qg, Pp=Pp, e=e)
    return x4, sv


def _layer_bwd(dx4, sv, layer, Gw, Wl, G, w_conv, cp, sp, bufs):
    S = dx4.shape[0]
    gg, small = {}, {}

    s = sv["ple"]
    de, gg["g_ple_post"] = _norm_bwd(s["e"], G["g_ple_post"], dx4, 1.0, None, F32, "norm_bwd_e")

    def ple_bwd(dev, q, pp):
        sg = _sigmoid(q)
        return dev * pp * sg * (1.0 - sg), dev * sg

    dQg, dPp = _ew(ple_bwd, [de, s["Qg"], s["Pp"]], [(D_MODEL, BF16)] * 2, "ple_mul_bwd")
    bufs["w_ple_proj"] = _dw_shards(s["p16"], dPp, 1, layer, bufs["w_ple_proj"], "ple_dw_proj")
    bufs["w_ple_gate"] = _dw_shards(s["h"], dQg, 0, layer, bufs["w_ple_gate"], "ple_dw_gate")
    dh4 = _mm([(dQg, (Gw["w_ple_gate_rows"], layer))], "nt", F32, "ple_dh")
    dx3, gg["g_ple_pre"] = _norm_bwd(s["x"], G["g_ple_pre"], dh4, 1.0, dx4, F32, "norm_bwd_pre")

    dx2, gg["g_ffn2_pre"], gg["g_ffn2_post"], (bufs["w2_gate"], bufs["w2_up"], bufs["w2_down"]) = _ffn_bwd(
        dx3, sv["ffn2"], G["g_ffn2_pre"], G["g_ffn2_post"], Gw["w2_gate"], Gw["w2_up"], Gw["w2_down_rows"], layer,
        (bufs["w2_gate"], bufs["w2_up"], bufs["w2_down"]))

    s = sv["mix"]
    dM, gg["g_mix_post"] = _norm_bwd(s["Mx"], G["g_mix_post"], dx2, 1.0, None, BF16, "norm_bwd_post")
    dC = _mm([(dM, (Gw["w_out_rows"], layer))], "nt", F32, "mix_out_dx")
    bufs["w_out"] = _dw_shards(s["C"], dM, 0, layer, bufs["w_out"], "mix_out_dw")

    db, dc, dhh, gg["w_conv"] = _conv_bwd(s["Z"], w_conv, dC)

    dqc, dkc, dv = _mla_bwd(s["qcat"], s["kcat"], s["KV"], s["o_mla"], s["lse"], dC)
    dQf, dkr = _mla_prep_bwd(dqc, dkc, cp, sp)
    small["w_mla_uq"] = _ungroup_w_uq(_mm([(s["nq"], dQf)], "tn", BF16, "mla_uq_dw"))
    dnq = _mm([(dQf, Wl["w_mla_uq"])], "nt", F32, "mla_uq_dx")
    dcq, gg["g_mla_q"] = _norm_bwd(s["cq"], G["g_mla_q"], dnq, 1.0, None, F32, "norm_bwd_q")
    dkv = jnp.concatenate([dkc, dv], axis=1).astype(BF16)
    small["w_mla_ukv"] = _ungroup_w_ukv(_mm([(s["nkv"], dkv)], "tn", BF16, "mla_ukv_dw"))
    dnkv = _mm([(dkv, Wl["w_mla_ukv"])], "nt", F32, "mla_ukv_dx")
    dckv, gg["g_mla_kv"] = _norm_bwd(s["ckv"], G["g_mla_kv"], dnkv, 1.0, None, F32, "norm_bwd_kv")

    dsq, dsk, dsv = _sb_bwd(s["Z"], s["tot_sb"], dC)
    dZ = jnp.concatenate([dsq, dsk, dsv, dckv, dkr, dcq, db, dc, dhh], axis=1).astype(BF16)
    small["w_in"] = _ungroup_w_in(_mm([(s["h"], dZ)], "tn", BF16, "mix_in_dw"))
    dh2 = _mm([(dZ, Wl["w_in"])], "nt", F32, "mix_in_dx")
    dx1, gg["g_mix_pre"] = _norm_bwd(s["x"], G["g_mix_pre"], dh2, 1.0, dx2, F32, "norm_bwd_pre")

    dx0, gg["g_ffn1_pre"], gg["g_ffn1_post"], (bufs["w1_gate"], bufs["w1_up"], bufs["w1_down"]) = _ffn_bwd(
        dx1, sv["ffn1"], G["g_ffn1_pre"], G["g_ffn1_post"], Gw["w1_gate"], Gw["w1_up"], Gw["w1_down_rows"], layer,
        (bufs["w1_gate"], bufs["w1_up"], bufs["w1_down"]))
    return dx0, gg, small, bufs


def _pack_small(vecs):
    flat = jnp.concatenate([v.reshape(-1) for v in vecs])
    rows = -(-flat.shape[0] // (8 * LANES)) * 8
    return jnp.pad(flat, (0, rows * LANES - flat.shape[0])).reshape(rows, LANES)


def kernel(x, p, positions, g_ffn1_pre, w1_gate, w1_up, w1_down, g_ffn1_post, g_mix_pre, w_in, g_mla_q, w_mla_uq, g_mla_kv, w_mla_ukv, w_conv, w_out, g_mix_post, g_ffn2_pre, w2_gate, w2_up, w2_down, g_ffn2_post, g_ple_pre, w_ple_gate, w_ple_proj, g_ple_post, loss_target, m_g_ffn1_pre, m_w1_gate, m_w1_up, m_w1_down, m_g_ffn1_post, m_g_mix_pre, m_w_in, m_g_mla_q, m_w_mla_uq, m_g_mla_kv, m_w_mla_ukv, m_w_conv, m_w_out, m_g_mix_post, m_g_ffn2_pre, m_w2_gate, m_w2_up, m_w2_down, m_g_ffn2_post, m_g_ple_pre, m_w_ple_gate, m_w_ple_proj, m_g_ple_post, v_g_ffn1_pre, v_w1_gate, v_w1_up, v_w1_down, v_g_ffn1_post, v_g_mix_pre, v_w_in, v_g_mla_q, v_w_mla_uq, v_g_mla_kv, v_w_mla_ukv, v_w_conv, v_w_out, v_g_mix_post, v_g_ffn2_pre, v_w2_gate, v_w2_up, v_w2_down, v_g_ffn2_post, v_g_ple_pre, v_w_ple_gate, v_w_ple_proj, v_g_ple_post):
    w = dict(g_ffn1_pre=g_ffn1_pre, w1_gate=w1_gate, w1_up=w1_up, w1_down=w1_down, g_ffn1_post=g_ffn1_post,
             g_mix_pre=g_mix_pre, w_in=w_in, g_mla_q=g_mla_q, w_mla_uq=w_mla_uq, g_mla_kv=g_mla_kv,
             w_mla_ukv=w_mla_ukv, w_conv=w_conv, w_out=w_out, g_mix_post=g_mix_post, g_ffn2_pre=g_ffn2_pre,
             w2_gate=w2_gate, w2_up=w2_up, w2_down=w2_down, g_ffn2_post=g_ffn2_post, g_ple_pre=g_ple_pre,
             w_ple_gate=w_ple_gate, w_ple_proj=w_ple_proj, g_ple_post=g_ple_post)
    m = dict(g_ffn1_pre=m_g_ffn1_pre, w1_gate=m_w1_gate, w1_up=m_w1_up, w1_down=m_w1_down, g_ffn1_post=m_g_ffn1_post,
             g_mix_pre=m_g_mix_pre, w_in=m_w_in, g_mla_q=m_g_mla_q, w_mla_uq=m_w_mla_uq, g_mla_kv=m_g_mla_kv,
             w_mla_ukv=m_w_mla_ukv, w_conv=m_w_conv, w_out=m_w_out, g_mix_post=m_g_mix_post, g_ffn2_pre=m_g_ffn2_pre,
             w2_gate=m_w2_gate, w2_up=m_w2_up, w2_down=m_w2_down, g_ffn2_post=m_g_ffn2_post, g_ple_pre=m_g_ple_pre,
             w_ple_gate=m_w_ple_gate, w_ple_proj=m_w_ple_proj, g_ple_post=m_g_ple_post)
    v = dict(g_ffn1_pre=v_g_ffn1_pre, w1_gate=v_w1_gate, w1_up=v_w1_up, w1_down=v_w1_down, g_ffn1_post=v_g_ffn1_post,
             g_mix_pre=v_g_mix_pre, w_in=v_w_in, g_mla_q=v_g_mla_q, w_mla_uq=v_w_mla_uq, g_mla_kv=v_g_mla_kv,
             w_mla_ukv=v_w_mla_ukv, w_conv=v_w_conv, w_out=v_w_out, g_mix_post=v_g_mix_post, g_ffn2_pre=v_g_ffn2_pre,
             w2_gate=v_w2_gate, w2_up=v_w2_up, w2_down=v_w2_down, g_ffn2_post=v_g_ffn2_post, g_ple_pre=v_g_ple_pre,
             w_ple_gate=v_w_ple_gate, w_ple_proj=v_w_ple_proj, g_ple_post=v_g_ple_post)

    depth = g_ffn1_pre.shape[0]
    assert depth == 2, "sibling cores split every exchange by layer"
    S = x.shape[1]
    cx, cy, cc = lax.axis_index("x"), lax.axis_index("y"), lax.axis_index("c")
    chip = 2 * cx + cy

    def padded(name):
        (ks, ns), (kp, np_) = BIG[name]
        return jnp.pad(w[name].astype(BF16), ((0, 0), (0, kp - ks), (0, np_ - ns)))

    mine = [padded(name) for name in BIG_NAMES]
    got = _gather_weights(mine)
    Gw = {name: lax.dynamic_update_slice(g, s[:, None], (0, chip, 0, 0)) for name, g, s in zip(BIG_NAMES, got, mine)}
    for name in ("w1_down", "w2_down", "w_out", "w_ple_gate"):
        g = Gw[name]
        Gw[name + "_rows"] = g.reshape(2, N_CHIPS * g.shape[2], g.shape[3])

    conv_slot = lax.dynamic_update_slice(jnp.zeros((depth, 3, 256), F32),
                                         w_conv * (cc == 0).astype(F32), (0, 0, 64 * chip))
    w_conv_full = _allreduce_small(_pack_small([conv_slot]))[:depth * 3 * 256 // LANES].reshape(depth, 3, 256)

    inv = ROPE_BASE ** (-jnp.arange(ROPE_HALF, dtype=F32) / ROPE_HALF)
    zeros = lambda n: jnp.zeros((n,), F32)
    ones = jnp.ones((ROPE_HALF,), F32)
    inv_pat = jnp.concatenate([zeros(64), inv, inv, zeros(32)]).reshape(1, LANES)
    sign_pat = jnp.concatenate([zeros(64), -ones, ones, zeros(32)]).reshape(1, LANES)
    cp, sp = _rope_tables(positions.reshape(S, 1), inv_pat, sign_pat)

    layers = []
    for i in range(depth):
        Wl = {"w_in": _regroup_w_in(_cat_cols(Gw["w_in"][i])),
              "w_mla_uq": _regroup_w_uq(_cat_cols(Gw["w_mla_uq"][i])),
              "w_mla_ukv": _regroup_w_ukv(_cat_cols(Gw["w_mla_ukv"][i]))}
        Gl = {name: w[name][i].reshape(1, n) for name, n in GAINS}
        layers.append((Wl, Gl))

    xs, saved = x[0], []
    for i in range(depth):
        xs, sv = _layer_fwd(xs, p[i, 0], i, Gw, layers[i][0], layers[i][1], w_conv_full[i], cp, sp)
        saved.append(sv)

    loss_part, dx = _loss_and_grad(xs, loss_target[0])
    loss = lax.psum(loss_part[0, 0], AXES)

    bufs = {name: None for name in BIG_NAMES}
    ggs, smalls = [None] * depth, [None] * depth
    for i in reversed(range(depth)):
        dx, ggs[i], smalls[i], bufs = _layer_bwd(dx, saved[i], i, Gw, layers[i][0], layers[i][1], w_conv_full[i],
                                                 cp, sp, bufs)
    for name in ("w_in", "w_mla_uq", "w_mla_ukv"):
        bufs[name] = jnp.stack([_cut_cols(smalls[i][name]) for i in range(depth)])

    c_idx = cc.reshape(1).astype(jnp.int32)
    idx2 = jnp.stack([chip, cc]).astype(jnp.int32)
    full = [bufs[name] for name in BIG_NAMES]
    sib = _swap_layers(full)
    s1 = [_add_layers(b, g, c_idx) for b, g in zip(full, sib)]
    arrived = _scatter_chip_sums(s1)
    fins = _join_layers([_add_chip_sums(a, g, idx2) for a, g in zip(s1, arrived)])
    grads = {name: f[:, :BIG[name][0][0], :BIG[name][0][1]] for name, f in zip(BIG_NAMES, fins)}

    small_names = [name for name, _ in GAINS] + ["w_conv"]
    small = _allreduce_small(_pack_small([jnp.stack([ggs[i][name] for i in range(depth)]) for name in small_names]))
    flat, off = small.reshape(-1), 0
    for name, n in GAINS:
        grads[name] = flat[off:off + depth * n].reshape(depth, n)
        off += depth * n
    conv_full = flat[off:off + depth * 3 * 256].reshape(depth, 3, 256)
    grads["w_conv"] = lax.dynamic_slice(conv_full, (0, 0, 64 * chip), (depth, 3, 64))

    deltas, new_m, new_v = {}, {}, {}
    for name in WEIGHT_ORDER:
        shape = w[name].shape
        two_d = (shape[0] * shape[1], shape[2]) if len(shape) == 3 else shape
        d_, m_, v_ = _adamw(w[name].reshape(two_d), grads[name].reshape(two_d), m[name].reshape(two_d),
                            v[name].reshape(two_d), "adamw")
        deltas[name], new_m[name], new_v[name] = d_.reshape(shape), m_.reshape(shape), v_.reshape(shape)

    return (loss, dx[None], *[grads[n] for n in WEIGHT_ORDER], *[deltas[n] for n in WEIGHT_ORDER],
            *[new_m[n] for n in WEIGHT_ORDER], *[new_v[n] for n in WEIGHT_ORDER])
```

```python
import functools

import jax
import jax.numpy as jnp
from jax import lax
from jax.experimental import pallas as pl
from jax.experimental.pallas import tpu as pltpu

F32 = jnp.float32
BF16 = jnp.bfloat16
MESH = pl.DeviceIdType.MESH
AXES = ("x", "y", "c")

D_MODEL = 1024
N_CHIPS = 4
FF_SHARD = 704
FF_PAD = 768
D_FF_PAD = N_CHIPS * FF_PAD
EPS = 1e-6
NEG_INF = -1e30
ROPE_BASE = 10000.0
ROPE_HALF = 16
LANES = 128
SB_HEADS, MLA_HEADS, HEAD_DIM = 4, 8, 64
MLA_SCALE = 96.0 ** -0.5
SB_SCALE = 64.0 ** -0.5
SB_BQ, SB_BK = 256, 128
MLA_BQ, MLA_BK = 256, 256
ADAM_LR, ADAM_B1, ADAM_B2, ADAM_EPS, ADAM_WD, ADAM_STEP = 0.001, 0.9, 0.999, 1e-08, 0.01, 10
VMEM_LIMIT = 48 * 2 ** 20

Z_SB, Z_KV, Z_KR, Z_Q, Z_CV, Z_W = 0, 768, 1024, 1152, 1536, 2304

NT = (((1,), (1,)), ((), ()))
TN = (((0,), (0,)), ((), ()))

BIG = {"w1_gate": ((1024, FF_SHARD), (1024, FF_PAD)), "w1_up": ((1024, FF_SHARD), (1024, FF_PAD)),
       "w1_down": ((FF_SHARD, 1024), (FF_PAD, 1024)), "w_in": ((1024, 552), (1024, 552)),
       "w_mla_uq": ((384, 192), (384, 192)), "w_mla_ukv": ((256, 256), (256, 256)),
       "w_out": ((256, 1024), (256, 1024)),
       "w2_gate": ((1024, FF_SHARD), (1024, FF_PAD)), "w2_up": ((1024, FF_SHARD), (1024, FF_PAD)),
       "w2_down": ((FF_SHARD, 1024), (FF_PAD, 1024)), "w_ple_gate": ((256, 1024), (256, 1024)),
       "w_ple_proj": ((256, 256), (256, 256))}
BIG_NAMES = tuple(BIG)
GAINS = (("g_ffn1_pre", 1024), ("g_ffn1_post", 1024), ("g_mix_pre", 1024), ("g_mla_q", 384),
         ("g_mla_kv", 256), ("g_mix_post", 1024), ("g_ffn2_pre", 1024), ("g_ffn2_post", 1024),
         ("g_ple_pre", 1024), ("g_ple_post", 1024))
WEIGHT_ORDER = ("g_ffn1_pre", "w1_gate", "w1_up", "w1_down", "g_ffn1_post", "g_mix_pre", "w_in", "g_mla_q",
                "w_mla_uq", "g_mla_kv", "w_mla_ukv", "w_conv", "w_out", "g_mix_post", "g_ffn2_pre", "w2_gate",
                "w2_up", "w2_down", "g_ffn2_post", "g_ple_pre", "w_ple_gate", "w_ple_proj", "g_ple_post")

_pcall = pl.pallas_call


def _params(sem=None):
    return pltpu.CompilerParams(dimension_semantics=sem, vmem_limit_bytes=VMEM_LIMIT)


def _dot(a, b, dims=None):
    a, b = a.astype(BF16), b.astype(BF16)
    if dims is None:
        return jnp.dot(a, b, preferred_element_type=F32)
    return lax.dot_general(a, b, dims, preferred_element_type=F32)


def _rstd(v):
    return lax.rsqrt(jnp.mean(v * v, axis=-1, keepdims=True) + EPS)


def _sigmoid(v):
    return 0.5 * jnp.tanh(0.5 * v) + 0.5


def _row_block(n, want):
    b = min(n, want)
    while n % b:
        b //= 2
    return b


def _spec(block, index, lead=None):
    if not lead:
        return pl.BlockSpec(block, index)
    lead = tuple(lead)
    return pl.BlockSpec((None,) * len(lead) + tuple(block), lambda *g: lead + tuple(index(*g)))


def _arr(op):
    return op[0] if isinstance(op, tuple) else op


def _lead(op):
    return tuple(op[1:]) if isinstance(op, tuple) else ()


def _mm(pairs, mode, out_dtype, name, bm=512, bn=512):
    a0, b0 = _arr(pairs[0][0]), _arr(pairs[0][1])
    if mode == "nn":
        M, N = a0.shape[-2], b0.shape[-1]
    elif mode == "nt":
        M, N = a0.shape[-2], b0.shape[-2]
    else:
        M, N = a0.shape[-1], b0.shape[-1]
    bm, bn = _row_block(M, bm), _row_block(N, bn)
    n_pairs = len(pairs)
    dims = {"nn": None, "nt": NT, "tn": TN}[mode]

    def body(*refs):
        acc = None
        for t in range(n_pairs):
            part = _dot(refs[2 * t][...], refs[2 * t + 1][...], dims)
            acc = part if acc is None else acc + part
        refs[-1][...] = acc.astype(refs[-1].dtype)

    in_specs, ops = [], []
    for a, b in pairs:
        sa, sb = _arr(a).shape, _arr(b).shape
        if mode == "nn":
            in_specs += [_spec((bm, sa[-1]), lambda j, i: (i, 0), _lead(a)),
                         _spec((sb[-2], bn), lambda j, i: (0, j), _lead(b))]
        elif mode == "nt":
            in_specs += [_spec((bm, sa[-1]), lambda j, i: (i, 0), _lead(a)),
                         _spec((bn, sb[-1]), lambda j, i: (j, 0), _lead(b))]
        else:
            in_specs += [_spec((sa[-2], bm), lambda j, i: (0, i), _lead(a)),
                         _spec((sb[-2], bn), lambda j, i: (0, j), _lead(b))]
        ops += [_arr(a), _arr(b)]
    return _pcall(body, name=name, grid=(N // bn, M // bm), in_specs=in_specs,
                  out_specs=pl.BlockSpec((bm, bn), lambda j, i: (i, j)),
                  out_shape=jax.ShapeDtypeStruct((M, N), out_dtype),
                  compiler_params=_params(("parallel", "parallel")))(*ops)


def _dw_shards(a, b, axis, layer, into, name):
    S = a.shape[0]
    if axis == 1:
        Kp, Np = a.shape[1], b.shape[1] // N_CHIPS
        bm, bn = _row_block(Kp, 512), Np
        grid = (N_CHIPS, Kp // bm)
        a_spec = pl.BlockSpec((S, bm), lambda j, i: (0, i))
        b_spec = pl.BlockSpec((S, Np), lambda j, i: (0, j))
        o_spec = pl.BlockSpec((None, None, bm, Np), lambda j, i: (layer, j, i, 0))
    else:
        Kp, Np = a.shape[1] // N_CHIPS, b.shape[1]
        bn = _row_block(Np, 512)
        grid = (N_CHIPS, Np // bn)
        a_spec = pl.BlockSpec((S, Kp), lambda j, i: (0, j))
        b_spec = pl.BlockSpec((S, bn), lambda j, i: (0, i))
        o_spec = pl.BlockSpec((None, None, Kp, bn), lambda j, i: (layer, j, 0, i))
    shape = jax.ShapeDtypeStruct((2, N_CHIPS, Kp, Np), BF16)

    if into is None:
        def body(a_ref, b_ref, o_ref):
            o_ref[...] = _dot(a_ref[...], b_ref[...], TN).astype(BF16)

        return _pcall(body, name=name, grid=grid, in_specs=[a_spec, b_spec], out_specs=o_spec, out_shape=shape,
                      compiler_params=_params(("parallel", "parallel")))(a, b)

    def body_into(a_ref, b_ref, buf_ref, o_ref):
        o_ref[...] = _dot(a_ref[...], b_ref[...], TN).astype(BF16)

    return _pcall(body_into, name=name + "_into", grid=grid,
                  in_specs=[a_spec, b_spec, pl.BlockSpec(memory_space=pl.ANY)], out_specs=o_spec, out_shape=shape,
                  input_output_aliases={2: 0}, compiler_params=_params(("parallel", "parallel")))(a, b, into)


def _ew(fn, ins, outs, name, br=256):
    S = max(a.shape[0] for a in ins)
    br = _row_block(S, br)
    n_in = len(ins)

    def body(*refs):
        res = fn(*[r[...] for r in refs[:n_in]])
        if not isinstance(res, tuple):
            res = (res,)
        for r, v in zip(refs[n_in:], res):
            r[...] = v.astype(r.dtype)

    in_specs = [pl.BlockSpec((br, a.shape[1]), lambda i: (i, 0)) if a.shape[0] == S and S > 1
                else pl.BlockSpec(a.shape, lambda i: (0, 0)) for a in ins]
    out = _pcall(body, name=name, grid=(S // br,), in_specs=in_specs,
                 out_specs=tuple(pl.BlockSpec((br, w), lambda i: (i, 0)) for w, _ in outs),
                 out_shape=tuple(jax.ShapeDtypeStruct((S, w), dt) for w, dt in outs),
                 compiler_params=_params(("parallel",)))(*ins)
    return out if len(outs) > 1 else out[0]


def _norm_fwd(x, g, name):
    return _ew(lambda xv, gv: xv * _rstd(xv) * gv, [x, g], [(x.shape[1], BF16)], name, br=512)


def _resid_norm(x, y, g, alpha, name):
    return _ew(lambda xv, yv, gv: xv + alpha * (yv * _rstd(yv) * gv), [x, y, g], [(x.shape[1], F32)], name)


def _norm_bwd(xin, g, dy, alpha, resid, out_dtype, name):
    S, W = xin.shape
    br = _row_block(S, 256)
    has_res = resid is not None

    def body(*refs):
        x_ref, g_ref, dy_ref = refs[:3]
        dx_ref, dg_ref = refs[-2:]
        xv, dyv = x_ref[...], dy_ref[...] * alpha
        r = _rstd(xv)
        xh = xv * r
        u = dyv * g_ref[...]
        dx = r * (u - xh * jnp.mean(u * xh, axis=-1, keepdims=True))
        if has_res:
            dx = dx + refs[3][...]
        dx_ref[...] = dx.astype(dx_ref.dtype)
        part = jnp.sum(dyv * xh, axis=0, keepdims=True)

        @pl.when(pl.program_id(0) == 0)
        def _():
            dg_ref[...] = part

        @pl.when(pl.program_id(0) > 0)
        def _():
            dg_ref[...] += part

    row = pl.BlockSpec((br, W), lambda i: (i, 0))
    vec = pl.BlockSpec((1, W), lambda i: (0, 0))
    ops = [xin, g, dy] + ([resid] if has_res else [])
    return _pcall(body, name=name, grid=(S // br,), in_specs=[row, vec, row] + ([row] if has_res else []),
                  out_specs=(row, vec),
                  out_shape=(jax.ShapeDtypeStruct((S, W), out_dtype), jax.ShapeDtypeStruct((1, W), F32)),
                  compiler_params=_params(("arbitrary",)))(*ops)


def _ffn_gate_up(h, wg, wu, layer, name):
    S, K = h.shape
    bm, bn = _row_block(S, 1024), FF_PAD
    per = FF_PAD // bn

    def body(h_ref, wg_ref, wu_ref, g_ref, u_ref, a_ref):
        hv = h_ref[...]
        g = _dot(hv, wg_ref[...])
        u = _dot(hv, wu_ref[...])
        g_ref[...] = g.astype(BF16)
        u_ref[...] = u.astype(BF16)
        a_ref[...] = (g * _sigmoid(g) * u).astype(BF16)

    blk = pl.BlockSpec((bm, bn), lambda n, i: (i, n))
    wsp = pl.BlockSpec((None, None, K, bn), lambda n, i: (layer, n // per, 0, n % per))
    return _pcall(body, name=name, grid=(D_FF_PAD // bn, S // bm),
                  in_specs=[pl.BlockSpec((bm, K), lambda n, i: (i, 0)), wsp, wsp],
                  out_specs=(blk, blk, blk), out_shape=(jax.ShapeDtypeStruct((S, D_FF_PAD), BF16),) * 3,
                  compiler_params=_params(("parallel", "parallel")))(h, wg, wu)


def _ffn_bwd_mid(dy, wd, g, u, name):
    S, D = dy.shape
    F = g.shape[1]
    bm, bn = _row_block(S, 1024), 256

    def body(dy_ref, wd_ref, g_ref, u_ref, dg_ref, du_ref):
        da = _dot(dy_ref[...], wd_ref[...], NT)
        gv, uv = g_ref[...].astype(F32), u_ref[...].astype(F32)
        s = _sigmoid(gv)
        dg_ref[...] = (da * uv * (s * (1.0 + gv * (1.0 - s)))).astype(BF16)
        du_ref[...] = (da * (gv * s)).astype(BF16)

    blk = pl.BlockSpec((bm, bn), lambda j, i: (i, j))
    return _pcall(body, name=name, grid=(F // bn, S // bm),
                  in_specs=[pl.BlockSpec((bm, D), lambda j, i: (i, 0)),
                            _spec((bn, D), lambda j, i: (j, 0), _lead(wd)), blk, blk],
                  out_specs=(blk, blk), out_shape=(jax.ShapeDtypeStruct((S, F), BF16),) * 2,
                  compiler_params=_params(("parallel", "parallel")))(dy, _arr(wd), g, u)


def _ffn_dh(dg, du, wg, wu, layer, name):
    S = dg.shape[0]
    D = wg.shape[2]
    bm, bn = _row_block(S, 1024), D

    def body(dg_ref, du_ref, wg_ref, wu_ref, o_ref, acc):
        j = pl.program_id(2)
        part = _dot(dg_ref[...], wg_ref[...], NT) + _dot(du_ref[...], wu_ref[...], NT)

        @pl.when(j == 0)
        def _():
            acc[...] = part

        @pl.when(j > 0)
        def _():
            acc[...] += part

        @pl.when(j == N_CHIPS - 1)
        def _():
            o_ref[...] = acc[...]

    asp = pl.BlockSpec((bm, FF_PAD), lambda n, i, j: (i, j))
    wsp = pl.BlockSpec((None, None, bn, FF_PAD), lambda n, i, j: (layer, j, n, 0))
    return _pcall(body, name=name, grid=(D // bn, S // bm, N_CHIPS), in_specs=[asp, asp, wsp, wsp],
                  out_specs=pl.BlockSpec((bm, bn), lambda n, i, j: (i, n)),
                  out_shape=jax.ShapeDtypeStruct((S, D), F32), scratch_shapes=[pltpu.VMEM((bm, bn), F32)],
                  compiler_params=_params(("parallel", "parallel", "arbitrary")))(dg, du, wg, wu)


def _ple_proj(p16, w, layer):
    S, K = p16.shape
    bm = _row_block(S, 1024)

    def body(p_ref, w_ref, o_ref):
        o_ref[...] = _dot(p_ref[...], w_ref[...])

    return _pcall(body, name="ple_proj", grid=(N_CHIPS, S // bm),
                  in_specs=[pl.BlockSpec((bm, K), lambda j, i: (i, 0)),
                            pl.BlockSpec((None, None, K, 256), lambda j, i: (layer, j, 0, 0))],
                  out_specs=pl.BlockSpec((bm, 256), lambda j, i: (i, j)),
                  out_shape=jax.ShapeDtypeStruct((S, N_CHIPS * 256), F32),
                  compiler_params=_params(("parallel", "parallel")))(p16, w)


def _loss_and_grad(xf, tgt):
    S, W = xf.shape
    br = _row_block(S, 256)

    def body(x_ref, t_ref, l_ref, d_ref):
        d = x_ref[...] - t_ref[...]
        d_ref[...] = d * (1.0 / W)
        part = 0.5 * jnp.sum(jnp.sum(d * d, axis=-1, keepdims=True) * (1.0 / W), axis=0, keepdims=True)

        @pl.when(pl.program_id(0) == 0)
        def _():
            l_ref[...] = part

        @pl.when(pl.program_id(0) > 0)
        def _():
            l_ref[...] += part

    row = pl.BlockSpec((br, W), lambda i: (i, 0))
    return _pcall(body, name="loss", grid=(S // br,), in_specs=[row, row],
                  out_specs=(pl.BlockSpec((1, 1), lambda i: (0, 0)), row),
                  out_shape=(jax.ShapeDtypeStruct((1, 1), F32), jax.ShapeDtypeStruct((S, W), F32)),
                  compiler_params=_params(("arbitrary",)))(xf, tgt)


def _adamw(w, g, m, v, name):
    R, C = w.shape
    br = _row_block(R, 256) if R % 8 == 0 else R

    def body(w_ref, g_ref, m_ref, v_ref, d_ref, nm_ref, nv_ref):
        gv = g_ref[...]
        nm = ADAM_B1 * m_ref[...] + (1.0 - ADAM_B1) * gv
        nv = ADAM_B2 * v_ref[...] + (1.0 - ADAM_B2) * (gv * gv)
        m_hat = nm / (1.0 - ADAM_B1 ** ADAM_STEP)
        v_hat = nv / (1.0 - ADAM_B2 ** ADAM_STEP)
        d_ref[...] = -ADAM_LR * (m_hat / (jnp.sqrt(v_hat) + ADAM_EPS) + ADAM_WD * w_ref[...])
        nm_ref[...] = nm
        nv_ref[...] = nv

    blk = pl.BlockSpec((br, C), lambda i: (i, 0))
    return _pcall(body, name=name, grid=(R // br,), in_specs=[blk] * 4, out_specs=(blk,) * 3,
                  out_shape=(jax.ShapeDtypeStruct((R, C), F32),) * 3,
                  compiler_params=_params(("parallel",)))(w, g, m, v)


def _rope_tables(pos, inv_pat, sign_pat):
    def fn(p, iv, sg):
        ang = p.astype(F32) * iv
        return jnp.cos(ang), jnp.sin(ang) * sg

    return _ew(fn, [pos, inv_pat, sign_pat], [(LANES, F32)] * 2, "rope_tables")


def _swap_halves_of_rope(v):
    W = v.shape[1]
    lane = lax.broadcasted_iota(jnp.int32, (1, W), 1) % LANES
    return jnp.where((lane >= 64) & (lane < 80), pltpu.roll(v, W - ROPE_HALF, 1),
                     jnp.where((lane >= 80) & (lane < 96), pltpu.roll(v, ROPE_HALF, 1), 0.0))


def _tile_lanes(v, n):
    return jnp.concatenate([v] * n, axis=1)


def _mla_prep(qf, kv, z, cp, sp):
    S = qf.shape[0]
    br = _row_block(S, 256)
    W = MLA_HEADS * LANES

    def body(q_ref, k_ref, r_ref, c_ref, s_ref, qo_ref, ko_ref):
        c, s = c_ref[...], s_ref[...]
        q = q_ref[...]
        qo_ref[...] = (q * _tile_lanes(c, MLA_HEADS) + _swap_halves_of_rope(q) * _tile_lanes(s, MLA_HEADS)).astype(BF16)
        r = r_ref[...]
        lane = lax.broadcasted_iota(jnp.int32, (1, LANES), 1)
        kr = jnp.where((lane >= 64) & (lane < 96), r * c + _swap_halves_of_rope(r) * s, 0.0)
        ko_ref[...] = (k_ref[...] + _tile_lanes(kr, MLA_HEADS)).astype(BF16)

    wide = pl.BlockSpec((br, W), lambda i: (i, 0))
    one = pl.BlockSpec((br, LANES), lambda i: (i, 0))
    return _pcall(body, name="mla_prep", grid=(S // br,),
                  in_specs=[wide, wide, pl.BlockSpec((br, LANES), lambda i: (i, Z_KR // LANES)), one, one],
                  out_specs=(wide, wide), out_shape=(jax.ShapeDtypeStruct((S, W), BF16),) * 2,
                  compiler_params=_params(("parallel",)))(qf, kv, z, cp, sp)


def _mla_prep_bwd(dq, dk, cp, sp):
    S, W = dq.shape
    br = _row_block(S, 256)

    def body(dq_ref, dk_ref, c_ref, s_ref, dqo_ref, dr_ref):
        c, s = c_ref[...], s_ref[...]
        d = dq_ref[...]
        dqo_ref[...] = (d * _tile_lanes(c, MLA_HEADS) + _swap_halves_of_rope(d * _tile_lanes(s, MLA_HEADS))).astype(BF16)
        dkv = dk_ref[...]
        tot = dkv[:, 0:LANES]
        for h in range(1, MLA_HEADS):
            tot = tot + dkv[:, h * LANES:(h + 1) * LANES]
        lane = lax.broadcasted_iota(jnp.int32, (1, LANES), 1)
        tot = jnp.where((lane >= 64) & (lane < 96), tot, 0.0)
        dr_ref[...] = tot * c + _swap_halves_of_rope(tot * s)

    wide = pl.BlockSpec((br, W), lambda i: (i, 0))
    one = pl.BlockSpec((br, LANES), lambda i: (i, 0))
    return _pcall(body, name="mla_prep_bwd", grid=(S // br,), in_specs=[wide, wide, one, one], out_specs=(wide, one),
                  out_shape=(jax.ShapeDtypeStruct((S, W), BF16), jax.ShapeDtypeStruct((S, LANES), F32)),
                  compiler_params=_params(("parallel",)))(dq, dk, cp, sp)


def _shift_down(v, k, row):
    return jnp.where(row >= k, pltpu.roll(v, k, 0), 0.0)


def _shift_up(v, k, row):
    n = v.shape[0]
    return jnp.where(row < n - k, pltpu.roll(v, n - k, 0), 0.0)


def _conv_fwd(z, w):
    S = z.shape[0]
    c0 = Z_CV // LANES

    def body(b_ref, c_ref, h_ref, w_ref, y_ref):
        u = c_ref[...] * h_ref[...]
        row = lax.broadcasted_iota(jnp.int32, u.shape, 0)
        wv = w_ref[...]
        conv = wv[0:1] * _shift_down(u, 2, row) + wv[1:2] * _shift_down(u, 1, row) + wv[2:3] * u
        y_ref[...] = b_ref[...] * conv

    def col(k):
        return pl.BlockSpec((S, LANES), lambda j: (0, c0 + 2 * k + j))

    return _pcall(body, name="conv_fwd", grid=(2,),
                  in_specs=[col(0), col(1), col(2), pl.BlockSpec((3, LANES), lambda j: (0, j))],
                  out_specs=pl.BlockSpec((S, LANES), lambda j: (0, j)),
                  out_shape=jax.ShapeDtypeStruct((S, 256), F32), compiler_params=_params(("parallel",)))(z, z, z, w)


def _conv_bwd(z, w, dc_all):
    S = z.shape[0]
    c0 = Z_CV // LANES

    def body(b_ref, c_ref, h_ref, w_ref, dy_ref, db_ref, dc_ref, dh_ref, dw_ref):
        cv, hv, bv, dyv = c_ref[...], h_ref[...], b_ref[...], dy_ref[...]
        u = cv * hv
        row = lax.broadcasted_iota(jnp.int32, u.shape, 0)
        wv = w_ref[...]
        u1, u2 = _shift_down(u, 1, row), _shift_down(u, 2, row)
        conv = wv[0:1] * u2 + wv[1:2] * u1 + wv[2:3] * u
        db_ref[...] = dyv * conv
        dconv = dyv * bv
        du = wv[2:3] * dconv + wv[1:2] * _shift_up(dconv, 1, row) + wv[0:1] * _shift_up(dconv, 2, row)
        dc_ref[...] = du * hv
        dh_ref[...] = du * cv
        dw_ref[0:1, :] = jnp.sum(dconv * u2, axis=0, keepdims=True)
        dw_ref[1:2, :] = jnp.sum(dconv * u1, axis=0, keepdims=True)
        dw_ref[2:3, :] = jnp.sum(dconv * u, axis=0, keepdims=True)

    def col(k):
        return pl.BlockSpec((S, LANES), lambda j: (0, c0 + 2 * k + j))

    wsp = pl.BlockSpec((3, LANES), lambda j: (0, j))
    osp = pl.BlockSpec((S, LANES), lambda j: (0, j))
    db, dc, dh, dw = _pcall(
        body, name="conv_bwd", grid=(2,),
        in_specs=[col(0), col(1), col(2), wsp, pl.BlockSpec((S, LANES), lambda j: (0, 6 + j))],
        out_specs=(osp, osp, osp, wsp),
        out_shape=(jax.ShapeDtypeStruct((S, 256), F32),) * 3 + (jax.ShapeDtypeStruct((3, 256), F32),),
        compiler_params=_params(("parallel",)))(z, z, z, w, dc_all)
    return db, dc, dh, dw


def _half_masks():
    lane = lax.broadcasted_iota(jnp.int32, (1, LANES), 1)
    return lane < HEAD_DIM, lane >= HEAD_DIM


def _pair(v, p):
    return v[:, p * LANES:(p + 1) * LANES]


def _keep(mask, v):
    return jnp.where(mask, v, jnp.zeros_like(v))


def _lane_col(vals):
    lane = lax.broadcasted_iota(jnp.int32, (1, LANES), 1)
    out = jnp.zeros((vals[0].shape[0], LANES), F32)
    for h, v in enumerate(vals):
        out = jnp.where(lane == h, v, out)
    return out


def _mla_fwd(q, k, kv):
    S = q.shape[0]
    H = MLA_HEADS
    bq, bk = _row_block(S, MLA_BQ), _row_block(S, MLA_BK)
    per = bq // bk

    def body(q_ref, k_ref, v_ref, o_ref, lse_ref):
        i = pl.program_id(0)
        lo, hi = _half_masks()
        qb = q_ref[...]

        def block(j, carry, ok):
            ms, ls, accs = carry
            off = pl.multiple_of(j * bk, bk)
            kb = k_ref[pl.ds(off, bk), :]
            vb = v_ref[pl.ds(off, bk), :].astype(BF16)
            ss = [_dot(_pair(qb, h), _pair(kb, h), NT) * MLA_SCALE for h in range(H)]
            if ok is not None:
                ss = [jnp.where(ok, s, NEG_INF) for s in ss]
            ms2 = [jnp.maximum(ms[h], jnp.max(ss[h], axis=-1, keepdims=True)) for h in range(H)]
            ps = [jnp.exp(ss[h] - ms2[h]) for h in range(H)]
            al = [jnp.exp(ms[h] - ms2[h]) for h in range(H)]
            ls2 = [al[h] * ls[h] + jnp.sum(ps[h], axis=-1, keepdims=True) for h in range(H)]
            pvs = [_dot(ps[h], _pair(vb, h // 2)) for h in range(H)]
            accs2 = []
            for p in range(H // 2):
                scale = jnp.where(lo, al[2 * p], al[2 * p + 1])
                accs2.append(scale * accs[p] + jnp.where(lo, pvs[2 * p], pvs[2 * p + 1]))
            return tuple(ms2), tuple(ls2), tuple(accs2)

        init = (tuple(jnp.full((bq, 1), NEG_INF, F32) for _ in range(H)),
                tuple(jnp.zeros((bq, 1), F32) for _ in range(H)),
                tuple(jnp.zeros((bq, LANES), F32) for _ in range(H // 2)))
        carry = lax.fori_loop(0, i * per, lambda j, c: block(j, c, None), init)
        row = lax.broadcasted_iota(jnp.int32, (bq, bk), 0)
        col = lax.broadcasted_iota(jnp.int32, (bq, bk), 1)
        for t in range(per):
            carry = block(i * per + t, carry, col + t * bk <= row)
        ms, ls, accs = carry
        o_ref[...] = jnp.concatenate(
            [accs[p] / jnp.where(lo, ls[2 * p], ls[2 * p + 1]) for p in range(H // 2)], axis=1)
        lse_ref[...] = _lane_col([ms[h] + jnp.log(ls[h]) for h in range(H)])

    return _pcall(body, name="mla_fwd", grid=(S // bq,),
                  in_specs=[pl.BlockSpec((bq, H * LANES), lambda i: (i, 0)),
                            pl.BlockSpec((S, H * LANES), lambda i: (0, 0)),
                            pl.BlockSpec((S, H * HEAD_DIM), lambda i: (0, 2))],
                  out_specs=(pl.BlockSpec((bq, H * HEAD_DIM), lambda i: (i, 0)),
                             pl.BlockSpec((bq, LANES), lambda i: (i, 0))),
                  out_shape=(jax.ShapeDtypeStruct((S, H * HEAD_DIM), F32), jax.ShapeDtypeStruct((S, LANES), F32)),
                  compiler_params=_params(("parallel",)))(q, k, kv)


def _mla_bwd(q, k, kv, o, lse, dc_all):
    S = q.shape[0]
    H = MLA_HEADS
    bq, bk = _row_block(S, MLA_BQ), _row_block(S, MLA_BK)
    per = bq // bk

    def body(q_ref, k_ref, v_ref, o_ref, lse_ref, doa_ref, dob_ref, dq_ref, dk_ref, dv_ref):
        i = pl.program_id(0)

        @pl.when(i == 0)
        def _():
            dk_ref[...] = jnp.zeros_like(dk_ref)
            dv_ref[...] = jnp.zeros_like(dv_ref)

        lo, hi = _half_masks()
        qb = q_ref[...]
        dob = jnp.concatenate([doa_ref[...], dob_ref[...]], axis=1)
        prod = dob * o_ref[...]
        lse_v = lse_ref[...]
        do16 = dob.astype(BF16)
        dom, deltas, lses = [], [], []
        for h in range(H):
            mk = lo if h % 2 == 0 else hi
            dom.append(_keep(mk, _pair(do16, h // 2)))
            deltas.append(jnp.sum(_keep(mk, _pair(prod, h // 2)), axis=-1, keepdims=True))
            lses.append(lse_v[:, h:h + 1])

        def block(j, dqs, ok):
            off = pl.multiple_of(j * bk, bk)
            kb = k_ref[pl.ds(off, bk), :]
            vb = v_ref[pl.ds(off, bk), :].astype(BF16)
            ss = [_dot(_pair(qb, h), _pair(kb, h), NT) * MLA_SCALE for h in range(H)]
            dps = [_dot(dom[h], _pair(vb, h // 2), NT) for h in range(H)]
            ps = [jnp.exp(ss[h] - lses[h]) for h in range(H)]
            if ok is not None:
                ps = [jnp.where(ok, p, 0.0) for p in ps]
            ds16 = [(ps[h] * (dps[h] - deltas[h]) * MLA_SCALE).astype(BF16) for h in range(H)]
            p16 = [p.astype(BF16) for p in ps]
            dks = [_dot(ds16[h], _pair(qb, h), TN) for h in range(H)]
            dvps = [_dot(p16[h], dom[h], TN) for h in range(H)]
            dqs2 = [dqs[h] + _dot(ds16[h], _pair(kb, h)) for h in range(H)]
            dk_ref[pl.ds(off, bk), :] += jnp.concatenate(dks, axis=1)
            dv_ref[pl.ds(off, bk), :] += jnp.concatenate(
                [dvps[2 * p] + dvps[2 * p + 1] for p in range(H // 2)], axis=1)
            return tuple(dqs2)

        dqs = lax.fori_loop(0, i * per, lambda j, c: block(j, c, None),
                            tuple(jnp.zeros((bq, LANES), F32) for _ in range(H)))
        row = lax.broadcasted_iota(jnp.int32, (bq, bk), 0)
        col = lax.broadcasted_iota(jnp.int32, (bq, bk), 1)
        for t in range(per):
            dqs = block(i * per + t, dqs, col + t * bk <= row)
        dq_ref[...] = jnp.concatenate(dqs, axis=1)

    wide = pl.BlockSpec((bq, H * LANES), lambda i: (i, 0))
    full = pl.BlockSpec((S, H * LANES), lambda i: (0, 0))
    return _pcall(body, name="mla_bwd", grid=(S // bq,),
                  in_specs=[wide, full, pl.BlockSpec((S, H * HEAD_DIM), lambda i: (0, 2)),
                            pl.BlockSpec((bq, H * HEAD_DIM), lambda i: (i, 0)),
                            pl.BlockSpec((bq, LANES), lambda i: (i, 0)),
                            pl.BlockSpec((bq, 256), lambda i: (i, 1)), pl.BlockSpec((bq, 256), lambda i: (i, 2))],
                  out_specs=(wide, full, pl.BlockSpec((S, H * HEAD_DIM), lambda i: (0, 0))),
                  out_shape=(jax.ShapeDtypeStruct((S, H * LANES), F32), jax.ShapeDtypeStruct((S, H * LANES), F32),
                             jax.ShapeDtypeStruct((S, H * HEAD_DIM), F32)),
                  compiler_params=_params(("arbitrary",)))(q, k, kv, o, lse, dc_all, dc_all)


def _dot_exact(x, tri):
    h1 = x.astype(BF16)
    h2 = (x - h1.astype(F32)).astype(BF16)
    return _dot(h1, tri) + _dot(h2, tri)


def _softplus(z):
    return jnp.maximum(z, 0.0) + jnp.log(1.0 + jnp.exp(-jnp.abs(z)))


def _sb_fwd(z):
    S = z.shape[0]
    H = SB_HEADS
    bq, bk = _row_block(S, SB_BQ), _row_block(S, SB_BK)
    per = bq // bk

    def body(q_ref, k_ref, v_ref, o_ref, t_ref):
        i = pl.program_id(0)
        lo, hi = _half_masks()
        q16 = (q_ref[...] * SB_SCALE).astype(BF16)
        qm = [_keep(lo if h % 2 == 0 else hi, _pair(q16, h // 2)) for h in range(H)]
        rr = lax.broadcasted_iota(jnp.int32, (bk, bk), 0)
        cc = lax.broadcasted_iota(jnp.int32, (bk, bk), 1)
        later = (rr > cc).astype(BF16)

        def block(j, carry, mask):
            tails, accs = carry
            off = pl.multiple_of(j * bk, bk)
            kb = k_ref[pl.ds(off, bk), :].astype(BF16)
            vb = v_ref[pl.ds(off, bk), :].astype(BF16)
            zs = [_dot(qm[h], _pair(kb, h // 2), NT) for h in range(H)]
            sps = [_softplus(z) for z in zs]
            lnots = [-sp if mask is None else jnp.where(mask, -sp, 0.0) for sp in sps]
            sums = [_dot_exact(lnot, later) for lnot in lnots]
            ws = []
            for h in range(H):
                w = jnp.exp((zs[h] - sps[h]) + (tails[h] + sums[h]))
                ws.append((w if mask is None else jnp.where(mask, w, 0.0)).astype(BF16))
            pvs = [_dot(ws[h], _pair(vb, h // 2)) for h in range(H)]
            accs2 = [accs[p] + jnp.where(lo, pvs[2 * p], pvs[2 * p + 1]) for p in range(H // 2)]
            tails2 = [tails[h] + jnp.sum(lnots[h], axis=-1, keepdims=True) for h in range(H)]
            return tuple(tails2), tuple(accs2)

        carry = (tuple(jnp.zeros((bq, 1), F32) for _ in range(H)),
                 tuple(jnp.zeros((bq, LANES), F32) for _ in range(H // 2)))
        row = lax.broadcasted_iota(jnp.int32, (bq, bk), 0)
        col = lax.broadcasted_iota(jnp.int32, (bq, bk), 1)
        for t in range(per):
            carry = block(i * per + per - 1 - t, carry, col + (per - 1 - t) * bk < row)
        tails, accs = lax.fori_loop(0, i * per, lambda t, c: block(i * per - 1 - t, c, None), carry)
        o_ref[...] = jnp.concatenate(accs, axis=1)
        t_ref[...] = _lane_col(tails)

    return _pcall(body, name="sb_fwd", grid=(S // bq,),
                  in_specs=[pl.BlockSpec((bq, 256), lambda i: (i, 0)), pl.BlockSpec((S, 256), lambda i: (0, 1)),
                            pl.BlockSpec((S, 256), lambda i: (0, 2))],
                  out_specs=(pl.BlockSpec((bq, 256), lambda i: (i, 0)), pl.BlockSpec((bq, LANES), lambda i: (i, 0))),
                  out_shape=(jax.ShapeDtypeStruct((S, 256), F32), jax.ShapeDtypeStruct((S, LANES), F32)),
                  compiler_params=_params(("parallel",)))(z, z, z)


def _sb_bwd(z, tot, dc_all):
    S = z.shape[0]
    H = SB_HEADS
    bq, bk = _row_block(S, SB_BQ), _row_block(S, SB_BK)
    per = bq // bk

    def body(q_ref, k_ref, v_ref, t_ref, do_ref, dq_ref, dk_ref, dv_ref):
        i = pl.program_id(0)

        @pl.when(i == 0)
        def _():
            dk_ref[...] = jnp.zeros_like(dk_ref)
            dv_ref[...] = jnp.zeros_like(dv_ref)

        lo, hi = _half_masks()
        q16 = (q_ref[...] * SB_SCALE).astype(BF16)
        do16 = do_ref[...].astype(BF16)
        tot_v = t_ref[...]
        masks = [lo if h % 2 == 0 else hi for h in range(H)]
        qm = [_keep(masks[h], _pair(q16, h // 2)) for h in range(H)]
        dom = [_keep(masks[h], _pair(do16, h // 2)) for h in range(H)]
        tots = [tot_v[:, h:h + 1] for h in range(H)]
        rr = lax.broadcasted_iota(jnp.int32, (bk, bk), 0)
        cc = lax.broadcasted_iota(jnp.int32, (bk, bk), 1)
        upto = (rr <= cc).astype(BF16)
        before = (rr < cc).astype(BF16)

        def block(j, carry, mask):
            heads, eheads, dqs = carry
            off = pl.multiple_of(j * bk, bk)
            kb = k_ref[pl.ds(off, bk), :].astype(BF16)
            vb = v_ref[pl.ds(off, bk), :].astype(BF16)
            zs = [_dot(qm[h], _pair(kb, h // 2), NT) for h in range(H)]
            dws = [_dot(dom[h], _pair(vb, h // 2), NT) for h in range(H)]
            sps = [_softplus(z) for z in zs]
            lnots = [-sp if mask is None else jnp.where(mask, -sp, 0.0) for sp in sps]
            pres = [_dot_exact(lnot, upto) for lnot in lnots]
            lsigs = [zs[h] - sps[h] for h in range(H)]
            ws = [jnp.exp(lsigs[h] + (tots[h] - (heads[h] + pres[h]))) for h in range(H)]
            if mask is not None:
                ws = [jnp.where(mask, w, 0.0) for w in ws]
            es = [ws[h] * dws[h] for h in range(H)]
            esums = [eheads[h] + _dot_exact(es[h], before) for h in range(H)]
            dz16 = []
            for h in range(H):
                sig = jnp.exp(lsigs[h])
                dz = es[h] * (1.0 - sig) - sig * esums[h]
                dz16.append((dz if mask is None else jnp.where(mask, dz, 0.0)).astype(BF16))
            w16 = [w.astype(BF16) for w in ws]
            dkps = [_dot(dz16[h], qm[h], TN) for h in range(H)]
            dvps = [_dot(w16[h], dom[h], TN) for h in range(H)]
            dqps = [_dot(dz16[h], _pair(kb, h // 2)) for h in range(H)]
            dk_ref[pl.ds(off, bk), :] += jnp.concatenate([dkps[2 * p] + dkps[2 * p + 1] for p in range(H // 2)], axis=1)
            dv_ref[pl.ds(off, bk), :] += jnp.concatenate([dvps[2 * p] + dvps[2 * p + 1] for p in range(H // 2)], axis=1)
            dqs2 = [dqs[p] + jnp.where(lo, dqps[2 * p], dqps[2 * p + 1]) for p in range(H // 2)]
            heads2 = [heads[h] + jnp.sum(lnots[h], axis=-1, keepdims=True) for h in range(H)]
            eheads2 = [eheads[h] + jnp.sum(es[h], axis=-1, keepdims=True) for h in range(H)]
            return tuple(heads2), tuple(eheads2), tuple(dqs2)

        zeros = tuple(jnp.zeros((bq, 1), F32) for _ in range(H))
        init = (zeros, zeros, tuple(jnp.zeros((bq, LANES), F32) for _ in range(H // 2)))
        carry = lax.fori_loop(0, i * per, lambda j, c: block(j, c, None), init)
        row = lax.broadcasted_iota(jnp.int32, (bq, bk), 0)
        col = lax.broadcasted_iota(jnp.int32, (bq, bk), 1)
        for t in range(per):
            carry = block(i * per + t, carry, col + t * bk < row)
        dq_ref[...] = jnp.concatenate(carry[2], axis=1) * SB_SCALE

    blk = pl.BlockSpec((bq, 256), lambda i: (i, 0))
    full = pl.BlockSpec((S, 256), lambda i: (0, 0))
    return _pcall(body, name="sb_bwd", grid=(S // bq,),
                  in_specs=[blk, pl.BlockSpec((S, 256), lambda i: (0, 1)), pl.BlockSpec((S, 256), lambda i: (0, 2)),
                            pl.BlockSpec((bq, LANES), lambda i: (i, 0)), blk],
                  out_specs=(blk, full, full), out_shape=(jax.ShapeDtypeStruct((S, 256), F32),) * 3,
                  compiler_params=_params(("arbitrary",)))(z, z, z, tot, dc_all)


_HBM = pl.BlockSpec(memory_space=pltpu.HBM)


def _place():
    x, y, c = lax.axis_index("x"), lax.axis_index("y"), lax.axis_index("c")
    other_chips = ((1 - x, y), (x, 1 - y), (1 - x, 1 - y))
    return x, y, c, other_chips


def _remote(src, dst, send_sems, recv_sems, k, dev):
    return pltpu.make_async_remote_copy(src_ref=src, dst_ref=dst, send_sem=send_sems.at[k], recv_sem=recv_sems.at[k],
                                        device_id=dev, device_id_type=MESH)


def _gather_weights(shards):
    n = len(shards)

    def body(*refs):
        src, dst = refs[:n], refs[n:2 * n]
        send_sems, recv_sems = refs[2 * n:]
        x, y, c, chips = _place()
        me, sib = 2 * x + y, (x, y, 1 - c)
        first = [_remote(src[w].at[c], dst[w].at[c, me], send_sems, recv_sems, 6 * w + k, (cx, cy, c))
                 for k, (cx, cy) in enumerate(chips) for w in range(n)]
        for cp in first:
            cp.start()
        passed = []
        for k, (cx, cy) in enumerate(chips):
            for w in range(n):
                blk = dst[w].at[c, 2 * cx + cy]
                _remote(blk, blk, send_sems, recv_sems, 6 * w + k, (cx, cy, c)).wait_recv()
                fwd = _remote(blk, blk, send_sems, recv_sems, 6 * w + 3 + k, sib)
                fwd.start()
                passed.append(fwd)
        for k, (cx, cy) in enumerate(chips):
            for w in range(n):
                blk = dst[w].at[1 - c, 2 * cx + cy]
                _remote(blk, blk, send_sems, recv_sems, 6 * w + 3 + k, sib).wait_recv()
        for cp in first + passed:
            cp.wait_send()

    return _pcall(body, name="gather_weights", in_specs=[_HBM] * n, out_specs=(_HBM,) * n,
                  out_shape=tuple(jax.ShapeDtypeStruct((2, N_CHIPS) + s.shape[1:], s.dtype) for s in shards),
                  scratch_shapes=[pltpu.SemaphoreType.DMA((6 * n,)), pltpu.SemaphoreType.DMA((6 * n,))])(*shards)


def _swap_layers(bufs):
    n = len(bufs)

    def body(*refs):
        src, dst = refs[:n], refs[n:2 * n]
        send_sems, recv_sems = refs[2 * n:]
        x, y, c, _ = _place()
        cps = [_remote(src[w].at[1 - c], dst[w], send_sems, recv_sems, w, (x, y, 1 - c)) for w in range(n)]
        for cp in cps:
            cp.start()
        for cp in cps:
            cp.wait()

    return _pcall(body, name="swap_layers", in_specs=[_HBM] * n, out_specs=(_HBM,) * n,
                  out_shape=tuple(jax.ShapeDtypeStruct(b.shape[1:], b.dtype) for b in bufs),
                  scratch_shapes=[pltpu.SemaphoreType.DMA((n,)), pltpu.SemaphoreType.DMA((n,))])(*bufs)


def _scatter_chip_sums(sums):
    n = len(sums)

    def body(*refs):
        src, dst = refs[:n], refs[n:2 * n]
        send_sems, recv_sems = refs[2 * n:]
        x, y, c, chips = _place()
        cps = [_remote(src[w].at[2 * cx + cy], dst[w].at[k], send_sems, recv_sems, 3 * w + k, (cx, cy, c))
               for k, (cx, cy) in enumerate(chips) for w in range(n)]
        for cp in cps:
            cp.start()
        for cp in cps:
            cp.wait()

    return _pcall(body, name="scatter_chip_sums", in_specs=[_HBM] * n, out_specs=(_HBM,) * n,
                  out_shape=tuple(jax.ShapeDtypeStruct((3,) + s.shape[1:], s.dtype) for s in sums),
                  scratch_shapes=[pltpu.SemaphoreType.DMA((3 * n,)), pltpu.SemaphoreType.DMA((3 * n,))])(*sums)


def _join_layers(fins):
    n = len(fins)

    def body(*refs):
        src, dst = refs[:n], refs[n:2 * n]
        send_sems, recv_sems = refs[2 * n:]
        x, y, c, _ = _place()
        cps = [_remote(src[w].at[c], dst[w].at[c], send_sems, recv_sems, w, (x, y, 1 - c)) for w in range(n)]
        for cp in cps:
            cp.start()
        for w in range(n):
            blk = dst[w].at[1 - c]
            _remote(blk, blk, send_sems, recv_sems, w, (x, y, 1 - c)).wait_recv()
        for cp in cps:
            cp.wait_send()

    return _pcall(body, name="join_layers", in_specs=[_HBM] * n, out_specs=(_HBM,) * n,
                  out_shape=tuple(jax.ShapeDtypeStruct(f.shape, f.dtype) for f in fins),
                  input_output_aliases={w: w for w in range(n)},
                  scratch_shapes=[pltpu.SemaphoreType.DMA((n,)), pltpu.SemaphoreType.DMA((n,))])(*fins)


def _add_layers(buf, got, c_idx):
    _, _, K, N = buf.shape
    bk = _row_block(K, 256)

    def body(c_ref, a_ref, b_ref, o_ref):
        o_ref[...] = (a_ref[...].astype(F32) + b_ref[...].astype(F32)).astype(o_ref.dtype)

    blk = pl.BlockSpec((None, bk, N), lambda j, i, c_ref: (j, i, 0))
    grid_spec = pltpu.PrefetchScalarGridSpec(
        num_scalar_prefetch=1, grid=(N_CHIPS, K // bk),
        in_specs=[pl.BlockSpec((None, None, bk, N), lambda j, i, c_ref: (c_ref[0], j, i, 0)), blk], out_specs=blk)
    return _pcall(body, name="add_layers", grid_spec=grid_spec, out_shape=jax.ShapeDtypeStruct((N_CHIPS, K, N), BF16),
                  compiler_params=_params(("parallel", "parallel")))(c_idx, buf, got)


def _add_chip_sums(s1, got, idx):
    _, K, N = s1.shape
    bk = _row_block(K, 256)

    def body(idx_ref, a_ref, b_ref, o_ref):
        acc = a_ref[...].astype(F32)
        for k in range(3):
            acc = acc + b_ref[k].astype(F32)
        o_ref[...] = acc

    grid_spec = pltpu.PrefetchScalarGridSpec(
        num_scalar_prefetch=1, grid=(K // bk,),
        in_specs=[pl.BlockSpec((None, bk, N), lambda i, idx_ref: (idx_ref[0], i, 0)),
                  pl.BlockSpec((3, bk, N), lambda i, idx_ref: (0, i, 0))],
        out_specs=pl.BlockSpec((None, bk, N), lambda i, idx_ref: (idx_ref[1], i, 0)))
    return _pcall(body, name="add_chip_sums", grid_spec=grid_spec, out_shape=jax.ShapeDtypeStruct((2, K, N), F32),
                  compiler_params=_params(("parallel",)))(idx, s1, got)


def _allreduce_small(v):
    R, W = v.shape

    def body(v_ref, o_ref, buf, send_sems, recv_sems):
        x, y, c, _ = _place()
        me = 4 * x + 2 * y + c
        buf[0] = v_ref[...]
        cps = []
        for r in range(1, 8):
            peer = (x if not r & 4 else 1 - x, y if not r & 2 else 1 - y, c if not r & 1 else 1 - c)
            cp = _remote(v_ref, buf.at[r], send_sems, recv_sems, r - 1, peer)
            cp.start()
            cps.append(cp)
        for cp in cps:
            cp.wait()
        acc = buf[me]
        for d in range(1, 8):
            acc = acc + buf[jnp.bitwise_xor(me, d)]
        o_ref[...] = acc

    return _pcall(body, name="allreduce_small", out_shape=jax.ShapeDtypeStruct((R, W), F32),
                  in_specs=[pl.BlockSpec(memory_space=pltpu.VMEM)], out_specs=pl.BlockSpec(memory_space=pltpu.VMEM),
                  scratch_shapes=[pltpu.VMEM((8, R, W), F32), pltpu.SemaphoreType.DMA((7,)),
                                  pltpu.SemaphoreType.DMA((7,))])(v)


def _cat_cols(g):
    return g.transpose(1, 0, 2).reshape(g.shape[1], -1)


def _cut_cols(w):
    K, N = w.shape
    return w.reshape(K, N_CHIPS, N // N_CHIPS).transpose(1, 0, 2)


def _regroup_w_in(w):
    zeros = lambda n: jnp.zeros((w.shape[0], n), w.dtype)
    return jnp.concatenate([w[:, 0:768], w[:, 1152:1408], zeros(64), w[:, 1408:1440], zeros(32),
                            w[:, 768:1152], w[:, 1440:2208]], axis=1)


def _ungroup_w_in(w):
    return jnp.concatenate([w[:, 0:768], w[:, Z_Q:Z_Q + 384], w[:, Z_KV:Z_KV + 256],
                            w[:, Z_KR + 64:Z_KR + 96], w[:, Z_CV:Z_W]], axis=1)


def _regroup_w_uq(w):
    K = w.shape[0]
    return jnp.pad(w.reshape(K, MLA_HEADS, 96), ((0, 0), (0, 0), (0, 32))).reshape(K, MLA_HEADS * LANES)


def _ungroup_w_uq(w):
    K = w.shape[0]
    return w.reshape(K, MLA_HEADS, LANES)[:, :, :96].reshape(K, MLA_HEADS * 96)


def _regroup_w_ukv(w):
    K = w.shape[0]
    t = w.reshape(K, MLA_HEADS, 128)
    return jnp.concatenate([jnp.pad(t[:, :, :64], ((0, 0), (0, 0), (0, 64))).reshape(K, MLA_HEADS * LANES),
                            t[:, :, 64:].reshape(K, MLA_HEADS * HEAD_DIM)], axis=1)


def _ungroup_w_ukv(w):
    K = w.shape[0]
    return jnp.concatenate([w[:, :MLA_HEADS * LANES].reshape(K, MLA_HEADS, LANES)[:, :, :64],
                            w[:, MLA_HEADS * LANES:].reshape(K, MLA_HEADS, 64)], axis=2).reshape(K, 1024)


def _ffn_fwd(x, g_pre, g_post, wg, wu, wd, layer):
    h = _norm_fwd(x, g_pre, "norm_fwd")
    G, U, A = _ffn_gate_up(h, wg, wu, layer, "ffn_gate_up")
    Y = _mm([(A, (wd, layer))], "nn", F32, "ffn_down")
    return _resid_norm(x, Y, g_post, 0.5, "resid_norm"), dict(x=x, h=h, G=G, U=U, A=A, Y=Y)


def _ffn_bwd(dxo, sv, g_pre, g_post, wg, wu, wd, layer, bufs):
    dY, dg_post = _norm_bwd(sv["Y"], g_post, dxo, 0.5, None, BF16, "norm_bwd_post")
    dG, dU = _ffn_bwd_mid(dY, (wd, layer), sv["G"], sv["U"], "ffn_bwd_mid")
    bufs = (_dw_shards(sv["h"], dG, 1, layer, bufs[0], "ffn_dw_in"),
            _dw_shards(sv["h"], dU, 1, layer, bufs[1], "ffn_dw_in"),
            _dw_shards(sv["A"], dY, 0, layer, bufs[2], "ffn_dw_down"))
    dh = _ffn_dh(dG, dU, wg, wu, layer, "ffn_dh")
    dx, dg_pre = _norm_bwd(sv["x"], g_pre, dh, 1.0, dxo, F32, "norm_bwd_pre")
    return dx, dg_pre, dg_post, bufs


def _layer_fwd(x, p_l, layer, Gw, Wl, G, w_conv, cp, sp):
    sv = {}
    x1, sv["ffn1"] = _ffn_fwd(x, G["g_ffn1_pre"], G["g_ffn1_post"], Gw["w1_gate"], Gw["w1_up"], Gw["w1_down_rows"], layer)

    h2 = _norm_fwd(x1, G["g_mix_pre"], "norm_fwd")
    Z = _mm([(h2, Wl["w_in"])], "nn", F32, "mix_in")
    o_sb, tot_sb = _sb_fwd(Z)
    cq, ckv = Z[:, Z_Q:Z_Q + 384], Z[:, Z_KV:Z_KV + 256]
    nq = _norm_fwd(cq, G["g_mla_q"], "norm_fwd_q")
    Qf = _mm([(nq, Wl["w_mla_uq"])], "nn", F32, "mla_uq")
    nkv = _norm_fwd(ckv, G["g_mla_kv"], "norm_fwd_kv")
    KV = _mm([(nkv, Wl["w_mla_ukv"])], "nn", F32, "mla_ukv")
    qcat, kcat = _mla_prep(Qf, KV, Z, cp, sp)
    o_mla, lse = _mla_fwd(qcat, kcat, KV)
    y_cv = _conv_fwd(Z, w_conv)
    C = jnp.concatenate([o_sb, o_mla, y_cv], axis=1).astype(BF16)
    Mx = _mm([(C, (Gw["w_out_rows"], layer))], "nn", F32, "mix_out")
    x2 = _resid_norm(x1, Mx, G["g_mix_post"], 1.0, "resid_norm")
    sv["mix"] = dict(x=x1, h=h2, Z=Z, tot_sb=tot_sb, cq=cq, ckv=ckv, nq=nq, nkv=nkv, qcat=qcat, kcat=kcat, KV=KV,
                     o_mla=o_mla, lse=lse, C=C, Mx=Mx)

    x3, sv["ffn2"] = _ffn_fwd(x2, G["g_ffn2_pre"], G["g_ffn2_post"], Gw["w2_gate"], Gw["w2_up"], Gw["w2_down_rows"], layer)

    h4 = _norm_fwd(x3, G["g_ple_pre"], "norm_fwd")
    p16 = p_l.astype(BF16)
    Qg = _mm([(h4, (Gw["w_ple_gate_rows"], layer))], "nn", F32, "ple_gate")
    Pp = _ple_proj(p16, Gw["w_ple_proj"], layer)
    e = _ew(lambda q, pp: _sigmoid(q) * pp, [Qg, Pp], [(D_MODEL, F32)], "ple_mul")
    x4 = _resid_norm(x3, e, G["g_ple_post"], 1.0, "resid_norm")
    sv["ple"] = dict(x=x3, h=h4, p16=p16, Qg=Qg, Pp=Pp, e=e)
    return x4, sv


def _layer_bwd(dx4, sv, layer, Gw, Wl, G, w_conv, cp, sp, bufs):
    S = dx4.shape[0]
    gg, small = {}, {}

    s = sv["ple"]
    de, gg["g_ple_post"] = _norm_bwd(s["e"], G["g_ple_post"], dx4, 1.0, None, F32, "norm_bwd_e")

    def ple_bwd(dev, q, pp):
        sg = _sigmoid(q)
        return dev * pp * sg * (1.0 - sg), dev * sg

    dQg, dPp = _ew(ple_bwd, [de, s["Qg"], s["Pp"]], [(D_MODEL, BF16)] * 2, "ple_mul_bwd")
    bufs["w_ple_proj"] = _dw_shards(s["p16"], dPp, 1, layer, bufs["w_ple_proj"], "ple_dw_proj")
    bufs["w_ple_gate"] = _dw_shards(s["h"], dQg, 0, layer, bufs["w_ple_gate"], "ple_dw_gate")
    dh4 = _mm([(dQg, (Gw["w_ple_gate_rows"], layer))], "nt", F32, "ple_dh")
    dx3, gg["g_ple_pre"] = _norm_bwd(s["x"], G["g_ple_pre"], dh4, 1.0, dx4, F32, "norm_bwd_pre")

    dx2, gg["g_ffn2_pre"], gg["g_ffn2_post"], (bufs["w2_gate"], bufs["w2_up"], bufs["w2_down"]) = _ffn_bwd(
        dx3, sv["ffn2"], G["g_ffn2_pre"], G["g_ffn2_post"], Gw["w2_gate"], Gw["w2_up"], Gw["w2_down_rows"], layer,
        (bufs["w2_gate"], bufs["w2_up"], bufs["w2_down"]))

    s = sv["mix"]
    dM, gg["g_mix_post"] = _norm_bwd(s["Mx"], G["g_mix_post"], dx2, 1.0, None, BF16, "norm_bwd_post")
    dC = _mm([(dM, (Gw["w_out_rows"], layer))], "nt", F32, "mix_out_dx")
    bufs["w_out"] = _dw_shards(s["C"], dM, 0, layer, bufs["w_out"], "mix_out_dw")

    db, dc, dhh, gg["w_conv"] = _conv_bwd(s["Z"], w_conv, dC)

    dqc, dkc, dv = _mla_bwd(s["qcat"], s["kcat"], s["KV"], s["o_mla"], s["lse"], dC)
    dQf, dkr = _mla_prep_bwd(dqc, dkc, cp, sp)
    small["w_mla_uq"] = _ungroup_w_uq(_mm([(s["nq"], dQf)], "tn", BF16, "mla_uq_dw"))
    dnq = _mm([(dQf, Wl["w_mla_uq"])], "nt", F32, "mla_uq_dx")
    dcq, gg["g_mla_q"] = _norm_bwd(s["cq"], G["g_mla_q"], dnq, 1.0, None, F32, "norm_bwd_q")
    dkv = jnp.concatenate([dkc, dv], axis=1).astype(BF16)
    small["w_mla_ukv"] = _ungroup_w_ukv(_mm([(s["nkv"], dkv)], "tn", BF16, "mla_ukv_dw"))
    dnkv = _mm([(dkv, Wl["w_mla_ukv"])], "nt", F32, "mla_ukv_dx")
    dckv, gg["g_mla_kv"] = _norm_bwd(s["ckv"], G["g_mla_kv"], dnkv, 1.0, None, F32, "norm_bwd_kv")

    dsq, dsk, dsv = _sb_bwd(s["Z"], s["tot_sb"], dC)
    dZ = jnp.concatenate([dsq, dsk, dsv, dckv, dkr, dcq, db, dc, dhh], axis=1).astype(BF16)
    small["w_in"] = _ungroup_w_in(_mm([(s["h"], dZ)], "tn", BF16, "mix_in_dw"))
    dh2 = _mm([(dZ, Wl["w_in"])], "nt", F32, "mix_in_dx")
    dx1, gg["g_mix_pre"] = _norm_bwd(s["x"], G["g_mix_pre"], dh2, 1.0, dx2, F32, "norm_bwd_pre")

    dx0, gg["g_ffn1_pre"], gg["g_ffn1_post"], (bufs["w1_gate"], bufs["w1_up"], bufs["w1_down"]) = _ffn_bwd(
        dx1, sv["ffn1"], G["g_ffn1_pre"], G["g_ffn1_post"], Gw["w1_gate"], Gw["w1_up"], Gw["w1_down_rows"], layer,
        (bufs["w1_gate"], bufs["w1_up"], bufs["w1_down"]))
    return dx0, gg, small, bufs


def _pack_small(vecs):
    flat = jnp.concatenate([v.reshape(-1) for v in vecs])
    rows = -(-flat.shape[0] // (8 * LANES)) * 8
    return jnp.pad(flat, (0, rows * LANES - flat.shape[0])).reshape(rows, LANES)


def kernel(x, p, positions, g_ffn1_pre, w1_gate, w1_up, w1_down, g_ffn1_post, g_mix_pre, w_in, g_mla_q, w_mla_uq, g_mla_kv, w_mla_ukv, w_conv, w_out, g_mix_post, g_ffn2_pre, w2_gate, w2_up, w2_down, g_ffn2_post, g_ple_pre, w_ple_gate, w_ple_proj, g_ple_post, loss_target, m_g_ffn1_pre, m_w1_gate, m_w1_up, m_w1_down, m_g_ffn1_post, m_g_mix_pre, m_w_in, m_g_mla_q, m_w_mla_uq, m_g_mla_kv, m_w_mla_ukv, m_w_conv, m_w_out, m_g_mix_post, m_g_ffn2_pre, m_w2_gate, m_w2_up, m_w2_down, m_g_ffn2_post, m_g_ple_pre, m_w_ple_gate, m_w_ple_proj, m_g_ple_post, v_g_ffn1_pre, v_w1_gate, v_w1_up, v_w1_down, v_g_ffn1_post, v_g_mix_pre, v_w_in, v_g_mla_q, v_w_mla_uq, v_g_mla_kv, v_w_mla_ukv, v_w_conv, v_w_out, v_g_mix_post, v_g_ffn2_pre, v_w2_gate, v_w2_up, v_w2_down, v_g_ffn2_post, v_g_ple_pre, v_w_ple_gate, v_w_ple_proj, v_g_ple_post):
    w = dict(g_ffn1_pre=g_ffn1_pre, w1_gate=w1_gate, w1_up=w1_up, w1_down=w1_down, g_ffn1_post=g_ffn1_post,
             g_mix_pre=g_mix_pre, w_in=w_in, g_mla_q=g_mla_q, w_mla_uq=w_mla_uq, g_mla_kv=g_mla_kv,
             w_mla_ukv=w_mla_ukv, w_conv=w_conv, w_out=w_out, g_mix_post=g_mix_post, g_ffn2_pre=g_ffn2_pre,
             w2_gate=w2_gate, w2_up=w2_up, w2_down=w2_down, g_ffn2_post=g_ffn2_post, g_ple_pre=g_ple_pre,
             w_ple_gate=w_ple_gate, w_ple_proj=w_ple_proj, g_ple_post=g_ple_post)
    m = dict(g_ffn1_pre=m_g_ffn1_pre, w1_gate=m_w1_gate, w1_up=m_w1_up, w1_down=m_w1_down, g_ffn1_post=m_g_ffn1_post,
             g_mix_pre=m_g_mix_pre, w_in=m_w_in, g_mla_q=m_g_mla_q, w_mla_uq=m_w_mla_uq, g_mla_kv=m_g_mla_kv,
             w_mla_ukv=m_w_mla_ukv, w_conv=m_w_conv, w_out=m_w_out, g_mix_post=m_g_mix_post, g_ffn2_pre=m_g_ffn2_pre,
             w2_gate=m_w2_gate, w2_up=m_w2_up, w2_down=m_w2_down, g_ffn2_post=m_g_ffn2_post, g_ple_pre=m_g_ple_pre,
             w_ple_gate=m_w_ple_gate, w_ple_proj=m_w_ple_proj, g_ple_post=m_g_ple_post)
    v = dict(g_ffn1_pre=v_g_ffn1_pre, w1_gate=v_w1_gate, w1_up=v_w1_up, w1_down=v_w1_down, g_ffn1_post=v_g_ffn1_post,
             g_mix_pre=v_g_mix_pre, w_in=v_w_in, g_mla_q=v_g_mla_q, w_mla_uq=v_w_mla_uq, g_mla_kv=v_g_mla_kv,
             w_mla_ukv=v_w_mla_ukv, w_conv=v_w_conv, w_out=v_w_out, g_mix_post=v_g_mix_post, g_ffn2_pre=v_g_ffn2_pre,
             w2_gate=v_w2_gate, w2_up=v_w2_up, w2_down=v_w2_down, g_ffn2_post=v_g_ffn2_post, g_ple_pre=v_g_ple_pre,
             w_ple_gate=v_w_ple_gate, w_ple_proj=v_w_ple_proj, g_ple_post=v_g_ple_post)

    depth = g_ffn1_pre.shape[0]
    assert depth == 2, "sibling cores split every exchange by layer"
    S = x.shape[1]
    cx, cy, cc = lax.axis_index("x"), lax.axis_index("y"), lax.axis_index("c")
    chip = 2 * cx + cy

    def padded(name):
        (ks, ns), (kp, np_) = BIG[name]
        return jnp.pad(w[name].astype(BF16), ((0, 0), (0, kp - ks), (0, np_ - ns)))

    mine = [padded(name) for name in BIG_NAMES]
    got = _gather_weights(mine)
    Gw = {name: lax.dynamic_update_slice(g, s[:, None], (0, chip, 0, 0)) for name, g, s in zip(BIG_NAMES, got, mine)}
    for name in ("w1_down", "w2_down", "w_out", "w_ple_gate"):
        g = Gw[name]
        Gw[name + "_rows"] = g.reshape(2, N_CHIPS * g.shape[2], g.shape[3])

    conv_slot = lax.dynamic_update_slice(jnp.zeros((depth, 3, 256), F32),
                                         w_conv * (cc == 0).astype(F32), (0, 0, 64 * chip))
    w_conv_full = _allreduce_small(_pack_small([conv_slot]))[:depth * 3 * 256 // LANES].reshape(depth, 3, 256)

    inv = ROPE_BASE ** (-jnp.arange(ROPE_HALF, dtype=F32) / ROPE_HALF)
    zeros = lambda n: jnp.zeros((n,), F32)
    ones = jnp.ones((ROPE_HALF,), F32)
    inv_pat = jnp.concatenate([zeros(64), inv, inv, zeros(32)]).reshape(1, LANES)
    sign_pat = jnp.concatenate([zeros(64), -ones, ones, zeros(32)]).reshape(1, LANES)
    cp, sp = _rope_tables(positions.reshape(S, 1), inv_pat, sign_pat)

    layers = []
    for i in range(depth):
        Wl = {"w_in": _regroup_w_in(_cat_cols(Gw["w_in"][i])),
              "w_mla_uq": _regroup_w_uq(_cat_cols(Gw["w_mla_uq"][i])),
              "w_mla_ukv": _regroup_w_ukv(_cat_cols(Gw["w_mla_ukv"][i]))}
        Gl = {name: w[name][i].reshape(1, n) for name, n in GAINS}
        layers.append((Wl, Gl))

    xs, saved = x[0], []
    for i in range(depth):
        xs, sv = _layer_fwd(xs, p[i, 0], i, Gw, layers[i][0], layers[i][1], w_conv_full[i], cp, sp)
        saved.append(sv)

    loss_part, dx = _loss_and_grad(xs, loss_target[0])
    loss = lax.psum(loss_part[0, 0], AXES)

    bufs = {name: None for name in BIG_NAMES}
    ggs, smalls = [None] * depth, [None] * depth
    for i in reversed(range(depth)):
        dx, ggs[i], smalls[i], bufs = _layer_bwd(dx, saved[i], i, Gw, layers[i][0], layers[i][1], w_conv_full[i],
                                                 cp, sp, bufs)
    for name in ("w_in", "w_mla_uq", "w_mla_ukv"):
        bufs[name] = jnp.stack([_cut_cols(smalls[i][name]) for i in range(depth)])

    c_idx = cc.reshape(1).astype(jnp.int32)
    idx2 = jnp.stack([chip, cc]).astype(jnp.int32)
    full = [bufs[name] for name in BIG_NAMES]
    sib = _swap_layers(full)
    s1 = [_add_layers(b, g, c_idx) for b, g in zip(full, sib)]
    arrived = _scatter_chip_sums(s1)
    fins = _join_layers([_add_chip_sums(a, g, idx2) for a, g in zip(s1, arrived)])
    grads = {name: f[:, :BIG[name][0][0], :BIG[name][0][1]] for name, f in zip(BIG_NAMES, fins)}

    small_names = [name for name, _ in GAINS] + ["w_conv"]
    small = _allreduce_small(_pack_small([jnp.stack([ggs[i][name] for i in range(depth)]) for name in small_names]))
    flat, off = small.reshape(-1), 0
    for name, n in GAINS:
        grads[name] = flat[off:off + depth * n].reshape(depth, n)
        off += depth * n
    conv_full = flat[off:off + depth * 3 * 256].reshape(depth, 3, 256)
    grads["w_conv"] = lax.dynamic_slice(conv_full, (0, 0, 64 * chip), (depth, 3, 64))

    deltas, new_m, new_v = {}, {}, {}
    for name in WEIGHT_ORDER:
        shape = w[name].shape
        two_d = (shape[0] * shape[1], shape[2]) if len(shape) == 3 else shape
        d_, m_, v_ = _adamw(w[name].reshape(two_d), grads[name].reshape(two_d), m[name].reshape(two_d),
                            v[name].reshape(two_d), "adamw")
        deltas[name], new_m[name], new_v[name] = d_.reshape(shape), m_.reshape(shape), v_.reshape(shape)

    return (loss, dx[None], *[grads[n] for n in WEIGHT_ORDER], *[deltas[n] for n in WEIGHT_ORDER],
            *[new_m[n] for n in WEIGHT_ORDER], *[new_v[n] for n in WEIGHT_ORDER])
```

```python
import functools

import jax
import jax.numpy as jnp
from jax import lax
from jax.experimental import pallas as pl
from jax.experimental.pallas import tpu as pltpu

F32 = jnp.float32
BF16 = jnp.bfloat16
MESH = pl.DeviceIdType.MESH
AXES = ("x", "y", "c")

D_MODEL = 1024
N_CHIPS = 4
FF_SHARD = 704
FF_PAD = 768
D_FF_PAD = N_CHIPS * FF_PAD
EPS = 1e-6
NEG_INF = -1e30
ROPE_BASE = 10000.0
ROPE_HALF = 16
LANES = 128
SB_HEADS, MLA_HEADS, HEAD_DIM = 4, 8, 64
MLA_SCALE = 96.0 ** -0.5
SB_SCALE = 64.0 ** -0.5
SB_BQ, SB_BK = 256, 128
MLA_BQ, MLA_BK = 256, 256
ADAM_LR, ADAM_B1, ADAM_B2, ADAM_EPS, ADAM_WD, ADAM_STEP = 0.001, 0.9, 0.999, 1e-08, 0.01, 10
VMEM_LIMIT = 48 * 2 ** 20

Z_SB, Z_KV, Z_KR, Z_Q, Z_CV, Z_W = 0, 768, 1024, 1152, 1536, 2304

NT = (((1,), (1,)), ((), ()))
TN = (((0,), (0,)), ((), ()))

BIG = {"w1_gate": ((1024, FF_SHARD), (1024, FF_PAD)), "w1_up": ((1024, FF_SHARD), (1024, FF_PAD)),
       "w1_down": ((FF_SHARD, 1024), (FF_PAD, 1024)), "w_in": ((1024, 552), (1024, 552)),
       "w_mla_uq": ((384, 192), (384, 192)), "w_mla_ukv": ((256, 256), (256, 256)),
       "w_out": ((256, 1024), (256, 1024)),
       "w2_gate": ((1024, FF_SHARD), (1024, FF_PAD)), "w2_up": ((1024, FF_SHARD), (1024, FF_PAD)),
       "w2_down": ((FF_SHARD, 1024), (FF_PAD, 1024)), "w_ple_gate": ((256, 1024), (256, 1024)),
       "w_ple_proj": ((256, 256), (256, 256))}
BIG_NAMES = tuple(BIG)
GAINS = (("g_ffn1_pre", 1024), ("g_ffn1_post", 1024), ("g_mix_pre", 1024), ("g_mla_q", 384),
         ("g_mla_kv", 256), ("g_mix_post", 1024), ("g_ffn2_pre", 1024), ("g_ffn2_post", 1024),
         ("g_ple_pre", 1024), ("g_ple_post", 1024))
WEIGHT_ORDER = ("g_ffn1_pre", "w1_gate", "w1_up", "w1_down", "g_ffn1_post", "g_mix_pre", "w_in", "g_mla_q",
                "w_mla_uq", "g_mla_kv", "w_mla_ukv", "w_conv", "w_out", "g_mix_post", "g_ffn2_pre", "w2_gate",
                "w2_up", "w2_down", "g_ffn2_post", "g_ple_pre", "w_ple_gate", "w_ple_proj", "g_ple_post")

_pcall = pl.pallas_call


def _params(sem=None):
    return pltpu.CompilerParams(dimension_semantics=sem, vmem_limit_bytes=VMEM_LIMIT)


def _dot(a, b, dims=None):
    a, b = a.astype(BF16), b.astype(BF16)
    if dims is None:
        return jnp.dot(a, b, preferred_element_type=F32)
    return lax.dot_general(a, b, dims, preferred_element_type=F32)


def _rstd(v):
    return lax.rsqrt(jnp.mean(v * v, axis=-1, keepdims=True) + EPS)


def _sigmoid(v):
    return 0.5 * jnp.tanh(0.5 * v) + 0.5


def _row_block(n, want):
    b = min(n, want)
    while n % b:
        b //= 2
    return b


def _spec(block, index, lead=None):
    if not lead:
        return pl.BlockSpec(block, index)
    lead = tuple(lead)
    return pl.BlockSpec((None,) * len(lead) + tuple(block), lambda *g: lead + tuple(index(*g)))


def _arr(op):
    return op[0] if isinstance(op, tuple) else op


def _lead(op):
    return tuple(op[1:]) if isinstance(op, tuple) else ()


def _mm(pairs, mode, out_dtype, name, bm=512, bn=512):
    a0, b0 = _arr(pairs[0][0]), _arr(pairs[0][1])
    if mode == "nn":
        M, N = a0.shape[-2], b0.shape[-1]
    elif mode == "nt":
        M, N = a0.shape[-2], b0.shape[-2]
    else:
        M, N = a0.shape[-1], b0.shape[-1]
    bm, bn = _row_block(M, bm), _row_block(N, bn)
    n_pairs = len(pairs)
    dims = {"nn": None, "nt": NT, "tn": TN}[mode]

    def body(*refs):
        acc = None
        for t in range(n_pairs):
            part = _dot(refs[2 * t][...], refs[2 * t + 1][...], dims)
            acc = part if acc is None else acc + part
        refs[-1][...] = acc.astype(refs[-1].dtype)

    in_specs, ops = [], []
    for a, b in pairs:
        sa, sb = _arr(a).shape, _arr(b).shape
        if mode == "nn":
            in_specs += [_spec((bm, sa[-1]), lambda j, i: (i, 0), _lead(a)),
                         _spec((sb[-2], bn), lambda j, i: (0, j), _lead(b))]
        elif mode == "nt":
            in_specs += [_spec((bm, sa[-1]), lambda j, i: (i, 0), _lead(a)),
                         _spec((bn, sb[-1]), lambda j, i: (j, 0), _lead(b))]
        else:
            in_specs += [_spec((sa[-2], bm), lambda j, i: (0, i), _lead(a)),
                         _spec((sb[-2], bn), lambda j, i: (0, j), _lead(b))]
        ops += [_arr(a), _arr(b)]
    return _pcall(body, name=name, grid=(N // bn, M // bm), in_specs=in_specs,
                  out_specs=pl.BlockSpec((bm, bn), lambda j, i: (i, j)),
                  out_shape=jax.ShapeDtypeStruct((M, N), out_dtype),
                  compiler_params=_params(("parallel", "parallel")))(*ops)


def _dw_shards(a, b, axis, name):
    S = a.shape[0]
    if axis == 1:
        Kp, Np = a.shape[1], b.shape[1] // N_CHIPS
        bm = _row_block(Kp, 512)
        grid = (N_CHIPS, Kp // bm)
        a_spec = pl.BlockSpec((S, bm), lambda j, i: (0, i))
        b_spec = pl.BlockSpec((S, Np), lambda j, i: (0, j))
        o_spec = pl.BlockSpec((None, bm, Np), lambda j, i: (j, i, 0))
    else:
        Kp, Np = a.shape[1] // N_CHIPS, b.shape[1]
        bn = _row_block(Np, 512)
        grid = (N_CHIPS, Np // bn)
        a_spec = pl.BlockSpec((S, Kp), lambda j, i: (0, j))
        b_spec = pl.BlockSpec((S, bn), lambda j, i: (0, i))
        o_spec = pl.BlockSpec((None, Kp, bn), lambda j, i: (j, 0, i))

    def body(a_ref, b_ref, o_ref):
        o_ref[...] = _dot(a_ref[...], b_ref[...], TN).astype(BF16)

    return _pcall(body, name=name, grid=grid, in_specs=[a_spec, b_spec], out_specs=o_spec,
                  out_shape=jax.ShapeDtypeStruct((N_CHIPS, Kp, Np), BF16),
                  compiler_params=_params(("parallel", "parallel")))(a, b)


def _ew(fn, ins, outs, name, br=256):
    S = max(a.shape[0] for a in ins)
    br = _row_block(S, br)
    n_in = len(ins)

    def body(*refs):
        res = fn(*[r[...] for r in refs[:n_in]])
        if not isinstance(res, tuple):
            res = (res,)
        for r, v in zip(refs[n_in:], res):
            r[...] = v.astype(r.dtype)

    in_specs = [pl.BlockSpec((br, a.shape[1]), lambda i: (i, 0)) if a.shape[0] == S and S > 1
                else pl.BlockSpec(a.shape, lambda i: (0, 0)) for a in ins]
    out = _pcall(body, name=name, grid=(S // br,), in_specs=in_specs,
                 out_specs=tuple(pl.BlockSpec((br, w), lambda i: (i, 0)) for w, _ in outs),
                 out_shape=tuple(jax.ShapeDtypeStruct((S, w), dt) for w, dt in outs),
                 compiler_params=_params(("parallel",)))(*ins)
    return out if len(outs) > 1 else out[0]


def _norm_fwd(x, g, name):
    return _ew(lambda xv, gv: xv * _rstd(xv) * gv, [x, g], [(x.shape[1], BF16)], name, br=512)


def _resid_norm(x, y, g, alpha, name):
    return _ew(lambda xv, yv, gv: xv + alpha * (yv * _rstd(yv) * gv), [x, y, g], [(x.shape[1], F32)], name)


def _norm_bwd(xin, g, dy, alpha, resid, out_dtype, name):
    S, W = xin.shape
    br = _row_block(S, 256)
    has_res = resid is not None

    def body(*refs):
        x_ref, g_ref, dy_ref = refs[:3]
        dx_ref, dg_ref = refs[-2:]
        xv, dyv = x_ref[...], dy_ref[...] * alpha
        r = _rstd(xv)
        xh = xv * r
        u = dyv * g_ref[...]
        dx = r * (u - xh * jnp.mean(u * xh, axis=-1, keepdims=True))
        if has_res:
            dx = dx + refs[3][...]
        dx_ref[...] = dx.astype(dx_ref.dtype)
        part = jnp.sum(dyv * xh, axis=0, keepdims=True)

        @pl.when(pl.program_id(0) == 0)
        def _():
            dg_ref[...] = part

        @pl.when(pl.program_id(0) > 0)
        def _():
            dg_ref[...] += part

    row = pl.BlockSpec((br, W), lambda i: (i, 0))
    vec = pl.BlockSpec((1, W), lambda i: (0, 0))
    ops = [xin, g, dy] + ([resid] if has_res else [])
    return _pcall(body, name=name, grid=(S // br,), in_specs=[row, vec, row] + ([row] if has_res else []),
                  out_specs=(row, vec),
                  out_shape=(jax.ShapeDtypeStruct((S, W), out_dtype), jax.ShapeDtypeStruct((1, W), F32)),
                  compiler_params=_params(("arbitrary",)))(*ops)


def _ffn_gate_up(h, wg, wu, name):
    S, K = h.shape
    bm, bn = _row_block(S, 1024), FF_PAD
    per = FF_PAD // bn

    def body(h_ref, wg_ref, wu_ref, g_ref, u_ref, a_ref):
        hv = h_ref[...]
        g = _dot(hv, wg_ref[...])
        u = _dot(hv, wu_ref[...])
        g_ref[...] = g.astype(BF16)
        u_ref[...] = u.astype(BF16)
        a_ref[...] = (g * _sigmoid(g) * u).astype(BF16)

    blk = pl.BlockSpec((bm, bn), lambda n, i: (i, n))
    wsp = pl.BlockSpec((None, K, bn), lambda n, i: (n // per, 0, n % per))
    return _pcall(body, name=name, grid=(D_FF_PAD // bn, S // bm),
                  in_specs=[pl.BlockSpec((bm, K), lambda n, i: (i, 0)), wsp, wsp],
                  out_specs=(blk, blk, blk), out_shape=(jax.ShapeDtypeStruct((S, D_FF_PAD), BF16),) * 3,
                  compiler_params=_params(("parallel", "parallel")))(h, wg, wu)


def _ffn_bwd_mid(dy, wd, g, u, name):
    S, D = dy.shape
    F = g.shape[1]
    bm, bn = _row_block(S, 1024), 256

    def body(dy_ref, wd_ref, g_ref, u_ref, dg_ref, du_ref):
        da = _dot(dy_ref[...], wd_ref[...], NT)
        gv, uv = g_ref[...].astype(F32), u_ref[...].astype(F32)
        s = _sigmoid(gv)
        dg_ref[...] = (da * uv * (s * (1.0 + gv * (1.0 - s)))).astype(BF16)
        du_ref[...] = (da * (gv * s)).astype(BF16)

    blk = pl.BlockSpec((bm, bn), lambda j, i: (i, j))
    return _pcall(body, name=name, grid=(F // bn, S // bm),
                  in_specs=[pl.BlockSpec((bm, D), lambda j, i: (i, 0)),
                            pl.BlockSpec((bn, D), lambda j, i: (j, 0)), blk, blk],
                  out_specs=(blk, blk), out_shape=(jax.ShapeDtypeStruct((S, F), BF16),) * 2,
                  compiler_params=_params(("parallel", "parallel")))(dy, wd, g, u)


def _ffn_dh(dg, du, wg, wu, name):
    S = dg.shape[0]
    D = wg.shape[1]
    bm, bn = _row_block(S, 1024), D

    def body(dg_ref, du_ref, wg_ref, wu_ref, o_ref, acc):
        j = pl.program_id(2)
        part = _dot(dg_ref[...], wg_ref[...], NT) + _dot(du_ref[...], wu_ref[...], NT)

        @pl.when(j == 0)
        def _():
            acc[...] = part

        @pl.when(j > 0)
        def _():
            acc[...] += part

        @pl.when(j == N_CHIPS - 1)
        def _():
            o_ref[...] = acc[...]

    asp = pl.BlockSpec((bm, FF_PAD), lambda n, i, j: (i, j))
    wsp = pl.BlockSpec((None, bn, FF_PAD), lambda n, i, j: (j, n, 0))
    return _pcall(body, name=name, grid=(D // bn, S // bm, N_CHIPS), in_specs=[asp, asp, wsp, wsp],
                  out_specs=pl.BlockSpec((bm, bn), lambda n, i, j: (i, n)),
                  out_shape=jax.ShapeDtypeStruct((S, D), F32), scratch_shapes=[pltpu.VMEM((bm, bn), F32)],
                  compiler_params=_params(("parallel", "parallel", "arbitrary")))(dg, du, wg, wu)


def _ple_proj(p16, w):
    S, K = p16.shape
    bm = _row_block(S, 1024)

    def body(p_ref, w_ref, o_ref):
        o_ref[...] = _dot(p_ref[...], w_ref[...])

    return _pcall(body, name="ple_proj", grid=(N_CHIPS, S // bm),
                  in_specs=[pl.BlockSpec((bm, K), lambda j, i: (i, 0)),
                            pl.BlockSpec((None, K, 256), lambda j, i: (j, 0, 0))],
                  out_specs=pl.BlockSpec((bm, 256), lambda j, i: (i, j)),
                  out_shape=jax.ShapeDtypeStruct((S, N_CHIPS * 256), F32),
                  compiler_params=_params(("parallel", "parallel")))(p16, w)


def _loss_and_grad(xf, tgt):
    S, W = xf.shape
    br = _row_block(S, 256)

    def body(x_ref, t_ref, l_ref, d_ref):
        d = x_ref[...] - t_ref[...]
        d_ref[...] = d * (1.0 / W)
        part = 0.5 * jnp.sum(jnp.sum(d * d, axis=-1, keepdims=True) * (1.0 / W), axis=0, keepdims=True)

        @pl.when(pl.program_id(0) == 0)
        def _():
            l_ref[...] = part

        @pl.when(pl.program_id(0) > 0)
        def _():
            l_ref[...] += part

    row = pl.BlockSpec((br, W), lambda i: (i, 0))
    return _pcall(body, name="loss", grid=(S // br,), in_specs=[row, row],
                  out_specs=(pl.BlockSpec((1, 1), lambda i: (0, 0)), row),
                  out_shape=(jax.ShapeDtypeStruct((1, 1), F32), jax.ShapeDtypeStruct((S, W), F32)),
                  compiler_params=_params(("arbitrary",)))(xf, tgt)


def _adamw(w, g, m, v, name):
    R, C = w.shape
    br = _row_block(R, 256) if R % 8 == 0 else R

    def body(w_ref, g_ref, m_ref, v_ref, d_ref, nm_ref, nv_ref):
        gv = g_ref[...]
        nm = ADAM_B1 * m_ref[...] + (1.0 - ADAM_B1) * gv
        nv = ADAM_B2 * v_ref[...] + (1.0 - ADAM_B2) * (gv * gv)
        m_hat = nm / (1.0 - ADAM_B1 ** ADAM_STEP)
        v_hat = nv / (1.0 - ADAM_B2 ** ADAM_STEP)
        d_ref[...] = -ADAM_LR * (m_hat / (jnp.sqrt(v_hat) + ADAM_EPS) + ADAM_WD * w_ref[...])
        nm_ref[...] = nm
        nv_ref[...] = nv

    blk = pl.BlockSpec((br, C), lambda i: (i, 0))
    return _pcall(body, name=name, grid=(R // br,), in_specs=[blk] * 4, out_specs=(blk,) * 3,
                  out_shape=(jax.ShapeDtypeStruct((R, C), F32),) * 3,
                  compiler_params=_params(("parallel",)))(w, g, m, v)


def _rope_tables(pos, inv_pat, sign_pat):
    def fn(p, iv, sg):
        ang = p.astype(F32) * iv
        return jnp.cos(ang), jnp.sin(ang) * sg

    return _ew(fn, [pos, inv_pat, sign_pat], [(LANES, F32)] * 2, "rope_tables")


def _swap_halves_of_rope(v):
    W = v.shape[1]
    lane = lax.broadcasted_iota(jnp.int32, (1, W), 1) % LANES
    return jnp.where((lane >= 64) & (lane < 80), pltpu.roll(v, W - ROPE_HALF, 1),
                     jnp.where((lane >= 80) & (lane < 96), pltpu.roll(v, ROPE_HALF, 1), 0.0))


def _tile_lanes(v, n):
    return jnp.concatenate([v] * n, axis=1)


def _mla_prep(qf, kv, z, cp, sp):
    S = qf.shape[0]
    br = _row_block(S, 256)
    W = MLA_HEADS * LANES

    def body(q_ref, k_ref, r_ref, c_ref, s_ref, qo_ref, ko_ref):
        c, s = c_ref[...], s_ref[...]
        q = q_ref[...]
        qo_ref[...] = (q * _tile_lanes(c, MLA_HEADS) + _swap_halves_of_rope(q) * _tile_lanes(s, MLA_HEADS)).astype(BF16)
        r = r_ref[...]
        lane = lax.broadcasted_iota(jnp.int32, (1, LANES), 1)
        kr = jnp.where((lane >= 64) & (lane < 96), r * c + _swap_halves_of_rope(r) * s, 0.0)
        ko_ref[...] = (k_ref[...] + _tile_lanes(kr, MLA_HEADS)).astype(BF16)

    wide = pl.BlockSpec((br, W), lambda i: (i, 0))
    one = pl.BlockSpec((br, LANES), lambda i: (i, 0))
    return _pcall(body, name="mla_prep", grid=(S // br,),
                  in_specs=[wide, wide, pl.BlockSpec((br, LANES), lambda i: (i, Z_KR // LANES)), one, one],
                  out_specs=(wide, wide), out_shape=(jax.ShapeDtypeStruct((S, W), BF16),) * 2,
                  compiler_params=_params(("parallel",)))(qf, kv, z, cp, sp)


def _mla_prep_bwd(dq, dk, cp, sp):
    S, W = dq.shape
    br = _row_block(S, 256)

    def body(dq_ref, dk_ref, c_ref, s_ref, dqo_ref, dr_ref):
        c, s = c_ref[...], s_ref[...]
        d = dq_ref[...]
        dqo_ref[...] = (d * _tile_lanes(c, MLA_HEADS) + _swap_halves_of_rope(d * _tile_lanes(s, MLA_HEADS))).astype(BF16)
        dkv = dk_ref[...]
        tot = dkv[:, 0:LANES]
        for h in range(1, MLA_HEADS):
            tot = tot + dkv[:, h * LANES:(h + 1) * LANES]
        lane = lax.broadcasted_iota(jnp.int32, (1, LANES), 1)
        tot = jnp.where((lane >= 64) & (lane < 96), tot, 0.0)
        dr_ref[...] = tot * c + _swap_halves_of_rope(tot * s)

    wide = pl.BlockSpec((br, W), lambda i: (i, 0))
    one = pl.BlockSpec((br, LANES), lambda i: (i, 0))
    return _pcall(body, name="mla_prep_bwd", grid=(S // br,), in_specs=[wide, wide, one, one], out_specs=(wide, one),
                  out_shape=(jax.ShapeDtypeStruct((S, W), BF16), jax.ShapeDtypeStruct((S, LANES), F32)),
                  compiler_params=_params(("parallel",)))(dq, dk, cp, sp)


def _shift_down(v, k, row):
    return jnp.where(row >= k, pltpu.roll(v, k, 0), 0.0)


def _shift_up(v, k, row):
    n = v.shape[0]
    return jnp.where(row < n - k, pltpu.roll(v, n - k, 0), 0.0)


def _conv_fwd(z, w):
    S = z.shape[0]
    c0 = Z_CV // LANES

    def body(b_ref, c_ref, h_ref, w_ref, y_ref):
        u = c_ref[...] * h_ref[...]
        row = lax.broadcasted_iota(jnp.int32, u.shape, 0)
        wv = w_ref[...]
        conv = wv[0:1] * _shift_down(u, 2, row) + wv[1:2] * _shift_down(u, 1, row) + wv[2:3] * u
        y_ref[...] = b_ref[...] * conv

    def col(k):
        return pl.BlockSpec((S, LANES), lambda j: (0, c0 + 2 * k + j))

    return _pcall(body, name="conv_fwd", grid=(2,),
                  in_specs=[col(0), col(1), col(2), pl.BlockSpec((3, LANES), lambda j: (0, j))],
                  out_specs=pl.BlockSpec((S, LANES), lambda j: (0, j)),
                  out_shape=jax.ShapeDtypeStruct((S, 256), F32), compiler_params=_params(("parallel",)))(z, z, z, w)


def _conv_bwd(z, w, dc_all):
    S = z.shape[0]
    c0 = Z_CV // LANES

    def body(b_ref, c_ref, h_ref, w_ref, dy_ref, db_ref, dc_ref, dh_ref, dw_ref):
        cv, hv, bv, dyv = c_ref[...], h_ref[...], b_ref[...], dy_ref[...]
        u = cv * hv
        row = lax.broadcasted_iota(jnp.int32, u.shape, 0)
        wv = w_ref[...]
        u1, u2 = _shift_down(u, 1, row), _shift_down(u, 2, row)
        conv = wv[0:1] * u2 + wv[1:2] * u1 + wv[2:3] * u
        db_ref[...] = dyv * conv
        dconv = dyv * bv
        du = wv[2:3] * dconv + wv[1:2] * _shift_up(dconv, 1, row) + wv[0:1] * _shift_up(dconv, 2, row)
        dc_ref[...] = du * hv
        dh_ref[...] = du * cv
        dw_ref[0:1, :] = jnp.sum(dconv * u2, axis=0, keepdims=True)
        dw_ref[1:2, :] = jnp.sum(dconv * u1, axis=0, keepdims=True)
        dw_ref[2:3, :] = jnp.sum(dconv * u, axis=0, keepdims=True)

    def col(k):
        return pl.BlockSpec((S, LANES), lambda j: (0, c0 + 2 * k + j))

    wsp = pl.BlockSpec((3, LANES), lambda j: (0, j))
    osp = pl.BlockSpec((S, LANES), lambda j: (0, j))
    db, dc, dh, dw = _pcall(
        body, name="conv_bwd", grid=(2,),
        in_specs=[col(0), col(1), col(2), wsp, pl.BlockSpec((S, LANES), lambda j: (0, 6 + j))],
        out_specs=(osp, osp, osp, wsp),
        out_shape=(jax.ShapeDtypeStruct((S, 256), F32),) * 3 + (jax.ShapeDtypeStruct((3, 256), F32),),
        compiler_params=_params(("parallel",)))(z, z, z, w, dc_all)
    return db, dc, dh, dw


def _half_masks():
    lane = lax.broadcasted_iota(jnp.int32, (1, LANES), 1)
    return lane < HEAD_DIM, lane >= HEAD_DIM


def _pair(v, p):
    return v[:, p * LANES:(p + 1) * LANES]


def _keep(mask, v):
    return jnp.where(mask, v, jnp.zeros_like(v))


def _lane_col(vals):
    lane = lax.broadcasted_iota(jnp.int32, (1, LANES), 1)
    out = jnp.zeros((vals[0].shape[0], LANES), F32)
    for h, v in enumerate(vals):
        out = jnp.where(lane == h, v, out)
    return out


def _mla_fwd(q, k, kv):
    S = q.shape[0]
    H = MLA_HEADS
    bq, bk = _row_block(S, MLA_BQ), _row_block(S, MLA_BK)
    per = bq // bk

    def body(q_ref, k_ref, v_ref, o_ref, lse_ref):
        i = pl.program_id(0)
        lo, hi = _half_masks()
        qb = q_ref[...]

        def block(j, carry, ok):
            ms, ls, accs = carry
            off = pl.multiple_of(j * bk, bk)
            kb = k_ref[pl.ds(off, bk), :]
            vb = v_ref[pl.ds(off, bk), :].astype(BF16)
            ss = [_dot(_pair(qb, h), _pair(kb, h), NT) * MLA_SCALE for h in range(H)]
            if ok is not None:
                ss = [jnp.where(ok, s, NEG_INF) for s in ss]
            ms2 = [jnp.maximum(ms[h], jnp.max(ss[h], axis=-1, keepdims=True)) for h in range(H)]
            ps = [jnp.exp(ss[h] - ms2[h]) for h in range(H)]
            al = [jnp.exp(ms[h] - ms2[h]) for h in range(H)]
            ls2 = [al[h] * ls[h] + jnp.sum(ps[h], axis=-1, keepdims=True) for h in range(H)]
            pvs = [_dot(ps[h], _pair(vb, h // 2)) for h in range(H)]
            accs2 = []
            for p in range(H // 2):
                scale = jnp.where(lo, al[2 * p], al[2 * p + 1])
                accs2.append(scale * accs[p] + jnp.where(lo, pvs[2 * p], pvs[2 * p + 1]))
            return tuple(ms2), tuple(ls2), tuple(accs2)

        init = (tuple(jnp.full((bq, 1), NEG_INF, F32) for _ in range(H)),
                tuple(jnp.zeros((bq, 1), F32) for _ in range(H)),
                tuple(jnp.zeros((bq, LANES), F32) for _ in range(H // 2)))
        carry = lax.fori_loop(0, i * per, lambda j, c: block(j, c, None), init)
        row = lax.broadcasted_iota(jnp.int32, (bq, bk), 0)
        col = lax.broadcasted_iota(jnp.int32, (bq, bk), 1)
        for t in range(per):
            carry = block(i * per + t, carry, col + t * bk <= row)
        ms, ls, accs = carry
        o_ref[...] = jnp.concatenate(
            [accs[p] / jnp.where(lo, ls[2 * p], ls[2 * p + 1]) for p in range(H // 2)], axis=1)
        lse_ref[...] = _lane_col([ms[h] + jnp.log(ls[h]) for h in range(H)])

    return _pcall(body, name="mla_fwd", grid=(S // bq,),
                  in_specs=[pl.BlockSpec((bq, H * LANES), lambda i: (i, 0)),
                            pl.BlockSpec((S, H * LANES), lambda i: (0, 0)),
                            pl.BlockSpec((S, H * HEAD_DIM), lambda i: (0, 2))],
                  out_specs=(pl.BlockSpec((bq, H * HEAD_DIM), lambda i: (i, 0)),
                             pl.BlockSpec((bq, LANES), lambda i: (i, 0))),
                  out_shape=(jax.ShapeDtypeStruct((S, H * HEAD_DIM), F32), jax.ShapeDtypeStruct((S, LANES), F32)),
                  compiler_params=_params(("parallel",)))(q, k, kv)


def _mla_bwd(q, k, kv, o, lse, dc_all):
    S = q.shape[0]
    H = MLA_HEADS
    bq, bk = _row_block(S, MLA_BQ), _row_block(S, MLA_BK)
    per = bq // bk

    def body(q_ref, k_ref, v_ref, o_ref, lse_ref, doa_ref, dob_ref, dq_ref, dk_ref, dv_ref):
        i = pl.program_id(0)

        @pl.when(i == 0)
        def _():
            dk_ref[...] = jnp.zeros_like(dk_ref)
            dv_ref[...] = jnp.zeros_like(dv_ref)

        lo, hi = _half_masks()
        qb = q_ref[...]
        dob = jnp.concatenate([doa_ref[...], dob_ref[...]], axis=1)
        prod = dob * o_ref[...]
        lse_v = lse_ref[...]
        do16 = dob.astype(BF16)
        dom, deltas, lses = [], [], []
        for h in range(H):
            mk = lo if h % 2 == 0 else hi
            dom.append(_keep(mk, _pair(do16, h // 2)))
            deltas.append(jnp.sum(_keep(mk, _pair(prod, h // 2)), axis=-1, keepdims=True))
            lses.append(lse_v[:, h:h + 1])

        def block(j, dqs, ok):
            off = pl.multiple_of(j * bk, bk)
            kb = k_ref[pl.ds(off, bk), :]
            vb = v_ref[pl.ds(off, bk), :].astype(BF16)
            ss = [_dot(_pair(qb, h), _pair(kb, h), NT) * MLA_SCALE for h in range(H)]
            dps = [_dot(dom[h], _pair(vb, h // 2), NT) for h in range(H)]
            ps = [jnp.exp(ss[h] - lses[h]) for h in range(H)]
            if ok is not None:
                ps = [jnp.where(ok, p, 0.0) for p in ps]
            ds16 = [(ps[h] * (dps[h] - deltas[h]) * MLA_SCALE).astype(BF16) for h in range(H)]
            p16 = [p.astype(BF16) for p in ps]
            dks = [_dot(ds16[h], _pair(qb, h), TN) for h in range(H)]
            dvps = [_dot(p16[h], dom[h], TN) for h in range(H)]
            dqs2 = [dqs[h] + _dot(ds16[h], _pair(kb, h)) for h in range(H)]
            dk_ref[pl.ds(off, bk), :] += jnp.concatenate(dks, axis=1)
            dv_ref[pl.ds(off, bk), :] += jnp.concatenate(
                [dvps[2 * p] + dvps[2 * p + 1] for p in range(H // 2)], axis=1)
            return tuple(dqs2)

        dqs = lax.fori_loop(0, i * per, lambda j, c: block(j, c, None),
                            tuple(jnp.zeros((bq, LANES), F32) for _ in range(H)))
        row = lax.broadcasted_iota(jnp.int32, (bq, bk), 0)
        col = lax.broadcasted_iota(jnp.int32, (bq, bk), 1)
        for t in range(per):
            dqs = block(i * per + t, dqs, col + t * bk <= row)
        dq_ref[...] = jnp.concatenate(dqs, axis=1)

    wide = pl.BlockSpec((bq, H * LANES), lambda i: (i, 0))
    full = pl.BlockSpec((S, H * LANES), lambda i: (0, 0))
    return _pcall(body, name="mla_bwd", grid=(S // bq,),
                  in_specs=[wide, full, pl.BlockSpec((S, H * HEAD_DIM), lambda i: (0, 2)),
                            pl.BlockSpec((bq, H * HEAD_DIM), lambda i: (i, 0)),
                            pl.BlockSpec((bq, LANES), lambda i: (i, 0)),
                            pl.BlockSpec((bq, 256), lambda i: (i, 1)), pl.BlockSpec((bq, 256), lambda i: (i, 2))],
                  out_specs=(wide, full, pl.BlockSpec((S, H * HEAD_DIM), lambda i: (0, 0))),
                  out_shape=(jax.ShapeDtypeStruct((S, H * LANES), F32), jax.ShapeDtypeStruct((S, H * LANES), F32),
                             jax.ShapeDtypeStruct((S, H * HEAD_DIM), F32)),
                  compiler_params=_params(("arbitrary",)))(q, k, kv, o, lse, dc_all, dc_all)


def _dot_exact(x, tri):
    h1 = x.astype(BF16)
    h2 = (x - h1.astype(F32)).astype(BF16)
    return _dot(h1, tri) + _dot(h2, tri)


def _softplus(z):
    return jnp.maximum(z, 0.0) + jnp.log(1.0 + jnp.exp(-jnp.abs(z)))


def _sb_fwd(z):
    S = z.shape[0]
    H = SB_HEADS
    bq, bk = _row_block(S, SB_BQ), _row_block(S, SB_BK)
    per = bq // bk

    def body(q_ref, k_ref, v_ref, o_ref, t_ref):
        i = pl.program_id(0)
        lo, hi = _half_masks()
        q16 = (q_ref[...] * SB_SCALE).astype(BF16)
        qm = [_keep(lo if h % 2 == 0 else hi, _pair(q16, h // 2)) for h in range(H)]
        rr = lax.broadcasted_iota(jnp.int32, (bk, bk), 0)
        cc = lax.broadcasted_iota(jnp.int32, (bk, bk), 1)
        later = (rr > cc).astype(BF16)

        def block(j, carry, mask):
            tails, accs = carry
            off = pl.multiple_of(j * bk, bk)
            kb = k_ref[pl.ds(off, bk), :].astype(BF16)
            vb = v_ref[pl.ds(off, bk), :].astype(BF16)
            zs = [_dot(qm[h], _pair(kb, h // 2), NT) for h in range(H)]
            sps = [_softplus(z) for z in zs]
            lnots = [-sp if mask is None else jnp.where(mask, -sp, 0.0) for sp in sps]
            sums = [_dot_exact(lnot, later) for lnot in lnots]
            ws = []
            for h in range(H):
                w = jnp.exp((zs[h] - sps[h]) + (tails[h] + sums[h]))
                ws.append((w if mask is None else jnp.where(mask, w, 0.0)).astype(BF16))
            pvs = [_dot(ws[h], _pair(vb, h // 2)) for h in range(H)]
            accs2 = [accs[p] + jnp.where(lo, pvs[2 * p], pvs[2 * p + 1]) for p in range(H // 2)]
            tails2 = [tails[h] + jnp.sum(lnots[h], axis=-1, keepdims=True) for h in range(H)]
            return tuple(tails2), tuple(accs2)

        carry = (tuple(jnp.zeros((bq, 1), F32) for _ in range(H)),
                 tuple(jnp.zeros((bq, LANES), F32) for _ in range(H // 2)))
        row = lax.broadcasted_iota(jnp.int32, (bq, bk), 0)
        col = lax.broadcasted_iota(jnp.int32, (bq, bk), 1)
        for t in range(per):
            carry = block(i * per + per - 1 - t, carry, col + (per - 1 - t) * bk < row)
        tails, accs = lax.fori_loop(0, i * per, lambda t, c: block(i * per - 1 - t, c, None), carry)
        o_ref[...] = jnp.concatenate(accs, axis=1)
        t_ref[...] = _lane_col(tails)

    return _pcall(body, name="sb_fwd", grid=(S // bq,),
                  in_specs=[pl.BlockSpec((bq, 256), lambda i: (i, 0)), pl.BlockSpec((S, 256), lambda i: (0, 1)),
                            pl.BlockSpec((S, 256), lambda i: (0, 2))],
                  out_specs=(pl.BlockSpec((bq, 256), lambda i: (i, 0)), pl.BlockSpec((bq, LANES), lambda i: (i, 0))),
                  out_shape=(jax.ShapeDtypeStruct((S, 256), F32), jax.ShapeDtypeStruct((S, LANES), F32)),
                  compiler_params=_params(("parallel",)))(z, z, z)


def _sb_bwd(z, tot, dc_all):
    S = z.shape[0]
    H = SB_HEADS
    bq, bk = _row_block(S, SB_BQ), _row_block(S, SB_BK)
    per = bq // bk

    def body(q_ref, k_ref, v_ref, t_ref, do_ref, dq_ref, dk_ref, dv_ref):
        i = pl.program_id(0)

        @pl.when(i == 0)
        def _():
            dk_ref[...] = jnp.zeros_like(dk_ref)
            dv_ref[...] = jnp.zeros_like(dv_ref)

        lo, hi = _half_masks()
        q16 = (q_ref[...] * SB_SCALE).astype(BF16)
        do16 = do_ref[...].astype(BF16)
        tot_v = t_ref[...]
        masks = [lo if h % 2 == 0 else hi for h in range(H)]
        qm = [_keep(masks[h], _pair(q16, h // 2)) for h in range(H)]
        dom = [_keep(masks[h], _pair(do16, h // 2)) for h in range(H)]
        tots = [tot_v[:, h:h + 1] for h in range(H)]
        rr = lax.broadcasted_iota(jnp.int32, (bk, bk), 0)
        cc = lax.broadcasted_iota(jnp.int32, (bk, bk), 1)
        upto = (rr <= cc).astype(BF16)
        before = (rr < cc).astype(BF16)

        def block(j, carry, mask):
            heads, eheads, dqs = carry
            off = pl.multiple_of(j * bk, bk)
            kb = k_ref[pl.ds(off, bk), :].astype(BF16)
            vb = v_ref[pl.ds(off, bk), :].astype(BF16)
            zs = [_dot(qm[h], _pair(kb, h // 2), NT) for h in range(H)]
            dws = [_dot(dom[h], _pair(vb, h // 2), NT) for h in range(H)]
            sps = [_softplus(z) for z in zs]
            lnots = [-sp if mask is None else jnp.where(mask, -sp, 0.0) for sp in sps]
            pres = [_dot_exact(lnot, upto) for lnot in lnots]
            lsigs = [zs[h] - sps[h] for h in range(H)]
            ws = [jnp.exp(lsigs[h] + (tots[h] - (heads[h] + pres[h]))) for h in range(H)]
            if mask is not None:
                ws = [jnp.where(mask, w, 0.0) for w in ws]
            es = [ws[h] * dws[h] for h in range(H)]
            esums = [eheads[h] + _dot_exact(es[h], before) for h in range(H)]
            dz16 = []
            for h in range(H):
                sig = jnp.exp(lsigs[h])
                dz = es[h] * (1.0 - sig) - sig * esums[h]
                dz16.append((dz if mask is None else jnp.where(mask, dz, 0.0)).astype(BF16))
            w16 = [w.astype(BF16) for w in ws]
            dkps = [_dot(dz16[h], qm[h], TN) for h in range(H)]
            dvps = [_dot(w16[h], dom[h], TN) for h in range(H)]
            dqps = [_dot(dz16[h], _pair(kb, h // 2)) for h in range(H)]
            dk_ref[pl.ds(off, bk), :] += jnp.concatenate([dkps[2 * p] + dkps[2 * p + 1] for p in range(H // 2)], axis=1)
            dv_ref[pl.ds(off, bk), :] += jnp.concatenate([dvps[2 * p] + dvps[2 * p + 1] for p in range(H // 2)], axis=1)
            dqs2 = [dqs[p] + jnp.where(lo, dqps[2 * p], dqps[2 * p + 1]) for p in range(H // 2)]
            heads2 = [heads[h] + jnp.sum(lnots[h], axis=-1, keepdims=True) for h in range(H)]
            eheads2 = [eheads[h] + jnp.sum(es[h], axis=-1, keepdims=True) for h in range(H)]
            return tuple(heads2), tuple(eheads2), tuple(dqs2)

        zeros = tuple(jnp.zeros((bq, 1), F32) for _ in range(H))
        init = (zeros, zeros, tuple(jnp.zeros((bq, LANES), F32) for _ in range(H // 2)))
        carry = lax.fori_loop(0, i * per, lambda j, c: block(j, c, None), init)
        row = lax.broadcasted_iota(jnp.int32, (bq, bk), 0)
        col = lax.broadcasted_iota(jnp.int32, (bq, bk), 1)
        for t in range(per):
            carry = block(i * per + t, carry, col + t * bk < row)
        dq_ref[...] = jnp.concatenate(carry[2], axis=1) * SB_SCALE

    blk = pl.BlockSpec((bq, 256), lambda i: (i, 0))
    full = pl.BlockSpec((S, 256), lambda i: (0, 0))
    return _pcall(body, name="sb_bwd", grid=(S // bq,),
                  in_specs=[blk, pl.BlockSpec((S, 256), lambda i: (0, 1)), pl.BlockSpec((S, 256), lambda i: (0, 2)),
                            pl.BlockSpec((bq, LANES), lambda i: (i, 0)), blk],
                  out_specs=(blk, full, full), out_shape=(jax.ShapeDtypeStruct((S, 256), F32),) * 3,
                  compiler_params=_params(("arbitrary",)))(z, z, z, tot, dc_all)


_HBM = pl.BlockSpec(memory_space=pltpu.HBM)
_SEM = pl.BlockSpec(memory_space=pltpu.SEMAPHORE)
_ANY = pl.BlockSpec(memory_space=pl.ANY)


def _place():
    x, y, c = lax.axis_index("x"), lax.axis_index("y"), lax.axis_index("c")
    other_chips = ((1 - x, y), (x, 1 - y), (1 - x, 1 - y))
    return x, y, c, other_chips


def _remote(src, dst, send_sems, recv_sems, k, dev):
    return pltpu.make_async_remote_copy(src_ref=src, dst_ref=dst, send_sem=send_sems.at[k], recv_sem=recv_sems.at[k],
                                        device_id=dev, device_id_type=MESH)


def _half_rows(c, n_rows):
    half = n_rows // 2
    return pl.ds(pl.multiple_of(c * half, 16), half)


def _gather_copies(src, land, send_sems, recv_sems):
    x, y, c, chips = _place()
    me = 2 * x + y
    cps = []
    for w in range(len(src)):
        rows = _half_rows(c, src[w].shape[0])
        for k, (cx, cy) in enumerate(chips):
            cps.append(_remote(src[w].at[rows], land[w].at[me, rows], send_sems, recv_sems, 3 * w + k, (cx, cy, c)))
    return cps


def _scatter_copies(src, land, send_sems, recv_sems):
    x, y, c, chips = _place()
    cps = []
    for w in range(len(src)):
        for k, (cx, cy) in enumerate(chips):
            cps.append(_remote(src[w].at[2 * cx + cy], land[w].at[k], send_sems, recv_sems, 3 * w + k, (cx, cy, c)))
    return cps


def _exchange_start(copies, srcs, lands, after, name):
    n = len(srcs)

    def body(*refs):
        src, land = refs[:n], refs[n:2 * n]
        send_sems, recv_sems = refs[2 * n + 1], refs[2 * n + 2]
        token = refs[-1]
        for cp in copies(src, land, send_sems, recv_sems):
            cp.start()
        token[...] = jnp.zeros_like(token)

    thru = tuple(pltpu.HBM(a.shape, a.dtype) for a in list(srcs) + list(lands))
    out = _pcall(body, name=name,
                 out_shape=(pltpu.SemaphoreType.DMA((3 * n,)), pltpu.SemaphoreType.DMA((3 * n,))) + thru
                 + (jax.ShapeDtypeStruct((8, LANES), F32),),
                 in_specs=[_HBM] * (2 * n) + [_ANY],
                 out_specs=(_SEM, _SEM) + (_HBM,) * (2 * n) + (pl.BlockSpec(memory_space=pltpu.VMEM),),
                 input_output_aliases={i: 2 + i for i in range(2 * n)},
                 compiler_params=pltpu.CompilerParams(has_side_effects=pltpu.SideEffectType.DATAFLOW_SIDE_EFFECTING))(
        *[pltpu.with_memory_space_constraint(a, pltpu.HBM) for a in list(srcs) + list(lands)], after)
    return out[0], out[1], list(out[2:2 + n]), list(out[2 + n:2 + 2 * n]), out[-1]


def _exchange_wait(copies, send_sems, recv_sems, srcs, lands, after, name):
    n = len(srcs)

    def body(*refs):
        src, land = refs[:n], refs[n:2 * n]
        ssem, rsem = refs[2 * n], refs[2 * n + 1]
        for cp in copies(src, land, ssem, rsem):
            cp.wait_send()
            cp.wait_recv()

    out = _pcall(body, name=name, out_shape=tuple(pltpu.HBM(a.shape, a.dtype) for a in list(srcs) + list(lands)),
                 in_specs=[_HBM] * (2 * n) + [_SEM, _SEM, _ANY], out_specs=(_HBM,) * (2 * n),
                 input_output_aliases={i: i for i in range(2 * n)},
                 compiler_params=pltpu.CompilerParams(has_side_effects=pltpu.SideEffectType.DATAFLOW_SIDE_EFFECTING))(
        *srcs, *lands, send_sems, recv_sems, after)
    return list(out[:n]), list(out[n:])


def _forward_rows(gathered):
    n = len(gathered)

    def body(*refs):
        src, dst = refs[:n], refs[n:2 * n]
        send_sems, recv_sems = refs[2 * n:]
        x, y, c, chips = _place()
        sib = (x, y, 1 - c)
        cps = []
        for w in range(n):
            K = src[w].shape[1]
            for k, (cx, cy) in enumerate(chips):
                blk = (2 * cx + cy, _half_rows(c, K))
                cps.append(_remote(src[w].at[blk], dst[w].at[blk], send_sems, recv_sems, 3 * w + k, sib))
        for cp in cps:
            cp.start()
        for w in range(n):
            K = src[w].shape[1]
            for k, (cx, cy) in enumerate(chips):
                blk = dst[w].at[2 * cx + cy, _half_rows(1 - c, K)]
                _remote(blk, blk, send_sems, recv_sems, 3 * w + k, sib).wait_recv()
        for cp in cps:
            cp.wait_send()

    return _pcall(body, name="forward_rows", in_specs=[_HBM] * n, out_specs=(_HBM,) * n,
                  out_shape=tuple(jax.ShapeDtypeStruct(g.shape, g.dtype) for g in gathered),
                  input_output_aliases={w: w for w in range(n)},
                  scratch_shapes=[pltpu.SemaphoreType.DMA((3 * n,)), pltpu.SemaphoreType.DMA((3 * n,))])(*gathered)


def _swap_rows(bufs):
    n = len(bufs)

    def body(*refs):
        src, dst = refs[:n], refs[n:2 * n]
        send_sems, recv_sems = refs[2 * n:]
        x, y, c, _ = _place()
        cps = [_remote(src[w].at[:, _half_rows(1 - c, src[w].shape[1])], dst[w], send_sems, recv_sems, w,
                       (x, y, 1 - c)) for w in range(n)]
        for cp in cps:
            cp.start()
        for cp in cps:
            cp.wait()

    return _pcall(body, name="swap_rows", in_specs=[_HBM] * n, out_specs=(_HBM,) * n,
                  out_shape=tuple(jax.ShapeDtypeStruct((b.shape[0], b.shape[1] // 2, b.shape[2]), b.dtype) for b in bufs),
                  scratch_shapes=[pltpu.SemaphoreType.DMA((n,)), pltpu.SemaphoreType.DMA((n,))])(*bufs)


def _join_rows(fins):
    n = len(fins)

    def body(*refs):
        src, dst = refs[:n], refs[n:2 * n]
        send_sems, recv_sems = refs[2 * n:]
        x, y, c, _ = _place()
        sib = (x, y, 1 - c)
        cps = []
        for w in range(n):
            blk = (slice(None), _half_rows(c, src[w].shape[1]))
            cps.append(_remote(src[w].at[blk], dst[w].at[blk], send_sems, recv_sems, w, sib))
        for cp in cps:
            cp.start()
        for w in range(n):
            blk = dst[w].at[:, _half_rows(1 - c, src[w].shape[1])]
            _remote(blk, blk, send_sems, recv_sems, w, sib).wait_recv()
        for cp in cps:
            cp.wait_send()

    return _pcall(body, name="join_rows", in_specs=[_HBM] * n, out_specs=(_HBM,) * n,
                  out_shape=tuple(jax.ShapeDtypeStruct(f.shape, f.dtype) for f in fins),
                  input_output_aliases={w: w for w in range(n)},
                  scratch_shapes=[pltpu.SemaphoreType.DMA((n,)), pltpu.SemaphoreType.DMA((n,))])(*fins)


def _add_rows(buf, got, c_idx):
    _, Kh, N = got.shape
    bk = _row_block(Kh, 256)
    nb = Kh // bk

    def body(c_ref, a_ref, b_ref, o_ref):
        o_ref[...] = (a_ref[...].astype(F32) + b_ref[...].astype(F32)).astype(o_ref.dtype)

    blk = pl.BlockSpec((None, bk, N), lambda j, i, c_ref: (j, i, 0))
    grid_spec = pltpu.PrefetchScalarGridSpec(
        num_scalar_prefetch=1, grid=(N_CHIPS, nb),
        in_specs=[pl.BlockSpec((None, bk, N), lambda j, i, c_ref: (j, c_ref[0] * nb + i, 0)), blk], out_specs=blk)
    return _pcall(body, name="add_rows", grid_spec=grid_spec, out_shape=jax.ShapeDtypeStruct((N_CHIPS, Kh, N), BF16),
                  compiler_params=_params(("parallel", "parallel")))(c_idx, buf, got)


def _add_chip_sums(s1, got, idx, layer, into):
    _, Kh, N = s1.shape
    bk = _row_block(Kh, 256)
    nb = Kh // bk

    def body(idx_ref, a_ref, b_ref, *rest):
        acc = a_ref[...].astype(F32)
        for k in range(3):
            acc = acc + b_ref[k].astype(F32)
        rest[-1][...] = acc

    in_specs = [pl.BlockSpec((None, bk, N), lambda i, idx_ref: (idx_ref[0], i, 0)),
                pl.BlockSpec((3, bk, N), lambda i, idx_ref: (0, i, 0))]
    ops = [idx, s1, got]
    if into is not None:
        in_specs.append(_ANY)
        ops.append(into)
    grid_spec = pltpu.PrefetchScalarGridSpec(
        num_scalar_prefetch=1, grid=(nb,), in_specs=in_specs,
        out_specs=pl.BlockSpec((None, bk, N), lambda i, idx_ref: (layer, idx_ref[1] * nb + i, 0)))
    return _pcall(body, name="add_chip_sums" if into is None else "add_chip_sums_into", grid_spec=grid_spec,
                  out_shape=jax.ShapeDtypeStruct((2, 2 * Kh, N), F32),
                  input_output_aliases={} if into is None else {3: 0},
                  compiler_params=_params(("parallel",)))(*ops)


def _allreduce_small(v):
    R, W = v.shape

    def body(v_ref, o_ref, buf, send_sems, recv_sems):
        x, y, c, _ = _place()
        me = 4 * x + 2 * y + c
        buf[0] = v_ref[...]
        cps = []
        for r in range(1, 8):
            peer = (x if not r & 4 else 1 - x, y if not r & 2 else 1 - y, c if not r & 1 else 1 - c)
            cp = _remote(v_ref, buf.at[r], send_sems, recv_sems, r - 1, peer)
            cp.start()
            cps.append(cp)
        for cp in cps:
            cp.wait()
        acc = buf[me]
        for d in range(1, 8):
            acc = acc + buf[jnp.bitwise_xor(me, d)]
        o_ref[...] = acc

    return _pcall(body, name="allreduce_small", out_shape=jax.ShapeDtypeStruct((R, W), F32),
                  in_specs=[pl.BlockSpec(memory_space=pltpu.VMEM)], out_specs=pl.BlockSpec(memory_space=pltpu.VMEM),
                  scratch_shapes=[pltpu.VMEM((8, R, W), F32), pltpu.SemaphoreType.DMA((7,)),
                                  pltpu.SemaphoreType.DMA((7,))])(v)


def _cat_cols(g):
    return g.transpose(1, 0, 2).reshape(g.shape[1], -1)


def _cut_cols(w):
    K, N = w.shape
    return w.reshape(K, N_CHIPS, N // N_CHIPS).transpose(1, 0, 2)


def _regroup_w_in(w):
    zeros = lambda n: jnp.zeros((w.shape[0], n), w.dtype)
    return jnp.concatenate([w[:, 0:768], w[:, 1152:1408], zeros(64), w[:, 1408:1440], zeros(32),
                            w[:, 768:1152], w[:, 1440:2208]], axis=1)


def _ungroup_w_in(w):
    return jnp.concatenate([w[:, 0:768], w[:, Z_Q:Z_Q + 384], w[:, Z_KV:Z_KV + 256],
                            w[:, Z_KR + 64:Z_KR + 96], w[:, Z_CV:Z_W]], axis=1)


def _regroup_w_uq(w):
    K = w.shape[0]
    return jnp.pad(w.reshape(K, MLA_HEADS, 96), ((0, 0), (0, 0), (0, 32))).reshape(K, MLA_HEADS * LANES)


def _ungroup_w_uq(w):
    K = w.shape[0]
    return w.reshape(K, MLA_HEADS, LANES)[:, :, :96].reshape(K, MLA_HEADS * 96)


def _regroup_w_ukv(w):
    K = w.shape[0]
    t = w.reshape(K, MLA_HEADS, 128)
    return jnp.concatenate([jnp.pad(t[:, :, :64], ((0, 0), (0, 0), (0, 64))).reshape(K, MLA_HEADS * LANES),
                            t[:, :, 64:].reshape(K, MLA_HEADS * HEAD_DIM)], axis=1)


def _ungroup_w_ukv(w):
    K = w.shape[0]
    return jnp.concatenate([w[:, :MLA_HEADS * LANES].reshape(K, MLA_HEADS, LANES)[:, :, :64],
                            w[:, MLA_HEADS * LANES:].reshape(K, MLA_HEADS, 64)], axis=2).reshape(K, 1024)


def _ffn_fwd(x, g_pre, g_post, wg, wu, wd_rows):
    h = _norm_fwd(x, g_pre, "norm_fwd")
    G, U, A = _ffn_gate_up(h, wg, wu, "ffn_gate_up")
    Y = _mm([(A, wd_rows)], "nn", F32, "ffn_down", bm=1024)
    return _resid_norm(x, Y, g_post, 0.5, "resid_norm"), dict(x=x, h=h, G=G, U=U, A=A, Y=Y)


def _ffn_bwd(dxo, sv, g_pre, g_post, wg, wu, wd_rows):
    dY, dg_post = _norm_bwd(sv["Y"], g_post, dxo, 0.5, None, BF16, "norm_bwd_post")
    dG, dU = _ffn_bwd_mid(dY, wd_rows, sv["G"], sv["U"], "ffn_bwd_mid")
    dws = (_dw_shards(sv["h"], dG, 1, "ffn_dw_in"), _dw_shards(sv["h"], dU, 1, "ffn_dw_in"),
           _dw_shards(sv["A"], dY, 0, "ffn_dw_down"))
    dh = _ffn_dh(dG, dU, wg, wu, "ffn_dh")
    dx, dg_pre = _norm_bwd(sv["x"], g_pre, dh, 1.0, dxo, F32, "norm_bwd_pre")
    return dx, dg_pre, dg_post, dws


def _layer_fwd(x, p_l, Gw, Wl, G, w_conv, cp, sp):
    sv = {}
    x1, sv["ffn1"] = _ffn_fwd(x, G["g_ffn1_pre"], G["g_ffn1_post"], Gw["w1_gate"], Gw["w1_up"], Wl["w1_down"])

    h2 = _norm_fwd(x1, G["g_mix_pre"], "norm_fwd")
    Z = _mm([(h2, Wl["w_in"])], "nn", F32, "mix_in")
    o_sb, tot_sb = _sb_fwd(Z)
    cq, ckv = Z[:, Z_Q:Z_Q + 384], Z[:, Z_KV:Z_KV + 256]
    nq = _norm_fwd(cq, G["g_mla_q"], "norm_fwd_q")
    Qf = _mm([(nq, Wl["w_mla_uq"])], "nn", F32, "mla_uq")
    nkv = _norm_fwd(ckv, G["g_mla_kv"], "norm_fwd_kv")
    KV = _mm([(nkv, Wl["w_mla_ukv"])], "nn", F32, "mla_ukv")
    qcat, kcat = _mla_prep(Qf, KV, Z, cp, sp)
    o_mla, lse = _mla_fwd(qcat, kcat, KV)
    y_cv = _conv_fwd(Z, w_conv)
    C = jnp.concatenate([o_sb, o_mla, y_cv], axis=1).astype(BF16)
    Mx = _mm([(C, Wl["w_out"])], "nn", F32, "mix_out")
    x2 = _resid_norm(x1, Mx, G["g_mix_post"], 1.0, "resid_norm")
    sv["mix"] = dict(x=x1, h=h2, Z=Z, tot_sb=tot_sb, cq=cq, ckv=ckv, nq=nq, nkv=nkv, qcat=qcat, kcat=kcat, KV=KV,
                     o_mla=o_mla, lse=lse, C=C, Mx=Mx)

    x3, sv["ffn2"] = _ffn_fwd(x2, G["g_ffn2_pre"], G["g_ffn2_post"], Gw["w2_gate"], Gw["w2_up"], Wl["w2_down"])

    h4 = _norm_fwd(x3, G["g_ple_pre"], "norm_fwd")
    p16 = p_l.astype(BF16)
    Qg = _mm([(h4, Wl["w_ple_gate"])], "nn", F32, "ple_gate")
    Pp = _ple_proj(p16, Gw["w_ple_proj"])
    e = _ew(lambda q, pp: _sigmoid(q) * pp, [Qg, Pp], [(D_MODEL, F32)], "ple_mul")
    x4 = _resid_norm(x3, e, G["g_ple_post"], 1.0, "resid_norm")
    sv["ple"] = dict(x=x3, h=h4, p16=p16, Qg=Qg, Pp=Pp, e=e)
    return x4, sv


def _layer_bwd(dx4, sv, Gw, Wl, G, w_conv, cp, sp):
    gg, gw = {}, {}

    s = sv["ple"]
    de, gg["g_ple_post"] = _norm_bwd(s["e"], G["g_ple_post"], dx4, 1.0, None, F32, "norm_bwd_e")

    def ple_bwd(dev, q, pp):
        sg = _sigmoid(q)
        return dev * pp * sg * (1.0 - sg), dev * sg

    dQg, dPp = _ew(ple_bwd, [de, s["Qg"], s["Pp"]], [(D_MODEL, BF16)] * 2, "ple_mul_bwd")
    gw["w_ple_proj"] = _dw_shards(s["p16"], dPp, 1, "ple_dw_proj")
    gw["w_ple_gate"] = _dw_shards(s["h"], dQg, 0, "ple_dw_gate")
    dh4 = _mm([(dQg, Wl["w_ple_gate"])], "nt", F32, "ple_dh")
    dx3, gg["g_ple_pre"] = _norm_bwd(s["x"], G["g_ple_pre"], dh4, 1.0, dx4, F32, "norm_bwd_pre")

    dx2, gg["g_ffn2_pre"], gg["g_ffn2_post"], (gw["w2_gate"], gw["w2_up"], gw["w2_down"]) = _ffn_bwd(
        dx3, sv["ffn2"], G["g_ffn2_pre"], G["g_ffn2_post"], Gw["w2_gate"], Gw["w2_up"], Wl["w2_down"])

    s = sv["mix"]
    dM, gg["g_mix_post"] = _norm_bwd(s["Mx"], G["g_mix_post"], dx2, 1.0, None, BF16, "norm_bwd_post")
    dC = _mm([(dM, Wl["w_out"])], "nt", F32, "mix_out_dx")
    gw["w_out"] = _dw_shards(s["C"], dM, 0, "mix_out_dw")

    db, dc, dhh, gg["w_conv"] = _conv_bwd(s["Z"], w_conv, dC)

    dqc, dkc, dv = _mla_bwd(s["qcat"], s["kcat"], s["KV"], s["o_mla"], s["lse"], dC)
    dQf, dkr = _mla_prep_bwd(dqc, dkc, cp, sp)
    gw["w_mla_uq"] = _cut_cols(_ungroup_w_uq(_mm([(s["nq"], dQf)], "tn", BF16, "mla_uq_dw")))
    dnq = _mm([(dQf, Wl["w_mla_uq"])], "nt", F32, "mla_uq_dx")
    dcq, gg["g_mla_q"] = _norm_bwd(s["cq"], G["g_mla_q"], dnq, 1.0, None, F32, "norm_bwd_q")
    dkv = jnp.concatenate([dkc, dv], axis=1).astype(BF16)
    gw["w_mla_ukv"] = _cut_cols(_ungroup_w_ukv(_mm([(s["nkv"], dkv)], "tn", BF16, "mla_ukv_dw")))
    dnkv = _mm([(dkv, Wl["w_mla_ukv"])], "nt", F32, "mla_ukv_dx")
    dckv, gg["g_mla_kv"] = _norm_bwd(s["ckv"], G["g_mla_kv"], dnkv, 1.0, None, F32, "norm_bwd_kv")

    dsq, dsk, dsv = _sb_bwd(s["Z"], s["tot_sb"], dC)
    dZ = jnp.concatenate([dsq, dsk, dsv, dckv, dkr, dcq, db, dc, dhh], axis=1).astype(BF16)
    gw["w_in"] = _cut_cols(_ungroup_w_in(_mm([(s["h"], dZ)], "tn", BF16, "mix_in_dw")))
    dh2 = _mm([(dZ, Wl["w_in"])], "nt", F32, "mix_in_dx")
    dx1, gg["g_mix_pre"] = _norm_bwd(s["x"], G["g_mix_pre"], dh2, 1.0, dx2, F32, "norm_bwd_pre")

    dx0, gg["g_ffn1_pre"], gg["g_ffn1_post"], (gw["w1_gate"], gw["w1_up"], gw["w1_down"]) = _ffn_bwd(
        dx1, sv["ffn1"], G["g_ffn1_pre"], G["g_ffn1_post"], Gw["w1_gate"], Gw["w1_up"], Wl["w1_down"])
    return dx0, gg, gw


def _pack_small(vecs):
    flat = jnp.concatenate([v.reshape(-1) for v in vecs])
    rows = -(-flat.shape[0] // (8 * LANES)) * 8
    return jnp.pad(flat, (0, rows * LANES - flat.shape[0])).reshape(rows, LANES)


def _layer_weights(landed, shards, chip):
    Gw = {name: lax.dynamic_update_slice(g, s[None], (chip, 0, 0)) for name, g, s in zip(BIG_NAMES, landed, shards)}
    Wl = {name: Gw[name].reshape(-1, Gw[name].shape[2]) for name in ("w1_down", "w2_down", "w_out", "w_ple_gate")}
    Wl["w_in"] = _regroup_w_in(_cat_cols(Gw["w_in"]))
    Wl["w_mla_uq"] = _regroup_w_uq(_cat_cols(Gw["w_mla_uq"]))
    Wl["w_mla_ukv"] = _regroup_w_ukv(_cat_cols(Gw["w_mla_ukv"]))
    return Gw, Wl


def kernel(x, p, positions, g_ffn1_pre, w1_gate, w1_up, w1_down, g_ffn1_post, g_mix_pre, w_in, g_mla_q, w_mla_uq, g_mla_kv, w_mla_ukv, w_conv, w_out, g_mix_post, g_ffn2_pre, w2_gate, w2_up, w2_down, g_ffn2_post, g_ple_pre, w_ple_gate, w_ple_proj, g_ple_post, loss_target, m_g_ffn1_pre, m_w1_gate, m_w1_up, m_w1_down, m_g_ffn1_post, m_g_mix_pre, m_w_in, m_g_mla_q, m_w_mla_uq, m_g_mla_kv, m_w_mla_ukv, m_w_conv, m_w_out, m_g_mix_post, m_g_ffn2_pre, m_w2_gate, m_w2_up, m_w2_down, m_g_ffn2_post, m_g_ple_pre, m_w_ple_gate, m_w_ple_proj, m_g_ple_post, v_g_ffn1_pre, v_w1_gate, v_w1_up, v_w1_down, v_g_ffn1_post, v_g_mix_pre, v_w_in, v_g_mla_q, v_w_mla_uq, v_g_mla_kv, v_w_mla_ukv, v_w_conv, v_w_out, v_g_mix_post, v_g_ffn2_pre, v_w2_gate, v_w2_up, v_w2_down, v_g_ffn2_post, v_g_ple_pre, v_w_ple_gate, v_w_ple_proj, v_g_ple_post):
    w = dict(g_ffn1_pre=g_ffn1_pre, w1_gate=w1_gate, w1_up=w1_up, w1_down=w1_down, g_ffn1_post=g_ffn1_post,
             g_mix_pre=g_mix_pre, w_in=w_in, g_mla_q=g_mla_q, w_mla_uq=w_mla_uq, g_mla_kv=g_mla_kv,
             w_mla_ukv=w_mla_ukv, w_conv=w_conv, w_out=w_out, g_mix_post=g_mix_post, g_ffn2_pre=g_ffn2_pre,
             w2_gate=w2_gate, w2_up=w2_up, w2_down=w2_down, g_ffn2_post=g_ffn2_post, g_ple_pre=g_ple_pre,
             w_ple_gate=w_ple_gate, w_ple_proj=w_ple_proj, g_ple_post=g_ple_post)
    m = dict(g_ffn1_pre=m_g_ffn1_pre, w1_gate=m_w1_gate, w1_up=m_w1_up, w1_down=m_w1_down, g_ffn1_post=m_g_ffn1_post,
             g_mix_pre=m_g_mix_pre, w_in=m_w_in, g_mla_q=m_g_mla_q, w_mla_uq=m_w_mla_uq, g_mla_kv=m_g_mla_kv,
             w_mla_ukv=m_w_mla_ukv, w_conv=m_w_conv, w_out=m_w_out, g_mix_post=m_g_mix_post, g_ffn2_pre=m_g_ffn2_pre,
             w2_gate=m_w2_gate, w2_up=m_w2_up, w2_down=m_w2_down, g_ffn2_post=m_g_ffn2_post, g_ple_pre=m_g_ple_pre,
             w_ple_gate=m_w_ple_gate, w_ple_proj=m_w_ple_proj, g_ple_post=m_g_ple_post)
    v = dict(g_ffn1_pre=v_g_ffn1_pre, w1_gate=v_w1_gate, w1_up=v_w1_up, w1_down=v_w1_down, g_ffn1_post=v_g_ffn1_post,
             g_mix_pre=v_g_mix_pre, w_in=v_w_in, g_mla_q=v_g_mla_q, w_mla_uq=v_w_mla_uq, g_mla_kv=v_g_mla_kv,
             w_mla_ukv=v_w_mla_ukv, w_conv=v_w_conv, w_out=v_w_out, g_mix_post=v_g_mix_post, g_ffn2_pre=v_g_ffn2_pre,
             w2_gate=v_w2_gate, w2_up=v_w2_up, w2_down=v_w2_down, g_ffn2_post=v_g_ffn2_post, g_ple_pre=v_g_ple_pre,
             w_ple_gate=v_w_ple_gate, w_ple_proj=v_w_ple_proj, g_ple_post=v_g_ple_post)

    depth = g_ffn1_pre.shape[0]
    assert depth == 2, "the reduced gradients are assembled in [2, K, N] buffers"
    S = x.shape[1]
    cx, cy, cc = lax.axis_index("x"), lax.axis_index("y"), lax.axis_index("c")
    chip = 2 * cx + cy
    c_idx = cc.reshape(1).astype(jnp.int32)
    idx2 = jnp.stack([chip, cc]).astype(jnp.int32)

    def padded(name, i):
        (ks, ns), (kp, np_) = BIG[name]
        return jnp.pad(w[name][i].astype(BF16), ((0, kp - ks), (0, np_ - ns)))

    shards = [[padded(name, i) for name in BIG_NAMES] for i in range(depth)]
    started, after = [], positions
    for i in range(depth):
        lands = [lax.empty((N_CHIPS,) + s.shape, BF16) for s in shards[i]]
        started.append(_exchange_start(_gather_copies, shards[i], lands, after, "gather_start_%d" % i))
        after = started[-1][4]

    conv_slot = lax.dynamic_update_slice(jnp.zeros((depth, 3, 256), F32),
                                         w_conv * (cc == 0).astype(F32), (0, 0, 64 * chip))
    w_conv_full = _allreduce_small(_pack_small([conv_slot]))[:depth * 3 * 256 // LANES].reshape(depth, 3, 256)

    inv = ROPE_BASE ** (-jnp.arange(ROPE_HALF, dtype=F32) / ROPE_HALF)
    zeros = lambda n: jnp.zeros((n,), F32)
    ones = jnp.ones((ROPE_HALF,), F32)
    inv_pat = jnp.concatenate([zeros(64), inv, inv, zeros(32)]).reshape(1, LANES)
    sign_pat = jnp.concatenate([zeros(64), -ones, ones, zeros(32)]).reshape(1, LANES)
    cp, sp = _rope_tables(positions.reshape(S, 1), inv_pat, sign_pat)

    xs, saved, layers = x[0], [], []
    for i in range(depth):
        ssem, rsem, srcs, lands, _ = started[i]
        srcs, landed = _exchange_wait(_gather_copies, ssem, rsem, srcs, lands, after if i == 0 else xs,
                                      "gather_wait_%d" % i)
        Gw, Wl = _layer_weights(_forward_rows(landed), srcs, chip)
        Gl = {name: w[name][i].reshape(1, n) for name, n in GAINS}
        layers.append((Gw, Wl, Gl))
        xs, sv = _layer_fwd(xs, p[i, 0], Gw, Wl, Gl, w_conv_full[i], cp, sp)
        saved.append(sv)

    loss_part, dx = _loss_and_grad(xs, loss_target[0])
    loss = lax.psum(loss_part[0, 0], AXES)

    ggs, scattering, token = [None] * depth, [None] * depth, None
    for i in reversed(range(depth)):
        Gw, Wl, Gl = layers[i]
        if token is not None:
            Gl = dict(Gl, g_ple_post=Gl["g_ple_post"] + token[0, 0])
        dx, ggs[i], gw = _layer_bwd(dx, saved[i], Gw, Wl, Gl, w_conv_full[i], cp, sp)
        full = [gw[name] for name in BIG_NAMES]
        pair_sums = [_add_rows(b, g, c_idx) for b, g in zip(full, _swap_rows(full))]
        lands = [lax.empty((3,) + s.shape[1:], BF16) for s in pair_sums]
        scattering[i] = _exchange_start(_scatter_copies, pair_sums, lands, c_idx, "scatter_start_%d" % i)
        token = scattering[i][4]

    fins, after = [None] * len(BIG_NAMES), dx
    for i in reversed(range(depth)):
        ssem, rsem, srcs, lands, _ = scattering[i]
        srcs, arrived = _exchange_wait(_scatter_copies, ssem, rsem, srcs, lands, after, "scatter_wait_%d" % i)
        fins = [_add_chip_sums(s1, got, idx2, i, into) for s1, got, into in zip(srcs, arrived, fins)]
        after = fins[0]
    fins = _join_rows(fins)
    grads = {name: f[:, :BIG[name][0][0], :BIG[name][0][1]] for name, f in zip(BIG_NAMES, fins)}

    small_names = [name for name, _ in GAINS] + ["w_conv"]
    small = _allreduce_small(_pack_small([jnp.stack([ggs[i][name] for i in range(depth)]) for name in small_names]))
    flat, off = small.reshape(-1), 0
    for name, n in GAINS:
        grads[name] = flat[off:off + depth * n].reshape(depth, n)
        off += depth * n
    conv_full = flat[off:off + depth * 3 * 256].reshape(depth, 3, 256)
    grads["w_conv"] = lax.dynamic_slice(conv_full, (0, 0, 64 * chip), (depth, 3, 64))

    deltas, new_m, new_v = {}, {}, {}
    for name in WEIGHT_ORDER:
        shape = w[name].shape
        two_d = (shape[0] * shape[1], shape[2]) if len(shape) == 3 else shape
        d_, m_, v_ = _adamw(w[name].reshape(two_d), grads[name].reshape(two_d), m[name].reshape(two_d),
                            v[name].reshape(two_d), "adamw")
        deltas[name], new_m[name], new_v[name] = d_.reshape(shape), m_.reshape(shape), v_.reshape(shape)

    return (loss, dx[None], *[grads[n] for n in WEIGHT_ORDER], *[deltas[n] for n in WEIGHT_ORDER],
            *[new_m[n] for n in WEIGHT_ORDER], *[new_v[n] for n in WEIGHT_ORDER])
```

```python
import functools

import jax
import jax.numpy as jnp
from jax import lax
from jax.experimental import pallas as pl
from jax.experimental.pallas import tpu as pltpu

F32 = jnp.float32
BF16 = jnp.bfloat16
MESH = pl.DeviceIdType.MESH
AXES = ("x", "y", "c")

D_MODEL = 1024
N_CHIPS = 4
FF_SHARD = 704
FF_PAD = 768
D_FF_PAD = N_CHIPS * FF_PAD
EPS = 1e-6
NEG_INF = -1e30
ROPE_BASE = 10000.0
ROPE_HALF = 16
LANES = 128
SB_HEADS, MLA_HEADS, HEAD_DIM = 4, 8, 64
MLA_SCALE = 96.0 ** -0.5
SB_SCALE = 64.0 ** -0.5
SB_BQ, SB_BK = 256, 128
MLA_BQ, MLA_BK = 256, 256
ADAM_LR, ADAM_B1, ADAM_B2, ADAM_EPS, ADAM_WD, ADAM_STEP = 0.001, 0.9, 0.999, 1e-08, 0.01, 10
VMEM_LIMIT = 48 * 2 ** 20

Z_SB, Z_KV, Z_KR, Z_Q, Z_CV, Z_W = 0, 768, 1024, 1152, 1536, 2304

NT = (((1,), (1,)), ((), ()))
TN = (((0,), (0,)), ((), ()))

BIG = {"w1_gate": ((1024, FF_SHARD), (1024, FF_PAD)), "w1_up": ((1024, FF_SHARD), (1024, FF_PAD)),
       "w1_down": ((FF_SHARD, 1024), (FF_PAD, 1024)), "w_in": ((1024, 552), (1024, 552)),
       "w_mla_uq": ((384, 192), (384, 192)), "w_mla_ukv": ((256, 256), (256, 256)),
       "w_out": ((256, 1024), (256, 1024)),
       "w2_gate": ((1024, FF_SHARD), (1024, FF_PAD)), "w2_up": ((1024, FF_SHARD), (1024, FF_PAD)),
       "w2_down": ((FF_SHARD, 1024), (FF_PAD, 1024)), "w_ple_gate": ((256, 1024), (256, 1024)),
       "w_ple_proj": ((256, 256), (256, 256))}
BIG_NAMES = tuple(BIG)
GAINS = (("g_ffn1_pre", 1024), ("g_ffn1_post", 1024), ("g_mix_pre", 1024), ("g_mla_q", 384),
         ("g_mla_kv", 256), ("g_mix_post", 1024), ("g_ffn2_pre", 1024), ("g_ffn2_post", 1024),
         ("g_ple_pre", 1024), ("g_ple_post", 1024))
WEIGHT_ORDER = ("g_ffn1_pre", "w1_gate", "w1_up", "w1_down", "g_ffn1_post", "g_mix_pre", "w_in", "g_mla_q",
                "w_mla_uq", "g_mla_kv", "w_mla_ukv", "w_conv", "w_out", "g_mix_post", "g_ffn2_pre", "w2_gate",
                "w2_up", "w2_down", "g_ffn2_post", "g_ple_pre", "w_ple_gate", "w_ple_proj", "g_ple_post")

_pcall = pl.pallas_call


def _params(sem=None):
    return pltpu.CompilerParams(dimension_semantics=sem, vmem_limit_bytes=VMEM_LIMIT)


def _dot(a, b, dims=None):
    a, b = a.astype(BF16), b.astype(BF16)
    if dims is None:
        return jnp.dot(a, b, preferred_element_type=F32)
    return lax.dot_general(a, b, dims, preferred_element_type=F32)


def _rstd(v):
    return lax.rsqrt(jnp.mean(v * v, axis=-1, keepdims=True) + EPS)


def _sigmoid(v):
    return 0.5 * jnp.tanh(0.5 * v) + 0.5


def _row_block(n, want):
    b = min(n, want)
    while n % b:
        b //= 2
    return b


def _spec(block, index, lead=None):
    if not lead:
        return pl.BlockSpec(block, index)
    lead = tuple(lead)
    return pl.BlockSpec((None,) * len(lead) + tuple(block), lambda *g: lead + tuple(index(*g)))


def _arr(op):
    return op[0] if isinstance(op, tuple) else op


def _lead(op):
    return tuple(op[1:]) if isinstance(op, tuple) else ()


def _mm(pairs, mode, out_dtype, name, bm=512, bn=512):
    a0, b0 = _arr(pairs[0][0]), _arr(pairs[0][1])
    if mode == "nn":
        M, N = a0.shape[-2], b0.shape[-1]
    elif mode == "nt":
        M, N = a0.shape[-2], b0.shape[-2]
    else:
        M, N = a0.shape[-1], b0.shape[-1]
    bm, bn = _row_block(M, bm), _row_block(N, bn)
    n_pairs = len(pairs)
    dims = {"nn": None, "nt": NT, "tn": TN}[mode]

    def body(*refs):
        acc = None
        for t in range(n_pairs):
            part = _dot(refs[2 * t][...], refs[2 * t + 1][...], dims)
            acc = part if acc is None else acc + part
        refs[-1][...] = acc.astype(refs[-1].dtype)

    in_specs, ops = [], []
    for a, b in pairs:
        sa, sb = _arr(a).shape, _arr(b).shape
        if mode == "nn":
            in_specs += [_spec((bm, sa[-1]), lambda j, i: (i, 0), _lead(a)),
                         _spec((sb[-2], bn), lambda j, i: (0, j), _lead(b))]
        elif mode == "nt":
            in_specs += [_spec((bm, sa[-1]), lambda j, i: (i, 0), _lead(a)),
                         _spec((bn, sb[-1]), lambda j, i: (j, 0), _lead(b))]
        else:
            in_specs += [_spec((sa[-2], bm), lambda j, i: (0, i), _lead(a)),
                         _spec((sb[-2], bn), lambda j, i: (0, j), _lead(b))]
        ops += [_arr(a), _arr(b)]
    return _pcall(body, name=name, grid=(N // bn, M // bm), in_specs=in_specs,
                  out_specs=pl.BlockSpec((bm, bn), lambda j, i: (i, j)),
                  out_shape=jax.ShapeDtypeStruct((M, N), out_dtype),
                  compiler_params=_params(("parallel", "parallel")))(*ops)


def _dw_shards(a, b, axis, name):
    S = a.shape[0]
    if axis == 1:
        Kp, Np = a.shape[1], b.shape[1] // N_CHIPS
        bm = _row_block(Kp, 512)
        grid = (N_CHIPS, Kp // bm)
        a_spec = pl.BlockSpec((S, bm), lambda j, i: (0, i))
        b_spec = pl.BlockSpec((S, Np), lambda j, i: (0, j))
        o_spec = pl.BlockSpec((None, bm, Np), lambda j, i: (j, i, 0))
    else:
        Kp, Np = a.shape[1] // N_CHIPS, b.shape[1]
        bn = _row_block(Np, 512)
        grid = (N_CHIPS, Np // bn)
        a_spec = pl.BlockSpec((S, Kp), lambda j, i: (0, j))
        b_spec = pl.BlockSpec((S, bn), lambda j, i: (0, i))
        o_spec = pl.BlockSpec((None, Kp, bn), lambda j, i: (j, 0, i))

    def body(a_ref, b_ref, o_ref):
        o_ref[...] = _dot(a_ref[...], b_ref[...], TN).astype(BF16)

    return _pcall(body, name=name, grid=grid, in_specs=[a_spec, b_spec], out_specs=o_spec,
                  out_shape=jax.ShapeDtypeStruct((N_CHIPS, Kp, Np), BF16),
                  compiler_params=_params(("parallel", "parallel")))(a, b)


def _ew(fn, ins, outs, name, br=256):
    S = max(a.shape[0] for a in ins)
    br = _row_block(S, br)
    n_in = len(ins)

    def body(*refs):
        res = fn(*[r[...] for r in refs[:n_in]])
        if not isinstance(res, tuple):
            res = (res,)
        for r, v in zip(refs[n_in:], res):
            r[...] = v.astype(r.dtype)

    in_specs = [pl.BlockSpec((br, a.shape[1]), lambda i: (i, 0)) if a.shape[0] == S and S > 1
                else pl.BlockSpec(a.shape, lambda i: (0, 0)) for a in ins]
    out = _pcall(body, name=name, grid=(S // br,), in_specs=in_specs,
                 out_specs=tuple(pl.BlockSpec((br, w), lambda i: (i, 0)) for w, _ in outs),
                 out_shape=tuple(jax.ShapeDtypeStruct((S, w), dt) for w, dt in outs),
                 compiler_params=_params(("parallel",)))(*ins)
    return out if len(outs) > 1 else out[0]


def _norm_fwd(x, g, name):
    return _ew(lambda xv, gv: xv * _rstd(xv) * gv, [x, g], [(x.shape[1], BF16)], name, br=512)


def _resid_norm(x, y, g, alpha, name):
    return _ew(lambda xv, yv, gv: xv + alpha * (yv * _rstd(yv) * gv), [x, y, g], [(x.shape[1], F32)], name)


def _norm_bwd(xin, g, dy, alpha, resid, out_dtype, name):
    S, W = xin.shape
    br = _row_block(S, 256)
    has_res = resid is not None

    def body(*refs):
        x_ref, g_ref, dy_ref = refs[:3]
        dx_ref, dg_ref = refs[-2:]
        xv, dyv = x_ref[...], dy_ref[...] * alpha
        r = _rstd(xv)
        xh = xv * r
        u = dyv * g_ref[...]
        dx = r * (u - xh * jnp.mean(u * xh, axis=-1, keepdims=True))
        if has_res:
            dx = dx + refs[3][...]
        dx_ref[...] = dx.astype(dx_ref.dtype)
        part = jnp.sum(dyv * xh, axis=0, keepdims=True)

        @pl.when(pl.program_id(0) == 0)
        def _():
            dg_ref[...] = part

        @pl.when(pl.program_id(0) > 0)
        def _():
            dg_ref[...] += part

    row = pl.BlockSpec((br, W), lambda i: (i, 0))
    vec = pl.BlockSpec((1, W), lambda i: (0, 0))
    ops = [xin, g, dy] + ([resid] if has_res else [])
    return _pcall(body, name=name, grid=(S // br,), in_specs=[row, vec, row] + ([row] if has_res else []),
                  out_specs=(row, vec),
                  out_shape=(jax.ShapeDtypeStruct((S, W), out_dtype), jax.ShapeDtypeStruct((1, W), F32)),
                  compiler_params=_params(("arbitrary",)))(*ops)


def _ffn_gate_up(h, wg, wu, name):
    S, K = h.shape
    bm, bn = _row_block(S, 1024), FF_PAD
    per = FF_PAD // bn

    def body(h_ref, wg_ref, wu_ref, g_ref, u_ref, a_ref):
        hv = h_ref[...]
        g = _dot(hv, wg_ref[...])
        u = _dot(hv, wu_ref[...])
        g_ref[...] = g.astype(BF16)
        u_ref[...] = u.astype(BF16)
        a_ref[...] = (g * _sigmoid(g) * u).astype(BF16)

    blk = pl.BlockSpec((bm, bn), lambda n, i: (i, n))
    wsp = pl.BlockSpec((None, K, bn), lambda n, i: (n // per, 0, n % per))
    return _pcall(body, name=name, grid=(D_FF_PAD // bn, S // bm),
                  in_specs=[pl.BlockSpec((bm, K), lambda n, i: (i, 0)), wsp, wsp],
                  out_specs=(blk, blk, blk), out_shape=(jax.ShapeDtypeStruct((S, D_FF_PAD), BF16),) * 3,
                  compiler_params=_params(("parallel", "parallel")))(h, wg, wu)


def _ffn_bwd_mid(dy, wd, g, u, name):
    S, D = dy.shape
    F = g.shape[1]
    bm, bn = _row_block(S, 1024), 256

    def body(dy_ref, wd_ref, g_ref, u_ref, dg_ref, du_ref):
        da = _dot(dy_ref[...], wd_ref[...], NT)
        gv, uv = g_ref[...].astype(F32), u_ref[...].astype(F32)
        s = _sigmoid(gv)
        dg_ref[...] = (da * uv * (s * (1.0 + gv * (1.0 - s)))).astype(BF16)
        du_ref[...] = (da * (gv * s)).astype(BF16)

    blk = pl.BlockSpec((bm, bn), lambda j, i: (i, j))
    return _pcall(body, name=name, grid=(F // bn, S // bm),
                  in_specs=[pl.BlockSpec((bm, D), lambda j, i: (i, 0)),
                            pl.BlockSpec((bn, D), lambda j, i: (j, 0)), blk, blk],
                  out_specs=(blk, blk), out_shape=(jax.ShapeDtypeStruct((S, F), BF16),) * 2,
                  compiler_params=_params(("parallel", "parallel")))(dy, wd, g, u)


def _ffn_dh(dg, du, wg, wu, name):
    S = dg.shape[0]
    D = wg.shape[1]
    bm, bn = _row_block(S, 1024), D

    def body(dg_ref, du_ref, wg_ref, wu_ref, o_ref, acc):
        j = pl.program_id(2)
        part = _dot(dg_ref[...], wg_ref[...], NT) + _dot(du_ref[...], wu_ref[...], NT)

        @pl.when(j == 0)
        def _():
            acc[...] = part

        @pl.when(j > 0)
        def _():
            acc[...] += part

        @pl.when(j == N_CHIPS - 1)
        def _():
            o_ref[...] = acc[...]

    asp = pl.BlockSpec((bm, FF_PAD), lambda n, i, j: (i, j))
    wsp = pl.BlockSpec((None, bn, FF_PAD), lambda n, i, j: (j, n, 0))
    return _pcall(body, name=name, grid=(D // bn, S // bm, N_CHIPS), in_specs=[asp, asp, wsp, wsp],
                  out_specs=pl.BlockSpec((bm, bn), lambda n, i, j: (i, n)),
                  out_shape=jax.ShapeDtypeStruct((S, D), F32), scratch_shapes=[pltpu.VMEM((bm, bn), F32)],
                  compiler_params=_params(("parallel", "parallel", "arbitrary")))(dg, du, wg, wu)


def _ple_proj(p16, w):
    S, K = p16.shape
    bm = _row_block(S, 1024)

    def body(p_ref, w_ref, o_ref):
        o_ref[...] = _dot(p_ref[...], w_ref[...])

    return _pcall(body, name="ple_proj", grid=(N_CHIPS, S // bm),
                  in_specs=[pl.BlockSpec((bm, K), lambda j, i: (i, 0)),
                            pl.BlockSpec((None, K, 256), lambda j, i: (j, 0, 0))],
                  out_specs=pl.BlockSpec((bm, 256), lambda j, i: (i, j)),
                  out_shape=jax.ShapeDtypeStruct((S, N_CHIPS * 256), F32),
                  compiler_params=_params(("parallel", "parallel")))(p16, w)


def _loss_and_grad(xf, tgt):
    S, W = xf.shape
    br = _row_block(S, 256)

    def body(x_ref, t_ref, l_ref, d_ref):
        d = x_ref[...] - t_ref[...]
        d_ref[...] = d * (1.0 / W)
        part = 0.5 * jnp.sum(jnp.sum(d * d, axis=-1, keepdims=True) * (1.0 / W), axis=0, keepdims=True)

        @pl.when(pl.program_id(0) == 0)
        def _():
            l_ref[...] = part

        @pl.when(pl.program_id(0) > 0)
        def _():
            l_ref[...] += part

    row = pl.BlockSpec((br, W), lambda i: (i, 0))
    return _pcall(body, name="loss", grid=(S // br,), in_specs=[row, row],
                  out_specs=(pl.BlockSpec((1, 1), lambda i: (0, 0)), row),
                  out_shape=(jax.ShapeDtypeStruct((1, 1), F32), jax.ShapeDtypeStruct((S, W), F32)),
                  compiler_params=_params(("arbitrary",)))(xf, tgt)


def _adamw(w, g, m, v, name):
    L, R, C = w.shape
    Cp = g.shape[2]
    br = _row_block(R, 256) if R % 8 == 0 else R

    def body(w_ref, g_ref, m_ref, v_ref, go_ref, d_ref, nm_ref, nv_ref):
        gv = g_ref[...][:, :C]
        nm = ADAM_B1 * m_ref[...] + (1.0 - ADAM_B1) * gv
        nv = ADAM_B2 * v_ref[...] + (1.0 - ADAM_B2) * (gv * gv)
        m_hat = nm / (1.0 - ADAM_B1 ** ADAM_STEP)
        v_hat = nv / (1.0 - ADAM_B2 ** ADAM_STEP)
        go_ref[...] = gv
        d_ref[...] = -ADAM_LR * (m_hat / (jnp.sqrt(v_hat) + ADAM_EPS) + ADAM_WD * w_ref[...])
        nm_ref[...] = nm
        nv_ref[...] = nv

    blk = pl.BlockSpec((None, br, C), lambda l, i: (l, i, 0))
    gblk = pl.BlockSpec((None, br, Cp), lambda l, i: (l, i, 0))
    return _pcall(body, name=name, grid=(L, R // br), in_specs=[blk, gblk, blk, blk], out_specs=(blk,) * 4,
                  out_shape=(jax.ShapeDtypeStruct((L, R, C), F32),) * 4,
                  compiler_params=_params(("parallel", "parallel")))(w, g, m, v)


def _rope_tables(pos, inv_pat, sign_pat):
    def fn(p, iv, sg):
        ang = p.astype(F32) * iv
        return jnp.cos(ang), jnp.sin(ang) * sg

    return _ew(fn, [pos, inv_pat, sign_pat], [(LANES, F32)] * 2, "rope_tables")


def _swap_halves_of_rope(v):
    W = v.shape[1]
    lane = lax.broadcasted_iota(jnp.int32, (1, W), 1) % LANES
    return jnp.where((lane >= 64) & (lane < 80), pltpu.roll(v, W - ROPE_HALF, 1),
                     jnp.where((lane >= 80) & (lane < 96), pltpu.roll(v, ROPE_HALF, 1), 0.0))


def _tile_lanes(v, n):
    return jnp.concatenate([v] * n, axis=1)


def _mla_prep(qf, kv, z, cp, sp):
    S = qf.shape[0]
    br = _row_block(S, 256)
    W = MLA_HEADS * LANES

    def body(q_ref, k_ref, r_ref, c_ref, s_ref, qo_ref, ko_ref):
        c, s = c_ref[...], s_ref[...]
        q = q_ref[...]
        qo_ref[...] = (q * _tile_lanes(c, MLA_HEADS) + _swap_halves_of_rope(q) * _tile_lanes(s, MLA_HEADS)).astype(BF16)
        r = r_ref[...]
        lane = lax.broadcasted_iota(jnp.int32, (1, LANES), 1)
        kr = jnp.where((lane >= 64) & (lane < 96), r * c + _swap_halves_of_rope(r) * s, 0.0)
        ko_ref[...] = (k_ref[...] + _tile_lanes(kr, MLA_HEADS)).astype(BF16)

    wide = pl.BlockSpec((br, W), lambda i: (i, 0))
    one = pl.BlockSpec((br, LANES), lambda i: (i, 0))
    return _pcall(body, name="mla_prep", grid=(S // br,),
                  in_specs=[wide, wide, pl.BlockSpec((br, LANES), lambda i: (i, Z_KR // LANES)), one, one],
                  out_specs=(wide, wide), out_shape=(jax.ShapeDtypeStruct((S, W), BF16),) * 2,
                  compiler_params=_params(("parallel",)))(qf, kv, z, cp, sp)


def _mla_prep_bwd(dq, dk, cp, sp):
    S, W = dq.shape
    br = _row_block(S, 256)

    def body(dq_ref, dk_ref, c_ref, s_ref, dqo_ref, dr_ref):
        c, s = c_ref[...], s_ref[...]
        d = dq_ref[...]
        dqo_ref[...] = (d * _tile_lanes(c, MLA_HEADS) + _swap_halves_of_rope(d * _tile_lanes(s, MLA_HEADS))).astype(BF16)
        dkv = dk_ref[...]
        tot = dkv[:, 0:LANES]
        for h in range(1, MLA_HEADS):
            tot = tot + dkv[:, h * LANES:(h + 1) * LANES]
        lane = lax.broadcasted_iota(jnp.int32, (1, LANES), 1)
        tot = jnp.where((lane >= 64) & (lane < 96), tot, 0.0)
        dr_ref[...] = tot * c + _swap_halves_of_rope(tot * s)

    wide = pl.BlockSpec((br, W), lambda i: (i, 0))
    one = pl.BlockSpec((br, LANES), lambda i: (i, 0))
    return _pcall(body, name="mla_prep_bwd", grid=(S // br,), in_specs=[wide, wide, one, one], out_specs=(wide, one),
                  out_shape=(jax.ShapeDtypeStruct((S, W), BF16), jax.ShapeDtypeStruct((S, LANES), F32)),
                  compiler_params=_params(("parallel",)))(dq, dk, cp, sp)


def _shift_down(v, k, row):
    return jnp.where(row >= k, pltpu.roll(v, k, 0), 0.0)


def _shift_up(v, k, row):
    n = v.shape[0]
    return jnp.where(row < n - k, pltpu.roll(v, n - k, 0), 0.0)


def _conv_fwd(z, w):
    S = z.shape[0]
    c0 = Z_CV // LANES

    def body(b_ref, c_ref, h_ref, w_ref, y_ref):
        u = c_ref[...] * h_ref[...]
        row = lax.broadcasted_iota(jnp.int32, u.shape, 0)
        wv = w_ref[...]
        conv = wv[0:1] * _shift_down(u, 2, row) + wv[1:2] * _shift_down(u, 1, row) + wv[2:3] * u
        y_ref[...] = b_ref[...] * conv

    def col(k):
        return pl.BlockSpec((S, LANES), lambda j: (0, c0 + 2 * k + j))

    return _pcall(body, name="conv_fwd", grid=(2,),
                  in_specs=[col(0), col(1), col(2), pl.BlockSpec((3, LANES), lambda j: (0, j))],
                  out_specs=pl.BlockSpec((S, LANES), lambda j: (0, j)),
                  out_shape=jax.ShapeDtypeStruct((S, 256), F32), compiler_params=_params(("parallel",)))(z, z, z, w)


def _conv_bwd(z, w, dc_all):
    S = z.shape[0]
    c0 = Z_CV // LANES

    def body(b_ref, c_ref, h_ref, w_ref, dy_ref, db_ref, dc_ref, dh_ref, dw_ref):
        cv, hv, bv, dyv = c_ref[...], h_ref[...], b_ref[...], dy_ref[...]
        u = cv * hv
        row = lax.broadcasted_iota(jnp.int32, u.shape, 0)
        wv = w_ref[...]
        u1, u2 = _shift_down(u, 1, row), _shift_down(u, 2, row)
        conv = wv[0:1] * u2 + wv[1:2] * u1 + wv[2:3] * u
        db_ref[...] = dyv * conv
        dconv = dyv * bv
        du = wv[2:3] * dconv + wv[1:2] * _shift_up(dconv, 1, row) + wv[0:1] * _shift_up(dconv, 2, row)
        dc_ref[...] = du * hv
        dh_ref[...] = du * cv
        dw_ref[0:1, :] = jnp.sum(dconv * u2, axis=0, keepdims=True)
        dw_ref[1:2, :] = jnp.sum(dconv * u1, axis=0, keepdims=True)
        dw_ref[2:3, :] = jnp.sum(dconv * u, axis=0, keepdims=True)

    def col(k):
        return pl.BlockSpec((S, LANES), lambda j: (0, c0 + 2 * k + j))

    wsp = pl.BlockSpec((3, LANES), lambda j: (0, j))
    osp = pl.BlockSpec((S, LANES), lambda j: (0, j))
    db, dc, dh, dw = _pcall(
        body, name="conv_bwd", grid=(2,),
        in_specs=[col(0), col(1), col(2), wsp, pl.BlockSpec((S, LANES), lambda j: (0, 6 + j))],
        out_specs=(osp, osp, osp, wsp),
        out_shape=(jax.ShapeDtypeStruct((S, 256), F32),) * 3 + (jax.ShapeDtypeStruct((3, 256), F32),),
        compiler_params=_params(("parallel",)))(z, z, z, w, dc_all)
    return db, dc, dh, dw


def _half_masks():
    lane = lax.broadcasted_iota(jnp.int32, (1, LANES), 1)
    return lane < HEAD_DIM, lane >= HEAD_DIM


def _pair(v, p):
    return v[:, p * LANES:(p + 1) * LANES]


def _keep(mask, v):
    return jnp.where(mask, v, jnp.zeros_like(v))


def _lane_col(vals):
    lane = lax.broadcasted_iota(jnp.int32, (1, LANES), 1)
    out = jnp.zeros((vals[0].shape[0], LANES), F32)
    for h, v in enumerate(vals):
        out = jnp.where(lane == h, v, out)
    return out


def _mla_fwd(q, k, kv):
    S = q.shape[0]
    H = MLA_HEADS
    bq, bk = _row_block(S, MLA_BQ), _row_block(S, MLA_BK)
    per = bq // bk

    def body(q_ref, k_ref, v_ref, o_ref, lse_ref):
        i = pl.program_id(0)
        lo, hi = _half_masks()
        qb = q_ref[...]

        def block(j, carry, ok):
            ms, ls, accs = carry
            off = pl.multiple_of(j * bk, bk)
            kb = k_ref[pl.ds(off, bk), :]
            vb = v_ref[pl.ds(off, bk), :].astype(BF16)
            ss = [_dot(_pair(qb, h), _pair(kb, h), NT) * MLA_SCALE for h in range(H)]
            if ok is not None:
                ss = [jnp.where(ok, s, NEG_INF) for s in ss]
            ms2 = [jnp.maximum(ms[h], jnp.max(ss[h], axis=-1, keepdims=True)) for h in range(H)]
            ps = [jnp.exp(ss[h] - ms2[h]) for h in range(H)]
            al = [jnp.exp(ms[h] - ms2[h]) for h in range(H)]
            ls2 = [al[h] * ls[h] + jnp.sum(ps[h], axis=-1, keepdims=True) for h in range(H)]
            pvs = [_dot(ps[h], _pair(vb, h // 2)) for h in range(H)]
            accs2 = []
            for p in range(H // 2):
                scale = jnp.where(lo, al[2 * p], al[2 * p + 1])
                accs2.append(scale * accs[p] + jnp.where(lo, pvs[2 * p], pvs[2 * p + 1]))
            return tuple(ms2), tuple(ls2), tuple(accs2)

        init = (tuple(jnp.full((bq, 1), NEG_INF, F32) for _ in range(H)),
                tuple(jnp.zeros((bq, 1), F32) for _ in range(H)),
                tuple(jnp.zeros((bq, LANES), F32) for _ in range(H // 2)))
        carry = lax.fori_loop(0, i * per, lambda j, c: block(j, c, None), init)
        row = lax.broadcasted_iota(jnp.int32, (bq, bk), 0)
        col = lax.broadcasted_iota(jnp.int32, (bq, bk), 1)
        for t in range(per):
            carry = block(i * per + t, carry, col + t * bk <= row)
        ms, ls, accs = carry
        o_ref[...] = jnp.concatenate(
            [accs[p] / jnp.where(lo, ls[2 * p], ls[2 * p + 1]) for p in range(H // 2)], axis=1)
        lse_ref[...] = _lane_col([ms[h] + jnp.log(ls[h]) for h in range(H)])

    return _pcall(body, name="mla_fwd", grid=(S // bq,),
                  in_specs=[pl.BlockSpec((bq, H * LANES), lambda i: (i, 0)),
                            pl.BlockSpec((S, H * LANES), lambda i: (0, 0)),
                            pl.BlockSpec((S, H * HEAD_DIM), lambda i: (0, 2))],
                  out_specs=(pl.BlockSpec((bq, H * HEAD_DIM), lambda i: (i, 0)),
                             pl.BlockSpec((bq, LANES), lambda i: (i, 0))),
                  out_shape=(jax.ShapeDtypeStruct((S, H * HEAD_DIM), F32), jax.ShapeDtypeStruct((S, LANES), F32)),
                  compiler_params=_params(("parallel",)))(q, k, kv)


def _mla_bwd(q, k, kv, o, lse, dc_all):
    S = q.shape[0]
    H = MLA_HEADS
    bq, bk = _row_block(S, MLA_BQ), _row_block(S, MLA_BK)
    per = bq // bk

    def body(q_ref, k_ref, v_ref, o_ref, lse_ref, doa_ref, dob_ref, dq_ref, dk_ref, dv_ref):
        i = pl.program_id(0)

        @pl.when(i == 0)
        def _():
            dk_ref[...] = jnp.zeros_like(dk_ref)
            dv_ref[...] = jnp.zeros_like(dv_ref)

        lo, hi = _half_masks()
        qb = q_ref[...]
        dob = jnp.concatenate([doa_ref[...], dob_ref[...]], axis=1)
        prod = dob * o_ref[...]
        lse_v = lse_ref[...]
        do16 = dob.astype(BF16)
        dom, deltas, lses = [], [], []
        for h in range(H):
            mk = lo if h % 2 == 0 else hi
            dom.append(_keep(mk, _pair(do16, h // 2)))
            deltas.append(jnp.sum(_keep(mk, _pair(prod, h // 2)), axis=-1, keepdims=True))
            lses.append(lse_v[:, h:h + 1])

        def block(j, dqs, ok):
            off = pl.multiple_of(j * bk, bk)
            kb = k_ref[pl.ds(off, bk), :]
            vb = v_ref[pl.ds(off, bk), :].astype(BF16)
            ss = [_dot(_pair(qb, h), _pair(kb, h), NT) * MLA_SCALE for h in range(H)]
            dps = [_dot(dom[h], _pair(vb, h // 2), NT) for h in range(H)]
            ps = [jnp.exp(ss[h] - lses[h]) for h in range(H)]
            if ok is not None:
                ps = [jnp.where(ok, p, 0.0) for p in ps]
            ds16 = [(ps[h] * (dps[h] - deltas[h]) * MLA_SCALE).astype(BF16) for h in range(H)]
            p16 = [p.astype(BF16) for p in ps]
            dks = [_dot(ds16[h], _pair(qb, h), TN) for h in range(H)]
            dvps = [_dot(p16[h], dom[h], TN) for h in range(H)]
            dqs2 = [dqs[h] + _dot(ds16[h], _pair(kb, h)) for h in range(H)]
            dk_ref[pl.ds(off, bk), :] += jnp.concatenate(dks, axis=1)
            dv_ref[pl.ds(off, bk), :] += jnp.concatenate(
                [dvps[2 * p] + dvps[2 * p + 1] for p in range(H // 2)], axis=1)
            return tuple(dqs2)

        dqs = lax.fori_loop(0, i * per, lambda j, c: block(j, c, None),
                            tuple(jnp.zeros((bq, LANES), F32) for _ in range(H)))
        row = lax.broadcasted_iota(jnp.int32, (bq, bk), 0)
        col = lax.broadcasted_iota(jnp.int32, (bq, bk), 1)
        for t in range(per):
            dqs = block(i * per + t, dqs, col + t * bk <= row)
        dq_ref[...] = jnp.concatenate(dqs, axis=1)

    wide = pl.BlockSpec((bq, H * LANES), lambda i: (i, 0))
    full = pl.BlockSpec((S, H * LANES), lambda i: (0, 0))
    return _pcall(body, name="mla_bwd", grid=(S // bq,),
                  in_specs=[wide, full, pl.BlockSpec((S, H * HEAD_DIM), lambda i: (0, 2)),
                            pl.BlockSpec((bq, H * HEAD_DIM), lambda i: (i, 0)),
                            pl.BlockSpec((bq, LANES), lambda i: (i, 0)),
                            pl.BlockSpec((bq, 256), lambda i: (i, 1)), pl.BlockSpec((bq, 256), lambda i: (i, 2))],
                  out_specs=(wide, full, pl.BlockSpec((S, H * HEAD_DIM), lambda i: (0, 0))),
                  out_shape=(jax.ShapeDtypeStruct((S, H * LANES), F32), jax.ShapeDtypeStruct((S, H * LANES), F32),
                             jax.ShapeDtypeStruct((S, H * HEAD_DIM), F32)),
                  compiler_params=_params(("arbitrary",)))(q, k, kv, o, lse, dc_all, dc_all)


def _dot_exact(x, tri):
    h1 = x.astype(BF16)
    h2 = (x - h1.astype(F32)).astype(BF16)
    return _dot(h1, tri) + _dot(h2, tri)


def _softplus(z):
    return jnp.maximum(z, 0.0) + jnp.log(1.0 + jnp.exp(-jnp.abs(z)))


def _sb_fwd(z):
    S = z.shape[0]
    H = SB_HEADS
    bq, bk = _row_block(S, SB_BQ), _row_block(S, SB_BK)
    per = bq // bk

    def body(q_ref, k_ref, v_ref, o_ref, t_ref):
        i = pl.program_id(0)
        lo, hi = _half_masks()
        q16 = (q_ref[...] * SB_SCALE).astype(BF16)
        qm = [_keep(lo if h % 2 == 0 else hi, _pair(q16, h // 2)) for h in range(H)]
        rr = lax.broadcasted_iota(jnp.int32, (bk, bk), 0)
        cc = lax.broadcasted_iota(jnp.int32, (bk, bk), 1)
        later = (rr > cc).astype(BF16)

        def block(j, carry, mask):
            tails, accs = carry
            off = pl.multiple_of(j * bk, bk)
            kb = k_ref[pl.ds(off, bk), :].astype(BF16)
            vb = v_ref[pl.ds(off, bk), :].astype(BF16)
            zs = [_dot(qm[h], _pair(kb, h // 2), NT) for h in range(H)]
            sps = [_softplus(z) for z in zs]
            lnots = [-sp if mask is None else jnp.where(mask, -sp, 0.0) for sp in sps]
            sums = [_dot_exact(lnot, later) for lnot in lnots]
            ws = []
            for h in range(H):
                w = jnp.exp((zs[h] - sps[h]) + (tails[h] + sums[h]))
                ws.append((w if mask is None else jnp.where(mask, w, 0.0)).astype(BF16))
            pvs = [_dot(ws[h], _pair(vb, h // 2)) for h in range(H)]
            accs2 = [accs[p] + jnp.where(lo, pvs[2 * p], pvs[2 * p + 1]) for p in range(H // 2)]
            tails2 = [tails[h] + jnp.sum(lnots[h], axis=-1, keepdims=True) for h in range(H)]
            return tuple(tails2), tuple(accs2)

        carry = (tuple(jnp.zeros((bq, 1), F32) for _ in range(H)),
                 tuple(jnp.zeros((bq, LANES), F32) for _ in range(H // 2)))
        row = lax.broadcasted_iota(jnp.int32, (bq, bk), 0)
        col = lax.broadcasted_iota(jnp.int32, (bq, bk), 1)
        for t in range(per):
            carry = block(i * per + per - 1 - t, carry, col + (per - 1 - t) * bk < row)
        tails, accs = lax.fori_loop(0, i * per, lambda t, c: block(i * per - 1 - t, c, None), carry)
        o_ref[...] = jnp.concatenate(accs, axis=1)
        t_ref[...] = _lane_col(tails)

    return _pcall(body, name="sb_fwd", grid=(S // bq,),
                  in_specs=[pl.BlockSpec((bq, 256), lambda i: (i, 0)), pl.BlockSpec((S, 256), lambda i: (0, 1)),
                            pl.BlockSpec((S, 256), lambda i: (0, 2))],
                  out_specs=(pl.BlockSpec((bq, 256), lambda i: (i, 0)), pl.BlockSpec((bq, LANES), lambda i: (i, 0))),
                  out_shape=(jax.ShapeDtypeStruct((S, 256), F32), jax.ShapeDtypeStruct((S, LANES), F32)),
                  compiler_params=_params(("parallel",)))(z, z, z)


def _sb_bwd(z, tot, dc_all):
    S = z.shape[0]
    H = SB_HEADS
    bq, bk = _row_block(S, SB_BQ), _row_block(S, SB_BK)
    per = bq // bk

    def body(q_ref, k_ref, v_ref, t_ref, do_ref, dq_ref, dk_ref, dv_ref):
        i = pl.program_id(0)

        @pl.when(i == 0)
        def _():
            dk_ref[...] = jnp.zeros_like(dk_ref)
            dv_ref[...] = jnp.zeros_like(dv_ref)

        lo, hi = _half_masks()
        q16 = (q_ref[...] * SB_SCALE).astype(BF16)
        do16 = do_ref[...].astype(BF16)
        tot_v = t_ref[...]
        masks = [lo if h % 2 == 0 else hi for h in range(H)]
        qm = [_keep(masks[h], _pair(q16, h // 2)) for h in range(H)]
        dom = [_keep(masks[h], _pair(do16, h // 2)) for h in range(H)]
        tots = [tot_v[:, h:h + 1] for h in range(H)]
        rr = lax.broadcasted_iota(jnp.int32, (bk, bk), 0)
        cc = lax.broadcasted_iota(jnp.int32, (bk, bk), 1)
        upto = (rr <= cc).astype(BF16)
        before = (rr < cc).astype(BF16)

        def block(j, carry, mask):
            heads, eheads, dqs = carry
            off = pl.multiple_of(j * bk, bk)
            kb = k_ref[pl.ds(off, bk), :].astype(BF16)
            vb = v_ref[pl.ds(off, bk), :].astype(BF16)
            zs = [_dot(qm[h], _pair(kb, h // 2), NT) for h in range(H)]
            dws = [_dot(dom[h], _pair(vb, h // 2), NT) for h in range(H)]
            sps = [_softplus(z) for z in zs]
            lnots = [-sp if mask is None else jnp.where(mask, -sp, 0.0) for sp in sps]
            pres = [_dot_exact(lnot, upto) for lnot in lnots]
            lsigs = [zs[h] - sps[h] for h in range(H)]
            ws = [jnp.exp(lsigs[h] + (tots[h] - (heads[h] + pres[h]))) for h in range(H)]
            if mask is not None:
                ws = [jnp.where(mask, w, 0.0) for w in ws]
            es = [ws[h] * dws[h] for h in range(H)]
            esums = [eheads[h] + _dot_exact(es[h], before) for h in range(H)]
            dz16 = []
            for h in range(H):
                sig = jnp.exp(lsigs[h])
                dz = es[h] * (1.0 - sig) - sig * esums[h]
                dz16.append((dz if mask is None else jnp.where(mask, dz, 0.0)).astype(BF16))
            w16 = [w.astype(BF16) for w in ws]
            dkps = [_dot(dz16[h], qm[h], TN) for h in range(H)]
            dvps = [_dot(w16[h], dom[h], TN) for h in range(H)]
            dqps = [_dot(dz16[h], _pair(kb, h // 2)) for h in range(H)]
            dk_ref[pl.ds(off, bk), :] += jnp.concatenate([dkps[2 * p] + dkps[2 * p + 1] for p in range(H // 2)], axis=1)
            dv_ref[pl.ds(off, bk), :] += jnp.concatenate([dvps[2 * p] + dvps[2 * p + 1] for p in range(H // 2)], axis=1)
            dqs2 = [dqs[p] + jnp.where(lo, dqps[2 * p], dqps[2 * p + 1]) for p in range(H // 2)]
            heads2 = [heads[h] + jnp.sum(lnots[h], axis=-1, keepdims=True) for h in range(H)]
            eheads2 = [eheads[h] + jnp.sum(es[h], axis=-1, keepdims=True) for h in range(H)]
            return tuple(heads2), tuple(eheads2), tuple(dqs2)

        zeros = tuple(jnp.zeros((bq, 1), F32) for _ in range(H))
        init = (zeros, zeros, tuple(jnp.zeros((bq, LANES), F32) for _ in range(H // 2)))
        carry = lax.fori_loop(0, i * per, lambda j, c: block(j, c, None), init)
        row = lax.broadcasted_iota(jnp.int32, (bq, bk), 0)
        col = lax.broadcasted_iota(jnp.int32, (bq, bk), 1)
        for t in range(per):
            carry = block(i * per + t, carry, col + t * bk < row)
        dq_ref[...] = jnp.concatenate(carry[2], axis=1) * SB_SCALE

    blk = pl.BlockSpec((bq, 256), lambda i: (i, 0))
    full = pl.BlockSpec((S, 256), lambda i: (0, 0))
    return _pcall(body, name="sb_bwd", grid=(S // bq,),
                  in_specs=[blk, pl.BlockSpec((S, 256), lambda i: (0, 1)), pl.BlockSpec((S, 256), lambda i: (0, 2)),
                            pl.BlockSpec((bq, LANES), lambda i: (i, 0)), blk],
                  out_specs=(blk, full, full), out_shape=(jax.ShapeDtypeStruct((S, 256), F32),) * 3,
                  compiler_params=_params(("arbitrary",)))(z, z, z, tot, dc_all)


_HBM = pl.BlockSpec(memory_space=pltpu.HBM)
_SEM = pl.BlockSpec(memory_space=pltpu.SEMAPHORE)
_ANY = pl.BlockSpec(memory_space=pl.ANY)


def _place():
    x, y, c = lax.axis_index("x"), lax.axis_index("y"), lax.axis_index("c")
    other_chips = ((1 - x, y), (x, 1 - y), (1 - x, 1 - y))
    return x, y, c, other_chips


def _remote(src, dst, send_sems, recv_sems, k, dev):
    return pltpu.make_async_remote_copy(src_ref=src, dst_ref=dst, send_sem=send_sems.at[k], recv_sem=recv_sems.at[k],
                                        device_id=dev, device_id_type=MESH)


def _half_rows(c, n_rows):
    half = n_rows // 2
    return pl.ds(pl.multiple_of(c * half, 16), half)


def _gather_copies(src, land, send_sems, recv_sems):
    x, y, c, chips = _place()
    me = 2 * x + y
    cps = []
    for w in range(len(src)):
        rows = _half_rows(c, src[w].shape[0])
        for k, (cx, cy) in enumerate(chips):
            cps.append(_remote(src[w].at[rows], land[w].at[me, rows], send_sems, recv_sems, 3 * w + k, (cx, cy, c)))
    return cps


def _scatter_copies(src, land, send_sems, recv_sems):
    x, y, c, chips = _place()
    cps = []
    for w in range(len(src)):
        for k, (cx, cy) in enumerate(chips):
            cps.append(_remote(src[w].at[2 * cx + cy], land[w].at[k], send_sems, recv_sems, 3 * w + k, (cx, cy, c)))
    return cps


def _exchange_start(copies, srcs, lands, after, name):
    n = len(srcs)

    def body(*refs):
        src, land = refs[:n], refs[n:2 * n]
        send_sems, recv_sems = refs[2 * n + 1], refs[2 * n + 2]
        token = refs[-1]
        for cp in copies(src, land, send_sems, recv_sems):
            cp.start()
        token[...] = jnp.zeros_like(token)

    thru = tuple(pltpu.HBM(a.shape, a.dtype) for a in list(srcs) + list(lands))
    out = _pcall(body, name=name,
                 out_shape=(pltpu.SemaphoreType.DMA((3 * n,)), pltpu.SemaphoreType.DMA((3 * n,))) + thru
                 + (jax.ShapeDtypeStruct((8, LANES), F32),),
                 in_specs=[_HBM] * (2 * n) + [_ANY],
                 out_specs=(_SEM, _SEM) + (_HBM,) * (2 * n) + (pl.BlockSpec(memory_space=pltpu.VMEM),),
                 input_output_aliases={i: 2 + i for i in range(2 * n)},
                 compiler_params=pltpu.CompilerParams(has_side_effects=pltpu.SideEffectType.DATAFLOW_SIDE_EFFECTING))(
        *[pltpu.with_memory_space_constraint(a, pltpu.HBM) for a in list(srcs) + list(lands)], after)
    return out[0], out[1], list(out[2:2 + n]), list(out[2 + n:2 + 2 * n]), out[-1]


def _exchange_wait(copies, send_sems, recv_sems, srcs, lands, after, name):
    n = len(srcs)

    def body(*refs):
        src, land = refs[:n], refs[n:2 * n]
        ssem, rsem = refs[2 * n], refs[2 * n + 1]
        for cp in copies(src, land, ssem, rsem):
            cp.wait_send()
            cp.wait_recv()

    out = _pcall(body, name=name, out_shape=tuple(pltpu.HBM(a.shape, a.dtype) for a in list(srcs) + list(lands)),
                 in_specs=[_HBM] * (2 * n) + [_SEM, _SEM, _ANY], out_specs=(_HBM,) * (2 * n),
                 input_output_aliases={i: i for i in range(2 * n)},
                 compiler_params=pltpu.CompilerParams(has_side_effects=pltpu.SideEffectType.DATAFLOW_SIDE_EFFECTING))(
        *srcs, *lands, send_sems, recv_sems, after)
    return list(out[:n]), list(out[n:])


def _forward_rows(gathered):
    n = len(gathered)

    def body(*refs):
        src, dst = refs[:n], refs[n:2 * n]
        send_sems, recv_sems = refs[2 * n:]
        x, y, c, chips = _place()
        sib = (x, y, 1 - c)
        cps = []
        for w in range(n):
            K = src[w].shape[1]
            for k, (cx, cy) in enumerate(chips):
                blk = (2 * cx + cy, _half_rows(c, K))
                cps.append(_remote(src[w].at[blk], dst[w].at[blk], send_sems, recv_sems, 3 * w + k, sib))
        for cp in cps:
            cp.start()
        for w in range(n):
            K = src[w].shape[1]
            for k, (cx, cy) in enumerate(chips):
                blk = dst[w].at[2 * cx + cy, _half_rows(1 - c, K)]
                _remote(blk, blk, send_sems, recv_sems, 3 * w + k, sib).wait_recv()
        for cp in cps:
            cp.wait_send()

    return _pcall(body, name="forward_rows", in_specs=[_HBM] * n, out_specs=(_HBM,) * n,
                  out_shape=tuple(jax.ShapeDtypeStruct(g.shape, g.dtype) for g in gathered),
                  input_output_aliases={w: w for w in range(n)},
                  scratch_shapes=[pltpu.SemaphoreType.DMA((3 * n,)), pltpu.SemaphoreType.DMA((3 * n,))])(*gathered)


def _swap_rows(bufs):
    n = len(bufs)

    def body(*refs):
        src, dst = refs[:n], refs[n:2 * n]
        send_sems, recv_sems = refs[2 * n:]
        x, y, c, _ = _place()
        cps = [_remote(src[w].at[:, _half_rows(1 - c, src[w].shape[1])], dst[w], send_sems, recv_sems, w,
                       (x, y, 1 - c)) for w in range(n)]
        for cp in cps:
            cp.start()
        for cp in cps:
            cp.wait()

    return _pcall(body, name="swap_rows", in_specs=[_HBM] * n, out_specs=(_HBM,) * n,
                  out_shape=tuple(jax.ShapeDtypeStruct((b.shape[0], b.shape[1] // 2, b.shape[2]), b.dtype) for b in bufs),
                  scratch_shapes=[pltpu.SemaphoreType.DMA((n,)), pltpu.SemaphoreType.DMA((n,))])(*bufs)


def _join_rows(fins):
    n = len(fins)

    def body(*refs):
        src, dst = refs[:n], refs[n:2 * n]
        send_sems, recv_sems = refs[2 * n:]
        x, y, c, _ = _place()
        sib = (x, y, 1 - c)
        cps = []
        for w in range(n):
            blk = (slice(None), _half_rows(c, src[w].shape[1]))
            cps.append(_remote(src[w].at[blk], dst[w].at[blk], send_sems, recv_sems, w, sib))
        for cp in cps:
            cp.start()
        for w in range(n):
            blk = dst[w].at[:, _half_rows(1 - c, src[w].shape[1])]
            _remote(blk, blk, send_sems, recv_sems, w, sib).wait_recv()
        for cp in cps:
            cp.wait_send()

    return _pcall(body, name="join_rows", in_specs=[_HBM] * n, out_specs=(_HBM,) * n,
                  out_shape=tuple(jax.ShapeDtypeStruct(f.shape, f.dtype) for f in fins),
                  input_output_aliases={w: w for w in range(n)},
                  scratch_shapes=[pltpu.SemaphoreType.DMA((n,)), pltpu.SemaphoreType.DMA((n,))])(*fins)


def _add_rows(buf, got, c_idx):
    _, Kh, N = got.shape
    bk = _row_block(Kh, 256)
    nb = Kh // bk

    def body(c_ref, a_ref, b_ref, o_ref):
        o_ref[...] = (a_ref[...].astype(F32) + b_ref[...].astype(F32)).astype(o_ref.dtype)

    blk = pl.BlockSpec((None, bk, N), lambda j, i, c_ref: (j, i, 0))
    grid_spec = pltpu.PrefetchScalarGridSpec(
        num_scalar_prefetch=1, grid=(N_CHIPS, nb),
        in_specs=[pl.BlockSpec((None, bk, N), lambda j, i, c_ref: (j, c_ref[0] * nb + i, 0)), blk], out_specs=blk)
    return _pcall(body, name="add_rows", grid_spec=grid_spec, out_shape=jax.ShapeDtypeStruct((N_CHIPS, Kh, N), BF16),
                  compiler_params=_params(("parallel", "parallel")))(c_idx, buf, got)


def _add_chip_sums(s1, got, idx, layer, into, after):
    _, Kh, N = s1.shape
    bk = _row_block(Kh, 256)
    nb = Kh // bk

    def body(idx_ref, a_ref, b_ref, *rest):
        acc = a_ref[...].astype(F32)
        for k in range(3):
            acc = acc + b_ref[k].astype(F32)
        rest[-1][...] = acc

    in_specs = [pl.BlockSpec((None, bk, N), lambda i, idx_ref: (idx_ref[0], i, 0)),
                pl.BlockSpec((3, bk, N), lambda i, idx_ref: (0, i, 0))]
    in_specs.append(_ANY)
    ops = [idx, s1, got, after]
    if into is not None:
        in_specs.append(_ANY)
        ops.append(into)
    grid_spec = pltpu.PrefetchScalarGridSpec(
        num_scalar_prefetch=1, grid=(nb,), in_specs=in_specs,
        out_specs=pl.BlockSpec((None, bk, N), lambda i, idx_ref: (layer, idx_ref[1] * nb + i, 0)))
    return _pcall(body, name="add_chip_sums" if into is None else "add_chip_sums_into", grid_spec=grid_spec,
                  out_shape=jax.ShapeDtypeStruct((2, 2 * Kh, N), F32),
                  input_output_aliases={} if into is None else {4: 0},
                  compiler_params=_params(("parallel",)))(*ops)


def _allreduce_small(v):
    R, W = v.shape

    def body(v_ref, o_ref, buf, send_sems, recv_sems):
        x, y, c, _ = _place()
        me = 4 * x + 2 * y + c
        buf[0] = v_ref[...]
        cps = []
        for r in range(1, 8):
            peer = (x if not r & 4 else 1 - x, y if not r & 2 else 1 - y, c if not r & 1 else 1 - c)
            cp = _remote(v_ref, buf.at[r], send_sems, recv_sems, r - 1, peer)
            cp.start()
            cps.append(cp)
        for cp in cps:
            cp.wait()
        acc = buf[me]
        for d in range(1, 8):
            acc = acc + buf[jnp.bitwise_xor(me, d)]
        o_ref[...] = acc

    return _pcall(body, name="allreduce_small", out_shape=jax.ShapeDtypeStruct((R, W), F32),
                  in_specs=[pl.BlockSpec(memory_space=pltpu.VMEM)], out_specs=pl.BlockSpec(memory_space=pltpu.VMEM),
                  scratch_shapes=[pltpu.VMEM((8, R, W), F32), pltpu.SemaphoreType.DMA((7,)),
                                  pltpu.SemaphoreType.DMA((7,))])(v)


def _cat_cols(g):
    return g.transpose(1, 0, 2).reshape(g.shape[1], -1)


def _cut_cols(w):
    K, N = w.shape
    return w.reshape(K, N_CHIPS, N // N_CHIPS).transpose(1, 0, 2)


def _regroup_w_in(w):
    zeros = lambda n: jnp.zeros((w.shape[0], n), w.dtype)
    return jnp.concatenate([w[:, 0:768], w[:, 1152:1408], zeros(64), w[:, 1408:1440], zeros(32),
                            w[:, 768:1152], w[:, 1440:2208]], axis=1)


def _ungroup_w_in(w):
    return jnp.concatenate([w[:, 0:768], w[:, Z_Q:Z_Q + 384], w[:, Z_KV:Z_KV + 256],
                            w[:, Z_KR + 64:Z_KR + 96], w[:, Z_CV:Z_W]], axis=1)


def _regroup_w_uq(w):
    K = w.shape[0]
    return jnp.pad(w.reshape(K, MLA_HEADS, 96), ((0, 0), (0, 0), (0, 32))).reshape(K, MLA_HEADS * LANES)


def _ungroup_w_uq(w):
    K = w.shape[0]
    return w.reshape(K, MLA_HEADS, LANES)[:, :, :96].reshape(K, MLA_HEADS * 96)


def _regroup_w_ukv(w):
    K = w.shape[0]
    t = w.reshape(K, MLA_HEADS, 128)
    return jnp.concatenate([jnp.pad(t[:, :, :64], ((0, 0), (0, 0), (0, 64))).reshape(K, MLA_HEADS * LANES),
                            t[:, :, 64:].reshape(K, MLA_HEADS * HEAD_DIM)], axis=1)


def _ungroup_w_ukv(w):
    K = w.shape[0]
    return jnp.concatenate([w[:, :MLA_HEADS * LANES].reshape(K, MLA_HEADS, LANES)[:, :, :64],
                            w[:, MLA_HEADS * LANES:].reshape(K, MLA_HEADS, 64)], axis=2).reshape(K, 1024)


def _ffn_fwd(x, g_pre, g_post, fetch, tag):
    W = fetch("ffn%sa" % tag, x)
    h = _norm_fwd(x, g_pre, "norm_fwd")
    G, U, A = _ffn_gate_up(h, W["w%s_gate" % tag], W["w%s_up" % tag], "ffn_gate_up")
    W = fetch("ffn%sb" % tag, A)
    Y = _mm([(A, W["w%s_down" % tag])], "nn", F32, "ffn_down", bm=1024)
    return _resid_norm(x, Y, g_post, 0.5, "resid_norm"), dict(x=x, h=h, G=G, U=U, A=A, Y=Y)


def _ffn_bwd(dxo, sv, g_pre, g_post, wg, wu, wd_rows, ready):
    dY, dg_post = _norm_bwd(sv["Y"], g_post, dxo, 0.5, None, BF16, "norm_bwd_post")
    dG, dU = _ffn_bwd_mid(dY, wd_rows, sv["G"], sv["U"], "ffn_bwd_mid")
    token = ready((_dw_shards(sv["h"], dG, 1, "ffn_dw_in"), _dw_shards(sv["h"], dU, 1, "ffn_dw_in"),
                   _dw_shards(sv["A"], dY, 0, "ffn_dw_down")))
    dh = _ffn_dh(dG, dU, wg, wu, "ffn_dh")
    dx, dg_pre = _norm_bwd(sv["x"], _ordered_after(g_pre, token), dh, 1.0, dxo, F32, "norm_bwd_pre")
    return dx, dg_pre, dg_post, token


def _ordered_after(g, token):
    return g if token is None else g + token[0, 0]


GATHER_GROUPS = (("ffn1a", ("w1_gate", "w1_up")), ("ffn1b", ("w1_down",)),
                 ("mix", ("w_in", "w_mla_uq", "w_mla_ukv", "w_out")),
                 ("ffn2a", ("w2_gate", "w2_up", "w2_down", "w_ple_gate", "w_ple_proj")))
SCATTER_GROUPS = (("ffn2", ("w2_gate", "w2_up", "w2_down", "w_ple_gate", "w_ple_proj")),
                  ("mix", ("w_in", "w_mla_uq", "w_mla_ukv", "w_out")),
                  ("ffn1", ("w1_gate", "w1_up", "w1_down")))


def _layer_fwd(x, p_l, fetch, G, w_conv, cp, sp):
    sv = {}
    x1, sv["ffn1"] = _ffn_fwd(x, G["g_ffn1_pre"], G["g_ffn1_post"], fetch, "1")

    W = fetch("mix", x1)
    h2 = _norm_fwd(x1, G["g_mix_pre"], "norm_fwd")
    Z = _mm([(h2, W["w_in"])], "nn", F32, "mix_in")
    o_sb, tot_sb = _sb_fwd(Z)
    cq, ckv = Z[:, Z_Q:Z_Q + 384], Z[:, Z_KV:Z_KV + 256]
    nq = _norm_fwd(cq, G["g_mla_q"], "norm_fwd_q")
    Qf = _mm([(nq, W["w_mla_uq"])], "nn", F32, "mla_uq")
    nkv = _norm_fwd(ckv, G["g_mla_kv"], "norm_fwd_kv")
    KV = _mm([(nkv, W["w_mla_ukv"])], "nn", F32, "mla_ukv")
    qcat, kcat = _mla_prep(Qf, KV, Z, cp, sp)
    o_mla, lse = _mla_fwd(qcat, kcat, KV)
    y_cv = _conv_fwd(Z, w_conv)
    C = jnp.concatenate([o_sb, o_mla, y_cv], axis=1).astype(BF16)
    Mx = _mm([(C, W["w_out"])], "nn", F32, "mix_out")
    x2 = _resid_norm(x1, Mx, G["g_mix_post"], 1.0, "resid_norm")
    sv["mix"] = dict(x=x1, h=h2, Z=Z, tot_sb=tot_sb, cq=cq, ckv=ckv, nq=nq, nkv=nkv, qcat=qcat, kcat=kcat, KV=KV,
                     o_mla=o_mla, lse=lse, C=C, Mx=Mx)

    x3, sv["ffn2"] = _ffn_fwd(x2, G["g_ffn2_pre"], G["g_ffn2_post"], fetch, "2")
    W = fetch("ffn2b", x3)

    h4 = _norm_fwd(x3, G["g_ple_pre"], "norm_fwd")
    p16 = p_l.astype(BF16)
    Qg = _mm([(h4, W["w_ple_gate"])], "nn", F32, "ple_gate")
    Pp = _ple_proj(p16, W["w_ple_proj"])
    e = _ew(lambda q, pp: _sigmoid(q) * pp, [Qg, Pp], [(D_MODEL, F32)], "ple_mul")
    x4 = _resid_norm(x3, e, G["g_ple_post"], 1.0, "resid_norm")
    sv["ple"] = dict(x=x3, h=h4, p16=p16, Qg=Qg, Pp=Pp, e=e)
    sv["W"] = W
    return x4, sv


def _layer_bwd(dx4, sv, G, w_conv, cp, sp, emit, token):
    gg, gw, W = {}, {}, sv["W"]

    s = sv["ple"]
    de, gg["g_ple_post"] = _norm_bwd(s["e"], _ordered_after(G["g_ple_post"], token), dx4, 1.0, None, F32,
                                     "norm_bwd_e")

    def ple_bwd(dev, q, pp):
        sg = _sigmoid(q)
        return dev * pp * sg * (1.0 - sg), dev * sg

    dQg, dPp = _ew(ple_bwd, [de, s["Qg"], s["Pp"]], [(D_MODEL, BF16)] * 2, "ple_mul_bwd")
    gw["w_ple_proj"] = _dw_shards(s["p16"], dPp, 1, "ple_dw_proj")
    gw["w_ple_gate"] = _dw_shards(s["h"], dQg, 0, "ple_dw_gate")
    dh4 = _mm([(dQg, W["w_ple_gate"])], "nt", F32, "ple_dh")
    dx3, gg["g_ple_pre"] = _norm_bwd(s["x"], G["g_ple_pre"], dh4, 1.0, dx4, F32, "norm_bwd_pre")

    def ready2(dws):
        gw["w2_gate"], gw["w2_up"], gw["w2_down"] = dws
        return emit("ffn2", gw)

    dx2, gg["g_ffn2_pre"], gg["g_ffn2_post"], token = _ffn_bwd(
        dx3, sv["ffn2"], G["g_ffn2_pre"], G["g_ffn2_post"], W["w2_gate"], W["w2_up"], W["w2_down"], ready2)

    s = sv["mix"]
    dM, gg["g_mix_post"] = _norm_bwd(s["Mx"], _ordered_after(G["g_mix_post"], token), dx2, 1.0, None, BF16,
                                     "norm_bwd_post")
    dC = _mm([(dM, W["w_out"])], "nt", F32, "mix_out_dx")
    gw["w_out"] = _dw_shards(s["C"], dM, 0, "mix_out_dw")

    db, dc, dhh, gg["w_conv"] = _conv_bwd(s["Z"], w_conv, dC)

    dqc, dkc, dv = _mla_bwd(s["qcat"], s["kcat"], s["KV"], s["o_mla"], s["lse"], dC)
    dQf, dkr = _mla_prep_bwd(dqc, dkc, cp, sp)
    gw["w_mla_uq"] = _cut_cols(_ungroup_w_uq(_mm([(s["nq"], dQf)], "tn", BF16, "mla_uq_dw")))
    dnq = _mm([(dQf, W["w_mla_uq"])], "nt", F32, "mla_uq_dx")
    dcq, gg["g_mla_q"] = _norm_bwd(s["cq"], G["g_mla_q"], dnq, 1.0, None, F32, "norm_bwd_q")
    dkv = jnp.concatenate([dkc, dv], axis=1).astype(BF16)
    gw["w_mla_ukv"] = _cut_cols(_ungroup_w_ukv(_mm([(s["nkv"], dkv)], "tn", BF16, "mla_ukv_dw")))
    dnkv = _mm([(dkv, W["w_mla_ukv"])], "nt", F32, "mla_ukv_dx")
    dckv, gg["g_mla_kv"] = _norm_bwd(s["ckv"], G["g_mla_kv"], dnkv, 1.0, None, F32, "norm_bwd_kv")

    dsq, dsk, dsv = _sb_bwd(s["Z"], s["tot_sb"], dC)
    dZ = jnp.concatenate([dsq, dsk, dsv, dckv, dkr, dcq, db, dc, dhh], axis=1).astype(BF16)
    gw["w_in"] = _cut_cols(_ungroup_w_in(_mm([(s["h"], dZ)], "tn", BF16, "mix_in_dw")))
    dh2 = _mm([(dZ, W["w_in"])], "nt", F32, "mix_in_dx")
    dx1, gg["g_mix_pre"] = _norm_bwd(s["x"], G["g_mix_pre"], dh2, 1.0, dx2, F32, "norm_bwd_pre")
    started_mix = emit("mix", gw)
    token = token if started_mix is None else started_mix

    def ready1(dws):
        gw["w1_gate"], gw["w1_up"], gw["w1_down"] = dws
        return emit("ffn1", gw)

    dx0, gg["g_ffn1_pre"], gg["g_ffn1_post"], started_ffn1 = _ffn_bwd(
        dx1, sv["ffn1"], G["g_ffn1_pre"], _ordered_after(G["g_ffn1_post"], token), W["w1_gate"], W["w1_up"],
        W["w1_down"], ready1)
    return dx0, gg, token if started_ffn1 is None else started_ffn1


def _pack_small(vecs):
    flat = jnp.concatenate([v.reshape(-1) for v in vecs])
    rows = -(-flat.shape[0] // (8 * LANES)) * 8
    return jnp.pad(flat, (0, rows * LANES - flat.shape[0])).reshape(rows, LANES)


def _group_weights(names, landed, shards, chip):
    out = {}
    for name, g, own in zip(names, landed, shards):
        g = lax.dynamic_update_slice(g, own[None], (chip, 0, 0))
        if name in ("w1_down", "w2_down", "w_out", "w_ple_gate"):
            g = g.reshape(-1, g.shape[2])
        elif name == "w_in":
            g = _regroup_w_in(_cat_cols(g))
        elif name == "w_mla_uq":
            g = _regroup_w_uq(_cat_cols(g))
        elif name == "w_mla_ukv":
            g = _regroup_w_ukv(_cat_cols(g))
        out[name] = g
    return out


def kernel(x, p, positions, g_ffn1_pre, w1_gate, w1_up, w1_down, g_ffn1_post, g_mix_pre, w_in, g_mla_q, w_mla_uq, g_mla_kv, w_mla_ukv, w_conv, w_out, g_mix_post, g_ffn2_pre, w2_gate, w2_up, w2_down, g_ffn2_post, g_ple_pre, w_ple_gate, w_ple_proj, g_ple_post, loss_target, m_g_ffn1_pre, m_w1_gate, m_w1_up, m_w1_down, m_g_ffn1_post, m_g_mix_pre, m_w_in, m_g_mla_q, m_w_mla_uq, m_g_mla_kv, m_w_mla_ukv, m_w_conv, m_w_out, m_g_mix_post, m_g_ffn2_pre, m_w2_gate, m_w2_up, m_w2_down, m_g_ffn2_post, m_g_ple_pre, m_w_ple_gate, m_w_ple_proj, m_g_ple_post, v_g_ffn1_pre, v_w1_gate, v_w1_up, v_w1_down, v_g_ffn1_post, v_g_mix_pre, v_w_in, v_g_mla_q, v_w_mla_uq, v_g_mla_kv, v_w_mla_ukv, v_w_conv, v_w_out, v_g_mix_post, v_g_ffn2_pre, v_w2_gate, v_w2_up, v_w2_down, v_g_ffn2_post, v_g_ple_pre, v_w_ple_gate, v_w_ple_proj, v_g_ple_post):
    w = dict(g_ffn1_pre=g_ffn1_pre, w1_gate=w1_gate, w1_up=w1_up, w1_down=w1_down, g_ffn1_post=g_ffn1_post,
             g_mix_pre=g_mix_pre, w_in=w_in, g_mla_q=g_mla_q, w_mla_uq=w_mla_uq, g_mla_kv=g_mla_kv,
             w_mla_ukv=w_mla_ukv, w_conv=w_conv, w_out=w_out, g_mix_post=g_mix_post, g_ffn2_pre=g_ffn2_pre,
             w2_gate=w2_gate, w2_up=w2_up, w2_down=w2_down, g_ffn2_post=g_ffn2_post, g_ple_pre=g_ple_pre,
             w_ple_gate=w_ple_gate, w_ple_proj=w_ple_proj, g_ple_post=g_ple_post)
    m = dict(g_ffn1_pre=m_g_ffn1_pre, w1_gate=m_w1_gate, w1_up=m_w1_up, w1_down=m_w1_down, g_ffn1_post=m_g_ffn1_post,
             g_mix_pre=m_g_mix_pre, w_in=m_w_in, g_mla_q=m_g_mla_q, w_mla_uq=m_w_mla_uq, g_mla_kv=m_g_mla_kv,
             w_mla_ukv=m_w_mla_ukv, w_conv=m_w_conv, w_out=m_w_out, g_mix_post=m_g_mix_post, g_ffn2_pre=m_g_ffn2_pre,
             w2_gate=m_w2_gate, w2_up=m_w2_up, w2_down=m_w2_down, g_ffn2_post=m_g_ffn2_post, g_ple_pre=m_g_ple_pre,
             w_ple_gate=m_w_ple_gate, w_ple_proj=m_w_ple_proj, g_ple_post=m_g_ple_post)
    v = dict(g_ffn1_pre=v_g_ffn1_pre, w1_gate=v_w1_gate, w1_up=v_w1_up, w1_down=v_w1_down, g_ffn1_post=v_g_ffn1_post,
             g_mix_pre=v_g_mix_pre, w_in=v_w_in, g_mla_q=v_g_mla_q, w_mla_uq=v_w_mla_uq, g_mla_kv=v_g_mla_kv,
             w_mla_ukv=v_w_mla_ukv, w_conv=v_w_conv, w_out=v_w_out, g_mix_post=v_g_mix_post, g_ffn2_pre=v_g_ffn2_pre,
             w2_gate=v_w2_gate, w2_up=v_w2_up, w2_down=v_w2_down, g_ffn2_post=v_g_ffn2_post, g_ple_pre=v_g_ple_pre,
             w_ple_gate=v_w_ple_gate, w_ple_proj=v_w_ple_proj, g_ple_post=v_g_ple_post)

    depth = g_ffn1_pre.shape[0]
    assert depth == 2, "the reduced gradients are assembled in [2, K, N] buffers"
    S = x.shape[1]
    cx, cy, cc = lax.axis_index("x"), lax.axis_index("y"), lax.axis_index("c")
    chip = 2 * cx + cy
    c_idx = cc.reshape(1).astype(jnp.int32)
    idx2 = jnp.stack([chip, cc]).astype(jnp.int32)

    def padded(name, i):
        (ks, ns), (kp, np_) = BIG[name]
        return jnp.pad(w[name][i].astype(BF16), ((0, kp - ks), (0, np_ - ns)))

    def groups_of(i, groups):
        return groups if i == 0 else (("all", BIG_NAMES),)

    started, after = {}, positions
    for i in range(depth):
        for gname, names in groups_of(i, GATHER_GROUPS):
            shards = [padded(name, i) for name in names]
            lands = [lax.empty((N_CHIPS,) + s.shape, BF16) for s in shards]
            started[i, gname] = _exchange_start(_gather_copies, shards, lands, after, "gather_start_%d_%s" % (i, gname))
            after = started[i, gname][4]
    all_started = after

    conv_slot = lax.dynamic_update_slice(jnp.zeros((depth, 3, 256), F32),
                                         w_conv * (cc == 0).astype(F32), (0, 0, 64 * chip))
    w_conv_full = _allreduce_small(_pack_small([conv_slot]))[:depth * 3 * 256 // LANES].reshape(depth, 3, 256)

    inv = ROPE_BASE ** (-jnp.arange(ROPE_HALF, dtype=F32) / ROPE_HALF)
    zeros = lambda n: jnp.zeros((n,), F32)
    ones = jnp.ones((ROPE_HALF,), F32)
    inv_pat = jnp.concatenate([zeros(64), inv, inv, zeros(32)]).reshape(1, LANES)
    sign_pat = jnp.concatenate([zeros(64), -ones, ones, zeros(32)]).reshape(1, LANES)
    cp, sp = _rope_tables(positions.reshape(S, 1), inv_pat, sign_pat)

    def fetcher(i):
        table, have = dict(groups_of(i, GATHER_GROUPS)), {}

        def fetch(group, after):
            key = group if group in table else "all"
            if (i, key) in started:
                ssem, rsem, srcs, lands, _ = started.pop((i, key))
                if i == 0 and group == "ffn1a":
                    after = all_started
                srcs, landed = _exchange_wait(_gather_copies, ssem, rsem, srcs, lands, after,
                                              "gather_wait_%d_%s" % (i, key))
                have.update(_group_weights(table[key], _forward_rows(landed), srcs, chip))
            return have

        return fetch

    xs, saved = x[0], []
    gains = [{name: w[name][i].reshape(1, n) for name, n in GAINS} for i in range(depth)]
    for i in range(depth):
        xs, sv = _layer_fwd(xs, p[i, 0], fetcher(i), gains[i], w_conv_full[i], cp, sp)
        saved.append(sv)

    loss_part, dx = _loss_and_grad(xs, loss_target[0])
    loss = lax.psum(loss_part[0, 0], AXES)

    scattering = []

    def emitter(i):
        table = dict(groups_of(i, SCATTER_GROUPS))

        def emit(group, gw):
            if group in table:
                names = table[group]
            elif group == "ffn1":
                names = table["all"]
            else:
                return None
            full = [gw[name] for name in names]
            pair_sums = [_add_rows(b, g, c_idx) for b, g in zip(full, _swap_rows(full))]
            lands = [lax.empty((3,) + s.shape[1:], BF16) for s in pair_sums]
            st = _exchange_start(_scatter_copies, pair_sums, lands, c_idx, "scatter_start_%d_%s" % (i, group))
            scattering.append((i, group, names, st))
            return st[4]

        return emit

    ggs, token = [None] * depth, None
    for i in reversed(range(depth)):
        dx, ggs[i], token = _layer_bwd(dx, saved[i], gains[i], w_conv_full[i], cp, sp, emitter(i), token)

    fins, after = {name: None for name in BIG_NAMES}, dx
    for i, group, names, (ssem, rsem, srcs, lands, _) in scattering:
        srcs, arrived = _exchange_wait(_scatter_copies, ssem, rsem, srcs, lands, after,
                                       "scatter_wait_%d_%s" % (i, group))
        for name, s1, got in zip(names, srcs, arrived):
            after = fins[name] = _add_chip_sums(s1, got, idx2, i, fins[name], after)
    reduced = dict(zip(BIG_NAMES, _join_rows([fins[name] for name in BIG_NAMES])))

    small_names = [name for name, _ in GAINS] + ["w_conv"]
    small = _allreduce_small(_pack_small([jnp.stack([ggs[i][name] for i in range(depth)]) for name in small_names]))
    flat, off = small.reshape(-1), 0
    for name, n in GAINS:
        reduced[name] = flat[off:off + depth * n].reshape(depth, n)
        off += depth * n
    conv_full = flat[off:off + depth * 3 * 256].reshape(depth, 3, 256)
    reduced["w_conv"] = lax.dynamic_slice(conv_full, (0, 0, 64 * chip), (depth, 3, 64))

    grads, deltas, new_m, new_v = {}, {}, {}, {}
    for name in WEIGHT_ORDER:
        shape = w[name].shape
        three_d = shape if len(shape) == 3 else (1,) + shape
        g_ = reduced[name] if len(shape) == 3 else reduced[name].reshape(three_d)
        outs = _adamw(w[name].reshape(three_d), g_, m[name].reshape(three_d), v[name].reshape(three_d), "adamw")
        grads[name], deltas[name], new_m[name], new_v[name] = (o.reshape(shape) for o in outs)

    return (loss, dx[None], *[grads[n] for n in WEIGHT_ORDER], *[deltas[n] for n in WEIGHT_ORDER],
            *[new_m[n] for n in WEIGHT_ORDER], *[new_v[n] for n in WEIGHT_ORDER])
```

```python
import functools

import jax
import jax.numpy as jnp
from jax import lax
from jax.experimental import pallas as pl
from jax.experimental.pallas import tpu as pltpu

F32 = jnp.float32
BF16 = jnp.bfloat16
MESH = pl.DeviceIdType.MESH
AXES = ("x", "y", "c")

D_MODEL = 1024
N_CHIPS = 4
FF_SHARD = 704
FF_PAD = 768
D_FF_PAD = N_CHIPS * FF_PAD
EPS = 1e-6
NEG_INF = -1e30
ROPE_BASE = 10000.0
ROPE_HALF = 16
LANES = 128
SB_HEADS, MLA_HEADS, HEAD_DIM = 4, 8, 64
MLA_SCALE = 96.0 ** -0.5
SB_SCALE = 64.0 ** -0.5
SB_BQ, SB_BK = 256, 128
MLA_BQ, MLA_BK = 256, 256
ADAM_LR, ADAM_B1, ADAM_B2, ADAM_EPS, ADAM_WD, ADAM_STEP = 0.001, 0.9, 0.999, 1e-08, 0.01, 10
VMEM_LIMIT = 48 * 2 ** 20

Z_SB, Z_KV, Z_KR, Z_Q, Z_CV, Z_W = 0, 768, 1024, 1152, 1536, 2304

NT = (((1,), (1,)), ((), ()))
TN = (((0,), (0,)), ((), ()))

BIG = {"w1_gate": ((1024, FF_SHARD), (1024, FF_PAD)), "w1_up": ((1024, FF_SHARD), (1024, FF_PAD)),
       "w1_down": ((FF_SHARD, 1024), (FF_PAD, 1024)), "w_in": ((1024, 552), (1024, 552)),
       "w_mla_uq": ((384, 192), (384, 192)), "w_mla_ukv": ((256, 256), (256, 256)),
       "w_out": ((256, 1024), (256, 1024)),
       "w2_gate": ((1024, FF_SHARD), (1024, FF_PAD)), "w2_up": ((1024, FF_SHARD), (1024, FF_PAD)),
       "w2_down": ((FF_SHARD, 1024), (FF_PAD, 1024)), "w_ple_gate": ((256, 1024), (256, 1024)),
       "w_ple_proj": ((256, 256), (256, 256))}
BIG_NAMES = tuple(BIG)
GAINS = (("g_ffn1_pre", 1024), ("g_ffn1_post", 1024), ("g_mix_pre", 1024), ("g_mla_q", 384),
         ("g_mla_kv", 256), ("g_mix_post", 1024), ("g_ffn2_pre", 1024), ("g_ffn2_post", 1024),
         ("g_ple_pre", 1024), ("g_ple_post", 1024))
WEIGHT_ORDER = ("g_ffn1_pre", "w1_gate", "w1_up", "w1_down", "g_ffn1_post", "g_mix_pre", "w_in", "g_mla_q",
                "w_mla_uq", "g_mla_kv", "w_mla_ukv", "w_conv", "w_out", "g_mix_post", "g_ffn2_pre", "w2_gate",
                "w2_up", "w2_down", "g_ffn2_post", "g_ple_pre", "w_ple_gate", "w_ple_proj", "g_ple_post")

_pcall = pl.pallas_call


def _params(sem=None):
    return pltpu.CompilerParams(dimension_semantics=sem, vmem_limit_bytes=VMEM_LIMIT)


def _dot(a, b, dims=None):
    a, b = a.astype(BF16), b.astype(BF16)
    if dims is None:
        return jnp.dot(a, b, preferred_element_type=F32)
    return lax.dot_general(a, b, dims, preferred_element_type=F32)


def _rstd(v):
    return lax.rsqrt(jnp.mean(v * v, axis=-1, keepdims=True) + EPS)


def _sigmoid(v):
    return 0.5 * jnp.tanh(0.5 * v) + 0.5


def _row_block(n, want):
    b = min(n, want)
    while n % b:
        b //= 2
    return b


def _spec(block, index, lead=None):
    if not lead:
        return pl.BlockSpec(block, index)
    lead = tuple(lead)
    return pl.BlockSpec((None,) * len(lead) + tuple(block), lambda *g: lead + tuple(index(*g)))


def _arr(op):
    return op[0] if isinstance(op, tuple) else op


def _lead(op):
    return tuple(op[1:]) if isinstance(op, tuple) else ()


def _mm(pairs, mode, out_dtype, name, bm=512, bn=512):
    a0, b0 = _arr(pairs[0][0]), _arr(pairs[0][1])
    if mode == "nn":
        M, N = a0.shape[-2], b0.shape[-1]
    elif mode == "nt":
        M, N = a0.shape[-2], b0.shape[-2]
    else:
        M, N = a0.shape[-1], b0.shape[-1]
    bm, bn = _row_block(M, bm), _row_block(N, bn)
    n_pairs = len(pairs)
    dims = {"nn": None, "nt": NT, "tn": TN}[mode]

    def body(*refs):
        acc = None
        for t in range(n_pairs):
            part = _dot(refs[2 * t][...], refs[2 * t + 1][...], dims)
            acc = part if acc is None else acc + part
        refs[-1][...] = acc.astype(refs[-1].dtype)

    in_specs, ops = [], []
    for a, b in pairs:
        sa, sb = _arr(a).shape, _arr(b).shape
        if mode == "nn":
            in_specs += [_spec((bm, sa[-1]), lambda j, i: (i, 0), _lead(a)),
                         _spec((sb[-2], bn), lambda j, i: (0, j), _lead(b))]
        elif mode == "nt":
            in_specs += [_spec((bm, sa[-1]), lambda j, i: (i, 0), _lead(a)),
                         _spec((bn, sb[-1]), lambda j, i: (j, 0), _lead(b))]
        else:
            in_specs += [_spec((sa[-2], bm), lambda j, i: (0, i), _lead(a)),
                         _spec((sb[-2], bn), lambda j, i: (0, j), _lead(b))]
        ops += [_arr(a), _arr(b)]
    return _pcall(body, name=name, grid=(N // bn, M // bm), in_specs=in_specs,
                  out_specs=pl.BlockSpec((bm, bn), lambda j, i: (i, j)),
                  out_shape=jax.ShapeDtypeStruct((M, N), out_dtype),
                  compiler_params=_params(("parallel", "parallel")))(*ops)


def _dw_shards(a, b, axis, name):
    S = a.shape[0]
    if axis == 1:
        Kp, Np = a.shape[1], b.shape[1] // N_CHIPS
        bm = _row_block(Kp, 512)
        grid = (N_CHIPS, Kp // bm)
        a_spec = pl.BlockSpec((S, bm), lambda j, i: (0, i))
        b_spec = pl.BlockSpec((S, Np), lambda j, i: (0, j))
        o_spec = pl.BlockSpec((None, bm, Np), lambda j, i: (j, i, 0))
    else:
        Kp, Np = a.shape[1] // N_CHIPS, b.shape[1]
        bn = _row_block(Np, 512)
        grid = (N_CHIPS, Np // bn)
        a_spec = pl.BlockSpec((S, Kp), lambda j, i: (0, j))
        b_spec = pl.BlockSpec((S, bn), lambda j, i: (0, i))
        o_spec = pl.BlockSpec((None, Kp, bn), lambda j, i: (j, 0, i))

    def body(a_ref, b_ref, o_ref):
        o_ref[...] = _dot(a_ref[...], b_ref[...], TN).astype(BF16)

    return _pcall(body, name=name, grid=grid, in_specs=[a_spec, b_spec], out_specs=o_spec,
                  out_shape=jax.ShapeDtypeStruct((N_CHIPS, Kp, Np), BF16),
                  compiler_params=_params(("parallel", "parallel")))(a, b)


def _ew(fn, ins, outs, name, br=256):
    S = max(a.shape[0] for a in ins)
    br = _row_block(S, br)
    n_in = len(ins)

    def body(*refs):
        res = fn(*[r[...] for r in refs[:n_in]])
        if not isinstance(res, tuple):
            res = (res,)
        for r, v in zip(refs[n_in:], res):
            r[...] = v.astype(r.dtype)

    in_specs = [pl.BlockSpec((br, a.shape[1]), lambda i: (i, 0)) if a.shape[0] == S and S > 1
                else pl.BlockSpec(a.shape, lambda i: (0, 0)) for a in ins]
    out = _pcall(body, name=name, grid=(S // br,), in_specs=in_specs,
                 out_specs=tuple(pl.BlockSpec((br, w), lambda i: (i, 0)) for w, _ in outs),
                 out_shape=tuple(jax.ShapeDtypeStruct((S, w), dt) for w, dt in outs),
                 compiler_params=_params(("parallel",)))(*ins)
    return out if len(outs) > 1 else out[0]


def _norm_fwd(x, g, name):
    return _ew(lambda xv, gv: xv * _rstd(xv) * gv, [x, g], [(x.shape[1], BF16)], name, br=512)


def _resid_norm(x, y, g, alpha, name):
    return _ew(lambda xv, yv, gv: xv + alpha * (yv * _rstd(yv) * gv), [x, y, g], [(x.shape[1], F32)], name)


def _norm_bwd(xin, g, dy, alpha, resid, out_dtype, name):
    S, W = xin.shape
    br = _row_block(S, 256)
    has_res = resid is not None

    def body(*refs):
        x_ref, g_ref, dy_ref = refs[:3]
        dx_ref, dg_ref = refs[-2:]
        xv, dyv = x_ref[...], dy_ref[...] * alpha
        r = _rstd(xv)
        xh = xv * r
        u = dyv * g_ref[...]
        dx = r * (u - xh * jnp.mean(u * xh, axis=-1, keepdims=True))
        if has_res:
            dx = dx + refs[3][...]
        dx_ref[...] = dx.astype(dx_ref.dtype)
        part = jnp.sum(dyv * xh, axis=0, keepdims=True)

        @pl.when(pl.program_id(0) == 0)
        def _():
            dg_ref[...] = part

        @pl.when(pl.program_id(0) > 0)
        def _():
            dg_ref[...] += part

    row = pl.BlockSpec((br, W), lambda i: (i, 0))
    vec = pl.BlockSpec((1, W), lambda i: (0, 0))
    ops = [xin, g, dy] + ([resid] if has_res else [])
    return _pcall(body, name=name, grid=(S // br,), in_specs=[row, vec, row] + ([row] if has_res else []),
                  out_specs=(row, vec),
                  out_shape=(jax.ShapeDtypeStruct((S, W), out_dtype), jax.ShapeDtypeStruct((1, W), F32)),
                  compiler_params=_params(("arbitrary",)))(*ops)


def _ffn_gate_up(h, wg, wu, name):
    S, K = h.shape
    bm, bn = _row_block(S, 1024), FF_PAD
    per = FF_PAD // bn

    def body(h_ref, wg_ref, wu_ref, g_ref, u_ref, a_ref):
        hv = h_ref[...]
        g = _dot(hv, wg_ref[...])
        u = _dot(hv, wu_ref[...])
        g_ref[...] = g.astype(BF16)
        u_ref[...] = u.astype(BF16)
        a_ref[...] = (g * _sigmoid(g) * u).astype(BF16)

    blk = pl.BlockSpec((bm, bn), lambda n, i: (i, n))
    wsp = pl.BlockSpec((None, K, bn), lambda n, i: (n // per, 0, n % per))
    return _pcall(body, name=name, grid=(D_FF_PAD // bn, S // bm),
                  in_specs=[pl.BlockSpec((bm, K), lambda n, i: (i, 0)), wsp, wsp],
                  out_specs=(blk, blk, blk), out_shape=(jax.ShapeDtypeStruct((S, D_FF_PAD), BF16),) * 3,
                  compiler_params=_params(("parallel", "parallel")))(h, wg, wu)


def _ffn_bwd_mid(dy, wd, g, u, name):
    S, D = dy.shape
    F = g.shape[1]
    bm, bn = _row_block(S, 1024), 256

    def body(dy_ref, wd_ref, g_ref, u_ref, dg_ref, du_ref):
        da = _dot(dy_ref[...], wd_ref[...], NT)
        gv, uv = g_ref[...].astype(F32), u_ref[...].astype(F32)
        s = _sigmoid(gv)
        dg_ref[...] = (da * uv * (s * (1.0 + gv * (1.0 - s)))).astype(BF16)
        du_ref[...] = (da * (gv * s)).astype(BF16)

    blk = pl.BlockSpec((bm, bn), lambda j, i: (i, j))
    return _pcall(body, name=name, grid=(F // bn, S // bm),
                  in_specs=[pl.BlockSpec((bm, D), lambda j, i: (i, 0)),
                            pl.BlockSpec((bn, D), lambda j, i: (j, 0)), blk, blk],
                  out_specs=(blk, blk), out_shape=(jax.ShapeDtypeStruct((S, F), BF16),) * 2,
                  compiler_params=_params(("parallel", "parallel")))(dy, wd, g, u)


def _ffn_dh(dg, du, wg, wu, name):
    S = dg.shape[0]
    D = wg.shape[1]
    bm, bn = _row_block(S, 1024), D

    def body(dg_ref, du_ref, wg_ref, wu_ref, o_ref, acc):
        j = pl.program_id(2)
        part = _dot(dg_ref[...], wg_ref[...], NT) + _dot(du_ref[...], wu_ref[...], NT)

        @pl.when(j == 0)
        def _():
            acc[...] = part

        @pl.when(j > 0)
        def _():
            acc[...] += part

        @pl.when(j == N_CHIPS - 1)
        def _():
            o_ref[...] = acc[...]

    asp = pl.BlockSpec((bm, FF_PAD), lambda n, i, j: (i, j))
    wsp = pl.BlockSpec((None, bn, FF_PAD), lambda n, i, j: (j, n, 0))
    return _pcall(body, name=name, grid=(D // bn, S // bm, N_CHIPS), in_specs=[asp, asp, wsp, wsp],
                  out_specs=pl.BlockSpec((bm, bn), lambda n, i, j: (i, n)),
                  out_shape=jax.ShapeDtypeStruct((S, D), F32), scratch_shapes=[pltpu.VMEM((bm, bn), F32)],
                  compiler_params=_params(("parallel", "parallel", "arbitrary")))(dg, du, wg, wu)


def _ple_proj(p16, w):
    S, K = p16.shape
    bm = _row_block(S, 1024)

    def body(p_ref, w_ref, o_ref):
        o_ref[...] = _dot(p_ref[...], w_ref[...])

    return _pcall(body, name="ple_proj", grid=(N_CHIPS, S // bm),
                  in_specs=[pl.BlockSpec((bm, K), lambda j, i: (i, 0)),
                            pl.BlockSpec((None, K, 256), lambda j, i: (j, 0, 0))],
                  out_specs=pl.BlockSpec((bm, 256), lambda j, i: (i, j)),
                  out_shape=jax.ShapeDtypeStruct((S, N_CHIPS * 256), F32),
                  compiler_params=_params(("parallel", "parallel")))(p16, w)


def _loss_and_grad(xf, tgt):
    S, W = xf.shape
    br = _row_block(S, 256)

    def body(x_ref, t_ref, l_ref, d_ref):
        d = x_ref[...] - t_ref[...]
        d_ref[...] = d * (1.0 / W)
        part = 0.5 * jnp.sum(jnp.sum(d * d, axis=-1, keepdims=True) * (1.0 / W), axis=0, keepdims=True)

        @pl.when(pl.program_id(0) == 0)
        def _():
            l_ref[...] = part

        @pl.when(pl.program_id(0) > 0)
        def _():
            l_ref[...] += part

    row = pl.BlockSpec((br, W), lambda i: (i, 0))
    return _pcall(body, name="loss", grid=(S // br,), in_specs=[row, row],
                  out_specs=(pl.BlockSpec((1, 1), lambda i: (0, 0)), row),
                  out_shape=(jax.ShapeDtypeStruct((1, 1), F32), jax.ShapeDtypeStruct((S, W), F32)),
                  compiler_params=_params(("arbitrary",)))(xf, tgt)


def _adamw(w, g, m, v, name):
    L, R, C = w.shape
    Cp = g.shape[2]
    br = _row_block(R, 256) if R % 8 == 0 else R

    def body(w_ref, g_ref, m_ref, v_ref, go_ref, d_ref, nm_ref, nv_ref):
        gv = g_ref[...][:, :C]
        nm = ADAM_B1 * m_ref[...] + (1.0 - ADAM_B1) * gv
        nv = ADAM_B2 * v_ref[...] + (1.0 - ADAM_B2) * (gv * gv)
        m_hat = nm / (1.0 - ADAM_B1 ** ADAM_STEP)
        v_hat = nv / (1.0 - ADAM_B2 ** ADAM_STEP)
        go_ref[...] = gv
        d_ref[...] = -ADAM_LR * (m_hat / (jnp.sqrt(v_hat) + ADAM_EPS) + ADAM_WD * w_ref[...])
        nm_ref[...] = nm
        nv_ref[...] = nv

    blk = pl.BlockSpec((None, br, C), lambda l, i: (l, i, 0))
    gblk = pl.BlockSpec((None, br, Cp), lambda l, i: (l, i, 0))
    return _pcall(body, name=name, grid=(L, R // br), in_specs=[blk, gblk, blk, blk], out_specs=(blk,) * 4,
                  out_shape=(jax.ShapeDtypeStruct((L, R, C), F32),) * 4,
                  compiler_params=_params(("parallel", "parallel")))(w, g, m, v)


def _rope_tables(pos, inv_pat, sign_pat):
    def fn(p, iv, sg):
        ang = p.astype(F32) * iv
        return jnp.cos(ang), jnp.sin(ang) * sg

    return _ew(fn, [pos, inv_pat, sign_pat], [(LANES, F32)] * 2, "rope_tables")


def _swap_halves_of_rope(v):
    W = v.shape[1]
    lane = lax.broadcasted_iota(jnp.int32, (1, W), 1) % LANES
    return jnp.where((lane >= 64) & (lane < 80), pltpu.roll(v, W - ROPE_HALF, 1),
                     jnp.where((lane >= 80) & (lane < 96), pltpu.roll(v, ROPE_HALF, 1), 0.0))


def _tile_lanes(v, n):
    return jnp.concatenate([v] * n, axis=1)


def _mla_prep(qf, kv, z, cp, sp):
    S = qf.shape[0]
    br = _row_block(S, 256)
    W = MLA_HEADS * LANES

    def body(q_ref, k_ref, r_ref, c_ref, s_ref, qo_ref, ko_ref):
        c, s = c_ref[...], s_ref[...]
        q = q_ref[...]
        qo_ref[...] = (q * _tile_lanes(c, MLA_HEADS) + _swap_halves_of_rope(q) * _tile_lanes(s, MLA_HEADS)).astype(BF16)
        r = r_ref[...]
        lane = lax.broadcasted_iota(jnp.int32, (1, LANES), 1)
        kr = jnp.where((lane >= 64) & (lane < 96), r * c + _swap_halves_of_rope(r) * s, 0.0)
        ko_ref[...] = (k_ref[...] + _tile_lanes(kr, MLA_HEADS)).astype(BF16)

    wide = pl.BlockSpec((br, W), lambda i: (i, 0))
    one = pl.BlockSpec((br, LANES), lambda i: (i, 0))
    return _pcall(body, name="mla_prep", grid=(S // br,),
                  in_specs=[wide, wide, pl.BlockSpec((br, LANES), lambda i: (i, Z_KR // LANES)), one, one],
                  out_specs=(wide, wide), out_shape=(jax.ShapeDtypeStruct((S, W), BF16),) * 2,
                  compiler_params=_params(("parallel",)))(qf, kv, z, cp, sp)


def _mla_prep_bwd(dq, dk, cp, sp):
    S, W = dq.shape
    br = _row_block(S, 256)

    def body(dq_ref, dk_ref, c_ref, s_ref, dqo_ref, dr_ref):
        c, s = c_ref[...], s_ref[...]
        d = dq_ref[...]
        dqo_ref[...] = (d * _tile_lanes(c, MLA_HEADS) + _swap_halves_of_rope(d * _tile_lanes(s, MLA_HEADS))).astype(BF16)
        dkv = dk_ref[...]
        tot = dkv[:, 0:LANES]
        for h in range(1, MLA_HEADS):
            tot = tot + dkv[:, h * LANES:(h + 1) * LANES]
        lane = lax.broadcasted_iota(jnp.int32, (1, LANES), 1)
        tot = jnp.where((lane >= 64) & (lane < 96), tot, 0.0)
        dr_ref[...] = tot * c + _swap_halves_of_rope(tot * s)

    wide = pl.BlockSpec((br, W), lambda i: (i, 0))
    one = pl.BlockSpec((br, LANES), lambda i: (i, 0))
    return _pcall(body, name="mla_prep_bwd", grid=(S // br,), in_specs=[wide, wide, one, one], out_specs=(wide, one),
                  out_shape=(jax.ShapeDtypeStruct((S, W), BF16), jax.ShapeDtypeStruct((S, LANES), F32)),
                  compiler_params=_params(("parallel",)))(dq, dk, cp, sp)


def _shift_down(v, k, row):
    return jnp.where(row >= k, pltpu.roll(v, k, 0), 0.0)


def _shift_up(v, k, row):
    n = v.shape[0]
    return jnp.where(row < n - k, pltpu.roll(v, n - k, 0), 0.0)


def _conv_fwd(z, w):
    S = z.shape[0]
    c0 = Z_CV // LANES

    def body(b_ref, c_ref, h_ref, w_ref, y_ref):
        u = c_ref[...] * h_ref[...]
        row = lax.broadcasted_iota(jnp.int32, u.shape, 0)
        wv = w_ref[...]
        conv = wv[0:1] * _shift_down(u, 2, row) + wv[1:2] * _shift_down(u, 1, row) + wv[2:3] * u
        y_ref[...] = b_ref[...] * conv

    def col(k):
        return pl.BlockSpec((S, LANES), lambda j: (0, c0 + 2 * k + j))

    return _pcall(body, name="conv_fwd", grid=(2,),
                  in_specs=[col(0), col(1), col(2), pl.BlockSpec((3, LANES), lambda j: (0, j))],
                  out_specs=pl.BlockSpec((S, LANES), lambda j: (0, j)),
                  out_shape=jax.ShapeDtypeStruct((S, 256), F32), compiler_params=_params(("parallel",)))(z, z, z, w)


def _conv_bwd(z, w, dc_all):
    S = z.shape[0]
    c0 = Z_CV // LANES

    def body(b_ref, c_ref, h_ref, w_ref, dy_ref, db_ref, dc_ref, dh_ref, dw_ref):
        cv, hv, bv, dyv = c_ref[...], h_ref[...], b_ref[...], dy_ref[...]
        u = cv * hv
        row = lax.broadcasted_iota(jnp.int32, u.shape, 0)
        wv = w_ref[...]
        u1, u2 = _shift_down(u, 1, row), _shift_down(u, 2, row)
        conv = wv[0:1] * u2 + wv[1:2] * u1 + wv[2:3] * u
        db_ref[...] = dyv * conv
        dconv = dyv * bv
        du = wv[2:3] * dconv + wv[1:2] * _shift_up(dconv, 1, row) + wv[0:1] * _shift_up(dconv, 2, row)
        dc_ref[...] = du * hv
        dh_ref[...] = du * cv
        dw_ref[0:1, :] = jnp.sum(dconv * u2, axis=0, keepdims=True)
        dw_ref[1:2, :] = jnp.sum(dconv * u1, axis=0, keepdims=True)
        dw_ref[2:3, :] = jnp.sum(dconv * u, axis=0, keepdims=True)

    def col(k):
        return pl.BlockSpec((S, LANES), lambda j: (0, c0 + 2 * k + j))

    wsp = pl.BlockSpec((3, LANES), lambda j: (0, j))
    osp = pl.BlockSpec((S, LANES), lambda j: (0, j))
    db, dc, dh, dw = _pcall(
        body, name="conv_bwd", grid=(2,),
        in_specs=[col(0), col(1), col(2), wsp, pl.BlockSpec((S, LANES), lambda j: (0, 6 + j))],
        out_specs=(osp, osp, osp, wsp),
        out_shape=(jax.ShapeDtypeStruct((S, 256), F32),) * 3 + (jax.ShapeDtypeStruct((3, 256), F32),),
        compiler_params=_params(("parallel",)))(z, z, z, w, dc_all)
    return db, dc, dh, dw


def _half_masks():
    lane = lax.broadcasted_iota(jnp.int32, (1, LANES), 1)
    return lane < HEAD_DIM, lane >= HEAD_DIM


def _pair(v, p):
    return v[:, p * LANES:(p + 1) * LANES]


def _keep(mask, v):
    return jnp.where(mask, v, jnp.zeros_like(v))


def _lane_col(vals):
    lane = lax.broadcasted_iota(jnp.int32, (1, LANES), 1)
    out = jnp.zeros((vals[0].shape[0], LANES), F32)
    for h, v in enumerate(vals):
        out = jnp.where(lane == h, v, out)
    return out


def _mla_fwd(q, k, kv):
    S = q.shape[0]
    H = MLA_HEADS
    bq, bk = _row_block(S, MLA_BQ), _row_block(S, MLA_BK)
    per = bq // bk

    def body(q_ref, k_ref, v_ref, o_ref, lse_ref):
        i = pl.program_id(0)
        lo, hi = _half_masks()
        qb = q_ref[...]

        def block(j, carry, ok):
            ms, ls, accs = carry
            off = pl.multiple_of(j * bk, bk)
            kb = k_ref[pl.ds(off, bk), :]
            vb = v_ref[pl.ds(off, bk), :].astype(BF16)
            ss = [_dot(_pair(qb, h), _pair(kb, h), NT) * MLA_SCALE for h in range(H)]
            if ok is not None:
                ss = [jnp.where(ok, s, NEG_INF) for s in ss]
            ms2 = [jnp.maximum(ms[h], jnp.max(ss[h], axis=-1, keepdims=True)) for h in range(H)]
            ps = [jnp.exp(ss[h] - ms2[h]) for h in range(H)]
            al = [jnp.exp(ms[h] - ms2[h]) for h in range(H)]
            ls2 = [al[h] * ls[h] + jnp.sum(ps[h], axis=-1, keepdims=True) for h in range(H)]
            pvs = [_dot(ps[h], _pair(vb, h // 2)) for h in range(H)]
            accs2 = []
            for p in range(H // 2):
                scale = jnp.where(lo, al[2 * p], al[2 * p + 1])
                accs2.append(scale * accs[p] + jnp.where(lo, pvs[2 * p], pvs[2 * p + 1]))
            return tuple(ms2), tuple(ls2), tuple(accs2)

        init = (tuple(jnp.full((bq, 1), NEG_INF, F32) for _ in range(H)),
                tuple(jnp.zeros((bq, 1), F32) for _ in range(H)),
                tuple(jnp.zeros((bq, LANES), F32) for _ in range(H // 2)))
        carry = lax.fori_loop(0, i * per, lambda j, c: block(j, c, None), init)
        row = lax.broadcasted_iota(jnp.int32, (bq, bk), 0)
        col = lax.broadcasted_iota(jnp.int32, (bq, bk), 1)
        for t in range(per):
            carry = block(i * per + t, carry, col + t * bk <= row)
        ms, ls, accs = carry
        o_ref[...] = jnp.concatenate(
            [accs[p] / jnp.where(lo, ls[2 * p], ls[2 * p + 1]) for p in range(H // 2)], axis=1)
        lse_ref[...] = _lane_col([ms[h] + jnp.log(ls[h]) for h in range(H)])

    return _pcall(body, name="mla_fwd", grid=(S // bq,),
                  in_specs=[pl.BlockSpec((bq, H * LANES), lambda i: (i, 0)),
                            pl.BlockSpec((S, H * LANES), lambda i: (0, 0)),
                            pl.BlockSpec((S, H * HEAD_DIM), lambda i: (0, 2))],
                  out_specs=(pl.BlockSpec((bq, H * HEAD_DIM), lambda i: (i, 0)),
                             pl.BlockSpec((bq, LANES), lambda i: (i, 0))),
                  out_shape=(jax.ShapeDtypeStruct((S, H * HEAD_DIM), F32), jax.ShapeDtypeStruct((S, LANES), F32)),
                  compiler_params=_params(("parallel",)))(q, k, kv)


def _mla_bwd(q, k, kv, o, lse, dc_all):
    S = q.shape[0]
    H = MLA_HEADS
    bq, bk = _row_block(S, MLA_BQ), _row_block(S, MLA_BK)
    per = bq // bk

    def body(q_ref, k_ref, v_ref, o_ref, lse_ref, doa_ref, dob_ref, dq_ref, dk_ref, dv_ref):
        i = pl.program_id(0)

        @pl.when(i == 0)
        def _():
            dk_ref[...] = jnp.zeros_like(dk_ref)
            dv_ref[...] = jnp.zeros_like(dv_ref)

        lo, hi = _half_masks()
        qb = q_ref[...]
        dob = jnp.concatenate([doa_ref[...], dob_ref[...]], axis=1)
        prod = dob * o_ref[...]
        lse_v = lse_ref[...]
        do16 = dob.astype(BF16)
        dom, deltas, lses = [], [], []
        for h in range(H):
            mk = lo if h % 2 == 0 else hi
            dom.append(_keep(mk, _pair(do16, h // 2)))
            deltas.append(jnp.sum(_keep(mk, _pair(prod, h // 2)), axis=-1, keepdims=True))
            lses.append(lse_v[:, h:h + 1])

        def block(j, dqs, ok):
            off = pl.multiple_of(j * bk, bk)
            kb = k_ref[pl.ds(off, bk), :]
            vb = v_ref[pl.ds(off, bk), :].astype(BF16)
            ss = [_dot(_pair(qb, h), _pair(kb, h), NT) * MLA_SCALE for h in range(H)]
            dps = [_dot(dom[h], _pair(vb, h // 2), NT) for h in range(H)]
            ps = [jnp.exp(ss[h] - lses[h]) for h in range(H)]
            if ok is not None:
                ps = [jnp.where(ok, p, 0.0) for p in ps]
            ds16 = [(ps[h] * (dps[h] - deltas[h]) * MLA_SCALE).astype(BF16) for h in range(H)]
            p16 = [p.astype(BF16) for p in ps]
            dks = [_dot(ds16[h], _pair(qb, h), TN) for h in range(H)]
            dvps = [_dot(p16[h], dom[h], TN) for h in range(H)]
            dqs2 = [dqs[h] + _dot(ds16[h], _pair(kb, h)) for h in range(H)]
            dk_ref[pl.ds(off, bk), :] += jnp.concatenate(dks, axis=1)
            dv_ref[pl.ds(off, bk), :] += jnp.concatenate(
                [dvps[2 * p] + dvps[2 * p + 1] for p in range(H // 2)], axis=1)
            return tuple(dqs2)

        dqs = lax.fori_loop(0, i * per, lambda j, c: block(j, c, None),
                            tuple(jnp.zeros((bq, LANES), F32) for _ in range(H)))
        row = lax.broadcasted_iota(jnp.int32, (bq, bk), 0)
        col = lax.broadcasted_iota(jnp.int32, (bq, bk), 1)
        for t in range(per):
            dqs = block(i * per + t, dqs, col + t * bk <= row)
        dq_ref[...] = jnp.concatenate(dqs, axis=1)

    wide = pl.BlockSpec((bq, H * LANES), lambda i: (i, 0))
    full = pl.BlockSpec((S, H * LANES), lambda i: (0, 0))
    return _pcall(body, name="mla_bwd", grid=(S // bq,),
                  in_specs=[wide, full, pl.BlockSpec((S, H * HEAD_DIM), lambda i: (0, 2)),
                            pl.BlockSpec((bq, H * HEAD_DIM), lambda i: (i, 0)),
                            pl.BlockSpec((bq, LANES), lambda i: (i, 0)),
                            pl.BlockSpec((bq, 256), lambda i: (i, 1)), pl.BlockSpec((bq, 256), lambda i: (i, 2))],
                  out_specs=(wide, full, pl.BlockSpec((S, H * HEAD_DIM), lambda i: (0, 0))),
                  out_shape=(jax.ShapeDtypeStruct((S, H * LANES), F32), jax.ShapeDtypeStruct((S, H * LANES), F32),
                             jax.ShapeDtypeStruct((S, H * HEAD_DIM), F32)),
                  compiler_params=_params(("arbitrary",)))(q, k, kv, o, lse, dc_all, dc_all)


def _dot_exact(x, tri):
    h1 = x.astype(BF16)
    h2 = (x - h1.astype(F32)).astype(BF16)
    return _dot(h1, tri) + _dot(h2, tri)


def _softplus(z):
    return jnp.maximum(z, 0.0) + jnp.log(1.0 + jnp.exp(-jnp.abs(z)))


def _sb_fwd(z):
    S = z.shape[0]
    H = SB_HEADS
    bq, bk = _row_block(S, SB_BQ), _row_block(S, SB_BK)
    per = bq // bk

    def body(q_ref, k_ref, v_ref, o_ref, t_ref):
        i = pl.program_id(0)
        lo, hi = _half_masks()
        q16 = (q_ref[...] * SB_SCALE).astype(BF16)
        qm = [_keep(lo if h % 2 == 0 else hi, _pair(q16, h // 2)) for h in range(H)]
        rr = lax.broadcasted_iota(jnp.int32, (bk, bk), 0)
        cc = lax.broadcasted_iota(jnp.int32, (bk, bk), 1)
        later = (rr > cc).astype(BF16)

        def block(j, carry, mask):
            tails, accs = carry
            off = pl.multiple_of(j * bk, bk)
            kb = k_ref[pl.ds(off, bk), :].astype(BF16)
            vb = v_ref[pl.ds(off, bk), :].astype(BF16)
            zs = [_dot(qm[h], _pair(kb, h // 2), NT) for h in range(H)]
            sps = [_softplus(z) for z in zs]
            lnots = [-sp if mask is None else jnp.where(mask, -sp, 0.0) for sp in sps]
            sums = [_dot_exact(lnot, later) for lnot in lnots]
            ws = []
            for h in range(H):
                w = jnp.exp((zs[h] - sps[h]) + (tails[h] + sums[h]))
                ws.append((w if mask is None else jnp.where(mask, w, 0.0)).astype(BF16))
            pvs = [_dot(ws[h], _pair(vb, h // 2)) for h in range(H)]
            accs2 = [accs[p] + jnp.where(lo, pvs[2 * p], pvs[2 * p + 1]) for p in range(H // 2)]
            tails2 = [tails[h] + jnp.sum(lnots[h], axis=-1, keepdims=True) for h in range(H)]
            return tuple(tails2), tuple(accs2)

        carry = (tuple(jnp.zeros((bq, 1), F32) for _ in range(H)),
                 tuple(jnp.zeros((bq, LANES), F32) for _ in range(H // 2)))
        row = lax.broadcasted_iota(jnp.int32, (bq, bk), 0)
        col = lax.broadcasted_iota(jnp.int32, (bq, bk), 1)
        for t in range(per):
            carry = block(i * per + per - 1 - t, carry, col + (per - 1 - t) * bk < row)
        tails, accs = lax.fori_loop(0, i * per, lambda t, c: block(i * per - 1 - t, c, None), carry)
        o_ref[...] = jnp.concatenate(accs, axis=1)
        t_ref[...] = _lane_col(tails)

    return _pcall(body, name="sb_fwd", grid=(S // bq,),
                  in_specs=[pl.BlockSpec((bq, 256), lambda i: (i, 0)), pl.BlockSpec((S, 256), lambda i: (0, 1)),
                            pl.BlockSpec((S, 256), lambda i: (0, 2))],
                  out_specs=(pl.BlockSpec((bq, 256), lambda i: (i, 0)), pl.BlockSpec((bq, LANES), lambda i: (i, 0))),
                  out_shape=(jax.ShapeDtypeStruct((S, 256), F32), jax.ShapeDtypeStruct((S, LANES), F32)),
                  compiler_params=_params(("parallel",)))(z, z, z)


def _sb_bwd(z, tot, dc_all):
    S = z.shape[0]
    H = SB_HEADS
    bq, bk = _row_block(S, SB_BQ), _row_block(S, SB_BK)
    per = bq // bk

    def body(q_ref, k_ref, v_ref, t_ref, do_ref, dq_ref, dk_ref, dv_ref):
        i = pl.program_id(0)

        @pl.when(i == 0)
        def _():
            dk_ref[...] = jnp.zeros_like(dk_ref)
            dv_ref[...] = jnp.zeros_like(dv_ref)

        lo, hi = _half_masks()
        q16 = (q_ref[...] * SB_SCALE).astype(BF16)
        do16 = do_ref[...].astype(BF16)
        tot_v = t_ref[...]
        masks = [lo if h % 2 == 0 else hi for h in range(H)]
        qm = [_keep(masks[h], _pair(q16, h // 2)) for h in range(H)]
        dom = [_keep(masks[h], _pair(do16, h // 2)) for h in range(H)]
        tots = [tot_v[:, h:h + 1] for h in range(H)]
        rr = lax.broadcasted_iota(jnp.int32, (bk, bk), 0)
        cc = lax.broadcasted_iota(jnp.int32, (bk, bk), 1)
        upto = (rr <= cc).astype(BF16)
        before = (rr < cc).astype(BF16)

        def block(j, carry, mask):
            heads, eheads, dqs = carry
            off = pl.multiple_of(j * bk, bk)
            kb = k_ref[pl.ds(off, bk), :].astype(BF16)
            vb = v_ref[pl.ds(off, bk), :].astype(BF16)
            zs = [_dot(qm[h], _pair(kb, h // 2), NT) for h in range(H)]
            dws = [_dot(dom[h], _pair(vb, h // 2), NT) for h in range(H)]
            sps = [_softplus(z) for z in zs]
            lnots = [-sp if mask is None else jnp.where(mask, -sp, 0.0) for sp in sps]
            pres = [_dot_exact(lnot, upto) for lnot in lnots]
            lsigs = [zs[h] - sps[h] for h in range(H)]
            ws = [jnp.exp(lsigs[h] + (tots[h] - (heads[h] + pres[h]))) for h in range(H)]
            if mask is not None:
                ws = [jnp.where(mask, w, 0.0) for w in ws]
            es = [ws[h] * dws[h] for h in range(H)]
            esums = [eheads[h] + _dot_exact(es[h], before) for h in range(H)]
            dz16 = []
            for h in range(H):
                sig = jnp.exp(lsigs[h])
                dz = es[h] * (1.0 - sig) - sig * esums[h]
                dz16.append((dz if mask is None else jnp.where(mask, dz, 0.0)).astype(BF16))
            w16 = [w.astype(BF16) for w in ws]
            dkps = [_dot(dz16[h], qm[h], TN) for h in range(H)]
            dvps = [_dot(w16[h], dom[h], TN) for h in range(H)]
            dqps = [_dot(dz16[h], _pair(kb, h // 2)) for h in range(H)]
            dk_ref[pl.ds(off, bk), :] += jnp.concatenate([dkps[2 * p] + dkps[2 * p + 1] for p in range(H // 2)], axis=1)
            dv_ref[pl.ds(off, bk), :] += jnp.concatenate([dvps[2 * p] + dvps[2 * p + 1] for p in range(H // 2)], axis=1)
            dqs2 = [dqs[p] + jnp.where(lo, dqps[2 * p], dqps[2 * p + 1]) for p in range(H // 2)]
            heads2 = [heads[h] + jnp.sum(lnots[h], axis=-1, keepdims=True) for h in range(H)]
            eheads2 = [eheads[h] + jnp.sum(es[h], axis=-1, keepdims=True) for h in range(H)]
            return tuple(heads2), tuple(eheads2), tuple(dqs2)

        zeros = tuple(jnp.zeros((bq, 1), F32) for _ in range(H))
        init = (zeros, zeros, tuple(jnp.zeros((bq, LANES), F32) for _ in range(H // 2)))
        carry = lax.fori_loop(0, i * per, lambda j, c: block(j, c, None), init)
        row = lax.broadcasted_iota(jnp.int32, (bq, bk), 0)
        col = lax.broadcasted_iota(jnp.int32, (bq, bk), 1)
        for t in range(per):
            carry = block(i * per + t, carry, col + t * bk < row)
        dq_ref[...] = jnp.concatenate(carry[2], axis=1) * SB_SCALE

    blk = pl.BlockSpec((bq, 256), lambda i: (i, 0))
    full = pl.BlockSpec((S, 256), lambda i: (0, 0))
    return _pcall(body, name="sb_bwd", grid=(S // bq,),
                  in_specs=[blk, pl.BlockSpec((S, 256), lambda i: (0, 1)), pl.BlockSpec((S, 256), lambda i: (0, 2)),
                            pl.BlockSpec((bq, LANES), lambda i: (i, 0)), blk],
                  out_specs=(blk, full, full), out_shape=(jax.ShapeDtypeStruct((S, 256), F32),) * 3,
                  compiler_params=_params(("arbitrary",)))(z, z, z, tot, dc_all)


_HBM = pl.BlockSpec(memory_space=pltpu.HBM)
_SEM = pl.BlockSpec(memory_space=pltpu.SEMAPHORE)
_ANY = pl.BlockSpec(memory_space=pl.ANY)


def _place():
    x, y, c = lax.axis_index("x"), lax.axis_index("y"), lax.axis_index("c")
    other_chips = ((1 - x, y), (x, 1 - y), (1 - x, 1 - y))
    return x, y, c, other_chips


def _remote(src, dst, send_sems, recv_sems, k, dev):
    return pltpu.make_async_remote_copy(src_ref=src, dst_ref=dst, send_sem=send_sems.at[k], recv_sem=recv_sems.at[k],
                                        device_id=dev, device_id_type=MESH)


def _half_rows(c, n_rows):
    half = n_rows // 2
    return pl.ds(pl.multiple_of(c * half, 16), half)


def _gather_copies(src, land, send_sems, recv_sems):
    x, y, c, chips = _place()
    me = 2 * x + y
    cps = []
    for w in range(len(src)):
        rows = _half_rows(c, src[w].shape[0])
        for k, (cx, cy) in enumerate(chips):
            cps.append(_remote(src[w].at[rows], land[w].at[me, rows], send_sems, recv_sems, 3 * w + k, (cx, cy, c)))
    return cps


def _scatter_copies(src, land, send_sems, recv_sems):
    x, y, c, chips = _place()
    cps = []
    for w in range(len(src)):
        for k, (cx, cy) in enumerate(chips):
            cps.append(_remote(src[w].at[2 * cx + cy], land[w].at[k], send_sems, recv_sems, 3 * w + k, (cx, cy, c)))
    return cps


def _exchange_start(copies, srcs, lands, after, name):
    n = len(srcs)

    def body(*refs):
        src, land = refs[:n], refs[n:2 * n]
        send_sems, recv_sems = refs[2 * n + 1], refs[2 * n + 2]
        token = refs[-1]
        for cp in copies(src, land, send_sems, recv_sems):
            cp.start()
        token[...] = jnp.zeros_like(token)

    thru = tuple(pltpu.HBM(a.shape, a.dtype) for a in list(srcs) + list(lands))
    out = _pcall(body, name=name,
                 out_shape=(pltpu.SemaphoreType.DMA((3 * n,)), pltpu.SemaphoreType.DMA((3 * n,))) + thru
                 + (jax.ShapeDtypeStruct((8, LANES), F32),),
                 in_specs=[_HBM] * (2 * n) + [_ANY],
                 out_specs=(_SEM, _SEM) + (_HBM,) * (2 * n) + (pl.BlockSpec(memory_space=pltpu.VMEM),),
                 input_output_aliases={i: 2 + i for i in range(2 * n)},
                 compiler_params=pltpu.CompilerParams(has_side_effects=pltpu.SideEffectType.DATAFLOW_SIDE_EFFECTING))(
        *[pltpu.with_memory_space_constraint(a, pltpu.HBM) for a in list(srcs) + list(lands)], after)
    return out[0], out[1], list(out[2:2 + n]), list(out[2 + n:2 + 2 * n]), out[-1]


def _exchange_wait(copies, send_sems, recv_sems, srcs, lands, after, name):
    n = len(srcs)

    def body(*refs):
        src, land = refs[:n], refs[n:2 * n]
        ssem, rsem = refs[2 * n], refs[2 * n + 1]
        for cp in copies(src, land, ssem, rsem):
            cp.wait_send()
            cp.wait_recv()

    out = _pcall(body, name=name, out_shape=tuple(pltpu.HBM(a.shape, a.dtype) for a in list(srcs) + list(lands)),
                 in_specs=[_HBM] * (2 * n) + [_SEM, _SEM, _ANY], out_specs=(_HBM,) * (2 * n),
                 input_output_aliases={i: i for i in range(2 * n)},
                 compiler_params=pltpu.CompilerParams(has_side_effects=pltpu.SideEffectType.DATAFLOW_SIDE_EFFECTING))(
        *srcs, *lands, send_sems, recv_sems, after)
    return list(out[:n]), list(out[n:])


def _forward_rows(gathered):
    n = len(gathered)

    def body(*refs):
        src, dst = refs[:n], refs[n:2 * n]
        send_sems, recv_sems = refs[2 * n:]
        x, y, c, chips = _place()
        sib = (x, y, 1 - c)
        cps = []
        for w in range(n):
            K = src[w].shape[1]
            for k, (cx, cy) in enumerate(chips):
                blk = (2 * cx + cy, _half_rows(c, K))
                cps.append(_remote(src[w].at[blk], dst[w].at[blk], send_sems, recv_sems, 3 * w + k, sib))
        for cp in cps:
            cp.start()
        for w in range(n):
            K = src[w].shape[1]
            for k, (cx, cy) in enumerate(chips):
                blk = dst[w].at[2 * cx + cy, _half_rows(1 - c, K)]
                _remote(blk, blk, send_sems, recv_sems, 3 * w + k, sib).wait_recv()
        for cp in cps:
            cp.wait_send()

    return _pcall(body, name="forward_rows", in_specs=[_HBM] * n, out_specs=(_HBM,) * n,
                  out_shape=tuple(jax.ShapeDtypeStruct(g.shape, g.dtype) for g in gathered),
                  input_output_aliases={w: w for w in range(n)},
                  scratch_shapes=[pltpu.SemaphoreType.DMA((3 * n,)), pltpu.SemaphoreType.DMA((3 * n,))])(*gathered)


def _swap_rows(bufs):
    n = len(bufs)

    def body(*refs):
        src, dst = refs[:n], refs[n:2 * n]
        send_sems, recv_sems = refs[2 * n:]
        x, y, c, _ = _place()
        cps = [_remote(src[w].at[:, _half_rows(1 - c, src[w].shape[1])], dst[w], send_sems, recv_sems, w,
                       (x, y, 1 - c)) for w in range(n)]
        for cp in cps:
            cp.start()
        for cp in cps:
            cp.wait()

    return _pcall(body, name="swap_rows", in_specs=[_HBM] * n, out_specs=(_HBM,) * n,
                  out_shape=tuple(jax.ShapeDtypeStruct((b.shape[0], b.shape[1] // 2, b.shape[2]), b.dtype) for b in bufs),
                  scratch_shapes=[pltpu.SemaphoreType.DMA((n,)), pltpu.SemaphoreType.DMA((n,))])(*bufs)


def _join_rows(fins):
    n = len(fins)

    def body(*refs):
        src, dst = refs[:n], refs[n:2 * n]
        send_sems, recv_sems = refs[2 * n:]
        x, y, c, _ = _place()
        sib = (x, y, 1 - c)
        cps = []
        for w in range(n):
            blk = (slice(None), _half_rows(c, src[w].shape[1]))
            cps.append(_remote(src[w].at[blk], dst[w].at[blk], send_sems, recv_sems, w, sib))
        for cp in cps:
            cp.start()
        for w in range(n):
            blk = dst[w].at[:, _half_rows(1 - c, src[w].shape[1])]
            _remote(blk, blk, send_sems, recv_sems, w, sib).wait_recv()
        for cp in cps:
            cp.wait_send()

    return _pcall(body, name="join_rows", in_specs=[_HBM] * n, out_specs=(_HBM,) * n,
                  out_shape=tuple(jax.ShapeDtypeStruct(f.shape, f.dtype) for f in fins),
                  input_output_aliases={w: w for w in range(n)},
                  scratch_shapes=[pltpu.SemaphoreType.DMA((n,)), pltpu.SemaphoreType.DMA((n,))])(*fins)


def _add_rows(buf, got, c_idx):
    _, Kh, N = got.shape
    bk = _row_block(Kh, 256)
    nb = Kh // bk

    def body(c_ref, a_ref, b_ref, o_ref):
        o_ref[...] = (a_ref[...].astype(F32) + b_ref[...].astype(F32)).astype(o_ref.dtype)

    blk = pl.BlockSpec((None, bk, N), lambda j, i, c_ref: (j, i, 0))
    grid_spec = pltpu.PrefetchScalarGridSpec(
        num_scalar_prefetch=1, grid=(N_CHIPS, nb),
        in_specs=[pl.BlockSpec((None, bk, N), lambda j, i, c_ref: (j, c_ref[0] * nb + i, 0)), blk], out_specs=blk)
    return _pcall(body, name="add_rows", grid_spec=grid_spec, out_shape=jax.ShapeDtypeStruct((N_CHIPS, Kh, N), BF16),
                  compiler_params=_params(("parallel", "parallel")))(c_idx, buf, got)


def _add_chip_sums(s1, got, idx, layer, into, after):
    _, Kh, N = s1.shape
    bk = _row_block(Kh, 256)
    nb = Kh // bk

    def body(idx_ref, a_ref, b_ref, *rest):
        acc = a_ref[...].astype(F32)
        for k in range(3):
            acc = acc + b_ref[k].astype(F32)
        rest[-1][...] = acc

    in_specs = [pl.BlockSpec((None, bk, N), lambda i, idx_ref: (idx_ref[0], i, 0)),
                pl.BlockSpec((3, bk, N), lambda i, idx_ref: (0, i, 0))]
    in_specs.append(_ANY)
    ops = [idx, s1, got, after]
    if into is not None:
        in_specs.append(_ANY)
        ops.append(into)
    grid_spec = pltpu.PrefetchScalarGridSpec(
        num_scalar_prefetch=1, grid=(nb,), in_specs=in_specs,
        out_specs=pl.BlockSpec((None, bk, N), lambda i, idx_ref: (layer, idx_ref[1] * nb + i, 0)))
    return _pcall(body, name="add_chip_sums" if into is None else "add_chip_sums_into", grid_spec=grid_spec,
                  out_shape=jax.ShapeDtypeStruct((2, 2 * Kh, N), F32),
                  input_output_aliases={} if into is None else {4: 0},
                  compiler_params=_params(("parallel",)))(*ops)


def _allreduce_small(v, after):
    R, W = v.shape

    def body(v_ref, after_ref, o_ref, buf, send_sems, recv_sems):
        x, y, c, _ = _place()
        me = 4 * x + 2 * y + c
        buf[0] = v_ref[...]
        cps = []
        for r in range(1, 8):
            peer = (x if not r & 4 else 1 - x, y if not r & 2 else 1 - y, c if not r & 1 else 1 - c)
            cp = _remote(v_ref, buf.at[r], send_sems, recv_sems, r - 1, peer)
            cp.start()
            cps.append(cp)
        for cp in cps:
            cp.wait()
        acc = buf[me]
        for d in range(1, 8):
            acc = acc + buf[jnp.bitwise_xor(me, d)]
        o_ref[...] = acc

    return _pcall(body, name="allreduce_small", out_shape=jax.ShapeDtypeStruct((R, W), F32),
                  in_specs=[pl.BlockSpec(memory_space=pltpu.VMEM), _ANY],
                  out_specs=pl.BlockSpec(memory_space=pltpu.VMEM),
                  scratch_shapes=[pltpu.VMEM((8, R, W), F32), pltpu.SemaphoreType.DMA((7,)),
                                  pltpu.SemaphoreType.DMA((7,))])(v, after)


def _cat_cols(g):
    return g.transpose(1, 0, 2).reshape(g.shape[1], -1)


def _cut_cols(w):
    K, N = w.shape
    return w.reshape(K, N_CHIPS, N // N_CHIPS).transpose(1, 0, 2)


def _regroup_w_in(w):
    zeros = lambda n: jnp.zeros((w.shape[0], n), w.dtype)
    return jnp.concatenate([w[:, 0:768], w[:, 1152:1408], zeros(64), w[:, 1408:1440], zeros(32),
                            w[:, 768:1152], w[:, 1440:2208]], axis=1)


def _ungroup_w_in(w):
    return jnp.concatenate([w[:, 0:768], w[:, Z_Q:Z_Q + 384], w[:, Z_KV:Z_KV + 256],
                            w[:, Z_KR + 64:Z_KR + 96], w[:, Z_CV:Z_W]], axis=1)


def _regroup_w_uq(w):
    K = w.shape[0]
    return jnp.pad(w.reshape(K, MLA_HEADS, 96), ((0, 0), (0, 0), (0, 32))).reshape(K, MLA_HEADS * LANES)


def _ungroup_w_uq(w):
    K = w.shape[0]
    return w.reshape(K, MLA_HEADS, LANES)[:, :, :96].reshape(K, MLA_HEADS * 96)


def _regroup_w_ukv(w):
    K = w.shape[0]
    t = w.reshape(K, MLA_HEADS, 128)
    return jnp.concatenate([jnp.pad(t[:, :, :64], ((0, 0), (0, 0), (0, 64))).reshape(K, MLA_HEADS * LANES),
                            t[:, :, 64:].reshape(K, MLA_HEADS * HEAD_DIM)], axis=1)


def _ungroup_w_ukv(w):
    K = w.shape[0]
    return jnp.concatenate([w[:, :MLA_HEADS * LANES].reshape(K, MLA_HEADS, LANES)[:, :, :64],
                            w[:, MLA_HEADS * LANES:].reshape(K, MLA_HEADS, 64)], axis=2).reshape(K, 1024)


def _ffn_fwd(x, g_pre, g_post, fetch, tag):
    W = fetch("ffn%sa" % tag, x)
    h = _norm_fwd(x, g_pre, "norm_fwd")
    G, U, A = _ffn_gate_up(h, W["w%s_gate" % tag], W["w%s_up" % tag], "ffn_gate_up")
    W = fetch("ffn%sb" % tag, A)
    Y = _mm([(A, W["w%s_down" % tag])], "nn", F32, "ffn_down", bm=1024)
    return _resid_norm(x, Y, g_post, 0.5, "resid_norm"), dict(x=x, h=h, G=G, U=U, A=A, Y=Y)


def _ffn_bwd(dxo, sv, g_pre, g_post, wg, wu, wd_rows, ready):
    dY, dg_post = _norm_bwd(sv["Y"], g_post, dxo, 0.5, None, BF16, "norm_bwd_post")
    dG, dU = _ffn_bwd_mid(dY, wd_rows, sv["G"], sv["U"], "ffn_bwd_mid")
    token = ready((_dw_shards(sv["h"], dG, 1, "ffn_dw_in"), _dw_shards(sv["h"], dU, 1, "ffn_dw_in"),
                   _dw_shards(sv["A"], dY, 0, "ffn_dw_down")))
    dh = _ffn_dh(dG, dU, wg, wu, "ffn_dh")
    dx, dg_pre = _norm_bwd(sv["x"], _ordered_after(g_pre, token), dh, 1.0, dxo, F32, "norm_bwd_pre")
    return dx, dg_pre, dg_post, token


def _ordered_after(g, token):
    return g if token is None else g + token[0, 0]


GATHER_GROUPS = (("ffn1a", ("w1_gate", "w1_up")), ("ffn1b", ("w1_down",)),
                 ("mix", ("w_in", "w_mla_uq", "w_mla_ukv", "w_out")),
                 ("ffn2a", ("w2_gate", "w2_up", "w2_down", "w_ple_gate", "w_ple_proj")))
SCATTER_GROUPS = (("ffn2", ("w2_gate", "w2_up", "w2_down", "w_ple_gate", "w_ple_proj")),
                  ("mix", ("w_in", "w_mla_uq", "w_mla_ukv", "w_out")),
                  ("ffn1", ("w1_gate", "w1_up", "w1_down")))


def _layer_fwd(x, p_l, fetch, G, w_conv, cp, sp):
    sv = {}
    x1, sv["ffn1"] = _ffn_fwd(x, G["g_ffn1_pre"], G["g_ffn1_post"], fetch, "1")

    W = fetch("mix", x1)
    h2 = _norm_fwd(x1, G["g_mix_pre"], "norm_fwd")
    Z = _mm([(h2, W["w_in"])], "nn", F32, "mix_in")
    o_sb, tot_sb = _sb_fwd(Z)
    cq, ckv = Z[:, Z_Q:Z_Q + 384], Z[:, Z_KV:Z_KV + 256]
    nq = _norm_fwd(cq, G["g_mla_q"], "norm_fwd_q")
    Qf = _mm([(nq, W["w_mla_uq"])], "nn", F32, "mla_uq")
    nkv = _norm_fwd(ckv, G["g_mla_kv"], "norm_fwd_kv")
    KV = _mm([(nkv, W["w_mla_ukv"])], "nn", F32, "mla_ukv")
    qcat, kcat = _mla_prep(Qf, KV, Z, cp, sp)
    o_mla, lse = _mla_fwd(qcat, kcat, KV)
    y_cv = _conv_fwd(Z, w_conv)
    C = jnp.concatenate([o_sb, o_mla, y_cv], axis=1).astype(BF16)
    Mx = _mm([(C, W["w_out"])], "nn", F32, "mix_out")
    x2 = _resid_norm(x1, Mx, G["g_mix_post"], 1.0, "resid_norm")
    sv["mix"] = dict(x=x1, h=h2, Z=Z, tot_sb=tot_sb, cq=cq, ckv=ckv, nq=nq, nkv=nkv, qcat=qcat, kcat=kcat, KV=KV,
                     o_mla=o_mla, lse=lse, C=C, Mx=Mx)

    x3, sv["ffn2"] = _ffn_fwd(x2, G["g_ffn2_pre"], G["g_ffn2_post"], fetch, "2")
    W = fetch("ffn2b", x3)

    h4 = _norm_fwd(x3, G["g_ple_pre"], "norm_fwd")
    p16 = p_l.astype(BF16)
    Qg = _mm([(h4, W["w_ple_gate"])], "nn", F32, "ple_gate")
    Pp = _ple_proj(p16, W["w_ple_proj"])
    e = _ew(lambda q, pp: _sigmoid(q) * pp, [Qg, Pp], [(D_MODEL, F32)], "ple_mul")
    x4 = _resid_norm(x3, e, G["g_ple_post"], 1.0, "resid_norm")
    sv["ple"] = dict(x=x3, h=h4, p16=p16, Qg=Qg, Pp=Pp, e=e)
    sv["W"] = W
    return x4, sv


def _layer_bwd(dx4, sv, G, w_conv, cp, sp, emit, token):
    gg, gw, W = {}, {}, sv["W"]

    s = sv["ple"]
    de, gg["g_ple_post"] = _norm_bwd(s["e"], _ordered_after(G["g_ple_post"], token), dx4, 1.0, None, F32,
                                     "norm_bwd_e")

    def ple_bwd(dev, q, pp):
        sg = _sigmoid(q)
        return dev * pp * sg * (1.0 - sg), dev * sg

    dQg, dPp = _ew(ple_bwd, [de, s["Qg"], s["Pp"]], [(D_MODEL, BF16)] * 2, "ple_mul_bwd")
    gw["w_ple_proj"] = _dw_shards(s["p16"], dPp, 1, "ple_dw_proj")
    gw["w_ple_gate"] = _dw_shards(s["h"], dQg, 0, "ple_dw_gate")
    dh4 = _mm([(dQg, W["w_ple_gate"])], "nt", F32, "ple_dh")
    dx3, gg["g_ple_pre"] = _norm_bwd(s["x"], G["g_ple_pre"], dh4, 1.0, dx4, F32, "norm_bwd_pre")

    def ready2(dws):
        gw["w2_gate"], gw["w2_up"], gw["w2_down"] = dws
        return emit("ffn2", gw)

    dx2, gg["g_ffn2_pre"], gg["g_ffn2_post"], token = _ffn_bwd(
        dx3, sv["ffn2"], G["g_ffn2_pre"], G["g_ffn2_post"], W["w2_gate"], W["w2_up"], W["w2_down"], ready2)

    s = sv["mix"]
    dM, gg["g_mix_post"] = _norm_bwd(s["Mx"], _ordered_after(G["g_mix_post"], token), dx2, 1.0, None, BF16,
                                     "norm_bwd_post")
    dC = _mm([(dM, W["w_out"])], "nt", F32, "mix_out_dx")
    gw["w_out"] = _dw_shards(s["C"], dM, 0, "mix_out_dw")

    db, dc, dhh, gg["w_conv"] = _conv_bwd(s["Z"], w_conv, dC)

    dqc, dkc, dv = _mla_bwd(s["qcat"], s["kcat"], s["KV"], s["o_mla"], s["lse"], dC)
    dQf, dkr = _mla_prep_bwd(dqc, dkc, cp, sp)
    gw["w_mla_uq"] = _cut_cols(_ungroup_w_uq(_mm([(s["nq"], dQf)], "tn", BF16, "mla_uq_dw")))
    dnq = _mm([(dQf, W["w_mla_uq"])], "nt", F32, "mla_uq_dx")
    dcq, gg["g_mla_q"] = _norm_bwd(s["cq"], G["g_mla_q"], dnq, 1.0, None, F32, "norm_bwd_q")
    dkv = jnp.concatenate([dkc, dv], axis=1).astype(BF16)
    gw["w_mla_ukv"] = _cut_cols(_ungroup_w_ukv(_mm([(s["nkv"], dkv)], "tn", BF16, "mla_ukv_dw")))
    dnkv = _mm([(dkv, W["w_mla_ukv"])], "nt", F32, "mla_ukv_dx")
    dckv, gg["g_mla_kv"] = _norm_bwd(s["ckv"], G["g_mla_kv"], dnkv, 1.0, None, F32, "norm_bwd_kv")

    dsq, dsk, dsv = _sb_bwd(s["Z"], s["tot_sb"], dC)
    dZ = jnp.concatenate([dsq, dsk, dsv, dckv, dkr, dcq, db, dc, dhh], axis=1).astype(BF16)
    gw["w_in"] = _cut_cols(_ungroup_w_in(_mm([(s["h"], dZ)], "tn", BF16, "mix_in_dw")))
    dh2 = _mm([(dZ, W["w_in"])], "nt", F32, "mix_in_dx")
    dx1, gg["g_mix_pre"] = _norm_bwd(s["x"], G["g_mix_pre"], dh2, 1.0, dx2, F32, "norm_bwd_pre")
    started_mix = emit("mix", gw)
    token = token if started_mix is None else started_mix

    def ready1(dws):
        gw["w1_gate"], gw["w1_up"], gw["w1_down"] = dws
        return emit("ffn1", gw)

    dx0, gg["g_ffn1_pre"], gg["g_ffn1_post"], started_ffn1 = _ffn_bwd(
        dx1, sv["ffn1"], G["g_ffn1_pre"], _ordered_after(G["g_ffn1_post"], token), W["w1_gate"], W["w1_up"],
        W["w1_down"], ready1)
    return dx0, gg, token if started_ffn1 is None else started_ffn1


def _pack_small(vecs):
    flat = jnp.concatenate([v.reshape(-1) for v in vecs])
    rows = -(-flat.shape[0] // (8 * LANES)) * 8
    return jnp.pad(flat, (0, rows * LANES - flat.shape[0])).reshape(rows, LANES)


def _group_weights(names, landed, shards, chip):
    out = {}
    for name, g, own in zip(names, landed, shards):
        g = lax.dynamic_update_slice(g, own[None], (chip, 0, 0))
        if name in ("w1_down", "w2_down", "w_out", "w_ple_gate"):
            g = g.reshape(-1, g.shape[2])
        elif name == "w_in":
            g = _regroup_w_in(_cat_cols(g))
        elif name == "w_mla_uq":
            g = _regroup_w_uq(_cat_cols(g))
        elif name == "w_mla_ukv":
            g = _regroup_w_ukv(_cat_cols(g))
        out[name] = g
    return out


def kernel(x, p, positions, g_ffn1_pre, w1_gate, w1_up, w1_down, g_ffn1_post, g_mix_pre, w_in, g_mla_q, w_mla_uq, g_mla_kv, w_mla_ukv, w_conv, w_out, g_mix_post, g_ffn2_pre, w2_gate, w2_up, w2_down, g_ffn2_post, g_ple_pre, w_ple_gate, w_ple_proj, g_ple_post, loss_target, m_g_ffn1_pre, m_w1_gate, m_w1_up, m_w1_down, m_g_ffn1_post, m_g_mix_pre, m_w_in, m_g_mla_q, m_w_mla_uq, m_g_mla_kv, m_w_mla_ukv, m_w_conv, m_w_out, m_g_mix_post, m_g_ffn2_pre, m_w2_gate, m_w2_up, m_w2_down, m_g_ffn2_post, m_g_ple_pre, m_w_ple_gate, m_w_ple_proj, m_g_ple_post, v_g_ffn1_pre, v_w1_gate, v_w1_up, v_w1_down, v_g_ffn1_post, v_g_mix_pre, v_w_in, v_g_mla_q, v_w_mla_uq, v_g_mla_kv, v_w_mla_ukv, v_w_conv, v_w_out, v_g_mix_post, v_g_ffn2_pre, v_w2_gate, v_w2_up, v_w2_down, v_g_ffn2_post, v_g_ple_pre, v_w_ple_gate, v_w_ple_proj, v_g_ple_post):
    w = dict(g_ffn1_pre=g_ffn1_pre, w1_gate=w1_gate, w1_up=w1_up, w1_down=w1_down, g_ffn1_post=g_ffn1_post,
             g_mix_pre=g_mix_pre, w_in=w_in, g_mla_q=g_mla_q, w_mla_uq=w_mla_uq, g_mla_kv=g_mla_kv,
             w_mla_ukv=w_mla_ukv, w_conv=w_conv, w_out=w_out, g_mix_post=g_mix_post, g_ffn2_pre=g_ffn2_pre,
             w2_gate=w2_gate, w2_up=w2_up, w2_down=w2_down, g_ffn2_post=g_ffn2_post, g_ple_pre=g_ple_pre,
             w_ple_gate=w_ple_gate, w_ple_proj=w_ple_proj, g_ple_post=g_ple_post)
    m = dict(g_ffn1_pre=m_g_ffn1_pre, w1_gate=m_w1_gate, w1_up=m_w1_up, w1_down=m_w1_down, g_ffn1_post=m_g_ffn1_post,
             g_mix_pre=m_g_mix_pre, w_in=m_w_in, g_mla_q=m_g_mla_q, w_mla_uq=m_w_mla_uq, g_mla_kv=m_g_mla_kv,
             w_mla_ukv=m_w_mla_ukv, w_conv=m_w_conv, w_out=m_w_out, g_mix_post=m_g_mix_post, g_ffn2_pre=m_g_ffn2_pre,
             w2_gate=m_w2_gate, w2_up=m_w2_up, w2_down=m_w2_down, g_ffn2_post=m_g_ffn2_post, g_ple_pre=m_g_ple_pre,
             w_ple_gate=m_w_ple_gate, w_ple_proj=m_w_ple_proj, g_ple_post=m_g_ple_post)
    v = dict(g_ffn1_pre=v_g_ffn1_pre, w1_gate=v_w1_gate, w1_up=v_w1_up, w1_down=v_w1_down, g_ffn1_post=v_g_ffn1_post,
             g_mix_pre=v_g_mix_pre, w_in=v_w_in, g_mla_q=v_g_mla_q, w_mla_uq=v_w_mla_uq, g_mla_kv=v_g_mla_kv,
             w_mla_ukv=v_w_mla_ukv, w_conv=v_w_conv, w_out=v_w_out, g_mix_post=v_g_mix_post, g_ffn2_pre=v_g_ffn2_pre,
             w2_gate=v_w2_gate, w2_up=v_w2_up, w2_down=v_w2_down, g_ffn2_post=v_g_ffn2_post, g_ple_pre=v_g_ple_pre,
             w_ple_gate=v_w_ple_gate, w_ple_proj=v_w_ple_proj, g_ple_post=v_g_ple_post)

    depth = g_ffn1_pre.shape[0]
    assert depth == 2, "the reduced gradients are assembled in [2, K, N] buffers"
    S = x.shape[1]
    cx, cy, cc = lax.axis_index("x"), lax.axis_index("y"), lax.axis_index("c")
    chip = 2 * cx + cy
    c_idx = cc.reshape(1).astype(jnp.int32)
    idx2 = jnp.stack([chip, cc]).astype(jnp.int32)

    conv_slot = lax.dynamic_update_slice(jnp.zeros((depth, 3, 256), F32),
                                         w_conv * (cc == 0).astype(F32), (0, 0, 64 * chip))
    conv_sum = _allreduce_small(_pack_small([conv_slot]), positions)
    w_conv_full = conv_sum[:depth * 3 * 256 // LANES].reshape(depth, 3, 256)

    def padded(name, i):
        (ks, ns), (kp, np_) = BIG[name]
        return jnp.pad(w[name][i].astype(BF16), ((0, kp - ks), (0, np_ - ns)))

    def groups_of(i, groups):
        return groups if i == 0 else (("all", BIG_NAMES),)

    started, after = {}, conv_sum
    for i in range(depth):
        for gname, names in groups_of(i, GATHER_GROUPS):
            shards = [padded(name, i) for name in names]
            lands = [lax.empty((N_CHIPS,) + s.shape, BF16) for s in shards]
            started[i, gname] = _exchange_start(_gather_copies, shards, lands, after, "gather_start_%d_%s" % (i, gname))
            after = started[i, gname][4]
    all_started = after

    inv = ROPE_BASE ** (-jnp.arange(ROPE_HALF, dtype=F32) / ROPE_HALF)
    zeros = lambda n: jnp.zeros((n,), F32)
    ones = jnp.ones((ROPE_HALF,), F32)
    inv_pat = jnp.concatenate([zeros(64), inv, inv, zeros(32)]).reshape(1, LANES)
    sign_pat = jnp.concatenate([zeros(64), -ones, ones, zeros(32)]).reshape(1, LANES)
    cp, sp = _rope_tables(positions.reshape(S, 1), inv_pat, sign_pat)

    def fetcher(i):
        table, have = dict(groups_of(i, GATHER_GROUPS)), {}

        def fetch(group, after):
            key = group if group in table else "all"
            if (i, key) in started:
                ssem, rsem, srcs, lands, _ = started.pop((i, key))
                if i == 0 and group == "ffn1a":
                    after = all_started
                srcs, landed = _exchange_wait(_gather_copies, ssem, rsem, srcs, lands, after,
                                              "gather_wait_%d_%s" % (i, key))
                have.update(_group_weights(table[key], _forward_rows(landed), srcs, chip))
            return have

        return fetch

    xs, saved = x[0], []
    gains = [{name: w[name][i].reshape(1, n) for name, n in GAINS} for i in range(depth)]
    for i in range(depth):
        xs, sv = _layer_fwd(xs, p[i, 0], fetcher(i), gains[i], w_conv_full[i], cp, sp)
        saved.append(sv)

    loss_part, dx = _loss_and_grad(xs, loss_target[0])
    loss = lax.psum(loss_part[0, 0], AXES)

    scattering = []

    def emitter(i):
        table = dict(groups_of(i, SCATTER_GROUPS))

        def emit(group, gw):
            if group in table:
                names = table[group]
            elif group == "ffn1":
                names = table["all"]
            else:
                return None
            full = [gw[name] for name in names]
            pair_sums = [_add_rows(b, g, c_idx) for b, g in zip(full, _swap_rows(full))]
            lands = [lax.empty((3,) + s.shape[1:], BF16) for s in pair_sums]
            st = _exchange_start(_scatter_copies, pair_sums, lands, c_idx, "scatter_start_%d_%s" % (i, group))
            scattering.append((i, group, names, st))
            return st[4]

        return emit

    ggs, token = [None] * depth, None
    for i in reversed(range(depth)):
        dx, ggs[i], token = _layer_bwd(dx, saved[i], gains[i], w_conv_full[i], cp, sp, emitter(i), token)

    fins, after = {name: None for name in BIG_NAMES}, dx
    for i, group, names, (ssem, rsem, srcs, lands, _) in scattering:
        srcs, arrived = _exchange_wait(_scatter_copies, ssem, rsem, srcs, lands, after,
                                       "scatter_wait_%d_%s" % (i, group))
        for name, s1, got in zip(names, srcs, arrived):
            after = fins[name] = _add_chip_sums(s1, got, idx2, i, fins[name], after)
    joined = _join_rows([fins[name] for name in BIG_NAMES])
    reduced = dict(zip(BIG_NAMES, joined))

    small_names = [name for name, _ in GAINS] + ["w_conv"]
    small = _allreduce_small(_pack_small([jnp.stack([ggs[i][name] for i in range(depth)]) for name in small_names]),
                             joined[0])
    flat, off = small.reshape(-1), 0
    for name, n in GAINS:
        reduced[name] = flat[off:off + depth * n].reshape(depth, n)
        off += depth * n
    conv_full = flat[off:off + depth * 3 * 256].reshape(depth, 3, 256)
    reduced["w_conv"] = lax.dynamic_slice(conv_full, (0, 0, 64 * chip), (depth, 3, 64))

    grads, deltas, new_m, new_v = {}, {}, {}, {}
    for name in WEIGHT_ORDER:
        shape = w[name].shape
        three_d = shape if len(shape) == 3 else (1,) + shape
        g_ = reduced[name] if len(shape) == 3 else reduced[name].reshape(three_d)
        outs = _adamw(w[name].reshape(three_d), g_, m[name].reshape(three_d), v[name].reshape(three_d), "adamw")
        grads[name], deltas[name], new_m[name], new_v[name] = (o.reshape(shape) for o in outs)

    return (loss, dx[None], *[grads[n] for n in WEIGHT_ORDER], *[deltas[n] for n in WEIGHT_ORDER],
            *[new_m[n] for n in WEIGHT_ORDER], *[new_v[n] for n in WEIGHT_ORDER])
```

```python
import functools

import jax
import jax.numpy as jnp
from jax import lax
from jax.experimental import pallas as pl
from jax.experimental.pallas import tpu as pltpu

F32 = jnp.float32
BF16 = jnp.bfloat16
MESH = pl.DeviceIdType.MESH
AXES = ("x", "y", "c")

D_MODEL = 1024
N_CHIPS = 4
D_FF = 2816
EPS = 1e-6
NEG_INF = -1e30
ROPE_BASE = 10000.0
ROPE_HALF = 16
LANES = 128
SB_HEADS, MLA_HEADS, HEAD_DIM = 4, 8, 64
MLA_SCALE = 96.0 ** -0.5
SB_SCALE = 64.0 ** -0.5
SB_BQ, SB_BK = 256, 128
MLA_BQ, MLA_BK = 256, 256
ADAM_LR, ADAM_B1, ADAM_B2, ADAM_EPS, ADAM_WD, ADAM_STEP = 0.001, 0.9, 0.999, 1e-08, 0.01, 10
VMEM_LIMIT = 48 * 2 ** 20

Z_SB, Z_KV, Z_KR, Z_Q, Z_CV, Z_W = 0, 768, 1024, 1152, 1536, 2304

NT = (((1,), (1,)), ((), ()))
TN = (((0,), (0,)), ((), ()))

TRANSPOSED = ("w1_gate", "w1_up", "w2_gate", "w2_up", "w_in", "w_mla_uq")
BIG = {"w1_gate": ((704, 1024), 704), "w1_up": ((704, 1024), 704), "w1_down": ((704, 1024), 704),
       "w_in": ((552, 1024), 576), "w_mla_uq": ((192, 384), 192), "w_mla_ukv": ((256, 256), 256),
       "w_out": ((256, 1024), 256),
       "w2_gate": ((704, 1024), 704), "w2_up": ((704, 1024), 704), "w2_down": ((704, 1024), 704),
       "w_ple_gate": ((256, 1024), 256), "w_ple_proj": ((256, 256), 256)}
BIG_NAMES = tuple(BIG)
GAINS = (("g_ffn1_pre", 1024), ("g_ffn1_post", 1024), ("g_mix_pre", 1024), ("g_mla_q", 384),
         ("g_mla_kv", 256), ("g_mix_post", 1024), ("g_ffn2_pre", 1024), ("g_ffn2_post", 1024),
         ("g_ple_pre", 1024), ("g_ple_post", 1024))
WEIGHT_ORDER = ("g_ffn1_pre", "w1_gate", "w1_up", "w1_down", "g_ffn1_post", "g_mix_pre", "w_in", "g_mla_q",
                "w_mla_uq", "g_mla_kv", "w_mla_ukv", "w_conv", "w_out", "g_mix_post", "g_ffn2_pre", "w2_gate",
                "w2_up", "w2_down", "g_ffn2_post", "g_ple_pre", "w_ple_gate", "w_ple_proj", "g_ple_post")

_pcall = pl.pallas_call


def _params(sem=None):
    return pltpu.CompilerParams(dimension_semantics=sem, vmem_limit_bytes=VMEM_LIMIT)


def _dot(a, b, dims=None):
    a, b = a.astype(BF16), b.astype(BF16)
    if dims is None:
        return jnp.dot(a, b, preferred_element_type=F32)
    return lax.dot_general(a, b, dims, preferred_element_type=F32)


def _rstd(v):
    return lax.rsqrt(jnp.mean(v * v, axis=-1, keepdims=True) + EPS)


def _sigmoid(v):
    return 0.5 * jnp.tanh(0.5 * v) + 0.5


def _row_block(n, want, mult=16):
    for b in range(min(n, want), 0, -1):
        if n % b == 0 and b % mult == 0:
            return b
    return n


def _spec(block, index, lead=None):
    if not lead:
        return pl.BlockSpec(block, index)
    lead = tuple(lead)
    return pl.BlockSpec((None,) * len(lead) + tuple(block), lambda *g: lead + tuple(index(*g)))


def _arr(op):
    return op[0] if isinstance(op, tuple) else op


def _lead(op):
    return tuple(op[1:]) if isinstance(op, tuple) else ()


def _mm(pairs, mode, out_dtype, name, bm=512, bn=512):
    a0, b0 = _arr(pairs[0][0]), _arr(pairs[0][1])
    if mode == "nn":
        M, N = a0.shape[-2], b0.shape[-1]
    elif mode == "nt":
        M, N = a0.shape[-2], b0.shape[-2]
    else:
        M, N = a0.shape[-1], b0.shape[-1]
    bm, bn = _row_block(M, bm, 128 if mode == "tn" else 16), _row_block(N, bn, 128)
    n_pairs = len(pairs)
    dims = {"nn": None, "nt": NT, "tn": TN}[mode]

    def body(*refs):
        acc = None
        for t in range(n_pairs):
            part = _dot(refs[2 * t][...], refs[2 * t + 1][...], dims)
            acc = part if acc is None else acc + part
        refs[-1][...] = acc.astype(refs[-1].dtype)

    in_specs, ops = [], []
    for a, b in pairs:
        sa, sb = _arr(a).shape, _arr(b).shape
        if mode == "nn":
            in_specs += [_spec((bm, sa[-1]), lambda j, i: (i, 0), _lead(a)),
                         _spec((sb[-2], bn), lambda j, i: (0, j), _lead(b))]
        elif mode == "nt":
            in_specs += [_spec((bm, sa[-1]), lambda j, i: (i, 0), _lead(a)),
                         _spec((bn, sb[-1]), lambda j, i: (j, 0), _lead(b))]
        else:
            in_specs += [_spec((sa[-2], bm), lambda j, i: (0, i), _lead(a)),
                         _spec((sb[-2], bn), lambda j, i: (0, j), _lead(b))]
        ops += [_arr(a), _arr(b)]
    return _pcall(body, name=name, grid=(N // bn, M // bm), in_specs=in_specs,
                  out_specs=pl.BlockSpec((bm, bn), lambda j, i: (i, j)),
                  out_shape=jax.ShapeDtypeStruct((M, N), out_dtype),
                  compiler_params=_params(("parallel", "parallel")))(*ops)


def _dw_col_shards(a, b, name):
    S, K = a.shape
    Np = b.shape[1] // N_CHIPS

    def body(a_ref, b_ref, o_ref):
        o_ref[...] = _dot(a_ref[...], b_ref[...], TN).astype(BF16)

    return _pcall(body, name=name, grid=(N_CHIPS,),
                  in_specs=[pl.BlockSpec((S, K), lambda j: (0, 0)), pl.BlockSpec((S, Np), lambda j: (0, j))],
                  out_specs=pl.BlockSpec((None, K, Np), lambda j: (j, 0, 0)),
                  out_shape=jax.ShapeDtypeStruct((N_CHIPS, K, Np), BF16),
                  compiler_params=_params(("parallel",)))(a, b)


def _ew(fn, ins, outs, name, br=256):
    S = max(a.shape[0] for a in ins)
    br = _row_block(S, br)
    n_in = len(ins)

    def body(*refs):
        res = fn(*[r[...] for r in refs[:n_in]])
        if not isinstance(res, tuple):
            res = (res,)
        for r, v in zip(refs[n_in:], res):
            r[...] = v.astype(r.dtype)

    in_specs = [pl.BlockSpec((br, a.shape[1]), lambda i: (i, 0)) if a.shape[0] == S and S > 1
                else pl.BlockSpec(a.shape, lambda i: (0, 0)) for a in ins]
    out = _pcall(body, name=name, grid=(S // br,), in_specs=in_specs,
                 out_specs=tuple(pl.BlockSpec((br, w), lambda i: (i, 0)) for w, _ in outs),
                 out_shape=tuple(jax.ShapeDtypeStruct((S, w), dt) for w, dt in outs),
                 compiler_params=_params(("parallel",)))(*ins)
    return out if len(outs) > 1 else out[0]


def _norm_fwd(x, g, name):
    return _ew(lambda xv, gv: xv * _rstd(xv) * gv, [x, g], [(x.shape[1], BF16)], name, br=512)


def _resid_norm(x, y, g, alpha, name):
    return _ew(lambda xv, yv, gv: xv + alpha * (yv * _rstd(yv) * gv), [x, y, g], [(x.shape[1], F32)], name)


def _norm_bwd(xin, g, dy, alpha, resid, out_dtype, name):
    S, W = xin.shape
    br = _row_block(S, 256)
    has_res = resid is not None

    def body(*refs):
        x_ref, g_ref, dy_ref = refs[:3]
        dx_ref, dg_ref = refs[-2:]
        xv, dyv = x_ref[...], dy_ref[...] * alpha
        r = _rstd(xv)
        xh = xv * r
        u = dyv * g_ref[...]
        dx = r * (u - xh * jnp.mean(u * xh, axis=-1, keepdims=True))
        if has_res:
            dx = dx + refs[3][...]
        dx_ref[...] = dx.astype(dx_ref.dtype)
        part = jnp.sum(dyv * xh, axis=0, keepdims=True)

        @pl.when(pl.program_id(0) == 0)
        def _():
            dg_ref[...] = part

        @pl.when(pl.program_id(0) > 0)
        def _():
            dg_ref[...] += part

    row = pl.BlockSpec((br, W), lambda i: (i, 0))
    vec = pl.BlockSpec((1, W), lambda i: (0, 0))
    ops = [xin, g, dy] + ([resid] if has_res else [])
    return _pcall(body, name=name, grid=(S // br,), in_specs=[row, vec, row] + ([row] if has_res else []),
                  out_specs=(row, vec),
                  out_shape=(jax.ShapeDtypeStruct((S, W), out_dtype), jax.ShapeDtypeStruct((1, W), F32)),
                  compiler_params=_params(("arbitrary",)))(*ops)


def _ffn_gate_up(h, wgt, wut, name):
    S, K = h.shape
    F = wgt.shape[0]
    bm, bn = _row_block(S, 512), _row_block(F, 1408, 128)

    def body(h_ref, wg_ref, wu_ref, g_ref, u_ref, a_ref):
        hv = h_ref[...]
        g = _dot(hv, wg_ref[...], NT)
        u = _dot(hv, wu_ref[...], NT)
        g_ref[...] = g.astype(BF16)
        u_ref[...] = u.astype(BF16)
        a_ref[...] = (g * _sigmoid(g) * u).astype(BF16)

    blk = pl.BlockSpec((bm, bn), lambda n, i: (i, n))
    wsp = pl.BlockSpec((bn, K), lambda n, i: (n, 0))
    return _pcall(body, name=name, grid=(F // bn, S // bm),
                  in_specs=[pl.BlockSpec((bm, K), lambda n, i: (i, 0)), wsp, wsp],
                  out_specs=(blk, blk, blk), out_shape=(jax.ShapeDtypeStruct((S, F), BF16),) * 3,
                  compiler_params=_params(("parallel", "parallel")))(h, wgt, wut)


def _ffn_bwd_mid(dy, wd, g, u, name):
    S, D = dy.shape
    F = g.shape[1]
    bm, bn = _row_block(S, 1024), 256

    def body(dy_ref, wd_ref, g_ref, u_ref, dg_ref, du_ref):
        da = _dot(dy_ref[...], wd_ref[...], NT)
        gv, uv = g_ref[...].astype(F32), u_ref[...].astype(F32)
        s = _sigmoid(gv)
        dg_ref[...] = (da * uv * (s * (1.0 + gv * (1.0 - s)))).astype(BF16)
        du_ref[...] = (da * (gv * s)).astype(BF16)

    blk = pl.BlockSpec((bm, bn), lambda j, i: (i, j))
    return _pcall(body, name=name, grid=(F // bn, S // bm),
                  in_specs=[pl.BlockSpec((bm, D), lambda j, i: (i, 0)),
                            pl.BlockSpec((bn, D), lambda j, i: (j, 0)), blk, blk],
                  out_specs=(blk, blk), out_shape=(jax.ShapeDtypeStruct((S, F), BF16),) * 2,
                  compiler_params=_params(("parallel", "parallel")))(dy, wd, g, u)


def _ple_proj(p16, w):
    S, K = p16.shape
    bm = _row_block(S, 1024)

    def body(p_ref, w_ref, o_ref):
        o_ref[...] = _dot(p_ref[...], w_ref[...])

    return _pcall(body, name="ple_proj", grid=(N_CHIPS, S // bm),
                  in_specs=[pl.BlockSpec((bm, K), lambda j, i: (i, 0)),
                            pl.BlockSpec((None, K, 256), lambda j, i: (j, 0, 0))],
                  out_specs=pl.BlockSpec((bm, 256), lambda j, i: (i, j)),
                  out_shape=jax.ShapeDtypeStruct((S, N_CHIPS * 256), F32),
                  compiler_params=_params(("parallel", "parallel")))(p16, w)


def _loss_and_grad(xf, tgt):
    S, W = xf.shape
    br = _row_block(S, 256)

    def body(x_ref, t_ref, l_ref, d_ref):
        d = x_ref[...] - t_ref[...]
        d_ref[...] = d * (1.0 / W)
        part = 0.5 * jnp.sum(jnp.sum(d * d, axis=-1, keepdims=True) * (1.0 / W), axis=0, keepdims=True)

        @pl.when(pl.program_id(0) == 0)
        def _():
            l_ref[...] = part

        @pl.when(pl.program_id(0) > 0)
        def _():
            l_ref[...] += part

    row = pl.BlockSpec((br, W), lambda i: (i, 0))
    return _pcall(body, name="loss", grid=(S // br,), in_specs=[row, row],
                  out_specs=(pl.BlockSpec((1, 1), lambda i: (0, 0)), row),
                  out_shape=(jax.ShapeDtypeStruct((1, 1), F32), jax.ShapeDtypeStruct((S, W), F32)),
                  compiler_params=_params(("arbitrary",)))(xf, tgt)


def _adamw(w, g, m, v, name):
    L, R, C = w.shape
    Cp = g.shape[2]
    br = _row_block(R, 256, 8) if R % 8 == 0 else R

    def body(w_ref, g_ref, m_ref, v_ref, go_ref, d_ref, nm_ref, nv_ref):
        gv = g_ref[...][:, :C]
        nm = ADAM_B1 * m_ref[...] + (1.0 - ADAM_B1) * gv
        nv = ADAM_B2 * v_ref[...] + (1.0 - ADAM_B2) * (gv * gv)
        m_hat = nm / (1.0 - ADAM_B1 ** ADAM_STEP)
        v_hat = nv / (1.0 - ADAM_B2 ** ADAM_STEP)
        go_ref[...] = gv
        d_ref[...] = -ADAM_LR * (m_hat / (jnp.sqrt(v_hat) + ADAM_EPS) + ADAM_WD * w_ref[...])
        nm_ref[...] = nm
        nv_ref[...] = nv

    blk = pl.BlockSpec((None, br, C), lambda l, i: (l, i, 0))
    gblk = pl.BlockSpec((None, br, Cp), lambda l, i: (l, i, 0))
    return _pcall(body, name=name, grid=(L, R // br), in_specs=[blk, gblk, blk, blk], out_specs=(blk,) * 4,
                  out_shape=(jax.ShapeDtypeStruct((L, R, C), F32),) * 4,
                  compiler_params=_params(("parallel", "parallel")))(w, g, m, v)


def _rope_tables(pos, inv_pat, sign_pat):
    def fn(p, iv, sg):
        ang = p.astype(F32) * iv
        return jnp.cos(ang), jnp.sin(ang) * sg

    return _ew(fn, [pos, inv_pat, sign_pat], [(LANES, F32)] * 2, "rope_tables")


def _swap_halves_of_rope(v):
    W = v.shape[1]
    lane = lax.broadcasted_iota(jnp.int32, (1, W), 1) % LANES
    return jnp.where((lane >= 64) & (lane < 80), pltpu.roll(v, W - ROPE_HALF, 1),
                     jnp.where((lane >= 80) & (lane < 96), pltpu.roll(v, ROPE_HALF, 1), 0.0))


def _tile_lanes(v, n):
    return jnp.concatenate([v] * n, axis=1)


def _mla_prep(qf, kv, z, cp, sp):
    S = qf.shape[0]
    br = _row_block(S, 256)
    W = MLA_HEADS * LANES

    def body(q_ref, k_ref, r_ref, c_ref, s_ref, qo_ref, ko_ref):
        c, s = c_ref[...], s_ref[...]
        q = q_ref[...]
        qo_ref[...] = (q * _tile_lanes(c, MLA_HEADS) + _swap_halves_of_rope(q) * _tile_lanes(s, MLA_HEADS)).astype(BF16)
        r = r_ref[...]
        lane = lax.broadcasted_iota(jnp.int32, (1, LANES), 1)
        kr = jnp.where((lane >= 64) & (lane < 96), r * c + _swap_halves_of_rope(r) * s, 0.0)
        ko_ref[...] = (k_ref[...] + _tile_lanes(kr, MLA_HEADS)).astype(BF16)

    wide = pl.BlockSpec((br, W), lambda i: (i, 0))
    one = pl.BlockSpec((br, LANES), lambda i: (i, 0))
    return _pcall(body, name="mla_prep", grid=(S // br,),
                  in_specs=[wide, wide, pl.BlockSpec((br, LANES), lambda i: (i, Z_KR // LANES)), one, one],
                  out_specs=(wide, wide), out_shape=(jax.ShapeDtypeStruct((S, W), BF16),) * 2,
                  compiler_params=_params(("parallel",)))(qf, kv, z, cp, sp)


def _mla_prep_bwd(dq, dk, cp, sp):
    S, W = dq.shape
    br = _row_block(S, 256)

    def body(dq_ref, dk_ref, c_ref, s_ref, dqo_ref, dr_ref):
        c, s = c_ref[...], s_ref[...]
        d = dq_ref[...]
        dqo_ref[...] = (d * _tile_lanes(c, MLA_HEADS) + _swap_halves_of_rope(d * _tile_lanes(s, MLA_HEADS))).astype(BF16)
        dkv = dk_ref[...]
        tot = dkv[:, 0:LANES]
        for h in range(1, MLA_HEADS):
            tot = tot + dkv[:, h * LANES:(h + 1) * LANES]
        lane = lax.broadcasted_iota(jnp.int32, (1, LANES), 1)
        tot = jnp.where((lane >= 64) & (lane < 96), tot, 0.0)
        dr_ref[...] = tot * c + _swap_halves_of_rope(tot * s)

    wide = pl.BlockSpec((br, W), lambda i: (i, 0))
    one = pl.BlockSpec((br, LANES), lambda i: (i, 0))
    return _pcall(body, name="mla_prep_bwd", grid=(S // br,), in_specs=[wide, wide, one, one], out_specs=(wide, one),
                  out_shape=(jax.ShapeDtypeStruct((S, W), BF16), jax.ShapeDtypeStruct((S, LANES), F32)),
                  compiler_params=_params(("parallel",)))(dq, dk, cp, sp)


def _shift_down(v, k, row):
    return jnp.where(row >= k, pltpu.roll(v, k, 0), 0.0)


def _shift_up(v, k, row):
    n = v.shape[0]
    return jnp.where(row < n - k, pltpu.roll(v, n - k, 0), 0.0)


def _conv_fwd(z, w):
    S = z.shape[0]
    c0 = Z_CV // LANES

    def body(b_ref, c_ref, h_ref, w_ref, y_ref):
        u = c_ref[...] * h_ref[...]
        row = lax.broadcasted_iota(jnp.int32, u.shape, 0)
        wv = w_ref[...]
        conv = wv[0:1] * _shift_down(u, 2, row) + wv[1:2] * _shift_down(u, 1, row) + wv[2:3] * u
        y_ref[...] = b_ref[...] * conv

    def col(k):
        return pl.BlockSpec((S, LANES), lambda j: (0, c0 + 2 * k + j))

    return _pcall(body, name="conv_fwd", grid=(2,),
                  in_specs=[col(0), col(1), col(2), pl.BlockSpec((3, LANES), lambda j: (0, j))],
                  out_specs=pl.BlockSpec((S, LANES), lambda j: (0, j)),
                  out_shape=jax.ShapeDtypeStruct((S, 256), F32), compiler_params=_params(("parallel",)))(z, z, z, w)


def _conv_bwd(z, w, dc_all):
    S = z.shape[0]
    c0 = Z_CV // LANES

    def body(b_ref, c_ref, h_ref, w_ref, dy_ref, db_ref, dc_ref, dh_ref, dw_ref):
        cv, hv, bv, dyv = c_ref[...], h_ref[...], b_ref[...], dy_ref[...]
        u = cv * hv
        row = lax.broadcasted_iota(jnp.int32, u.shape, 0)
        wv = w_ref[...]
        u1, u2 = _shift_down(u, 1, row), _shift_down(u, 2, row)
        conv = wv[0:1] * u2 + wv[1:2] * u1 + wv[2:3] * u
        db_ref[...] = dyv * conv
        dconv = dyv * bv
        du = wv[2:3] * dconv + wv[1:2] * _shift_up(dconv, 1, row) + wv[0:1] * _shift_up(dconv, 2, row)
        dc_ref[...] = du * hv
        dh_ref[...] = du * cv
        dw_ref[0:1, :] = jnp.sum(dconv * u2, axis=0, keepdims=True)
        dw_ref[1:2, :] = jnp.sum(dconv * u1, axis=0, keepdims=True)
        dw_ref[2:3, :] = jnp.sum(dconv * u, axis=0, keepdims=True)

    def col(k):
        return pl.BlockSpec((S, LANES), lambda j: (0, c0 + 2 * k + j))

    wsp = pl.BlockSpec((3, LANES), lambda j: (0, j))
    osp = pl.BlockSpec((S, LANES), lambda j: (0, j))
    db, dc, dh, dw = _pcall(
        body, name="conv_bwd", grid=(2,),
        in_specs=[col(0), col(1), col(2), wsp, pl.BlockSpec((S, LANES), lambda j: (0, 6 + j))],
        out_specs=(osp, osp, osp, wsp),
        out_shape=(jax.ShapeDtypeStruct((S, 256), F32),) * 3 + (jax.ShapeDtypeStruct((3, 256), F32),),
        compiler_params=_params(("parallel",)))(z, z, z, w, dc_all)
    return db, dc, dh, dw


def _half_masks():
    lane = lax.broadcasted_iota(jnp.int32, (1, LANES), 1)
    return lane < HEAD_DIM, lane >= HEAD_DIM


def _pair(v, p):
    return v[:, p * LANES:(p + 1) * LANES]


def _keep(mask, v):
    return jnp.where(mask, v, jnp.zeros_like(v))


def _lane_col(vals):
    lane = lax.broadcasted_iota(jnp.int32, (1, LANES), 1)
    out = jnp.zeros((vals[0].shape[0], LANES), F32)
    for h, v in enumerate(vals):
        out = jnp.where(lane == h, v, out)
    return out


def _mla_fwd(q, k, kv):
    S = q.shape[0]
    H = MLA_HEADS
    bq, bk = _row_block(S, MLA_BQ), _row_block(S, MLA_BK)
    per = bq // bk

    def body(q_ref, k_ref, v_ref, o_ref, lse_ref):
        i = pl.program_id(0)
        lo, hi = _half_masks()
        qb = q_ref[...]

        def block(j, carry, ok):
            ms, ls, accs = carry
            off = pl.multiple_of(j * bk, bk)
            kb = k_ref[pl.ds(off, bk), :]
            vb = v_ref[pl.ds(off, bk), :].astype(BF16)
            ss = [_dot(_pair(qb, h), _pair(kb, h), NT) * MLA_SCALE for h in range(H)]
            if ok is not None:
                ss = [jnp.where(ok, s, NEG_INF) for s in ss]
            ms2 = [jnp.maximum(ms[h], jnp.max(ss[h], axis=-1, keepdims=True)) for h in range(H)]
            ps = [jnp.exp(ss[h] - ms2[h]) for h in range(H)]
            al = [jnp.exp(ms[h] - ms2[h]) for h in range(H)]
            ls2 = [al[h] * ls[h] + jnp.sum(ps[h], axis=-1, keepdims=True) for h in range(H)]
            pvs = [_dot(ps[h], _pair(vb, h // 2)) for h in range(H)]
            accs2 = []
            for p in range(H // 2):
                scale = jnp.where(lo, al[2 * p], al[2 * p + 1])
                accs2.append(scale * accs[p] + jnp.where(lo, pvs[2 * p], pvs[2 * p + 1]))
            return tuple(ms2), tuple(ls2), tuple(accs2)

        init = (tuple(jnp.full((bq, 1), NEG_INF, F32) for _ in range(H)),
                tuple(jnp.zeros((bq, 1), F32) for _ in range(H)),
                tuple(jnp.zeros((bq, LANES), F32) for _ in range(H // 2)))
        carry = lax.fori_loop(0, i * per, lambda j, c: block(j, c, None), init)
        row = lax.broadcasted_iota(jnp.int32, (bq, bk), 0)
        col = lax.broadcasted_iota(jnp.int32, (bq, bk), 1)
        for t in range(per):
            carry = block(i * per + t, carry, col + t * bk <= row)
        ms, ls, accs = carry
        o_ref[...] = jnp.concatenate(
            [accs[p] / jnp.where(lo, ls[2 * p], ls[2 * p + 1]) for p in range(H // 2)], axis=1)
        lse_ref[...] = _lane_col([ms[h] + jnp.log(ls[h]) for h in range(H)])

    return _pcall(body, name="mla_fwd", grid=(S // bq,),
                  in_specs=[pl.BlockSpec((bq, H * LANES), lambda i: (i, 0)),
                            pl.BlockSpec((S, H * LANES), lambda i: (0, 0)),
                            pl.BlockSpec((S, H * HEAD_DIM), lambda i: (0, 2))],
                  out_specs=(pl.BlockSpec((bq, H * HEAD_DIM), lambda i: (i, 0)),
                             pl.BlockSpec((bq, LANES), lambda i: (i, 0))),
                  out_shape=(jax.ShapeDtypeStruct((S, H * HEAD_DIM), F32), jax.ShapeDtypeStruct((S, LANES), F32)),
                  compiler_params=_params(("parallel",)))(q, k, kv)


def _mla_bwd(q, k, kv, o, lse, dc_all):
    S = q.shape[0]
    H = MLA_HEADS
    bq, bk = _row_block(S, MLA_BQ), _row_block(S, MLA_BK)
    per = bq // bk

    def body(q_ref, k_ref, v_ref, o_ref, lse_ref, doa_ref, dob_ref, dq_ref, dk_ref, dv_ref):
        i = pl.program_id(0)

        @pl.when(i == 0)
        def _():
            dk_ref[...] = jnp.zeros_like(dk_ref)
            dv_ref[...] = jnp.zeros_like(dv_ref)

        lo, hi = _half_masks()
        qb = q_ref[...]
        dob = jnp.concatenate([doa_ref[...], dob_ref[...]], axis=1)
        prod = dob * o_ref[...]
        lse_v = lse_ref[...]
        do16 = dob.astype(BF16)
        dom, deltas, lses = [], [], []
        for h in range(H):
            mk = lo if h % 2 == 0 else hi
            dom.append(_keep(mk, _pair(do16, h // 2)))
            deltas.append(jnp.sum(_keep(mk, _pair(prod, h // 2)), axis=-1, keepdims=True))
            lses.append(lse_v[:, h:h + 1])

        def block(j, dqs, ok):
            off = pl.multiple_of(j * bk, bk)
            kb = k_ref[pl.ds(off, bk), :]
            vb = v_ref[pl.ds(off, bk), :].astype(BF16)
            ss = [_dot(_pair(qb, h), _pair(kb, h), NT) * MLA_SCALE for h in range(H)]
            dps = [_dot(dom[h], _pair(vb, h // 2), NT) for h in range(H)]
            ps = [jnp.exp(ss[h] - lses[h]) for h in range(H)]
            if ok is not None:
                ps = [jnp.where(ok, p, 0.0) for p in ps]
            ds16 = [(ps[h] * (dps[h] - deltas[h]) * MLA_SCALE).astype(BF16) for h in range(H)]
            p16 = [p.astype(BF16) for p in ps]
            dks = [_dot(ds16[h], _pair(qb, h), TN) for h in range(H)]
            dvps = [_dot(p16[h], dom[h], TN) for h in range(H)]
            dqs2 = [dqs[h] + _dot(ds16[h], _pair(kb, h)) for h in range(H)]
            dk_ref[pl.ds(off, bk), :] += jnp.concatenate(dks, axis=1)
            dv_ref[pl.ds(off, bk), :] += jnp.concatenate(
                [dvps[2 * p] + dvps[2 * p + 1] for p in range(H // 2)], axis=1)
            return tuple(dqs2)

        dqs = lax.fori_loop(0, i * per, lambda j, c: block(j, c, None),
                            tuple(jnp.zeros((bq, LANES), F32) for _ in range(H)))
        row = lax.broadcasted_iota(jnp.int32, (bq, bk), 0)
        col = lax.broadcasted_iota(jnp.int32, (bq, bk), 1)
        for t in range(per):
            dqs = block(i * per + t, dqs, col + t * bk <= row)
        dq_ref[...] = jnp.concatenate(dqs, axis=1)

    wide = pl.BlockSpec((bq, H * LANES), lambda i: (i, 0))
    full = pl.BlockSpec((S, H * LANES), lambda i: (0, 0))
    return _pcall(body, name="mla_bwd", grid=(S // bq,),
                  in_specs=[wide, full, pl.BlockSpec((S, H * HEAD_DIM), lambda i: (0, 2)),
                            pl.BlockSpec((bq, H * HEAD_DIM), lambda i: (i, 0)),
                            pl.BlockSpec((bq, LANES), lambda i: (i, 0)),
                            pl.BlockSpec((bq, 256), lambda i: (i, 1)), pl.BlockSpec((bq, 256), lambda i: (i, 2))],
                  out_specs=(wide, full, pl.BlockSpec((S, H * HEAD_DIM), lambda i: (0, 0))),
                  out_shape=(jax.ShapeDtypeStruct((S, H * LANES), F32), jax.ShapeDtypeStruct((S, H * LANES), F32),
                             jax.ShapeDtypeStruct((S, H * HEAD_DIM), F32)),
                  compiler_params=_params(("arbitrary",)))(q, k, kv, o, lse, dc_all, dc_all)


def _dot_exact(x, tri):
    h1 = x.astype(BF16)
    h2 = (x - h1.astype(F32)).astype(BF16)
    return _dot(h1, tri) + _dot(h2, tri)


def _softplus(z):
    return jnp.maximum(z, 0.0) + jnp.log(1.0 + jnp.exp(-jnp.abs(z)))


def _sb_fwd(z):
    S = z.shape[0]
    H = SB_HEADS
    bq, bk = _row_block(S, SB_BQ), _row_block(S, SB_BK)
    per = bq // bk

    def body(q_ref, k_ref, v_ref, o_ref, t_ref):
        i = pl.program_id(0)
        lo, hi = _half_masks()
        q16 = (q_ref[...] * SB_SCALE).astype(BF16)
        qm = [_keep(lo if h % 2 == 0 else hi, _pair(q16, h // 2)) for h in range(H)]
        rr = lax.broadcasted_iota(jnp.int32, (bk, bk), 0)
        cc = lax.broadcasted_iota(jnp.int32, (bk, bk), 1)
        later = (rr > cc).astype(BF16)

        def block(j, carry, mask):
            tails, accs = carry
            off = pl.multiple_of(j * bk, bk)
            kb = k_ref[pl.ds(off, bk), :].astype(BF16)
            vb = v_ref[pl.ds(off, bk), :].astype(BF16)
            zs = [_dot(qm[h], _pair(kb, h // 2), NT) for h in range(H)]
            sps = [_softplus(z) for z in zs]
            lnots = [-sp if mask is None else jnp.where(mask, -sp, 0.0) for sp in sps]
            sums = [_dot_exact(lnot, later) for lnot in lnots]
            ws = []
            for h in range(H):
                w = jnp.exp((zs[h] - sps[h]) + (tails[h] + sums[h]))
                ws.append((w if mask is None else jnp.where(mask, w, 0.0)).astype(BF16))
            pvs = [_dot(ws[h], _pair(vb, h // 2)) for h in range(H)]
            accs2 = [accs[p] + jnp.where(lo, pvs[2 * p], pvs[2 * p + 1]) for p in range(H // 2)]
            tails2 = [tails[h] + jnp.sum(lnots[h], axis=-1, keepdims=True) for h in range(H)]
            return tuple(tails2), tuple(accs2)

        carry = (tuple(jnp.zeros((bq, 1), F32) for _ in range(H)),
                 tuple(jnp.zeros((bq, LANES), F32) for _ in range(H // 2)))
        row = lax.broadcasted_iota(jnp.int32, (bq, bk), 0)
        col = lax.broadcasted_iota(jnp.int32, (bq, bk), 1)
        for t in range(per):
            carry = block(i * per + per - 1 - t, carry, col + (per - 1 - t) * bk < row)
        tails, accs = lax.fori_loop(0, i * per, lambda t, c: block(i * per - 1 - t, c, None), carry)
        o_ref[...] = jnp.concatenate(accs, axis=1)
        t_ref[...] = _lane_col(tails)

    return _pcall(body, name="sb_fwd", grid=(S // bq,),
                  in_specs=[pl.BlockSpec((bq, 256), lambda i: (i, 0)), pl.BlockSpec((S, 256), lambda i: (0, 1)),
                            pl.BlockSpec((S, 256), lambda i: (0, 2))],
                  out_specs=(pl.BlockSpec((bq, 256), lambda i: (i, 0)), pl.BlockSpec((bq, LANES), lambda i: (i, 0))),
                  out_shape=(jax.ShapeDtypeStruct((S, 256), F32), jax.ShapeDtypeStruct((S, LANES), F32)),
                  compiler_params=_params(("parallel",)))(z, z, z)


def _sb_bwd(z, tot, dc_all):
    S = z.shape[0]
    H = SB_HEADS
    bq, bk = _row_block(S, SB_BQ), _row_block(S, SB_BK)
    per = bq // bk

    def body(q_ref, k_ref, v_ref, t_ref, do_ref, dq_ref, dk_ref, dv_ref):
        i = pl.program_id(0)

        @pl.when(i == 0)
        def _():
            dk_ref[...] = jnp.zeros_like(dk_ref)
            dv_ref[...] = jnp.zeros_like(dv_ref)

        lo, hi = _half_masks()
        q16 = (q_ref[...] * SB_SCALE).astype(BF16)
        do16 = do_ref[...].astype(BF16)
        tot_v = t_ref[...]
        masks = [lo if h % 2 == 0 else hi for h in range(H)]
        qm = [_keep(masks[h], _pair(q16, h // 2)) for h in range(H)]
        dom = [_keep(masks[h], _pair(do16, h // 2)) for h in range(H)]
        tots = [tot_v[:, h:h + 1] for h in range(H)]
        rr = lax.broadcasted_iota(jnp.int32, (bk, bk), 0)
        cc = lax.broadcasted_iota(jnp.int32, (bk, bk), 1)
        upto = (rr <= cc).astype(BF16)
        before = (rr < cc).astype(BF16)

        def block(j, carry, mask):
            heads, eheads, dqs = carry
            off = pl.multiple_of(j * bk, bk)
            kb = k_ref[pl.ds(off, bk), :].astype(BF16)
            vb = v_ref[pl.ds(off, bk), :].astype(BF16)
            zs = [_dot(qm[h], _pair(kb, h // 2), NT) for h in range(H)]
            dws = [_dot(dom[h], _pair(vb, h // 2), NT) for h in range(H)]
            sps = [_softplus(z) for z in zs]
            lnots = [-sp if mask is None else jnp.where(mask, -sp, 0.0) for sp in sps]
            pres = [_dot_exact(lnot, upto) for lnot in lnots]
            lsigs = [zs[h] - sps[h] for h in range(H)]
            ws = [jnp.exp(lsigs[h] + (tots[h] - (heads[h] + pres[h]))) for h in range(H)]
            if mask is not None:
                ws = [jnp.where(mask, w, 0.0) for w in ws]
            es = [ws[h] * dws[h] for h in range(H)]
            esums = [eheads[h] + _dot_exact(es[h], before) for h in range(H)]
            dz16 = []
            for h in range(H):
                sig = jnp.exp(lsigs[h])
                dz = es[h] * (1.0 - sig) - sig * esums[h]
                dz16.append((dz if mask is None else jnp.where(mask, dz, 0.0)).astype(BF16))
            w16 = [w.astype(BF16) for w in ws]
            dkps = [_dot(dz16[h], qm[h], TN) for h in range(H)]
            dvps = [_dot(w16[h], dom[h], TN) for h in range(H)]
            dqps = [_dot(dz16[h], _pair(kb, h // 2)) for h in range(H)]
            dk_ref[pl.ds(off, bk), :] += jnp.concatenate([dkps[2 * p] + dkps[2 * p + 1] for p in range(H // 2)], axis=1)
            dv_ref[pl.ds(off, bk), :] += jnp.concatenate([dvps[2 * p] + dvps[2 * p + 1] for p in range(H // 2)], axis=1)
            dqs2 = [dqs[p] + jnp.where(lo, dqps[2 * p], dqps[2 * p + 1]) for p in range(H // 2)]
            heads2 = [heads[h] + jnp.sum(lnots[h], axis=-1, keepdims=True) for h in range(H)]
            eheads2 = [eheads[h] + jnp.sum(es[h], axis=-1, keepdims=True) for h in range(H)]
            return tuple(heads2), tuple(eheads2), tuple(dqs2)

        zeros = tuple(jnp.zeros((bq, 1), F32) for _ in range(H))
        init = (zeros, zeros, tuple(jnp.zeros((bq, LANES), F32) for _ in range(H // 2)))
        carry = lax.fori_loop(0, i * per, lambda j, c: block(j, c, None), init)
        row = lax.broadcasted_iota(jnp.int32, (bq, bk), 0)
        col = lax.broadcasted_iota(jnp.int32, (bq, bk), 1)
        for t in range(per):
            carry = block(i * per + t, carry, col + t * bk < row)
        dq_ref[...] = jnp.concatenate(carry[2], axis=1) * SB_SCALE

    blk = pl.BlockSpec((bq, 256), lambda i: (i, 0))
    full = pl.BlockSpec((S, 256), lambda i: (0, 0))
    return _pcall(body, name="sb_bwd", grid=(S // bq,),
                  in_specs=[blk, pl.BlockSpec((S, 256), lambda i: (0, 1)), pl.BlockSpec((S, 256), lambda i: (0, 2)),
                            pl.BlockSpec((bq, LANES), lambda i: (i, 0)), blk],
                  out_specs=(blk, full, full), out_shape=(jax.ShapeDtypeStruct((S, 256), F32),) * 3,
                  compiler_params=_params(("arbitrary",)))(z, z, z, tot, dc_all)


_HBM = pl.BlockSpec(memory_space=pltpu.HBM)
_SEM = pl.BlockSpec(memory_space=pltpu.SEMAPHORE)
_ANY = pl.BlockSpec(memory_space=pl.ANY)


def _place():
    x, y, c = lax.axis_index("x"), lax.axis_index("y"), lax.axis_index("c")
    other_chips = ((1 - x, y), (x, 1 - y), (1 - x, 1 - y))
    return x, y, c, other_chips


def _remote(src, dst, send_sems, recv_sems, k, dev):
    return pltpu.make_async_remote_copy(src_ref=src, dst_ref=dst, send_sem=send_sems.at[k], recv_sem=recv_sems.at[k],
                                        device_id=dev, device_id_type=MESH)


def _half_rows(c, n_rows):
    half = n_rows // 2
    return pl.ds(pl.multiple_of(c * half, 16), half)


def _gather_copies(src, land, send_sems, recv_sems):
    x, y, c, chips = _place()
    me = 2 * x + y
    cps = []
    for w in range(len(src)):
        rows = _half_rows(c, src[w].shape[0])
        for k, (cx, cy) in enumerate(chips):
            cps.append(_remote(src[w].at[rows], land[w].at[me, rows], send_sems, recv_sems, 3 * w + k, (cx, cy, c)))
    return cps


def _scatter_copies(src, land, send_sems, recv_sems):
    x, y, c, chips = _place()
    cps = []
    for w in range(len(src)):
        for k, (cx, cy) in enumerate(chips):
            cps.append(_remote(src[w].at[2 * cx + cy], land[w].at[k], send_sems, recv_sems, 3 * w + k, (cx, cy, c)))
    return cps


def _exchange_start(copies, srcs, lands, after, name):
    n = len(srcs)

    def body(*refs):
        src, land = refs[:n], refs[n:2 * n]
        send_sems, recv_sems = refs[2 * n + 1], refs[2 * n + 2]
        token = refs[-1]
        for cp in copies(src, land, send_sems, recv_sems):
            cp.start()
        token[...] = jnp.zeros_like(token)

    thru = tuple(pltpu.HBM(a.shape, a.dtype) for a in list(srcs) + list(lands))
    out = _pcall(body, name=name,
                 out_shape=(pltpu.SemaphoreType.DMA((3 * n,)), pltpu.SemaphoreType.DMA((3 * n,))) + thru
                 + (jax.ShapeDtypeStruct((8, LANES), F32),),
                 in_specs=[_HBM] * (2 * n) + [_ANY],
                 out_specs=(_SEM, _SEM) + (_HBM,) * (2 * n) + (pl.BlockSpec(memory_space=pltpu.VMEM),),
                 input_output_aliases={i: 2 + i for i in range(2 * n)},
                 compiler_params=pltpu.CompilerParams(has_side_effects=pltpu.SideEffectType.DATAFLOW_SIDE_EFFECTING))(
        *[pltpu.with_memory_space_constraint(a, pltpu.HBM) for a in list(srcs) + list(lands)], after)
    return out[0], out[1], list(out[2:2 + n]), list(out[2 + n:2 + 2 * n]), out[-1]


def _exchange_wait(copies, send_sems, recv_sems, srcs, lands, after, name):
    n = len(srcs)

    def body(*refs):
        src, land = refs[:n], refs[n:2 * n]
        ssem, rsem = refs[2 * n], refs[2 * n + 1]
        for cp in copies(src, land, ssem, rsem):
            cp.wait_send()
            cp.wait_recv()

    out = _pcall(body, name=name, out_shape=tuple(pltpu.HBM(a.shape, a.dtype) for a in list(srcs) + list(lands)),
                 in_specs=[_HBM] * (2 * n) + [_SEM, _SEM, _ANY], out_specs=(_HBM,) * (2 * n),
                 input_output_aliases={i: i for i in range(2 * n)},
                 compiler_params=pltpu.CompilerParams(has_side_effects=pltpu.SideEffectType.DATAFLOW_SIDE_EFFECTING))(
        *srcs, *lands, send_sems, recv_sems, after)
    return list(out[:n]), list(out[n:])


def _forward_rows(gathered):
    n = len(gathered)

    def body(*refs):
        src, dst = refs[:n], refs[n:2 * n]
        send_sems, recv_sems = refs[2 * n:]
        x, y, c, chips = _place()
        sib = (x, y, 1 - c)
        cps = []
        for w in range(n):
            K = src[w].shape[1]
            for k, (cx, cy) in enumerate(chips):
                blk = (2 * cx + cy, _half_rows(c, K))
                cps.append(_remote(src[w].at[blk], dst[w].at[blk], send_sems, recv_sems, 3 * w + k, sib))
        for cp in cps:
            cp.start()
        for w in range(n):
            K = src[w].shape[1]
            for k, (cx, cy) in enumerate(chips):
                blk = dst[w].at[2 * cx + cy, _half_rows(1 - c, K)]
                _remote(blk, blk, send_sems, recv_sems, 3 * w + k, sib).wait_recv()
        for cp in cps:
            cp.wait_send()

    return _pcall(body, name="forward_rows", in_specs=[_HBM] * n, out_specs=(_HBM,) * n,
                  out_shape=tuple(jax.ShapeDtypeStruct(g.shape, g.dtype) for g in gathered),
                  input_output_aliases={w: w for w in range(n)},
                  scratch_shapes=[pltpu.SemaphoreType.DMA((3 * n,)), pltpu.SemaphoreType.DMA((3 * n,))])(*gathered)


def _swap_rows(bufs):
    n = len(bufs)

    def body(*refs):
        src, dst = refs[:n], refs[n:2 * n]
        send_sems, recv_sems = refs[2 * n:]
        x, y, c, _ = _place()
        cps = [_remote(src[w].at[:, _half_rows(1 - c, src[w].shape[1])], dst[w], send_sems, recv_sems, w,
                       (x, y, 1 - c)) for w in range(n)]
        for cp in cps:
            cp.start()
        for cp in cps:
            cp.wait()

    return _pcall(body, name="swap_rows", in_specs=[_HBM] * n, out_specs=(_HBM,) * n,
                  out_shape=tuple(jax.ShapeDtypeStruct((b.shape[0], b.shape[1] // 2, b.shape[2]), b.dtype) for b in bufs),
                  scratch_shapes=[pltpu.SemaphoreType.DMA((n,)), pltpu.SemaphoreType.DMA((n,))])(*bufs)


def _join_rows(fins):
    n = len(fins)

    def body(*refs):
        src, dst = refs[:n], refs[n:2 * n]
        send_sems, recv_sems = refs[2 * n:]
        x, y, c, _ = _place()
        sib = (x, y, 1 - c)
        cps = []
        for w in range(n):
            blk = (slice(None), _half_rows(c, src[w].shape[1]))
            cps.append(_remote(src[w].at[blk], dst[w].at[blk], send_sems, recv_sems, w, sib))
        for cp in cps:
            cp.start()
        for w in range(n):
            blk = dst[w].at[:, _half_rows(1 - c, src[w].shape[1])]
            _remote(blk, blk, send_sems, recv_sems, w, sib).wait_recv()
        for cp in cps:
            cp.wait_send()

    return _pcall(body, name="join_rows", in_specs=[_HBM] * n, out_specs=(_HBM,) * n,
                  out_shape=tuple(jax.ShapeDtypeStruct(f.shape, f.dtype) for f in fins),
                  input_output_aliases={w: w for w in range(n)},
                  scratch_shapes=[pltpu.SemaphoreType.DMA((n,)), pltpu.SemaphoreType.DMA((n,))])(*fins)


def _add_rows(buf, got, c_idx):
    _, Kh, N = got.shape
    bk = _row_block(Kh, 256)
    nb = Kh // bk

    def body(c_ref, a_ref, b_ref, o_ref):
        o_ref[...] = (a_ref[...].astype(F32) + b_ref[...].astype(F32)).astype(o_ref.dtype)

    blk = pl.BlockSpec((None, bk, N), lambda j, i, c_ref: (j, i, 0))
    grid_spec = pltpu.PrefetchScalarGridSpec(
        num_scalar_prefetch=1, grid=(N_CHIPS, nb),
        in_specs=[pl.BlockSpec((None, bk, N), lambda j, i, c_ref: (j, c_ref[0] * nb + i, 0)), blk], out_specs=blk)
    return _pcall(body, name="add_rows", grid_spec=grid_spec, out_shape=jax.ShapeDtypeStruct((N_CHIPS, Kh, N), BF16),
                  compiler_params=_params(("parallel", "parallel")))(c_idx, buf, got)


def _add_chip_sums(s1, got, idx, layer, into, after):
    _, Kh, N = s1.shape
    bk = _row_block(Kh, 256)
    nb = Kh // bk

    def body(idx_ref, a_ref, b_ref, *rest):
        acc = a_ref[...].astype(F32)
        for k in range(3):
            acc = acc + b_ref[k].astype(F32)
        rest[-1][...] = acc

    in_specs = [pl.BlockSpec((None, bk, N), lambda i, idx_ref: (idx_ref[0], i, 0)),
                pl.BlockSpec((3, bk, N), lambda i, idx_ref: (0, i, 0))]
    in_specs.append(_ANY)
    ops = [idx, s1, got, after]
    if into is not None:
        in_specs.append(_ANY)
        ops.append(into)
    grid_spec = pltpu.PrefetchScalarGridSpec(
        num_scalar_prefetch=1, grid=(nb,), in_specs=in_specs,
        out_specs=pl.BlockSpec((None, bk, N), lambda i, idx_ref: (layer, idx_ref[1] * nb + i, 0)))
    return _pcall(body, name="add_chip_sums" if into is None else "add_chip_sums_into", grid_spec=grid_spec,
                  out_shape=jax.ShapeDtypeStruct((2, 2 * Kh, N), F32),
                  input_output_aliases={} if into is None else {4: 0},
                  compiler_params=_params(("parallel",)))(*ops)


def _allreduce_small(v, after):
    R, W = v.shape

    def body(v_ref, after_ref, o_ref, buf, send_sems, recv_sems):
        x, y, c, _ = _place()
        me = 4 * x + 2 * y + c
        buf[0] = v_ref[...]
        cps = []
        for r in range(1, 8):
            peer = (x if not r & 4 else 1 - x, y if not r & 2 else 1 - y, c if not r & 1 else 1 - c)
            cp = _remote(v_ref, buf.at[r], send_sems, recv_sems, r - 1, peer)
            cp.start()
            cps.append(cp)
        for cp in cps:
            cp.wait()
        acc = buf[me]
        for d in range(1, 8):
            acc = acc + buf[jnp.bitwise_xor(me, d)]
        o_ref[...] = acc

    return _pcall(body, name="allreduce_small", out_shape=jax.ShapeDtypeStruct((R, W), F32),
                  in_specs=[pl.BlockSpec(memory_space=pltpu.VMEM), _ANY],
                  out_specs=pl.BlockSpec(memory_space=pltpu.VMEM),
                  scratch_shapes=[pltpu.VMEM((8, R, W), F32), pltpu.SemaphoreType.DMA((7,)),
                                  pltpu.SemaphoreType.DMA((7,))])(v, after)


def _cat_cols(g):
    return g.transpose(1, 0, 2).reshape(g.shape[1], -1)


def _cut_cols(w):
    K, N = w.shape
    return w.reshape(K, N_CHIPS, N // N_CHIPS).transpose(1, 0, 2)


def _regroup_w_in_t(g):
    w = g[:, :552].reshape(2208, g.shape[2])
    zeros = lambda n: jnp.zeros((n, w.shape[1]), w.dtype)
    return jnp.concatenate([w[0:768], w[1152:1408], zeros(64), w[1408:1440], zeros(32), w[768:1152], w[1440:2208]],
                           axis=0)


def _ungroup_w_in_t(w):
    nat = jnp.concatenate([w[0:768], w[Z_Q:Z_Q + 384], w[Z_KV:Z_KV + 256], w[Z_KR + 64:Z_KR + 96], w[Z_CV:Z_W]],
                          axis=0)
    return jnp.pad(nat.reshape(N_CHIPS, 552, w.shape[1]), ((0, 0), (0, 24), (0, 0)))


def _regroup_w_uq_t(g):
    K = g.shape[2]
    return jnp.pad(g.reshape(MLA_HEADS, 96, K), ((0, 0), (0, 32), (0, 0))).reshape(MLA_HEADS * LANES, K)


def _ungroup_w_uq_t(w):
    K = w.shape[1]
    return w.reshape(MLA_HEADS, LANES, K)[:, :96].reshape(N_CHIPS, 192, K)


def _regroup_w_ukv(w):
    K = w.shape[0]
    t = w.reshape(K, MLA_HEADS, 128)
    return jnp.concatenate([jnp.pad(t[:, :, :64], ((0, 0), (0, 0), (0, 64))).reshape(K, MLA_HEADS * LANES),
                            t[:, :, 64:].reshape(K, MLA_HEADS * HEAD_DIM)], axis=1)


def _ungroup_w_ukv(w):
    K = w.shape[0]
    return jnp.concatenate([w[:, :MLA_HEADS * LANES].reshape(K, MLA_HEADS, LANES)[:, :, :64],
                            w[:, MLA_HEADS * LANES:].reshape(K, MLA_HEADS, 64)], axis=2).reshape(K, 1024)


def _ffn_fwd(x, g_pre, g_post, fetch, tag):
    W = fetch("ffn%sa" % tag, x)
    h = _norm_fwd(x, g_pre, "norm_fwd")
    G, U, A = _ffn_gate_up(h, W["w%s_gate" % tag], W["w%s_up" % tag], "ffn_gate_up")
    W = fetch("ffn%sb" % tag, A)
    Y = _mm([(A, W["w%s_down" % tag])], "nn", F32, "ffn_down", bm=1024)
    return _resid_norm(x, Y, g_post, 0.5, "resid_norm"), dict(x=x, h=h, G=G, U=U, A=A, Y=Y)


def _ffn_bwd(dxo, sv, g_pre, g_post, wg, wu, wd_rows, ready):
    dY, dg_post = _norm_bwd(sv["Y"], g_post, dxo, 0.5, None, BF16, "norm_bwd_post")
    dG, dU = _ffn_bwd_mid(dY, wd_rows, sv["G"], sv["U"], "ffn_bwd_mid")
    cut = lambda t: t.reshape(N_CHIPS, -1, t.shape[1])
    token = ready((cut(_mm([(dG, sv["h"])], "tn", BF16, "ffn_dw_in")), cut(_mm([(dU, sv["h"])], "tn", BF16, "ffn_dw_in")),
                   cut(_mm([(sv["A"], dY)], "tn", BF16, "ffn_dw_down"))))
    dh = _mm([(dG, wg), (dU, wu)], "nn", F32, "ffn_dh")
    dx, dg_pre = _norm_bwd(sv["x"], _ordered_after(g_pre, token), dh, 1.0, dxo, F32, "norm_bwd_pre")
    return dx, dg_pre, dg_post, token


def _ordered_after(g, token):
    return g if token is None else g + token[0, 0]


GATHER_GROUPS = (("ffn1a", ("w1_gate", "w1_up")), ("ffn1b", ("w1_down",)),
                 ("mix", ("w_in", "w_mla_uq", "w_mla_ukv", "w_out")),
                 ("ffn2a", ("w2_gate", "w2_up", "w2_down", "w_ple_gate", "w_ple_proj")))
SCATTER_GROUPS = (("ffn2", ("w2_gate", "w2_up", "w2_down", "w_ple_gate", "w_ple_proj")),
                  ("mix", ("w_in", "w_mla_uq", "w_mla_ukv", "w_out")),
                  ("ffn1", ("w1_gate", "w1_up", "w1_down")))


def _layer_fwd(x, p_l, fetch, G, w_conv, cp, sp):
    sv = {}
    x1, sv["ffn1"] = _ffn_fwd(x, G["g_ffn1_pre"], G["g_ffn1_post"], fetch, "1")

    W = fetch("mix", x1)
    h2 = _norm_fwd(x1, G["g_mix_pre"], "norm_fwd")
    Z = _mm([(h2, W["w_in"])], "nt", F32, "mix_in")
    o_sb, tot_sb = _sb_fwd(Z)
    cq, ckv = Z[:, Z_Q:Z_Q + 384], Z[:, Z_KV:Z_KV + 256]
    nq = _norm_fwd(cq, G["g_mla_q"], "norm_fwd_q")
    Qf = _mm([(nq, W["w_mla_uq"])], "nt", F32, "mla_uq")
    nkv = _norm_fwd(ckv, G["g_mla_kv"], "norm_fwd_kv")
    KV = _mm([(nkv, W["w_mla_ukv"])], "nn", F32, "mla_ukv")
    qcat, kcat = _mla_prep(Qf, KV, Z, cp, sp)
    o_mla, lse = _mla_fwd(qcat, kcat, KV)
    y_cv = _conv_fwd(Z, w_conv)
    C = jnp.concatenate([o_sb, o_mla, y_cv], axis=1).astype(BF16)
    Mx = _mm([(C, W["w_out"])], "nn", F32, "mix_out")
    x2 = _resid_norm(x1, Mx, G["g_mix_post"], 1.0, "resid_norm")
    sv["mix"] = dict(x=x1, h=h2, Z=Z, tot_sb=tot_sb, cq=cq, ckv=ckv, nq=nq, nkv=nkv, qcat=qcat, kcat=kcat, KV=KV,
                     o_mla=o_mla, lse=lse, C=C, Mx=Mx)

    x3, sv["ffn2"] = _ffn_fwd(x2, G["g_ffn2_pre"], G["g_ffn2_post"], fetch, "2")
    W = fetch("ffn2b", x3)

    h4 = _norm_fwd(x3, G["g_ple_pre"], "norm_fwd")
    p16 = p_l.astype(BF16)
    Qg = _mm([(h4, W["w_ple_gate"])], "nn", F32, "ple_gate")
    Pp = _ple_proj(p16, W["w_ple_proj"])
    e = _ew(lambda q, pp: _sigmoid(q) * pp, [Qg, Pp], [(D_MODEL, F32)], "ple_mul")
    x4 = _resid_norm(x3, e, G["g_ple_post"], 1.0, "resid_norm")
    sv["ple"] = dict(x=x3, h=h4, p16=p16, Qg=Qg, Pp=Pp, e=e)
    sv["W"] = W
    return x4, sv


def _layer_bwd(dx4, sv, G, w_conv, cp, sp, emit, token):
    gg, gw, W = {}, {}, sv["W"]

    s = sv["ple"]
    de, gg["g_ple_post"] = _norm_bwd(s["e"], _ordered_after(G["g_ple_post"], token), dx4, 1.0, None, F32,
                                     "norm_bwd_e")

    def ple_bwd(dev, q, pp):
        sg = _sigmoid(q)
        return dev * pp * sg * (1.0 - sg), dev * sg

    dQg, dPp = _ew(ple_bwd, [de, s["Qg"], s["Pp"]], [(D_MODEL, BF16)] * 2, "ple_mul_bwd")
    cut = lambda t: t.reshape(N_CHIPS, -1, t.shape[1])
    gw["w_ple_proj"] = _dw_col_shards(s["p16"], dPp, "ple_dw_proj")
    gw["w_ple_gate"] = cut(_mm([(s["h"], dQg)], "tn", BF16, "ple_dw_gate"))
    dh4 = _mm([(dQg, W["w_ple_gate"])], "nt", F32, "ple_dh")
    dx3, gg["g_ple_pre"] = _norm_bwd(s["x"], G["g_ple_pre"], dh4, 1.0, dx4, F32, "norm_bwd_pre")

    def ready2(dws):
        gw["w2_gate"], gw["w2_up"], gw["w2_down"] = dws
        return emit("ffn2", gw)

    dx2, gg["g_ffn2_pre"], gg["g_ffn2_post"], token = _ffn_bwd(
        dx3, sv["ffn2"], G["g_ffn2_pre"], G["g_ffn2_post"], W["w2_gate"], W["w2_up"], W["w2_down"], ready2)

    s = sv["mix"]
    dM, gg["g_mix_post"] = _norm_bwd(s["Mx"], _ordered_after(G["g_mix_post"], token), dx2, 1.0, None, BF16,
                                     "norm_bwd_post")
    dC = _mm([(dM, W["w_out"])], "nt", F32, "mix_out_dx")
    gw["w_out"] = cut(_mm([(s["C"], dM)], "tn", BF16, "mix_out_dw"))

    db, dc, dhh, gg["w_conv"] = _conv_bwd(s["Z"], w_conv, dC)

    dqc, dkc, dv = _mla_bwd(s["qcat"], s["kcat"], s["KV"], s["o_mla"], s["lse"], dC)
    dQf, dkr = _mla_prep_bwd(dqc, dkc, cp, sp)
    gw["w_mla_uq"] = _ungroup_w_uq_t(_mm([(dQf, s["nq"])], "tn", BF16, "mla_uq_dw"))
    dnq = _mm([(dQf, W["w_mla_uq"])], "nn", F32, "mla_uq_dx")
    dcq, gg["g_mla_q"] = _norm_bwd(s["cq"], G["g_mla_q"], dnq, 1.0, None, F32, "norm_bwd_q")
    dkv = jnp.concatenate([dkc, dv], axis=1).astype(BF16)
    gw["w_mla_ukv"] = _cut_cols(_ungroup_w_ukv(_mm([(s["nkv"], dkv)], "tn", BF16, "mla_ukv_dw")))
    dnkv = _mm([(dkv, W["w_mla_ukv"])], "nt", F32, "mla_ukv_dx")
    dckv, gg["g_mla_kv"] = _norm_bwd(s["ckv"], G["g_mla_kv"], dnkv, 1.0, None, F32, "norm_bwd_kv")

    dsq, dsk, dsv = _sb_bwd(s["Z"], s["tot_sb"], dC)
    dZ = jnp.concatenate([dsq, dsk, dsv, dckv, dkr, dcq, db, dc, dhh], axis=1).astype(BF16)
    gw["w_in"] = _ungroup_w_in_t(_mm([(dZ, s["h"])], "tn", BF16, "mix_in_dw"))
    dh2 = _mm([(dZ, W["w_in"])], "nn", F32, "mix_in_dx")
    dx1, gg["g_mix_pre"] = _norm_bwd(s["x"], G["g_mix_pre"], dh2, 1.0, dx2, F32, "norm_bwd_pre")
    started_mix = emit("mix", gw)
    token = token if started_mix is None else started_mix

    def ready1(dws):
        gw["w1_gate"], gw["w1_up"], gw["w1_down"] = dws
        return emit("ffn1", gw)

    dx0, gg["g_ffn1_pre"], gg["g_ffn1_post"], started_ffn1 = _ffn_bwd(
        dx1, sv["ffn1"], G["g_ffn1_pre"], _ordered_after(G["g_ffn1_post"], token), W["w1_gate"], W["w1_up"],
        W["w1_down"], ready1)
    return dx0, gg, token if started_ffn1 is None else started_ffn1


def _pack_small(vecs):
    flat = jnp.concatenate([v.reshape(-1) for v in vecs])
    rows = -(-flat.shape[0] // (8 * LANES)) * 8
    return jnp.pad(flat, (0, rows * LANES - flat.shape[0])).reshape(rows, LANES)


def _group_weights(names, landed, shards, chip):
    out = {}
    for name, g, own in zip(names, landed, shards):
        g = lax.dynamic_update_slice(g, own[None], (chip, 0, 0))
        if name == "w_in":
            g = _regroup_w_in_t(g)
        elif name == "w_mla_uq":
            g = _regroup_w_uq_t(g)
        elif name not in ("w_mla_ukv", "w_ple_proj"):
            g = g.reshape(-1, g.shape[2])
        elif name == "w_mla_ukv":
            g = _regroup_w_ukv(_cat_cols(g))
        out[name] = g
    return out


def kernel(x, p, positions, g_ffn1_pre, w1_gate, w1_up, w1_down, g_ffn1_post, g_mix_pre, w_in, g_mla_q, w_mla_uq, g_mla_kv, w_mla_ukv, w_conv, w_out, g_mix_post, g_ffn2_pre, w2_gate, w2_up, w2_down, g_ffn2_post, g_ple_pre, w_ple_gate, w_ple_proj, g_ple_post, loss_target, m_g_ffn1_pre, m_w1_gate, m_w1_up, m_w1_down, m_g_ffn1_post, m_g_mix_pre, m_w_in, m_g_mla_q, m_w_mla_uq, m_g_mla_kv, m_w_mla_ukv, m_w_conv, m_w_out, m_g_mix_post, m_g_ffn2_pre, m_w2_gate, m_w2_up, m_w2_down, m_g_ffn2_post, m_g_ple_pre, m_w_ple_gate, m_w_ple_proj, m_g_ple_post, v_g_ffn1_pre, v_w1_gate, v_w1_up, v_w1_down, v_g_ffn1_post, v_g_mix_pre, v_w_in, v_g_mla_q, v_w_mla_uq, v_g_mla_kv, v_w_mla_ukv, v_w_conv, v_w_out, v_g_mix_post, v_g_ffn2_pre, v_w2_gate, v_w2_up, v_w2_down, v_g_ffn2_post, v_g_ple_pre, v_w_ple_gate, v_w_ple_proj, v_g_ple_post):
    w = dict(g_ffn1_pre=g_ffn1_pre, w1_gate=w1_gate, w1_up=w1_up, w1_down=w1_down, g_ffn1_post=g_ffn1_post,
             g_mix_pre=g_mix_pre, w_in=w_in, g_mla_q=g_mla_q, w_mla_uq=w_mla_uq, g_mla_kv=g_mla_kv,
             w_mla_ukv=w_mla_ukv, w_conv=w_conv, w_out=w_out, g_mix_post=g_mix_post, g_ffn2_pre=g_ffn2_pre,
             w2_gate=w2_gate, w2_up=w2_up, w2_down=w2_down, g_ffn2_post=g_ffn2_post, g_ple_pre=g_ple_pre,
             w_ple_gate=w_ple_gate, w_ple_proj=w_ple_proj, g_ple_post=g_ple_post)
    m = dict(g_ffn1_pre=m_g_ffn1_pre, w1_gate=m_w1_gate, w1_up=m_w1_up, w1_down=m_w1_down, g_ffn1_post=m_g_ffn1_post,
             g_mix_pre=m_g_mix_pre, w_in=m_w_in, g_mla_q=m_g_mla_q, w_mla_uq=m_w_mla_uq, g_mla_kv=m_g_mla_kv,
             w_mla_ukv=m_w_mla_ukv, w_conv=m_w_conv, w_out=m_w_out, g_mix_post=m_g_mix_post, g_ffn2_pre=m_g_ffn2_pre,
             w2_gate=m_w2_gate, w2_up=m_w2_up, w2_down=m_w2_down, g_ffn2_post=m_g_ffn2_post, g_ple_pre=m_g_ple_pre,
             w_ple_gate=m_w_ple_gate, w_ple_proj=m_w_ple_proj, g_ple_post=m_g_ple_post)
    v = dict(g_ffn1_pre=v_g_ffn1_pre, w1_gate=v_w1_gate, w1_up=v_w1_up, w1_down=v_w1_down, g_ffn1_post=v_g_ffn1_post,
             g_mix_pre=v_g_mix_pre, w_in=v_w_in, g_mla_q=v_g_mla_q, w_mla_uq=v_w_mla_uq, g_mla_kv=v_g_mla_kv,
             w_mla_ukv=v_w_mla_ukv, w_conv=v_w_conv, w_out=v_w_out, g_mix_post=v_g_mix_post, g_ffn2_pre=v_g_ffn2_pre,
             w2_gate=v_w2_gate, w2_up=v_w2_up, w2_down=v_w2_down, g_ffn2_post=v_g_ffn2_post, g_ple_pre=v_g_ple_pre,
             w_ple_gate=v_w_ple_gate, w_ple_proj=v_w_ple_proj, g_ple_post=v_g_ple_post)

    for name in TRANSPOSED:
        w[name], m[name], v[name] = (jnp.swapaxes(t[name], 1, 2) for t in (w, m, v))

    depth = g_ffn1_pre.shape[0]
    assert depth == 2, "the reduced gradients are assembled in [2, K, N] buffers"
    S = x.shape[1]
    cx, cy, cc = lax.axis_index("x"), lax.axis_index("y"), lax.axis_index("c")
    chip = 2 * cx + cy
    c_idx = cc.reshape(1).astype(jnp.int32)
    idx2 = jnp.stack([chip, cc]).astype(jnp.int32)

    conv_slot = lax.dynamic_update_slice(jnp.zeros((depth, 3, 256), F32),
                                         w_conv * (cc == 0).astype(F32), (0, 0, 64 * chip))
    conv_sum = _allreduce_small(_pack_small([conv_slot]), positions)
    w_conv_full = conv_sum[:depth * 3 * 256 // LANES].reshape(depth, 3, 256)

    def padded(name, i):
        (rows, _), rows_p = BIG[name]
        return jnp.pad(w[name][i].astype(BF16), ((0, rows_p - rows), (0, 0)))

    def groups_of(i, groups):
        return groups if i == 0 else (("all", BIG_NAMES),)

    started, after = {}, conv_sum
    for i in range(depth):
        for gname, names in groups_of(i, GATHER_GROUPS):
            shards = [padded(name, i) for name in names]
            lands = [lax.empty((N_CHIPS,) + s.shape, BF16) for s in shards]
            started[i, gname] = _exchange_start(_gather_copies, shards, lands, after, "gather_start_%d_%s" % (i, gname))
            after = started[i, gname][4]
    all_started = after

    inv = ROPE_BASE ** (-jnp.arange(ROPE_HALF, dtype=F32) / ROPE_HALF)
    zeros = lambda n: jnp.zeros((n,), F32)
    ones = jnp.ones((ROPE_HALF,), F32)
    inv_pat = jnp.concatenate([zeros(64), inv, inv, zeros(32)]).reshape(1, LANES)
    sign_pat = jnp.concatenate([zeros(64), -ones, ones, zeros(32)]).reshape(1, LANES)
    cp, sp = _rope_tables(positions.reshape(S, 1), inv_pat, sign_pat)

    def fetcher(i):
        table, have = dict(groups_of(i, GATHER_GROUPS)), {}

        def fetch(group, after):
            key = group if group in table else "all"
            if (i, key) in started:
                ssem, rsem, srcs, lands, _ = started.pop((i, key))
                if i == 0 and group == "ffn1a":
                    after = all_started
                srcs, landed = _exchange_wait(_gather_copies, ssem, rsem, srcs, lands, after,
                                              "gather_wait_%d_%s" % (i, key))
                have.update(_group_weights(table[key], _forward_rows(landed), srcs, chip))
            return have

        return fetch

    xs, saved = x[0], []
    gains = [{name: w[name][i].reshape(1, n) for name, n in GAINS} for i in range(depth)]
    for i in range(depth):
        xs, sv = _layer_fwd(xs, p[i, 0], fetcher(i), gains[i], w_conv_full[i], cp, sp)
        saved.append(sv)

    loss_part, dx = _loss_and_grad(xs, loss_target[0])
    loss = lax.psum(loss_part[0, 0], AXES)

    scattering = []

    def emitter(i):
        table = dict(groups_of(i, SCATTER_GROUPS))

        def emit(group, gw):
            if group in table:
                names = table[group]
            elif group == "ffn1":
                names = table["all"]
            else:
                return None
            full = [gw[name] for name in names]
            pair_sums = [_add_rows(b, g, c_idx) for b, g in zip(full, _swap_rows(full))]
            lands = [lax.empty((3,) + s.shape[1:], BF16) for s in pair_sums]
            st = _exchange_start(_scatter_copies, pair_sums, lands, c_idx, "scatter_start_%d_%s" % (i, group))
            scattering.append((i, group, names, st))
            return st[4]

        return emit

    ggs, token = [None] * depth, None
    for i in reversed(range(depth)):
        dx, ggs[i], token = _layer_bwd(dx, saved[i], gains[i], w_conv_full[i], cp, sp, emitter(i), token)

    fins, after = {name: None for name in BIG_NAMES}, dx
    for i, group, names, (ssem, rsem, srcs, lands, _) in scattering:
        srcs, arrived = _exchange_wait(_scatter_copies, ssem, rsem, srcs, lands, after,
                                       "scatter_wait_%d_%s" % (i, group))
        for name, s1, got in zip(names, srcs, arrived):
            after = fins[name] = _add_chip_sums(s1, got, idx2, i, fins[name], after)
    joined = _join_rows([fins[name] for name in BIG_NAMES])
    reduced = dict(zip(BIG_NAMES, joined))

    small_names = [name for name, _ in GAINS] + ["w_conv"]
    small = _allreduce_small(_pack_small([jnp.stack([ggs[i][name] for i in range(depth)]) for name in small_names]),
                             joined[0])
    flat, off = small.reshape(-1), 0
    for name, n in GAINS:
        reduced[name] = flat[off:off + depth * n].reshape(depth, n)
        off += depth * n
    conv_full = flat[off:off + depth * 3 * 256].reshape(depth, 3, 256)
    reduced["w_conv"] = lax.dynamic_slice(conv_full, (0, 0, 64 * chip), (depth, 3, 64))

    grads, deltas, new_m, new_v = {}, {}, {}, {}
    for name in WEIGHT_ORDER:
        shape = w[name].shape
        three_d = shape if len(shape) == 3 else (1,) + shape
        g_ = reduced[name] if len(shape) == 3 else reduced[name].reshape(three_d)
        outs = _adamw(w[name].reshape(three_d), g_, m[name].reshape(three_d), v[name].reshape(three_d), "adamw")
        if name in TRANSPOSED:
            outs = [jnp.swapaxes(o, 1, 2) for o in outs]
        grads[name], deltas[name], new_m[name], new_v[name] = (o.reshape(o.shape[-len(shape):]) for o in outs)

    return (loss, dx[None], *[grads[n] for n in WEIGHT_ORDER], *[deltas[n] for n in WEIGHT_ORDER],
            *[new_m[n] for n in WEIGHT_ORDER], *[new_v[n] for n in WEIGHT_ORDER])
```

```python
import functools

import jax
import jax.numpy as jnp
from jax import lax
from jax.experimental import pallas as pl
from jax.experimental.pallas import tpu as pltpu

F32 = jnp.float32
BF16 = jnp.bfloat16
MESH = pl.DeviceIdType.MESH
AXES = ("x", "y", "c")

D_MODEL = 1024
N_CHIPS = 4
D_FF = 2816
EPS = 1e-6
NEG_INF = -1e30
ROPE_BASE = 10000.0
ROPE_HALF = 16
LANES = 128
SB_HEADS, MLA_HEADS, HEAD_DIM = 4, 8, 64
MLA_SCALE = 96.0 ** -0.5
SB_SCALE = 64.0 ** -0.5
SB_BQ, SB_BK = 512, 128
MLA_BQ, MLA_BK = 256, 256
ADAM_LR, ADAM_B1, ADAM_B2, ADAM_EPS, ADAM_WD, ADAM_STEP = 0.001, 0.9, 0.999, 1e-08, 0.01, 10
VMEM_LIMIT = 48 * 2 ** 20

Z_SB, Z_KV, Z_KR, Z_Q, Z_CV, Z_W = 0, 768, 1024, 1152, 1536, 2304

NT = (((1,), (1,)), ((), ()))
TN = (((0,), (0,)), ((), ()))

TRANSPOSED = ("w1_gate", "w1_up", "w2_gate", "w2_up", "w_in", "w_mla_uq")
BIG = {"w1_gate": ((704, 1024), 704), "w1_up": ((704, 1024), 704), "w1_down": ((704, 1024), 704),
       "w_in": ((552, 1024), 576), "w_mla_uq": ((192, 384), 192), "w_mla_ukv": ((256, 256), 256),
       "w_out": ((256, 1024), 256),
       "w2_gate": ((704, 1024), 704), "w2_up": ((704, 1024), 704), "w2_down": ((704, 1024), 704),
       "w_ple_gate": ((256, 1024), 256), "w_ple_proj": ((256, 256), 256)}
BIG_NAMES = tuple(BIG)
GAINS = (("g_ffn1_pre", 1024), ("g_ffn1_post", 1024), ("g_mix_pre", 1024), ("g_mla_q", 384),
         ("g_mla_kv", 256), ("g_mix_post", 1024), ("g_ffn2_pre", 1024), ("g_ffn2_post", 1024),
         ("g_ple_pre", 1024), ("g_ple_post", 1024))
WEIGHT_ORDER = ("g_ffn1_pre", "w1_gate", "w1_up", "w1_down", "g_ffn1_post", "g_mix_pre", "w_in", "g_mla_q",
                "w_mla_uq", "g_mla_kv", "w_mla_ukv", "w_conv", "w_out", "g_mix_post", "g_ffn2_pre", "w2_gate",
                "w2_up", "w2_down", "g_ffn2_post", "g_ple_pre", "w_ple_gate", "w_ple_proj", "g_ple_post")

_pcall = pl.pallas_call


def _params(sem=None):
    return pltpu.CompilerParams(dimension_semantics=sem, vmem_limit_bytes=VMEM_LIMIT)


def _dot(a, b, dims=None):
    a, b = a.astype(BF16), b.astype(BF16)
    if dims is None:
        return jnp.dot(a, b, preferred_element_type=F32)
    return lax.dot_general(a, b, dims, preferred_element_type=F32)


def _rstd(v):
    return lax.rsqrt(jnp.mean(v * v, axis=-1, keepdims=True) + EPS)


def _sigmoid(v):
    return 0.5 * jnp.tanh(0.5 * v) + 0.5


def _row_block(n, want, mult=16):
    for b in range(min(n, want), 0, -1):
        if n % b == 0 and b % mult == 0:
            return b
    return n


def _spec(block, index, lead=None):
    if not lead:
        return pl.BlockSpec(block, index)
    lead = tuple(lead)
    return pl.BlockSpec((None,) * len(lead) + tuple(block), lambda *g: lead + tuple(index(*g)))


def _arr(op):
    return op[0] if isinstance(op, tuple) else op


def _lead(op):
    return tuple(op[1:]) if isinstance(op, tuple) else ()


def _mm(pairs, mode, out_dtype, name, bm=512, bn=512):
    a0, b0 = _arr(pairs[0][0]), _arr(pairs[0][1])
    if mode == "nn":
        M, N = a0.shape[-2], b0.shape[-1]
    elif mode == "nt":
        M, N = a0.shape[-2], b0.shape[-2]
    else:
        M, N = a0.shape[-1], b0.shape[-1]
    if mode == "tn":
        bn = max(bn, 1024)
    bm, bn = _row_block(M, bm, 128 if mode == "tn" else 16), _row_block(N, bn, 128)
    n_pairs = len(pairs)
    dims = {"nn": None, "nt": NT, "tn": TN}[mode]

    def body(*refs):
        acc = None
        for t in range(n_pairs):
            part = _dot(refs[2 * t][...], refs[2 * t + 1][...], dims)
            acc = part if acc is None else acc + part
        refs[-1][...] = acc.astype(refs[-1].dtype)

    in_specs, ops = [], []
    for a, b in pairs:
        sa, sb = _arr(a).shape, _arr(b).shape
        if mode == "nn":
            in_specs += [_spec((bm, sa[-1]), lambda j, i: (i, 0), _lead(a)),
                         _spec((sb[-2], bn), lambda j, i: (0, j), _lead(b))]
        elif mode == "nt":
            in_specs += [_spec((bm, sa[-1]), lambda j, i: (i, 0), _lead(a)),
                         _spec((bn, sb[-1]), lambda j, i: (j, 0), _lead(b))]
        else:
            in_specs += [_spec((sa[-2], bm), lambda j, i: (0, i), _lead(a)),
                         _spec((sb[-2], bn), lambda j, i: (0, j), _lead(b))]
        ops += [_arr(a), _arr(b)]
    return _pcall(body, name=name, grid=(N // bn, M // bm), in_specs=in_specs,
                  out_specs=pl.BlockSpec((bm, bn), lambda j, i: (i, j)),
                  out_shape=jax.ShapeDtypeStruct((M, N), out_dtype),
                  compiler_params=_params(("parallel", "parallel")))(*ops)


def _dw_col_shards(a, b, name):
    S, K = a.shape
    Np = b.shape[1] // N_CHIPS

    def body(a_ref, b_ref, o_ref):
        o_ref[...] = _dot(a_ref[...], b_ref[...], TN).astype(BF16)

    return _pcall(body, name=name, grid=(N_CHIPS,),
                  in_specs=[pl.BlockSpec((S, K), lambda j: (0, 0)), pl.BlockSpec((S, Np), lambda j: (0, j))],
                  out_specs=pl.BlockSpec((None, K, Np), lambda j: (j, 0, 0)),
                  out_shape=jax.ShapeDtypeStruct((N_CHIPS, K, Np), BF16),
                  compiler_params=_params(("parallel",)))(a, b)


def _ew(fn, ins, outs, name, br=256):
    S = max(a.shape[0] for a in ins)
    br = _row_block(S, br)
    n_in = len(ins)

    def body(*refs):
        res = fn(*[r[...] for r in refs[:n_in]])
        if not isinstance(res, tuple):
            res = (res,)
        for r, v in zip(refs[n_in:], res):
            r[...] = v.astype(r.dtype)

    in_specs = [pl.BlockSpec((br, a.shape[1]), lambda i: (i, 0)) if a.shape[0] == S and S > 1
                else pl.BlockSpec(a.shape, lambda i: (0, 0)) for a in ins]
    out = _pcall(body, name=name, grid=(S // br,), in_specs=in_specs,
                 out_specs=tuple(pl.BlockSpec((br, w), lambda i: (i, 0)) for w, _ in outs),
                 out_shape=tuple(jax.ShapeDtypeStruct((S, w), dt) for w, dt in outs),
                 compiler_params=_params(("parallel",)))(*ins)
    return out if len(outs) > 1 else out[0]


def _norm_fwd(x, g, name):
    return _ew(lambda xv, gv: xv * _rstd(xv) * gv, [x, g], [(x.shape[1], BF16)], name, br=512)


def _resid_norm(x, y, g, alpha, name):
    return _ew(lambda xv, yv, gv: xv + alpha * (yv * _rstd(yv) * gv), [x, y, g], [(x.shape[1], F32)], name)


def _norm_bwd(xin, g, dy, alpha, resid, out_dtype, name):
    S, W = xin.shape
    br = _row_block(S, 256)
    has_res = resid is not None

    def body(*refs):
        x_ref, g_ref, dy_ref = refs[:3]
        dx_ref, dg_ref = refs[-2:]
        xv, dyv = x_ref[...], dy_ref[...] * alpha
        r = _rstd(xv)
        xh = xv * r
        u = dyv * g_ref[...]
        dx = r * (u - xh * jnp.mean(u * xh, axis=-1, keepdims=True))
        if has_res:
            dx = dx + refs[3][...]
        dx_ref[...] = dx.astype(dx_ref.dtype)
        part = jnp.sum(dyv * xh, axis=0, keepdims=True)

        @pl.when(pl.program_id(0) == 0)
        def _():
            dg_ref[...] = part

        @pl.when(pl.program_id(0) > 0)
        def _():
            dg_ref[...] += part

    row = pl.BlockSpec((br, W), lambda i: (i, 0))
    vec = pl.BlockSpec((1, W), lambda i: (0, 0))
    ops = [xin, g, dy] + ([resid] if has_res else [])
    return _pcall(body, name=name, grid=(S // br,), in_specs=[row, vec, row] + ([row] if has_res else []),
                  out_specs=(row, vec),
                  out_shape=(jax.ShapeDtypeStruct((S, W), out_dtype), jax.ShapeDtypeStruct((1, W), F32)),
                  compiler_params=_params(("arbitrary",)))(*ops)


def _ffn_gate_up(h, wgt, wut, name):
    S, K = h.shape
    F = wgt.shape[0]
    bm, bn = _row_block(S, 512), _row_block(F, 1408, 128)

    def body(h_ref, wg_ref, wu_ref, g_ref, u_ref, a_ref):
        hv = h_ref[...]
        g = _dot(hv, wg_ref[...], NT)
        u = _dot(hv, wu_ref[...], NT)
        g_ref[...] = g.astype(BF16)
        u_ref[...] = u.astype(BF16)
        a_ref[...] = (g * _sigmoid(g) * u).astype(BF16)

    blk = pl.BlockSpec((bm, bn), lambda n, i: (i, n))
    wsp = pl.BlockSpec((bn, K), lambda n, i: (n, 0))
    return _pcall(body, name=name, grid=(F // bn, S // bm),
                  in_specs=[pl.BlockSpec((bm, K), lambda n, i: (i, 0)), wsp, wsp],
                  out_specs=(blk, blk, blk), out_shape=(jax.ShapeDtypeStruct((S, F), BF16),) * 3,
                  compiler_params=_params(("parallel", "parallel")))(h, wgt, wut)


def _ffn_bwd_mid(dy, wd, g, u, name):
    S, D = dy.shape
    F = g.shape[1]
    bm, bn = _row_block(S, 1024), 256

    def body(dy_ref, wd_ref, g_ref, u_ref, dg_ref, du_ref):
        da = _dot(dy_ref[...], wd_ref[...], NT)
        gv, uv = g_ref[...].astype(F32), u_ref[...].astype(F32)
        s = _sigmoid(gv)
        dg_ref[...] = (da * uv * (s * (1.0 + gv * (1.0 - s)))).astype(BF16)
        du_ref[...] = (da * (gv * s)).astype(BF16)

    blk = pl.BlockSpec((bm, bn), lambda j, i: (i, j))
    return _pcall(body, name=name, grid=(F // bn, S // bm),
                  in_specs=[pl.BlockSpec((bm, D), lambda j, i: (i, 0)),
                            pl.BlockSpec((bn, D), lambda j, i: (j, 0)), blk, blk],
                  out_specs=(blk, blk), out_shape=(jax.ShapeDtypeStruct((S, F), BF16),) * 2,
                  compiler_params=_params(("parallel", "parallel")))(dy, wd, g, u)


def _ple_proj(p16, w):
    S, K = p16.shape
    bm = _row_block(S, 1024)

    def body(p_ref, w_ref, o_ref):
        o_ref[...] = _dot(p_ref[...], w_ref[...])

    return _pcall(body, name="ple_proj", grid=(N_CHIPS, S // bm),
                  in_specs=[pl.BlockSpec((bm, K), lambda j, i: (i, 0)),
                            pl.BlockSpec((None, K, 256), lambda j, i: (j, 0, 0))],
                  out_specs=pl.BlockSpec((bm, 256), lambda j, i: (i, j)),
                  out_shape=jax.ShapeDtypeStruct((S, N_CHIPS * 256), F32),
                  compiler_params=_params(("parallel", "parallel")))(p16, w)


def _loss_and_grad(xf, tgt):
    S, W = xf.shape
    br = _row_block(S, 256)

    def body(x_ref, t_ref, l_ref, d_ref):
        d = x_ref[...] - t_ref[...]
        d_ref[...] = d * (1.0 / W)
        part = 0.5 * jnp.sum(jnp.sum(d * d, axis=-1, keepdims=True) * (1.0 / W), axis=0, keepdims=True)

        @pl.when(pl.program_id(0) == 0)
        def _():
            l_ref[...] = part

        @pl.when(pl.program_id(0) > 0)
        def _():
            l_ref[...] += part

    row = pl.BlockSpec((br, W), lambda i: (i, 0))
    return _pcall(body, name="loss", grid=(S // br,), in_specs=[row, row],
                  out_specs=(pl.BlockSpec((1, 1), lambda i: (0, 0)), row),
                  out_shape=(jax.ShapeDtypeStruct((1, 1), F32), jax.ShapeDtypeStruct((S, W), F32)),
                  compiler_params=_params(("arbitrary",)))(xf, tgt)


def _adamw(w, g, m, v, name):
    L, R, C = w.shape
    Cp = g.shape[2]
    br = _row_block(R, 256, 8) if R % 8 == 0 else R

    def body(w_ref, g_ref, m_ref, v_ref, go_ref, d_ref, nm_ref, nv_ref):
        gv = g_ref[...][:, :C]
        nm = ADAM_B1 * m_ref[...] + (1.0 - ADAM_B1) * gv
        nv = ADAM_B2 * v_ref[...] + (1.0 - ADAM_B2) * (gv * gv)
        m_hat = nm / (1.0 - ADAM_B1 ** ADAM_STEP)
        v_hat = nv / (1.0 - ADAM_B2 ** ADAM_STEP)
        go_ref[...] = gv
        d_ref[...] = -ADAM_LR * (m_hat / (jnp.sqrt(v_hat) + ADAM_EPS) + ADAM_WD * w_ref[...])
        nm_ref[...] = nm
        nv_ref[...] = nv

    blk = pl.BlockSpec((None, br, C), lambda l, i: (l, i, 0))
    gblk = pl.BlockSpec((None, br, Cp), lambda l, i: (l, i, 0))
    return _pcall(body, name=name, grid=(L, R // br), in_specs=[blk, gblk, blk, blk], out_specs=(blk,) * 4,
                  out_shape=(jax.ShapeDtypeStruct((L, R, C), F32),) * 4,
                  compiler_params=_params(("parallel", "parallel")))(w, g, m, v)


def _rope_tables(pos, inv_pat, sign_pat):
    def fn(p, iv, sg):
        ang = p.astype(F32) * iv
        return jnp.cos(ang), jnp.sin(ang) * sg

    return _ew(fn, [pos, inv_pat, sign_pat], [(LANES, F32)] * 2, "rope_tables")


def _swap_halves_of_rope(v):
    W = v.shape[1]
    lane = lax.broadcasted_iota(jnp.int32, (1, W), 1) % LANES
    return jnp.where((lane >= 64) & (lane < 80), pltpu.roll(v, W - ROPE_HALF, 1),
                     jnp.where((lane >= 80) & (lane < 96), pltpu.roll(v, ROPE_HALF, 1), 0.0))


def _tile_lanes(v, n):
    return jnp.concatenate([v] * n, axis=1)


def _mla_prep(qf, kv, z, cp, sp):
    S = qf.shape[0]
    br = _row_block(S, 256)
    W = MLA_HEADS * LANES

    def body(q_ref, k_ref, r_ref, c_ref, s_ref, qo_ref, ko_ref):
        c, s = c_ref[...], s_ref[...]
        q = q_ref[...]
        qo_ref[...] = (q * _tile_lanes(c, MLA_HEADS) + _swap_halves_of_rope(q) * _tile_lanes(s, MLA_HEADS)).astype(BF16)
        r = r_ref[...]
        lane = lax.broadcasted_iota(jnp.int32, (1, LANES), 1)
        kr = jnp.where((lane >= 64) & (lane < 96), r * c + _swap_halves_of_rope(r) * s, 0.0)
        ko_ref[...] = (k_ref[...] + _tile_lanes(kr, MLA_HEADS)).astype(BF16)

    wide = pl.BlockSpec((br, W), lambda i: (i, 0))
    one = pl.BlockSpec((br, LANES), lambda i: (i, 0))
    return _pcall(body, name="mla_prep", grid=(S // br,),
                  in_specs=[wide, wide, pl.BlockSpec((br, LANES), lambda i: (i, Z_KR // LANES)), one, one],
                  out_specs=(wide, wide), out_shape=(jax.ShapeDtypeStruct((S, W), BF16),) * 2,
                  compiler_params=_params(("parallel",)))(qf, kv, z, cp, sp)


def _mla_prep_bwd(dq, dk, cp, sp):
    S, W = dq.shape
    br = _row_block(S, 256)

    def body(dq_ref, dk_ref, c_ref, s_ref, dqo_ref, dr_ref):
        c, s = c_ref[...], s_ref[...]
        d = dq_ref[...]
        dqo_ref[...] = (d * _tile_lanes(c, MLA_HEADS) + _swap_halves_of_rope(d * _tile_lanes(s, MLA_HEADS))).astype(BF16)
        dkv = dk_ref[...]
        tot = dkv[:, 0:LANES]
        for h in range(1, MLA_HEADS):
            tot = tot + dkv[:, h * LANES:(h + 1) * LANES]
        lane = lax.broadcasted_iota(jnp.int32, (1, LANES), 1)
        tot = jnp.where((lane >= 64) & (lane < 96), tot, 0.0)
        dr_ref[...] = tot * c + _swap_halves_of_rope(tot * s)

    wide = pl.BlockSpec((br, W), lambda i: (i, 0))
    one = pl.BlockSpec((br, LANES), lambda i: (i, 0))
    return _pcall(body, name="mla_prep_bwd", grid=(S // br,), in_specs=[wide, wide, one, one], out_specs=(wide, one),
                  out_shape=(jax.ShapeDtypeStruct((S, W), BF16), jax.ShapeDtypeStruct((S, LANES), F32)),
                  compiler_params=_params(("parallel",)))(dq, dk, cp, sp)


def _shift_down(v, k, row):
    return jnp.where(row >= k, pltpu.roll(v, k, 0), 0.0)


def _shift_up(v, k, row):
    n = v.shape[0]
    return jnp.where(row < n - k, pltpu.roll(v, n - k, 0), 0.0)


def _conv_fwd(z, w):
    S = z.shape[0]
    c0 = Z_CV // LANES

    def body(b_ref, c_ref, h_ref, w_ref, y_ref):
        u = c_ref[...] * h_ref[...]
        row = lax.broadcasted_iota(jnp.int32, u.shape, 0)
        wv = w_ref[...]
        conv = wv[0:1] * _shift_down(u, 2, row) + wv[1:2] * _shift_down(u, 1, row) + wv[2:3] * u
        y_ref[...] = b_ref[...] * conv

    def col(k):
        return pl.BlockSpec((S, LANES), lambda j: (0, c0 + 2 * k + j))

    return _pcall(body, name="conv_fwd", grid=(2,),
                  in_specs=[col(0), col(1), col(2), pl.BlockSpec((3, LANES), lambda j: (0, j))],
                  out_specs=pl.BlockSpec((S, LANES), lambda j: (0, j)),
                  out_shape=jax.ShapeDtypeStruct((S, 256), F32), compiler_params=_params(("parallel",)))(z, z, z, w)


def _conv_bwd(z, w, dc_all):
    S = z.shape[0]
    c0 = Z_CV // LANES

    def body(b_ref, c_ref, h_ref, w_ref, dy_ref, db_ref, dc_ref, dh_ref, dw_ref):
        cv, hv, bv, dyv = c_ref[...], h_ref[...], b_ref[...], dy_ref[...]
        u = cv * hv
        row = lax.broadcasted_iota(jnp.int32, u.shape, 0)
        wv = w_ref[...]
        u1, u2 = _shift_down(u, 1, row), _shift_down(u, 2, row)
        conv = wv[0:1] * u2 + wv[1:2] * u1 + wv[2:3] * u
        db_ref[...] = dyv * conv
        dconv = dyv * bv
        du = wv[2:3] * dconv + wv[1:2] * _shift_up(dconv, 1, row) + wv[0:1] * _shift_up(dconv, 2, row)
        dc_ref[...] = du * hv
        dh_ref[...] = du * cv
        dw_ref[0:1, :] = jnp.sum(dconv * u2, axis=0, keepdims=True)
        dw_ref[1:2, :] = jnp.sum(dconv * u1, axis=0, keepdims=True)
        dw_ref[2:3, :] = jnp.sum(dconv * u, axis=0, keepdims=True)

    def col(k):
        return pl.BlockSpec((S, LANES), lambda j: (0, c0 + 2 * k + j))

    wsp = pl.BlockSpec((3, LANES), lambda j: (0, j))
    osp = pl.BlockSpec((S, LANES), lambda j: (0, j))
    db, dc, dh, dw = _pcall(
        body, name="conv_bwd", grid=(2,),
        in_specs=[col(0), col(1), col(2), wsp, pl.BlockSpec((S, LANES), lambda j: (0, 6 + j))],
        out_specs=(osp, osp, osp, wsp),
        out_shape=(jax.ShapeDtypeStruct((S, 256), F32),) * 3 + (jax.ShapeDtypeStruct((3, 256), F32),),
        compiler_params=_params(("parallel",)))(z, z, z, w, dc_all)
    return db, dc, dh, dw


def _half_masks():
    lane = lax.broadcasted_iota(jnp.int32, (1, LANES), 1)
    return lane < HEAD_DIM, lane >= HEAD_DIM


def _pair(v, p):
    return v[:, p * LANES:(p + 1) * LANES]


def _keep(mask, v):
    return jnp.where(mask, v, jnp.zeros_like(v))


def _lane_col(vals):
    lane = lax.broadcasted_iota(jnp.int32, (1, LANES), 1)
    out = jnp.zeros((vals[0].shape[0], LANES), F32)
    for h, v in enumerate(vals):
        out = jnp.where(lane == h, v, out)
    return out


def _mla_fwd(q, k, kv):
    S = q.shape[0]
    H = MLA_HEADS
    bq, bk = _row_block(S, MLA_BQ), _row_block(S, MLA_BK)
    per = bq // bk

    def body(q_ref, k_ref, v_ref, o_ref, lse_ref):
        i = pl.program_id(0)
        lo, hi = _half_masks()
        qb = q_ref[...]

        def block(j, carry, ok):
            ms, ls, accs = carry
            off = pl.multiple_of(j * bk, bk)
            kb = k_ref[pl.ds(off, bk), :]
            vb = v_ref[pl.ds(off, bk), :].astype(BF16)
            ss = [_dot(_pair(qb, h), _pair(kb, h), NT) * MLA_SCALE for h in range(H)]
            if ok is not None:
                ss = [jnp.where(ok, s, NEG_INF) for s in ss]
            ms2 = [jnp.maximum(ms[h], jnp.max(ss[h], axis=-1, keepdims=True)) for h in range(H)]
            ps = [jnp.exp(ss[h] - ms2[h]) for h in range(H)]
            al = [jnp.exp(ms[h] - ms2[h]) for h in range(H)]
            ls2 = [al[h] * ls[h] + jnp.sum(ps[h], axis=-1, keepdims=True) for h in range(H)]
            pvs = [_dot(ps[h], _pair(vb, h // 2)) for h in range(H)]
            accs2 = []
            for p in range(H // 2):
                scale = jnp.where(lo, al[2 * p], al[2 * p + 1])
                accs2.append(scale * accs[p] + jnp.where(lo, pvs[2 * p], pvs[2 * p + 1]))
            return tuple(ms2), tuple(ls2), tuple(accs2)

        init = (tuple(jnp.full((bq, 1), NEG_INF, F32) for _ in range(H)),
                tuple(jnp.zeros((bq, 1), F32) for _ in range(H)),
                tuple(jnp.zeros((bq, LANES), F32) for _ in range(H // 2)))
        carry = lax.fori_loop(0, i * per, lambda j, c: block(j, c, None), init)
        row = lax.broadcasted_iota(jnp.int32, (bq, bk), 0)
        col = lax.broadcasted_iota(jnp.int32, (bq, bk), 1)
        for t in range(per):
            carry = block(i * per + t, carry, col + t * bk <= row)
        ms, ls, accs = carry
        o_ref[...] = jnp.concatenate(
            [accs[p] / jnp.where(lo, ls[2 * p], ls[2 * p + 1]) for p in range(H // 2)], axis=1)
        lse_ref[...] = _lane_col([ms[h] + jnp.log(ls[h]) for h in range(H)])

    return _pcall(body, name="mla_fwd", grid=(S // bq,),
                  in_specs=[pl.BlockSpec((bq, H * LANES), lambda i: (i, 0)),
                            pl.BlockSpec((S, H * LANES), lambda i: (0, 0)),
                            pl.BlockSpec((S, H * HEAD_DIM), lambda i: (0, 2))],
                  out_specs=(pl.BlockSpec((bq, H * HEAD_DIM), lambda i: (i, 0)),
                             pl.BlockSpec((bq, LANES), lambda i: (i, 0))),
                  out_shape=(jax.ShapeDtypeStruct((S, H * HEAD_DIM), F32), jax.ShapeDtypeStruct((S, LANES), F32)),
                  compiler_params=_params(("parallel",)))(q, k, kv)


def _mla_bwd(q, k, kv, o, lse, dc_all):
    S = q.shape[0]
    H = MLA_HEADS
    bq, bk = _row_block(S, MLA_BQ), _row_block(S, MLA_BK)
    per = bq // bk

    def body(q_ref, k_ref, v_ref, o_ref, lse_ref, doa_ref, dob_ref, dq_ref, dk_ref, dv_ref):
        i = pl.program_id(0)

        @pl.when(i == 0)
        def _():
            dk_ref[...] = jnp.zeros_like(dk_ref)
            dv_ref[...] = jnp.zeros_like(dv_ref)

        lo, hi = _half_masks()
        qb = q_ref[...]
        dob = jnp.concatenate([doa_ref[...], dob_ref[...]], axis=1)
        prod = dob * o_ref[...]
        lse_v = lse_ref[...]
        do16 = dob.astype(BF16)
        dom, deltas, lses = [], [], []
        for h in range(H):
            mk = lo if h % 2 == 0 else hi
            dom.append(_keep(mk, _pair(do16, h // 2)))
            deltas.append(jnp.sum(_keep(mk, _pair(prod, h // 2)), axis=-1, keepdims=True))
            lses.append(lse_v[:, h:h + 1])

        def block(j, dqs, ok):
            off = pl.multiple_of(j * bk, bk)
            kb = k_ref[pl.ds(off, bk), :]
            vb = v_ref[pl.ds(off, bk), :].astype(BF16)
            ss = [_dot(_pair(qb, h), _pair(kb, h), NT) * MLA_SCALE for h in range(H)]
            dps = [_dot(dom[h], _pair(vb, h // 2), NT) for h in range(H)]
            ps = [jnp.exp(ss[h] - lses[h]) for h in range(H)]
            if ok is not None:
                ps = [jnp.where(ok, p, 0.0) for p in ps]
            ds16 = [(ps[h] * (dps[h] - deltas[h]) * MLA_SCALE).astype(BF16) for h in range(H)]
            p16 = [p.astype(BF16) for p in ps]
            dks = [_dot(ds16[h], _pair(qb, h), TN) for h in range(H)]
            dvps = [_dot(p16[h], dom[h], TN) for h in range(H)]
            dqs2 = [dqs[h] + _dot(ds16[h], _pair(kb, h)) for h in range(H)]
            dk_ref[pl.ds(off, bk), :] += jnp.concatenate(dks, axis=1)
            dv_ref[pl.ds(off, bk), :] += jnp.concatenate(
                [dvps[2 * p] + dvps[2 * p + 1] for p in range(H // 2)], axis=1)
            return tuple(dqs2)

        dqs = lax.fori_loop(0, i * per, lambda j, c: block(j, c, None),
                            tuple(jnp.zeros((bq, LANES), F32) for _ in range(H)))
        row = lax.broadcasted_iota(jnp.int32, (bq, bk), 0)
        col = lax.broadcasted_iota(jnp.int32, (bq, bk), 1)
        for t in range(per):
            dqs = block(i * per + t, dqs, col + t * bk <= row)
        dq_ref[...] = jnp.concatenate(dqs, axis=1)

    wide = pl.BlockSpec((bq, H * LANES), lambda i: (i, 0))
    full = pl.BlockSpec((S, H * LANES), lambda i: (0, 0))
    return _pcall(body, name="mla_bwd", grid=(S // bq,),
                  in_specs=[wide, full, pl.BlockSpec((S, H * HEAD_DIM), lambda i: (0, 2)),
                            pl.BlockSpec((bq, H * HEAD_DIM), lambda i: (i, 0)),
                            pl.BlockSpec((bq, LANES), lambda i: (i, 0)),
                            pl.BlockSpec((bq, 256), lambda i: (i, 1)), pl.BlockSpec((bq, 256), lambda i: (i, 2))],
                  out_specs=(wide, full, pl.BlockSpec((S, H * HEAD_DIM), lambda i: (0, 0))),
                  out_shape=(jax.ShapeDtypeStruct((S, H * LANES), F32), jax.ShapeDtypeStruct((S, H * LANES), F32),
                             jax.ShapeDtypeStruct((S, H * HEAD_DIM), F32)),
                  compiler_params=_params(("arbitrary",)))(q, k, kv, o, lse, dc_all, dc_all)


def _dot_exact(x, tri):
    h1 = x.astype(BF16)
    h2 = (x - h1.astype(F32)).astype(BF16)
    return _dot(h1, tri) + _dot(h2, tri)


def _softplus(z):
    return jnp.maximum(z, 0.0) + jnp.log(1.0 + jnp.exp(-jnp.abs(z)))


def _sb_fwd(z):
    S = z.shape[0]
    H = SB_HEADS
    bq, bk = _row_block(S, SB_BQ), _row_block(S, SB_BK)
    per = bq // bk

    def body(q_ref, k_ref, v_ref, o_ref, t_ref):
        i = pl.program_id(0)
        lo, hi = _half_masks()
        q16 = (q_ref[...] * SB_SCALE).astype(BF16)
        qm = [_keep(lo if h % 2 == 0 else hi, _pair(q16, h // 2)) for h in range(H)]
        rr = lax.broadcasted_iota(jnp.int32, (bk, bk), 0)
        cc = lax.broadcasted_iota(jnp.int32, (bk, bk), 1)
        later = (rr > cc).astype(BF16)

        def block(j, carry, mask):
            tails, accs = carry
            off = pl.multiple_of(j * bk, bk)
            kb = k_ref[pl.ds(off, bk), :].astype(BF16)
            vb = v_ref[pl.ds(off, bk), :].astype(BF16)
            zs = [_dot(qm[h], _pair(kb, h // 2), NT) for h in range(H)]
            sps = [_softplus(z) for z in zs]
            lnots = [-sp if mask is None else jnp.where(mask, -sp, 0.0) for sp in sps]
            sums = [_dot_exact(lnot, later) for lnot in lnots]
            ws = []
            for h in range(H):
                w = jnp.exp((zs[h] - sps[h]) + (tails[h] + sums[h]))
                ws.append((w if mask is None else jnp.where(mask, w, 0.0)).astype(BF16))
            pvs = [_dot(ws[h], _pair(vb, h // 2)) for h in range(H)]
            accs2 = [accs[p] + jnp.where(lo, pvs[2 * p], pvs[2 * p + 1]) for p in range(H // 2)]
            tails2 = [tails[h] + jnp.sum(lnots[h], axis=-1, keepdims=True) for h in range(H)]
            return tuple(tails2), tuple(accs2)

        carry = (tuple(jnp.zeros((bq, 1), F32) for _ in range(H)),
                 tuple(jnp.zeros((bq, LANES), F32) for _ in range(H // 2)))
        row = lax.broadcasted_iota(jnp.int32, (bq, bk), 0)
        col = lax.broadcasted_iota(jnp.int32, (bq, bk), 1)
        for t in range(per):
            carry = block(i * per + per - 1 - t, carry, col + (per - 1 - t) * bk < row)
        tails, accs = lax.fori_loop(0, i * per, lambda t, c: block(i * per - 1 - t, c, None), carry)
        o_ref[...] = jnp.concatenate(accs, axis=1)
        t_ref[...] = _lane_col(tails)

    return _pcall(body, name="sb_fwd", grid=(S // bq,),
                  in_specs=[pl.BlockSpec((bq, 256), lambda i: (i, 0)), pl.BlockSpec((S, 256), lambda i: (0, 1)),
                            pl.BlockSpec((S, 256), lambda i: (0, 2))],
                  out_specs=(pl.BlockSpec((bq, 256), lambda i: (i, 0)), pl.BlockSpec((bq, LANES), lambda i: (i, 0))),
                  out_shape=(jax.ShapeDtypeStruct((S, 256), F32), jax.ShapeDtypeStruct((S, LANES), F32)),
                  compiler_params=_params(("parallel",)))(z, z, z)


def _sb_bwd(z, tot, dc_all):
    S = z.shape[0]
    H = SB_HEADS
    bq, bk = _row_block(S, SB_BQ), _row_block(S, SB_BK)
    per = bq // bk

    def body(q_ref, k_ref, v_ref, t_ref, do_ref, dq_ref, dk_ref, dv_ref):
        i = pl.program_id(0)

        @pl.when(i == 0)
        def _():
            dk_ref[...] = jnp.zeros_like(dk_ref)
            dv_ref[...] = jnp.zeros_like(dv_ref)

        lo, hi = _half_masks()
        q16 = (q_ref[...] * SB_SCALE).astype(BF16)
        do16 = do_ref[...].astype(BF16)
        tot_v = t_ref[...]
        masks = [lo if h % 2 == 0 else hi for h in range(H)]
        qm = [_keep(masks[h], _pair(q16, h // 2)) for h in range(H)]
        dom = [_keep(masks[h], _pair(do16, h // 2)) for h in range(H)]
        tots = [tot_v[:, h:h + 1] for h in range(H)]
        rr = lax.broadcasted_iota(jnp.int32, (bk, bk), 0)
        cc = lax.broadcasted_iota(jnp.int32, (bk, bk), 1)
        upto = (rr <= cc).astype(BF16)
        before = (rr < cc).astype(BF16)

        def block(j, carry, mask):
            heads, eheads, dqs = carry
            off = pl.multiple_of(j * bk, bk)
            kb = k_ref[pl.ds(off, bk), :].astype(BF16)
            vb = v_ref[pl.ds(off, bk), :].astype(BF16)
            zs = [_dot(qm[h], _pair(kb, h // 2), NT) for h in range(H)]
            dws = [_dot(dom[h], _pair(vb, h // 2), NT) for h in range(H)]
            sps = [_softplus(z) for z in zs]
            lnots = [-sp if mask is None else jnp.where(mask, -sp, 0.0) for sp in sps]
            pres = [_dot_exact(lnot, upto) for lnot in lnots]
            lsigs = [zs[h] - sps[h] for h in range(H)]
            ws = [jnp.exp(lsigs[h] + (tots[h] - (heads[h] + pres[h]))) for h in range(H)]
            if mask is not None:
                ws = [jnp.where(mask, w, 0.0) for w in ws]
            es = [ws[h] * dws[h] for h in range(H)]
            esums = [eheads[h] + _dot_exact(es[h], before) for h in range(H)]
            dz16 = []
            for h in range(H):
                sig = jnp.exp(lsigs[h])
                dz = es[h] * (1.0 - sig) - sig * esums[h]
                dz16.append((dz if mask is None else jnp.where(mask, dz, 0.0)).astype(BF16))
            w16 = [w.astype(BF16) for w in ws]
            dkps = [_dot(dz16[h], qm[h], TN) for h in range(H)]
            dvps = [_dot(w16[h], dom[h], TN) for h in range(H)]
            dqps = [_dot(dz16[h], _pair(kb, h // 2)) for h in range(H)]
            dk_ref[pl.ds(off, bk), :] += jnp.concatenate([dkps[2 * p] + dkps[2 * p + 1] for p in range(H // 2)], axis=1)
            dv_ref[pl.ds(off, bk), :] += jnp.concatenate([dvps[2 * p] + dvps[2 * p + 1] for p in range(H // 2)], axis=1)
            dqs2 = [dqs[p] + jnp.where(lo, dqps[2 * p], dqps[2 * p + 1]) for p in range(H // 2)]
            heads2 = [heads[h] + jnp.sum(lnots[h], axis=-1, keepdims=True) for h in range(H)]
            eheads2 = [eheads[h] + jnp.sum(es[h], axis=-1, keepdims=True) for h in range(H)]
            return tuple(heads2), tuple(eheads2), tuple(dqs2)

        zeros = tuple(jnp.zeros((bq, 1), F32) for _ in range(H))
        init = (zeros, zeros, tuple(jnp.zeros((bq, LANES), F32) for _ in range(H // 2)))
        carry = lax.fori_loop(0, i * per, lambda j, c: block(j, c, None), init)
        row = lax.broadcasted_iota(jnp.int32, (bq, bk), 0)
        col = lax.broadcasted_iota(jnp.int32, (bq, bk), 1)
        for t in range(per):
            carry = block(i * per + t, carry, col + t * bk < row)
        dq_ref[...] = jnp.concatenate(carry[2], axis=1) * SB_SCALE

    blk = pl.BlockSpec((bq, 256), lambda i: (i, 0))
    full = pl.BlockSpec((S, 256), lambda i: (0, 0))
    return _pcall(body, name="sb_bwd", grid=(S // bq,),
                  in_specs=[blk, pl.BlockSpec((S, 256), lambda i: (0, 1)), pl.BlockSpec((S, 256), lambda i: (0, 2)),
                            pl.BlockSpec((bq, LANES), lambda i: (i, 0)), blk],
                  out_specs=(blk, full, full), out_shape=(jax.ShapeDtypeStruct((S, 256), F32),) * 3,
                  compiler_params=_params(("arbitrary",)))(z, z, z, tot, dc_all)


_HBM = pl.BlockSpec(memory_space=pltpu.HBM)
_SEM = pl.BlockSpec(memory_space=pltpu.SEMAPHORE)
_ANY = pl.BlockSpec(memory_space=pl.ANY)


def _place():
    x, y, c = lax.axis_index("x"), lax.axis_index("y"), lax.axis_index("c")
    other_chips = ((1 - x, y), (x, 1 - y), (1 - x, 1 - y))
    return x, y, c, other_chips


def _remote(src, dst, send_sems, recv_sems, k, dev):
    return pltpu.make_async_remote_copy(src_ref=src, dst_ref=dst, send_sem=send_sems.at[k], recv_sem=recv_sems.at[k],
                                        device_id=dev, device_id_type=MESH)


def _half_rows(c, n_rows):
    half = n_rows // 2
    return pl.ds(pl.multiple_of(c * half, 16), half)


def _gather_copies(src, land, send_sems, recv_sems):
    x, y, c, chips = _place()
    me = 2 * x + y
    cps = []
    for w in range(len(src)):
        rows = _half_rows(c, src[w].shape[0])
        for k, (cx, cy) in enumerate(chips):
            cps.append(_remote(src[w].at[rows], land[w].at[me, rows], send_sems, recv_sems, 3 * w + k, (cx, cy, c)))
    return cps


def _scatter_copies(src, land, send_sems, recv_sems):
    x, y, c, chips = _place()
    cps = []
    for w in range(len(src)):
        for k, (cx, cy) in enumerate(chips):
            cps.append(_remote(src[w].at[2 * cx + cy], land[w].at[k], send_sems, recv_sems, 3 * w + k, (cx, cy, c)))
    return cps


def _exchange_start(copies, srcs, lands, after, name):
    n = len(srcs)

    def body(*refs):
        src, land = refs[:n], refs[n:2 * n]
        send_sems, recv_sems = refs[2 * n + 1], refs[2 * n + 2]
        token = refs[-1]
        for cp in copies(src, land, send_sems, recv_sems):
            cp.start()
        token[...] = jnp.zeros_like(token)

    thru = tuple(pltpu.HBM(a.shape, a.dtype) for a in list(srcs) + list(lands))
    out = _pcall(body, name=name,
                 out_shape=(pltpu.SemaphoreType.DMA((3 * n,)), pltpu.SemaphoreType.DMA((3 * n,))) + thru
                 + (jax.ShapeDtypeStruct((8, LANES), F32),),
                 in_specs=[_HBM] * (2 * n) + [_ANY],
                 out_specs=(_SEM, _SEM) + (_HBM,) * (2 * n) + (pl.BlockSpec(memory_space=pltpu.VMEM),),
                 input_output_aliases={i: 2 + i for i in range(2 * n)},
                 compiler_params=pltpu.CompilerParams(has_side_effects=pltpu.SideEffectType.DATAFLOW_SIDE_EFFECTING))(
        *[pltpu.with_memory_space_constraint(a, pltpu.HBM) for a in list(srcs) + list(lands)], after)
    return out[0], out[1], list(out[2:2 + n]), list(out[2 + n:2 + 2 * n]), out[-1]


def _exchange_wait(copies, send_sems, recv_sems, srcs, lands, after, name):
    n = len(srcs)

    def body(*refs):
        src, land = refs[:n], refs[n:2 * n]
        ssem, rsem = refs[2 * n], refs[2 * n + 1]
        for cp in copies(src, land, ssem, rsem):
            cp.wait_send()
            cp.wait_recv()

    out = _pcall(body, name=name, out_shape=tuple(pltpu.HBM(a.shape, a.dtype) for a in list(srcs) + list(lands)),
                 in_specs=[_HBM] * (2 * n) + [_SEM, _SEM, _ANY], out_specs=(_HBM,) * (2 * n),
                 input_output_aliases={i: i for i in range(2 * n)},
                 compiler_params=pltpu.CompilerParams(has_side_effects=pltpu.SideEffectType.DATAFLOW_SIDE_EFFECTING))(
        *srcs, *lands, send_sems, recv_sems, after)
    return list(out[:n]), list(out[n:])


def _forward_rows(gathered):
    n = len(gathered)

    def body(*refs):
        src, dst = refs[:n], refs[n:2 * n]
        send_sems, recv_sems = refs[2 * n:]
        x, y, c, chips = _place()
        sib = (x, y, 1 - c)
        cps = []
        for w in range(n):
            K = src[w].shape[1]
            for k, (cx, cy) in enumerate(chips):
                blk = (2 * cx + cy, _half_rows(c, K))
                cps.append(_remote(src[w].at[blk], dst[w].at[blk], send_sems, recv_sems, 3 * w + k, sib))
        for cp in cps:
            cp.start()
        for w in range(n):
            K = src[w].shape[1]
            for k, (cx, cy) in enumerate(chips):
                blk = dst[w].at[2 * cx + cy, _half_rows(1 - c, K)]
                _remote(blk, blk, send_sems, recv_sems, 3 * w + k, sib).wait_recv()
        for cp in cps:
            cp.wait_send()

    return _pcall(body, name="forward_rows", in_specs=[_HBM] * n, out_specs=(_HBM,) * n,
                  out_shape=tuple(jax.ShapeDtypeStruct(g.shape, g.dtype) for g in gathered),
                  input_output_aliases={w: w for w in range(n)},
                  scratch_shapes=[pltpu.SemaphoreType.DMA((3 * n,)), pltpu.SemaphoreType.DMA((3 * n,))])(*gathered)


def _swap_rows(bufs):
    n = len(bufs)

    def body(*refs):
        src, dst = refs[:n], refs[n:2 * n]
        send_sems, recv_sems = refs[2 * n:]
        x, y, c, _ = _place()
        cps = [_remote(src[w].at[:, _half_rows(1 - c, src[w].shape[1])], dst[w], send_sems, recv_sems, w,
                       (x, y, 1 - c)) for w in range(n)]
        for cp in cps:
            cp.start()
        for cp in cps:
            cp.wait()

    return _pcall(body, name="swap_rows", in_specs=[_HBM] * n, out_specs=(_HBM,) * n,
                  out_shape=tuple(jax.ShapeDtypeStruct((b.shape[0], b.shape[1] // 2, b.shape[2]), b.dtype) for b in bufs),
                  scratch_shapes=[pltpu.SemaphoreType.DMA((n,)), pltpu.SemaphoreType.DMA((n,))])(*bufs)


def _join_rows(fins):
    n = len(fins)

    def body(*refs):
        src, dst = refs[:n], refs[n:2 * n]
        send_sems, recv_sems = refs[2 * n:]
        x, y, c, _ = _place()
        sib = (x, y, 1 - c)
        cps = []
        for w in range(n):
            blk = (slice(None), _half_rows(c, src[w].shape[1]))
            cps.append(_remote(src[w].at[blk], dst[w].at[blk], send_sems, recv_sems, w, sib))
        for cp in cps:
            cp.start()
        for w in range(n):
            blk = dst[w].at[:, _half_rows(1 - c, src[w].shape[1])]
            _remote(blk, blk, send_sems, recv_sems, w, sib).wait_recv()
        for cp in cps:
            cp.wait_send()

    return _pcall(body, name="join_rows", in_specs=[_HBM] * n, out_specs=(_HBM,) * n,
                  out_shape=tuple(jax.ShapeDtypeStruct(f.shape, f.dtype) for f in fins),
                  input_output_aliases={w: w for w in range(n)},
                  scratch_shapes=[pltpu.SemaphoreType.DMA((n,)), pltpu.SemaphoreType.DMA((n,))])(*fins)


def _add_rows(buf, got, c_idx):
    _, Kh, N = got.shape
    bk = _row_block(Kh, 256)
    nb = Kh // bk

    def body(c_ref, a_ref, b_ref, o_ref):
        o_ref[...] = (a_ref[...].astype(F32) + b_ref[...].astype(F32)).astype(o_ref.dtype)

    blk = pl.BlockSpec((None, bk, N), lambda j, i, c_ref: (j, i, 0))
    grid_spec = pltpu.PrefetchScalarGridSpec(
        num_scalar_prefetch=1, grid=(N_CHIPS, nb),
        in_specs=[pl.BlockSpec((None, bk, N), lambda j, i, c_ref: (j, c_ref[0] * nb + i, 0)), blk], out_specs=blk)
    return _pcall(body, name="add_rows", grid_spec=grid_spec, out_shape=jax.ShapeDtypeStruct((N_CHIPS, Kh, N), BF16),
                  compiler_params=_params(("parallel", "parallel")))(c_idx, buf, got)


def _add_chip_sums(s1, got, idx, layer, into, after):
    _, Kh, N = s1.shape
    bk = _row_block(Kh, 256)
    nb = Kh // bk

    def body(idx_ref, a_ref, b_ref, *rest):
        acc = a_ref[...].astype(F32)
        for k in range(3):
            acc = acc + b_ref[k].astype(F32)
        rest[-1][...] = acc

    in_specs = [pl.BlockSpec((None, bk, N), lambda i, idx_ref: (idx_ref[0], i, 0)),
                pl.BlockSpec((3, bk, N), lambda i, idx_ref: (0, i, 0))]
    in_specs.append(_ANY)
    ops = [idx, s1, got, after]
    if into is not None:
        in_specs.append(_ANY)
        ops.append(into)
    grid_spec = pltpu.PrefetchScalarGridSpec(
        num_scalar_prefetch=1, grid=(nb,), in_specs=in_specs,
        out_specs=pl.BlockSpec((None, bk, N), lambda i, idx_ref: (layer, idx_ref[1] * nb + i, 0)))
    return _pcall(body, name="add_chip_sums" if into is None else "add_chip_sums_into", grid_spec=grid_spec,
                  out_shape=jax.ShapeDtypeStruct((2, 2 * Kh, N), F32),
                  input_output_aliases={} if into is None else {4: 0},
                  compiler_params=_params(("parallel",)))(*ops)


def _allreduce_small(v, after):
    R, W = v.shape

    def body(v_ref, after_ref, o_ref, buf, send_sems, recv_sems):
        x, y, c, _ = _place()
        me = 4 * x + 2 * y + c
        buf[0] = v_ref[...]
        cps = []
        for r in range(1, 8):
            peer = (x if not r & 4 else 1 - x, y if not r & 2 else 1 - y, c if not r & 1 else 1 - c)
            cp = _remote(v_ref, buf.at[r], send_sems, recv_sems, r - 1, peer)
            cp.start()
            cps.append(cp)
        for cp in cps:
            cp.wait()
        acc = buf[me]
        for d in range(1, 8):
            acc = acc + buf[jnp.bitwise_xor(me, d)]
        o_ref[...] = acc

    return _pcall(body, name="allreduce_small", out_shape=jax.ShapeDtypeStruct((R, W), F32),
                  in_specs=[pl.BlockSpec(memory_space=pltpu.VMEM), _ANY],
                  out_specs=pl.BlockSpec(memory_space=pltpu.VMEM),
                  scratch_shapes=[pltpu.VMEM((8, R, W), F32), pltpu.SemaphoreType.DMA((7,)),
                                  pltpu.SemaphoreType.DMA((7,))])(v, after)


def _cat_cols(g):
    return g.transpose(1, 0, 2).reshape(g.shape[1], -1)


def _cut_cols(w):
    K, N = w.shape
    return w.reshape(K, N_CHIPS, N // N_CHIPS).transpose(1, 0, 2)


def _regroup_w_in_t(g):
    w = g[:, :552].reshape(2208, g.shape[2])
    zeros = lambda n: jnp.zeros((n, w.shape[1]), w.dtype)
    return jnp.concatenate([w[0:768], w[1152:1408], zeros(64), w[1408:1440], zeros(32), w[768:1152], w[1440:2208]],
                           axis=0)


def _ungroup_w_in_t(w):
    nat = jnp.concatenate([w[0:768], w[Z_Q:Z_Q + 384], w[Z_KV:Z_KV + 256], w[Z_KR + 64:Z_KR + 96], w[Z_CV:Z_W]],
                          axis=0)
    return jnp.pad(nat.reshape(N_CHIPS, 552, w.shape[1]), ((0, 0), (0, 24), (0, 0)))


def _regroup_w_uq_t(g):
    K = g.shape[2]
    return jnp.pad(g.reshape(MLA_HEADS, 96, K), ((0, 0), (0, 32), (0, 0))).reshape(MLA_HEADS * LANES, K)


def _ungroup_w_uq_t(w):
    K = w.shape[1]
    return w.reshape(MLA_HEADS, LANES, K)[:, :96].reshape(N_CHIPS, 192, K)


def _regroup_w_ukv(w):
    K = w.shape[0]
    t = w.reshape(K, MLA_HEADS, 128)
    return jnp.concatenate([jnp.pad(t[:, :, :64], ((0, 0), (0, 0), (0, 64))).reshape(K, MLA_HEADS * LANES),
                            t[:, :, 64:].reshape(K, MLA_HEADS * HEAD_DIM)], axis=1)


def _ungroup_w_ukv(w):
    K = w.shape[0]
    return jnp.concatenate([w[:, :MLA_HEADS * LANES].reshape(K, MLA_HEADS, LANES)[:, :, :64],
                            w[:, MLA_HEADS * LANES:].reshape(K, MLA_HEADS, 64)], axis=2).reshape(K, 1024)


def _ffn_fwd(x, g_pre, g_post, fetch, tag):
    W = fetch("ffn%sa" % tag, x)
    h = _norm_fwd(x, g_pre, "norm_fwd")
    G, U, A = _ffn_gate_up(h, W["w%s_gate" % tag], W["w%s_up" % tag], "ffn_gate_up")
    W = fetch("ffn%sb" % tag, A)
    Y = _mm([(A, W["w%s_down" % tag])], "nn", F32, "ffn_down", bm=1024)
    return _resid_norm(x, Y, g_post, 0.5, "resid_norm"), dict(x=x, h=h, G=G, U=U, A=A, Y=Y)


def _ffn_bwd(dxo, sv, g_pre, g_post, wg, wu, wd_rows, ready):
    dY, dg_post = _norm_bwd(sv["Y"], g_post, dxo, 0.5, None, BF16, "norm_bwd_post")
    dG, dU = _ffn_bwd_mid(dY, wd_rows, sv["G"], sv["U"], "ffn_bwd_mid")
    cut = lambda t: t.reshape(N_CHIPS, -1, t.shape[1])
    token = ready((cut(_mm([(dG, sv["h"])], "tn", BF16, "ffn_dw_in")), cut(_mm([(dU, sv["h"])], "tn", BF16, "ffn_dw_in")),
                   cut(_mm([(sv["A"], dY)], "tn", BF16, "ffn_dw_down"))))
    dh = _mm([(dG, wg), (dU, wu)], "nn", F32, "ffn_dh")
    dx, dg_pre = _norm_bwd(sv["x"], _ordered_after(g_pre, token), dh, 1.0, dxo, F32, "norm_bwd_pre")
    return dx, dg_pre, dg_post, token


def _ordered_after(g, token):
    return g if token is None else g + token[0, 0]


GATHER_GROUPS = (("ffn1a", ("w1_gate", "w1_up")), ("ffn1b", ("w1_down",)),
                 ("mix", ("w_in", "w_mla_uq", "w_mla_ukv", "w_out")),
                 ("ffn2a", ("w2_gate", "w2_up", "w2_down", "w_ple_gate", "w_ple_proj")))
SCATTER_GROUPS = (("ffn2", ("w2_gate", "w2_up", "w2_down", "w_ple_gate", "w_ple_proj")),
                  ("mix", ("w_in", "w_mla_uq", "w_mla_ukv", "w_out")),
                  ("ffn1", ("w1_gate", "w1_up", "w1_down")))


def _layer_fwd(x, p_l, fetch, G, w_conv, cp, sp):
    sv = {}
    x1, sv["ffn1"] = _ffn_fwd(x, G["g_ffn1_pre"], G["g_ffn1_post"], fetch, "1")

    W = fetch("mix", x1)
    h2 = _norm_fwd(x1, G["g_mix_pre"], "norm_fwd")
    Z = _mm([(h2, W["w_in"])], "nt", F32, "mix_in")
    o_sb, tot_sb = _sb_fwd(Z)
    cq, ckv = Z[:, Z_Q:Z_Q + 384], Z[:, Z_KV:Z_KV + 256]
    nq = _norm_fwd(cq, G["g_mla_q"], "norm_fwd_q")
    Qf = _mm([(nq, W["w_mla_uq"])], "nt", F32, "mla_uq")
    nkv = _norm_fwd(ckv, G["g_mla_kv"], "norm_fwd_kv")
    KV = _mm([(nkv, W["w_mla_ukv"])], "nn", F32, "mla_ukv")
    qcat, kcat = _mla_prep(Qf, KV, Z, cp, sp)
    o_mla, lse = _mla_fwd(qcat, kcat, KV)
    y_cv = _conv_fwd(Z, w_conv)
    C = jnp.concatenate([o_sb, o_mla, y_cv], axis=1).astype(BF16)
    Mx = _mm([(C, W["w_out"])], "nn", F32, "mix_out")
    x2 = _resid_norm(x1, Mx, G["g_mix_post"], 1.0, "resid_norm")
    sv["mix"] = dict(x=x1, h=h2, Z=Z, tot_sb=tot_sb, cq=cq, ckv=ckv, nq=nq, nkv=nkv, qcat=qcat, kcat=kcat, KV=KV,
                     o_mla=o_mla, lse=lse, C=C, Mx=Mx)

    x3, sv["ffn2"] = _ffn_fwd(x2, G["g_ffn2_pre"], G["g_ffn2_post"], fetch, "2")
    W = fetch("ffn2b", x3)

    h4 = _norm_fwd(x3, G["g_ple_pre"], "norm_fwd")
    p16 = p_l.astype(BF16)
    Qg = _mm([(h4, W["w_ple_gate"])], "nn", F32, "ple_gate")
    Pp = _ple_proj(p16, W["w_ple_proj"])
    e = _ew(lambda q, pp: _sigmoid(q) * pp, [Qg, Pp], [(D_MODEL, F32)], "ple_mul")
    x4 = _resid_norm(x3, e, G["g_ple_post"], 1.0, "resid_norm")
    sv["ple"] = dict(x=x3, h=h4, p16=p16, Qg=Qg, Pp=Pp, e=e)
    sv["W"] = W
    return x4, sv


def _layer_bwd(dx4, sv, G, w_conv, cp, sp, emit, token):
    gg, gw, W = {}, {}, sv["W"]

    s = sv["ple"]
    de, gg["g_ple_post"] = _norm_bwd(s["e"], _ordered_after(G["g_ple_post"], token), dx4, 1.0, None, F32,
                                     "norm_bwd_e")

    def ple_bwd(dev, q, pp):
        sg = _sigmoid(q)
        return dev * pp * sg * (1.0 - sg), dev * sg

    dQg, dPp = _ew(ple_bwd, [de, s["Qg"], s["Pp"]], [(D_MODEL, BF16)] * 2, "ple_mul_bwd")
    cut = lambda t: t.reshape(N_CHIPS, -1, t.shape[1])
    gw["w_ple_proj"] = _dw_col_shards(s["p16"], dPp, "ple_dw_proj")
    gw["w_ple_gate"] = cut(_mm([(s["h"], dQg)], "tn", BF16, "ple_dw_gate"))
    dh4 = _mm([(dQg, W["w_ple_gate"])], "nt", F32, "ple_dh")
    dx3, gg["g_ple_pre"] = _norm_bwd(s["x"], G["g_ple_pre"], dh4, 1.0, dx4, F32, "norm_bwd_pre")

    def ready2(dws):
        gw["w2_gate"], gw["w2_up"], gw["w2_down"] = dws
        return emit("ffn2", gw)

    dx2, gg["g_ffn2_pre"], gg["g_ffn2_post"], token = _ffn_bwd(
        dx3, sv["ffn2"], G["g_ffn2_pre"], G["g_ffn2_post"], W["w2_gate"], W["w2_up"], W["w2_down"], ready2)

    s = sv["mix"]
    dM, gg["g_mix_post"] = _norm_bwd(s["Mx"], _ordered_after(G["g_mix_post"], token), dx2, 1.0, None, BF16,
                                     "norm_bwd_post")
    dC = _mm([(dM, W["w_out"])], "nt", F32, "mix_out_dx")
    gw["w_out"] = cut(_mm([(s["C"], dM)], "tn", BF16, "mix_out_dw"))

    db, dc, dhh, gg["w_conv"] = _conv_bwd(s["Z"], w_conv, dC)

    dqc, dkc, dv = _mla_bwd(s["qcat"], s["kcat"], s["KV"], s["o_mla"], s["lse"], dC)
    dQf, dkr = _mla_prep_bwd(dqc, dkc, cp, sp)
    gw["w_mla_uq"] = _ungroup_w_uq_t(_mm([(dQf, s["nq"])], "tn", BF16, "mla_uq_dw"))
    dnq = _mm([(dQf, W["w_mla_uq"])], "nn", F32, "mla_uq_dx")
    dcq, gg["g_mla_q"] = _norm_bwd(s["cq"], G["g_mla_q"], dnq, 1.0, None, F32, "norm_bwd_q")
    dkv = jnp.concatenate([dkc, dv], axis=1).astype(BF16)
    gw["w_mla_ukv"] = _cut_cols(_ungroup_w_ukv(_mm([(s["nkv"], dkv)], "tn", BF16, "mla_ukv_dw")))
    dnkv = _mm([(dkv, W["w_mla_ukv"])], "nt", F32, "mla_ukv_dx")
    dckv, gg["g_mla_kv"] = _norm_bwd(s["ckv"], G["g_mla_kv"], dnkv, 1.0, None, F32, "norm_bwd_kv")

    dsq, dsk, dsv = _sb_bwd(s["Z"], s["tot_sb"], dC)
    dZ = jnp.concatenate([dsq, dsk, dsv, dckv, dkr, dcq, db, dc, dhh], axis=1).astype(BF16)
    gw["w_in"] = _ungroup_w_in_t(_mm([(dZ, s["h"])], "tn", BF16, "mix_in_dw"))
    dh2 = _mm([(dZ, W["w_in"])], "nn", F32, "mix_in_dx")
    dx1, gg["g_mix_pre"] = _norm_bwd(s["x"], G["g_mix_pre"], dh2, 1.0, dx2, F32, "norm_bwd_pre")
    started_mix = emit("mix", gw)
    token = token if started_mix is None else started_mix

    def ready1(dws):
        gw["w1_gate"], gw["w1_up"], gw["w1_down"] = dws
        return emit("ffn1", gw)

    dx0, gg["g_ffn1_pre"], gg["g_ffn1_post"], started_ffn1 = _ffn_bwd(
        dx1, sv["ffn1"], G["g_ffn1_pre"], _ordered_after(G["g_ffn1_post"], token), W["w1_gate"], W["w1_up"],
        W["w1_down"], ready1)
    return dx0, gg, token if started_ffn1 is None else started_ffn1


def _pack_small(vecs):
    flat = jnp.concatenate([v.reshape(-1) for v in vecs])
    rows = -(-flat.shape[0] // (8 * LANES)) * 8
    return jnp.pad(flat, (0, rows * LANES - flat.shape[0])).reshape(rows, LANES)


def _group_weights(names, landed, shards, chip):
    out = {}
    for name, g, own in zip(names, landed, shards):
        g = lax.dynamic_update_slice(g, own[None], (chip, 0, 0))
        if name == "w_in":
            g = _regroup_w_in_t(g)
        elif name == "w_mla_uq":
            g = _regroup_w_uq_t(g)
        elif name not in ("w_mla_ukv", "w_ple_proj"):
            g = g.reshape(-1, g.shape[2])
        elif name == "w_mla_ukv":
            g = _regroup_w_ukv(_cat_cols(g))
        out[name] = g
    return out


def kernel(x, p, positions, g_ffn1_pre, w1_gate, w1_up, w1_down, g_ffn1_post, g_mix_pre, w_in, g_mla_q, w_mla_uq, g_mla_kv, w_mla_ukv, w_conv, w_out, g_mix_post, g_ffn2_pre, w2_gate, w2_up, w2_down, g_ffn2_post, g_ple_pre, w_ple_gate, w_ple_proj, g_ple_post, loss_target, m_g_ffn1_pre, m_w1_gate, m_w1_up, m_w1_down, m_g_ffn1_post, m_g_mix_pre, m_w_in, m_g_mla_q, m_w_mla_uq, m_g_mla_kv, m_w_mla_ukv, m_w_conv, m_w_out, m_g_mix_post, m_g_ffn2_pre, m_w2_gate, m_w2_up, m_w2_down, m_g_ffn2_post, m_g_ple_pre, m_w_ple_gate, m_w_ple_proj, m_g_ple_post, v_g_ffn1_pre, v_w1_gate, v_w1_up, v_w1_down, v_g_ffn1_post, v_g_mix_pre, v_w_in, v_g_mla_q, v_w_mla_uq, v_g_mla_kv, v_w_mla_ukv, v_w_conv, v_w_out, v_g_mix_post, v_g_ffn2_pre, v_w2_gate, v_w2_up, v_w2_down, v_g_ffn2_post, v_g_ple_pre, v_w_ple_gate, v_w_ple_proj, v_g_ple_post):
    w = dict(g_ffn1_pre=g_ffn1_pre, w1_gate=w1_gate, w1_up=w1_up, w1_down=w1_down, g_ffn1_post=g_ffn1_post,
             g_mix_pre=g_mix_pre, w_in=w_in, g_mla_q=g_mla_q, w_mla_uq=w_mla_uq, g_mla_kv=g_mla_kv,
             w_mla_ukv=w_mla_ukv, w_conv=w_conv, w_out=w_out, g_mix_post=g_mix_post, g_ffn2_pre=g_ffn2_pre,
             w2_gate=w2_gate, w2_up=w2_up, w2_down=w2_down, g_ffn2_post=g_ffn2_post, g_ple_pre=g_ple_pre,
             w_ple_gate=w_ple_gate, w_ple_proj=w_ple_proj, g_ple_post=g_ple_post)
    m = dict(g_ffn1_pre=m_g_ffn1_pre, w1_gate=m_w1_gate, w1_up=m_w1_up, w1_down=m_w1_down, g_ffn1_post=m_g_ffn1_post,
             g_mix_pre=m_g_mix_pre, w_in=m_w_in, g_mla_q=m_g_mla_q, w_mla_uq=m_w_mla_uq, g_mla_kv=m_g_mla_kv,
             w_mla_ukv=m_w_mla_ukv, w_conv=m_w_conv, w_out=m_w_out, g_mix_post=m_g_mix_post, g_ffn2_pre=m_g_ffn2_pre,
             w2_gate=m_w2_gate, w2_up=m_w2_up, w2_down=m_w2_down, g_ffn2_post=m_g_ffn2_post, g_ple_pre=m_g_ple_pre,
             w_ple_gate=m_w_ple_gate, w_ple_proj=m_w_ple_proj, g_ple_post=m_g_ple_post)
    v = dict(g_ffn1_pre=v_g_ffn1_pre, w1_gate=v_w1_gate, w1_up=v_w1_up, w1_down=v_w1_down, g_ffn1_post=v_g_ffn1_post,
             g_mix_pre=v_g_mix_pre, w_in=v_w_in, g_mla_q=v_g_mla_q, w_mla_uq=v_w_mla_uq, g_mla_kv=v_g_mla_kv,
             w_mla_ukv=v_w_mla_ukv, w_conv=v_w_conv, w_out=v_w_out, g_mix_post=v_g_mix_post, g_ffn2_pre=v_g_ffn2_pre,
             w2_gate=v_w2_gate, w2_up=v_w2_up, w2_down=v_w2_down, g_ffn2_post=v_g_ffn2_post, g_ple_pre=v_g_ple_pre,
             w_ple_gate=v_w_ple_gate, w_ple_proj=v_w_ple_proj, g_ple_post=v_g_ple_post)

    for name in TRANSPOSED:
        w[name], m[name], v[name] = (jnp.swapaxes(t[name], 1, 2) for t in (w, m, v))

    depth = g_ffn1_pre.shape[0]
    assert depth == 2, "the reduced gradients are assembled in [2, K, N] buffers"
    S = x.shape[1]
    cx, cy, cc = lax.axis_index("x"), lax.axis_index("y"), lax.axis_index("c")
    chip = 2 * cx + cy
    c_idx = cc.reshape(1).astype(jnp.int32)
    idx2 = jnp.stack([chip, cc]).astype(jnp.int32)

    conv_slot = lax.dynamic_update_slice(jnp.zeros((depth, 3, 256), F32),
                                         w_conv * (cc == 0).astype(F32), (0, 0, 64 * chip))
    conv_sum = _allreduce_small(_pack_small([conv_slot]), positions)
    w_conv_full = conv_sum[:depth * 3 * 256 // LANES].reshape(depth, 3, 256)

    def padded(name, i):
        (rows, _), rows_p = BIG[name]
        return jnp.pad(w[name][i].astype(BF16), ((0, rows_p - rows), (0, 0)))

    def groups_of(i, groups):
        return groups if i == 0 else (("all", BIG_NAMES),)

    started, after = {}, conv_sum
    for i in range(depth):
        for gname, names in groups_of(i, GATHER_GROUPS):
            shards = [padded(name, i) for name in names]
            lands = [lax.empty((N_CHIPS,) + s.shape, BF16) for s in shards]
            started[i, gname] = _exchange_start(_gather_copies, shards, lands, after, "gather_start_%d_%s" % (i, gname))
            after = started[i, gname][4]
    all_started = after

    inv = ROPE_BASE ** (-jnp.arange(ROPE_HALF, dtype=F32) / ROPE_HALF)
    zeros = lambda n: jnp.zeros((n,), F32)
    ones = jnp.ones((ROPE_HALF,), F32)
    inv_pat = jnp.concatenate([zeros(64), inv, inv, zeros(32)]).reshape(1, LANES)
    sign_pat = jnp.concatenate([zeros(64), -ones, ones, zeros(32)]).reshape(1, LANES)
    cp, sp = _rope_tables(positions.reshape(S, 1), inv_pat, sign_pat)

    def fetcher(i):
        table, have = dict(groups_of(i, GATHER_GROUPS)), {}

        def fetch(group, after):
            key = group if group in table else "all"
            if (i, key) in started:
                ssem, rsem, srcs, lands, _ = started.pop((i, key))
                if i == 0 and group == "ffn1a":
                    after = all_started
                srcs, landed = _exchange_wait(_gather_copies, ssem, rsem, srcs, lands, after,
                                              "gather_wait_%d_%s" % (i, key))
                have.update(_group_weights(table[key], _forward_rows(landed), srcs, chip))
            return have

        return fetch

    xs, saved = x[0], []
    gains = [{name: w[name][i].reshape(1, n) for name, n in GAINS} for i in range(depth)]
    for i in range(depth):
        xs, sv = _layer_fwd(xs, p[i, 0], fetcher(i), gains[i], w_conv_full[i], cp, sp)
        saved.append(sv)

    loss_part, dx = _loss_and_grad(xs, loss_target[0])
    loss = lax.psum(loss_part[0, 0], AXES)

    scattering = []

    def emitter(i):
        table = dict(groups_of(i, SCATTER_GROUPS))

        def emit(group, gw):
            if group in table:
                names = table[group]
            elif group == "ffn1":
                names = table["all"]
            else:
                return None
            full = [gw[name] for name in names]
            pair_sums = [_add_rows(b, g, c_idx) for b, g in zip(full, _swap_rows(full))]
            lands = [lax.empty((3,) + s.shape[1:], BF16) for s in pair_sums]
            st = _exchange_start(_scatter_copies, pair_sums, lands, c_idx, "scatter_start_%d_%s" % (i, group))
            scattering.append((i, group, names, st))
            return st[4]

        return emit

    ggs, token = [None] * depth, None
    for i in reversed(range(depth)):
        dx, ggs[i], token = _layer_bwd(dx, saved[i], gains[i], w_conv_full[i], cp, sp, emitter(i), token)

    fins, after = {name: None for name in BIG_NAMES}, dx
    for i, group, names, (ssem, rsem, srcs, lands, _) in scattering:
        srcs, arrived = _exchange_wait(_scatter_copies, ssem, rsem, srcs, lands, after,
                                       "scatter_wait_%d_%s" % (i, group))
        for name, s1, got in zip(names, srcs, arrived):
            after = fins[name] = _add_chip_sums(s1, got, idx2, i, fins[name], after)
    joined = _join_rows([fins[name] for name in BIG_NAMES])
    reduced = dict(zip(BIG_NAMES, joined))

    small_names = [name for name, _ in GAINS] + ["w_conv"]
    small = _allreduce_small(_pack_small([jnp.stack([ggs[i][name] for i in range(depth)]) for name in small_names]),
                             joined[0])
    flat, off = small.reshape(-1), 0
    for name, n in GAINS:
        reduced[name] = flat[off:off + depth * n].reshape(depth, n)
        off += depth * n
    conv_full = flat[off:off + depth * 3 * 256].reshape(depth, 3, 256)
    reduced["w_conv"] = lax.dynamic_slice(conv_full, (0, 0, 64 * chip), (depth, 3, 64))

    grads, deltas, new_m, new_v = {}, {}, {}, {}
    for name in WEIGHT_ORDER:
        shape = w[name].shape
        three_d = shape if len(shape) == 3 else (1,) + shape
        g_ = reduced[name] if len(shape) == 3 else reduced[name].reshape(three_d)
        outs = _adamw(w[name].reshape(three_d), g_, m[name].reshape(three_d), v[name].reshape(three_d), "adamw")
        if name in TRANSPOSED:
            outs = [jnp.swapaxes(o, 1, 2) for o in outs]
        grads[name], deltas[name], new_m[name], new_v[name] = (o.reshape(o.shape[-len(shape):]) for o in outs)

    return (loss, dx[None], *[grads[n] for n in WEIGHT_ORDER], *[deltas[n] for n in WEIGHT_ORDER],
            *[new_m[n] for n in WEIGHT_ORDER], *[new_v[n] for n in WEIGHT_ORDER])
```

```python
import jax
import jax.numpy as jnp
from jax import lax
from jax.experimental import pallas as pl
from jax.experimental.pallas import tpu as pltpu

F32 = jnp.float32
BF16 = jnp.bfloat16
MESH = pl.DeviceIdType.MESH
AXES = ("x", "y", "c")

D_MODEL = 1024
N_CHIPS = 4
EPS = 1e-6
NEG_INF = -1e30
ROPE_BASE = 10000.0
ROPE_HALF = 16
LANES = 128
SB_HEADS, MLA_HEADS, HEAD_DIM = 4, 8, 64
MLA_SCALE = 96.0 ** -0.5
SB_SCALE = 64.0 ** -0.5
SB_BQ, SB_BK = 512, 128
MLA_BQ, MLA_BK = 256, 256
ADAM_LR, ADAM_B1, ADAM_B2, ADAM_EPS, ADAM_WD, ADAM_STEP = 0.001, 0.9, 0.999, 1e-08, 0.01, 10
VMEM_LIMIT = 48 * 2 ** 20

Z_SB, Z_KV, Z_KR, Z_Q, Z_CV, Z_W = 0, 768, 1024, 1152, 1536, 2304

NT = (((1,), (1,)), ((), ()))
TN = (((0,), (0,)), ((), ()))

TRANSPOSED = ("w1_gate", "w1_up", "w2_gate", "w2_up", "w_in", "w_mla_uq")
BIG = {"w1_gate": ((704, 1024), 704), "w1_up": ((704, 1024), 704), "w1_down": ((704, 1024), 704),
       "w_in": ((552, 1024), 576), "w_mla_uq": ((192, 384), 192), "w_mla_ukv": ((256, 256), 256),
       "w_out": ((256, 1024), 256),
       "w2_gate": ((704, 1024), 704), "w2_up": ((704, 1024), 704), "w2_down": ((704, 1024), 704),
       "w_ple_gate": ((256, 1024), 256), "w_ple_proj": ((256, 256), 256)}
BIG_NAMES = tuple(BIG)
GAINS = (("g_ffn1_pre", 1024), ("g_ffn1_post", 1024), ("g_mix_pre", 1024), ("g_mla_q", 384),
         ("g_mla_kv", 256), ("g_mix_post", 1024), ("g_ffn2_pre", 1024), ("g_ffn2_post", 1024),
         ("g_ple_pre", 1024), ("g_ple_post", 1024))
WEIGHT_ORDER = ("g_ffn1_pre", "w1_gate", "w1_up", "w1_down", "g_ffn1_post", "g_mix_pre", "w_in", "g_mla_q",
                "w_mla_uq", "g_mla_kv", "w_mla_ukv", "w_conv", "w_out", "g_mix_post", "g_ffn2_pre", "w2_gate",
                "w2_up", "w2_down", "g_ffn2_post", "g_ple_pre", "w_ple_gate", "w_ple_proj", "g_ple_post")

_pcall = pl.pallas_call


def _params(sem=None):
    return pltpu.CompilerParams(dimension_semantics=sem, vmem_limit_bytes=VMEM_LIMIT)


def _dot(a, b, dims=None):
    a, b = a.astype(BF16), b.astype(BF16)
    if dims is None:
        return jnp.dot(a, b, preferred_element_type=F32)
    return lax.dot_general(a, b, dims, preferred_element_type=F32)


def _rstd(v):
    return lax.rsqrt(jnp.mean(v * v, axis=-1, keepdims=True) + EPS)


def _sigmoid(v):
    return 0.5 * jnp.tanh(0.5 * v) + 0.5


def _row_block(n, want, mult=16):
    for b in range(min(n, want), 0, -1):
        if n % b == 0 and b % mult == 0:
            return b
    return n


def _mm(pairs, mode, out_dtype, name, bm=512, bn=512):
    a0, b0 = pairs[0]
    if mode == "nn":
        M, N = a0.shape[0], b0.shape[1]
    elif mode == "nt":
        M, N = a0.shape[0], b0.shape[0]
    else:
        M, N = a0.shape[1], b0.shape[1]
        bn = max(bn, 1024)
    bm, bn = _row_block(M, bm, 128 if mode == "tn" else 16), _row_block(N, bn, 128)
    n_pairs = len(pairs)
    dims = {"nn": None, "nt": NT, "tn": TN}[mode]

    def body(*refs):
        acc = None
        for t in range(n_pairs):
            part = _dot(refs[2 * t][...], refs[2 * t + 1][...], dims)
            acc = part if acc is None else acc + part
        refs[-1][...] = acc.astype(refs[-1].dtype)

    in_specs, ops = [], []
    for a, b in pairs:
        if mode == "nn":
            in_specs += [pl.BlockSpec((bm, a.shape[1]), lambda j, i: (i, 0)),
                         pl.BlockSpec((b.shape[0], bn), lambda j, i: (0, j))]
        elif mode == "nt":
            in_specs += [pl.BlockSpec((bm, a.shape[1]), lambda j, i: (i, 0)),
                         pl.BlockSpec((bn, b.shape[1]), lambda j, i: (j, 0))]
        else:
            in_specs += [pl.BlockSpec((a.shape[0], bm), lambda j, i: (0, i)),
                         pl.BlockSpec((b.shape[0], bn), lambda j, i: (0, j))]
        ops += [a, b]
    return _pcall(body, name=name, grid=(N // bn, M // bm), in_specs=in_specs,
                  out_specs=pl.BlockSpec((bm, bn), lambda j, i: (i, j)),
                  out_shape=jax.ShapeDtypeStruct((M, N), out_dtype),
                  compiler_params=_params(("parallel", "parallel")))(*ops)


def _dw_col_shards(a, b, name):
    S, K = a.shape
    Np = b.shape[1] // N_CHIPS

    def body(a_ref, b_ref, o_ref):
        o_ref[...] = _dot(a_ref[...], b_ref[...], TN).astype(BF16)

    return _pcall(body, name=name, grid=(N_CHIPS,),
                  in_specs=[pl.BlockSpec((S, K), lambda j: (0, 0)), pl.BlockSpec((S, Np), lambda j: (0, j))],
                  out_specs=pl.BlockSpec((None, K, Np), lambda j: (j, 0, 0)),
                  out_shape=jax.ShapeDtypeStruct((N_CHIPS, K, Np), BF16),
                  compiler_params=_params(("parallel",)))(a, b)


def _ew(fn, ins, outs, name, br=512):
    S = max(a.shape[0] for a in ins)
    br = _row_block(S, br)
    n_in = len(ins)

    def body(*refs):
        res = fn(*[r[...] for r in refs[:n_in]])
        if not isinstance(res, tuple):
            res = (res,)
        for r, v in zip(refs[n_in:], res):
            r[...] = v.astype(r.dtype)

    in_specs = [pl.BlockSpec((br, a.shape[1]), lambda i: (i, 0)) if a.shape[0] == S and S > 1
                else pl.BlockSpec(a.shape, lambda i: (0, 0)) for a in ins]
    out = _pcall(body, name=name, grid=(S // br,), in_specs=in_specs,
                 out_specs=tuple(pl.BlockSpec((br, w), lambda i: (i, 0)) for w, _ in outs),
                 out_shape=tuple(jax.ShapeDtypeStruct((S, w), dt) for w, dt in outs),
                 compiler_params=_params(("parallel",)))(*ins)
    return out if len(outs) > 1 else out[0]


def _norm_fwd(x, g, name):
    return _ew(lambda xv, gv: xv * _rstd(xv) * gv, [x, g], [(x.shape[1], BF16)], name, br=512)


def _resid_norm(x, y, g, alpha, name):
    return _ew(lambda xv, yv, gv: xv + alpha * (yv * _rstd(yv) * gv), [x, y, g], [(x.shape[1], F32)], name)


def _norm_bwd(xin, g, dy, alpha, resid, out_dtype, name):
    S, W = xin.shape
    br = _row_block(S, 512)
    has_res = resid is not None

    def body(*refs):
        x_ref, g_ref, dy_ref = refs[:3]
        dx_ref, dg_ref = refs[-2:]
        xv, dyv = x_ref[...], dy_ref[...] * alpha
        r = _rstd(xv)
        xh = xv * r
        u = dyv * g_ref[...]
        dx = r * (u - xh * jnp.mean(u * xh, axis=-1, keepdims=True))
        if has_res:
            dx = dx + refs[3][...]
        dx_ref[...] = dx.astype(dx_ref.dtype)
        part = jnp.sum(dyv * xh, axis=0, keepdims=True)

        @pl.when(pl.program_id(0) == 0)
        def _():
            dg_ref[...] = part

        @pl.when(pl.program_id(0) > 0)
        def _():
            dg_ref[...] += part

    row = pl.BlockSpec((br, W), lambda i: (i, 0))
    vec = pl.BlockSpec((1, W), lambda i: (0, 0))
    ops = [xin, g, dy] + ([resid] if has_res else [])
    return _pcall(body, name=name, grid=(S // br,), in_specs=[row, vec, row] + ([row] if has_res else []),
                  out_specs=(row, vec),
                  out_shape=(jax.ShapeDtypeStruct((S, W), out_dtype), jax.ShapeDtypeStruct((1, W), F32)),
                  compiler_params=_params(("arbitrary",)))(*ops)


def _ffn_gate_up(h, wgt, wut, name):
    S, K = h.shape
    F = wgt.shape[0]
    bm, bn = _row_block(S, 512), _row_block(F, 1408, 128)

    def body(h_ref, wg_ref, wu_ref, g_ref, u_ref, a_ref):
        hv = h_ref[...]
        g = _dot(hv, wg_ref[...], NT)
        u = _dot(hv, wu_ref[...], NT)
        g_ref[...] = g.astype(BF16)
        u_ref[...] = u.astype(BF16)
        a_ref[...] = (g * _sigmoid(g) * u).astype(BF16)

    blk = pl.BlockSpec((bm, bn), lambda n, i: (i, n))
    wsp = pl.BlockSpec((bn, K), lambda n, i: (n, 0))
    return _pcall(body, name=name, grid=(F // bn, S // bm),
                  in_specs=[pl.BlockSpec((bm, K), lambda n, i: (i, 0)), wsp, wsp],
                  out_specs=(blk, blk, blk), out_shape=(jax.ShapeDtypeStruct((S, F), BF16),) * 3,
                  compiler_params=_params(("parallel", "parallel")))(h, wgt, wut)


def _ffn_bwd_mid(dy, wd, g, u, name):
    S, D = dy.shape
    F = g.shape[1]
    bm, bn = _row_block(S, 512), _row_block(F, 1408, 128)

    def body(dy_ref, wd_ref, g_ref, u_ref, dg_ref, du_ref):
        da = _dot(dy_ref[...], wd_ref[...], NT)
        gv, uv = g_ref[...].astype(F32), u_ref[...].astype(F32)
        s = _sigmoid(gv)
        dg_ref[...] = (da * uv * (s * (1.0 + gv * (1.0 - s)))).astype(BF16)
        du_ref[...] = (da * (gv * s)).astype(BF16)

    blk = pl.BlockSpec((bm, bn), lambda j, i: (i, j))
    return _pcall(body, name=name, grid=(F // bn, S // bm),
                  in_specs=[pl.BlockSpec((bm, D), lambda j, i: (i, 0)),
                            pl.BlockSpec((bn, D), lambda j, i: (j, 0)), blk, blk],
                  out_specs=(blk, blk), out_shape=(jax.ShapeDtypeStruct((S, F), BF16),) * 2,
                  compiler_params=_params(("parallel", "parallel")))(dy, wd, g, u)


def _ple_proj(p16, w):
    S, K = p16.shape
    bm = _row_block(S, 1024)

    def body(p_ref, w_ref, o_ref):
        o_ref[...] = _dot(p_ref[...], w_ref[...])

    return _pcall(body, name="ple_proj", grid=(N_CHIPS, S // bm),
                  in_specs=[pl.BlockSpec((bm, K), lambda j, i: (i, 0)),
                            pl.BlockSpec((None, K, 256), lambda j, i: (j, 0, 0))],
                  out_specs=pl.BlockSpec((bm, 256), lambda j, i: (i, j)),
                  out_shape=jax.ShapeDtypeStruct((S, N_CHIPS * 256), F32),
                  compiler_params=_params(("parallel", "parallel")))(p16, w)


def _loss_and_grad(xf, tgt):
    S, W = xf.shape
    br = _row_block(S, 256)

    def body(x_ref, t_ref, l_ref, d_ref):
        d = x_ref[...] - t_ref[...]
        d_ref[...] = d * (1.0 / W)
        part = 0.5 * jnp.sum(jnp.sum(d * d, axis=-1, keepdims=True) * (1.0 / W), axis=0, keepdims=True)

        @pl.when(pl.program_id(0) == 0)
        def _():
            l_ref[...] = part

        @pl.when(pl.program_id(0) > 0)
        def _():
            l_ref[...] += part

    row = pl.BlockSpec((br, W), lambda i: (i, 0))
    return _pcall(body, name="loss", grid=(S // br,), in_specs=[row, row],
                  out_specs=(pl.BlockSpec((1, 1), lambda i: (0, 0)), row),
                  out_shape=(jax.ShapeDtypeStruct((1, 1), F32), jax.ShapeDtypeStruct((S, W), F32)),
                  compiler_params=_params(("arbitrary",)))(xf, tgt)


def _adamw(w, g, m, v, name):
    L, R, C = w.shape
    Cp = g.shape[2]
    br = _row_block(R, 256, 8) if R % 8 == 0 else R

    def body(w_ref, g_ref, m_ref, v_ref, go_ref, d_ref, nm_ref, nv_ref):
        gv = g_ref[...][:, :C]
        nm = ADAM_B1 * m_ref[...] + (1.0 - ADAM_B1) * gv
        nv = ADAM_B2 * v_ref[...] + (1.0 - ADAM_B2) * (gv * gv)
        m_hat = nm / (1.0 - ADAM_B1 ** ADAM_STEP)
        v_hat = nv / (1.0 - ADAM_B2 ** ADAM_STEP)
        go_ref[...] = gv
        d_ref[...] = -ADAM_LR * (m_hat / (jnp.sqrt(v_hat) + ADAM_EPS) + ADAM_WD * w_ref[...])
        nm_ref[...] = nm
        nv_ref[...] = nv

    blk = pl.BlockSpec((None, br, C), lambda l, i: (l, i, 0))
    gblk = pl.BlockSpec((None, br, Cp), lambda l, i: (l, i, 0))
    return _pcall(body, name=name, grid=(L, R // br), in_specs=[blk, gblk, blk, blk], out_specs=(blk,) * 4,
                  out_shape=(jax.ShapeDtypeStruct((L, R, C), F32),) * 4,
                  compiler_params=_params(("parallel", "parallel")))(w, g, m, v)


def _rope_tables(pos, inv_pat, sign_pat):
    def fn(p, iv, sg):
        ang = p.astype(F32) * iv
        return jnp.cos(ang), jnp.sin(ang) * sg

    return _ew(fn, [pos, inv_pat, sign_pat], [(LANES, F32)] * 2, "rope_tables")


def _swap_halves_of_rope(v):
    W = v.shape[1]
    lane = lax.broadcasted_iota(jnp.int32, (1, W), 1) % LANES
    return jnp.where((lane >= 64) & (lane < 80), pltpu.roll(v, W - ROPE_HALF, 1),
                     jnp.where((lane >= 80) & (lane < 96), pltpu.roll(v, ROPE_HALF, 1), 0.0))


def _tile_lanes(v, n):
    return jnp.concatenate([v] * n, axis=1)


def _mla_prep(qf, kv, z, cp, sp):
    S = qf.shape[0]
    br = _row_block(S, 256)
    W = MLA_HEADS * LANES

    def body(q_ref, k_ref, r_ref, c_ref, s_ref, qo_ref, ko_ref):
        c, s = c_ref[...], s_ref[...]
        q = q_ref[...]
        qo_ref[...] = (q * _tile_lanes(c, MLA_HEADS) + _swap_halves_of_rope(q) * _tile_lanes(s, MLA_HEADS)).astype(BF16)
        r = r_ref[...]
        lane = lax.broadcasted_iota(jnp.int32, (1, LANES), 1)
        kr = jnp.where((lane >= 64) & (lane < 96), r * c + _swap_halves_of_rope(r) * s, 0.0)
        ko_ref[...] = (k_ref[...] + _tile_lanes(kr, MLA_HEADS)).astype(BF16)

    wide = pl.BlockSpec((br, W), lambda i: (i, 0))
    one = pl.BlockSpec((br, LANES), lambda i: (i, 0))
    return _pcall(body, name="mla_prep", grid=(S // br,),
                  in_specs=[wide, wide, pl.BlockSpec((br, LANES), lambda i: (i, Z_KR // LANES)), one, one],
                  out_specs=(wide, wide), out_shape=(jax.ShapeDtypeStruct((S, W), BF16),) * 2,
                  compiler_params=_params(("parallel",)))(qf, kv, z, cp, sp)


def _mla_prep_bwd(dq, dk, cp, sp):
    S, W = dq.shape
    br = _row_block(S, 256)

    def body(dq_ref, dk_ref, c_ref, s_ref, dqo_ref, dr_ref):
        c, s = c_ref[...], s_ref[...]
        d = dq_ref[...]
        dqo_ref[...] = (d * _tile_lanes(c, MLA_HEADS) + _swap_halves_of_rope(d * _tile_lanes(s, MLA_HEADS))).astype(BF16)
        dkv = dk_ref[...]
        tot = dkv[:, 0:LANES]
        for h in range(1, MLA_HEADS):
            tot = tot + dkv[:, h * LANES:(h + 1) * LANES]
        lane = lax.broadcasted_iota(jnp.int32, (1, LANES), 1)
        tot = jnp.where((lane >= 64) & (lane < 96), tot, 0.0)
        dr_ref[...] = tot * c + _swap_halves_of_rope(tot * s)

    wide = pl.BlockSpec((br, W), lambda i: (i, 0))
    one = pl.BlockSpec((br, LANES), lambda i: (i, 0))
    return _pcall(body, name="mla_prep_bwd", grid=(S // br,), in_specs=[wide, wide, one, one], out_specs=(wide, one),
                  out_shape=(jax.ShapeDtypeStruct((S, W), BF16), jax.ShapeDtypeStruct((S, LANES), F32)),
                  compiler_params=_params(("parallel",)))(dq, dk, cp, sp)


def _shift_down(v, k, row):
    return jnp.where(row >= k, pltpu.roll(v, k, 0), 0.0)


def _shift_up(v, k, row):
    n = v.shape[0]
    return jnp.where(row < n - k, pltpu.roll(v, n - k, 0), 0.0)


def _conv_fwd(z, w):
    S = z.shape[0]
    c0 = Z_CV // LANES

    def body(b_ref, c_ref, h_ref, w_ref, y_ref):
        u = c_ref[...] * h_ref[...]
        row = lax.broadcasted_iota(jnp.int32, u.shape, 0)
        wv = w_ref[...]
        conv = wv[0:1] * _shift_down(u, 2, row) + wv[1:2] * _shift_down(u, 1, row) + wv[2:3] * u
        y_ref[...] = b_ref[...] * conv

    def col(k):
        return pl.BlockSpec((S, LANES), lambda j: (0, c0 + 2 * k + j))

    return _pcall(body, name="conv_fwd", grid=(2,),
                  in_specs=[col(0), col(1), col(2), pl.BlockSpec((3, LANES), lambda j: (0, j))],
                  out_specs=pl.BlockSpec((S, LANES), lambda j: (0, j)),
                  out_shape=jax.ShapeDtypeStruct((S, 256), F32), compiler_params=_params(("parallel",)))(z, z, z, w)


def _conv_bwd(z, w, dc_all):
    S = z.shape[0]
    c0 = Z_CV // LANES

    def body(b_ref, c_ref, h_ref, w_ref, dy_ref, db_ref, dc_ref, dh_ref, dw_ref):
        cv, hv, bv, dyv = c_ref[...], h_ref[...], b_ref[...], dy_ref[...]
        u = cv * hv
        row = lax.broadcasted_iota(jnp.int32, u.shape, 0)
        wv = w_ref[...]
        u1, u2 = _shift_down(u, 1, row), _shift_down(u, 2, row)
        conv = wv[0:1] * u2 + wv[1:2] * u1 + wv[2:3] * u
        db_ref[...] = dyv * conv
        dconv = dyv * bv
        du = wv[2:3] * dconv + wv[1:2] * _shift_up(dconv, 1, row) + wv[0:1] * _shift_up(dconv, 2, row)
        dc_ref[...] = du * hv
        dh_ref[...] = du * cv
        dw_ref[0:1, :] = jnp.sum(dconv * u2, axis=0, keepdims=True)
        dw_ref[1:2, :] = jnp.sum(dconv * u1, axis=0, keepdims=True)
        dw_ref[2:3, :] = jnp.sum(dconv * u, axis=0, keepdims=True)

    def col(k):
        return pl.BlockSpec((S, LANES), lambda j: (0, c0 + 2 * k + j))

    wsp = pl.BlockSpec((3, LANES), lambda j: (0, j))
    osp = pl.BlockSpec((S, LANES), lambda j: (0, j))
    db, dc, dh, dw = _pcall(
        body, name="conv_bwd", grid=(2,),
        in_specs=[col(0), col(1), col(2), wsp, pl.BlockSpec((S, LANES), lambda j: (0, 6 + j))],
        out_specs=(osp, osp, osp, wsp),
        out_shape=(jax.ShapeDtypeStruct((S, 256), F32),) * 3 + (jax.ShapeDtypeStruct((3, 256), F32),),
        compiler_params=_params(("parallel",)))(z, z, z, w, dc_all)
    return db, dc, dh, dw


def _half_masks():
    lane = lax.broadcasted_iota(jnp.int32, (1, LANES), 1)
    return lane < HEAD_DIM, lane >= HEAD_DIM


def _pair(v, p):
    return v[:, p * LANES:(p + 1) * LANES]


def _keep(mask, v):
    return jnp.where(mask, v, jnp.zeros_like(v))


def _lane_col(vals):
    lane = lax.broadcasted_iota(jnp.int32, (1, LANES), 1)
    out = jnp.zeros((vals[0].shape[0], LANES), F32)
    for h, v in enumerate(vals):
        out = jnp.where(lane == h, v, out)
    return out


def _mla_fwd(q, k, kv):
    S = q.shape[0]
    H = MLA_HEADS
    bq, bk = _row_block(S, MLA_BQ), _row_block(S, MLA_BK)
    per = bq // bk

    def body(q_ref, k_ref, v_ref, o_ref, lse_ref):
        i = pl.program_id(0)
        lo, hi = _half_masks()
        qb = q_ref[...]

        def block(j, carry, ok):
            ms, ls, accs = carry
            off = pl.multiple_of(j * bk, bk)
            kb = k_ref[pl.ds(off, bk), :]
            vb = v_ref[pl.ds(off, bk), :].astype(BF16)
            ss = [_dot(_pair(qb, h), _pair(kb, h), NT) * MLA_SCALE for h in range(H)]
            if ok is not None:
                ss = [jnp.where(ok, s, NEG_INF) for s in ss]
            ms2 = [jnp.maximum(ms[h], jnp.max(ss[h], axis=-1, keepdims=True)) for h in range(H)]
            ps = [jnp.exp(ss[h] - ms2[h]) for h in range(H)]
            al = [jnp.exp(ms[h] - ms2[h]) for h in range(H)]
            ls2 = [al[h] * ls[h] + jnp.sum(ps[h], axis=-1, keepdims=True) for h in range(H)]
            pvs = [_dot(ps[h], _pair(vb, h // 2)) for h in range(H)]
            accs2 = []
            for p in range(H // 2):
                scale = jnp.where(lo, al[2 * p], al[2 * p + 1])
                accs2.append(scale * accs[p] + jnp.where(lo, pvs[2 * p], pvs[2 * p + 1]))
            return tuple(ms2), tuple(ls2), tuple(accs2)

        init = (tuple(jnp.full((bq, 1), NEG_INF, F32) for _ in range(H)),
                tuple(jnp.zeros((bq, 1), F32) for _ in range(H)),
                tuple(jnp.zeros((bq, LANES), F32) for _ in range(H // 2)))
        carry = lax.fori_loop(0, i * per, lambda j, c: block(j, c, None), init)
        row = lax.broadcasted_iota(jnp.int32, (bq, bk), 0)
        col = lax.broadcasted_iota(jnp.int32, (bq, bk), 1)
        for t in range(per):
            carry = block(i * per + t, carry, col + t * bk <= row)
        ms, ls, accs = carry
        o_ref[...] = jnp.concatenate(
            [accs[p] / jnp.where(lo, ls[2 * p], ls[2 * p + 1]) for p in range(H // 2)], axis=1)
        lse_ref[...] = _lane_col([ms[h] + jnp.log(ls[h]) for h in range(H)])

    return _pcall(body, name="mla_fwd", grid=(S // bq,),
                  in_specs=[pl.BlockSpec((bq, H * LANES), lambda i: (i, 0)),
                            pl.BlockSpec((S, H * LANES), lambda i: (0, 0)),
                            pl.BlockSpec((S, H * HEAD_DIM), lambda i: (0, 2))],
                  out_specs=(pl.BlockSpec((bq, H * HEAD_DIM), lambda i: (i, 0)),
                             pl.BlockSpec((bq, LANES), lambda i: (i, 0))),
                  out_shape=(jax.ShapeDtypeStruct((S, H * HEAD_DIM), F32), jax.ShapeDtypeStruct((S, LANES), F32)),
                  compiler_params=_params(("parallel",)))(q, k, kv)


def _mla_bwd(q, k, kv, o, lse, dc_all):
    S = q.shape[0]
    H = MLA_HEADS
    bq, bk = _row_block(S, MLA_BQ), _row_block(S, MLA_BK)
    per = bq // bk

    def body(q_ref, k_ref, v_ref, o_ref, lse_ref, doa_ref, dob_ref, dq_ref, dk_ref, dv_ref):
        i = pl.program_id(0)

        @pl.when(i == 0)
        def _():
            dk_ref[...] = jnp.zeros_like(dk_ref)
            dv_ref[...] = jnp.zeros_like(dv_ref)

        lo, hi = _half_masks()
        qb = q_ref[...]
        dob = jnp.concatenate([doa_ref[...], dob_ref[...]], axis=1)
        prod = dob * o_ref[...]
        lse_v = lse_ref[...]
        do16 = dob.astype(BF16)
        dom, deltas, lses = [], [], []
        for h in range(H):
            mk = lo if h % 2 == 0 else hi
            dom.append(_keep(mk, _pair(do16, h // 2)))
            deltas.append(jnp.sum(_keep(mk, _pair(prod, h // 2)), axis=-1, keepdims=True))
            lses.append(lse_v[:, h:h + 1])

        def block(j, dqs, ok):
            off = pl.multiple_of(j * bk, bk)
            kb = k_ref[pl.ds(off, bk), :]
            vb = v_ref[pl.ds(off, bk), :].astype(BF16)
            ss = [_dot(_pair(qb, h), _pair(kb, h), NT) * MLA_SCALE for h in range(H)]
            dps = [_dot(dom[h], _pair(vb, h // 2), NT) for h in range(H)]
            ps = [jnp.exp(ss[h] - lses[h]) for h in range(H)]
            if ok is not None:
                ps = [jnp.where(ok, p, 0.0) for p in ps]
            ds16 = [(ps[h] * (dps[h] - deltas[h]) * MLA_SCALE).astype(BF16) for h in range(H)]
            p16 = [p.astype(BF16) for p in ps]
            dks = [_dot(ds16[h], _pair(qb, h), TN) for h in range(H)]
            dvps = [_dot(p16[h], dom[h], TN) for h in range(H)]
            dqs2 = [dqs[h] + _dot(ds16[h], _pair(kb, h)) for h in range(H)]
            dk_ref[pl.ds(off, bk), :] += jnp.concatenate(dks, axis=1)
            dv_ref[pl.ds(off, bk), :] += jnp.concatenate(
                [dvps[2 * p] + dvps[2 * p + 1] for p in range(H // 2)], axis=1)
            return tuple(dqs2)

        dqs = lax.fori_loop(0, i * per, lambda j, c: block(j, c, None),
                            tuple(jnp.zeros((bq, LANES), F32) for _ in range(H)))
        row = lax.broadcasted_iota(jnp.int32, (bq, bk), 0)
        col = lax.broadcasted_iota(jnp.int32, (bq, bk), 1)
        for t in range(per):
            dqs = block(i * per + t, dqs, col + t * bk <= row)
        dq_ref[...] = jnp.concatenate(dqs, axis=1)

    wide = pl.BlockSpec((bq, H * LANES), lambda i: (i, 0))
    full = pl.BlockSpec((S, H * LANES), lambda i: (0, 0))
    return _pcall(body, name="mla_bwd", grid=(S // bq,),
                  in_specs=[wide, full, pl.BlockSpec((S, H * HEAD_DIM), lambda i: (0, 2)),
                            pl.BlockSpec((bq, H * HEAD_DIM), lambda i: (i, 0)),
                            pl.BlockSpec((bq, LANES), lambda i: (i, 0)),
                            pl.BlockSpec((bq, 256), lambda i: (i, 1)), pl.BlockSpec((bq, 256), lambda i: (i, 2))],
                  out_specs=(wide, full, pl.BlockSpec((S, H * HEAD_DIM), lambda i: (0, 0))),
                  out_shape=(jax.ShapeDtypeStruct((S, H * LANES), F32), jax.ShapeDtypeStruct((S, H * LANES), F32),
                             jax.ShapeDtypeStruct((S, H * HEAD_DIM), F32)),
                  compiler_params=_params(("arbitrary",)))(q, k, kv, o, lse, dc_all, dc_all)


def _dot_exact(x, tri):
    h1 = x.astype(BF16)
    h2 = (x - h1.astype(F32)).astype(BF16)
    return _dot(h1, tri) + _dot(h2, tri)


def _softplus(z):
    return jnp.maximum(z, 0.0) + jnp.log(1.0 + jnp.exp(-jnp.abs(z)))


def _sb_fwd(z):
    S = z.shape[0]
    H = SB_HEADS
    bq, bk = _row_block(S, SB_BQ), _row_block(S, SB_BK)
    per = bq // bk

    def body(q_ref, k_ref, v_ref, o_ref, t_ref):
        i = pl.program_id(0)
        lo, hi = _half_masks()
        q16 = (q_ref[...] * SB_SCALE).astype(BF16)
        qm = [_keep(lo if h % 2 == 0 else hi, _pair(q16, h // 2)) for h in range(H)]
        rr = lax.broadcasted_iota(jnp.int32, (bk, bk), 0)
        cc = lax.broadcasted_iota(jnp.int32, (bk, bk), 1)
        later = (rr > cc).astype(BF16)

        def block(j, carry, mask):
            tails, accs = carry
            off = pl.multiple_of(j * bk, bk)
            kb = k_ref[pl.ds(off, bk), :].astype(BF16)
            vb = v_ref[pl.ds(off, bk), :].astype(BF16)
            zs = [_dot(qm[h], _pair(kb, h // 2), NT) for h in range(H)]
            sps = [_softplus(z) for z in zs]
            lnots = [-sp if mask is None else jnp.where(mask, -sp, 0.0) for sp in sps]
            sums = [_dot_exact(lnot, later) for lnot in lnots]
            ws = []
            for h in range(H):
                w = jnp.exp((zs[h] - sps[h]) + (tails[h] + sums[h]))
                ws.append((w if mask is None else jnp.where(mask, w, 0.0)).astype(BF16))
            pvs = [_dot(ws[h], _pair(vb, h // 2)) for h in range(H)]
            accs2 = [accs[p] + jnp.where(lo, pvs[2 * p], pvs[2 * p + 1]) for p in range(H // 2)]
            tails2 = [tails[h] + jnp.sum(lnots[h], axis=-1, keepdims=True) for h in range(H)]
            return tuple(tails2), tuple(accs2)

        carry = (tuple(jnp.zeros((bq, 1), F32) for _ in range(H)),
                 tuple(jnp.zeros((bq, LANES), F32) for _ in range(H // 2)))
        row = lax.broadcasted_iota(jnp.int32, (bq, bk), 0)
        col = lax.broadcasted_iota(jnp.int32, (bq, bk), 1)
        for t in range(per):
            carry = block(i * per + per - 1 - t, carry, col + (per - 1 - t) * bk < row)
        tails, accs = lax.fori_loop(0, i * per, lambda t, c: block(i * per - 1 - t, c, None), carry)
        o_ref[...] = jnp.concatenate(accs, axis=1)
        t_ref[...] = _lane_col(tails)

    return _pcall(body, name="sb_fwd", grid=(S // bq,),
                  in_specs=[pl.BlockSpec((bq, 256), lambda i: (i, 0)), pl.BlockSpec((S, 256), lambda i: (0, 1)),
                            pl.BlockSpec((S, 256), lambda i: (0, 2))],
                  out_specs=(pl.BlockSpec((bq, 256), lambda i: (i, 0)), pl.BlockSpec((bq, LANES), lambda i: (i, 0))),
                  out_shape=(jax.ShapeDtypeStruct((S, 256), F32), jax.ShapeDtypeStruct((S, LANES), F32)),
                  compiler_params=_params(("parallel",)))(z, z, z)


def _sb_bwd(z, tot, dc_all):
    S = z.shape[0]
    H = SB_HEADS
    bq, bk = _row_block(S, SB_BQ), _row_block(S, SB_BK)
    per = bq // bk

    def body(q_ref, k_ref, v_ref, t_ref, do_ref, dq_ref, dk_ref, dv_ref):
        i = pl.program_id(0)

        @pl.when(i == 0)
        def _():
            dk_ref[...] = jnp.zeros_like(dk_ref)
            dv_ref[...] = jnp.zeros_like(dv_ref)

        lo, hi = _half_masks()
        q16 = (q_ref[...] * SB_SCALE).astype(BF16)
        do16 = do_ref[...].astype(BF16)
        tot_v = t_ref[...]
        masks = [lo if h % 2 == 0 else hi for h in range(H)]
        qm = [_keep(masks[h], _pair(q16, h // 2)) for h in range(H)]
        dom = [_keep(masks[h], _pair(do16, h // 2)) for h in range(H)]
        tots = [tot_v[:, h:h + 1] for h in range(H)]
        rr = lax.broadcasted_iota(jnp.int32, (bk, bk), 0)
        cc = lax.broadcasted_iota(jnp.int32, (bk, bk), 1)
        upto = (rr <= cc).astype(BF16)
        before = (rr < cc).astype(BF16)

        def block(j, carry, mask):
            heads, eheads, dqs = carry
            off = pl.multiple_of(j * bk, bk)
            kb = k_ref[pl.ds(off, bk), :].astype(BF16)
            vb = v_ref[pl.ds(off, bk), :].astype(BF16)
            zs = [_dot(qm[h], _pair(kb, h // 2), NT) for h in range(H)]
            dws = [_dot(dom[h], _pair(vb, h // 2), NT) for h in range(H)]
            sps = [_softplus(z) for z in zs]
            lnots = [-sp if mask is None else jnp.where(mask, -sp, 0.0) for sp in sps]
            pres = [_dot_exact(lnot, upto) for lnot in lnots]
            lsigs = [zs[h] - sps[h] for h in range(H)]
            ws = [jnp.exp(lsigs[h] + (tots[h] - (heads[h] + pres[h]))) for h in range(H)]
            if mask is not None:
                ws = [jnp.where(mask, w, 0.0) for w in ws]
            es = [ws[h] * dws[h] for h in range(H)]
            esums = [eheads[h] + _dot_exact(es[h], before) for h in range(H)]
            dz16 = []
            for h in range(H):
                sig = jnp.exp(lsigs[h])
                dz = es[h] * (1.0 - sig) - sig * esums[h]
                dz16.append((dz if mask is None else jnp.where(mask, dz, 0.0)).astype(BF16))
            w16 = [w.astype(BF16) for w in ws]
            dkps = [_dot(dz16[h], qm[h], TN) for h in range(H)]
            dvps = [_dot(w16[h], dom[h], TN) for h in range(H)]
            dqps = [_dot(dz16[h], _pair(kb, h // 2)) for h in range(H)]
            dk_ref[pl.ds(off, bk), :] += jnp.concatenate([dkps[2 * p] + dkps[2 * p + 1] for p in range(H // 2)], axis=1)
            dv_ref[pl.ds(off, bk), :] += jnp.concatenate([dvps[2 * p] + dvps[2 * p + 1] for p in range(H // 2)], axis=1)
            dqs2 = [dqs[p] + jnp.where(lo, dqps[2 * p], dqps[2 * p + 1]) for p in range(H // 2)]
            heads2 = [heads[h] + jnp.sum(lnots[h], axis=-1, keepdims=True) for h in range(H)]
            eheads2 = [eheads[h] + jnp.sum(es[h], axis=-1, keepdims=True) for h in range(H)]
            return tuple(heads2), tuple(eheads2), tuple(dqs2)

        zeros = tuple(jnp.zeros((bq, 1), F32) for _ in range(H))
        init = (zeros, zeros, tuple(jnp.zeros((bq, LANES), F32) for _ in range(H // 2)))
        carry = lax.fori_loop(0, i * per, lambda j, c: block(j, c, None), init)
        row = lax.broadcasted_iota(jnp.int32, (bq, bk), 0)
        col = lax.broadcasted_iota(jnp.int32, (bq, bk), 1)
        for t in range(per):
            carry = block(i * per + t, carry, col + t * bk < row)
        dq_ref[...] = jnp.concatenate(carry[2], axis=1) * SB_SCALE

    blk = pl.BlockSpec((bq, 256), lambda i: (i, 0))
    full = pl.BlockSpec((S, 256), lambda i: (0, 0))
    return _pcall(body, name="sb_bwd", grid=(S // bq,),
                  in_specs=[blk, pl.BlockSpec((S, 256), lambda i: (0, 1)), pl.BlockSpec((S, 256), lambda i: (0, 2)),
                            pl.BlockSpec((bq, LANES), lambda i: (i, 0)), blk],
                  out_specs=(blk, full, full), out_shape=(jax.ShapeDtypeStruct((S, 256), F32),) * 3,
                  compiler_params=_params(("arbitrary",)))(z, z, z, tot, dc_all)


_HBM = pl.BlockSpec(memory_space=pltpu.HBM)
_SEM = pl.BlockSpec(memory_space=pltpu.SEMAPHORE)
_ANY = pl.BlockSpec(memory_space=pl.ANY)


def _place():
    x, y, c = lax.axis_index("x"), lax.axis_index("y"), lax.axis_index("c")
    other_chips = ((1 - x, y), (x, 1 - y), (1 - x, 1 - y))
    return x, y, c, other_chips


def _remote(src, dst, send_sems, recv_sems, k, dev):
    return pltpu.make_async_remote_copy(src_ref=src, dst_ref=dst, send_sem=send_sems.at[k], recv_sem=recv_sems.at[k],
                                        device_id=dev, device_id_type=MESH)


def _half_rows(c, n_rows):
    half = n_rows // 2
    return pl.ds(pl.multiple_of(c * half, 16), half)


def _gather_copies(src, land, send_sems, recv_sems):
    x, y, c, chips = _place()
    me = 2 * x + y
    cps = []
    for w in range(len(src)):
        rows = _half_rows(c, src[w].shape[0])
        for k, (cx, cy) in enumerate(chips):
            cps.append(_remote(src[w].at[rows], land[w].at[me, rows], send_sems, recv_sems, 3 * w + k, (cx, cy, c)))
    return cps


def _scatter_copies(src, land, send_sems, recv_sems):
    x, y, c, chips = _place()
    cps = []
    for w in range(len(src)):
        for k, (cx, cy) in enumerate(chips):
            cps.append(_remote(src[w].at[2 * cx + cy], land[w].at[k], send_sems, recv_sems, 3 * w + k, (cx, cy, c)))
    return cps


def _exchange_start(copies, srcs, lands, after, name):
    n = len(srcs)

    def body(*refs):
        src, land = refs[:n], refs[n:2 * n]
        send_sems, recv_sems = refs[2 * n + 1], refs[2 * n + 2]
        token = refs[-1]
        for cp in copies(src, land, send_sems, recv_sems):
            cp.start()
        token[...] = jnp.zeros_like(token)

    thru = tuple(pltpu.HBM(a.shape, a.dtype) for a in list(srcs) + list(lands))
    out = _pcall(body, name=name,
                 out_shape=(pltpu.SemaphoreType.DMA((3 * n,)), pltpu.SemaphoreType.DMA((3 * n,))) + thru
                 + (jax.ShapeDtypeStruct((8, LANES), F32),),
                 in_specs=[_HBM] * (2 * n) + [_ANY],
                 out_specs=(_SEM, _SEM) + (_HBM,) * (2 * n) + (pl.BlockSpec(memory_space=pltpu.VMEM),),
                 input_output_aliases={i: 2 + i for i in range(2 * n)},
                 compiler_params=pltpu.CompilerParams(has_side_effects=pltpu.SideEffectType.DATAFLOW_SIDE_EFFECTING))(
        *[pltpu.with_memory_space_constraint(a, pltpu.HBM) for a in list(srcs) + list(lands)], after)
    return out[0], out[1], list(out[2:2 + n]), list(out[2 + n:2 + 2 * n]), out[-1]


def _exchange_wait(copies, send_sems, recv_sems, srcs, lands, after, name):
    n = len(srcs)

    def body(*refs):
        src, land = refs[:n], refs[n:2 * n]
        ssem, rsem = refs[2 * n], refs[2 * n + 1]
        for cp in copies(src, land, ssem, rsem):
            cp.wait_send()
            cp.wait_recv()

    out = _pcall(body, name=name, out_shape=tuple(pltpu.HBM(a.shape, a.dtype) for a in list(srcs) + list(lands)),
                 in_specs=[_HBM] * (2 * n) + [_SEM, _SEM, _ANY], out_specs=(_HBM,) * (2 * n),
                 input_output_aliases={i: i for i in range(2 * n)},
                 compiler_params=pltpu.CompilerParams(has_side_effects=pltpu.SideEffectType.DATAFLOW_SIDE_EFFECTING))(
        *srcs, *lands, send_sems, recv_sems, after)
    return list(out[:n]), list(out[n:])


def _forward_rows(gathered):
    n = len(gathered)

    def body(*refs):
        src, dst = refs[:n], refs[n:2 * n]
        send_sems, recv_sems = refs[2 * n:]
        x, y, c, chips = _place()
        sib = (x, y, 1 - c)
        cps = []
        for w in range(n):
            K = src[w].shape[1]
            for k, (cx, cy) in enumerate(chips):
                blk = (2 * cx + cy, _half_rows(c, K))
                cps.append(_remote(src[w].at[blk], dst[w].at[blk], send_sems, recv_sems, 3 * w + k, sib))
        for cp in cps:
            cp.start()
        for w in range(n):
            K = src[w].shape[1]
            for k, (cx, cy) in enumerate(chips):
                blk = dst[w].at[2 * cx + cy, _half_rows(1 - c, K)]
                _remote(blk, blk, send_sems, recv_sems, 3 * w + k, sib).wait_recv()
        for cp in cps:
            cp.wait_send()

    return _pcall(body, name="forward_rows", in_specs=[_HBM] * n, out_specs=(_HBM,) * n,
                  out_shape=tuple(jax.ShapeDtypeStruct(g.shape, g.dtype) for g in gathered),
                  input_output_aliases={w: w for w in range(n)},
                  scratch_shapes=[pltpu.SemaphoreType.DMA((3 * n,)), pltpu.SemaphoreType.DMA((3 * n,))])(*gathered)


def _swap_rows(bufs):
    n = len(bufs)

    def body(*refs):
        src, dst = refs[:n], refs[n:2 * n]
        send_sems, recv_sems = refs[2 * n:]
        x, y, c, _ = _place()
        cps = [_remote(src[w].at[:, _half_rows(1 - c, src[w].shape[1])], dst[w], send_sems, recv_sems, w,
                       (x, y, 1 - c)) for w in range(n)]
        for cp in cps:
            cp.start()
        for cp in cps:
            cp.wait()

    return _pcall(body, name="swap_rows", in_specs=[_HBM] * n, out_specs=(_HBM,) * n,
                  out_shape=tuple(jax.ShapeDtypeStruct((b.shape[0], b.shape[1] // 2, b.shape[2]), b.dtype) for b in bufs),
                  scratch_shapes=[pltpu.SemaphoreType.DMA((n,)), pltpu.SemaphoreType.DMA((n,))])(*bufs)


def _join_rows(fins):
    n = len(fins)

    def body(*refs):
        src, dst = refs[:n], refs[n:2 * n]
        send_sems, recv_sems = refs[2 * n:]
        x, y, c, _ = _place()
        sib = (x, y, 1 - c)
        cps = []
        for w in range(n):
            blk = (slice(None), _half_rows(c, src[w].shape[1]))
            cps.append(_remote(src[w].at[blk], dst[w].at[blk], send_sems, recv_sems, w, sib))
        for cp in cps:
            cp.start()
        for w in range(n):
            blk = dst[w].at[:, _half_rows(1 - c, src[w].shape[1])]
            _remote(blk, blk, send_sems, recv_sems, w, sib).wait_recv()
        for cp in cps:
            cp.wait_send()

    return _pcall(body, name="join_rows", in_specs=[_HBM] * n, out_specs=(_HBM,) * n,
                  out_shape=tuple(jax.ShapeDtypeStruct(f.shape, f.dtype) for f in fins),
                  input_output_aliases={w: w for w in range(n)},
                  scratch_shapes=[pltpu.SemaphoreType.DMA((n,)), pltpu.SemaphoreType.DMA((n,))])(*fins)


def _add_rows(buf, got, c_idx):
    _, Kh, N = got.shape
    bk = _row_block(Kh, 256)
    nb = Kh // bk

    def body(c_ref, a_ref, b_ref, o_ref):
        o_ref[...] = (a_ref[...].astype(F32) + b_ref[...].astype(F32)).astype(o_ref.dtype)

    blk = pl.BlockSpec((None, bk, N), lambda j, i, c_ref: (j, i, 0))
    grid_spec = pltpu.PrefetchScalarGridSpec(
        num_scalar_prefetch=1, grid=(N_CHIPS, nb),
        in_specs=[pl.BlockSpec((None, bk, N), lambda j, i, c_ref: (j, c_ref[0] * nb + i, 0)), blk], out_specs=blk)
    return _pcall(body, name="add_rows", grid_spec=grid_spec, out_shape=jax.ShapeDtypeStruct((N_CHIPS, Kh, N), BF16),
                  compiler_params=_params(("parallel", "parallel")))(c_idx, buf, got)


def _add_chip_sums(s1, got, idx, layer, into, after):
    _, Kh, N = s1.shape
    bk = _row_block(Kh, 256)
    nb = Kh // bk

    def body(idx_ref, a_ref, b_ref, *rest):
        acc = a_ref[...].astype(F32)
        for k in range(3):
            acc = acc + b_ref[k].astype(F32)
        rest[-1][...] = acc

    in_specs = [pl.BlockSpec((None, bk, N), lambda i, idx_ref: (idx_ref[0], i, 0)),
                pl.BlockSpec((3, bk, N), lambda i, idx_ref: (0, i, 0))]
    in_specs.append(_ANY)
    ops = [idx, s1, got, after]
    if into is not None:
        in_specs.append(_ANY)
        ops.append(into)
    grid_spec = pltpu.PrefetchScalarGridSpec(
        num_scalar_prefetch=1, grid=(nb,), in_specs=in_specs,
        out_specs=pl.BlockSpec((None, bk, N), lambda i, idx_ref: (layer, idx_ref[1] * nb + i, 0)))
    return _pcall(body, name="add_chip_sums" if into is None else "add_chip_sums_into", grid_spec=grid_spec,
                  out_shape=jax.ShapeDtypeStruct((2, 2 * Kh, N), F32),
                  input_output_aliases={} if into is None else {4: 0},
                  compiler_params=_params(("parallel",)))(*ops)


def _allreduce_small(v, after):
    R, W = v.shape

    def body(v_ref, after_ref, o_ref, buf, send_sems, recv_sems):
        x, y, c, _ = _place()
        me = 4 * x + 2 * y + c
        buf[0] = v_ref[...]
        cps = []
        for r in range(1, 8):
            peer = (x if not r & 4 else 1 - x, y if not r & 2 else 1 - y, c if not r & 1 else 1 - c)
            cp = _remote(v_ref, buf.at[r], send_sems, recv_sems, r - 1, peer)
            cp.start()
            cps.append(cp)
        for cp in cps:
            cp.wait()
        acc = buf[me]
        for d in range(1, 8):
            acc = acc + buf[jnp.bitwise_xor(me, d)]
        o_ref[...] = acc

    return _pcall(body, name="allreduce_small", out_shape=jax.ShapeDtypeStruct((R, W), F32),
                  in_specs=[pl.BlockSpec(memory_space=pltpu.VMEM), _ANY],
                  out_specs=pl.BlockSpec(memory_space=pltpu.VMEM),
                  scratch_shapes=[pltpu.VMEM((8, R, W), F32), pltpu.SemaphoreType.DMA((7,)),
                                  pltpu.SemaphoreType.DMA((7,))])(v, after)


def _cat_cols(g):
    return g.transpose(1, 0, 2).reshape(g.shape[1], -1)


def _cut_cols(w):
    K, N = w.shape
    return w.reshape(K, N_CHIPS, N // N_CHIPS).transpose(1, 0, 2)


def _regroup_w_in_t(g):
    w = g[:, :552].reshape(2208, g.shape[2])
    zeros = lambda n: jnp.zeros((n, w.shape[1]), w.dtype)
    return jnp.concatenate([w[0:768], w[1152:1408], zeros(64), w[1408:1440], zeros(32), w[768:1152], w[1440:2208]],
                           axis=0)


def _ungroup_w_in_t(w):
    nat = jnp.concatenate([w[0:768], w[Z_Q:Z_Q + 384], w[Z_KV:Z_KV + 256], w[Z_KR + 64:Z_KR + 96], w[Z_CV:Z_W]],
                          axis=0)
    return jnp.pad(nat.reshape(N_CHIPS, 552, w.shape[1]), ((0, 0), (0, 24), (0, 0)))


def _regroup_w_uq_t(g):
    K = g.shape[2]
    return jnp.pad(g.reshape(MLA_HEADS, 96, K), ((0, 0), (0, 32), (0, 0))).reshape(MLA_HEADS * LANES, K)


def _ungroup_w_uq_t(w):
    K = w.shape[1]
    return w.reshape(MLA_HEADS, LANES, K)[:, :96].reshape(N_CHIPS, 192, K)


def _regroup_w_ukv(w):
    K = w.shape[0]
    t = w.reshape(K, MLA_HEADS, 128)
    return jnp.concatenate([jnp.pad(t[:, :, :64], ((0, 0), (0, 0), (0, 64))).reshape(K, MLA_HEADS * LANES),
                            t[:, :, 64:].reshape(K, MLA_HEADS * HEAD_DIM)], axis=1)


def _ungroup_w_ukv(w):
    K = w.shape[0]
    return jnp.concatenate([w[:, :MLA_HEADS * LANES].reshape(K, MLA_HEADS, LANES)[:, :, :64],
                            w[:, MLA_HEADS * LANES:].reshape(K, MLA_HEADS, 64)], axis=2).reshape(K, 1024)


def _ffn_fwd(x, g_pre, g_post, fetch, tag):
    W = fetch("ffn%sa" % tag, x)
    h = _norm_fwd(x, g_pre, "norm_fwd")
    G, U, A = _ffn_gate_up(h, W["w%s_gate" % tag], W["w%s_up" % tag], "ffn_gate_up")
    W = fetch("ffn%sb" % tag, A)
    Y = _mm([(A, W["w%s_down" % tag])], "nn", F32, "ffn_down", bm=1024)
    return _resid_norm(x, Y, g_post, 0.5, "resid_norm"), dict(x=x, h=h, G=G, U=U, A=A, Y=Y)


def _ffn_bwd(dxo, sv, g_pre, g_post, wg, wu, wd_rows, ready):
    dY, dg_post = _norm_bwd(sv["Y"], g_post, dxo, 0.5, None, BF16, "norm_bwd_post")
    dG, dU = _ffn_bwd_mid(dY, wd_rows, sv["G"], sv["U"], "ffn_bwd_mid")
    cut = lambda t: t.reshape(N_CHIPS, -1, t.shape[1])
    token = ready((cut(_mm([(dG, sv["h"])], "tn", BF16, "ffn_dw_in")), cut(_mm([(dU, sv["h"])], "tn", BF16, "ffn_dw_in")),
                   cut(_mm([(sv["A"], dY)], "tn", BF16, "ffn_dw_down"))))
    dh = _mm([(dG, wg), (dU, wu)], "nn", F32, "ffn_dh")
    dx, dg_pre = _norm_bwd(sv["x"], _ordered_after(g_pre, token), dh, 1.0, dxo, F32, "norm_bwd_pre")
    return dx, dg_pre, dg_post, token


def _ordered_after(g, token):
    return g if token is None else g + token[0, 0]


GATHER_GROUPS = (("ffn1a", ("w1_gate", "w1_up")), ("ffn1b", ("w1_down",)),
                 ("mix", ("w_in", "w_mla_uq", "w_mla_ukv", "w_out")),
                 ("ffn2a", ("w2_gate", "w2_up", "w2_down", "w_ple_gate", "w_ple_proj")))
SCATTER_GROUPS = (("ffn2", ("w2_gate", "w2_up", "w2_down", "w_ple_gate", "w_ple_proj")),
                  ("mix", ("w_in", "w_mla_uq", "w_mla_ukv", "w_out")),
                  ("ffn1", ("w1_gate", "w1_up", "w1_down")))


def _layer_fwd(x, p_l, fetch, G, w_conv, cp, sp):
    sv = {}
    x1, sv["ffn1"] = _ffn_fwd(x, G["g_ffn1_pre"], G["g_ffn1_post"], fetch, "1")

    W = fetch("mix", x1)
    h2 = _norm_fwd(x1, G["g_mix_pre"], "norm_fwd")
    Z = _mm([(h2, W["w_in"])], "nt", F32, "mix_in")
    o_sb, tot_sb = _sb_fwd(Z)
    cq, ckv = Z[:, Z_Q:Z_Q + 384], Z[:, Z_KV:Z_KV + 256]
    nq = _norm_fwd(cq, G["g_mla_q"], "norm_fwd_q")
    Qf = _mm([(nq, W["w_mla_uq"])], "nt", F32, "mla_uq")
    nkv = _norm_fwd(ckv, G["g_mla_kv"], "norm_fwd_kv")
    KV = _mm([(nkv, W["w_mla_ukv"])], "nn", F32, "mla_ukv")
    qcat, kcat = _mla_prep(Qf, KV, Z, cp, sp)
    o_mla, lse = _mla_fwd(qcat, kcat, KV)
    y_cv = _conv_fwd(Z, w_conv)
    C = jnp.concatenate([o_sb, o_mla, y_cv], axis=1).astype(BF16)
    Mx = _mm([(C, W["w_out"])], "nn", F32, "mix_out")
    x2 = _resid_norm(x1, Mx, G["g_mix_post"], 1.0, "resid_norm")
    sv["mix"] = dict(x=x1, h=h2, Z=Z, tot_sb=tot_sb, cq=cq, ckv=ckv, nq=nq, nkv=nkv, qcat=qcat, kcat=kcat, KV=KV,
                     o_mla=o_mla, lse=lse, C=C, Mx=Mx)

    x3, sv["ffn2"] = _ffn_fwd(x2, G["g_ffn2_pre"], G["g_ffn2_post"], fetch, "2")
    W = fetch("ffn2b", x3)

    h4 = _norm_fwd(x3, G["g_ple_pre"], "norm_fwd")
    p16 = p_l.astype(BF16)
    Qg = _mm([(h4, W["w_ple_gate"])], "nn", F32, "ple_gate")
    Pp = _ple_proj(p16, W["w_ple_proj"])
    e = _ew(lambda q, pp: _sigmoid(q) * pp, [Qg, Pp], [(D_MODEL, F32)], "ple_mul")
    x4 = _resid_norm(x3, e, G["g_ple_post"], 1.0, "resid_norm")
    sv["ple"] = dict(x=x3, h=h4, p16=p16, Qg=Qg, Pp=Pp, e=e)
    sv["W"] = W
    return x4, sv


def _layer_bwd(dx4, sv, G, w_conv, cp, sp, emit, token):
    gg, gw, W = {}, {}, sv["W"]

    s = sv["ple"]
    de, gg["g_ple_post"] = _norm_bwd(s["e"], _ordered_after(G["g_ple_post"], token), dx4, 1.0, None, F32,
                                     "norm_bwd_e")

    def ple_bwd(dev, q, pp):
        sg = _sigmoid(q)
        return dev * pp * sg * (1.0 - sg), dev * sg

    dQg, dPp = _ew(ple_bwd, [de, s["Qg"], s["Pp"]], [(D_MODEL, BF16)] * 2, "ple_mul_bwd")
    cut = lambda t: t.reshape(N_CHIPS, -1, t.shape[1])
    gw["w_ple_proj"] = _dw_col_shards(s["p16"], dPp, "ple_dw_proj")
    gw["w_ple_gate"] = cut(_mm([(s["h"], dQg)], "tn", BF16, "ple_dw_gate"))
    dh4 = _mm([(dQg, W["w_ple_gate"])], "nt", F32, "ple_dh")
    dx3, gg["g_ple_pre"] = _norm_bwd(s["x"], G["g_ple_pre"], dh4, 1.0, dx4, F32, "norm_bwd_pre")

    def ready2(dws):
        gw["w2_gate"], gw["w2_up"], gw["w2_down"] = dws
        return emit("ffn2", gw)

    dx2, gg["g_ffn2_pre"], gg["g_ffn2_post"], token = _ffn_bwd(
        dx3, sv["ffn2"], G["g_ffn2_pre"], G["g_ffn2_post"], W["w2_gate"], W["w2_up"], W["w2_down"], ready2)

    s = sv["mix"]
    dM, gg["g_mix_post"] = _norm_bwd(s["Mx"], _ordered_after(G["g_mix_post"], token), dx2, 1.0, None, BF16,
                                     "norm_bwd_post")
    dC = _mm([(dM, W["w_out"])], "nt", F32, "mix_out_dx")
    gw["w_out"] = cut(_mm([(s["C"], dM)], "tn", BF16, "mix_out_dw"))

    db, dc, dhh, gg["w_conv"] = _conv_bwd(s["Z"], w_conv, dC)

    dqc, dkc, dv = _mla_bwd(s["qcat"], s["kcat"], s["KV"], s["o_mla"], s["lse"], dC)
    dQf, dkr = _mla_prep_bwd(dqc, dkc, cp, sp)
    gw["w_mla_uq"] = _ungroup_w_uq_t(_mm([(dQf, s["nq"])], "tn", BF16, "mla_uq_dw"))
    dnq = _mm([(dQf, W["w_mla_uq"])], "nn", F32, "mla_uq_dx")
    dcq, gg["g_mla_q"] = _norm_bwd(s["cq"], G["g_mla_q"], dnq, 1.0, None, F32, "norm_bwd_q")
    dkv = jnp.concatenate([dkc, dv], axis=1).astype(BF16)
    gw["w_mla_ukv"] = _cut_cols(_ungroup_w_ukv(_mm([(s["nkv"], dkv)], "tn", BF16, "mla_ukv_dw")))
    dnkv = _mm([(dkv, W["w_mla_ukv"])], "nt", F32, "mla_ukv_dx")
    dckv, gg["g_mla_kv"] = _norm_bwd(s["ckv"], G["g_mla_kv"], dnkv, 1.0, None, F32, "norm_bwd_kv")

    dsq, dsk, dsv = _sb_bwd(s["Z"], s["tot_sb"], dC)
    dZ = jnp.concatenate([dsq, dsk, dsv, dckv, dkr, dcq, db, dc, dhh], axis=1).astype(BF16)
    gw["w_in"] = _ungroup_w_in_t(_mm([(dZ, s["h"])], "tn", BF16, "mix_in_dw"))
    dh2 = _mm([(dZ, W["w_in"])], "nn", F32, "mix_in_dx")
    dx1, gg["g_mix_pre"] = _norm_bwd(s["x"], G["g_mix_pre"], dh2, 1.0, dx2, F32, "norm_bwd_pre")
    started_mix = emit("mix", gw)
    token = token if started_mix is None else started_mix

    def ready1(dws):
        gw["w1_gate"], gw["w1_up"], gw["w1_down"] = dws
        return emit("ffn1", gw)

    dx0, gg["g_ffn1_pre"], gg["g_ffn1_post"], started_ffn1 = _ffn_bwd(
        dx1, sv["ffn1"], G["g_ffn1_pre"], _ordered_after(G["g_ffn1_post"], token), W["w1_gate"], W["w1_up"],
        W["w1_down"], ready1)
    return dx0, gg, token if started_ffn1 is None else started_ffn1


def _pack_small(vecs):
    flat = jnp.concatenate([v.reshape(-1) for v in vecs])
    rows = -(-flat.shape[0] // (8 * LANES)) * 8
    return jnp.pad(flat, (0, rows * LANES - flat.shape[0])).reshape(rows, LANES)


def _group_weights(names, landed, shards, chip):
    out = {}
    for name, g, own in zip(names, landed, shards):
        g = lax.dynamic_update_slice(g, own[None], (chip, 0, 0))
        if name == "w_in":
            g = _regroup_w_in_t(g)
        elif name == "w_mla_uq":
            g = _regroup_w_uq_t(g)
        elif name not in ("w_mla_ukv", "w_ple_proj"):
            g = g.reshape(-1, g.shape[2])
        elif name == "w_mla_ukv":
            g = _regroup_w_ukv(_cat_cols(g))
        out[name] = g
    return out


def kernel(x, p, positions, g_ffn1_pre, w1_gate, w1_up, w1_down, g_ffn1_post, g_mix_pre, w_in, g_mla_q, w_mla_uq, g_mla_kv, w_mla_ukv, w_conv, w_out, g_mix_post, g_ffn2_pre, w2_gate, w2_up, w2_down, g_ffn2_post, g_ple_pre, w_ple_gate, w_ple_proj, g_ple_post, loss_target, m_g_ffn1_pre, m_w1_gate, m_w1_up, m_w1_down, m_g_ffn1_post, m_g_mix_pre, m_w_in, m_g_mla_q, m_w_mla_uq, m_g_mla_kv, m_w_mla_ukv, m_w_conv, m_w_out, m_g_mix_post, m_g_ffn2_pre, m_w2_gate, m_w2_up, m_w2_down, m_g_ffn2_post, m_g_ple_pre, m_w_ple_gate, m_w_ple_proj, m_g_ple_post, v_g_ffn1_pre, v_w1_gate, v_w1_up, v_w1_down, v_g_ffn1_post, v_g_mix_pre, v_w_in, v_g_mla_q, v_w_mla_uq, v_g_mla_kv, v_w_mla_ukv, v_w_conv, v_w_out, v_g_mix_post, v_g_ffn2_pre, v_w2_gate, v_w2_up, v_w2_down, v_g_ffn2_post, v_g_ple_pre, v_w_ple_gate, v_w_ple_proj, v_g_ple_post):
    w = dict(g_ffn1_pre=g_ffn1_pre, w1_gate=w1_gate, w1_up=w1_up, w1_down=w1_down, g_ffn1_post=g_ffn1_post,
             g_mix_pre=g_mix_pre, w_in=w_in, g_mla_q=g_mla_q, w_mla_uq=w_mla_uq, g_mla_kv=g_mla_kv,
             w_mla_ukv=w_mla_ukv, w_conv=w_conv, w_out=w_out, g_mix_post=g_mix_post, g_ffn2_pre=g_ffn2_pre,
             w2_gate=w2_gate, w2_up=w2_up, w2_down=w2_down, g_ffn2_post=g_ffn2_post, g_ple_pre=g_ple_pre,
             w_ple_gate=w_ple_gate, w_ple_proj=w_ple_proj, g_ple_post=g_ple_post)
    m = dict(g_ffn1_pre=m_g_ffn1_pre, w1_gate=m_w1_gate, w1_up=m_w1_up, w1_down=m_w1_down, g_ffn1_post=m_g_ffn1_post,
             g_mix_pre=m_g_mix_pre, w_in=m_w_in, g_mla_q=m_g_mla_q, w_mla_uq=m_w_mla_uq, g_mla_kv=m_g_mla_kv,
             w_mla_ukv=m_w_mla_ukv, w_conv=m_w_conv, w_out=m_w_out, g_mix_post=m_g_mix_post, g_ffn2_pre=m_g_ffn2_pre,
             w2_gate=m_w2_gate, w2_up=m_w2_up, w2_down=m_w2_down, g_ffn2_post=m_g_ffn2_post, g_ple_pre=m_g_ple_pre,
             w_ple_gate=m_w_ple_gate, w_ple_proj=m_w_ple_proj, g_ple_post=m_g_ple_post)
    v = dict(g_ffn1_pre=v_g_ffn1_pre, w1_gate=v_w1_gate, w1_up=v_w1_up, w1_down=v_w1_down, g_ffn1_post=v_g_ffn1_post,
             g_mix_pre=v_g_mix_pre, w_in=v_w_in, g_mla_q=v_g_mla_q, w_mla_uq=v_w_mla_uq, g_mla_kv=v_g_mla_kv,
             w_mla_ukv=v_w_mla_ukv, w_conv=v_w_conv, w_out=v_w_out, g_mix_post=v_g_mix_post, g_ffn2_pre=v_g_ffn2_pre,
             w2_gate=v_w2_gate, w2_up=v_w2_up, w2_down=v_w2_down, g_ffn2_post=v_g_ffn2_post, g_ple_pre=v_g_ple_pre,
             w_ple_gate=v_w_ple_gate, w_ple_proj=v_w_ple_proj, g_ple_post=v_g_ple_post)

    for name in TRANSPOSED:
        w[name], m[name], v[name] = (jnp.swapaxes(t[name], 1, 2) for t in (w, m, v))

    depth = g_ffn1_pre.shape[0]
    assert depth == 2, "the reduced gradients are assembled in [2, K, N] buffers"
    S = x.shape[1]
    cx, cy, cc = lax.axis_index("x"), lax.axis_index("y"), lax.axis_index("c")
    chip = 2 * cx + cy
    c_idx = cc.reshape(1).astype(jnp.int32)
    idx2 = jnp.stack([chip, cc]).astype(jnp.int32)

    conv_slot = lax.dynamic_update_slice(jnp.zeros((depth, 3, 256), F32),
                                         w_conv * (cc == 0).astype(F32), (0, 0, 64 * chip))
    conv_sum = _allreduce_small(_pack_small([conv_slot]), positions)
    w_conv_full = conv_sum[:depth * 3 * 256 // LANES].reshape(depth, 3, 256)

    def padded(name, i):
        (rows, _), rows_p = BIG[name]
        return jnp.pad(w[name][i].astype(BF16), ((0, rows_p - rows), (0, 0)))

    def groups_of(i, groups):
        return groups if i == 0 else (("all", BIG_NAMES),)

    started, after = {}, conv_sum
    for i in range(depth):
        for gname, names in groups_of(i, GATHER_GROUPS):
            shards = [padded(name, i) for name in names]
            lands = [lax.empty((N_CHIPS,) + s.shape, BF16) for s in shards]
            started[i, gname] = _exchange_start(_gather_copies, shards, lands, after, "gather_start_%d_%s" % (i, gname))
            after = started[i, gname][4]
    all_started = after

    inv = ROPE_BASE ** (-jnp.arange(ROPE_HALF, dtype=F32) / ROPE_HALF)
    zeros = lambda n: jnp.zeros((n,), F32)
    ones = jnp.ones((ROPE_HALF,), F32)
    inv_pat = jnp.concatenate([zeros(64), inv, inv, zeros(32)]).reshape(1, LANES)
    sign_pat = jnp.concatenate([zeros(64), -ones, ones, zeros(32)]).reshape(1, LANES)
    cp, sp = _rope_tables(positions.reshape(S, 1), inv_pat, sign_pat)

    def fetcher(i):
        table, have = dict(groups_of(i, GATHER_GROUPS)), {}

        def fetch(group, after):
            key = group if group in table else "all"
            if (i, key) in started:
                ssem, rsem, srcs, lands, _ = started.pop((i, key))
                if i == 0 and group == "ffn1a":
                    after = all_started
                srcs, landed = _exchange_wait(_gather_copies, ssem, rsem, srcs, lands, after,
                                              "gather_wait_%d_%s" % (i, key))
                have.update(_group_weights(table[key], _forward_rows(landed), srcs, chip))
            return have

        return fetch

    xs, saved = x[0], []
    gains = [{name: w[name][i].reshape(1, n) for name, n in GAINS} for i in range(depth)]
    for i in range(depth):
        xs, sv = _layer_fwd(xs, p[i, 0], fetcher(i), gains[i], w_conv_full[i], cp, sp)
        saved.append(sv)

    loss_part, dx = _loss_and_grad(xs, loss_target[0])
    loss = lax.psum(loss_part[0, 0], AXES)

    scattering = []

    def emitter(i):
        table = dict(groups_of(i, SCATTER_GROUPS))

        def emit(group, gw):
            if group in table:
                names = table[group]
            elif group == "ffn1":
                names = table["all"]
            else:
                return None
            full = [gw[name] for name in names]
            pair_sums = [_add_rows(b, g, c_idx) for b, g in zip(full, _swap_rows(full))]
            lands = [lax.empty((3,) + s.shape[1:], BF16) for s in pair_sums]
            st = _exchange_start(_scatter_copies, pair_sums, lands, c_idx, "scatter_start_%d_%s" % (i, group))
            scattering.append((i, group, names, st))
            return st[4]

        return emit

    ggs, token = [None] * depth, None
    for i in reversed(range(depth)):
        dx, ggs[i], token = _layer_bwd(dx, saved[i], gains[i], w_conv_full[i], cp, sp, emitter(i), token)

    fins, after = {name: None for name in BIG_NAMES}, dx
    for i, group, names, (ssem, rsem, srcs, lands, _) in scattering:
        srcs, arrived = _exchange_wait(_scatter_copies, ssem, rsem, srcs, lands, after,
                                       "scatter_wait_%d_%s" % (i, group))
        for name, s1, got in zip(names, srcs, arrived):
            after = fins[name] = _add_chip_sums(s1, got, idx2, i, fins[name], after)
    joined = _join_rows([fins[name] for name in BIG_NAMES])
    reduced = dict(zip(BIG_NAMES, joined))

    small_names = [name for name, _ in GAINS] + ["w_conv"]
    small = _allreduce_small(_pack_small([jnp.stack([ggs[i][name] for i in range(depth)]) for name in small_names]),
                             joined[0])
    flat, off = small.reshape(-1), 0
    for name, n in GAINS:
        reduced[name] = flat[off:off + depth * n].reshape(depth, n)
        off += depth * n
    conv_full = flat[off:off + depth * 3 * 256].reshape(depth, 3, 256)
    reduced["w_conv"] = lax.dynamic_slice(conv_full, (0, 0, 64 * chip), (depth, 3, 64))

    grads, deltas, new_m, new_v = {}, {}, {}, {}
    for name in WEIGHT_ORDER:
        shape = w[name].shape
        three_d = shape if len(shape) == 3 else (1,) + shape
        g_ = reduced[name] if len(shape) == 3 else reduced[name].reshape(three_d)
        outs = _adamw(w[name].reshape(three_d), g_, m[name].reshape(three_d), v[name].reshape(three_d), "adamw")
        if name in TRANSPOSED:
            outs = [jnp.swapaxes(o, 1, 2) for o in outs]
        grads[name], deltas[name], new_m[name], new_v[name] = (o.reshape(o.shape[-len(shape):]) for o in outs)

    return (loss, dx[None], *[grads[n] for n in WEIGHT_ORDER], *[deltas[n] for n in WEIGHT_ORDER],
            *[new_m[n] for n in WEIGHT_ORDER], *[new_v[n] for n in WEIGHT_ORDER])
```

```python
import jax
import jax.numpy as jnp
from jax import lax
from jax.experimental import pallas as pl
from jax.experimental.pallas import tpu as pltpu

F32 = jnp.float32
BF16 = jnp.bfloat16
MESH = pl.DeviceIdType.MESH
AXES = ("x", "y", "c")

D_MODEL = 1024
N_CHIPS = 4
EPS = 1e-6
NEG_INF = -1e30
ROPE_BASE = 10000.0
ROPE_HALF = 16
LANES = 128
SB_HEADS, MLA_HEADS, HEAD_DIM = 4, 8, 64
MLA_SCALE = 96.0 ** -0.5
SB_SCALE = 64.0 ** -0.5
SB_BQ, SB_BK = 512, 128
MLA_BQ, MLA_BK = 256, 256
ADAM_LR, ADAM_B1, ADAM_B2, ADAM_EPS, ADAM_WD, ADAM_STEP = 0.001, 0.9, 0.999, 1e-08, 0.01, 10
VMEM_LIMIT = 48 * 2 ** 20

Z_SB, Z_KV, Z_KR, Z_Q, Z_CV, Z_W = 0, 768, 1024, 1152, 1536, 2304

NT = (((1,), (1,)), ((), ()))
TN = (((0,), (0,)), ((), ()))

TRANSPOSED = ("w1_gate", "w1_up", "w2_gate", "w2_up", "w_in", "w_mla_uq")
BIG = {"w1_gate": ((704, 1024), 704), "w1_up": ((704, 1024), 704), "w1_down": ((704, 1024), 704),
       "w_in": ((552, 1024), 576), "w_mla_uq": ((192, 384), 192), "w_mla_ukv": ((256, 256), 256),
       "w_out": ((256, 1024), 256),
       "w2_gate": ((704, 1024), 704), "w2_up": ((704, 1024), 704), "w2_down": ((704, 1024), 704),
       "w_ple_gate": ((256, 1024), 256), "w_ple_proj": ((256, 256), 256)}
BIG_NAMES = tuple(BIG)
GAINS = (("g_ffn1_pre", 1024), ("g_ffn1_post", 1024), ("g_mix_pre", 1024), ("g_mla_q", 384),
         ("g_mla_kv", 256), ("g_mix_post", 1024), ("g_ffn2_pre", 1024), ("g_ffn2_post", 1024),
         ("g_ple_pre", 1024), ("g_ple_post", 1024))
WEIGHT_ORDER = ("g_ffn1_pre", "w1_gate", "w1_up", "w1_down", "g_ffn1_post", "g_mix_pre", "w_in", "g_mla_q",
                "w_mla_uq", "g_mla_kv", "w_mla_ukv", "w_conv", "w_out", "g_mix_post", "g_ffn2_pre", "w2_gate",
                "w2_up", "w2_down", "g_ffn2_post", "g_ple_pre", "w_ple_gate", "w_ple_proj", "g_ple_post")

_pcall = pl.pallas_call


def _params(sem=None):
    return pltpu.CompilerParams(dimension_semantics=sem, vmem_limit_bytes=VMEM_LIMIT)


def _dot(a, b, dims=None):
    a, b = a.astype(BF16), b.astype(BF16)
    if dims is None:
        return jnp.dot(a, b, preferred_element_type=F32)
    return lax.dot_general(a, b, dims, preferred_element_type=F32)


def _rstd(v):
    return lax.rsqrt(jnp.mean(v * v, axis=-1, keepdims=True) + EPS)


def _sigmoid(v):
    return 0.5 * jnp.tanh(0.5 * v) + 0.5


def _row_block(n, want, mult=16):
    for b in range(min(n, want), 0, -1):
        if n % b == 0 and b % mult == 0:
            return b
    return n


def _mm(pairs, mode, out_dtype, name, bm=512, bn=512):
    a0, b0 = pairs[0]
    if mode == "nn":
        M, N = a0.shape[0], b0.shape[1]
    elif mode == "nt":
        M, N = a0.shape[0], b0.shape[0]
    else:
        M, N = a0.shape[1], b0.shape[1]
        bn = max(bn, 1024)
    bm, bn = _row_block(M, bm, 128 if mode == "tn" else 16), _row_block(N, bn, 128)
    n_pairs = len(pairs)
    dims = {"nn": None, "nt": NT, "tn": TN}[mode]

    def body(*refs):
        acc = None
        for t in range(n_pairs):
            part = _dot(refs[2 * t][...], refs[2 * t + 1][...], dims)
            acc = part if acc is None else acc + part
        refs[-1][...] = acc.astype(refs[-1].dtype)

    in_specs, ops = [], []
    for a, b in pairs:
        if mode == "nn":
            in_specs += [pl.BlockSpec((bm, a.shape[1]), lambda j, i: (i, 0)),
                         pl.BlockSpec((b.shape[0], bn), lambda j, i: (0, j))]
        elif mode == "nt":
            in_specs += [pl.BlockSpec((bm, a.shape[1]), lambda j, i: (i, 0)),
                         pl.BlockSpec((bn, b.shape[1]), lambda j, i: (j, 0))]
        else:
            in_specs += [pl.BlockSpec((a.shape[0], bm), lambda j, i: (0, i)),
                         pl.BlockSpec((b.shape[0], bn), lambda j, i: (0, j))]
        ops += [a, b]
    return _pcall(body, name=name, grid=(N // bn, M // bm), in_specs=in_specs,
                  out_specs=pl.BlockSpec((bm, bn), lambda j, i: (i, j)),
                  out_shape=jax.ShapeDtypeStruct((M, N), out_dtype),
                  compiler_params=_params(("parallel", "parallel")))(*ops)


def _dw_col_shards(a, b, name):
    S, K = a.shape
    Np = b.shape[1] // N_CHIPS

    def body(a_ref, b_ref, o_ref):
        o_ref[...] = _dot(a_ref[...], b_ref[...], TN).astype(BF16)

    return _pcall(body, name=name, grid=(N_CHIPS,),
                  in_specs=[pl.BlockSpec((S, K), lambda j: (0, 0)), pl.BlockSpec((S, Np), lambda j: (0, j))],
                  out_specs=pl.BlockSpec((None, K, Np), lambda j: (j, 0, 0)),
                  out_shape=jax.ShapeDtypeStruct((N_CHIPS, K, Np), BF16),
                  compiler_params=_params(("parallel",)))(a, b)


def _ew(fn, ins, outs, name, br=512):
    S = max(a.shape[0] for a in ins)
    br = _row_block(S, br)
    n_in = len(ins)

    def body(*refs):
        res = fn(*[r[...] for r in refs[:n_in]])
        if not isinstance(res, tuple):
            res = (res,)
        for r, v in zip(refs[n_in:], res):
            r[...] = v.astype(r.dtype)

    in_specs = [pl.BlockSpec((br, a.shape[1]), lambda i: (i, 0)) if a.shape[0] == S and S > 1
                else pl.BlockSpec(a.shape, lambda i: (0, 0)) for a in ins]
    out = _pcall(body, name=name, grid=(S // br,), in_specs=in_specs,
                 out_specs=tuple(pl.BlockSpec((br, w), lambda i: (i, 0)) for w, _ in outs),
                 out_shape=tuple(jax.ShapeDtypeStruct((S, w), dt) for w, dt in outs),
                 compiler_params=_params(("parallel",)))(*ins)
    return out if len(outs) > 1 else out[0]


def _norm_fwd(x, g, name):
    return _ew(lambda xv, gv: xv * _rstd(xv) * gv, [x, g], [(x.shape[1], BF16)], name, br=512)


def _resid_norm(x, y, g, alpha, name):
    return _ew(lambda xv, yv, gv: xv + alpha * (yv * _rstd(yv) * gv), [x, y, g], [(x.shape[1], F32)], name)


def _norm_bwd(xin, g, dy, alpha, resid, out_dtype, name):
    S, W = xin.shape
    br = _row_block(S, 512)
    has_res = resid is not None

    def body(*refs):
        x_ref, g_ref, dy_ref = refs[:3]
        dx_ref, dg_ref = refs[-2:]
        xv, dyv = x_ref[...], dy_ref[...] * alpha
        r = _rstd(xv)
        xh = xv * r
        u = dyv * g_ref[...]
        dx = r * (u - xh * jnp.mean(u * xh, axis=-1, keepdims=True))
        if has_res:
            dx = dx + refs[3][...]
        dx_ref[...] = dx.astype(dx_ref.dtype)
        part = jnp.sum(dyv * xh, axis=0, keepdims=True)

        @pl.when(pl.program_id(0) == 0)
        def _():
            dg_ref[...] = part

        @pl.when(pl.program_id(0) > 0)
        def _():
            dg_ref[...] += part

    row = pl.BlockSpec((br, W), lambda i: (i, 0))
    vec = pl.BlockSpec((1, W), lambda i: (0, 0))
    ops = [xin, g, dy] + ([resid] if has_res else [])
    return _pcall(body, name=name, grid=(S // br,), in_specs=[row, vec, row] + ([row] if has_res else []),
                  out_specs=(row, vec),
                  out_shape=(jax.ShapeDtypeStruct((S, W), out_dtype), jax.ShapeDtypeStruct((1, W), F32)),
                  compiler_params=_params(("arbitrary",)))(*ops)


def _ffn_gate_up(h, wgt, wut, name):
    S, K = h.shape
    F = wgt.shape[0]
    bm, bn = _row_block(S, 512), _row_block(F, 1408, 128)

    def body(h_ref, wg_ref, wu_ref, g_ref, u_ref, a_ref):
        hv = h_ref[...]
        g = _dot(hv, wg_ref[...], NT)
        u = _dot(hv, wu_ref[...], NT)
        g_ref[...] = g.astype(BF16)
        u_ref[...] = u.astype(BF16)
        a_ref[...] = (g * _sigmoid(g) * u).astype(BF16)

    blk = pl.BlockSpec((bm, bn), lambda n, i: (i, n))
    wsp = pl.BlockSpec((bn, K), lambda n, i: (n, 0))
    return _pcall(body, name=name, grid=(F // bn, S // bm),
                  in_specs=[pl.BlockSpec((bm, K), lambda n, i: (i, 0)), wsp, wsp],
                  out_specs=(blk, blk, blk), out_shape=(jax.ShapeDtypeStruct((S, F), BF16),) * 3,
                  compiler_params=_params(("parallel", "parallel")))(h, wgt, wut)


def _ffn_bwd_mid(dy, wd, g, u, name):
    S, D = dy.shape
    F = g.shape[1]
    bm, bn = _row_block(S, 512), _row_block(F, 1408, 128)

    def body(dy_ref, wd_ref, g_ref, u_ref, dg_ref, du_ref):
        da = _dot(dy_ref[...], wd_ref[...], NT)
        gv, uv = g_ref[...].astype(F32), u_ref[...].astype(F32)
        s = _sigmoid(gv)
        dg_ref[...] = (da * uv * (s * (1.0 + gv * (1.0 - s)))).astype(BF16)
        du_ref[...] = (da * (gv * s)).astype(BF16)

    blk = pl.BlockSpec((bm, bn), lambda j, i: (i, j))
    return _pcall(body, name=name, grid=(F // bn, S // bm),
                  in_specs=[pl.BlockSpec((bm, D), lambda j, i: (i, 0)),
                            pl.BlockSpec((bn, D), lambda j, i: (j, 0)), blk, blk],
                  out_specs=(blk, blk), out_shape=(jax.ShapeDtypeStruct((S, F), BF16),) * 2,
                  compiler_params=_params(("parallel", "parallel")))(dy, wd, g, u)


def _ple_proj(p16, w):
    S, K = p16.shape
    bm = _row_block(S, 1024)

    def body(p_ref, w_ref, o_ref):
        o_ref[...] = _dot(p_ref[...], w_ref[...])

    return _pcall(body, name="ple_proj", grid=(N_CHIPS, S // bm),
                  in_specs=[pl.BlockSpec((bm, K), lambda j, i: (i, 0)),
                            pl.BlockSpec((None, K, 256), lambda j, i: (j, 0, 0))],
                  out_specs=pl.BlockSpec((bm, 256), lambda j, i: (i, j)),
                  out_shape=jax.ShapeDtypeStruct((S, N_CHIPS * 256), F32),
                  compiler_params=_params(("parallel", "parallel")))(p16, w)


def _loss_and_grad(xf, tgt):
    S, W = xf.shape
    br = _row_block(S, 256)

    def body(x_ref, t_ref, l_ref, d_ref):
        d = x_ref[...] - t_ref[...]
        d_ref[...] = d * (1.0 / W)
        part = 0.5 * jnp.sum(jnp.sum(d * d, axis=-1, keepdims=True) * (1.0 / W), axis=0, keepdims=True)

        @pl.when(pl.program_id(0) == 0)
        def _():
            l_ref[...] = part

        @pl.when(pl.program_id(0) > 0)
        def _():
            l_ref[...] += part

    row = pl.BlockSpec((br, W), lambda i: (i, 0))
    return _pcall(body, name="loss", grid=(S // br,), in_specs=[row, row],
                  out_specs=(pl.BlockSpec((1, 1), lambda i: (0, 0)), row),
                  out_shape=(jax.ShapeDtypeStruct((1, 1), F32), jax.ShapeDtypeStruct((S, W), F32)),
                  compiler_params=_params(("arbitrary",)))(xf, tgt)


def _adamw(w, g, m, v, name):
    L, R, C = w.shape
    Cp = g.shape[2]
    br = _row_block(R, 256, 8) if R % 8 == 0 else R

    def body(w_ref, g_ref, m_ref, v_ref, go_ref, d_ref, nm_ref, nv_ref):
        gv = g_ref[...][:, :C]
        nm = ADAM_B1 * m_ref[...] + (1.0 - ADAM_B1) * gv
        nv = ADAM_B2 * v_ref[...] + (1.0 - ADAM_B2) * (gv * gv)
        m_hat = nm / (1.0 - ADAM_B1 ** ADAM_STEP)
        v_hat = nv / (1.0 - ADAM_B2 ** ADAM_STEP)
        go_ref[...] = gv
        d_ref[...] = -ADAM_LR * (m_hat / (jnp.sqrt(v_hat) + ADAM_EPS) + ADAM_WD * w_ref[...])
        nm_ref[...] = nm
        nv_ref[...] = nv

    blk = pl.BlockSpec((None, br, C), lambda l, i: (l, i, 0))
    gblk = pl.BlockSpec((None, br, Cp), lambda l, i: (l, i, 0))
    return _pcall(body, name=name, grid=(L, R // br), in_specs=[blk, gblk, blk, blk], out_specs=(blk,) * 4,
                  out_shape=(jax.ShapeDtypeStruct((L, R, C), F32),) * 4,
                  compiler_params=_params(("parallel", "parallel")))(w, g, m, v)


def _rope_tables(pos, inv_pat, sign_pat):
    def fn(p, iv, sg):
        ang = p.astype(F32) * iv
        return jnp.cos(ang), jnp.sin(ang) * sg

    return _ew(fn, [pos, inv_pat, sign_pat], [(LANES, F32)] * 2, "rope_tables")


def _swap_halves_of_rope(v):
    W = v.shape[1]
    lane = lax.broadcasted_iota(jnp.int32, (1, W), 1) % LANES
    return jnp.where((lane >= 64) & (lane < 80), pltpu.roll(v, W - ROPE_HALF, 1),
                     jnp.where((lane >= 80) & (lane < 96), pltpu.roll(v, ROPE_HALF, 1), 0.0))


def _tile_lanes(v, n):
    return jnp.concatenate([v] * n, axis=1)


def _mla_prep(qf, kv, z, cp, sp):
    S = qf.shape[0]
    br = _row_block(S, 256)
    W = MLA_HEADS * LANES

    def body(q_ref, k_ref, r_ref, c_ref, s_ref, qo_ref, ko_ref):
        c, s = c_ref[...], s_ref[...]
        q = q_ref[...]
        qo_ref[...] = (q * _tile_lanes(c, MLA_HEADS) + _swap_halves_of_rope(q) * _tile_lanes(s, MLA_HEADS)).astype(BF16)
        r = r_ref[...]
        lane = lax.broadcasted_iota(jnp.int32, (1, LANES), 1)
        kr = jnp.where((lane >= 64) & (lane < 96), r * c + _swap_halves_of_rope(r) * s, 0.0)
        ko_ref[...] = (k_ref[...] + _tile_lanes(kr, MLA_HEADS)).astype(BF16)

    wide = pl.BlockSpec((br, W), lambda i: (i, 0))
    one = pl.BlockSpec((br, LANES), lambda i: (i, 0))
    return _pcall(body, name="mla_prep", grid=(S // br,),
                  in_specs=[wide, wide, pl.BlockSpec((br, LANES), lambda i: (i, Z_KR // LANES)), one, one],
                  out_specs=(wide, wide), out_shape=(jax.ShapeDtypeStruct((S, W), BF16),) * 2,
                  compiler_params=_params(("parallel",)))(qf, kv, z, cp, sp)


def _mla_prep_bwd(dq, dk, cp, sp):
    S, W = dq.shape
    br = _row_block(S, 256)

    def body(dq_ref, dk_ref, c_ref, s_ref, dqo_ref, dr_ref):
        c, s = c_ref[...], s_ref[...]
        d = dq_ref[...]
        dqo_ref[...] = (d * _tile_lanes(c, MLA_HEADS) + _swap_halves_of_rope(d * _tile_lanes(s, MLA_HEADS))).astype(BF16)
        dkv = dk_ref[...]
        tot = dkv[:, 0:LANES]
        for h in range(1, MLA_HEADS):
            tot = tot + dkv[:, h * LANES:(h + 1) * LANES]
        lane = lax.broadcasted_iota(jnp.int32, (1, LANES), 1)
        tot = jnp.where((lane >= 64) & (lane < 96), tot, 0.0)
        dr_ref[...] = tot * c + _swap_halves_of_rope(tot * s)

    wide = pl.BlockSpec((br, W), lambda i: (i, 0))
    one = pl.BlockSpec((br, LANES), lambda i: (i, 0))
    return _pcall(body, name="mla_prep_bwd", grid=(S // br,), in_specs=[wide, wide, one, one], out_specs=(wide, one),
                  out_shape=(jax.ShapeDtypeStruct((S, W), BF16), jax.ShapeDtypeStruct((S, LANES), F32)),
                  compiler_params=_params(("parallel",)))(dq, dk, cp, sp)


def _shift_down(v, k, row):
    return jnp.where(row >= k, pltpu.roll(v, k, 0), 0.0)


def _shift_up(v, k, row):
    n = v.shape[0]
    return jnp.where(row < n - k, pltpu.roll(v, n - k, 0), 0.0)


def _conv_fwd(z, w):
    S = z.shape[0]
    c0 = Z_CV // LANES

    def body(b_ref, c_ref, h_ref, w_ref, y_ref):
        u = c_ref[...] * h_ref[...]
        row = lax.broadcasted_iota(jnp.int32, u.shape, 0)
        wv = w_ref[...]
        conv = wv[0:1] * _shift_down(u, 2, row) + wv[1:2] * _shift_down(u, 1, row) + wv[2:3] * u
        y_ref[...] = b_ref[...] * conv

    def col(k):
        return pl.BlockSpec((S, LANES), lambda j: (0, c0 + 2 * k + j))

    return _pcall(body, name="conv_fwd", grid=(2,),
                  in_specs=[col(0), col(1), col(2), pl.BlockSpec((3, LANES), lambda j: (0, j))],
                  out_specs=pl.BlockSpec((S, LANES), lambda j: (0, j)),
                  out_shape=jax.ShapeDtypeStruct((S, 256), F32), compiler_params=_params(("parallel",)))(z, z, z, w)


def _conv_bwd(z, w, dc_all):
    S = z.shape[0]
    c0 = Z_CV // LANES

    def body(b_ref, c_ref, h_ref, w_ref, dy_ref, db_ref, dc_ref, dh_ref, dw_ref):
        cv, hv, bv, dyv = c_ref[...], h_ref[...], b_ref[...], dy_ref[...]
        u = cv * hv
        row = lax.broadcasted_iota(jnp.int32, u.shape, 0)
        wv = w_ref[...]
        u1, u2 = _shift_down(u, 1, row), _shift_down(u, 2, row)
        conv = wv[0:1] * u2 + wv[1:2] * u1 + wv[2:3] * u
        db_ref[...] = dyv * conv
        dconv = dyv * bv
        du = wv[2:3] * dconv + wv[1:2] * _shift_up(dconv, 1, row) + wv[0:1] * _shift_up(dconv, 2, row)
        dc_ref[...] = du * hv
        dh_ref[...] = du * cv
        dw_ref[0:1, :] = jnp.sum(dconv * u2, axis=0, keepdims=True)
        dw_ref[1:2, :] = jnp.sum(dconv * u1, axis=0, keepdims=True)
        dw_ref[2:3, :] = jnp.sum(dconv * u, axis=0, keepdims=True)

    def col(k):
        return pl.BlockSpec((S, LANES), lambda j: (0, c0 + 2 * k + j))

    wsp = pl.BlockSpec((3, LANES), lambda j: (0, j))
    osp = pl.BlockSpec((S, LANES), lambda j: (0, j))
    db, dc, dh, dw = _pcall(
        body, name="conv_bwd", grid=(2,),
        in_specs=[col(0), col(1), col(2), wsp, pl.BlockSpec((S, LANES), lambda j: (0, 6 + j))],
        out_specs=(osp, osp, osp, wsp),
        out_shape=(jax.ShapeDtypeStruct((S, 256), F32),) * 3 + (jax.ShapeDtypeStruct((3, 256), F32),),
        compiler_params=_params(("parallel",)))(z, z, z, w, dc_all)
    return db, dc, dh, dw


def _half_masks():
    lane = lax.broadcasted_iota(jnp.int32, (1, LANES), 1)
    return lane < HEAD_DIM, lane >= HEAD_DIM


def _pair(v, p):
    return v[:, p * LANES:(p + 1) * LANES]


def _keep(mask, v):
    return jnp.where(mask, v, jnp.zeros_like(v))


def _lane_col(vals):
    lane = lax.broadcasted_iota(jnp.int32, (1, LANES), 1)
    out = jnp.zeros((vals[0].shape[0], LANES), F32)
    for h, v in enumerate(vals):
        out = jnp.where(lane == h, v, out)
    return out


def _mla_fwd(q, k, kv):
    S = q.shape[0]
    H = MLA_HEADS
    bq, bk = _row_block(S, MLA_BQ), _row_block(S, MLA_BK)
    per = bq // bk

    def body(q_ref, k_ref, v_ref, o_ref, lse_ref):
        i = pl.program_id(0)
        lo, hi = _half_masks()
        qb = q_ref[...]

        def block(j, carry, ok):
            ms, ls, accs = carry
            off = pl.multiple_of(j * bk, bk)
            kb = k_ref[pl.ds(off, bk), :]
            vb = v_ref[pl.ds(off, bk), :].astype(BF16)
            ss = [_dot(_pair(qb, h), _pair(kb, h), NT) * MLA_SCALE for h in range(H)]
            if ok is not None:
                ss = [jnp.where(ok, s, NEG_INF) for s in ss]
            ms2 = [jnp.maximum(ms[h], jnp.max(ss[h], axis=-1, keepdims=True)) for h in range(H)]
            ps = [jnp.exp(ss[h] - ms2[h]) for h in range(H)]
            al = [jnp.exp(ms[h] - ms2[h]) for h in range(H)]
            ls2 = [al[h] * ls[h] + jnp.sum(ps[h], axis=-1, keepdims=True) for h in range(H)]
            pvs = [_dot(ps[h], _pair(vb, h // 2)) for h in range(H)]
            accs2 = []
            for p in range(H // 2):
                scale = jnp.where(lo, al[2 * p], al[2 * p + 1])
                accs2.append(scale * accs[p] + jnp.where(lo, pvs[2 * p], pvs[2 * p + 1]))
            return tuple(ms2), tuple(ls2), tuple(accs2)

        init = (tuple(jnp.full((bq, 1), NEG_INF, F32) for _ in range(H)),
                tuple(jnp.zeros((bq, 1), F32) for _ in range(H)),
                tuple(jnp.zeros((bq, LANES), F32) for _ in range(H // 2)))
        carry = lax.fori_loop(0, i * per, lambda j, c: block(j, c, None), init)
        row = lax.broadcasted_iota(jnp.int32, (bq, bk), 0)
        col = lax.broadcasted_iota(jnp.int32, (bq, bk), 1)
        for t in range(per):
            carry = block(i * per + t, carry, col + t * bk <= row)
        ms, ls, accs = carry
        o_ref[...] = jnp.concatenate(
            [accs[p] / jnp.where(lo, ls[2 * p], ls[2 * p + 1]) for p in range(H // 2)], axis=1)
        lse_ref[...] = _lane_col([ms[h] + jnp.log(ls[h]) for h in range(H)])

    return _pcall(body, name="mla_fwd", grid=(S // bq,),
                  in_specs=[pl.BlockSpec((bq, H * LANES), lambda i: (i, 0)),
                            pl.BlockSpec((S, H * LANES), lambda i: (0, 0)),
                            pl.BlockSpec((S, H * HEAD_DIM), lambda i: (0, 2))],
                  out_specs=(pl.BlockSpec((bq, H * HEAD_DIM), lambda i: (i, 0)),
                             pl.BlockSpec((bq, LANES), lambda i: (i, 0))),
                  out_shape=(jax.ShapeDtypeStruct((S, H * HEAD_DIM), F32), jax.ShapeDtypeStruct((S, LANES), F32)),
                  compiler_params=_params(("parallel",)))(q, k, kv)


def _mla_bwd(q, k, kv, o, lse, dc_all):
    S = q.shape[0]
    H = MLA_HEADS
    bq, bk = _row_block(S, MLA_BQ), _row_block(S, MLA_BK)
    per = bq // bk

    def body(q_ref, k_ref, v_ref, o_ref, lse_ref, doa_ref, dob_ref, dq_ref, dk_ref, dv_ref):
        i = pl.program_id(0)

        @pl.when(i == 0)
        def _():
            dk_ref[...] = jnp.zeros_like(dk_ref)
            dv_ref[...] = jnp.zeros_like(dv_ref)

        lo, hi = _half_masks()
        qb = q_ref[...]
        dob = jnp.concatenate([doa_ref[...], dob_ref[...]], axis=1)
        prod = dob * o_ref[...]
        lse_v = lse_ref[...]
        do16 = dob.astype(BF16)
        dom, deltas, lses = [], [], []
        for h in range(H):
            mk = lo if h % 2 == 0 else hi
            dom.append(_keep(mk, _pair(do16, h // 2)))
            deltas.append(jnp.sum(_keep(mk, _pair(prod, h // 2)), axis=-1, keepdims=True))
            lses.append(lse_v[:, h:h + 1])

        def block(j, dqs, ok):
            off = pl.multiple_of(j * bk, bk)
            kb = k_ref[pl.ds(off, bk), :]
            vb = v_ref[pl.ds(off, bk), :].astype(BF16)
            ss = [_dot(_pair(qb, h), _pair(kb, h), NT) * MLA_SCALE for h in range(H)]
            dps = [_dot(dom[h], _pair(vb, h // 2), NT) for h in range(H)]
            ps = [jnp.exp(ss[h] - lses[h]) for h in range(H)]
            if ok is not None:
                ps = [jnp.where(ok, p, 0.0) for p in ps]
            ds16 = [(ps[h] * (dps[h] - deltas[h]) * MLA_SCALE).astype(BF16) for h in range(H)]
            p16 = [p.astype(BF16) for p in ps]
            dks = [_dot(ds16[h], _pair(qb, h), TN) for h in range(H)]
            dvps = [_dot(p16[h], dom[h], TN) for h in range(H)]
            dqs2 = [dqs[h] + _dot(ds16[h], _pair(kb, h)) for h in range(H)]
            dk_ref[pl.ds(off, bk), :] += jnp.concatenate(dks, axis=1)
            dv_ref[pl.ds(off, bk), :] += jnp.concatenate(
                [dvps[2 * p] + dvps[2 * p + 1] for p in range(H // 2)], axis=1)
            return tuple(dqs2)

        dqs = lax.fori_loop(0, i * per, lambda j, c: block(j, c, None),
                            tuple(jnp.zeros((bq, LANES), F32) for _ in range(H)))
        row = lax.broadcasted_iota(jnp.int32, (bq, bk), 0)
        col = lax.broadcasted_iota(jnp.int32, (bq, bk), 1)
        for t in range(per):
            dqs = block(i * per + t, dqs, col + t * bk <= row)
        dq_ref[...] = jnp.concatenate(dqs, axis=1)

    wide = pl.BlockSpec((bq, H * LANES), lambda i: (i, 0))
    full = pl.BlockSpec((S, H * LANES), lambda i: (0, 0))
    return _pcall(body, name="mla_bwd", grid=(S // bq,),
                  in_specs=[wide, full, pl.BlockSpec((S, H * HEAD_DIM), lambda i: (0, 2)),
                            pl.BlockSpec((bq, H * HEAD_DIM), lambda i: (i, 0)),
                            pl.BlockSpec((bq, LANES), lambda i: (i, 0)),
                            pl.BlockSpec((bq, 256), lambda i: (i, 1)), pl.BlockSpec((bq, 256), lambda i: (i, 2))],
                  out_specs=(wide, full, pl.BlockSpec((S, H * HEAD_DIM), lambda i: (0, 0))),
                  out_shape=(jax.ShapeDtypeStruct((S, H * LANES), F32), jax.ShapeDtypeStruct((S, H * LANES), F32),
                             jax.ShapeDtypeStruct((S, H * HEAD_DIM), F32)),
                  compiler_params=_params(("arbitrary",)))(q, k, kv, o, lse, dc_all, dc_all)


def _dot_exact(x, tri):
    h1 = x.astype(BF16)
    h2 = (x - h1.astype(F32)).astype(BF16)
    return _dot(h1, tri) + _dot(h2, tri)


def _softplus(z):
    return jnp.maximum(z, 0.0) + jnp.log(1.0 + jnp.exp(-jnp.abs(z)))


def _sb_fwd(z):
    S = z.shape[0]
    H = SB_HEADS
    bq, bk = _row_block(S, SB_BQ), _row_block(S, SB_BK)
    per = bq // bk

    def body(q_ref, k_ref, v_ref, o_ref, t_ref):
        i = pl.program_id(0)
        lo, hi = _half_masks()
        q16 = (q_ref[...] * SB_SCALE).astype(BF16)
        qm = [_keep(lo if h % 2 == 0 else hi, _pair(q16, h // 2)) for h in range(H)]
        rr = lax.broadcasted_iota(jnp.int32, (bk, bk), 0)
        cc = lax.broadcasted_iota(jnp.int32, (bk, bk), 1)
        later = (rr > cc).astype(BF16)

        def block(j, carry, mask):
            tails, accs = carry
            off = pl.multiple_of(j * bk, bk)
            kb = k_ref[pl.ds(off, bk), :].astype(BF16)
            vb = v_ref[pl.ds(off, bk), :].astype(BF16)
            zs = [_dot(qm[h], _pair(kb, h // 2), NT) for h in range(H)]
            sps = [_softplus(z) for z in zs]
            lnots = [-sp if mask is None else jnp.where(mask, -sp, 0.0) for sp in sps]
            sums = [_dot_exact(lnot, later) for lnot in lnots]
            ws = []
            for h in range(H):
                w = jnp.exp((zs[h] - sps[h]) + (tails[h] + sums[h]))
                ws.append((w if mask is None else jnp.where(mask, w, 0.0)).astype(BF16))
            pvs = [_dot(ws[h], _pair(vb, h // 2)) for h in range(H)]
            accs2 = [accs[p] + jnp.where(lo, pvs[2 * p], pvs[2 * p + 1]) for p in range(H // 2)]
            tails2 = [tails[h] + jnp.sum(lnots[h], axis=-1, keepdims=True) for h in range(H)]
            return tuple(tails2), tuple(accs2)

        carry = (tuple(jnp.zeros((bq, 1), F32) for _ in range(H)),
                 tuple(jnp.zeros((bq, LANES), F32) for _ in range(H // 2)))
        row = lax.broadcasted_iota(jnp.int32, (bq, bk), 0)
        col = lax.broadcasted_iota(jnp.int32, (bq, bk), 1)
        for t in range(per):
            carry = block(i * per + per - 1 - t, carry, col + (per - 1 - t) * bk < row)
        tails, accs = lax.fori_loop(0, i * per, lambda t, c: block(i * per - 1 - t, c, None), carry)
        o_ref[...] = jnp.concatenate(accs, axis=1)
        t_ref[...] = _lane_col(tails)

    return _pcall(body, name="sb_fwd", grid=(S // bq,),
                  in_specs=[pl.BlockSpec((bq, 256), lambda i: (i, 0)), pl.BlockSpec((S, 256), lambda i: (0, 1)),
                            pl.BlockSpec((S, 256), lambda i: (0, 2))],
                  out_specs=(pl.BlockSpec((bq, 256), lambda i: (i, 0)), pl.BlockSpec((bq, LANES), lambda i: (i, 0))),
                  out_shape=(jax.ShapeDtypeStruct((S, 256), F32), jax.ShapeDtypeStruct((S, LANES), F32)),
                  compiler_params=_params(("parallel",)))(z, z, z)


def _sb_bwd(z, tot, dc_all):
    S = z.shape[0]
    H = SB_HEADS
    bq, bk = _row_block(S, SB_BQ), _row_block(S, SB_BK)
    per = bq // bk

    def body(q_ref, k_ref, v_ref, t_ref, do_ref, dq_ref, dk_ref, dv_ref):
        i = pl.program_id(0)

        @pl.when(i == 0)
        def _():
            dk_ref[...] = jnp.zeros_like(dk_ref)
            dv_ref[...] = jnp.zeros_like(dv_ref)

        lo, hi = _half_masks()
        q16 = (q_ref[...] * SB_SCALE).astype(BF16)
        do16 = do_ref[...].astype(BF16)
        tot_v = t_ref[...]
        masks = [lo if h % 2 == 0 else hi for h in range(H)]
        qm = [_keep(masks[h], _pair(q16, h // 2)) for h in range(H)]
        dom = [_keep(masks[h], _pair(do16, h // 2)) for h in range(H)]
        tots = [tot_v[:, h:h + 1] for h in range(H)]
        rr = lax.broadcasted_iota(jnp.int32, (bk, bk), 0)
        cc = lax.broadcasted_iota(jnp.int32, (bk, bk), 1)
        upto = (rr <= cc).astype(BF16)
        before = (rr < cc).astype(BF16)

        def block(j, carry, mask):
            heads, eheads, dqs = carry
            off = pl.multiple_of(j * bk, bk)
            kb = k_ref[pl.ds(off, bk), :].astype(BF16)
            vb = v_ref[pl.ds(off, bk), :].astype(BF16)
            zs = [_dot(qm[h], _pair(kb, h // 2), NT) for h in range(H)]
            dws = [_dot(dom[h], _pair(vb, h // 2), NT) for h in range(H)]
            sps = [_softplus(z) for z in zs]
            lnots = [-sp if mask is None else jnp.where(mask, -sp, 0.0) for sp in sps]
            pres = [_dot_exact(lnot, upto) for lnot in lnots]
            lsigs = [zs[h] - sps[h] for h in range(H)]
            ws = [jnp.exp(lsigs[h] + (tots[h] - (heads[h] + pres[h]))) for h in range(H)]
            if mask is not None:
                ws = [jnp.where(mask, w, 0.0) for w in ws]
            es = [ws[h] * dws[h] for h in range(H)]
            esums = [eheads[h] + _dot_exact(es[h], before) for h in range(H)]
            dz16 = []
            for h in range(H):
                sig = jnp.exp(lsigs[h])
                dz = es[h] * (1.0 - sig) - sig * esums[h]
                dz16.append((dz if mask is None else jnp.where(mask, dz, 0.0)).astype(BF16))
            w16 = [w.astype(BF16) for w in ws]
            dkps = [_dot(dz16[h], qm[h], TN) for h in range(H)]
            dvps = [_dot(w16[h], dom[h], TN) for h in range(H)]
            dqps = [_dot(dz16[h], _pair(kb, h // 2)) for h in range(H)]
            dk_ref[pl.ds(off, bk), :] += jnp.concatenate([dkps[2 * p] + dkps[2 * p + 1] for p in range(H // 2)], axis=1)
            dv_ref[pl.ds(off, bk), :] += jnp.concatenate([dvps[2 * p] + dvps[2 * p + 1] for p in range(H // 2)], axis=1)
            dqs2 = [dqs[p] + jnp.where(lo, dqps[2 * p], dqps[2 * p + 1]) for p in range(H // 2)]
            heads2 = [heads[h] + jnp.sum(lnots[h], axis=-1, keepdims=True) for h in range(H)]
            eheads2 = [eheads[h] + jnp.sum(es[h], axis=-1, keepdims=True) for h in range(H)]
            return tuple(heads2), tuple(eheads2), tuple(dqs2)

        zeros = tuple(jnp.zeros((bq, 1), F32) for _ in range(H))
        init = (zeros, zeros, tuple(jnp.zeros((bq, LANES), F32) for _ in range(H // 2)))
        carry = lax.fori_loop(0, i * per, lambda j, c: block(j, c, None), init)
        row = lax.broadcasted_iota(jnp.int32, (bq, bk), 0)
        col = lax.broadcasted_iota(jnp.int32, (bq, bk), 1)
        for t in range(per):
            carry = block(i * per + t, carry, col + t * bk < row)
        dq_ref[...] = jnp.concatenate(carry[2], axis=1) * SB_SCALE

    blk = pl.BlockSpec((bq, 256), lambda i: (i, 0))
    full = pl.BlockSpec((S, 256), lambda i: (0, 0))
    return _pcall(body, name="sb_bwd", grid=(S // bq,),
                  in_specs=[blk, pl.BlockSpec((S, 256), lambda i: (0, 1)), pl.BlockSpec((S, 256), lambda i: (0, 2)),
                            pl.BlockSpec((bq, LANES), lambda i: (i, 0)), blk],
                  out_specs=(blk, full, full), out_shape=(jax.ShapeDtypeStruct((S, 256), F32),) * 3,
                  compiler_params=_params(("arbitrary",)))(z, z, z, tot, dc_all)


_HBM = pl.BlockSpec(memory_space=pltpu.HBM)
_SEM = pl.BlockSpec(memory_space=pltpu.SEMAPHORE)
_ANY = pl.BlockSpec(memory_space=pl.ANY)


def _place():
    x, y, c = lax.axis_index("x"), lax.axis_index("y"), lax.axis_index("c")
    other_chips = ((1 - x, y), (x, 1 - y), (1 - x, 1 - y))
    return x, y, c, other_chips


def _remote(src, dst, send_sems, recv_sems, k, dev):
    return pltpu.make_async_remote_copy(src_ref=src, dst_ref=dst, send_sem=send_sems.at[k], recv_sem=recv_sems.at[k],
                                        device_id=dev, device_id_type=MESH)


def _half_rows(c, n_rows):
    half = n_rows // 2
    return pl.ds(pl.multiple_of(c * half, 16), half)


def _gather_copies(src, land, send_sems, recv_sems):
    x, y, c, chips = _place()
    me = 2 * x + y
    cps = []
    for w in range(len(src)):
        rows = _half_rows(c, src[w].shape[0])
        for k, (cx, cy) in enumerate(chips):
            cps.append(_remote(src[w].at[rows], land[w].at[me, rows], send_sems, recv_sems, 3 * w + k, (cx, cy, c)))
    return cps


def _scatter_copies(src, land, send_sems, recv_sems):
    x, y, c, chips = _place()
    cps = []
    for w in range(len(src)):
        for k, (cx, cy) in enumerate(chips):
            cps.append(_remote(src[w].at[2 * cx + cy], land[w].at[k], send_sems, recv_sems, 3 * w + k, (cx, cy, c)))
    return cps


def _exchange_start(copies, srcs, lands, after, name):
    n = len(srcs)

    def body(*refs):
        src, land = refs[:n], refs[n:2 * n]
        send_sems, recv_sems = refs[2 * n + 1], refs[2 * n + 2]
        token = refs[-1]
        for cp in copies(src, land, send_sems, recv_sems):
            cp.start()
        token[...] = jnp.zeros_like(token)

    thru = tuple(pltpu.HBM(a.shape, a.dtype) for a in list(srcs) + list(lands))
    out = _pcall(body, name=name,
                 out_shape=(pltpu.SemaphoreType.DMA((3 * n,)), pltpu.SemaphoreType.DMA((3 * n,))) + thru
                 + (jax.ShapeDtypeStruct((8, LANES), F32),),
                 in_specs=[_HBM] * (2 * n) + [_ANY],
                 out_specs=(_SEM, _SEM) + (_HBM,) * (2 * n) + (pl.BlockSpec(memory_space=pltpu.VMEM),),
                 input_output_aliases={i: 2 + i for i in range(2 * n)},
                 compiler_params=pltpu.CompilerParams(has_side_effects=pltpu.SideEffectType.DATAFLOW_SIDE_EFFECTING))(
        *[pltpu.with_memory_space_constraint(a, pltpu.HBM) for a in list(srcs) + list(lands)], after)
    return out[0], out[1], list(out[2:2 + n]), list(out[2 + n:2 + 2 * n]), out[-1]


def _exchange_wait(copies, send_sems, recv_sems, srcs, lands, after, name):
    n = len(srcs)

    def body(*refs):
        src, land = refs[:n], refs[n:2 * n]
        ssem, rsem = refs[2 * n], refs[2 * n + 1]
        for cp in copies(src, land, ssem, rsem):
            cp.wait_send()
            cp.wait_recv()

    out = _pcall(body, name=name, out_shape=tuple(pltpu.HBM(a.shape, a.dtype) for a in list(srcs) + list(lands)),
                 in_specs=[_HBM] * (2 * n) + [_SEM, _SEM, _ANY], out_specs=(_HBM,) * (2 * n),
                 input_output_aliases={i: i for i in range(2 * n)},
                 compiler_params=pltpu.CompilerParams(has_side_effects=pltpu.SideEffectType.DATAFLOW_SIDE_EFFECTING))(
        *srcs, *lands, send_sems, recv_sems, after)
    return list(out[:n]), list(out[n:])


def _forward_rows(gathered):
    n = len(gathered)

    def body(*refs):
        src, dst = refs[:n], refs[n:2 * n]
        send_sems, recv_sems = refs[2 * n:]
        x, y, c, chips = _place()
        sib = (x, y, 1 - c)
        cps = []
        for w in range(n):
            K = src[w].shape[1]
            for k, (cx, cy) in enumerate(chips):
                blk = (2 * cx + cy, _half_rows(c, K))
                cps.append(_remote(src[w].at[blk], dst[w].at[blk], send_sems, recv_sems, 3 * w + k, sib))
        for cp in cps:
            cp.start()
        for w in range(n):
            K = src[w].shape[1]
            for k, (cx, cy) in enumerate(chips):
                blk = dst[w].at[2 * cx + cy, _half_rows(1 - c, K)]
                _remote(blk, blk, send_sems, recv_sems, 3 * w + k, sib).wait_recv()
        for cp in cps:
            cp.wait_send()

    return _pcall(body, name="forward_rows", in_specs=[_HBM] * n, out_specs=(_HBM,) * n,
                  out_shape=tuple(jax.ShapeDtypeStruct(g.shape, g.dtype) for g in gathered),
                  input_output_aliases={w: w for w in range(n)},
                  scratch_shapes=[pltpu.SemaphoreType.DMA((3 * n,)), pltpu.SemaphoreType.DMA((3 * n,))])(*gathered)


def _swap_rows(bufs):
    n = len(bufs)

    def body(*refs):
        src, dst = refs[:n], refs[n:2 * n]
        send_sems, recv_sems = refs[2 * n:]
        x, y, c, _ = _place()
        cps = [_remote(src[w].at[:, _half_rows(1 - c, src[w].shape[1])], dst[w], send_sems, recv_sems, w,
                       (x, y, 1 - c)) for w in range(n)]
        for cp in cps:
            cp.start()
        for cp in cps:
            cp.wait()

    return _pcall(body, name="swap_rows", in_specs=[_HBM] * n, out_specs=(_HBM,) * n,
                  out_shape=tuple(jax.ShapeDtypeStruct((b.shape[0], b.shape[1] // 2, b.shape[2]), b.dtype) for b in bufs),
                  scratch_shapes=[pltpu.SemaphoreType.DMA((n,)), pltpu.SemaphoreType.DMA((n,))])(*bufs)


def _join_rows(fins):
    n = len(fins)

    def body(*refs):
        src, dst = refs[:n], refs[n:2 * n]
        send_sems, recv_sems = refs[2 * n:]
        x, y, c, _ = _place()
        sib = (x, y, 1 - c)
        cps = []
        for w in range(n):
            blk = (slice(None), _half_rows(c, src[w].shape[1]))
            cps.append(_remote(src[w].at[blk], dst[w].at[blk], send_sems, recv_sems, w, sib))
        for cp in cps:
            cp.start()
        for w in range(n):
            blk = dst[w].at[:, _half_rows(1 - c, src[w].shape[1])]
            _remote(blk, blk, send_sems, recv_sems, w, sib).wait_recv()
        for cp in cps:
            cp.wait_send()

    return _pcall(body, name="join_rows", in_specs=[_HBM] * n, out_specs=(_HBM,) * n,
                  out_shape=tuple(jax.ShapeDtypeStruct(f.shape, f.dtype) for f in fins),
                  input_output_aliases={w: w for w in range(n)},
                  scratch_shapes=[pltpu.SemaphoreType.DMA((n,)), pltpu.SemaphoreType.DMA((n,))])(*fins)


def _add_rows(bufs, gots, c_idx):
    n = len(bufs)
    bks = [_row_block(g.shape[1], 192) for g in gots]
    nbs = [g.shape[1] // bk for g, bk in zip(gots, bks)]

    def body(c_ref, *refs):
        i = pl.program_id(1)
        for w in range(n):
            @pl.when(i < nbs[w])
            def _(w=w):
                refs[2 * n + w][...] = (refs[w][...].astype(F32) + refs[n + w][...].astype(F32)).astype(BF16)

    def mine(w):
        return pl.BlockSpec((None, bks[w], gots[w].shape[2]),
                            lambda j, i, c_ref: (j, c_ref[0] * nbs[w] + jnp.minimum(i, nbs[w] - 1), 0))

    def half(w):
        return pl.BlockSpec((None, bks[w], gots[w].shape[2]), lambda j, i, c_ref: (j, jnp.minimum(i, nbs[w] - 1), 0))

    grid_spec = pltpu.PrefetchScalarGridSpec(
        num_scalar_prefetch=1, grid=(N_CHIPS, max(nbs)),
        in_specs=[mine(w) for w in range(n)] + [half(w) for w in range(n)], out_specs=[half(w) for w in range(n)])
    return _pcall(body, name="add_rows", grid_spec=grid_spec,
                  out_shape=[jax.ShapeDtypeStruct(g.shape, BF16) for g in gots],
                  compiler_params=_params(("parallel", "arbitrary")))(c_idx, *bufs, *gots)


def _add_chip_sums(s1s, gots, idx, layer, intos, after):
    n = len(s1s)
    bks = [_row_block(a.shape[1], 64) for a in s1s]
    nbs = [a.shape[1] // bk for a, bk in zip(s1s, bks)]
    fresh = intos[0] is None

    def body(idx_ref, *refs):
        i = pl.program_id(0)
        outs = refs[len(refs) - n:]
        for w in range(n):
            @pl.when(i < nbs[w])
            def _(w=w):
                acc = refs[w][...].astype(F32)
                for k in range(3):
                    acc = acc + refs[n + w][k].astype(F32)
                outs[w][...] = acc

    def blk(w, rows):
        return pl.BlockSpec((rows, bks[w], s1s[w].shape[2]),
                            lambda i, idx_ref: (0 if rows else idx_ref[0], jnp.minimum(i, nbs[w] - 1), 0))

    def out(w):
        return pl.BlockSpec((None, bks[w], s1s[w].shape[2]),
                            lambda i, idx_ref: (layer, idx_ref[1] * nbs[w] + jnp.minimum(i, nbs[w] - 1), 0))

    ops = [idx, *s1s, *gots, after] + ([] if fresh else list(intos))
    in_specs = [blk(w, None) for w in range(n)] + [blk(w, 3) for w in range(n)] + [_ANY] * (1 if fresh else 1 + n)
    grid_spec = pltpu.PrefetchScalarGridSpec(num_scalar_prefetch=1, grid=(max(nbs),), in_specs=in_specs,
                                             out_specs=[out(w) for w in range(n)])
    return _pcall(body, name="add_chip_sums" if fresh else "add_chip_sums_into", grid_spec=grid_spec,
                  out_shape=[jax.ShapeDtypeStruct((2, 2 * a.shape[1], a.shape[2]), F32) for a in s1s],
                  input_output_aliases={} if fresh else {2 * n + 2 + w: w for w in range(n)},
                  compiler_params=_params(("arbitrary",)))(*ops)


def _allreduce_small(v, after):
    R, W = v.shape

    def body(v_ref, after_ref, o_ref, buf, send_sems, recv_sems):
        x, y, c, _ = _place()
        me = 4 * x + 2 * y + c
        buf[0] = v_ref[...]
        cps = []
        for r in range(1, 8):
            peer = (x if not r & 4 else 1 - x, y if not r & 2 else 1 - y, c if not r & 1 else 1 - c)
            cp = _remote(v_ref, buf.at[r], send_sems, recv_sems, r - 1, peer)
            cp.start()
            cps.append(cp)
        for cp in cps:
            cp.wait()
        acc = buf[me]
        for d in range(1, 8):
            acc = acc + buf[jnp.bitwise_xor(me, d)]
        o_ref[...] = acc

    return _pcall(body, name="allreduce_small", out_shape=jax.ShapeDtypeStruct((R, W), F32),
                  in_specs=[pl.BlockSpec(memory_space=pltpu.VMEM), _ANY],
                  out_specs=pl.BlockSpec(memory_space=pltpu.VMEM),
                  scratch_shapes=[pltpu.VMEM((8, R, W), F32), pltpu.SemaphoreType.DMA((7,)),
                                  pltpu.SemaphoreType.DMA((7,))])(v, after)


def _cat_cols(g):
    return g.transpose(1, 0, 2).reshape(g.shape[1], -1)


def _cut_cols(w):
    K, N = w.shape
    return w.reshape(K, N_CHIPS, N // N_CHIPS).transpose(1, 0, 2)


def _regroup_w_in_t(g):
    w = g[:, :552].reshape(2208, g.shape[2])
    zeros = lambda n: jnp.zeros((n, w.shape[1]), w.dtype)
    return jnp.concatenate([w[0:768], w[1152:1408], zeros(64), w[1408:1440], zeros(32), w[768:1152], w[1440:2208]],
                           axis=0)


def _ungroup_w_in_t(w):
    nat = jnp.concatenate([w[0:768], w[Z_Q:Z_Q + 384], w[Z_KV:Z_KV + 256], w[Z_KR + 64:Z_KR + 96], w[Z_CV:Z_W]],
                          axis=0)
    return jnp.pad(nat.reshape(N_CHIPS, 552, w.shape[1]), ((0, 0), (0, 24), (0, 0)))


def _regroup_w_uq_t(g):
    K = g.shape[2]
    return jnp.pad(g.reshape(MLA_HEADS, 96, K), ((0, 0), (0, 32), (0, 0))).reshape(MLA_HEADS * LANES, K)


def _ungroup_w_uq_t(w):
    K = w.shape[1]
    return w.reshape(MLA_HEADS, LANES, K)[:, :96].reshape(N_CHIPS, 192, K)


def _regroup_w_ukv(w):
    K = w.shape[0]
    t = w.reshape(K, MLA_HEADS, 128)
    return jnp.concatenate([jnp.pad(t[:, :, :64], ((0, 0), (0, 0), (0, 64))).reshape(K, MLA_HEADS * LANES),
                            t[:, :, 64:].reshape(K, MLA_HEADS * HEAD_DIM)], axis=1)


def _ungroup_w_ukv(w):
    K = w.shape[0]
    return jnp.concatenate([w[:, :MLA_HEADS * LANES].reshape(K, MLA_HEADS, LANES)[:, :, :64],
                            w[:, MLA_HEADS * LANES:].reshape(K, MLA_HEADS, 64)], axis=2).reshape(K, 1024)


def _ffn_fwd(x, g_pre, g_post, fetch, tag):
    W = fetch("ffn%sa" % tag, x)
    h = _norm_fwd(x, g_pre, "norm_fwd")
    G, U, A = _ffn_gate_up(h, W["w%s_gate" % tag], W["w%s_up" % tag], "ffn_gate_up")
    W = fetch("ffn%sb" % tag, A)
    Y = _mm([(A, W["w%s_down" % tag])], "nn", F32, "ffn_down", bm=1024)
    return _resid_norm(x, Y, g_post, 0.5, "resid_norm"), dict(x=x, h=h, G=G, U=U, A=A, Y=Y)


def _ffn_bwd(dxo, sv, g_pre, g_post, wg, wu, wd_rows, ready):
    dY, dg_post = _norm_bwd(sv["Y"], g_post, dxo, 0.5, None, BF16, "norm_bwd_post")
    dG, dU = _ffn_bwd_mid(dY, wd_rows, sv["G"], sv["U"], "ffn_bwd_mid")
    cut = lambda t: t.reshape(N_CHIPS, -1, t.shape[1])
    token = ready((cut(_mm([(dG, sv["h"])], "tn", BF16, "ffn_dw_in")), cut(_mm([(dU, sv["h"])], "tn", BF16, "ffn_dw_in")),
                   cut(_mm([(sv["A"], dY)], "tn", BF16, "ffn_dw_down"))))
    dh = _mm([(dG, wg), (dU, wu)], "nn", F32, "ffn_dh")
    dx, dg_pre = _norm_bwd(sv["x"], _ordered_after(g_pre, token), dh, 1.0, dxo, F32, "norm_bwd_pre")
    return dx, dg_pre, dg_post, token


def _ordered_after(g, token):
    return g if token is None else g + token[0, 0]


GATHER_GROUPS = (("ffn1a", ("w1_gate", "w1_up")), ("ffn1b", ("w1_down",)),
                 ("mix", ("w_in", "w_mla_uq", "w_mla_ukv", "w_out")),
                 ("ffn2a", ("w2_gate", "w2_up", "w2_down", "w_ple_gate", "w_ple_proj")))
SCATTER_GROUPS = (("ffn2", ("w2_gate", "w2_up", "w2_down", "w_ple_gate", "w_ple_proj")),
                  ("mix", ("w_in", "w_mla_uq", "w_mla_ukv", "w_out")),
                  ("ffn1", ("w1_gate", "w1_up", "w1_down")))


def _layer_fwd(x, p_l, fetch, G, w_conv, cp, sp):
    sv = {}
    x1, sv["ffn1"] = _ffn_fwd(x, G["g_ffn1_pre"], G["g_ffn1_post"], fetch, "1")

    W = fetch("mix", x1)
    h2 = _norm_fwd(x1, G["g_mix_pre"], "norm_fwd")
    Z = _mm([(h2, W["w_in"])], "nt", F32, "mix_in")
    o_sb, tot_sb = _sb_fwd(Z)
    cq, ckv = Z[:, Z_Q:Z_Q + 384], Z[:, Z_KV:Z_KV + 256]
    nq = _norm_fwd(cq, G["g_mla_q"], "norm_fwd_q")
    Qf = _mm([(nq, W["w_mla_uq"])], "nt", F32, "mla_uq")
    nkv = _norm_fwd(ckv, G["g_mla_kv"], "norm_fwd_kv")
    KV = _mm([(nkv, W["w_mla_ukv"])], "nn", F32, "mla_ukv")
    qcat, kcat = _mla_prep(Qf, KV, Z, cp, sp)
    o_mla, lse = _mla_fwd(qcat, kcat, KV)
    y_cv = _conv_fwd(Z, w_conv)
    C = jnp.concatenate([o_sb, o_mla, y_cv], axis=1).astype(BF16)
    Mx = _mm([(C, W["w_out"])], "nn", F32, "mix_out")
    x2 = _resid_norm(x1, Mx, G["g_mix_post"], 1.0, "resid_norm")
    sv["mix"] = dict(x=x1, h=h2, Z=Z, tot_sb=tot_sb, cq=cq, ckv=ckv, nq=nq, nkv=nkv, qcat=qcat, kcat=kcat, KV=KV,
                     o_mla=o_mla, lse=lse, C=C, Mx=Mx)

    x3, sv["ffn2"] = _ffn_fwd(x2, G["g_ffn2_pre"], G["g_ffn2_post"], fetch, "2")
    W = fetch("ffn2b", x3)

    h4 = _norm_fwd(x3, G["g_ple_pre"], "norm_fwd")
    p16 = p_l.astype(BF16)
    Qg = _mm([(h4, W["w_ple_gate"])], "nn", F32, "ple_gate")
    Pp = _ple_proj(p16, W["w_ple_proj"])
    e = _ew(lambda q, pp: _sigmoid(q) * pp, [Qg, Pp], [(D_MODEL, F32)], "ple_mul")
    x4 = _resid_norm(x3, e, G["g_ple_post"], 1.0, "resid_norm")
    sv["ple"] = dict(x=x3, h=h4, p16=p16, Qg=Qg, Pp=Pp, e=e)
    sv["W"] = W
    return x4, sv


def _layer_bwd(dx4, sv, G, w_conv, cp, sp, emit, token):
    gg, gw, W = {}, {}, sv["W"]

    s = sv["ple"]
    de, gg["g_ple_post"] = _norm_bwd(s["e"], _ordered_after(G["g_ple_post"], token), dx4, 1.0, None, F32,
                                     "norm_bwd_e")

    def ple_bwd(dev, q, pp):
        sg = _sigmoid(q)
        return dev * pp * sg * (1.0 - sg), dev * sg

    dQg, dPp = _ew(ple_bwd, [de, s["Qg"], s["Pp"]], [(D_MODEL, BF16)] * 2, "ple_mul_bwd")
    cut = lambda t: t.reshape(N_CHIPS, -1, t.shape[1])
    gw["w_ple_proj"] = _dw_col_shards(s["p16"], dPp, "ple_dw_proj")
    gw["w_ple_gate"] = cut(_mm([(s["h"], dQg)], "tn", BF16, "ple_dw_gate"))
    dh4 = _mm([(dQg, W["w_ple_gate"])], "nt", F32, "ple_dh")
    dx3, gg["g_ple_pre"] = _norm_bwd(s["x"], G["g_ple_pre"], dh4, 1.0, dx4, F32, "norm_bwd_pre")

    def ready2(dws):
        gw["w2_gate"], gw["w2_up"], gw["w2_down"] = dws
        return emit("ffn2", gw)

    dx2, gg["g_ffn2_pre"], gg["g_ffn2_post"], token = _ffn_bwd(
        dx3, sv["ffn2"], G["g_ffn2_pre"], G["g_ffn2_post"], W["w2_gate"], W["w2_up"], W["w2_down"], ready2)

    s = sv["mix"]
    dM, gg["g_mix_post"] = _norm_bwd(s["Mx"], _ordered_after(G["g_mix_post"], token), dx2, 1.0, None, BF16,
                                     "norm_bwd_post")
    dC = _mm([(dM, W["w_out"])], "nt", F32, "mix_out_dx")
    gw["w_out"] = cut(_mm([(s["C"], dM)], "tn", BF16, "mix_out_dw"))

    db, dc, dhh, gg["w_conv"] = _conv_bwd(s["Z"], w_conv, dC)

    dqc, dkc, dv = _mla_bwd(s["qcat"], s["kcat"], s["KV"], s["o_mla"], s["lse"], dC)
    dQf, dkr = _mla_prep_bwd(dqc, dkc, cp, sp)
    gw["w_mla_uq"] = _ungroup_w_uq_t(_mm([(dQf, s["nq"])], "tn", BF16, "mla_uq_dw"))
    dnq = _mm([(dQf, W["w_mla_uq"])], "nn", F32, "mla_uq_dx")
    dcq, gg["g_mla_q"] = _norm_bwd(s["cq"], G["g_mla_q"], dnq, 1.0, None, F32, "norm_bwd_q")
    dkv = jnp.concatenate([dkc, dv], axis=1).astype(BF16)
    gw["w_mla_ukv"] = _cut_cols(_ungroup_w_ukv(_mm([(s["nkv"], dkv)], "tn", BF16, "mla_ukv_dw")))
    dnkv = _mm([(dkv, W["w_mla_ukv"])], "nt", F32, "mla_ukv_dx")
    dckv, gg["g_mla_kv"] = _norm_bwd(s["ckv"], G["g_mla_kv"], dnkv, 1.0, None, F32, "norm_bwd_kv")

    dsq, dsk, dsv = _sb_bwd(s["Z"], s["tot_sb"], dC)
    dZ = jnp.concatenate([dsq, dsk, dsv, dckv, dkr, dcq, db, dc, dhh], axis=1).astype(BF16)
    gw["w_in"] = _ungroup_w_in_t(_mm([(dZ, s["h"])], "tn", BF16, "mix_in_dw"))
    dh2 = _mm([(dZ, W["w_in"])], "nn", F32, "mix_in_dx")
    dx1, gg["g_mix_pre"] = _norm_bwd(s["x"], G["g_mix_pre"], dh2, 1.0, dx2, F32, "norm_bwd_pre")
    started_mix = emit("mix", gw)
    token = token if started_mix is None else started_mix

    def ready1(dws):
        gw["w1_gate"], gw["w1_up"], gw["w1_down"] = dws
        return emit("ffn1", gw)

    dx0, gg["g_ffn1_pre"], gg["g_ffn1_post"], started_ffn1 = _ffn_bwd(
        dx1, sv["ffn1"], G["g_ffn1_pre"], _ordered_after(G["g_ffn1_post"], token), W["w1_gate"], W["w1_up"],
        W["w1_down"], ready1)
    return dx0, gg, token if started_ffn1 is None else started_ffn1


def _pack_small(vecs):
    flat = jnp.concatenate([v.reshape(-1) for v in vecs])
    rows = -(-flat.shape[0] // (8 * LANES)) * 8
    return jnp.pad(flat, (0, rows * LANES - flat.shape[0])).reshape(rows, LANES)


def _group_weights(names, landed, shards, chip):
    out = {}
    for name, g, own in zip(names, landed, shards):
        g = lax.dynamic_update_slice(g, own[None], (chip, 0, 0))
        if name == "w_in":
            g = _regroup_w_in_t(g)
        elif name == "w_mla_uq":
            g = _regroup_w_uq_t(g)
        elif name not in ("w_mla_ukv", "w_ple_proj"):
            g = g.reshape(-1, g.shape[2])
        elif name == "w_mla_ukv":
            g = _regroup_w_ukv(_cat_cols(g))
        out[name] = g
    return out


def kernel(x, p, positions, g_ffn1_pre, w1_gate, w1_up, w1_down, g_ffn1_post, g_mix_pre, w_in, g_mla_q, w_mla_uq, g_mla_kv, w_mla_ukv, w_conv, w_out, g_mix_post, g_ffn2_pre, w2_gate, w2_up, w2_down, g_ffn2_post, g_ple_pre, w_ple_gate, w_ple_proj, g_ple_post, loss_target, m_g_ffn1_pre, m_w1_gate, m_w1_up, m_w1_down, m_g_ffn1_post, m_g_mix_pre, m_w_in, m_g_mla_q, m_w_mla_uq, m_g_mla_kv, m_w_mla_ukv, m_w_conv, m_w_out, m_g_mix_post, m_g_ffn2_pre, m_w2_gate, m_w2_up, m_w2_down, m_g_ffn2_post, m_g_ple_pre, m_w_ple_gate, m_w_ple_proj, m_g_ple_post, v_g_ffn1_pre, v_w1_gate, v_w1_up, v_w1_down, v_g_ffn1_post, v_g_mix_pre, v_w_in, v_g_mla_q, v_w_mla_uq, v_g_mla_kv, v_w_mla_ukv, v_w_conv, v_w_out, v_g_mix_post, v_g_ffn2_pre, v_w2_gate, v_w2_up, v_w2_down, v_g_ffn2_post, v_g_ple_pre, v_w_ple_gate, v_w_ple_proj, v_g_ple_post):
    w = dict(g_ffn1_pre=g_ffn1_pre, w1_gate=w1_gate, w1_up=w1_up, w1_down=w1_down, g_ffn1_post=g_ffn1_post,
             g_mix_pre=g_mix_pre, w_in=w_in, g_mla_q=g_mla_q, w_mla_uq=w_mla_uq, g_mla_kv=g_mla_kv,
             w_mla_ukv=w_mla_ukv, w_conv=w_conv, w_out=w_out, g_mix_post=g_mix_post, g_ffn2_pre=g_ffn2_pre,
             w2_gate=w2_gate, w2_up=w2_up, w2_down=w2_down, g_ffn2_post=g_ffn2_post, g_ple_pre=g_ple_pre,
             w_ple_gate=w_ple_gate, w_ple_proj=w_ple_proj, g_ple_post=g_ple_post)
    m = dict(g_ffn1_pre=m_g_ffn1_pre, w1_gate=m_w1_gate, w1_up=m_w1_up, w1_down=m_w1_down, g_ffn1_post=m_g_ffn1_post,
             g_mix_pre=m_g_mix_pre, w_in=m_w_in, g_mla_q=m_g_mla_q, w_mla_uq=m_w_mla_uq, g_mla_kv=m_g_mla_kv,
             w_mla_ukv=m_w_mla_ukv, w_conv=m_w_conv, w_out=m_w_out, g_mix_post=m_g_mix_post, g_ffn2_pre=m_g_ffn2_pre,
             w2_gate=m_w2_gate, w2_up=m_w2_up, w2_down=m_w2_down, g_ffn2_post=m_g_ffn2_post, g_ple_pre=m_g_ple_pre,
             w_ple_gate=m_w_ple_gate, w_ple_proj=m_w_ple_proj, g_ple_post=m_g_ple_post)
    v = dict(g_ffn1_pre=v_g_ffn1_pre, w1_gate=v_w1_gate, w1_up=v_w1_up, w1_down=v_w1_down, g_ffn1_post=v_g_ffn1_post,
             g_mix_pre=v_g_mix_pre, w_in=v_w_in, g_mla_q=v_g_mla_q, w_mla_uq=v_w_mla_uq, g_mla_kv=v_g_mla_kv,
             w_mla_ukv=v_w_mla_ukv, w_conv=v_w_conv, w_out=v_w_out, g_mix_post=v_g_mix_post, g_ffn2_pre=v_g_ffn2_pre,
             w2_gate=v_w2_gate, w2_up=v_w2_up, w2_down=v_w2_down, g_ffn2_post=v_g_ffn2_post, g_ple_pre=v_g_ple_pre,
             w_ple_gate=v_w_ple_gate, w_ple_proj=v_w_ple_proj, g_ple_post=v_g_ple_post)

    for name in TRANSPOSED:
        w[name], m[name], v[name] = (jnp.swapaxes(t[name], 1, 2) for t in (w, m, v))

    depth = g_ffn1_pre.shape[0]
    assert depth == 2, "the reduced gradients are assembled in [2, K, N] buffers"
    S = x.shape[1]
    cx, cy, cc = lax.axis_index("x"), lax.axis_index("y"), lax.axis_index("c")
    chip = 2 * cx + cy
    c_idx = cc.reshape(1).astype(jnp.int32)
    idx2 = jnp.stack([chip, cc]).astype(jnp.int32)

    conv_slot = lax.dynamic_update_slice(jnp.zeros((depth, 3, 256), F32),
                                         w_conv * (cc == 0).astype(F32), (0, 0, 64 * chip))
    conv_sum = _allreduce_small(_pack_small([conv_slot]), positions)
    w_conv_full = conv_sum[:depth * 3 * 256 // LANES].reshape(depth, 3, 256)

    def padded(name, i):
        (rows, _), rows_p = BIG[name]
        return jnp.pad(w[name][i].astype(BF16), ((0, rows_p - rows), (0, 0)))

    def groups_of(i, groups):
        return groups if i == 0 else (("all", BIG_NAMES),)

    started, after = {}, conv_sum
    for i in range(depth):
        for gname, names in groups_of(i, GATHER_GROUPS):
            shards = [padded(name, i) for name in names]
            lands = [lax.empty((N_CHIPS,) + s.shape, BF16) for s in shards]
            started[i, gname] = _exchange_start(_gather_copies, shards, lands, after, "gather_start_%d_%s" % (i, gname))
            after = started[i, gname][4]
    all_started = after

    inv = ROPE_BASE ** (-jnp.arange(ROPE_HALF, dtype=F32) / ROPE_HALF)
    zeros = lambda n: jnp.zeros((n,), F32)
    ones = jnp.ones((ROPE_HALF,), F32)
    inv_pat = jnp.concatenate([zeros(64), inv, inv, zeros(32)]).reshape(1, LANES)
    sign_pat = jnp.concatenate([zeros(64), -ones, ones, zeros(32)]).reshape(1, LANES)
    cp, sp = _rope_tables(positions.reshape(S, 1), inv_pat, sign_pat)

    def fetcher(i):
        table, have = dict(groups_of(i, GATHER_GROUPS)), {}

        def fetch(group, after):
            key = group if group in table else "all"
            if (i, key) in started:
                ssem, rsem, srcs, lands, _ = started.pop((i, key))
                if i == 0 and group == "ffn1a":
                    after = all_started
                srcs, landed = _exchange_wait(_gather_copies, ssem, rsem, srcs, lands, after,
                                              "gather_wait_%d_%s" % (i, key))
                have.update(_group_weights(table[key], _forward_rows(landed), srcs, chip))
            return have

        return fetch

    xs, saved = x[0], []
    gains = [{name: w[name][i].reshape(1, n) for name, n in GAINS} for i in range(depth)]
    for i in range(depth):
        xs, sv = _layer_fwd(xs, p[i, 0], fetcher(i), gains[i], w_conv_full[i], cp, sp)
        saved.append(sv)

    loss_part, dx = _loss_and_grad(xs, loss_target[0])
    loss = lax.psum(loss_part[0, 0], AXES)

    scattering = []

    def emitter(i):
        table = dict(groups_of(i, SCATTER_GROUPS))

        def emit(group, gw):
            if group in table:
                names = table[group]
            elif group == "ffn1":
                names = table["all"]
            else:
                return None
            full = [gw[name] for name in names]
            pair_sums = _add_rows(full, _swap_rows(full), c_idx)
            lands = [lax.empty((3,) + s.shape[1:], BF16) for s in pair_sums]
            st = _exchange_start(_scatter_copies, pair_sums, lands, c_idx, "scatter_start_%d_%s" % (i, group))
            scattering.append((i, group, names, st))
            return st[4]

        return emit

    ggs, token = [None] * depth, None
    for i in reversed(range(depth)):
        dx, ggs[i], token = _layer_bwd(dx, saved[i], gains[i], w_conv_full[i], cp, sp, emitter(i), token)

    fins, after = {name: None for name in BIG_NAMES}, dx
    for i, group, names, (ssem, rsem, srcs, lands, _) in scattering:
        srcs, arrived = _exchange_wait(_scatter_copies, ssem, rsem, srcs, lands, after,
                                       "scatter_wait_%d_%s" % (i, group))
        sums = _add_chip_sums(srcs, arrived, idx2, i, [fins[name] for name in names], after)
        fins.update(zip(names, sums))
        after = sums[0]
    joined = _join_rows([fins[name] for name in BIG_NAMES])
    reduced = dict(zip(BIG_NAMES, joined))

    small_names = [name for name, _ in GAINS] + ["w_conv"]
    small = _allreduce_small(_pack_small([jnp.stack([ggs[i][name] for i in range(depth)]) for name in small_names]),
                             joined[0])
    flat, off = small.reshape(-1), 0
    for name, n in GAINS:
        reduced[name] = flat[off:off + depth * n].reshape(depth, n)
        off += depth * n
    conv_full = flat[off:off + depth * 3 * 256].reshape(depth, 3, 256)
    reduced["w_conv"] = lax.dynamic_slice(conv_full, (0, 0, 64 * chip), (depth, 3, 64))

    grads, deltas, new_m, new_v = {}, {}, {}, {}
    for name in WEIGHT_ORDER:
        shape = w[name].shape
        three_d = shape if len(shape) == 3 else (1,) + shape
        g_ = reduced[name] if len(shape) == 3 else reduced[name].reshape(three_d)
        outs = _adamw(w[name].reshape(three_d), g_, m[name].reshape(three_d), v[name].reshape(three_d), "adamw")
        if name in TRANSPOSED:
            outs = [jnp.swapaxes(o, 1, 2) for o in outs]
        grads[name], deltas[name], new_m[name], new_v[name] = (o.reshape(o.shape[-len(shape):]) for o in outs)

    return (loss, dx[None], *[grads[n] for n in WEIGHT_ORDER], *[deltas[n] for n in WEIGHT_ORDER],
            *[new_m[n] for n in WEIGHT_ORDER], *[new_v[n] for n in WEIGHT_ORDER])
```

```python
import jax
import jax.numpy as jnp
from jax import lax
from jax.experimental import pallas as pl
from jax.experimental.pallas import tpu as pltpu

F32 = jnp.float32
BF16 = jnp.bfloat16
MESH = pl.DeviceIdType.MESH
AXES = ("x", "y", "c")

D_MODEL = 1024
N_CHIPS = 4
EPS = 1e-6
NEG_INF = -1e30
ROPE_BASE = 10000.0
ROPE_HALF = 16
LANES = 128
SB_HEADS, MLA_HEADS, HEAD_DIM = 4, 8, 64
MLA_SCALE = 96.0 ** -0.5
SB_SCALE = 64.0 ** -0.5
SB_BQ, SB_BK = 512, 128
MLA_BQ, MLA_BK = 256, 256
ADAM_LR, ADAM_B1, ADAM_B2, ADAM_EPS, ADAM_WD, ADAM_STEP = 0.001, 0.9, 0.999, 1e-08, 0.01, 10
VMEM_LIMIT = 48 * 2 ** 20

Z_SB, Z_KV, Z_KR, Z_Q, Z_CV, Z_W = 0, 768, 1024, 1152, 1536, 2304

NT = (((1,), (1,)), ((), ()))
TN = (((0,), (0,)), ((), ()))

TRANSPOSED = ("w1_gate", "w1_up", "w2_gate", "w2_up", "w_in", "w_mla_uq")
BIG = {"w1_gate": ((704, 1024), 704), "w1_up": ((704, 1024), 704), "w1_down": ((704, 1024), 704),
       "w_in": ((552, 1024), 576), "w_mla_uq": ((192, 384), 192), "w_mla_ukv": ((256, 256), 256),
       "w_out": ((256, 1024), 256),
       "w2_gate": ((704, 1024), 704), "w2_up": ((704, 1024), 704), "w2_down": ((704, 1024), 704),
       "w_ple_gate": ((256, 1024), 256), "w_ple_proj": ((256, 256), 256)}
BIG_NAMES = tuple(BIG)
GAINS = (("g_ffn1_pre", 1024), ("g_ffn1_post", 1024), ("g_mix_pre", 1024), ("g_mla_q", 384),
         ("g_mla_kv", 256), ("g_mix_post", 1024), ("g_ffn2_pre", 1024), ("g_ffn2_post", 1024),
         ("g_ple_pre", 1024), ("g_ple_post", 1024))
WEIGHT_ORDER = ("g_ffn1_pre", "w1_gate", "w1_up", "w1_down", "g_ffn1_post", "g_mix_pre", "w_in", "g_mla_q",
                "w_mla_uq", "g_mla_kv", "w_mla_ukv", "w_conv", "w_out", "g_mix_post", "g_ffn2_pre", "w2_gate",
                "w2_up", "w2_down", "g_ffn2_post", "g_ple_pre", "w_ple_gate", "w_ple_proj", "g_ple_post")

_pcall = pl.pallas_call


def _params(sem=None):
    return pltpu.CompilerParams(dimension_semantics=sem, vmem_limit_bytes=VMEM_LIMIT)


def _dot(a, b, dims=None):
    a, b = a.astype(BF16), b.astype(BF16)
    if dims is None:
        return jnp.dot(a, b, preferred_element_type=F32)
    return lax.dot_general(a, b, dims, preferred_element_type=F32)


def _rstd(v):
    return lax.rsqrt(jnp.mean(v * v, axis=-1, keepdims=True) + EPS)


def _sigmoid(v):
    return 0.5 * jnp.tanh(0.5 * v) + 0.5


def _row_block(n, want, mult=16):
    for b in range(min(n, want), 0, -1):
        if n % b == 0 and b % mult == 0:
            return b
    return n


def _mm(pairs, mode, out_dtype, name, bm=512, bn=512):
    a0, b0 = pairs[0]
    if mode == "nn":
        M, N = a0.shape[0], b0.shape[1]
    elif mode == "nt":
        M, N = a0.shape[0], b0.shape[0]
    else:
        M, N = a0.shape[1], b0.shape[1]
        bn = max(bn, 1024)
    bm, bn = _row_block(M, bm, 128 if mode == "tn" else 16), _row_block(N, bn, 128)
    n_pairs = len(pairs)
    dims = {"nn": None, "nt": NT, "tn": TN}[mode]

    def body(*refs):
        acc = None
        for t in range(n_pairs):
            part = _dot(refs[2 * t][...], refs[2 * t + 1][...], dims)
            acc = part if acc is None else acc + part
        refs[-1][...] = acc.astype(refs[-1].dtype)

    in_specs, ops = [], []
    for a, b in pairs:
        if mode == "nn":
            in_specs += [pl.BlockSpec((bm, a.shape[1]), lambda j, i: (i, 0)),
                         pl.BlockSpec((b.shape[0], bn), lambda j, i: (0, j))]
        elif mode == "nt":
            in_specs += [pl.BlockSpec((bm, a.shape[1]), lambda j, i: (i, 0)),
                         pl.BlockSpec((bn, b.shape[1]), lambda j, i: (j, 0))]
        else:
            in_specs += [pl.BlockSpec((a.shape[0], bm), lambda j, i: (0, i)),
                         pl.BlockSpec((b.shape[0], bn), lambda j, i: (0, j))]
        ops += [a, b]
    return _pcall(body, name=name, grid=(N // bn, M // bm), in_specs=in_specs,
                  out_specs=pl.BlockSpec((bm, bn), lambda j, i: (i, j)),
                  out_shape=jax.ShapeDtypeStruct((M, N), out_dtype),
                  compiler_params=_params(("parallel", "parallel")))(*ops)


def _dw_col_shards(a, b, name):
    S, K = a.shape
    Np = b.shape[1] // N_CHIPS

    def body(a_ref, b_ref, o_ref):
        o_ref[...] = _dot(a_ref[...], b_ref[...], TN).astype(BF16)

    return _pcall(body, name=name, grid=(N_CHIPS,),
                  in_specs=[pl.BlockSpec((S, K), lambda j: (0, 0)), pl.BlockSpec((S, Np), lambda j: (0, j))],
                  out_specs=pl.BlockSpec((None, K, Np), lambda j: (j, 0, 0)),
                  out_shape=jax.ShapeDtypeStruct((N_CHIPS, K, Np), BF16),
                  compiler_params=_params(("parallel",)))(a, b)


def _ew(fn, ins, outs, name, br=512):
    S = max(a.shape[0] for a in ins)
    br = _row_block(S, br)
    n_in = len(ins)

    def body(*refs):
        res = fn(*[r[...] for r in refs[:n_in]])
        if not isinstance(res, tuple):
            res = (res,)
        for r, v in zip(refs[n_in:], res):
            r[...] = v.astype(r.dtype)

    in_specs = [pl.BlockSpec((br, a.shape[1]), lambda i: (i, 0)) if a.shape[0] == S and S > 1
                else pl.BlockSpec(a.shape, lambda i: (0, 0)) for a in ins]
    out = _pcall(body, name=name, grid=(S // br,), in_specs=in_specs,
                 out_specs=tuple(pl.BlockSpec((br, w), lambda i: (i, 0)) for w, _ in outs),
                 out_shape=tuple(jax.ShapeDtypeStruct((S, w), dt) for w, dt in outs),
                 compiler_params=_params(("parallel",)))(*ins)
    return out if len(outs) > 1 else out[0]


def _norm_fwd(x, g, name):
    return _ew(lambda xv, gv: xv * _rstd(xv) * gv, [x, g], [(x.shape[1], BF16)], name, br=512)


def _resid_norm(x, y, g, alpha, name, g_next=None):
    W = x.shape[1]
    if g_next is None:
        return _ew(lambda xv, yv, gv: xv + alpha * (yv * _rstd(yv) * gv), [x, y, g], [(W, F32)], name), None

    def both(xv, yv, gv, gn):
        xn = xv + alpha * (yv * _rstd(yv) * gv)
        return xn, xn * _rstd(xn) * gn

    return _ew(both, [x, y, g, g_next], [(W, F32), (W, BF16)], name + "_next")


def _norm_bwd(xin, g, dy, alpha, resid, out_dtype, name):
    S, W = xin.shape
    br = _row_block(S, 512)
    has_res = resid is not None

    def body(*refs):
        x_ref, g_ref, dy_ref = refs[:3]
        dx_ref, dg_ref = refs[-2:]
        xv, dyv = x_ref[...], dy_ref[...] * alpha
        r = _rstd(xv)
        xh = xv * r
        u = dyv * g_ref[...]
        dx = r * (u - xh * jnp.mean(u * xh, axis=-1, keepdims=True))
        if has_res:
            dx = dx + refs[3][...]
        dx_ref[...] = dx.astype(dx_ref.dtype)
        part = jnp.sum(dyv * xh, axis=0, keepdims=True)

        @pl.when(pl.program_id(0) == 0)
        def _():
            dg_ref[...] = part

        @pl.when(pl.program_id(0) > 0)
        def _():
            dg_ref[...] += part

    row = pl.BlockSpec((br, W), lambda i: (i, 0))
    vec = pl.BlockSpec((1, W), lambda i: (0, 0))
    ops = [xin, g, dy] + ([resid] if has_res else [])
    return _pcall(body, name=name, grid=(S // br,), in_specs=[row, vec, row] + ([row] if has_res else []),
                  out_specs=(row, vec),
                  out_shape=(jax.ShapeDtypeStruct((S, W), out_dtype), jax.ShapeDtypeStruct((1, W), F32)),
                  compiler_params=_params(("arbitrary",)))(*ops)


def _ffn_gate_up(h, wgt, wut, name):
    S, K = h.shape
    F = wgt.shape[0]
    bm, bn = _row_block(S, 512), _row_block(F, 1408, 128)

    def body(h_ref, wg_ref, wu_ref, g_ref, u_ref, a_ref):
        hv = h_ref[...]
        g = _dot(hv, wg_ref[...], NT)
        u = _dot(hv, wu_ref[...], NT)
        g_ref[...] = g.astype(BF16)
        u_ref[...] = u.astype(BF16)
        a_ref[...] = (g * _sigmoid(g) * u).astype(BF16)

    blk = pl.BlockSpec((bm, bn), lambda n, i: (i, n))
    wsp = pl.BlockSpec((bn, K), lambda n, i: (n, 0))
    return _pcall(body, name=name, grid=(F // bn, S // bm),
                  in_specs=[pl.BlockSpec((bm, K), lambda n, i: (i, 0)), wsp, wsp],
                  out_specs=(blk, blk, blk), out_shape=(jax.ShapeDtypeStruct((S, F), BF16),) * 3,
                  compiler_params=_params(("parallel", "parallel")))(h, wgt, wut)


def _ffn_bwd_mid(dy, wd, g, u, name):
    S, D = dy.shape
    F = g.shape[1]
    bm, bn = _row_block(S, 512), _row_block(F, 1408, 128)

    def body(dy_ref, wd_ref, g_ref, u_ref, dg_ref, du_ref):
        da = _dot(dy_ref[...], wd_ref[...], NT)
        gv, uv = g_ref[...].astype(F32), u_ref[...].astype(F32)
        s = _sigmoid(gv)
        dg_ref[...] = (da * uv * (s * (1.0 + gv * (1.0 - s)))).astype(BF16)
        du_ref[...] = (da * (gv * s)).astype(BF16)

    blk = pl.BlockSpec((bm, bn), lambda j, i: (i, j))
    return _pcall(body, name=name, grid=(F // bn, S // bm),
                  in_specs=[pl.BlockSpec((bm, D), lambda j, i: (i, 0)),
                            pl.BlockSpec((bn, D), lambda j, i: (j, 0)), blk, blk],
                  out_specs=(blk, blk), out_shape=(jax.ShapeDtypeStruct((S, F), BF16),) * 2,
                  compiler_params=_params(("parallel", "parallel")))(dy, wd, g, u)


def _ple_proj(p16, w):
    S, K = p16.shape
    bm = _row_block(S, 1024)

    def body(p_ref, w_ref, o_ref):
        o_ref[...] = _dot(p_ref[...], w_ref[...])

    return _pcall(body, name="ple_proj", grid=(N_CHIPS, S // bm),
                  in_specs=[pl.BlockSpec((bm, K), lambda j, i: (i, 0)),
                            pl.BlockSpec((None, K, 256), lambda j, i: (j, 0, 0))],
                  out_specs=pl.BlockSpec((bm, 256), lambda j, i: (i, j)),
                  out_shape=jax.ShapeDtypeStruct((S, N_CHIPS * 256), F32),
                  compiler_params=_params(("parallel", "parallel")))(p16, w)


def _loss_and_grad(xf, tgt):
    S, W = xf.shape
    br = _row_block(S, 256)

    def body(x_ref, t_ref, l_ref, d_ref):
        d = x_ref[...] - t_ref[...]
        d_ref[...] = d * (1.0 / W)
        part = 0.5 * jnp.sum(jnp.sum(d * d, axis=-1, keepdims=True) * (1.0 / W), axis=0, keepdims=True)

        @pl.when(pl.program_id(0) == 0)
        def _():
            l_ref[...] = part

        @pl.when(pl.program_id(0) > 0)
        def _():
            l_ref[...] += part

    row = pl.BlockSpec((br, W), lambda i: (i, 0))
    return _pcall(body, name="loss", grid=(S // br,), in_specs=[row, row],
                  out_specs=(pl.BlockSpec((1, 1), lambda i: (0, 0)), row),
                  out_shape=(jax.ShapeDtypeStruct((1, 1), F32), jax.ShapeDtypeStruct((S, W), F32)),
                  compiler_params=_params(("arbitrary",)))(xf, tgt)


def _adamw(w, g, m, v, name):
    L, R, C = w.shape
    Cp = g.shape[2]
    br = _row_block(R, 256, 8) if R % 8 == 0 else R

    def body(w_ref, g_ref, m_ref, v_ref, go_ref, d_ref, nm_ref, nv_ref):
        gv = g_ref[...][:, :C]
        nm = ADAM_B1 * m_ref[...] + (1.0 - ADAM_B1) * gv
        nv = ADAM_B2 * v_ref[...] + (1.0 - ADAM_B2) * (gv * gv)
        m_hat = nm / (1.0 - ADAM_B1 ** ADAM_STEP)
        v_hat = nv / (1.0 - ADAM_B2 ** ADAM_STEP)
        go_ref[...] = gv
        d_ref[...] = -ADAM_LR * (m_hat / (jnp.sqrt(v_hat) + ADAM_EPS) + ADAM_WD * w_ref[...])
        nm_ref[...] = nm
        nv_ref[...] = nv

    blk = pl.BlockSpec((None, br, C), lambda l, i: (l, i, 0))
    gblk = pl.BlockSpec((None, br, Cp), lambda l, i: (l, i, 0))
    return _pcall(body, name=name, grid=(L, R // br), in_specs=[blk, gblk, blk, blk], out_specs=(blk,) * 4,
                  out_shape=(jax.ShapeDtypeStruct((L, R, C), F32),) * 4,
                  compiler_params=_params(("parallel", "parallel")))(w, g, m, v)


def _rope_tables(pos, inv_pat, sign_pat):
    def fn(p, iv, sg):
        ang = p.astype(F32) * iv
        return jnp.cos(ang), jnp.sin(ang) * sg

    return _ew(fn, [pos, inv_pat, sign_pat], [(LANES, F32)] * 2, "rope_tables")


def _swap_halves_of_rope(v):
    W = v.shape[1]
    lane = lax.broadcasted_iota(jnp.int32, (1, W), 1) % LANES
    return jnp.where((lane >= 64) & (lane < 80), pltpu.roll(v, W - ROPE_HALF, 1),
                     jnp.where((lane >= 80) & (lane < 96), pltpu.roll(v, ROPE_HALF, 1), 0.0))


def _tile_lanes(v, n):
    return jnp.concatenate([v] * n, axis=1)


def _mla_prep(qf, kv, z, cp, sp):
    S = qf.shape[0]
    br = _row_block(S, 256)
    W = MLA_HEADS * LANES

    def body(q_ref, k_ref, r_ref, c_ref, s_ref, qo_ref, ko_ref):
        c, s = c_ref[...], s_ref[...]
        q = q_ref[...]
        qo_ref[...] = (q * _tile_lanes(c, MLA_HEADS) + _swap_halves_of_rope(q) * _tile_lanes(s, MLA_HEADS)).astype(BF16)
        r = r_ref[...]
        lane = lax.broadcasted_iota(jnp.int32, (1, LANES), 1)
        kr = jnp.where((lane >= 64) & (lane < 96), r * c + _swap_halves_of_rope(r) * s, 0.0)
        ko_ref[...] = (k_ref[...] + _tile_lanes(kr, MLA_HEADS)).astype(BF16)

    wide = pl.BlockSpec((br, W), lambda i: (i, 0))
    one = pl.BlockSpec((br, LANES), lambda i: (i, 0))
    return _pcall(body, name="mla_prep", grid=(S // br,),
                  in_specs=[wide, wide, pl.BlockSpec((br, LANES), lambda i: (i, Z_KR // LANES)), one, one],
                  out_specs=(wide, wide), out_shape=(jax.ShapeDtypeStruct((S, W), BF16),) * 2,
                  compiler_params=_params(("parallel",)))(qf, kv, z, cp, sp)


def _mla_prep_bwd(dq, dk, cp, sp):
    S, W = dq.shape
    br = _row_block(S, 256)

    def body(dq_ref, dk_ref, c_ref, s_ref, dqo_ref, dr_ref):
        c, s = c_ref[...], s_ref[...]
        d = dq_ref[...]
        dqo_ref[...] = (d * _tile_lanes(c, MLA_HEADS) + _swap_halves_of_rope(d * _tile_lanes(s, MLA_HEADS))).astype(BF16)
        dkv = dk_ref[...]
        tot = dkv[:, 0:LANES]
        for h in range(1, MLA_HEADS):
            tot = tot + dkv[:, h * LANES:(h + 1) * LANES]
        lane = lax.broadcasted_iota(jnp.int32, (1, LANES), 1)
        tot = jnp.where((lane >= 64) & (lane < 96), tot, 0.0)
        dr_ref[...] = tot * c + _swap_halves_of_rope(tot * s)

    wide = pl.BlockSpec((br, W), lambda i: (i, 0))
    one = pl.BlockSpec((br, LANES), lambda i: (i, 0))
    return _pcall(body, name="mla_prep_bwd", grid=(S // br,), in_specs=[wide, wide, one, one], out_specs=(wide, one),
                  out_shape=(jax.ShapeDtypeStruct((S, W), BF16), jax.ShapeDtypeStruct((S, LANES), F32)),
                  compiler_params=_params(("parallel",)))(dq, dk, cp, sp)


def _shift_down(v, k, row):
    return jnp.where(row >= k, pltpu.roll(v, k, 0), 0.0)


def _shift_up(v, k, row):
    n = v.shape[0]
    return jnp.where(row < n - k, pltpu.roll(v, n - k, 0), 0.0)


def _conv_fwd(z, w):
    S = z.shape[0]
    c0 = Z_CV // LANES

    def body(b_ref, c_ref, h_ref, w_ref, y_ref):
        u = c_ref[...] * h_ref[...]
        row = lax.broadcasted_iota(jnp.int32, u.shape, 0)
        wv = w_ref[...]
        conv = wv[0:1] * _shift_down(u, 2, row) + wv[1:2] * _shift_down(u, 1, row) + wv[2:3] * u
        y_ref[...] = b_ref[...] * conv

    def col(k):
        return pl.BlockSpec((S, LANES), lambda j: (0, c0 + 2 * k + j))

    return _pcall(body, name="conv_fwd", grid=(2,),
                  in_specs=[col(0), col(1), col(2), pl.BlockSpec((3, LANES), lambda j: (0, j))],
                  out_specs=pl.BlockSpec((S, LANES), lambda j: (0, j)),
                  out_shape=jax.ShapeDtypeStruct((S, 256), F32), compiler_params=_params(("parallel",)))(z, z, z, w)


def _conv_bwd(z, w, dc_all):
    S = z.shape[0]
    c0 = Z_CV // LANES

    def body(b_ref, c_ref, h_ref, w_ref, dy_ref, db_ref, dc_ref, dh_ref, dw_ref):
        cv, hv, bv, dyv = c_ref[...], h_ref[...], b_ref[...], dy_ref[...]
        u = cv * hv
        row = lax.broadcasted_iota(jnp.int32, u.shape, 0)
        wv = w_ref[...]
        u1, u2 = _shift_down(u, 1, row), _shift_down(u, 2, row)
        conv = wv[0:1] * u2 + wv[1:2] * u1 + wv[2:3] * u
        db_ref[...] = dyv * conv
        dconv = dyv * bv
        du = wv[2:3] * dconv + wv[1:2] * _shift_up(dconv, 1, row) + wv[0:1] * _shift_up(dconv, 2, row)
        dc_ref[...] = du * hv
        dh_ref[...] = du * cv
        dw_ref[0:1, :] = jnp.sum(dconv * u2, axis=0, keepdims=True)
        dw_ref[1:2, :] = jnp.sum(dconv * u1, axis=0, keepdims=True)
        dw_ref[2:3, :] = jnp.sum(dconv * u, axis=0, keepdims=True)

    def col(k):
        return pl.BlockSpec((S, LANES), lambda j: (0, c0 + 2 * k + j))

    wsp = pl.BlockSpec((3, LANES), lambda j: (0, j))
    osp = pl.BlockSpec((S, LANES), lambda j: (0, j))
    db, dc, dh, dw = _pcall(
        body, name="conv_bwd", grid=(2,),
        in_specs=[col(0), col(1), col(2), wsp, pl.BlockSpec((S, LANES), lambda j: (0, 6 + j))],
        out_specs=(osp, osp, osp, wsp),
        out_shape=(jax.ShapeDtypeStruct((S, 256), F32),) * 3 + (jax.ShapeDtypeStruct((3, 256), F32),),
        compiler_params=_params(("parallel",)))(z, z, z, w, dc_all)
    return db, dc, dh, dw


def _half_masks():
    lane = lax.broadcasted_iota(jnp.int32, (1, LANES), 1)
    return lane < HEAD_DIM, lane >= HEAD_DIM


def _pair(v, p):
    return v[:, p * LANES:(p + 1) * LANES]


def _keep(mask, v):
    return jnp.where(mask, v, jnp.zeros_like(v))


def _lane_col(vals):
    lane = lax.broadcasted_iota(jnp.int32, (1, LANES), 1)
    out = jnp.zeros((vals[0].shape[0], LANES), F32)
    for h, v in enumerate(vals):
        out = jnp.where(lane == h, v, out)
    return out


def _mla_fwd(q, k, kv):
    S = q.shape[0]
    H = MLA_HEADS
    bq, bk = _row_block(S, MLA_BQ), _row_block(S, MLA_BK)
    per = bq // bk

    def body(q_ref, k_ref, v_ref, o_ref, lse_ref):
        i = pl.program_id(0)
        lo, hi = _half_masks()
        qb = q_ref[...]

        def block(j, carry, ok):
            ms, ls, accs = carry
            off = pl.multiple_of(j * bk, bk)
            kb = k_ref[pl.ds(off, bk), :]
            vb = v_ref[pl.ds(off, bk), :].astype(BF16)
            ss = [_dot(_pair(qb, h), _pair(kb, h), NT) * MLA_SCALE for h in range(H)]
            if ok is not None:
                ss = [jnp.where(ok, s, NEG_INF) for s in ss]
            ms2 = [jnp.maximum(ms[h], jnp.max(ss[h], axis=-1, keepdims=True)) for h in range(H)]
            ps = [jnp.exp(ss[h] - ms2[h]) for h in range(H)]
            al = [jnp.exp(ms[h] - ms2[h]) for h in range(H)]
            ls2 = [al[h] * ls[h] + jnp.sum(ps[h], axis=-1, keepdims=True) for h in range(H)]
            pvs = [_dot(ps[h], _pair(vb, h // 2)) for h in range(H)]
            accs2 = []
            for p in range(H // 2):
                scale = jnp.where(lo, al[2 * p], al[2 * p + 1])
                accs2.append(scale * accs[p] + jnp.where(lo, pvs[2 * p], pvs[2 * p + 1]))
            return tuple(ms2), tuple(ls2), tuple(accs2)

        init = (tuple(jnp.full((bq, 1), NEG_INF, F32) for _ in range(H)),
                tuple(jnp.zeros((bq, 1), F32) for _ in range(H)),
                tuple(jnp.zeros((bq, LANES), F32) for _ in range(H // 2)))
        carry = lax.fori_loop(0, i * per, lambda j, c: block(j, c, None), init)
        row = lax.broadcasted_iota(jnp.int32, (bq, bk), 0)
        col = lax.broadcasted_iota(jnp.int32, (bq, bk), 1)
        for t in range(per):
            carry = block(i * per + t, carry, col + t * bk <= row)
        ms, ls, accs = carry
        o_ref[...] = jnp.concatenate(
            [accs[p] / jnp.where(lo, ls[2 * p], ls[2 * p + 1]) for p in range(H // 2)], axis=1)
        lse_ref[...] = _lane_col([ms[h] + jnp.log(ls[h]) for h in range(H)])

    return _pcall(body, name="mla_fwd", grid=(S // bq,),
                  in_specs=[pl.BlockSpec((bq, H * LANES), lambda i: (i, 0)),
                            pl.BlockSpec((S, H * LANES), lambda i: (0, 0)),
                            pl.BlockSpec((S, H * HEAD_DIM), lambda i: (0, 2))],
                  out_specs=(pl.BlockSpec((bq, H * HEAD_DIM), lambda i: (i, 0)),
                             pl.BlockSpec((bq, LANES), lambda i: (i, 0))),
                  out_shape=(jax.ShapeDtypeStruct((S, H * HEAD_DIM), F32), jax.ShapeDtypeStruct((S, LANES), F32)),
                  compiler_params=_params(("parallel",)))(q, k, kv)


def _mla_bwd(q, k, kv, o, lse, dc_all):
    S = q.shape[0]
    H = MLA_HEADS
    bq, bk = _row_block(S, MLA_BQ), _row_block(S, MLA_BK)
    per = bq // bk

    def body(q_ref, k_ref, v_ref, o_ref, lse_ref, doa_ref, dob_ref, dq_ref, dk_ref, dv_ref):
        i = pl.program_id(0)

        @pl.when(i == 0)
        def _():
            dk_ref[...] = jnp.zeros_like(dk_ref)
            dv_ref[...] = jnp.zeros_like(dv_ref)

        lo, hi = _half_masks()
        qb = q_ref[...]
        dob = jnp.concatenate([doa_ref[...], dob_ref[...]], axis=1)
        prod = dob * o_ref[...]
        lse_v = lse_ref[...]
        do16 = dob.astype(BF16)
        dom, deltas, lses = [], [], []
        for h in range(H):
            mk = lo if h % 2 == 0 else hi
            dom.append(_keep(mk, _pair(do16, h // 2)))
            deltas.append(jnp.sum(_keep(mk, _pair(prod, h // 2)), axis=-1, keepdims=True))
            lses.append(lse_v[:, h:h + 1])

        def block(j, dqs, ok):
            off = pl.multiple_of(j * bk, bk)
            kb = k_ref[pl.ds(off, bk), :]
            vb = v_ref[pl.ds(off, bk), :].astype(BF16)
            ss = [_dot(_pair(qb, h), _pair(kb, h), NT) * MLA_SCALE for h in range(H)]
            dps = [_dot(dom[h], _pair(vb, h // 2), NT) for h in range(H)]
            ps = [jnp.exp(ss[h] - lses[h]) for h in range(H)]
            if ok is not None:
                ps = [jnp.where(ok, p, 0.0) for p in ps]
            ds16 = [(ps[h] * (dps[h] - deltas[h]) * MLA_SCALE).astype(BF16) for h in range(H)]
            p16 = [p.astype(BF16) for p in ps]
            dks = [_dot(ds16[h], _pair(qb, h), TN) for h in range(H)]
            dvps = [_dot(p16[h], dom[h], TN) for h in range(H)]
            dqs2 = [dqs[h] + _dot(ds16[h], _pair(kb, h)) for h in range(H)]
            dk_ref[pl.ds(off, bk), :] += jnp.concatenate(dks, axis=1)
            dv_ref[pl.ds(off, bk), :] += jnp.concatenate(
                [dvps[2 * p] + dvps[2 * p + 1] for p in range(H // 2)], axis=1)
            return tuple(dqs2)

        dqs = lax.fori_loop(0, i * per, lambda j, c: block(j, c, None),
                            tuple(jnp.zeros((bq, LANES), F32) for _ in range(H)))
        row = lax.broadcasted_iota(jnp.int32, (bq, bk), 0)
        col = lax.broadcasted_iota(jnp.int32, (bq, bk), 1)
        for t in range(per):
            dqs = block(i * per + t, dqs, col + t * bk <= row)
        dq_ref[...] = jnp.concatenate(dqs, axis=1)

    wide = pl.BlockSpec((bq, H * LANES), lambda i: (i, 0))
    full = pl.BlockSpec((S, H * LANES), lambda i: (0, 0))
    return _pcall(body, name="mla_bwd", grid=(S // bq,),
                  in_specs=[wide, full, pl.BlockSpec((S, H * HEAD_DIM), lambda i: (0, 2)),
                            pl.BlockSpec((bq, H * HEAD_DIM), lambda i: (i, 0)),
                            pl.BlockSpec((bq, LANES), lambda i: (i, 0)),
                            pl.BlockSpec((bq, 256), lambda i: (i, 1)), pl.BlockSpec((bq, 256), lambda i: (i, 2))],
                  out_specs=(wide, full, pl.BlockSpec((S, H * HEAD_DIM), lambda i: (0, 0))),
                  out_shape=(jax.ShapeDtypeStruct((S, H * LANES), F32), jax.ShapeDtypeStruct((S, H * LANES), F32),
                             jax.ShapeDtypeStruct((S, H * HEAD_DIM), F32)),
                  compiler_params=_params(("arbitrary",)))(q, k, kv, o, lse, dc_all, dc_all)


def _dot_exact(x, tri):
    h1 = x.astype(BF16)
    h2 = (x - h1.astype(F32)).astype(BF16)
    return _dot(h1, tri) + _dot(h2, tri)


def _softplus(z):
    return jnp.maximum(z, 0.0) + jnp.log(1.0 + jnp.exp(-jnp.abs(z)))


def _sb_fwd(z):
    S = z.shape[0]
    H = SB_HEADS
    bq, bk = _row_block(S, SB_BQ), _row_block(S, SB_BK)
    per = bq // bk

    def body(q_ref, k_ref, v_ref, o_ref, t_ref):
        i = pl.program_id(0)
        lo, hi = _half_masks()
        q16 = (q_ref[...] * SB_SCALE).astype(BF16)
        qm = [_keep(lo if h % 2 == 0 else hi, _pair(q16, h // 2)) for h in range(H)]
        rr = lax.broadcasted_iota(jnp.int32, (bk, bk), 0)
        cc = lax.broadcasted_iota(jnp.int32, (bk, bk), 1)
        later = (rr > cc).astype(BF16)

        def block(j, carry, mask):
            tails, accs = carry
            off = pl.multiple_of(j * bk, bk)
            kb = k_ref[pl.ds(off, bk), :].astype(BF16)
            vb = v_ref[pl.ds(off, bk), :].astype(BF16)
            zs = [_dot(qm[h], _pair(kb, h // 2), NT) for h in range(H)]
            sps = [_softplus(z) for z in zs]
            lnots = [-sp if mask is None else jnp.where(mask, -sp, 0.0) for sp in sps]
            sums = [_dot_exact(lnot, later) for lnot in lnots]
            ws = []
            for h in range(H):
                w = jnp.exp((zs[h] - sps[h]) + (tails[h] + sums[h]))
                ws.append((w if mask is None else jnp.where(mask, w, 0.0)).astype(BF16))
            pvs = [_dot(ws[h], _pair(vb, h // 2)) for h in range(H)]
            accs2 = [accs[p] + jnp.where(lo, pvs[2 * p], pvs[2 * p + 1]) for p in range(H // 2)]
            tails2 = [tails[h] + jnp.sum(lnots[h], axis=-1, keepdims=True) for h in range(H)]
            return tuple(tails2), tuple(accs2)

        carry = (tuple(jnp.zeros((bq, 1), F32) for _ in range(H)),
                 tuple(jnp.zeros((bq, LANES), F32) for _ in range(H // 2)))
        row = lax.broadcasted_iota(jnp.int32, (bq, bk), 0)
        col = lax.broadcasted_iota(jnp.int32, (bq, bk), 1)
        for t in range(per):
            carry = block(i * per + per - 1 - t, carry, col + (per - 1 - t) * bk < row)
        tails, accs = lax.fori_loop(0, i * per, lambda t, c: block(i * per - 1 - t, c, None), carry)
        o_ref[...] = jnp.concatenate(accs, axis=1)
        t_ref[...] = _lane_col(tails)

    return _pcall(body, name="sb_fwd", grid=(S // bq,),
                  in_specs=[pl.BlockSpec((bq, 256), lambda i: (i, 0)), pl.BlockSpec((S, 256), lambda i: (0, 1)),
                            pl.BlockSpec((S, 256), lambda i: (0, 2))],
                  out_specs=(pl.BlockSpec((bq, 256), lambda i: (i, 0)), pl.BlockSpec((bq, LANES), lambda i: (i, 0))),
                  out_shape=(jax.ShapeDtypeStruct((S, 256), F32), jax.ShapeDtypeStruct((S, LANES), F32)),
                  compiler_params=_params(("parallel",)))(z, z, z)


def _sb_bwd(z, tot, dc_all):
    S = z.shape[0]
    H = SB_HEADS
    bq, bk = _row_block(S, SB_BQ), _row_block(S, SB_BK)
    per = bq // bk

    def body(q_ref, k_ref, v_ref, t_ref, do_ref, dq_ref, dk_ref, dv_ref):
        i = pl.program_id(0)

        @pl.when(i == 0)
        def _():
            dk_ref[...] = jnp.zeros_like(dk_ref)
            dv_ref[...] = jnp.zeros_like(dv_ref)

        lo, hi = _half_masks()
        q16 = (q_ref[...] * SB_SCALE).astype(BF16)
        do16 = do_ref[...].astype(BF16)
        tot_v = t_ref[...]
        masks = [lo if h % 2 == 0 else hi for h in range(H)]
        qm = [_keep(masks[h], _pair(q16, h // 2)) for h in range(H)]
        dom = [_keep(masks[h], _pair(do16, h // 2)) for h in range(H)]
        tots = [tot_v[:, h:h + 1] for h in range(H)]
        rr = lax.broadcasted_iota(jnp.int32, (bk, bk), 0)
        cc = lax.broadcasted_iota(jnp.int32, (bk, bk), 1)
        upto = (rr <= cc).astype(BF16)
        before = (rr < cc).astype(BF16)

        def block(j, carry, mask):
            heads, eheads, dqs = carry
            off = pl.multiple_of(j * bk, bk)
            kb = k_ref[pl.ds(off, bk), :].astype(BF16)
            vb = v_ref[pl.ds(off, bk), :].astype(BF16)
            zs = [_dot(qm[h], _pair(kb, h // 2), NT) for h in range(H)]
            dws = [_dot(dom[h], _pair(vb, h // 2), NT) for h in range(H)]
            sps = [_softplus(z) for z in zs]
            lnots = [-sp if mask is None else jnp.where(mask, -sp, 0.0) for sp in sps]
            pres = [_dot_exact(lnot, upto) for lnot in lnots]
            lsigs = [zs[h] - sps[h] for h in range(H)]
            ws = [jnp.exp(lsigs[h] + (tots[h] - (heads[h] + pres[h]))) for h in range(H)]
            if mask is not None:
                ws = [jnp.where(mask, w, 0.0) for w in ws]
            es = [ws[h] * dws[h] for h in range(H)]
            esums = [eheads[h] + _dot_exact(es[h], before) for h in range(H)]
            dz16 = []
            for h in range(H):
                sig = jnp.exp(lsigs[h])
                dz = es[h] * (1.0 - sig) - sig * esums[h]
                dz16.append((dz if mask is None else jnp.where(mask, dz, 0.0)).astype(BF16))
            w16 = [w.astype(BF16) for w in ws]
            dkps = [_dot(dz16[h], qm[h], TN) for h in range(H)]
            dvps = [_dot(w16[h], dom[h], TN) for h in range(H)]
            dqps = [_dot(dz16[h], _pair(kb, h // 2)) for h in range(H)]
            dk_ref[pl.ds(off, bk), :] += jnp.concatenate([dkps[2 * p] + dkps[2 * p + 1] for p in range(H // 2)], axis=1)
            dv_ref[pl.ds(off, bk), :] += jnp.concatenate([dvps[2 * p] + dvps[2 * p + 1] for p in range(H // 2)], axis=1)
            dqs2 = [dqs[p] + jnp.where(lo, dqps[2 * p], dqps[2 * p + 1]) for p in range(H // 2)]
            heads2 = [heads[h] + jnp.sum(lnots[h], axis=-1, keepdims=True) for h in range(H)]
            eheads2 = [eheads[h] + jnp.sum(es[h], axis=-1, keepdims=True) for h in range(H)]
            return tuple(heads2), tuple(eheads2), tuple(dqs2)

        zeros = tuple(jnp.zeros((bq, 1), F32) for _ in range(H))
        init = (zeros, zeros, tuple(jnp.zeros((bq, LANES), F32) for _ in range(H // 2)))
        carry = lax.fori_loop(0, i * per, lambda j, c: block(j, c, None), init)
        row = lax.broadcasted_iota(jnp.int32, (bq, bk), 0)
        col = lax.broadcasted_iota(jnp.int32, (bq, bk), 1)
        for t in range(per):
            carry = block(i * per + t, carry, col + t * bk < row)
        dq_ref[...] = jnp.concatenate(carry[2], axis=1) * SB_SCALE

    blk = pl.BlockSpec((bq, 256), lambda i: (i, 0))
    full = pl.BlockSpec((S, 256), lambda i: (0, 0))
    return _pcall(body, name="sb_bwd", grid=(S // bq,),
                  in_specs=[blk, pl.BlockSpec((S, 256), lambda i: (0, 1)), pl.BlockSpec((S, 256), lambda i: (0, 2)),
                            pl.BlockSpec((bq, LANES), lambda i: (i, 0)), blk],
                  out_specs=(blk, full, full), out_shape=(jax.ShapeDtypeStruct((S, 256), F32),) * 3,
                  compiler_params=_params(("arbitrary",)))(z, z, z, tot, dc_all)


_HBM = pl.BlockSpec(memory_space=pltpu.HBM)
_SEM = pl.BlockSpec(memory_space=pltpu.SEMAPHORE)
_ANY = pl.BlockSpec(memory_space=pl.ANY)


def _place():
    x, y, c = lax.axis_index("x"), lax.axis_index("y"), lax.axis_index("c")
    other_chips = ((1 - x, y), (x, 1 - y), (1 - x, 1 - y))
    return x, y, c, other_chips


def _remote(src, dst, send_sems, recv_sems, k, dev):
    return pltpu.make_async_remote_copy(src_ref=src, dst_ref=dst, send_sem=send_sems.at[k], recv_sem=recv_sems.at[k],
                                        device_id=dev, device_id_type=MESH)


def _half_rows(c, n_rows):
    half = n_rows // 2
    return pl.ds(pl.multiple_of(c * half, 16), half)


def _gather_copies(src, land, send_sems, recv_sems):
    x, y, c, chips = _place()
    me = 2 * x + y
    cps = []
    for w in range(len(src)):
        rows = _half_rows(c, src[w].shape[0])
        for k, (cx, cy) in enumerate(chips):
            cps.append(_remote(src[w].at[rows], land[w].at[me, rows], send_sems, recv_sems, 3 * w + k, (cx, cy, c)))
    return cps


def _scatter_copies(src, land, send_sems, recv_sems):
    x, y, c, chips = _place()
    cps = []
    for w in range(len(src)):
        for k, (cx, cy) in enumerate(chips):
            cps.append(_remote(src[w].at[2 * cx + cy], land[w].at[k], send_sems, recv_sems, 3 * w + k, (cx, cy, c)))
    return cps


def _exchange_start(copies, srcs, lands, after, name):
    n = len(srcs)

    def body(*refs):
        src, land = refs[:n], refs[n:2 * n]
        send_sems, recv_sems = refs[2 * n + 1], refs[2 * n + 2]
        token = refs[-1]
        for cp in copies(src, land, send_sems, recv_sems):
            cp.start()
        token[...] = jnp.zeros_like(token)

    thru = tuple(pltpu.HBM(a.shape, a.dtype) for a in list(srcs) + list(lands))
    out = _pcall(body, name=name,
                 out_shape=(pltpu.SemaphoreType.DMA((3 * n,)), pltpu.SemaphoreType.DMA((3 * n,))) + thru
                 + (jax.ShapeDtypeStruct((8, LANES), F32),),
                 in_specs=[_HBM] * (2 * n) + [_ANY],
                 out_specs=(_SEM, _SEM) + (_HBM,) * (2 * n) + (pl.BlockSpec(memory_space=pltpu.VMEM),),
                 input_output_aliases={i: 2 + i for i in range(2 * n)},
                 compiler_params=pltpu.CompilerParams(has_side_effects=pltpu.SideEffectType.DATAFLOW_SIDE_EFFECTING))(
        *[pltpu.with_memory_space_constraint(a, pltpu.HBM) for a in list(srcs) + list(lands)], after)
    return out[0], out[1], list(out[2:2 + n]), list(out[2 + n:2 + 2 * n]), out[-1]


def _exchange_wait(copies, send_sems, recv_sems, srcs, lands, after, name):
    n = len(srcs)

    def body(*refs):
        src, land = refs[:n], refs[n:2 * n]
        ssem, rsem = refs[2 * n], refs[2 * n + 1]
        for cp in copies(src, land, ssem, rsem):
            cp.wait_send()
            cp.wait_recv()

    out = _pcall(body, name=name, out_shape=tuple(pltpu.HBM(a.shape, a.dtype) for a in list(srcs) + list(lands)),
                 in_specs=[_HBM] * (2 * n) + [_SEM, _SEM, _ANY], out_specs=(_HBM,) * (2 * n),
                 input_output_aliases={i: i for i in range(2 * n)},
                 compiler_params=pltpu.CompilerParams(has_side_effects=pltpu.SideEffectType.DATAFLOW_SIDE_EFFECTING))(
        *srcs, *lands, send_sems, recv_sems, after)
    return list(out[:n]), list(out[n:])


def _forward_rows(gathered):
    n = len(gathered)

    def body(*refs):
        src, dst = refs[:n], refs[n:2 * n]
        send_sems, recv_sems = refs[2 * n:]
        x, y, c, chips = _place()
        sib = (x, y, 1 - c)
        cps = []
        for w in range(n):
            K = src[w].shape[1]
            for k, (cx, cy) in enumerate(chips):
                blk = (2 * cx + cy, _half_rows(c, K))
                cps.append(_remote(src[w].at[blk], dst[w].at[blk], send_sems, recv_sems, 3 * w + k, sib))
        for cp in cps:
            cp.start()
        for w in range(n):
            K = src[w].shape[1]
            for k, (cx, cy) in enumerate(chips):
                blk = dst[w].at[2 * cx + cy, _half_rows(1 - c, K)]
                _remote(blk, blk, send_sems, recv_sems, 3 * w + k, sib).wait_recv()
        for cp in cps:
            cp.wait_send()

    return _pcall(body, name="forward_rows", in_specs=[_HBM] * n, out_specs=(_HBM,) * n,
                  out_shape=tuple(jax.ShapeDtypeStruct(g.shape, g.dtype) for g in gathered),
                  input_output_aliases={w: w for w in range(n)},
                  scratch_shapes=[pltpu.SemaphoreType.DMA((3 * n,)), pltpu.SemaphoreType.DMA((3 * n,))])(*gathered)


def _swap_rows(bufs):
    n = len(bufs)

    def body(*refs):
        src, dst = refs[:n], refs[n:2 * n]
        send_sems, recv_sems = refs[2 * n:]
        x, y, c, _ = _place()
        cps = [_remote(src[w].at[:, _half_rows(1 - c, src[w].shape[1])], dst[w], send_sems, recv_sems, w,
                       (x, y, 1 - c)) for w in range(n)]
        for cp in cps:
            cp.start()
        for cp in cps:
            cp.wait()

    return _pcall(body, name="swap_rows", in_specs=[_HBM] * n, out_specs=(_HBM,) * n,
                  out_shape=tuple(jax.ShapeDtypeStruct((b.shape[0], b.shape[1] // 2, b.shape[2]), b.dtype) for b in bufs),
                  scratch_shapes=[pltpu.SemaphoreType.DMA((n,)), pltpu.SemaphoreType.DMA((n,))])(*bufs)


def _join_rows(fins):
    n = len(fins)

    def body(*refs):
        src, dst = refs[:n], refs[n:2 * n]
        send_sems, recv_sems = refs[2 * n:]
        x, y, c, _ = _place()
        sib = (x, y, 1 - c)
        cps = []
        for w in range(n):
            blk = (slice(None), _half_rows(c, src[w].shape[1]))
            cps.append(_remote(src[w].at[blk], dst[w].at[blk], send_sems, recv_sems, w, sib))
        for cp in cps:
            cp.start()
        for w in range(n):
            blk = dst[w].at[:, _half_rows(1 - c, src[w].shape[1])]
            _remote(blk, blk, send_sems, recv_sems, w, sib).wait_recv()
        for cp in cps:
            cp.wait_send()

    return _pcall(body, name="join_rows", in_specs=[_HBM] * n, out_specs=(_HBM,) * n,
                  out_shape=tuple(jax.ShapeDtypeStruct(f.shape, f.dtype) for f in fins),
                  input_output_aliases={w: w for w in range(n)},
                  scratch_shapes=[pltpu.SemaphoreType.DMA((n,)), pltpu.SemaphoreType.DMA((n,))])(*fins)


def _add_rows(bufs, gots, c_idx):
    n = len(bufs)
    bks = [_row_block(g.shape[1], 192) for g in gots]
    nbs = [g.shape[1] // bk for g, bk in zip(gots, bks)]

    def body(c_ref, *refs):
        i = pl.program_id(1)
        for w in range(n):
            @pl.when(i < nbs[w])
            def _(w=w):
                refs[2 * n + w][...] = (refs[w][...].astype(F32) + refs[n + w][...].astype(F32)).astype(BF16)

    def mine(w):
        return pl.BlockSpec((None, bks[w], gots[w].shape[2]),
                            lambda j, i, c_ref: (j, c_ref[0] * nbs[w] + jnp.minimum(i, nbs[w] - 1), 0))

    def half(w):
        return pl.BlockSpec((None, bks[w], gots[w].shape[2]), lambda j, i, c_ref: (j, jnp.minimum(i, nbs[w] - 1), 0))

    grid_spec = pltpu.PrefetchScalarGridSpec(
        num_scalar_prefetch=1, grid=(N_CHIPS, max(nbs)),
        in_specs=[mine(w) for w in range(n)] + [half(w) for w in range(n)], out_specs=[half(w) for w in range(n)])
    return _pcall(body, name="add_rows", grid_spec=grid_spec,
                  out_shape=[jax.ShapeDtypeStruct(g.shape, BF16) for g in gots],
                  compiler_params=_params(("parallel", "arbitrary")))(c_idx, *bufs, *gots)


def _add_chip_sums(s1s, gots, idx, layer, intos, after):
    n = len(s1s)
    bks = [_row_block(a.shape[1], 64) for a in s1s]
    nbs = [a.shape[1] // bk for a, bk in zip(s1s, bks)]
    fresh = intos[0] is None

    def body(idx_ref, *refs):
        i = pl.program_id(0)
        outs = refs[len(refs) - n:]
        for w in range(n):
            @pl.when(i < nbs[w])
            def _(w=w):
                acc = refs[w][...].astype(F32)
                for k in range(3):
                    acc = acc + refs[n + w][k].astype(F32)
                outs[w][...] = acc

    def blk(w, rows):
        return pl.BlockSpec((rows, bks[w], s1s[w].shape[2]),
                            lambda i, idx_ref: (0 if rows else idx_ref[0], jnp.minimum(i, nbs[w] - 1), 0))

    def out(w):
        return pl.BlockSpec((None, bks[w], s1s[w].shape[2]),
                            lambda i, idx_ref: (layer, idx_ref[1] * nbs[w] + jnp.minimum(i, nbs[w] - 1), 0))

    ops = [idx, *s1s, *gots, after] + ([] if fresh else list(intos))
    in_specs = [blk(w, None) for w in range(n)] + [blk(w, 3) for w in range(n)] + [_ANY] * (1 if fresh else 1 + n)
    grid_spec = pltpu.PrefetchScalarGridSpec(num_scalar_prefetch=1, grid=(max(nbs),), in_specs=in_specs,
                                             out_specs=[out(w) for w in range(n)])
    return _pcall(body, name="add_chip_sums" if fresh else "add_chip_sums_into", grid_spec=grid_spec,
                  out_shape=[jax.ShapeDtypeStruct((2, 2 * a.shape[1], a.shape[2]), F32) for a in s1s],
                  input_output_aliases={} if fresh else {2 * n + 2 + w: w for w in range(n)},
                  compiler_params=_params(("arbitrary",)))(*ops)


def _allreduce_small(v, after):
    R, W = v.shape

    def body(v_ref, after_ref, o_ref, buf, send_sems, recv_sems):
        x, y, c, _ = _place()
        me = 4 * x + 2 * y + c
        buf[0] = v_ref[...]
        cps = []
        for r in range(1, 8):
            peer = (x if not r & 4 else 1 - x, y if not r & 2 else 1 - y, c if not r & 1 else 1 - c)
            cp = _remote(v_ref, buf.at[r], send_sems, recv_sems, r - 1, peer)
            cp.start()
            cps.append(cp)
        for cp in cps:
            cp.wait()
        acc = buf[me]
        for d in range(1, 8):
            acc = acc + buf[jnp.bitwise_xor(me, d)]
        o_ref[...] = acc

    return _pcall(body, name="allreduce_small", out_shape=jax.ShapeDtypeStruct((R, W), F32),
                  in_specs=[pl.BlockSpec(memory_space=pltpu.VMEM), _ANY],
                  out_specs=pl.BlockSpec(memory_space=pltpu.VMEM),
                  scratch_shapes=[pltpu.VMEM((8, R, W), F32), pltpu.SemaphoreType.DMA((7,)),
                                  pltpu.SemaphoreType.DMA((7,))])(v, after)


def _cat_cols(g):
    return g.transpose(1, 0, 2).reshape(g.shape[1], -1)


def _cut_cols(w):
    K, N = w.shape
    return w.reshape(K, N_CHIPS, N // N_CHIPS).transpose(1, 0, 2)


def _regroup_w_in_t(g):
    w = g[:, :552].reshape(2208, g.shape[2])
    zeros = lambda n: jnp.zeros((n, w.shape[1]), w.dtype)
    return jnp.concatenate([w[0:768], w[1152:1408], zeros(64), w[1408:1440], zeros(32), w[768:1152], w[1440:2208]],
                           axis=0)


def _ungroup_w_in_t(w):
    nat = jnp.concatenate([w[0:768], w[Z_Q:Z_Q + 384], w[Z_KV:Z_KV + 256], w[Z_KR + 64:Z_KR + 96], w[Z_CV:Z_W]],
                          axis=0)
    return jnp.pad(nat.reshape(N_CHIPS, 552, w.shape[1]), ((0, 0), (0, 24), (0, 0)))


def _regroup_w_uq_t(g):
    K = g.shape[2]
    return jnp.pad(g.reshape(MLA_HEADS, 96, K), ((0, 0), (0, 32), (0, 0))).reshape(MLA_HEADS * LANES, K)


def _ungroup_w_uq_t(w):
    K = w.shape[1]
    return w.reshape(MLA_HEADS, LANES, K)[:, :96].reshape(N_CHIPS, 192, K)


def _regroup_w_ukv(w):
    K = w.shape[0]
    t = w.reshape(K, MLA_HEADS, 128)
    return jnp.concatenate([jnp.pad(t[:, :, :64], ((0, 0), (0, 0), (0, 64))).reshape(K, MLA_HEADS * LANES),
                            t[:, :, 64:].reshape(K, MLA_HEADS * HEAD_DIM)], axis=1)


def _ungroup_w_ukv(w):
    K = w.shape[0]
    return jnp.concatenate([w[:, :MLA_HEADS * LANES].reshape(K, MLA_HEADS, LANES)[:, :, :64],
                            w[:, MLA_HEADS * LANES:].reshape(K, MLA_HEADS, 64)], axis=2).reshape(K, 1024)


def _ffn_fwd(x, g_pre, g_post, fetch, tag, h, g_next):
    W = fetch("ffn%sa" % tag, x)
    if h is None:
        h = _norm_fwd(x, g_pre, "norm_fwd")
    G, U, A = _ffn_gate_up(h, W["w%s_gate" % tag], W["w%s_up" % tag], "ffn_gate_up")
    W = fetch("ffn%sb" % tag, A)
    Y = _mm([(A, W["w%s_down" % tag])], "nn", F32, "ffn_down", bm=1024)
    x_new, h_next = _resid_norm(x, Y, g_post, 0.5, "resid_norm", g_next)
    return x_new, dict(x=x, h=h, G=G, U=U, A=A, Y=Y), h_next


def _ffn_bwd(dxo, sv, g_pre, g_post, wg, wu, wd_rows, ready):
    dY, dg_post = _norm_bwd(sv["Y"], g_post, dxo, 0.5, None, BF16, "norm_bwd_post")
    dG, dU = _ffn_bwd_mid(dY, wd_rows, sv["G"], sv["U"], "ffn_bwd_mid")
    cut = lambda t: t.reshape(N_CHIPS, -1, t.shape[1])
    token = ready((cut(_mm([(dG, sv["h"])], "tn", BF16, "ffn_dw_in")), cut(_mm([(dU, sv["h"])], "tn", BF16, "ffn_dw_in")),
                   cut(_mm([(sv["A"], dY)], "tn", BF16, "ffn_dw_down"))))
    dh = _mm([(dG, wg), (dU, wu)], "nn", F32, "ffn_dh")
    dx, dg_pre = _norm_bwd(sv["x"], _ordered_after(g_pre, token), dh, 1.0, dxo, F32, "norm_bwd_pre")
    return dx, dg_pre, dg_post, token


def _ordered_after(g, token):
    return g if token is None else g + token[0, 0]


GATHER_GROUPS = (("ffn1a", ("w1_gate", "w1_up")), ("ffn1b", ("w1_down",)),
                 ("mix", ("w_in", "w_mla_uq", "w_mla_ukv", "w_out")),
                 ("ffn2a", ("w2_gate", "w2_up", "w2_down", "w_ple_gate", "w_ple_proj")))
SCATTER_GROUPS = (("ffn2", ("w2_gate", "w2_up", "w2_down", "w_ple_gate", "w_ple_proj")),
                  ("mix", ("w_in", "w_mla_uq", "w_mla_ukv", "w_out")),
                  ("ffn1", ("w1_gate", "w1_up", "w1_down")))


def _layer_fwd(x, p_l, fetch, G, w_conv, cp, sp, h_in, g_next):
    sv = {}
    x1, sv["ffn1"], h2 = _ffn_fwd(x, G["g_ffn1_pre"], G["g_ffn1_post"], fetch, "1", h_in, G["g_mix_pre"])

    W = fetch("mix", x1)
    Z = _mm([(h2, W["w_in"])], "nt", F32, "mix_in")
    o_sb, tot_sb = _sb_fwd(Z)
    cq, ckv = Z[:, Z_Q:Z_Q + 384], Z[:, Z_KV:Z_KV + 256]
    nq = _norm_fwd(cq, G["g_mla_q"], "norm_fwd_q")
    Qf = _mm([(nq, W["w_mla_uq"])], "nt", F32, "mla_uq")
    nkv = _norm_fwd(ckv, G["g_mla_kv"], "norm_fwd_kv")
    KV = _mm([(nkv, W["w_mla_ukv"])], "nn", F32, "mla_ukv")
    qcat, kcat = _mla_prep(Qf, KV, Z, cp, sp)
    o_mla, lse = _mla_fwd(qcat, kcat, KV)
    y_cv = _conv_fwd(Z, w_conv)
    C = jnp.concatenate([o_sb, o_mla, y_cv], axis=1).astype(BF16)
    Mx = _mm([(C, W["w_out"])], "nn", F32, "mix_out")
    x2, h3 = _resid_norm(x1, Mx, G["g_mix_post"], 1.0, "resid_norm", G["g_ffn2_pre"])
    sv["mix"] = dict(x=x1, h=h2, Z=Z, tot_sb=tot_sb, cq=cq, ckv=ckv, nq=nq, nkv=nkv, qcat=qcat, kcat=kcat, KV=KV,
                     o_mla=o_mla, lse=lse, C=C, Mx=Mx)

    x3, sv["ffn2"], h4 = _ffn_fwd(x2, G["g_ffn2_pre"], G["g_ffn2_post"], fetch, "2", h3, G["g_ple_pre"])
    W = fetch("ffn2b", x3)

    p16 = p_l.astype(BF16)
    Qg = _mm([(h4, W["w_ple_gate"])], "nn", F32, "ple_gate")
    Pp = _ple_proj(p16, W["w_ple_proj"])
    e = _ew(lambda q, pp: _sigmoid(q) * pp, [Qg, Pp], [(D_MODEL, F32)], "ple_mul")
    x4, h_next = _resid_norm(x3, e, G["g_ple_post"], 1.0, "resid_norm", g_next)
    sv["ple"] = dict(x=x3, h=h4, p16=p16, Qg=Qg, Pp=Pp, e=e)
    sv["W"] = W
    return x4, sv, h_next


def _layer_bwd(dx4, sv, G, w_conv, cp, sp, emit, token):
    gg, gw, W = {}, {}, sv["W"]

    s = sv["ple"]
    de, gg["g_ple_post"] = _norm_bwd(s["e"], _ordered_after(G["g_ple_post"], token), dx4, 1.0, None, F32,
                                     "norm_bwd_e")

    def ple_bwd(dev, q, pp):
        sg = _sigmoid(q)
        return dev * pp * sg * (1.0 - sg), dev * sg

    dQg, dPp = _ew(ple_bwd, [de, s["Qg"], s["Pp"]], [(D_MODEL, BF16)] * 2, "ple_mul_bwd")
    cut = lambda t: t.reshape(N_CHIPS, -1, t.shape[1])
    gw["w_ple_proj"] = _dw_col_shards(s["p16"], dPp, "ple_dw_proj")
    gw["w_ple_gate"] = cut(_mm([(s["h"], dQg)], "tn", BF16, "ple_dw_gate"))
    dh4 = _mm([(dQg, W["w_ple_gate"])], "nt", F32, "ple_dh")
    dx3, gg["g_ple_pre"] = _norm_bwd(s["x"], G["g_ple_pre"], dh4, 1.0, dx4, F32, "norm_bwd_pre")

    def ready2(dws):
        gw["w2_gate"], gw["w2_up"], gw["w2_down"] = dws
        return emit("ffn2", gw)

    dx2, gg["g_ffn2_pre"], gg["g_ffn2_post"], token = _ffn_bwd(
        dx3, sv["ffn2"], G["g_ffn2_pre"], G["g_ffn2_post"], W["w2_gate"], W["w2_up"], W["w2_down"], ready2)

    s = sv["mix"]
    dM, gg["g_mix_post"] = _norm_bwd(s["Mx"], _ordered_after(G["g_mix_post"], token), dx2, 1.0, None, BF16,
                                     "norm_bwd_post")
    dC = _mm([(dM, W["w_out"])], "nt", F32, "mix_out_dx")
    gw["w_out"] = cut(_mm([(s["C"], dM)], "tn", BF16, "mix_out_dw"))

    db, dc, dhh, gg["w_conv"] = _conv_bwd(s["Z"], w_conv, dC)

    dqc, dkc, dv = _mla_bwd(s["qcat"], s["kcat"], s["KV"], s["o_mla"], s["lse"], dC)
    dQf, dkr = _mla_prep_bwd(dqc, dkc, cp, sp)
    gw["w_mla_uq"] = _ungroup_w_uq_t(_mm([(dQf, s["nq"])], "tn", BF16, "mla_uq_dw"))
    dnq = _mm([(dQf, W["w_mla_uq"])], "nn", F32, "mla_uq_dx")
    dcq, gg["g_mla_q"] = _norm_bwd(s["cq"], G["g_mla_q"], dnq, 1.0, None, F32, "norm_bwd_q")
    dkv = jnp.concatenate([dkc, dv], axis=1).astype(BF16)
    gw["w_mla_ukv"] = _cut_cols(_ungroup_w_ukv(_mm([(s["nkv"], dkv)], "tn", BF16, "mla_ukv_dw")))
    dnkv = _mm([(dkv, W["w_mla_ukv"])], "nt", F32, "mla_ukv_dx")
    dckv, gg["g_mla_kv"] = _norm_bwd(s["ckv"], G["g_mla_kv"], dnkv, 1.0, None, F32, "norm_bwd_kv")

    dsq, dsk, dsv = _sb_bwd(s["Z"], s["tot_sb"], dC)
    dZ = jnp.concatenate([dsq, dsk, dsv, dckv, dkr, dcq, db, dc, dhh], axis=1).astype(BF16)
    gw["w_in"] = _ungroup_w_in_t(_mm([(dZ, s["h"])], "tn", BF16, "mix_in_dw"))
    dh2 = _mm([(dZ, W["w_in"])], "nn", F32, "mix_in_dx")
    dx1, gg["g_mix_pre"] = _norm_bwd(s["x"], G["g_mix_pre"], dh2, 1.0, dx2, F32, "norm_bwd_pre")
    started_mix = emit("mix", gw)
    token = token if started_mix is None else started_mix

    def ready1(dws):
        gw["w1_gate"], gw["w1_up"], gw["w1_down"] = dws
        return emit("ffn1", gw)

    dx0, gg["g_ffn1_pre"], gg["g_ffn1_post"], started_ffn1 = _ffn_bwd(
        dx1, sv["ffn1"], G["g_ffn1_pre"], _ordered_after(G["g_ffn1_post"], token), W["w1_gate"], W["w1_up"],
        W["w1_down"], ready1)
    return dx0, gg, token if started_ffn1 is None else started_ffn1


def _pack_small(vecs):
    flat = jnp.concatenate([v.reshape(-1) for v in vecs])
    rows = -(-flat.shape[0] // (8 * LANES)) * 8
    return jnp.pad(flat, (0, rows * LANES - flat.shape[0])).reshape(rows, LANES)


def _group_weights(names, landed, shards, chip):
    out = {}
    for name, g, own in zip(names, landed, shards):
        g = lax.dynamic_update_slice(g, own[None], (chip, 0, 0))
        if name == "w_in":
            g = _regroup_w_in_t(g)
        elif name == "w_mla_uq":
            g = _regroup_w_uq_t(g)
        elif name not in ("w_mla_ukv", "w_ple_proj"):
            g = g.reshape(-1, g.shape[2])
        elif name == "w_mla_ukv":
            g = _regroup_w_ukv(_cat_cols(g))
        out[name] = g
    return out


def kernel(x, p, positions, g_ffn1_pre, w1_gate, w1_up, w1_down, g_ffn1_post, g_mix_pre, w_in, g_mla_q, w_mla_uq, g_mla_kv, w_mla_ukv, w_conv, w_out, g_mix_post, g_ffn2_pre, w2_gate, w2_up, w2_down, g_ffn2_post, g_ple_pre, w_ple_gate, w_ple_proj, g_ple_post, loss_target, m_g_ffn1_pre, m_w1_gate, m_w1_up, m_w1_down, m_g_ffn1_post, m_g_mix_pre, m_w_in, m_g_mla_q, m_w_mla_uq, m_g_mla_kv, m_w_mla_ukv, m_w_conv, m_w_out, m_g_mix_post, m_g_ffn2_pre, m_w2_gate, m_w2_up, m_w2_down, m_g_ffn2_post, m_g_ple_pre, m_w_ple_gate, m_w_ple_proj, m_g_ple_post, v_g_ffn1_pre, v_w1_gate, v_w1_up, v_w1_down, v_g_ffn1_post, v_g_mix_pre, v_w_in, v_g_mla_q, v_w_mla_uq, v_g_mla_kv, v_w_mla_ukv, v_w_conv, v_w_out, v_g_mix_post, v_g_ffn2_pre, v_w2_gate, v_w2_up, v_w2_down, v_g_ffn2_post, v_g_ple_pre, v_w_ple_gate, v_w_ple_proj, v_g_ple_post):
    w = dict(g_ffn1_pre=g_ffn1_pre, w1_gate=w1_gate, w1_up=w1_up, w1_down=w1_down, g_ffn1_post=g_ffn1_post,
             g_mix_pre=g_mix_pre, w_in=w_in, g_mla_q=g_mla_q, w_mla_uq=w_mla_uq, g_mla_kv=g_mla_kv,
             w_mla_ukv=w_mla_ukv, w_conv=w_conv, w_out=w_out, g_mix_post=g_mix_post, g_ffn2_pre=g_ffn2_pre,
             w2_gate=w2_gate, w2_up=w2_up, w2_down=w2_down, g_ffn2_post=g_ffn2_post, g_ple_pre=g_ple_pre,
             w_ple_gate=w_ple_gate, w_ple_proj=w_ple_proj, g_ple_post=g_ple_post)
    m = dict(g_ffn1_pre=m_g_ffn1_pre, w1_gate=m_w1_gate, w1_up=m_w1_up, w1_down=m_w1_down, g_ffn1_post=m_g_ffn1_post,
             g_mix_pre=m_g_mix_pre, w_in=m_w_in, g_mla_q=m_g_mla_q, w_mla_uq=m_w_mla_uq, g_mla_kv=m_g_mla_kv,
             w_mla_ukv=m_w_mla_ukv, w_conv=m_w_conv, w_out=m_w_out, g_mix_post=m_g_mix_post, g_ffn2_pre=m_g_ffn2_pre,
             w2_gate=m_w2_gate, w2_up=m_w2_up, w2_down=m_w2_down, g_ffn2_post=m_g_ffn2_post, g_ple_pre=m_g_ple_pre,
             w_ple_gate=m_w_ple_gate, w_ple_proj=m_w_ple_proj, g_ple_post=m_g_ple_post)
    v = dict(g_ffn1_pre=v_g_ffn1_pre, w1_gate=v_w1_gate, w1_up=v_w1_up, w1_down=v_w1_down, g_ffn1_post=v_g_ffn1_post,
             g_mix_pre=v_g_mix_pre, w_in=v_w_in, g_mla_q=v_g_mla_q, w_mla_uq=v_w_mla_uq, g_mla_kv=v_g_mla_kv,
             w_mla_ukv=v_w_mla_ukv, w_conv=v_w_conv, w_out=v_w_out, g_mix_post=v_g_mix_post, g_ffn2_pre=v_g_ffn2_pre,
             w2_gate=v_w2_gate, w2_up=v_w2_up, w2_down=v_w2_down, g_ffn2_post=v_g_ffn2_post, g_ple_pre=v_g_ple_pre,
             w_ple_gate=v_w_ple_gate, w_ple_proj=v_w_ple_proj, g_ple_post=v_g_ple_post)

    for name in TRANSPOSED:
        w[name], m[name], v[name] = (jnp.swapaxes(t[name], 1, 2) for t in (w, m, v))

    depth = g_ffn1_pre.shape[0]
    assert depth == 2, "the reduced gradients are assembled in [2, K, N] buffers"
    S = x.shape[1]
    cx, cy, cc = lax.axis_index("x"), lax.axis_index("y"), lax.axis_index("c")
    chip = 2 * cx + cy
    c_idx = cc.reshape(1).astype(jnp.int32)
    idx2 = jnp.stack([chip, cc]).astype(jnp.int32)

    conv_slot = lax.dynamic_update_slice(jnp.zeros((depth, 3, 256), F32),
                                         w_conv * (cc == 0).astype(F32), (0, 0, 64 * chip))
    conv_sum = _allreduce_small(_pack_small([conv_slot]), positions)
    w_conv_full = conv_sum[:depth * 3 * 256 // LANES].reshape(depth, 3, 256)

    def padded(name, i):
        (rows, _), rows_p = BIG[name]
        return jnp.pad(w[name][i].astype(BF16), ((0, rows_p - rows), (0, 0)))

    def groups_of(i, groups):
        return groups if i == 0 else (("all", BIG_NAMES),)

    started, after = {}, conv_sum
    for i in range(depth):
        for gname, names in groups_of(i, GATHER_GROUPS):
            shards = [padded(name, i) for name in names]
            lands = [lax.empty((N_CHIPS,) + s.shape, BF16) for s in shards]
            started[i, gname] = _exchange_start(_gather_copies, shards, lands, after, "gather_start_%d_%s" % (i, gname))
            after = started[i, gname][4]
    all_started = after

    inv = ROPE_BASE ** (-jnp.arange(ROPE_HALF, dtype=F32) / ROPE_HALF)
    zeros = lambda n: jnp.zeros((n,), F32)
    ones = jnp.ones((ROPE_HALF,), F32)
    inv_pat = jnp.concatenate([zeros(64), inv, inv, zeros(32)]).reshape(1, LANES)
    sign_pat = jnp.concatenate([zeros(64), -ones, ones, zeros(32)]).reshape(1, LANES)
    cp, sp = _rope_tables(positions.reshape(S, 1), inv_pat, sign_pat)

    def fetcher(i):
        table, have = dict(groups_of(i, GATHER_GROUPS)), {}

        def fetch(group, after):
            key = group if group in table else "all"
            if (i, key) in started:
                ssem, rsem, srcs, lands, _ = started.pop((i, key))
                if i == 0 and group == "ffn1a":
                    after = all_started
                srcs, landed = _exchange_wait(_gather_copies, ssem, rsem, srcs, lands, after,
                                              "gather_wait_%d_%s" % (i, key))
                have.update(_group_weights(table[key], _forward_rows(landed), srcs, chip))
            return have

        return fetch

    xs, saved, h_in = x[0], [], None
    gains = [{name: w[name][i].reshape(1, n) for name, n in GAINS} for i in range(depth)]
    for i in range(depth):
        g_next = gains[i + 1]["g_ffn1_pre"] if i + 1 < depth else None
        xs, sv, h_in = _layer_fwd(xs, p[i, 0], fetcher(i), gains[i], w_conv_full[i], cp, sp, h_in, g_next)
        saved.append(sv)

    loss_part, dx = _loss_and_grad(xs, loss_target[0])
    loss = lax.psum(loss_part[0, 0], AXES)

    scattering = []

    def emitter(i):
        table = dict(groups_of(i, SCATTER_GROUPS))

        def emit(group, gw):
            if group in table:
                names = table[group]
            elif group == "ffn1":
                names = table["all"]
            else:
                return None
            full = [gw[name] for name in names]
            pair_sums = _add_rows(full, _swap_rows(full), c_idx)
            lands = [lax.empty((3,) + s.shape[1:], BF16) for s in pair_sums]
            st = _exchange_start(_scatter_copies, pair_sums, lands, c_idx, "scatter_start_%d_%s" % (i, group))
            scattering.append((i, group, names, st))
            return st[4]

        return emit

    ggs, token = [None] * depth, None
    for i in reversed(range(depth)):
        dx, ggs[i], token = _layer_bwd(dx, saved[i], gains[i], w_conv_full[i], cp, sp, emitter(i), token)

    fins, after = {name: None for name in BIG_NAMES}, dx
    for i, group, names, (ssem, rsem, srcs, lands, _) in scattering:
        srcs, arrived = _exchange_wait(_scatter_copies, ssem, rsem, srcs, lands, after,
                                       "scatter_wait_%d_%s" % (i, group))
        sums = _add_chip_sums(srcs, arrived, idx2, i, [fins[name] for name in names], after)
        fins.update(zip(names, sums))
        after = sums[0]
    joined = _join_rows([fins[name] for name in BIG_NAMES])
    reduced = dict(zip(BIG_NAMES, joined))

    small_names = [name for name, _ in GAINS] + ["w_conv"]
    small = _allreduce_small(_pack_small([jnp.stack([ggs[i][name] for i in range(depth)]) for name in small_names]),
                             joined[0])
    flat, off = small.reshape(-1), 0
    for name, n in GAINS:
        reduced[name] = flat[off:off + depth * n].reshape(depth, n)
        off += depth * n
    conv_full = flat[off:off + depth * 3 * 256].reshape(depth, 3, 256)
    reduced["w_conv"] = lax.dynamic_slice(conv_full, (0, 0, 64 * chip), (depth, 3, 64))

    grads, deltas, new_m, new_v = {}, {}, {}, {}
    for name in WEIGHT_ORDER:
        shape = w[name].shape
        three_d = shape if len(shape) == 3 else (1,) + shape
        g_ = reduced[name] if len(shape) == 3 else reduced[name].reshape(three_d)
        outs = _adamw(w[name].reshape(three_d), g_, m[name].reshape(three_d), v[name].reshape(three_d), "adamw")
        if name in TRANSPOSED:
            outs = [jnp.swapaxes(o, 1, 2) for o in outs]
        grads[name], deltas[name], new_m[name], new_v[name] = (o.reshape(o.shape[-len(shape):]) for o in outs)

    return (loss, dx[None], *[grads[n] for n in WEIGHT_ORDER], *[deltas[n] for n in WEIGHT_ORDER],
            *[new_m[n] for n in WEIGHT_ORDER], *[new_v[n] for n in WEIGHT_ORDER])
```

```python
import jax
import jax.numpy as jnp
from jax import lax
from jax.experimental import pallas as pl
from jax.experimental.pallas import tpu as pltpu

F32 = jnp.float32
BF16 = jnp.bfloat16
MESH = pl.DeviceIdType.MESH
AXES = ("x", "y", "c")

D_MODEL = 1024
N_CHIPS = 4
EPS = 1e-6
NEG_INF = -1e30
ROPE_BASE = 10000.0
ROPE_HALF = 16
LANES = 128
SB_HEADS, MLA_HEADS, HEAD_DIM = 4, 8, 64
MLA_SCALE = 96.0 ** -0.5
SB_SCALE = 64.0 ** -0.5
SB_BQ, SB_BK = 512, 128
MLA_BQ, MLA_BK = 256, 256
ADAM_LR, ADAM_B1, ADAM_B2, ADAM_EPS, ADAM_WD, ADAM_STEP = 0.001, 0.9, 0.999, 1e-08, 0.01, 10
VMEM_LIMIT = 48 * 2 ** 20

Z_SB, Z_KV, Z_KR, Z_Q, Z_CV, Z_W = 0, 768, 1024, 1152, 1536, 2304

NT = (((1,), (1,)), ((), ()))
TN = (((0,), (0,)), ((), ()))

TRANSPOSED = ("w1_gate", "w1_up", "w2_gate", "w2_up", "w_in", "w_mla_uq")
BIG = {"w1_gate": ((704, 1024), 704), "w1_up": ((704, 1024), 704), "w1_down": ((704, 1024), 704),
       "w_in": ((552, 1024), 576), "w_mla_uq": ((192, 384), 192), "w_mla_ukv": ((256, 256), 256),
       "w_out": ((256, 1024), 256),
       "w2_gate": ((704, 1024), 704), "w2_up": ((704, 1024), 704), "w2_down": ((704, 1024), 704),
       "w_ple_gate": ((256, 1024), 256), "w_ple_proj": ((256, 256), 256)}
BIG_NAMES = tuple(BIG)
GAINS = (("g_ffn1_pre", 1024), ("g_ffn1_post", 1024), ("g_mix_pre", 1024), ("g_mla_q", 384),
         ("g_mla_kv", 256), ("g_mix_post", 1024), ("g_ffn2_pre", 1024), ("g_ffn2_post", 1024),
         ("g_ple_pre", 1024), ("g_ple_post", 1024))
WEIGHT_ORDER = ("g_ffn1_pre", "w1_gate", "w1_up", "w1_down", "g_ffn1_post", "g_mix_pre", "w_in", "g_mla_q",
                "w_mla_uq", "g_mla_kv", "w_mla_ukv", "w_conv", "w_out", "g_mix_post", "g_ffn2_pre", "w2_gate",
                "w2_up", "w2_down", "g_ffn2_post", "g_ple_pre", "w_ple_gate", "w_ple_proj", "g_ple_post")

_pcall = pl.pallas_call


def _params(sem=None):
    return pltpu.CompilerParams(dimension_semantics=sem, vmem_limit_bytes=VMEM_LIMIT)


def _dot(a, b, dims=None):
    a, b = a.astype(BF16), b.astype(BF16)
    if dims is None:
        return jnp.dot(a, b, preferred_element_type=F32)
    return lax.dot_general(a, b, dims, preferred_element_type=F32)


def _rstd(v):
    return lax.rsqrt(jnp.mean(v * v, axis=-1, keepdims=True) + EPS)


def _sigmoid(v):
    return 0.5 * jnp.tanh(0.5 * v) + 0.5


def _row_block(n, want, mult=16):
    for b in range(min(n, want), 0, -1):
        if n % b == 0 and b % mult == 0:
            return b
    return n


def _mm(pairs, mode, out_dtype, name, bm=512, bn=512):
    a0, b0 = pairs[0]
    if mode == "nn":
        M, N = a0.shape[0], b0.shape[1]
    elif mode == "nt":
        M, N = a0.shape[0], b0.shape[0]
    else:
        M, N = a0.shape[1], b0.shape[1]
        bn = max(bn, 1024)
    bm, bn = _row_block(M, bm, 128 if mode == "tn" else 16), _row_block(N, bn, 128)
    n_pairs = len(pairs)
    dims = {"nn": None, "nt": NT, "tn": TN}[mode]

    def body(*refs):
        acc = None
        for t in range(n_pairs):
            part = _dot(refs[2 * t][...], refs[2 * t + 1][...], dims)
            acc = part if acc is None else acc + part
        refs[-1][...] = acc.astype(refs[-1].dtype)

    in_specs, ops = [], []
    for a, b in pairs:
        if mode == "nn":
            in_specs += [pl.BlockSpec((bm, a.shape[1]), lambda j, i: (i, 0)),
                         pl.BlockSpec((b.shape[0], bn), lambda j, i: (0, j))]
        elif mode == "nt":
            in_specs += [pl.BlockSpec((bm, a.shape[1]), lambda j, i: (i, 0)),
                         pl.BlockSpec((bn, b.shape[1]), lambda j, i: (j, 0))]
        else:
            in_specs += [pl.BlockSpec((a.shape[0], bm), lambda j, i: (0, i)),
                         pl.BlockSpec((b.shape[0], bn), lambda j, i: (0, j))]
        ops += [a, b]
    return _pcall(body, name=name, grid=(N // bn, M // bm), in_specs=in_specs,
                  out_specs=pl.BlockSpec((bm, bn), lambda j, i: (i, j)),
                  out_shape=jax.ShapeDtypeStruct((M, N), out_dtype),
                  compiler_params=_params(("parallel", "parallel")))(*ops)


def _dw_col_shards(a, b, name):
    S, K = a.shape
    Np = b.shape[1] // N_CHIPS

    def body(a_ref, b_ref, o_ref):
        o_ref[...] = _dot(a_ref[...], b_ref[...], TN).astype(BF16)

    return _pcall(body, name=name, grid=(N_CHIPS,),
                  in_specs=[pl.BlockSpec((S, K), lambda j: (0, 0)), pl.BlockSpec((S, Np), lambda j: (0, j))],
                  out_specs=pl.BlockSpec((None, K, Np), lambda j: (j, 0, 0)),
                  out_shape=jax.ShapeDtypeStruct((N_CHIPS, K, Np), BF16),
                  compiler_params=_params(("parallel",)))(a, b)


def _ew(fn, ins, outs, name, br=512):
    S = max(a.shape[0] for a in ins)
    br = _row_block(S, br)
    n_in = len(ins)

    def body(*refs):
        res = fn(*[r[...] for r in refs[:n_in]])
        if not isinstance(res, tuple):
            res = (res,)
        for r, v in zip(refs[n_in:], res):
            r[...] = v.astype(r.dtype)

    in_specs = [pl.BlockSpec((br, a.shape[1]), lambda i: (i, 0)) if a.shape[0] == S and S > 1
                else pl.BlockSpec(a.shape, lambda i: (0, 0)) for a in ins]
    out = _pcall(body, name=name, grid=(S // br,), in_specs=in_specs,
                 out_specs=tuple(pl.BlockSpec((br, w), lambda i: (i, 0)) for w, _ in outs),
                 out_shape=tuple(jax.ShapeDtypeStruct((S, w), dt) for w, dt in outs),
                 compiler_params=_params(("parallel",)))(*ins)
    return out if len(outs) > 1 else out[0]


def _norm_fwd(x, g, name):
    return _ew(lambda xv, gv: xv * _rstd(xv) * gv, [x, g], [(x.shape[1], BF16)], name, br=512)


def _resid_norm(x, y, g, alpha, name, g_next=None):
    W = x.shape[1]
    if g_next is None:
        return _ew(lambda xv, yv, gv: xv + alpha * (yv * _rstd(yv) * gv), [x, y, g], [(W, F32)], name), None

    def both(xv, yv, gv, gn):
        xn = xv + alpha * (yv * _rstd(yv) * gv)
        return xn, xn * _rstd(xn) * gn

    return _ew(both, [x, y, g, g_next], [(W, F32), (W, BF16)], name + "_next")


def _norm_bwd(xin, g, dy, alpha, resid, out_dtype, name):
    S, W = xin.shape
    br = _row_block(S, 512)
    has_res = resid is not None

    def body(*refs):
        x_ref, g_ref, dy_ref = refs[:3]
        dx_ref, dg_ref = refs[-2:]
        xv, dyv = x_ref[...], dy_ref[...] * alpha
        r = _rstd(xv)
        xh = xv * r
        u = dyv * g_ref[...]
        dx = r * (u - xh * jnp.mean(u * xh, axis=-1, keepdims=True))
        if has_res:
            dx = dx + refs[3][...]
        dx_ref[...] = dx.astype(dx_ref.dtype)
        part = jnp.sum(dyv * xh, axis=0, keepdims=True)

        @pl.when(pl.program_id(0) == 0)
        def _():
            dg_ref[...] = part

        @pl.when(pl.program_id(0) > 0)
        def _():
            dg_ref[...] += part

    row = pl.BlockSpec((br, W), lambda i: (i, 0))
    vec = pl.BlockSpec((1, W), lambda i: (0, 0))
    ops = [xin, g, dy] + ([resid] if has_res else [])
    return _pcall(body, name=name, grid=(S // br,), in_specs=[row, vec, row] + ([row] if has_res else []),
                  out_specs=(row, vec),
                  out_shape=(jax.ShapeDtypeStruct((S, W), out_dtype), jax.ShapeDtypeStruct((1, W), F32)),
                  compiler_params=_params(("arbitrary",)))(*ops)


def _ffn_gate_up(h, wgt, wut, name):
    S, K = h.shape
    F = wgt.shape[0]
    bm, bn = _row_block(S, 512), _row_block(F, 1408, 128)

    def body(h_ref, wg_ref, wu_ref, g_ref, u_ref, a_ref):
        hv = h_ref[...]
        g = _dot(hv, wg_ref[...], NT)
        u = _dot(hv, wu_ref[...], NT)
        g_ref[...] = g.astype(BF16)
        u_ref[...] = u.astype(BF16)
        a_ref[...] = (g * _sigmoid(g) * u).astype(BF16)

    blk = pl.BlockSpec((bm, bn), lambda n, i: (i, n))
    wsp = pl.BlockSpec((bn, K), lambda n, i: (n, 0))
    return _pcall(body, name=name, grid=(F // bn, S // bm),
                  in_specs=[pl.BlockSpec((bm, K), lambda n, i: (i, 0)), wsp, wsp],
                  out_specs=(blk, blk, blk), out_shape=(jax.ShapeDtypeStruct((S, F), BF16),) * 3,
                  compiler_params=_params(("parallel", "parallel")))(h, wgt, wut)


def _ffn_bwd_mid(dy, wd, g, u, name):
    S, D = dy.shape
    F = g.shape[1]
    bm, bn = _row_block(S, 512), _row_block(F, 1408, 128)

    def body(dy_ref, wd_ref, g_ref, u_ref, dg_ref, du_ref):
        da = _dot(dy_ref[...], wd_ref[...], NT)
        gv, uv = g_ref[...].astype(F32), u_ref[...].astype(F32)
        s = _sigmoid(gv)
        dg_ref[...] = (da * uv * (s * (1.0 + gv * (1.0 - s)))).astype(BF16)
        du_ref[...] = (da * (gv * s)).astype(BF16)

    blk = pl.BlockSpec((bm, bn), lambda j, i: (i, j))
    return _pcall(body, name=name, grid=(F // bn, S // bm),
                  in_specs=[pl.BlockSpec((bm, D), lambda j, i: (i, 0)),
                            pl.BlockSpec((bn, D), lambda j, i: (j, 0)), blk, blk],
                  out_specs=(blk, blk), out_shape=(jax.ShapeDtypeStruct((S, F), BF16),) * 2,
                  compiler_params=_params(("parallel", "parallel")))(dy, wd, g, u)


def _ple_proj(p16, w):
    S, K = p16.shape
    bm = _row_block(S, 1024)

    def body(p_ref, w_ref, o_ref):
        o_ref[...] = _dot(p_ref[...], w_ref[...])

    return _pcall(body, name="ple_proj", grid=(N_CHIPS, S // bm),
                  in_specs=[pl.BlockSpec((bm, K), lambda j, i: (i, 0)),
                            pl.BlockSpec((None, K, 256), lambda j, i: (j, 0, 0))],
                  out_specs=pl.BlockSpec((bm, 256), lambda j, i: (i, j)),
                  out_shape=jax.ShapeDtypeStruct((S, N_CHIPS * 256), F32),
                  compiler_params=_params(("parallel", "parallel")))(p16, w)


def _loss_and_grad(xf, tgt):
    S, W = xf.shape
    br = _row_block(S, 256)

    def body(x_ref, t_ref, l_ref, d_ref):
        d = x_ref[...] - t_ref[...]
        d_ref[...] = d * (1.0 / W)
        part = 0.5 * jnp.sum(jnp.sum(d * d, axis=-1, keepdims=True) * (1.0 / W), axis=0, keepdims=True)

        @pl.when(pl.program_id(0) == 0)
        def _():
            l_ref[...] = part

        @pl.when(pl.program_id(0) > 0)
        def _():
            l_ref[...] += part

    row = pl.BlockSpec((br, W), lambda i: (i, 0))
    return _pcall(body, name="loss", grid=(S // br,), in_specs=[row, row],
                  out_specs=(pl.BlockSpec((1, 1), lambda i: (0, 0)), row),
                  out_shape=(jax.ShapeDtypeStruct((1, 1), F32), jax.ShapeDtypeStruct((S, W), F32)),
                  compiler_params=_params(("arbitrary",)))(xf, tgt)


def _adamw(w, g, m, v, name):
    L, R, C = w.shape
    Cp = g.shape[2]
    br = _row_block(R, 256, 8) if R % 8 == 0 else R

    def body(w_ref, g_ref, m_ref, v_ref, go_ref, d_ref, nm_ref, nv_ref):
        gv = g_ref[...][:, :C]
        nm = ADAM_B1 * m_ref[...] + (1.0 - ADAM_B1) * gv
        nv = ADAM_B2 * v_ref[...] + (1.0 - ADAM_B2) * (gv * gv)
        m_hat = nm / (1.0 - ADAM_B1 ** ADAM_STEP)
        v_hat = nv / (1.0 - ADAM_B2 ** ADAM_STEP)
        go_ref[...] = gv
        d_ref[...] = -ADAM_LR * (m_hat / (jnp.sqrt(v_hat) + ADAM_EPS) + ADAM_WD * w_ref[...])
        nm_ref[...] = nm
        nv_ref[...] = nv

    blk = pl.BlockSpec((None, br, C), lambda l, i: (l, i, 0))
    gblk = pl.BlockSpec((None, br, Cp), lambda l, i: (l, i, 0))
    return _pcall(body, name=name, grid=(L, R // br), in_specs=[blk, gblk, blk, blk], out_specs=(blk,) * 4,
                  out_shape=(jax.ShapeDtypeStruct((L, R, C), F32),) * 4,
                  compiler_params=_params(("parallel", "parallel")))(w, g, m, v)


def _rope_tables(pos, inv_pat, sign_pat):
    def fn(p, iv, sg):
        ang = p.astype(F32) * iv
        return jnp.cos(ang), jnp.sin(ang) * sg

    return _ew(fn, [pos, inv_pat, sign_pat], [(LANES, F32)] * 2, "rope_tables")


def _swap_halves_of_rope(v):
    W = v.shape[1]
    lane = lax.broadcasted_iota(jnp.int32, (1, W), 1) % LANES
    return jnp.where((lane >= 64) & (lane < 80), pltpu.roll(v, W - ROPE_HALF, 1),
                     jnp.where((lane >= 80) & (lane < 96), pltpu.roll(v, ROPE_HALF, 1), 0.0))


def _tile_lanes(v, n):
    return jnp.concatenate([v] * n, axis=1)


def _mla_prep(qf, kv, z, cp, sp):
    S = qf.shape[0]
    br = _row_block(S, 256)
    W = MLA_HEADS * LANES

    def body(q_ref, k_ref, r_ref, c_ref, s_ref, qo_ref, ko_ref):
        c, s = c_ref[...], s_ref[...]
        q = q_ref[...]
        qo_ref[...] = (q * _tile_lanes(c, MLA_HEADS) + _swap_halves_of_rope(q) * _tile_lanes(s, MLA_HEADS)).astype(BF16)
        r = r_ref[...]
        lane = lax.broadcasted_iota(jnp.int32, (1, LANES), 1)
        kr = jnp.where((lane >= 64) & (lane < 96), r * c + _swap_halves_of_rope(r) * s, 0.0)
        ko_ref[...] = (k_ref[...] + _tile_lanes(kr, MLA_HEADS)).astype(BF16)

    wide = pl.BlockSpec((br, W), lambda i: (i, 0))
    one = pl.BlockSpec((br, LANES), lambda i: (i, 0))
    return _pcall(body, name="mla_prep", grid=(S // br,),
                  in_specs=[wide, wide, pl.BlockSpec((br, LANES), lambda i: (i, Z_KR // LANES)), one, one],
                  out_specs=(wide, wide), out_shape=(jax.ShapeDtypeStruct((S, W), BF16),) * 2,
                  compiler_params=_params(("parallel",)))(qf, kv, z, cp, sp)


def _mla_prep_bwd(dq, dk, cp, sp):
    S, W = dq.shape
    br = _row_block(S, 256)

    def body(dq_ref, dk_ref, c_ref, s_ref, dqo_ref, dr_ref):
        c, s = c_ref[...], s_ref[...]
        d = dq_ref[...]
        dqo_ref[...] = (d * _tile_lanes(c, MLA_HEADS) + _swap_halves_of_rope(d * _tile_lanes(s, MLA_HEADS))).astype(BF16)
        dkv = dk_ref[...]
        tot = dkv[:, 0:LANES]
        for h in range(1, MLA_HEADS):
            tot = tot + dkv[:, h * LANES:(h + 1) * LANES]
        lane = lax.broadcasted_iota(jnp.int32, (1, LANES), 1)
        tot = jnp.where((lane >= 64) & (lane < 96), tot, 0.0)
        dr_ref[...] = tot * c + _swap_halves_of_rope(tot * s)

    wide = pl.BlockSpec((br, W), lambda i: (i, 0))
    one = pl.BlockSpec((br, LANES), lambda i: (i, 0))
    return _pcall(body, name="mla_prep_bwd", grid=(S // br,), in_specs=[wide, wide, one, one], out_specs=(wide, one),
                  out_shape=(jax.ShapeDtypeStruct((S, W), BF16), jax.ShapeDtypeStruct((S, LANES), F32)),
                  compiler_params=_params(("parallel",)))(dq, dk, cp, sp)


def _shift_down(v, k, row):
    return jnp.where(row >= k, pltpu.roll(v, k, 0), 0.0)


def _shift_up(v, k, row):
    n = v.shape[0]
    return jnp.where(row < n - k, pltpu.roll(v, n - k, 0), 0.0)


def _conv_fwd(z, w):
    S = z.shape[0]
    c0 = Z_CV // LANES

    def body(b_ref, c_ref, h_ref, w_ref, y_ref):
        u = c_ref[...] * h_ref[...]
        row = lax.broadcasted_iota(jnp.int32, u.shape, 0)
        wv = w_ref[...]
        conv = wv[0:1] * _shift_down(u, 2, row) + wv[1:2] * _shift_down(u, 1, row) + wv[2:3] * u
        y_ref[...] = b_ref[...] * conv

    def col(k):
        return pl.BlockSpec((S, LANES), lambda j: (0, c0 + 2 * k + j))

    return _pcall(body, name="conv_fwd", grid=(2,),
                  in_specs=[col(0), col(1), col(2), pl.BlockSpec((3, LANES), lambda j: (0, j))],
                  out_specs=pl.BlockSpec((S, LANES), lambda j: (0, j)),
                  out_shape=jax.ShapeDtypeStruct((S, 256), F32), compiler_params=_params(("parallel",)))(z, z, z, w)


def _conv_bwd(z, w, dc_all):
    S = z.shape[0]
    c0 = Z_CV // LANES

    def body(b_ref, c_ref, h_ref, w_ref, dy_ref, db_ref, dc_ref, dh_ref, dw_ref):
        cv, hv, bv, dyv = c_ref[...], h_ref[...], b_ref[...], dy_ref[...]
        u = cv * hv
        row = lax.broadcasted_iota(jnp.int32, u.shape, 0)
        wv = w_ref[...]
        u1, u2 = _shift_down(u, 1, row), _shift_down(u, 2, row)
        conv = wv[0:1] * u2 + wv[1:2] * u1 + wv[2:3] * u
        db_ref[...] = dyv * conv
        dconv = dyv * bv
        du = wv[2:3] * dconv + wv[1:2] * _shift_up(dconv, 1, row) + wv[0:1] * _shift_up(dconv, 2, row)
        dc_ref[...] = du * hv
        dh_ref[...] = du * cv
        dw_ref[0:1, :] = jnp.sum(dconv * u2, axis=0, keepdims=True)
        dw_ref[1:2, :] = jnp.sum(dconv * u1, axis=0, keepdims=True)
        dw_ref[2:3, :] = jnp.sum(dconv * u, axis=0, keepdims=True)

    def col(k):
        return pl.BlockSpec((S, LANES), lambda j: (0, c0 + 2 * k + j))

    wsp = pl.BlockSpec((3, LANES), lambda j: (0, j))
    osp = pl.BlockSpec((S, LANES), lambda j: (0, j))
    db, dc, dh, dw = _pcall(
        body, name="conv_bwd", grid=(2,),
        in_specs=[col(0), col(1), col(2), wsp, pl.BlockSpec((S, LANES), lambda j: (0, 6 + j))],
        out_specs=(osp, osp, osp, wsp),
        out_shape=(jax.ShapeDtypeStruct((S, 256), F32),) * 3 + (jax.ShapeDtypeStruct((3, 256), F32),),
        compiler_params=_params(("parallel",)))(z, z, z, w, dc_all)
    return db, dc, dh, dw


def _half_masks():
    lane = lax.broadcasted_iota(jnp.int32, (1, LANES), 1)
    return lane < HEAD_DIM, lane >= HEAD_DIM


def _pair(v, p):
    return v[:, p * LANES:(p + 1) * LANES]


def _keep(mask, v):
    return jnp.where(mask, v, jnp.zeros_like(v))


def _lane_col(vals):
    lane = lax.broadcasted_iota(jnp.int32, (1, LANES), 1)
    out = jnp.zeros((vals[0].shape[0], LANES), F32)
    for h, v in enumerate(vals):
        out = jnp.where(lane == h, v, out)
    return out


def _mla_fwd(q, k, kv):
    S = q.shape[0]
    H = MLA_HEADS
    bq, bk = _row_block(S, MLA_BQ), _row_block(S, MLA_BK)
    per = bq // bk

    def body(q_ref, k_ref, v_ref, o_ref, lse_ref):
        i = pl.program_id(0)
        lo, hi = _half_masks()
        qb = q_ref[...]

        def block(j, carry, ok):
            ms, ls, accs = carry
            off = pl.multiple_of(j * bk, bk)
            kb = k_ref[pl.ds(off, bk), :]
            vb = v_ref[pl.ds(off, bk), :].astype(BF16)
            ss = [_dot(_pair(qb, h), _pair(kb, h), NT) * MLA_SCALE for h in range(H)]
            if ok is not None:
                ss = [jnp.where(ok, s, NEG_INF) for s in ss]
            ms2 = [jnp.maximum(ms[h], jnp.max(ss[h], axis=-1, keepdims=True)) for h in range(H)]
            ps = [jnp.exp(ss[h] - ms2[h]) for h in range(H)]
            al = [jnp.exp(ms[h] - ms2[h]) for h in range(H)]
            ls2 = [al[h] * ls[h] + jnp.sum(ps[h], axis=-1, keepdims=True) for h in range(H)]
            pvs = [_dot(ps[h], _pair(vb, h // 2)) for h in range(H)]
            accs2 = []
            for p in range(H // 2):
                scale = jnp.where(lo, al[2 * p], al[2 * p + 1])
                accs2.append(scale * accs[p] + jnp.where(lo, pvs[2 * p], pvs[2 * p + 1]))
            return tuple(ms2), tuple(ls2), tuple(accs2)

        init = (tuple(jnp.full((bq, 1), NEG_INF, F32) for _ in range(H)),
                tuple(jnp.zeros((bq, 1), F32) for _ in range(H)),
                tuple(jnp.zeros((bq, LANES), F32) for _ in range(H // 2)))
        carry = lax.fori_loop(0, i * per, lambda j, c: block(j, c, None), init)
        row = lax.broadcasted_iota(jnp.int32, (bq, bk), 0)
        col = lax.broadcasted_iota(jnp.int32, (bq, bk), 1)
        for t in range(per):
            carry = block(i * per + t, carry, col + t * bk <= row)
        ms, ls, accs = carry
        o_ref[...] = jnp.concatenate(
            [accs[p] / jnp.where(lo, ls[2 * p], ls[2 * p + 1]) for p in range(H // 2)], axis=1)
        lse_ref[...] = _lane_col([ms[h] + jnp.log(ls[h]) for h in range(H)])

    return _pcall(body, name="mla_fwd", grid=(S // bq,),
                  in_specs=[pl.BlockSpec((bq, H * LANES), lambda i: (i, 0)),
                            pl.BlockSpec((S, H * LANES), lambda i: (0, 0)),
                            pl.BlockSpec((S, H * HEAD_DIM), lambda i: (0, 2))],
                  out_specs=(pl.BlockSpec((bq, H * HEAD_DIM), lambda i: (i, 0)),
                             pl.BlockSpec((bq, LANES), lambda i: (i, 0))),
                  out_shape=(jax.ShapeDtypeStruct((S, H * HEAD_DIM), F32), jax.ShapeDtypeStruct((S, LANES), F32)),
                  compiler_params=_params(("parallel",)))(q, k, kv)


def _mla_bwd(q, k, kv, o, lse, dc_all):
    S = q.shape[0]
    H = MLA_HEADS
    bq, bk = _row_block(S, MLA_BQ), _row_block(S, MLA_BK)
    per = bq // bk

    def body(q_ref, k_ref, v_ref, o_ref, lse_ref, doa_ref, dob_ref, dq_ref, dk_ref, dv_ref):
        i = pl.program_id(0)

        @pl.when(i == 0)
        def _():
            dk_ref[...] = jnp.zeros_like(dk_ref)
            dv_ref[...] = jnp.zeros_like(dv_ref)

        lo, hi = _half_masks()
        qb = q_ref[...]
        dob = jnp.concatenate([doa_ref[...], dob_ref[...]], axis=1)
        prod = dob * o_ref[...]
        lse_v = lse_ref[...]
        do16 = dob.astype(BF16)
        dom, deltas, lses = [], [], []
        for h in range(H):
            mk = lo if h % 2 == 0 else hi
            dom.append(_keep(mk, _pair(do16, h // 2)))
            deltas.append(jnp.sum(_keep(mk, _pair(prod, h // 2)), axis=-1, keepdims=True))
            lses.append(lse_v[:, h:h + 1])

        def block(j, dqs, ok):
            off = pl.multiple_of(j * bk, bk)
            kb = k_ref[pl.ds(off, bk), :]
            vb = v_ref[pl.ds(off, bk), :].astype(BF16)
            ss = [_dot(_pair(qb, h), _pair(kb, h), NT) * MLA_SCALE for h in range(H)]
            dps = [_dot(dom[h], _pair(vb, h // 2), NT) for h in range(H)]
            ps = [jnp.exp(ss[h] - lses[h]) for h in range(H)]
            if ok is not None:
                ps = [jnp.where(ok, p, 0.0) for p in ps]
            ds16 = [(ps[h] * (dps[h] - deltas[h]) * MLA_SCALE).astype(BF16) for h in range(H)]
            p16 = [p.astype(BF16) for p in ps]
            dks = [_dot(ds16[h], _pair(qb, h), TN) for h in range(H)]
            dvps = [_dot(p16[h], dom[h], TN) for h in range(H)]
            dqs2 = [dqs[h] + _dot(ds16[h], _pair(kb, h)) for h in range(H)]
            dk_ref[pl.ds(off, bk), :] += jnp.concatenate(dks, axis=1)
            dv_ref[pl.ds(off, bk), :] += jnp.concatenate(
                [dvps[2 * p] + dvps[2 * p + 1] for p in range(H // 2)], axis=1)
            return tuple(dqs2)

        dqs = lax.fori_loop(0, i * per, lambda j, c: block(j, c, None),
                            tuple(jnp.zeros((bq, LANES), F32) for _ in range(H)))
        row = lax.broadcasted_iota(jnp.int32, (bq, bk), 0)
        col = lax.broadcasted_iota(jnp.int32, (bq, bk), 1)
        for t in range(per):
            dqs = block(i * per + t, dqs, col + t * bk <= row)
        dq_ref[...] = jnp.concatenate(dqs, axis=1)

    wide = pl.BlockSpec((bq, H * LANES), lambda i: (i, 0))
    full = pl.BlockSpec((S, H * LANES), lambda i: (0, 0))
    return _pcall(body, name="mla_bwd", grid=(S // bq,),
                  in_specs=[wide, full, pl.BlockSpec((S, H * HEAD_DIM), lambda i: (0, 2)),
                            pl.BlockSpec((bq, H * HEAD_DIM), lambda i: (i, 0)),
                            pl.BlockSpec((bq, LANES), lambda i: (i, 0)),
                            pl.BlockSpec((bq, 256), lambda i: (i, 1)), pl.BlockSpec((bq, 256), lambda i: (i, 2))],
                  out_specs=(wide, full, pl.BlockSpec((S, H * HEAD_DIM), lambda i: (0, 0))),
                  out_shape=(jax.ShapeDtypeStruct((S, H * LANES), F32), jax.ShapeDtypeStruct((S, H * LANES), F32),
                             jax.ShapeDtypeStruct((S, H * HEAD_DIM), F32)),
                  compiler_params=_params(("arbitrary",)))(q, k, kv, o, lse, dc_all, dc_all)


def _dot_exact(x, tri):
    h1 = x.astype(BF16)
    h2 = (x - h1.astype(F32)).astype(BF16)
    return _dot(h1, tri) + _dot(h2, tri)


def _softplus(z):
    return jnp.maximum(z, 0.0) + jnp.log(1.0 + jnp.exp(-jnp.abs(z)))


def _sb_fwd(z):
    S = z.shape[0]
    H = SB_HEADS
    bq, bk = _row_block(S, SB_BQ), _row_block(S, SB_BK)
    per = bq // bk

    def body(q_ref, k_ref, v_ref, o_ref, t_ref):
        i = pl.program_id(0)
        lo, hi = _half_masks()
        q16 = (q_ref[...] * SB_SCALE).astype(BF16)
        qm = [_keep(lo if h % 2 == 0 else hi, _pair(q16, h // 2)) for h in range(H)]
        rr = lax.broadcasted_iota(jnp.int32, (bk, bk), 0)
        cc = lax.broadcasted_iota(jnp.int32, (bk, bk), 1)
        later = (rr > cc).astype(BF16)

        def block(j, carry, mask):
            tails, accs = carry
            off = pl.multiple_of(j * bk, bk)
            kb = k_ref[pl.ds(off, bk), :].astype(BF16)
            vb = v_ref[pl.ds(off, bk), :].astype(BF16)
            zs = [_dot(qm[h], _pair(kb, h // 2), NT) for h in range(H)]
            sps = [_softplus(z) for z in zs]
            lnots = [-sp if mask is None else jnp.where(mask, -sp, 0.0) for sp in sps]
            sums = [_dot_exact(lnot, later) for lnot in lnots]
            ws = []
            for h in range(H):
                w = jnp.exp((zs[h] - sps[h]) + (tails[h] + sums[h]))
                ws.append((w if mask is None else jnp.where(mask, w, 0.0)).astype(BF16))
            pvs = [_dot(ws[h], _pair(vb, h // 2)) for h in range(H)]
            accs2 = [accs[p] + jnp.where(lo, pvs[2 * p], pvs[2 * p + 1]) for p in range(H // 2)]
            tails2 = [tails[h] + jnp.sum(lnots[h], axis=-1, keepdims=True) for h in range(H)]
            return tuple(tails2), tuple(accs2)

        carry = (tuple(jnp.zeros((bq, 1), F32) for _ in range(H)),
                 tuple(jnp.zeros((bq, LANES), F32) for _ in range(H // 2)))
        row = lax.broadcasted_iota(jnp.int32, (bq, bk), 0)
        col = lax.broadcasted_iota(jnp.int32, (bq, bk), 1)
        for t in range(per):
            carry = block(i * per + per - 1 - t, carry, col + (per - 1 - t) * bk < row)
        tails, accs = lax.fori_loop(0, i * per, lambda t, c: block(i * per - 1 - t, c, None), carry)
        o_ref[...] = jnp.concatenate(accs, axis=1)
        t_ref[...] = _lane_col(tails)

    return _pcall(body, name="sb_fwd", grid=(S // bq,),
                  in_specs=[pl.BlockSpec((bq, 256), lambda i: (i, 0)), pl.BlockSpec((S, 256), lambda i: (0, 1)),
                            pl.BlockSpec((S, 256), lambda i: (0, 2))],
                  out_specs=(pl.BlockSpec((bq, 256), lambda i: (i, 0)), pl.BlockSpec((bq, LANES), lambda i: (i, 0))),
                  out_shape=(jax.ShapeDtypeStruct((S, 256), F32), jax.ShapeDtypeStruct((S, LANES), F32)),
                  compiler_params=_params(("parallel",)))(z, z, z)


def _sb_bwd(z, tot, dc_all):
    S = z.shape[0]
    H = SB_HEADS
    bq, bk = _row_block(S, SB_BQ), _row_block(S, SB_BK)
    per = bq // bk

    def body(q_ref, k_ref, v_ref, t_ref, do_ref, dq_ref, dk_ref, dv_ref):
        i = pl.program_id(0)

        @pl.when(i == 0)
        def _():
            dk_ref[...] = jnp.zeros_like(dk_ref)
            dv_ref[...] = jnp.zeros_like(dv_ref)

        lo, hi = _half_masks()
        q16 = (q_ref[...] * SB_SCALE).astype(BF16)
        do16 = do_ref[...].astype(BF16)
        tot_v = t_ref[...]
        masks = [lo if h % 2 == 0 else hi for h in range(H)]
        qm = [_keep(masks[h], _pair(q16, h // 2)) for h in range(H)]
        dom = [_keep(masks[h], _pair(do16, h // 2)) for h in range(H)]
        tots = [tot_v[:, h:h + 1] for h in range(H)]
        rr = lax.broadcasted_iota(jnp.int32, (bk, bk), 0)
        cc = lax.broadcasted_iota(jnp.int32, (bk, bk), 1)
        upto = (rr <= cc).astype(BF16)
        before = (rr < cc).astype(BF16)

        def block(j, carry, mask):
            heads, eheads, dqs = carry
            off = pl.multiple_of(j * bk, bk)
            kb = k_ref[pl.ds(off, bk), :].astype(BF16)
            vb = v_ref[pl.ds(off, bk), :].astype(BF16)
            zs = [_dot(qm[h], _pair(kb, h // 2), NT) for h in range(H)]
            dws = [_dot(dom[h], _pair(vb, h // 2), NT) for h in range(H)]
            sps = [_softplus(z) for z in zs]
            lnots = [-sp if mask is None else jnp.where(mask, -sp, 0.0) for sp in sps]
            pres = [_dot_exact(lnot, upto) for lnot in lnots]
            lsigs = [zs[h] - sps[h] for h in range(H)]
            ws = [jnp.exp(lsigs[h] + (tots[h] - (heads[h] + pres[h]))) for h in range(H)]
            if mask is not None:
                ws = [jnp.where(mask, w, 0.0) for w in ws]
            es = [ws[h] * dws[h] for h in range(H)]
            esums = [eheads[h] + _dot_exact(es[h], before) for h in range(H)]
            dz16 = []
            for h in range(H):
                sig = jnp.exp(lsigs[h])
                dz = es[h] * (1.0 - sig) - sig * esums[h]
                dz16.append((dz if mask is None else jnp.where(mask, dz, 0.0)).astype(BF16))
            w16 = [w.astype(BF16) for w in ws]
            dkps = [_dot(dz16[h], qm[h], TN) for h in range(H)]
            dvps = [_dot(w16[h], dom[h], TN) for h in range(H)]
            dqps = [_dot(dz16[h], _pair(kb, h // 2)) for h in range(H)]
            dk_ref[pl.ds(off, bk), :] += jnp.concatenate([dkps[2 * p] + dkps[2 * p + 1] for p in range(H // 2)], axis=1)
            dv_ref[pl.ds(off, bk), :] += jnp.concatenate([dvps[2 * p] + dvps[2 * p + 1] for p in range(H // 2)], axis=1)
            dqs2 = [dqs[p] + jnp.where(lo, dqps[2 * p], dqps[2 * p + 1]) for p in range(H // 2)]
            heads2 = [heads[h] + jnp.sum(lnots[h], axis=-1, keepdims=True) for h in range(H)]
            eheads2 = [eheads[h] + jnp.sum(es[h], axis=-1, keepdims=True) for h in range(H)]
            return tuple(heads2), tuple(eheads2), tuple(dqs2)

        zeros = tuple(jnp.zeros((bq, 1), F32) for _ in range(H))
        init = (zeros, zeros, tuple(jnp.zeros((bq, LANES), F32) for _ in range(H // 2)))
        carry = lax.fori_loop(0, i * per, lambda j, c: block(j, c, None), init)
        row = lax.broadcasted_iota(jnp.int32, (bq, bk), 0)
        col = lax.broadcasted_iota(jnp.int32, (bq, bk), 1)
        for t in range(per):
            carry = block(i * per + t, carry, col + t * bk < row)
        dq_ref[...] = jnp.concatenate(carry[2], axis=1) * SB_SCALE

    blk = pl.BlockSpec((bq, 256), lambda i: (i, 0))
    full = pl.BlockSpec((S, 256), lambda i: (0, 0))
    return _pcall(body, name="sb_bwd", grid=(S // bq,),
                  in_specs=[blk, pl.BlockSpec((S, 256), lambda i: (0, 1)), pl.BlockSpec((S, 256), lambda i: (0, 2)),
                            pl.BlockSpec((bq, LANES), lambda i: (i, 0)), blk],
                  out_specs=(blk, full, full), out_shape=(jax.ShapeDtypeStruct((S, 256), F32),) * 3,
                  compiler_params=_params(("arbitrary",)))(z, z, z, tot, dc_all)


_HBM = pl.BlockSpec(memory_space=pltpu.HBM)
_SEM = pl.BlockSpec(memory_space=pltpu.SEMAPHORE)
_ANY = pl.BlockSpec(memory_space=pl.ANY)


def _place():
    x, y, c = lax.axis_index("x"), lax.axis_index("y"), lax.axis_index("c")
    other_chips = ((1 - x, y), (x, 1 - y), (1 - x, 1 - y))
    return x, y, c, other_chips


def _remote(src, dst, send_sems, recv_sems, k, dev):
    return pltpu.make_async_remote_copy(src_ref=src, dst_ref=dst, send_sem=send_sems.at[k], recv_sem=recv_sems.at[k],
                                        device_id=dev, device_id_type=MESH)


def _half_rows(c, n_rows):
    half = n_rows // 2
    return pl.ds(pl.multiple_of(c * half, 16), half)


def _gather_copies(src, land, send_sems, recv_sems):
    x, y, c, chips = _place()
    me = 2 * x + y
    cps = []
    for w in range(len(src)):
        rows = _half_rows(c, src[w].shape[0])
        for k, (cx, cy) in enumerate(chips):
            cps.append(_remote(src[w].at[rows], land[w].at[me, rows], send_sems, recv_sems, 3 * w + k, (cx, cy, c)))
    return cps


def _scatter_copies(src, land, send_sems, recv_sems):
    x, y, c, chips = _place()
    cps = []
    for w in range(len(src)):
        for k, (cx, cy) in enumerate(chips):
            cps.append(_remote(src[w].at[2 * cx + cy], land[w].at[k], send_sems, recv_sems, 3 * w + k, (cx, cy, c)))
    return cps


def _exchange_start(copies, srcs, lands, after, name):
    n = len(srcs)

    def body(*refs):
        src, land = refs[:n], refs[n:2 * n]
        send_sems, recv_sems = refs[2 * n + 1], refs[2 * n + 2]
        token = refs[-1]
        for cp in copies(src, land, send_sems, recv_sems):
            cp.start()
        token[...] = jnp.zeros_like(token)

    thru = tuple(pltpu.HBM(a.shape, a.dtype) for a in list(srcs) + list(lands))
    out = _pcall(body, name=name,
                 out_shape=(pltpu.SemaphoreType.DMA((3 * n,)), pltpu.SemaphoreType.DMA((3 * n,))) + thru
                 + (jax.ShapeDtypeStruct((8, LANES), F32),),
                 in_specs=[_HBM] * (2 * n) + [_ANY],
                 out_specs=(_SEM, _SEM) + (_HBM,) * (2 * n) + (pl.BlockSpec(memory_space=pltpu.VMEM),),
                 input_output_aliases={i: 2 + i for i in range(2 * n)},
                 compiler_params=pltpu.CompilerParams(has_side_effects=pltpu.SideEffectType.DATAFLOW_SIDE_EFFECTING))(
        *[pltpu.with_memory_space_constraint(a, pltpu.HBM) for a in list(srcs) + list(lands)], after)
    return out[0], out[1], list(out[2:2 + n]), list(out[2 + n:2 + 2 * n]), out[-1]


def _exchange_wait(copies, send_sems, recv_sems, srcs, lands, after, name):
    n = len(srcs)

    def body(*refs):
        src, land = refs[:n], refs[n:2 * n]
        ssem, rsem = refs[2 * n], refs[2 * n + 1]
        for cp in copies(src, land, ssem, rsem):
            cp.wait_send()
            cp.wait_recv()

    out = _pcall(body, name=name, out_shape=tuple(pltpu.HBM(a.shape, a.dtype) for a in list(srcs) + list(lands)),
                 in_specs=[_HBM] * (2 * n) + [_SEM, _SEM, _ANY], out_specs=(_HBM,) * (2 * n),
                 input_output_aliases={i: i for i in range(2 * n)},
                 compiler_params=pltpu.CompilerParams(has_side_effects=pltpu.SideEffectType.DATAFLOW_SIDE_EFFECTING))(
        *srcs, *lands, send_sems, recv_sems, after)
    return list(out[:n]), list(out[n:])


def _forward_rows(gathered):
    n = len(gathered)

    def body(*refs):
        src, dst = refs[:n], refs[n:2 * n]
        send_sems, recv_sems = refs[2 * n:]
        x, y, c, chips = _place()
        sib = (x, y, 1 - c)
        cps = []
        for w in range(n):
            K = src[w].shape[1]
            for k, (cx, cy) in enumerate(chips):
                blk = (2 * cx + cy, _half_rows(c, K))
                cps.append(_remote(src[w].at[blk], dst[w].at[blk], send_sems, recv_sems, 3 * w + k, sib))
        for cp in cps:
            cp.start()
        for w in range(n):
            K = src[w].shape[1]
            for k, (cx, cy) in enumerate(chips):
                blk = dst[w].at[2 * cx + cy, _half_rows(1 - c, K)]
                _remote(blk, blk, send_sems, recv_sems, 3 * w + k, sib).wait_recv()
        for cp in cps:
            cp.wait_send()

    return _pcall(body, name="forward_rows", in_specs=[_HBM] * n, out_specs=(_HBM,) * n,
                  out_shape=tuple(jax.ShapeDtypeStruct(g.shape, g.dtype) for g in gathered),
                  input_output_aliases={w: w for w in range(n)},
                  scratch_shapes=[pltpu.SemaphoreType.DMA((3 * n,)), pltpu.SemaphoreType.DMA((3 * n,))])(*gathered)


def _swap_rows(bufs):
    n = len(bufs)

    def body(*refs):
        src, dst = refs[:n], refs[n:2 * n]
        send_sems, recv_sems = refs[2 * n:]
        x, y, c, _ = _place()
        cps = [_remote(src[w].at[:, _half_rows(1 - c, src[w].shape[1])], dst[w], send_sems, recv_sems, w,
                       (x, y, 1 - c)) for w in range(n)]
        for cp in cps:
            cp.start()
        for cp in cps:
            cp.wait()

    return _pcall(body, name="swap_rows", in_specs=[_HBM] * n, out_specs=(_HBM,) * n,
                  out_shape=tuple(jax.ShapeDtypeStruct((b.shape[0], b.shape[1] // 2, b.shape[2]), b.dtype) for b in bufs),
                  scratch_shapes=[pltpu.SemaphoreType.DMA((n,)), pltpu.SemaphoreType.DMA((n,))])(*bufs)


def _join_rows(fins):
    n = len(fins)

    def body(*refs):
        src, dst = refs[:n], refs[n:2 * n]
        send_sems, recv_sems = refs[2 * n:]
        x, y, c, _ = _place()
        sib = (x, y, 1 - c)
        cps = []
        for w in range(n):
            blk = (slice(None), _half_rows(c, src[w].shape[1]))
            cps.append(_remote(src[w].at[blk], dst[w].at[blk], send_sems, recv_sems, w, sib))
        for cp in cps:
            cp.start()
        for w in range(n):
            blk = dst[w].at[:, _half_rows(1 - c, src[w].shape[1])]
            _remote(blk, blk, send_sems, recv_sems, w, sib).wait_recv()
        for cp in cps:
            cp.wait_send()

    return _pcall(body, name="join_rows", in_specs=[_HBM] * n, out_specs=(_HBM,) * n,
                  out_shape=tuple(jax.ShapeDtypeStruct(f.shape, f.dtype) for f in fins),
                  input_output_aliases={w: w for w in range(n)},
                  scratch_shapes=[pltpu.SemaphoreType.DMA((n,)), pltpu.SemaphoreType.DMA((n,))])(*fins)


def _add_rows(bufs, gots, c_idx):
    n = len(bufs)
    bks = [_row_block(g.shape[1], 192) for g in gots]
    nbs = [g.shape[1] // bk for g, bk in zip(gots, bks)]

    def body(c_ref, *refs):
        i = pl.program_id(1)
        for w in range(n):
            @pl.when(i < nbs[w])
            def _(w=w):
                refs[2 * n + w][...] = (refs[w][...].astype(F32) + refs[n + w][...].astype(F32)).astype(BF16)

    def mine(w):
        return pl.BlockSpec((None, bks[w], gots[w].shape[2]),
                            lambda j, i, c_ref: (j, c_ref[0] * nbs[w] + jnp.minimum(i, nbs[w] - 1), 0))

    def half(w):
        return pl.BlockSpec((None, bks[w], gots[w].shape[2]), lambda j, i, c_ref: (j, jnp.minimum(i, nbs[w] - 1), 0))

    grid_spec = pltpu.PrefetchScalarGridSpec(
        num_scalar_prefetch=1, grid=(N_CHIPS, max(nbs)),
        in_specs=[mine(w) for w in range(n)] + [half(w) for w in range(n)], out_specs=[half(w) for w in range(n)])
    return _pcall(body, name="add_rows", grid_spec=grid_spec,
                  out_shape=[jax.ShapeDtypeStruct(g.shape, BF16) for g in gots],
                  compiler_params=_params(("parallel", "arbitrary")))(c_idx, *bufs, *gots)


def _add_chip_sums(s1s, gots, idx, layer, intos, after):
    n = len(s1s)
    bks = [_row_block(a.shape[1], 64) for a in s1s]
    nbs = [a.shape[1] // bk for a, bk in zip(s1s, bks)]
    fresh = intos[0] is None

    def body(idx_ref, *refs):
        i = pl.program_id(0)
        outs = refs[len(refs) - n:]
        for w in range(n):
            @pl.when(i < nbs[w])
            def _(w=w):
                acc = refs[w][...].astype(F32)
                for k in range(3):
                    acc = acc + refs[n + w][k].astype(F32)
                outs[w][...] = acc

    def blk(w, rows):
        return pl.BlockSpec((rows, bks[w], s1s[w].shape[2]),
                            lambda i, idx_ref: (0 if rows else idx_ref[0], jnp.minimum(i, nbs[w] - 1), 0))

    def out(w):
        return pl.BlockSpec((None, bks[w], s1s[w].shape[2]),
                            lambda i, idx_ref: (layer, idx_ref[1] * nbs[w] + jnp.minimum(i, nbs[w] - 1), 0))

    ops = [idx, *s1s, *gots, after] + ([] if fresh else list(intos))
    in_specs = [blk(w, None) for w in range(n)] + [blk(w, 3) for w in range(n)] + [_ANY] * (1 if fresh else 1 + n)
    grid_spec = pltpu.PrefetchScalarGridSpec(num_scalar_prefetch=1, grid=(max(nbs),), in_specs=in_specs,
                                             out_specs=[out(w) for w in range(n)])
    return _pcall(body, name="add_chip_sums" if fresh else "add_chip_sums_into", grid_spec=grid_spec,
                  out_shape=[jax.ShapeDtypeStruct((2, 2 * a.shape[1], a.shape[2]), F32) for a in s1s],
                  input_output_aliases={} if fresh else {2 * n + 2 + w: w for w in range(n)},
                  compiler_params=_params(("arbitrary",)))(*ops)


def _allreduce_small(v, after):
    R, W = v.shape

    def body(v_ref, after_ref, o_ref, buf, send_sems, recv_sems):
        x, y, c, _ = _place()
        me = 4 * x + 2 * y + c
        buf[0] = v_ref[...]
        cps = []
        for r in range(1, 8):
            peer = (x if not r & 4 else 1 - x, y if not r & 2 else 1 - y, c if not r & 1 else 1 - c)
            cp = _remote(v_ref, buf.at[r], send_sems, recv_sems, r - 1, peer)
            cp.start()
            cps.append(cp)
        for cp in cps:
            cp.wait()
        acc = buf[me]
        for d in range(1, 8):
            acc = acc + buf[jnp.bitwise_xor(me, d)]
        o_ref[...] = acc

    return _pcall(body, name="allreduce_small", out_shape=jax.ShapeDtypeStruct((R, W), F32),
                  in_specs=[pl.BlockSpec(memory_space=pltpu.VMEM), _ANY],
                  out_specs=pl.BlockSpec(memory_space=pltpu.VMEM),
                  scratch_shapes=[pltpu.VMEM((8, R, W), F32), pltpu.SemaphoreType.DMA((7,)),
                                  pltpu.SemaphoreType.DMA((7,))])(v, after)


def _cat_cols(g):
    return g.transpose(1, 0, 2).reshape(g.shape[1], -1)


def _cut_cols(w):
    K, N = w.shape
    return w.reshape(K, N_CHIPS, N // N_CHIPS).transpose(1, 0, 2)


def _regroup_w_in_t(g):
    w = g[:, :552].reshape(2208, g.shape[2])
    zeros = lambda n: jnp.zeros((n, w.shape[1]), w.dtype)
    return jnp.concatenate([w[0:768], w[1152:1408], zeros(64), w[1408:1440], zeros(32), w[768:1152], w[1440:2208]],
                           axis=0)


def _ungroup_w_in_t(w):
    nat = jnp.concatenate([w[0:768], w[Z_Q:Z_Q + 384], w[Z_KV:Z_KV + 256], w[Z_KR + 64:Z_KR + 96], w[Z_CV:Z_W]],
                          axis=0)
    return jnp.pad(nat.reshape(N_CHIPS, 552, w.shape[1]), ((0, 0), (0, 24), (0, 0)))


def _regroup_w_uq_t(g):
    K = g.shape[2]
    return jnp.pad(g.reshape(MLA_HEADS, 96, K), ((0, 0), (0, 32), (0, 0))).reshape(MLA_HEADS * LANES, K)


def _ungroup_w_uq_t(w):
    K = w.shape[1]
    return w.reshape(MLA_HEADS, LANES, K)[:, :96].reshape(N_CHIPS, 192, K)


def _regroup_w_ukv(w):
    K = w.shape[0]
    t = w.reshape(K, MLA_HEADS, 128)
    return jnp.concatenate([jnp.pad(t[:, :, :64], ((0, 0), (0, 0), (0, 64))).reshape(K, MLA_HEADS * LANES),
                            t[:, :, 64:].reshape(K, MLA_HEADS * HEAD_DIM)], axis=1)


def _ungroup_w_ukv(w):
    K = w.shape[0]
    return jnp.concatenate([w[:, :MLA_HEADS * LANES].reshape(K, MLA_HEADS, LANES)[:, :, :64],
                            w[:, MLA_HEADS * LANES:].reshape(K, MLA_HEADS, 64)], axis=2).reshape(K, 1024)


def _ffn_fwd(x, g_pre, g_post, fetch, tag, h, g_next):
    W = fetch("ffn%sa" % tag, x)
    if h is None:
        h = _norm_fwd(x, g_pre, "norm_fwd")
    G, U, A = _ffn_gate_up(h, W["w%s_gate" % tag], W["w%s_up" % tag], "ffn_gate_up")
    W = fetch("ffn%sb" % tag, A)
    Y = _mm([(A, W["w%s_down" % tag])], "nn", F32, "ffn_down", bm=1024)
    x_new, h_next = _resid_norm(x, Y, g_post, 0.5, "resid_norm", g_next)
    return x_new, dict(x=x, h=h, G=G, U=U, A=A, Y=Y), h_next


def _ffn_bwd(dxo, sv, g_pre, g_post, wg, wu, wd_rows, ready):
    dY, dg_post = _norm_bwd(sv["Y"], g_post, dxo, 0.5, None, BF16, "norm_bwd_post")
    dG, dU = _ffn_bwd_mid(dY, wd_rows, sv["G"], sv["U"], "ffn_bwd_mid")
    cut = lambda t: t.reshape(N_CHIPS, -1, t.shape[1])
    token = ready((cut(_mm([(dG, sv["h"])], "tn", BF16, "ffn_dw_in")), cut(_mm([(dU, sv["h"])], "tn", BF16, "ffn_dw_in")),
                   cut(_mm([(sv["A"], dY)], "tn", BF16, "ffn_dw_down"))))
    dh = _mm([(dG, wg), (dU, wu)], "nn", F32, "ffn_dh")
    dx, dg_pre = _norm_bwd(sv["x"], _ordered_after(g_pre, token), dh, 1.0, dxo, F32, "norm_bwd_pre")
    return dx, dg_pre, dg_post, token


def _ordered_after(g, token):
    return g if token is None else g + token[0, 0]


GATHER_GROUPS = (("ffn1a", ("w1_gate", "w1_up")), ("ffn1b", ("w1_down",)),
                 ("mix", ("w_in", "w_mla_uq", "w_mla_ukv", "w_out")),
                 ("ffn2a", ("w2_gate", "w2_up", "w2_down", "w_ple_gate", "w_ple_proj")))
SCATTER_GROUPS = (("ffn2", ("w2_gate", "w2_up", "w2_down", "w_ple_gate", "w_ple_proj")),
                  ("mix", ("w_in", "w_mla_uq", "w_mla_ukv", "w_out")),
                  ("ffn1", ("w1_gate", "w1_up", "w1_down")))


def _layer_fwd(x, p_l, fetch, G, w_conv, cp, sp, h_in, g_next):
    sv = {}
    x1, sv["ffn1"], h2 = _ffn_fwd(x, G["g_ffn1_pre"], G["g_ffn1_post"], fetch, "1", h_in, G["g_mix_pre"])

    W = fetch("mix", x1)
    Z = _mm([(h2, W["w_in"])], "nt", F32, "mix_in", bm=1024, bn=768)
    o_sb, tot_sb = _sb_fwd(Z)
    cq, ckv = Z[:, Z_Q:Z_Q + 384], Z[:, Z_KV:Z_KV + 256]
    nq = _norm_fwd(cq, G["g_mla_q"], "norm_fwd_q")
    Qf = _mm([(nq, W["w_mla_uq"])], "nt", F32, "mla_uq")
    nkv = _norm_fwd(ckv, G["g_mla_kv"], "norm_fwd_kv")
    KV = _mm([(nkv, W["w_mla_ukv"])], "nn", F32, "mla_ukv")
    qcat, kcat = _mla_prep(Qf, KV, Z, cp, sp)
    o_mla, lse = _mla_fwd(qcat, kcat, KV)
    y_cv = _conv_fwd(Z, w_conv)
    C = jnp.concatenate([o_sb, o_mla, y_cv], axis=1).astype(BF16)
    Mx = _mm([(C, W["w_out"])], "nn", F32, "mix_out", bm=1024, bn=1024)
    x2, h3 = _resid_norm(x1, Mx, G["g_mix_post"], 1.0, "resid_norm", G["g_ffn2_pre"])
    sv["mix"] = dict(x=x1, h=h2, Z=Z, tot_sb=tot_sb, cq=cq, ckv=ckv, nq=nq, nkv=nkv, qcat=qcat, kcat=kcat, KV=KV,
                     o_mla=o_mla, lse=lse, C=C, Mx=Mx)

    x3, sv["ffn2"], h4 = _ffn_fwd(x2, G["g_ffn2_pre"], G["g_ffn2_post"], fetch, "2", h3, G["g_ple_pre"])
    W = fetch("ffn2b", x3)

    p16 = p_l.astype(BF16)
    Qg = _mm([(h4, W["w_ple_gate"])], "nn", F32, "ple_gate", bm=1024, bn=1024)
    Pp = _ple_proj(p16, W["w_ple_proj"])
    e = _ew(lambda q, pp: _sigmoid(q) * pp, [Qg, Pp], [(D_MODEL, F32)], "ple_mul")
    x4, h_next = _resid_norm(x3, e, G["g_ple_post"], 1.0, "resid_norm", g_next)
    sv["ple"] = dict(x=x3, h=h4, p16=p16, Qg=Qg, Pp=Pp, e=e)
    sv["W"] = W
    return x4, sv, h_next


def _layer_bwd(dx4, sv, G, w_conv, cp, sp, emit, token):
    gg, gw, W = {}, {}, sv["W"]

    s = sv["ple"]
    de, gg["g_ple_post"] = _norm_bwd(s["e"], _ordered_after(G["g_ple_post"], token), dx4, 1.0, None, F32,
                                     "norm_bwd_e")

    def ple_bwd(dev, q, pp):
        sg = _sigmoid(q)
        return dev * pp * sg * (1.0 - sg), dev * sg

    dQg, dPp = _ew(ple_bwd, [de, s["Qg"], s["Pp"]], [(D_MODEL, BF16)] * 2, "ple_mul_bwd")
    cut = lambda t: t.reshape(N_CHIPS, -1, t.shape[1])
    gw["w_ple_proj"] = _dw_col_shards(s["p16"], dPp, "ple_dw_proj")
    gw["w_ple_gate"] = cut(_mm([(s["h"], dQg)], "tn", BF16, "ple_dw_gate"))
    dh4 = _mm([(dQg, W["w_ple_gate"])], "nt", F32, "ple_dh", bm=1024, bn=1024)
    dx3, gg["g_ple_pre"] = _norm_bwd(s["x"], G["g_ple_pre"], dh4, 1.0, dx4, F32, "norm_bwd_pre")

    def ready2(dws):
        gw["w2_gate"], gw["w2_up"], gw["w2_down"] = dws
        return emit("ffn2", gw)

    dx2, gg["g_ffn2_pre"], gg["g_ffn2_post"], token = _ffn_bwd(
        dx3, sv["ffn2"], G["g_ffn2_pre"], G["g_ffn2_post"], W["w2_gate"], W["w2_up"], W["w2_down"], ready2)

    s = sv["mix"]
    dM, gg["g_mix_post"] = _norm_bwd(s["Mx"], _ordered_after(G["g_mix_post"], token), dx2, 1.0, None, BF16,
                                     "norm_bwd_post")
    dC = _mm([(dM, W["w_out"])], "nt", F32, "mix_out_dx", bm=1024, bn=1024)
    gw["w_out"] = cut(_mm([(s["C"], dM)], "tn", BF16, "mix_out_dw"))

    db, dc, dhh, gg["w_conv"] = _conv_bwd(s["Z"], w_conv, dC)

    dqc, dkc, dv = _mla_bwd(s["qcat"], s["kcat"], s["KV"], s["o_mla"], s["lse"], dC)
    dQf, dkr = _mla_prep_bwd(dqc, dkc, cp, sp)
    gw["w_mla_uq"] = _ungroup_w_uq_t(_mm([(dQf, s["nq"])], "tn", BF16, "mla_uq_dw"))
    dnq = _mm([(dQf, W["w_mla_uq"])], "nn", F32, "mla_uq_dx")
    dcq, gg["g_mla_q"] = _norm_bwd(s["cq"], G["g_mla_q"], dnq, 1.0, None, F32, "norm_bwd_q")
    dkv = jnp.concatenate([dkc, dv], axis=1).astype(BF16)
    gw["w_mla_ukv"] = _cut_cols(_ungroup_w_ukv(_mm([(s["nkv"], dkv)], "tn", BF16, "mla_ukv_dw")))
    dnkv = _mm([(dkv, W["w_mla_ukv"])], "nt", F32, "mla_ukv_dx")
    dckv, gg["g_mla_kv"] = _norm_bwd(s["ckv"], G["g_mla_kv"], dnkv, 1.0, None, F32, "norm_bwd_kv")

    dsq, dsk, dsv = _sb_bwd(s["Z"], s["tot_sb"], dC)
    dZ = jnp.concatenate([dsq, dsk, dsv, dckv, dkr, dcq, db, dc, dhh], axis=1).astype(BF16)
    gw["w_in"] = _ungroup_w_in_t(_mm([(dZ, s["h"])], "tn", BF16, "mix_in_dw"))
    dh2 = _mm([(dZ, W["w_in"])], "nn", F32, "mix_in_dx")
    dx1, gg["g_mix_pre"] = _norm_bwd(s["x"], G["g_mix_pre"], dh2, 1.0, dx2, F32, "norm_bwd_pre")
    started_mix = emit("mix", gw)
    token = token if started_mix is None else started_mix

    def ready1(dws):
        gw["w1_gate"], gw["w1_up"], gw["w1_down"] = dws
        return emit("ffn1", gw)

    dx0, gg["g_ffn1_pre"], gg["g_ffn1_post"], started_ffn1 = _ffn_bwd(
        dx1, sv["ffn1"], G["g_ffn1_pre"], _ordered_after(G["g_ffn1_post"], token), W["w1_gate"], W["w1_up"],
        W["w1_down"], ready1)
    return dx0, gg, token if started_ffn1 is None else started_ffn1


def _pack_small(vecs):
    flat = jnp.concatenate([v.reshape(-1) for v in vecs])
    rows = -(-flat.shape[0] // (8 * LANES)) * 8
    return jnp.pad(flat, (0, rows * LANES - flat.shape[0])).reshape(rows, LANES)


def _group_weights(names, landed, shards, chip):
    out = {}
    for name, g, own in zip(names, landed, shards):
        g = lax.dynamic_update_slice(g, own[None], (chip, 0, 0))
        if name == "w_in":
            g = _regroup_w_in_t(g)
        elif name == "w_mla_uq":
            g = _regroup_w_uq_t(g)
        elif name not in ("w_mla_ukv", "w_ple_proj"):
            g = g.reshape(-1, g.shape[2])
        elif name == "w_mla_ukv":
            g = _regroup_w_ukv(_cat_cols(g))
        out[name] = g
    return out


def kernel(x, p, positions, g_ffn1_pre, w1_gate, w1_up, w1_down, g_ffn1_post, g_mix_pre, w_in, g_mla_q, w_mla_uq, g_mla_kv, w_mla_ukv, w_conv, w_out, g_mix_post, g_ffn2_pre, w2_gate, w2_up, w2_down, g_ffn2_post, g_ple_pre, w_ple_gate, w_ple_proj, g_ple_post, loss_target, m_g_ffn1_pre, m_w1_gate, m_w1_up, m_w1_down, m_g_ffn1_post, m_g_mix_pre, m_w_in, m_g_mla_q, m_w_mla_uq, m_g_mla_kv, m_w_mla_ukv, m_w_conv, m_w_out, m_g_mix_post, m_g_ffn2_pre, m_w2_gate, m_w2_up, m_w2_down, m_g_ffn2_post, m_g_ple_pre, m_w_ple_gate, m_w_ple_proj, m_g_ple_post, v_g_ffn1_pre, v_w1_gate, v_w1_up, v_w1_down, v_g_ffn1_post, v_g_mix_pre, v_w_in, v_g_mla_q, v_w_mla_uq, v_g_mla_kv, v_w_mla_ukv, v_w_conv, v_w_out, v_g_mix_post, v_g_ffn2_pre, v_w2_gate, v_w2_up, v_w2_down, v_g_ffn2_post, v_g_ple_pre, v_w_ple_gate, v_w_ple_proj, v_g_ple_post):
    w = dict(g_ffn1_pre=g_ffn1_pre, w1_gate=w1_gate, w1_up=w1_up, w1_down=w1_down, g_ffn1_post=g_ffn1_post,
             g_mix_pre=g_mix_pre, w_in=w_in, g_mla_q=g_mla_q, w_mla_uq=w_mla_uq, g_mla_kv=g_mla_kv,
             w_mla_ukv=w_mla_ukv, w_conv=w_conv, w_out=w_out, g_mix_post=g_mix_post, g_ffn2_pre=g_ffn2_pre,
             w2_gate=w2_gate, w2_up=w2_up, w2_down=w2_down, g_ffn2_post=g_ffn2_post, g_ple_pre=g_ple_pre,
             w_ple_gate=w_ple_gate, w_ple_proj=w_ple_proj, g_ple_post=g_ple_post)
    m = dict(g_ffn1_pre=m_g_ffn1_pre, w1_gate=m_w1_gate, w1_up=m_w1_up, w1_down=m_w1_down, g_ffn1_post=m_g_ffn1_post,
             g_mix_pre=m_g_mix_pre, w_in=m_w_in, g_mla_q=m_g_mla_q, w_mla_uq=m_w_mla_uq, g_mla_kv=m_g_mla_kv,
             w_mla_ukv=m_w_mla_ukv, w_conv=m_w_conv, w_out=m_w_out, g_mix_post=m_g_mix_post, g_ffn2_pre=m_g_ffn2_pre,
             w2_gate=m_w2_gate, w2_up=m_w2_up, w2_down=m_w2_down, g_ffn2_post=m_g_ffn2_post, g_ple_pre=m_g_ple_pre,
             w_ple_gate=m_w_ple_gate, w_ple_proj=m_w_ple_proj, g_ple_post=m_g_ple_post)
    v = dict(g_ffn1_pre=v_g_ffn1_pre, w1_gate=v_w1_gate, w1_up=v_w1_up, w1_down=v_w1_down, g_ffn1_post=v_g_ffn1_post,
             g_mix_pre=v_g_mix_pre, w_in=v_w_in, g_mla_q=v_g_mla_q, w_mla_uq=v_w_mla_uq, g_mla_kv=v_g_mla_kv,
             w_mla_ukv=v_w_mla_ukv, w_conv=v_w_conv, w_out=v_w_out, g_mix_post=v_g_mix_post, g_ffn2_pre=v_g_ffn2_pre,
             w2_gate=v_w2_gate, w2_up=v_w2_up, w2_down=v_w2_down, g_ffn2_post=v_g_ffn2_post, g_ple_pre=v_g_ple_pre,
             w_ple_gate=v_w_ple_gate, w_ple_proj=v_w_ple_proj, g_ple_post=v_g_ple_post)

    for name in TRANSPOSED:
        w[name], m[name], v[name] = (jnp.swapaxes(t[name], 1, 2) for t in (w, m, v))

    depth = g_ffn1_pre.shape[0]
    assert depth == 2, "the reduced gradients are assembled in [2, K, N] buffers"
    S = x.shape[1]
    cx, cy, cc = lax.axis_index("x"), lax.axis_index("y"), lax.axis_index("c")
    chip = 2 * cx + cy
    c_idx = cc.reshape(1).astype(jnp.int32)
    idx2 = jnp.stack([chip, cc]).astype(jnp.int32)

    conv_slot = lax.dynamic_update_slice(jnp.zeros((depth, 3, 256), F32),
                                         w_conv * (cc == 0).astype(F32), (0, 0, 64 * chip))
    conv_sum = _allreduce_small(_pack_small([conv_slot]), positions)
    w_conv_full = conv_sum[:depth * 3 * 256 // LANES].reshape(depth, 3, 256)

    def padded(name, i):
        (rows, _), rows_p = BIG[name]
        return jnp.pad(w[name][i].astype(BF16), ((0, rows_p - rows), (0, 0)))

    def groups_of(i, groups):
        return groups if i == 0 else (("all", BIG_NAMES),)

    started, after = {}, conv_sum
    for i in range(depth):
        for gname, names in groups_of(i, GATHER_GROUPS):
            shards = [padded(name, i) for name in names]
            lands = [lax.empty((N_CHIPS,) + s.shape, BF16) for s in shards]
            started[i, gname] = _exchange_start(_gather_copies, shards, lands, after, "gather_start_%d_%s" % (i, gname))
            after = started[i, gname][4]
    all_started = after

    inv = ROPE_BASE ** (-jnp.arange(ROPE_HALF, dtype=F32) / ROPE_HALF)
    zeros = lambda n: jnp.zeros((n,), F32)
    ones = jnp.ones((ROPE_HALF,), F32)
    inv_pat = jnp.concatenate([zeros(64), inv, inv, zeros(32)]).reshape(1, LANES)
    sign_pat = jnp.concatenate([zeros(64), -ones, ones, zeros(32)]).reshape(1, LANES)
    cp, sp = _rope_tables(positions.reshape(S, 1), inv_pat, sign_pat)

    def fetcher(i):
        table, have = dict(groups_of(i, GATHER_GROUPS)), {}

        def fetch(group, after):
            key = group if group in table else "all"
            if (i, key) in started:
                ssem, rsem, srcs, lands, _ = started.pop((i, key))
                if i == 0 and group == "ffn1a":
                    after = all_started
                srcs, landed = _exchange_wait(_gather_copies, ssem, rsem, srcs, lands, after,
                                              "gather_wait_%d_%s" % (i, key))
                have.update(_group_weights(table[key], _forward_rows(landed), srcs, chip))
            return have

        return fetch

    xs, saved, h_in = x[0], [], None
    gains = [{name: w[name][i].reshape(1, n) for name, n in GAINS} for i in range(depth)]
    for i in range(depth):
        g_next = gains[i + 1]["g_ffn1_pre"] if i + 1 < depth else None
        xs, sv, h_in = _layer_fwd(xs, p[i, 0], fetcher(i), gains[i], w_conv_full[i], cp, sp, h_in, g_next)
        saved.append(sv)

    loss_part, dx = _loss_and_grad(xs, loss_target[0])
    loss = lax.psum(loss_part[0, 0], AXES)

    scattering = []

    def emitter(i):
        table = dict(groups_of(i, SCATTER_GROUPS))

        def emit(group, gw):
            if group in table:
                names = table[group]
            elif group == "ffn1":
                names = table["all"]
            else:
                return None
            full = [gw[name] for name in names]
            pair_sums = _add_rows(full, _swap_rows(full), c_idx)
            lands = [lax.empty((3,) + s.shape[1:], BF16) for s in pair_sums]
            st = _exchange_start(_scatter_copies, pair_sums, lands, c_idx, "scatter_start_%d_%s" % (i, group))
            scattering.append((i, group, names, st))
            return st[4]

        return emit

    ggs, token = [None] * depth, None
    for i in reversed(range(depth)):
        dx, ggs[i], token = _layer_bwd(dx, saved[i], gains[i], w_conv_full[i], cp, sp, emitter(i), token)

    fins, after = {name: None for name in BIG_NAMES}, dx
    for i, group, names, (ssem, rsem, srcs, lands, _) in scattering:
        srcs, arrived = _exchange_wait(_scatter_copies, ssem, rsem, srcs, lands, after,
                                       "scatter_wait_%d_%s" % (i, group))
        sums = _add_chip_sums(srcs, arrived, idx2, i, [fins[name] for name in names], after)
        fins.update(zip(names, sums))
        after = sums[0]
    joined = _join_rows([fins[name] for name in BIG_NAMES])
    reduced = dict(zip(BIG_NAMES, joined))

    small_names = [name for name, _ in GAINS] + ["w_conv"]
    small = _allreduce_small(_pack_small([jnp.stack([ggs[i][name] for i in range(depth)]) for name in small_names]),
                             joined[0])
    flat, off = small.reshape(-1), 0
    for name, n in GAINS:
        reduced[name] = flat[off:off + depth * n].reshape(depth, n)
        off += depth * n
    conv_full = flat[off:off + depth * 3 * 256].reshape(depth, 3, 256)
    reduced["w_conv"] = lax.dynamic_slice(conv_full, (0, 0, 64 * chip), (depth, 3, 64))

    grads, deltas, new_m, new_v = {}, {}, {}, {}
    for name in WEIGHT_ORDER:
        shape = w[name].shape
        three_d = shape if len(shape) == 3 else (1,) + shape
        g_ = reduced[name] if len(shape) == 3 else reduced[name].reshape(three_d)
        outs = _adamw(w[name].reshape(three_d), g_, m[name].reshape(three_d), v[name].reshape(three_d), "adamw")
        if name in TRANSPOSED:
            outs = [jnp.swapaxes(o, 1, 2) for o in outs]
        grads[name], deltas[name], new_m[name], new_v[name] = (o.reshape(o.shape[-len(shape):]) for o in outs)

    return (loss, dx[None], *[grads[n] for n in WEIGHT_ORDER], *[deltas[n] for n in WEIGHT_ORDER],
            *[new_m[n] for n in WEIGHT_ORDER], *[new_v[n] for n in WEIGHT_ORDER])
```

```python
import jax
import jax.numpy as jnp
from jax import lax
from jax.experimental import pallas as pl
from jax.experimental.pallas import tpu as pltpu

F32 = jnp.float32
BF16 = jnp.bfloat16
MESH = pl.DeviceIdType.MESH
AXES = ("x", "y", "c")

D_MODEL = 1024
N_CHIPS = 4
EPS = 1e-6
NEG_INF = -1e30
ROPE_BASE = 10000.0
ROPE_HALF = 16
LANES = 128
SB_HEADS, MLA_HEADS, HEAD_DIM = 4, 8, 64
MLA_SCALE = 96.0 ** -0.5
SB_SCALE = 64.0 ** -0.5
SB_BQ, SB_BK = 512, 128
MLA_BQ, MLA_BK = 256, 256
ADAM_LR, ADAM_B1, ADAM_B2, ADAM_EPS, ADAM_WD, ADAM_STEP = 0.001, 0.9, 0.999, 1e-08, 0.01, 10
VMEM_LIMIT = 48 * 2 ** 20

Z_SB, Z_KV, Z_KR, Z_Q, Z_CV, Z_W = 0, 768, 1024, 1152, 1536, 2304

NT = (((1,), (1,)), ((), ()))
TN = (((0,), (0,)), ((), ()))

TRANSPOSED = ("w1_gate", "w1_up", "w2_gate", "w2_up", "w_in", "w_mla_uq")
BIG = {"w1_gate": ((704, 1024), 704), "w1_up": ((704, 1024), 704), "w1_down": ((704, 1024), 704),
       "w_in": ((552, 1024), 576), "w_mla_uq": ((192, 384), 192), "w_mla_ukv": ((256, 256), 256),
       "w_out": ((256, 1024), 256),
       "w2_gate": ((704, 1024), 704), "w2_up": ((704, 1024), 704), "w2_down": ((704, 1024), 704),
       "w_ple_gate": ((256, 1024), 256), "w_ple_proj": ((256, 256), 256)}
BIG_NAMES = tuple(BIG)
GAINS = (("g_ffn1_pre", 1024), ("g_ffn1_post", 1024), ("g_mix_pre", 1024), ("g_mla_q", 384),
         ("g_mla_kv", 256), ("g_mix_post", 1024), ("g_ffn2_pre", 1024), ("g_ffn2_post", 1024),
         ("g_ple_pre", 1024), ("g_ple_post", 1024))
WEIGHT_ORDER = ("g_ffn1_pre", "w1_gate", "w1_up", "w1_down", "g_ffn1_post", "g_mix_pre", "w_in", "g_mla_q",
                "w_mla_uq", "g_mla_kv", "w_mla_ukv", "w_conv", "w_out", "g_mix_post", "g_ffn2_pre", "w2_gate",
                "w2_up", "w2_down", "g_ffn2_post", "g_ple_pre", "w_ple_gate", "w_ple_proj", "g_ple_post")

_pcall = pl.pallas_call


def _params(sem=None):
    return pltpu.CompilerParams(dimension_semantics=sem, vmem_limit_bytes=VMEM_LIMIT)


def _dot(a, b, dims=None):
    a, b = a.astype(BF16), b.astype(BF16)
    if dims is None:
        return jnp.dot(a, b, preferred_element_type=F32)
    return lax.dot_general(a, b, dims, preferred_element_type=F32)


def _rstd(v):
    return lax.rsqrt(jnp.mean(v * v, axis=-1, keepdims=True) + EPS)


def _sigmoid(v):
    return 0.5 * jnp.tanh(0.5 * v) + 0.5


def _row_block(n, want, mult=16):
    for b in range(min(n, want), 0, -1):
        if n % b == 0 and b % mult == 0:
            return b
    return n


def _mm(pairs, mode, out_dtype, name, bm=512, bn=512):
    a0, b0 = pairs[0]
    if mode == "nn":
        M, N = a0.shape[0], b0.shape[1]
    elif mode == "nt":
        M, N = a0.shape[0], b0.shape[0]
    else:
        M, N = a0.shape[1], b0.shape[1]
        bn = max(bn, 1024)
    bm, bn = _row_block(M, bm, 128 if mode == "tn" else 16), _row_block(N, bn, 128)
    n_pairs = len(pairs)
    dims = {"nn": None, "nt": NT, "tn": TN}[mode]

    def body(*refs):
        acc = None
        for t in range(n_pairs):
            part = _dot(refs[2 * t][...], refs[2 * t + 1][...], dims)
            acc = part if acc is None else acc + part
        refs[-1][...] = acc.astype(refs[-1].dtype)

    in_specs, ops = [], []
    for a, b in pairs:
        if mode == "nn":
            in_specs += [pl.BlockSpec((bm, a.shape[1]), lambda j, i: (i, 0)),
                         pl.BlockSpec((b.shape[0], bn), lambda j, i: (0, j))]
        elif mode == "nt":
            in_specs += [pl.BlockSpec((bm, a.shape[1]), lambda j, i: (i, 0)),
                         pl.BlockSpec((bn, b.shape[1]), lambda j, i: (j, 0))]
        else:
            in_specs += [pl.BlockSpec((a.shape[0], bm), lambda j, i: (0, i)),
                         pl.BlockSpec((b.shape[0], bn), lambda j, i: (0, j))]
        ops += [a, b]
    return _pcall(body, name=name, grid=(N // bn, M // bm), in_specs=in_specs,
                  out_specs=pl.BlockSpec((bm, bn), lambda j, i: (i, j)),
                  out_shape=jax.ShapeDtypeStruct((M, N), out_dtype),
                  compiler_params=_params(("parallel", "parallel")))(*ops)


def _dw_col_shards(a, b, name):
    S, K = a.shape
    Np = b.shape[1] // N_CHIPS

    def body(a_ref, b_ref, o_ref):
        o_ref[...] = _dot(a_ref[...], b_ref[...], TN).astype(BF16)

    return _pcall(body, name=name, grid=(N_CHIPS,),
                  in_specs=[pl.BlockSpec((S, K), lambda j: (0, 0)), pl.BlockSpec((S, Np), lambda j: (0, j))],
                  out_specs=pl.BlockSpec((None, K, Np), lambda j: (j, 0, 0)),
                  out_shape=jax.ShapeDtypeStruct((N_CHIPS, K, Np), BF16),
                  compiler_params=_params(("parallel",)))(a, b)


def _ew(fn, ins, outs, name, br=512):
    S = max(a.shape[0] for a in ins)
    br = _row_block(S, br)
    n_in = len(ins)

    def body(*refs):
        res = fn(*[r[...] for r in refs[:n_in]])
        if not isinstance(res, tuple):
            res = (res,)
        for r, v in zip(refs[n_in:], res):
            r[...] = v.astype(r.dtype)

    in_specs = [pl.BlockSpec((br, a.shape[1]), lambda i: (i, 0)) if a.shape[0] == S and S > 1
                else pl.BlockSpec(a.shape, lambda i: (0, 0)) for a in ins]
    out = _pcall(body, name=name, grid=(S // br,), in_specs=in_specs,
                 out_specs=tuple(pl.BlockSpec((br, w), lambda i: (i, 0)) for w, _ in outs),
                 out_shape=tuple(jax.ShapeDtypeStruct((S, w), dt) for w, dt in outs),
                 compiler_params=_params(("parallel",)))(*ins)
    return out if len(outs) > 1 else out[0]


def _norm_fwd(x, g, name):
    return _ew(lambda xv, gv: xv * _rstd(xv) * gv, [x, g], [(x.shape[1], BF16)], name, br=512)


def _resid_norm(x, y, g, alpha, name, g_next=None):
    W = x.shape[1]
    if g_next is None:
        return _ew(lambda xv, yv, gv: xv + alpha * (yv * _rstd(yv) * gv), [x, y, g], [(W, F32)], name), None

    def both(xv, yv, gv, gn):
        xn = xv + alpha * (yv * _rstd(yv) * gv)
        return xn, xn * _rstd(xn) * gn

    return _ew(both, [x, y, g, g_next], [(W, F32), (W, BF16)], name + "_next")


def _norm_bwd(xin, g, dy, alpha, resid, out_dtype, name):
    S, W = xin.shape
    br = _row_block(S, 512)
    has_res = resid is not None

    def body(*refs):
        x_ref, g_ref, dy_ref = refs[:3]
        dx_ref, dg_ref = refs[-2:]
        xv, dyv = x_ref[...], dy_ref[...] * alpha
        r = _rstd(xv)
        xh = xv * r
        u = dyv * g_ref[...]
        dx = r * (u - xh * jnp.mean(u * xh, axis=-1, keepdims=True))
        if has_res:
            dx = dx + refs[3][...]
        dx_ref[...] = dx.astype(dx_ref.dtype)
        part = jnp.sum(dyv * xh, axis=0, keepdims=True)

        @pl.when(pl.program_id(0) == 0)
        def _():
            dg_ref[...] = part

        @pl.when(pl.program_id(0) > 0)
        def _():
            dg_ref[...] += part

    row = pl.BlockSpec((br, W), lambda i: (i, 0))
    vec = pl.BlockSpec((1, W), lambda i: (0, 0))
    ops = [xin, g, dy] + ([resid] if has_res else [])
    return _pcall(body, name=name, grid=(S // br,), in_specs=[row, vec, row] + ([row] if has_res else []),
                  out_specs=(row, vec),
                  out_shape=(jax.ShapeDtypeStruct((S, W), out_dtype), jax.ShapeDtypeStruct((1, W), F32)),
                  compiler_params=_params(("arbitrary",)))(*ops)


def _ffn_gate_up(h, wgt, wut, name):
    S, K = h.shape
    F = wgt.shape[0]
    bm, bn = _row_block(S, 512), _row_block(F, 1408, 128)

    def body(h_ref, wg_ref, wu_ref, g_ref, u_ref, a_ref):
        hv = h_ref[...]
        g = _dot(hv, wg_ref[...], NT)
        u = _dot(hv, wu_ref[...], NT)
        g_ref[...] = g.astype(BF16)
        u_ref[...] = u.astype(BF16)
        a_ref[...] = (g * _sigmoid(g) * u).astype(BF16)

    blk = pl.BlockSpec((bm, bn), lambda n, i: (i, n))
    wsp = pl.BlockSpec((bn, K), lambda n, i: (n, 0))
    return _pcall(body, name=name, grid=(F // bn, S // bm),
                  in_specs=[pl.BlockSpec((bm, K), lambda n, i: (i, 0)), wsp, wsp],
                  out_specs=(blk, blk, blk), out_shape=(jax.ShapeDtypeStruct((S, F), BF16),) * 3,
                  compiler_params=_params(("parallel", "parallel")))(h, wgt, wut)


def _ffn_bwd_mid(dy, wd, g, u, name):
    S, D = dy.shape
    F = g.shape[1]
    bm, bn = _row_block(S, 512), _row_block(F, 1408, 128)

    def body(dy_ref, wd_ref, g_ref, u_ref, dg_ref, du_ref):
        da = _dot(dy_ref[...], wd_ref[...], NT)
        gv, uv = g_ref[...].astype(F32), u_ref[...].astype(F32)
        s = _sigmoid(gv)
        dg_ref[...] = (da * uv * (s * (1.0 + gv * (1.0 - s)))).astype(BF16)
        du_ref[...] = (da * (gv * s)).astype(BF16)

    blk = pl.BlockSpec((bm, bn), lambda j, i: (i, j))
    return _pcall(body, name=name, grid=(F // bn, S // bm),
                  in_specs=[pl.BlockSpec((bm, D), lambda j, i: (i, 0)),
                            pl.BlockSpec((bn, D), lambda j, i: (j, 0)), blk, blk],
                  out_specs=(blk, blk), out_shape=(jax.ShapeDtypeStruct((S, F), BF16),) * 2,
                  compiler_params=_params(("parallel", "parallel")))(dy, wd, g, u)


def _ple_proj(p16, w):
    S, K = p16.shape
    bm = _row_block(S, 1024)

    def body(p_ref, w_ref, o_ref):
        o_ref[...] = _dot(p_ref[...], w_ref[...])

    return _pcall(body, name="ple_proj", grid=(N_CHIPS, S // bm),
                  in_specs=[pl.BlockSpec((bm, K), lambda j, i: (i, 0)),
                            pl.BlockSpec((None, K, 256), lambda j, i: (j, 0, 0))],
                  out_specs=pl.BlockSpec((bm, 256), lambda j, i: (i, j)),
                  out_shape=jax.ShapeDtypeStruct((S, N_CHIPS * 256), F32),
                  compiler_params=_params(("parallel", "parallel")))(p16, w)


def _loss_and_grad(xf, tgt):
    S, W = xf.shape
    br = _row_block(S, 256)

    def body(x_ref, t_ref, l_ref, d_ref):
        d = x_ref[...] - t_ref[...]
        d_ref[...] = d * (1.0 / W)
        part = 0.5 * jnp.sum(jnp.sum(d * d, axis=-1, keepdims=True) * (1.0 / W), axis=0, keepdims=True)

        @pl.when(pl.program_id(0) == 0)
        def _():
            l_ref[...] = part

        @pl.when(pl.program_id(0) > 0)
        def _():
            l_ref[...] += part

    row = pl.BlockSpec((br, W), lambda i: (i, 0))
    return _pcall(body, name="loss", grid=(S // br,), in_specs=[row, row],
                  out_specs=(pl.BlockSpec((1, 1), lambda i: (0, 0)), row),
                  out_shape=(jax.ShapeDtypeStruct((1, 1), F32), jax.ShapeDtypeStruct((S, W), F32)),
                  compiler_params=_params(("arbitrary",)))(xf, tgt)


def _adamw(w, g, m, v, name):
    L, R, C = w.shape
    Cp = g.shape[2]
    br = _row_block(R, 256, 8) if R % 8 == 0 else R

    def body(w_ref, g_ref, m_ref, v_ref, go_ref, d_ref, nm_ref, nv_ref):
        gv = g_ref[...][:, :C]
        nm = ADAM_B1 * m_ref[...] + (1.0 - ADAM_B1) * gv
        nv = ADAM_B2 * v_ref[...] + (1.0 - ADAM_B2) * (gv * gv)
        m_hat = nm / (1.0 - ADAM_B1 ** ADAM_STEP)
        v_hat = nv / (1.0 - ADAM_B2 ** ADAM_STEP)
        go_ref[...] = gv
        d_ref[...] = -ADAM_LR * (m_hat / (jnp.sqrt(v_hat) + ADAM_EPS) + ADAM_WD * w_ref[...])
        nm_ref[...] = nm
        nv_ref[...] = nv

    blk = pl.BlockSpec((None, br, C), lambda l, i: (l, i, 0))
    gblk = pl.BlockSpec((None, br, Cp), lambda l, i: (l, i, 0))
    return _pcall(body, name=name, grid=(L, R // br), in_specs=[blk, gblk, blk, blk], out_specs=(blk,) * 4,
                  out_shape=(jax.ShapeDtypeStruct((L, R, C), F32),) * 4,
                  compiler_params=_params(("parallel", "parallel")))(w, g, m, v)


def _rope_tables(pos, inv_pat, sign_pat):
    def fn(p, iv, sg):
        ang = p.astype(F32) * iv
        return jnp.cos(ang), jnp.sin(ang) * sg

    return _ew(fn, [pos, inv_pat, sign_pat], [(LANES, F32)] * 2, "rope_tables")


def _swap_halves_of_rope(v):
    W = v.shape[1]
    lane = lax.broadcasted_iota(jnp.int32, (1, W), 1) % LANES
    return jnp.where((lane >= 64) & (lane < 80), pltpu.roll(v, W - ROPE_HALF, 1),
                     jnp.where((lane >= 80) & (lane < 96), pltpu.roll(v, ROPE_HALF, 1), 0.0))


def _tile_lanes(v, n):
    return jnp.concatenate([v] * n, axis=1)


def _mla_prep(qf, kv, z, cp, sp):
    S = qf.shape[0]
    br = _row_block(S, 256)
    W = MLA_HEADS * LANES

    def body(q_ref, k_ref, r_ref, c_ref, s_ref, qo_ref, ko_ref):
        c, s = c_ref[...], s_ref[...]
        q = q_ref[...]
        qo_ref[...] = (q * _tile_lanes(c, MLA_HEADS) + _swap_halves_of_rope(q) * _tile_lanes(s, MLA_HEADS)).astype(BF16)
        r = r_ref[...]
        lane = lax.broadcasted_iota(jnp.int32, (1, LANES), 1)
        kr = jnp.where((lane >= 64) & (lane < 96), r * c + _swap_halves_of_rope(r) * s, 0.0)
        ko_ref[...] = (k_ref[...] + _tile_lanes(kr, MLA_HEADS)).astype(BF16)

    wide = pl.BlockSpec((br, W), lambda i: (i, 0))
    one = pl.BlockSpec((br, LANES), lambda i: (i, 0))
    return _pcall(body, name="mla_prep", grid=(S // br,),
                  in_specs=[wide, wide, pl.BlockSpec((br, LANES), lambda i: (i, Z_KR // LANES)), one, one],
                  out_specs=(wide, wide), out_shape=(jax.ShapeDtypeStruct((S, W), BF16),) * 2,
                  compiler_params=_params(("parallel",)))(qf, kv, z, cp, sp)


def _mla_prep_bwd(dq, dk, cp, sp):
    S, W = dq.shape
    br = _row_block(S, 256)

    def body(dq_ref, dk_ref, c_ref, s_ref, dqo_ref, dr_ref):
        c, s = c_ref[...], s_ref[...]
        d = dq_ref[...]
        dqo_ref[...] = (d * _tile_lanes(c, MLA_HEADS) + _swap_halves_of_rope(d * _tile_lanes(s, MLA_HEADS))).astype(BF16)
        dkv = dk_ref[...]
        tot = dkv[:, 0:LANES]
        for h in range(1, MLA_HEADS):
            tot = tot + dkv[:, h * LANES:(h + 1) * LANES]
        lane = lax.broadcasted_iota(jnp.int32, (1, LANES), 1)
        tot = jnp.where((lane >= 64) & (lane < 96), tot, 0.0)
        dr_ref[...] = tot * c + _swap_halves_of_rope(tot * s)

    wide = pl.BlockSpec((br, W), lambda i: (i, 0))
    one = pl.BlockSpec((br, LANES), lambda i: (i, 0))
    return _pcall(body, name="mla_prep_bwd", grid=(S // br,), in_specs=[wide, wide, one, one], out_specs=(wide, one),
                  out_shape=(jax.ShapeDtypeStruct((S, W), BF16), jax.ShapeDtypeStruct((S, LANES), F32)),
                  compiler_params=_params(("parallel",)))(dq, dk, cp, sp)


def _shift_down(v, k, row):
    return jnp.where(row >= k, pltpu.roll(v, k, 0), 0.0)


def _shift_up(v, k, row):
    n = v.shape[0]
    return jnp.where(row < n - k, pltpu.roll(v, n - k, 0), 0.0)


def _conv_fwd(z, w):
    S = z.shape[0]
    c0 = Z_CV // LANES

    def body(b_ref, c_ref, h_ref, w_ref, y_ref):
        u = c_ref[...] * h_ref[...]
        row = lax.broadcasted_iota(jnp.int32, u.shape, 0)
        wv = w_ref[...]
        conv = wv[0:1] * _shift_down(u, 2, row) + wv[1:2] * _shift_down(u, 1, row) + wv[2:3] * u
        y_ref[...] = b_ref[...] * conv

    def col(k):
        return pl.BlockSpec((S, LANES), lambda j: (0, c0 + 2 * k + j))

    return _pcall(body, name="conv_fwd", grid=(2,),
                  in_specs=[col(0), col(1), col(2), pl.BlockSpec((3, LANES), lambda j: (0, j))],
                  out_specs=pl.BlockSpec((S, LANES), lambda j: (0, j)),
                  out_shape=jax.ShapeDtypeStruct((S, 256), F32), compiler_params=_params(("parallel",)))(z, z, z, w)


def _conv_bwd(z, w, dc_all):
    S = z.shape[0]
    c0 = Z_CV // LANES

    def body(b_ref, c_ref, h_ref, w_ref, dy_ref, db_ref, dc_ref, dh_ref, dw_ref):
        cv, hv, bv, dyv = c_ref[...], h_ref[...], b_ref[...], dy_ref[...]
        u = cv * hv
        row = lax.broadcasted_iota(jnp.int32, u.shape, 0)
        wv = w_ref[...]
        u1, u2 = _shift_down(u, 1, row), _shift_down(u, 2, row)
        conv = wv[0:1] * u2 + wv[1:2] * u1 + wv[2:3] * u
        db_ref[...] = dyv * conv
        dconv = dyv * bv
        du = wv[2:3] * dconv + wv[1:2] * _shift_up(dconv, 1, row) + wv[0:1] * _shift_up(dconv, 2, row)
        dc_ref[...] = du * hv
        dh_ref[...] = du * cv
        dw_ref[0:1, :] = jnp.sum(dconv * u2, axis=0, keepdims=True)
        dw_ref[1:2, :] = jnp.sum(dconv * u1, axis=0, keepdims=True)
        dw_ref[2:3, :] = jnp.sum(dconv * u, axis=0, keepdims=True)

    def col(k):
        return pl.BlockSpec((S, LANES), lambda j: (0, c0 + 2 * k + j))

    wsp = pl.BlockSpec((3, LANES), lambda j: (0, j))
    osp = pl.BlockSpec((S, LANES), lambda j: (0, j))
    db, dc, dh, dw = _pcall(
        body, name="conv_bwd", grid=(2,),
        in_specs=[col(0), col(1), col(2), wsp, pl.BlockSpec((S, LANES), lambda j: (0, 6 + j))],
        out_specs=(osp, osp, osp, wsp),
        out_shape=(jax.ShapeDtypeStruct((S, 256), F32),) * 3 + (jax.ShapeDtypeStruct((3, 256), F32),),
        compiler_params=_params(("parallel",)))(z, z, z, w, dc_all)
    return db, dc, dh, dw


def _half_masks():
    lane = lax.broadcasted_iota(jnp.int32, (1, LANES), 1)
    return lane < HEAD_DIM, lane >= HEAD_DIM


def _pair(v, p):
    return v[:, p * LANES:(p + 1) * LANES]


def _keep(mask, v):
    return jnp.where(mask, v, jnp.zeros_like(v))


def _lane_col(vals):
    lane = lax.broadcasted_iota(jnp.int32, (1, LANES), 1)
    out = jnp.zeros((vals[0].shape[0], LANES), F32)
    for h, v in enumerate(vals):
        out = jnp.where(lane == h, v, out)
    return out


def _mla_fwd(q, k, kv):
    S = q.shape[0]
    H = MLA_HEADS
    bq, bk = _row_block(S, MLA_BQ), _row_block(S, MLA_BK)
    per = bq // bk

    def body(q_ref, k_ref, v_ref, o_ref, lse_ref):
        i = pl.program_id(0)
        lo, hi = _half_masks()
        qb = q_ref[...]

        def block(j, carry, ok):
            ms, ls, accs = carry
            off = pl.multiple_of(j * bk, bk)
            kb = k_ref[pl.ds(off, bk), :]
            vb = v_ref[pl.ds(off, bk), :].astype(BF16)
            ss = [_dot(_pair(qb, h), _pair(kb, h), NT) * MLA_SCALE for h in range(H)]
            if ok is not None:
                ss = [jnp.where(ok, s, NEG_INF) for s in ss]
            ms2 = [jnp.maximum(ms[h], jnp.max(ss[h], axis=-1, keepdims=True)) for h in range(H)]
            ps = [jnp.exp(ss[h] - ms2[h]) for h in range(H)]
            al = [jnp.exp(ms[h] - ms2[h]) for h in range(H)]
            ls2 = [al[h] * ls[h] + jnp.sum(ps[h], axis=-1, keepdims=True) for h in range(H)]
            pvs = [_dot(ps[h], _pair(vb, h // 2)) for h in range(H)]
            accs2 = []
            for p in range(H // 2):
                scale = jnp.where(lo, al[2 * p], al[2 * p + 1])
                accs2.append(scale * accs[p] + jnp.where(lo, pvs[2 * p], pvs[2 * p + 1]))
            return tuple(ms2), tuple(ls2), tuple(accs2)

        init = (tuple(jnp.full((bq, 1), NEG_INF, F32) for _ in range(H)),
                tuple(jnp.zeros((bq, 1), F32) for _ in range(H)),
                tuple(jnp.zeros((bq, LANES), F32) for _ in range(H // 2)))
        carry = lax.fori_loop(0, i * per, lambda j, c: block(j, c, None), init)
        row = lax.broadcasted_iota(jnp.int32, (bq, bk), 0)
        col = lax.broadcasted_iota(jnp.int32, (bq, bk), 1)
        for t in range(per):
            carry = block(i * per + t, carry, col + t * bk <= row)
        ms, ls, accs = carry
        o_ref[...] = jnp.concatenate(
            [accs[p] / jnp.where(lo, ls[2 * p], ls[2 * p + 1]) for p in range(H // 2)], axis=1)
        lse_ref[...] = _lane_col([ms[h] + jnp.log(ls[h]) for h in range(H)])

    return _pcall(body, name="mla_fwd", grid=(S // bq,),
                  in_specs=[pl.BlockSpec((bq, H * LANES), lambda i: (i, 0)),
                            pl.BlockSpec((S, H * LANES), lambda i: (0, 0)),
                            pl.BlockSpec((S, H * HEAD_DIM), lambda i: (0, 2))],
                  out_specs=(pl.BlockSpec((bq, H * HEAD_DIM), lambda i: (i, 0)),
                             pl.BlockSpec((bq, LANES), lambda i: (i, 0))),
                  out_shape=(jax.ShapeDtypeStruct((S, H * HEAD_DIM), F32), jax.ShapeDtypeStruct((S, LANES), F32)),
                  compiler_params=_params(("parallel",)))(q, k, kv)


def _mla_bwd(q, k, kv, o, lse, dc_all):
    S = q.shape[0]
    H = MLA_HEADS
    bq, bk = _row_block(S, MLA_BQ), _row_block(S, MLA_BK)
    per = bq // bk

    def body(q_ref, k_ref, v_ref, o_ref, lse_ref, doa_ref, dob_ref, dq_ref, dk_ref, dv_ref):
        i = pl.program_id(0)

        @pl.when(i == 0)
        def _():
            dk_ref[...] = jnp.zeros_like(dk_ref)
            dv_ref[...] = jnp.zeros_like(dv_ref)

        lo, hi = _half_masks()
        qb = q_ref[...]
        dob = jnp.concatenate([doa_ref[...], dob_ref[...]], axis=1)
        prod = dob * o_ref[...]
        lse_v = lse_ref[...]
        do16 = dob.astype(BF16)
        dom, deltas, lses = [], [], []
        for h in range(H):
            mk = lo if h % 2 == 0 else hi
            dom.append(_keep(mk, _pair(do16, h // 2)))
            deltas.append(jnp.sum(_keep(mk, _pair(prod, h // 2)), axis=-1, keepdims=True))
            lses.append(lse_v[:, h:h + 1])

        def block(j, dqs, ok):
            off = pl.multiple_of(j * bk, bk)
            kb = k_ref[pl.ds(off, bk), :]
            vb = v_ref[pl.ds(off, bk), :].astype(BF16)
            ss = [_dot(_pair(qb, h), _pair(kb, h), NT) * MLA_SCALE for h in range(H)]
            dps = [_dot(dom[h], _pair(vb, h // 2), NT) for h in range(H)]
            ps = [jnp.exp(ss[h] - lses[h]) for h in range(H)]
            if ok is not None:
                ps = [jnp.where(ok, p, 0.0) for p in ps]
            ds16 = [(ps[h] * (dps[h] - deltas[h]) * MLA_SCALE).astype(BF16) for h in range(H)]
            p16 = [p.astype(BF16) for p in ps]
            dks = [_dot(ds16[h], _pair(qb, h), TN) for h in range(H)]
            dvps = [_dot(p16[h], dom[h], TN) for h in range(H)]
            dqs2 = [dqs[h] + _dot(ds16[h], _pair(kb, h)) for h in range(H)]
            dk_ref[pl.ds(off, bk), :] += jnp.concatenate(dks, axis=1)
            dv_ref[pl.ds(off, bk), :] += jnp.concatenate(
                [dvps[2 * p] + dvps[2 * p + 1] for p in range(H // 2)], axis=1)
            return tuple(dqs2)

        dqs = lax.fori_loop(0, i * per, lambda j, c: block(j, c, None),
                            tuple(jnp.zeros((bq, LANES), F32) for _ in range(H)))
        row = lax.broadcasted_iota(jnp.int32, (bq, bk), 0)
        col = lax.broadcasted_iota(jnp.int32, (bq, bk), 1)
        for t in range(per):
            dqs = block(i * per + t, dqs, col + t * bk <= row)
        dq_ref[...] = jnp.concatenate(dqs, axis=1)

    wide = pl.BlockSpec((bq, H * LANES), lambda i: (i, 0))
    full = pl.BlockSpec((S, H * LANES), lambda i: (0, 0))
    return _pcall(body, name="mla_bwd", grid=(S // bq,),
                  in_specs=[wide, full, pl.BlockSpec((S, H * HEAD_DIM), lambda i: (0, 2)),
                            pl.BlockSpec((bq, H * HEAD_DIM), lambda i: (i, 0)),
                            pl.BlockSpec((bq, LANES), lambda i: (i, 0)),
                            pl.BlockSpec((bq, 256), lambda i: (i, 1)), pl.BlockSpec((bq, 256), lambda i: (i, 2))],
                  out_specs=(wide, full, pl.BlockSpec((S, H * HEAD_DIM), lambda i: (0, 0))),
                  out_shape=(jax.ShapeDtypeStruct((S, H * LANES), F32), jax.ShapeDtypeStruct((S, H * LANES), F32),
                             jax.ShapeDtypeStruct((S, H * HEAD_DIM), F32)),
                  compiler_params=_params(("arbitrary",)))(q, k, kv, o, lse, dc_all, dc_all)


def _dot_exact(x, tri):
    h1 = x.astype(BF16)
    h2 = (x - h1.astype(F32)).astype(BF16)
    return _dot(h1, tri) + _dot(h2, tri)


def _softplus(z):
    return jnp.maximum(z, 0.0) + jnp.log(1.0 + jnp.exp(-jnp.abs(z)))


def _sb_fwd(z):
    S = z.shape[0]
    H = SB_HEADS
    bq, bk = _row_block(S, SB_BQ), _row_block(S, SB_BK)
    per = bq // bk

    def body(q_ref, k_ref, v_ref, o_ref, t_ref):
        i = pl.program_id(0)
        lo, hi = _half_masks()
        q16 = (q_ref[...] * SB_SCALE).astype(BF16)
        qm = [_keep(lo if h % 2 == 0 else hi, _pair(q16, h // 2)) for h in range(H)]
        rr = lax.broadcasted_iota(jnp.int32, (bk, bk), 0)
        cc = lax.broadcasted_iota(jnp.int32, (bk, bk), 1)
        later = (rr > cc).astype(BF16)

        def block(j, carry, mask):
            tails, accs = carry
            off = pl.multiple_of(j * bk, bk)
            kb = k_ref[pl.ds(off, bk), :].astype(BF16)
            vb = v_ref[pl.ds(off, bk), :].astype(BF16)
            zs = [_dot(qm[h], _pair(kb, h // 2), NT) for h in range(H)]
            sps = [_softplus(z) for z in zs]
            lnots = [-sp if mask is None else jnp.where(mask, -sp, 0.0) for sp in sps]
            sums = [_dot_exact(lnot, later) for lnot in lnots]
            ws = []
            for h in range(H):
                w = jnp.exp((zs[h] - sps[h]) + (tails[h] + sums[h]))
                ws.append((w if mask is None else jnp.where(mask, w, 0.0)).astype(BF16))
            pvs = [_dot(ws[h], _pair(vb, h // 2)) for h in range(H)]
            accs2 = [accs[p] + jnp.where(lo, pvs[2 * p], pvs[2 * p + 1]) for p in range(H // 2)]
            tails2 = [tails[h] + jnp.sum(lnots[h], axis=-1, keepdims=True) for h in range(H)]
            return tuple(tails2), tuple(accs2)

        carry = (tuple(jnp.zeros((bq, 1), F32) for _ in range(H)),
                 tuple(jnp.zeros((bq, LANES), F32) for _ in range(H // 2)))
        row = lax.broadcasted_iota(jnp.int32, (bq, bk), 0)
        col = lax.broadcasted_iota(jnp.int32, (bq, bk), 1)
        for t in range(per):
            carry = block(i * per + per - 1 - t, carry, col + (per - 1 - t) * bk < row)
        tails, accs = lax.fori_loop(0, i * per, lambda t, c: block(i * per - 1 - t, c, None), carry)
        o_ref[...] = jnp.concatenate(accs, axis=1)
        t_ref[...] = _lane_col(tails)

    return _pcall(body, name="sb_fwd", grid=(S // bq,),
                  in_specs=[pl.BlockSpec((bq, 256), lambda i: (i, 0)), pl.BlockSpec((S, 256), lambda i: (0, 1)),
                            pl.BlockSpec((S, 256), lambda i: (0, 2))],
                  out_specs=(pl.BlockSpec((bq, 256), lambda i: (i, 0)), pl.BlockSpec((bq, LANES), lambda i: (i, 0))),
                  out_shape=(jax.ShapeDtypeStruct((S, 256), F32), jax.ShapeDtypeStruct((S, LANES), F32)),
                  compiler_params=_params(("parallel",)))(z, z, z)


def _sb_bwd(z, tot, dc_all):
    S = z.shape[0]
    H = SB_HEADS
    bq, bk = _row_block(S, SB_BQ), _row_block(S, SB_BK)
    per = bq // bk

    def body(q_ref, k_ref, v_ref, t_ref, do_ref, dq_ref, dk_ref, dv_ref):
        i = pl.program_id(0)

        @pl.when(i == 0)
        def _():
            dk_ref[...] = jnp.zeros_like(dk_ref)
            dv_ref[...] = jnp.zeros_like(dv_ref)

        lo, hi = _half_masks()
        q16 = (q_ref[...] * SB_SCALE).astype(BF16)
        do16 = do_ref[...].astype(BF16)
        tot_v = t_ref[...]
        masks = [lo if h % 2 == 0 else hi for h in range(H)]
        qm = [_keep(masks[h], _pair(q16, h // 2)) for h in range(H)]
        dom = [_keep(masks[h], _pair(do16, h // 2)) for h in range(H)]
        tots = [tot_v[:, h:h + 1] for h in range(H)]
        rr = lax.broadcasted_iota(jnp.int32, (bk, bk), 0)
        cc = lax.broadcasted_iota(jnp.int32, (bk, bk), 1)
        upto = (rr <= cc).astype(BF16)
        before = (rr < cc).astype(BF16)

        def block(j, carry, mask):
            heads, eheads, dqs = carry
            off = pl.multiple_of(j * bk, bk)
            kb = k_ref[pl.ds(off, bk), :].astype(BF16)
            vb = v_ref[pl.ds(off, bk), :].astype(BF16)
            zs = [_dot(qm[h], _pair(kb, h // 2), NT) for h in range(H)]
            dws = [_dot(dom[h], _pair(vb, h // 2), NT) for h in range(H)]
            sps = [_softplus(z) for z in zs]
            lnots = [-sp if mask is None else jnp.where(mask, -sp, 0.0) for sp in sps]
            pres = [_dot_exact(lnot, upto) for lnot in lnots]
            lsigs = [zs[h] - sps[h] for h in range(H)]
            ws = [jnp.exp(lsigs[h] + (tots[h] - (heads[h] + pres[h]))) for h in range(H)]
            if mask is not None:
                ws = [jnp.where(mask, w, 0.0) for w in ws]
            es = [ws[h] * dws[h] for h in range(H)]
            esums = [eheads[h] + _dot_exact(es[h], before) for h in range(H)]
            dz16 = []
            for h in range(H):
                sig = jnp.exp(lsigs[h])
                dz = es[h] * (1.0 - sig) - sig * esums[h]
                dz16.append((dz if mask is None else jnp.where(mask, dz, 0.0)).astype(BF16))
            w16 = [w.astype(BF16) for w in ws]
            dkps = [_dot(dz16[h], qm[h], TN) for h in range(H)]
            dvps = [_dot(w16[h], dom[h], TN) for h in range(H)]
            dqps = [_dot(dz16[h], _pair(kb, h // 2)) for h in range(H)]
            dk_ref[pl.ds(off, bk), :] += jnp.concatenate([dkps[2 * p] + dkps[2 * p + 1] for p in range(H // 2)], axis=1)
            dv_ref[pl.ds(off, bk), :] += jnp.concatenate([dvps[2 * p] + dvps[2 * p + 1] for p in range(H // 2)], axis=1)
            dqs2 = [dqs[p] + jnp.where(lo, dqps[2 * p], dqps[2 * p + 1]) for p in range(H // 2)]
            heads2 = [heads[h] + jnp.sum(lnots[h], axis=-1, keepdims=True) for h in range(H)]
            eheads2 = [eheads[h] + jnp.sum(es[h], axis=-1, keepdims=True) for h in range(H)]
            return tuple(heads2), tuple(eheads2), tuple(dqs2)

        zeros = tuple(jnp.zeros((bq, 1), F32) for _ in range(H))
        init = (zeros, zeros, tuple(jnp.zeros((bq, LANES), F32) for _ in range(H // 2)))
        carry = lax.fori_loop(0, i * per, lambda j, c: block(j, c, None), init)
        row = lax.broadcasted_iota(jnp.int32, (bq, bk), 0)
        col = lax.broadcasted_iota(jnp.int32, (bq, bk), 1)
        for t in range(per):
            carry = block(i * per + t, carry, col + t * bk < row)
        dq_ref[...] = jnp.concatenate(carry[2], axis=1) * SB_SCALE

    blk = pl.BlockSpec((bq, 256), lambda i: (i, 0))
    full = pl.BlockSpec((S, 256), lambda i: (0, 0))
    return _pcall(body, name="sb_bwd", grid=(S // bq,),
                  in_specs=[blk, pl.BlockSpec((S, 256), lambda i: (0, 1)), pl.BlockSpec((S, 256), lambda i: (0, 2)),
                            pl.BlockSpec((bq, LANES), lambda i: (i, 0)), blk],
                  out_specs=(blk, full, full), out_shape=(jax.ShapeDtypeStruct((S, 256), F32),) * 3,
                  compiler_params=_params(("arbitrary",)))(z, z, z, tot, dc_all)


_HBM = pl.BlockSpec(memory_space=pltpu.HBM)
_SEM = pl.BlockSpec(memory_space=pltpu.SEMAPHORE)
_ANY = pl.BlockSpec(memory_space=pl.ANY)


def _place():
    x, y, c = lax.axis_index("x"), lax.axis_index("y"), lax.axis_index("c")
    other_chips = ((1 - x, y), (x, 1 - y), (1 - x, 1 - y))
    return x, y, c, other_chips


def _remote(src, dst, send_sems, recv_sems, k, dev):
    return pltpu.make_async_remote_copy(src_ref=src, dst_ref=dst, send_sem=send_sems.at[k], recv_sem=recv_sems.at[k],
                                        device_id=dev, device_id_type=MESH)


def _half_rows(c, n_rows):
    half = n_rows // 2
    return pl.ds(pl.multiple_of(c * half, 16), half)


def _gather_copies(src, land, send_sems, recv_sems):
    x, y, c, chips = _place()
    me = 2 * x + y
    cps = []
    for w in range(len(src)):
        rows = _half_rows(c, src[w].shape[0])
        for k, (cx, cy) in enumerate(chips):
            cps.append(_remote(src[w].at[rows], land[w].at[me, rows], send_sems, recv_sems, 3 * w + k, (cx, cy, c)))
    return cps


def _scatter_copies(src, land, send_sems, recv_sems):
    x, y, c, chips = _place()
    cps = []
    for w in range(len(src)):
        for k, (cx, cy) in enumerate(chips):
            cps.append(_remote(src[w].at[2 * cx + cy], land[w].at[k], send_sems, recv_sems, 3 * w + k, (cx, cy, c)))
    return cps


def _exchange_start(copies, srcs, lands, after, name):
    n = len(srcs)

    def body(*refs):
        src, land = refs[:n], refs[n:2 * n]
        send_sems, recv_sems = refs[2 * n + 1], refs[2 * n + 2]
        token = refs[-1]
        for cp in copies(src, land, send_sems, recv_sems):
            cp.start()
        token[...] = jnp.zeros_like(token)

    thru = tuple(pltpu.HBM(a.shape, a.dtype) for a in list(srcs) + list(lands))
    out = _pcall(body, name=name,
                 out_shape=(pltpu.SemaphoreType.DMA((3 * n,)), pltpu.SemaphoreType.DMA((3 * n,))) + thru
                 + (jax.ShapeDtypeStruct((8, LANES), F32),),
                 in_specs=[_HBM] * (2 * n) + [_ANY],
                 out_specs=(_SEM, _SEM) + (_HBM,) * (2 * n) + (pl.BlockSpec(memory_space=pltpu.VMEM),),
                 input_output_aliases={i: 2 + i for i in range(2 * n)},
                 compiler_params=pltpu.CompilerParams(has_side_effects=pltpu.SideEffectType.DATAFLOW_SIDE_EFFECTING))(
        *[pltpu.with_memory_space_constraint(a, pltpu.HBM) for a in list(srcs) + list(lands)], after)
    return out[0], out[1], list(out[2:2 + n]), list(out[2 + n:2 + 2 * n]), out[-1]


def _exchange_wait(copies, send_sems, recv_sems, srcs, lands, after, name):
    n = len(srcs)

    def body(*refs):
        src, land = refs[:n], refs[n:2 * n]
        ssem, rsem = refs[2 * n], refs[2 * n + 1]
        for cp in copies(src, land, ssem, rsem):
            cp.wait_send()
            cp.wait_recv()

    out = _pcall(body, name=name, out_shape=tuple(pltpu.HBM(a.shape, a.dtype) for a in list(srcs) + list(lands)),
                 in_specs=[_HBM] * (2 * n) + [_SEM, _SEM, _ANY], out_specs=(_HBM,) * (2 * n),
                 input_output_aliases={i: i for i in range(2 * n)},
                 compiler_params=pltpu.CompilerParams(has_side_effects=pltpu.SideEffectType.DATAFLOW_SIDE_EFFECTING))(
        *srcs, *lands, send_sems, recv_sems, after)
    return list(out[:n]), list(out[n:])


def _forward_rows(gathered):
    n = len(gathered)

    def body(*refs):
        src, dst = refs[:n], refs[n:2 * n]
        send_sems, recv_sems = refs[2 * n:]
        x, y, c, chips = _place()
        sib = (x, y, 1 - c)
        cps = []
        for w in range(n):
            K = src[w].shape[1]
            for k, (cx, cy) in enumerate(chips):
                blk = (2 * cx + cy, _half_rows(c, K))
                cps.append(_remote(src[w].at[blk], dst[w].at[blk], send_sems, recv_sems, 3 * w + k, sib))
        for cp in cps:
            cp.start()
        for w in range(n):
            K = src[w].shape[1]
            for k, (cx, cy) in enumerate(chips):
                blk = dst[w].at[2 * cx + cy, _half_rows(1 - c, K)]
                _remote(blk, blk, send_sems, recv_sems, 3 * w + k, sib).wait_recv()
        for cp in cps:
            cp.wait_send()

    return _pcall(body, name="forward_rows", in_specs=[_HBM] * n, out_specs=(_HBM,) * n,
                  out_shape=tuple(jax.ShapeDtypeStruct(g.shape, g.dtype) for g in gathered),
                  input_output_aliases={w: w for w in range(n)},
                  scratch_shapes=[pltpu.SemaphoreType.DMA((3 * n,)), pltpu.SemaphoreType.DMA((3 * n,))])(*gathered)


def _swap_rows(bufs):
    n = len(bufs)

    def body(*refs):
        src, dst = refs[:n], refs[n:2 * n]
        send_sems, recv_sems = refs[2 * n:]
        x, y, c, _ = _place()
        cps = [_remote(src[w].at[:, _half_rows(1 - c, src[w].shape[1])], dst[w], send_sems, recv_sems, w,
                       (x, y, 1 - c)) for w in range(n)]
        for cp in cps:
            cp.start()
        for cp in cps:
            cp.wait()

    return _pcall(body, name="swap_rows", in_specs=[_HBM] * n, out_specs=(_HBM,) * n,
                  out_shape=tuple(jax.ShapeDtypeStruct((b.shape[0], b.shape[1] // 2, b.shape[2]), b.dtype) for b in bufs),
                  scratch_shapes=[pltpu.SemaphoreType.DMA((n,)), pltpu.SemaphoreType.DMA((n,))])(*bufs)


def _join_rows(fins):
    n = len(fins)

    def body(*refs):
        src, dst = refs[:n], refs[n:2 * n]
        send_sems, recv_sems = refs[2 * n:]
        x, y, c, _ = _place()
        sib = (x, y, 1 - c)
        cps = []
        for w in range(n):
            blk = (slice(None), _half_rows(c, src[w].shape[1]))
            cps.append(_remote(src[w].at[blk], dst[w].at[blk], send_sems, recv_sems, w, sib))
        for cp in cps:
            cp.start()
        for w in range(n):
            blk = dst[w].at[:, _half_rows(1 - c, src[w].shape[1])]
            _remote(blk, blk, send_sems, recv_sems, w, sib).wait_recv()
        for cp in cps:
            cp.wait_send()

    return _pcall(body, name="join_rows", in_specs=[_HBM] * n, out_specs=(_HBM,) * n,
                  out_shape=tuple(jax.ShapeDtypeStruct(f.shape, f.dtype) for f in fins),
                  input_output_aliases={w: w for w in range(n)},
                  scratch_shapes=[pltpu.SemaphoreType.DMA((n,)), pltpu.SemaphoreType.DMA((n,))])(*fins)


def _add_rows(bufs, gots, c_idx):
    n = len(bufs)
    bks = [_row_block(g.shape[1], 192) for g in gots]
    nbs = [g.shape[1] // bk for g, bk in zip(gots, bks)]

    def body(c_ref, *refs):
        i = pl.program_id(1)
        for w in range(n):
            @pl.when(i < nbs[w])
            def _(w=w):
                refs[2 * n + w][...] = (refs[w][...].astype(F32) + refs[n + w][...].astype(F32)).astype(BF16)

    def mine(w):
        return pl.BlockSpec((None, bks[w], gots[w].shape[2]),
                            lambda j, i, c_ref: (j, c_ref[0] * nbs[w] + jnp.minimum(i, nbs[w] - 1), 0))

    def half(w):
        return pl.BlockSpec((None, bks[w], gots[w].shape[2]), lambda j, i, c_ref: (j, jnp.minimum(i, nbs[w] - 1), 0))

    grid_spec = pltpu.PrefetchScalarGridSpec(
        num_scalar_prefetch=1, grid=(N_CHIPS, max(nbs)),
        in_specs=[mine(w) for w in range(n)] + [half(w) for w in range(n)], out_specs=[half(w) for w in range(n)])
    return _pcall(body, name="add_rows", grid_spec=grid_spec,
                  out_shape=[jax.ShapeDtypeStruct(g.shape, BF16) for g in gots],
                  compiler_params=_params(("parallel", "arbitrary")))(c_idx, *bufs, *gots)


def _add_chip_sums(s1s, gots, idx, layer, intos, after):
    n = len(s1s)
    bks = [_row_block(a.shape[1], 64) for a in s1s]
    nbs = [a.shape[1] // bk for a, bk in zip(s1s, bks)]
    fresh = intos[0] is None

    def body(idx_ref, *refs):
        i = pl.program_id(0)
        outs = refs[len(refs) - n:]
        for w in range(n):
            @pl.when(i < nbs[w])
            def _(w=w):
                acc = refs[w][...].astype(F32)
                for k in range(3):
                    acc = acc + refs[n + w][k].astype(F32)
                outs[w][...] = acc

    def blk(w, rows):
        return pl.BlockSpec((rows, bks[w], s1s[w].shape[2]),
                            lambda i, idx_ref: (0 if rows else idx_ref[0], jnp.minimum(i, nbs[w] - 1), 0))

    def out(w):
        return pl.BlockSpec((None, bks[w], s1s[w].shape[2]),
                            lambda i, idx_ref: (layer, idx_ref[1] * nbs[w] + jnp.minimum(i, nbs[w] - 1), 0))

    ops = [idx, *s1s, *gots, after] + ([] if fresh else list(intos))
    in_specs = [blk(w, None) for w in range(n)] + [blk(w, 3) for w in range(n)] + [_ANY] * (1 if fresh else 1 + n)
    grid_spec = pltpu.PrefetchScalarGridSpec(num_scalar_prefetch=1, grid=(max(nbs),), in_specs=in_specs,
                                             out_specs=[out(w) for w in range(n)])
    return _pcall(body, name="add_chip_sums" if fresh else "add_chip_sums_into", grid_spec=grid_spec,
                  out_shape=[jax.ShapeDtypeStruct((2, 2 * a.shape[1], a.shape[2]), F32) for a in s1s],
                  input_output_aliases={} if fresh else {2 * n + 2 + w: w for w in range(n)},
                  compiler_params=_params(("arbitrary",)))(*ops)


def _allreduce_small(v, after):
    R, W = v.shape

    def body(v_ref, after_ref, o_ref, buf, send_sems, recv_sems):
        x, y, c, _ = _place()
        me = 4 * x + 2 * y + c
        buf[0] = v_ref[...]
        cps = []
        for r in range(1, 8):
            peer = (x if not r & 4 else 1 - x, y if not r & 2 else 1 - y, c if not r & 1 else 1 - c)
            cp = _remote(v_ref, buf.at[r], send_sems, recv_sems, r - 1, peer)
            cp.start()
            cps.append(cp)
        for cp in cps:
            cp.wait()
        acc = buf[me]
        for d in range(1, 8):
            acc = acc + buf[jnp.bitwise_xor(me, d)]
        o_ref[...] = acc

    return _pcall(body, name="allreduce_small", out_shape=jax.ShapeDtypeStruct((R, W), F32),
                  in_specs=[pl.BlockSpec(memory_space=pltpu.VMEM), _ANY],
                  out_specs=pl.BlockSpec(memory_space=pltpu.VMEM),
                  scratch_shapes=[pltpu.VMEM((8, R, W), F32), pltpu.SemaphoreType.DMA((7,)),
                                  pltpu.SemaphoreType.DMA((7,))])(v, after)


def _cat_cols(g):
    return g.transpose(1, 0, 2).reshape(g.shape[1], -1)


def _cut_cols(w):
    K, N = w.shape
    return w.reshape(K, N_CHIPS, N // N_CHIPS).transpose(1, 0, 2)


def _regroup_w_in_t(g):
    w = g[:, :552].reshape(2208, g.shape[2])
    zeros = lambda n: jnp.zeros((n, w.shape[1]), w.dtype)
    return jnp.concatenate([w[0:768], w[1152:1408], zeros(64), w[1408:1440], zeros(32), w[768:1152], w[1440:2208]],
                           axis=0)


def _ungroup_w_in_t(w):
    nat = jnp.concatenate([w[0:768], w[Z_Q:Z_Q + 384], w[Z_KV:Z_KV + 256], w[Z_KR + 64:Z_KR + 96], w[Z_CV:Z_W]],
                          axis=0)
    return jnp.pad(nat.reshape(N_CHIPS, 552, w.shape[1]), ((0, 0), (0, 24), (0, 0)))


def _regroup_w_uq_t(g):
    K = g.shape[2]
    return jnp.pad(g.reshape(MLA_HEADS, 96, K), ((0, 0), (0, 32), (0, 0))).reshape(MLA_HEADS * LANES, K)


def _ungroup_w_uq_t(w):
    K = w.shape[1]
    return w.reshape(MLA_HEADS, LANES, K)[:, :96].reshape(N_CHIPS, 192, K)


def _regroup_w_ukv(w):
    K = w.shape[0]
    t = w.reshape(K, MLA_HEADS, 128)
    return jnp.concatenate([jnp.pad(t[:, :, :64], ((0, 0), (0, 0), (0, 64))).reshape(K, MLA_HEADS * LANES),
                            t[:, :, 64:].reshape(K, MLA_HEADS * HEAD_DIM)], axis=1)


def _ungroup_w_ukv(w):
    K = w.shape[0]
    return jnp.concatenate([w[:, :MLA_HEADS * LANES].reshape(K, MLA_HEADS, LANES)[:, :, :64],
                            w[:, MLA_HEADS * LANES:].reshape(K, MLA_HEADS, 64)], axis=2).reshape(K, 1024)


def _ffn_fwd(x, g_pre, g_post, fetch, tag, h, g_next):
    W = fetch("ffn%sa" % tag, x)
    if h is None:
        h = _norm_fwd(x, g_pre, "norm_fwd")
    G, U, A = _ffn_gate_up(h, W["w%s_gate" % tag], W["w%s_up" % tag], "ffn_gate_up")
    W = fetch("ffn%sb" % tag, A)
    Y = _mm([(A, W["w%s_down" % tag])], "nn", F32, "ffn_down", bm=1024)
    x_new, h_next = _resid_norm(x, Y, g_post, 0.5, "resid_norm", g_next)
    return x_new, dict(x=x, h=h, G=G, U=U, A=A, Y=Y), h_next


def _ffn_bwd(dxo, sv, g_pre, g_post, wg, wu, wd_rows, ready):
    dY, dg_post = _norm_bwd(sv["Y"], g_post, dxo, 0.5, None, BF16, "norm_bwd_post")
    dG, dU = _ffn_bwd_mid(dY, wd_rows, sv["G"], sv["U"], "ffn_bwd_mid")
    cut = lambda t: t.reshape(N_CHIPS, -1, t.shape[1])
    token = ready((cut(_mm([(dG, sv["h"])], "tn", BF16, "ffn_dw_in")), cut(_mm([(dU, sv["h"])], "tn", BF16, "ffn_dw_in")),
                   cut(_mm([(sv["A"], dY)], "tn", BF16, "ffn_dw_down"))))
    dh = _mm([(dG, wg), (dU, wu)], "nn", F32, "ffn_dh")
    dx, dg_pre = _norm_bwd(sv["x"], _ordered_after(g_pre, token), dh, 1.0, dxo, F32, "norm_bwd_pre")
    return dx, dg_pre, dg_post, token


def _ordered_after(g, token):
    return g if token is None else g + token[0, 0]


GATHER_GROUPS = (("ffn1a", ("w1_gate", "w1_up")), ("ffn1b", ("w1_down",)),
                 ("mix", ("w_in", "w_mla_uq", "w_mla_ukv", "w_out")),
                 ("ffn2a", ("w2_gate", "w2_up", "w2_down", "w_ple_gate", "w_ple_proj")))
SCATTER_GROUPS = (("ffn2", ("w2_gate", "w2_up", "w2_down", "w_ple_gate", "w_ple_proj")),
                  ("mix", ("w_in", "w_mla_uq", "w_mla_ukv", "w_out")),
                  ("ffn1", ("w1_gate", "w1_up", "w1_down")))


def _layer_fwd(x, p_l, fetch, G, w_conv, cp, sp, h_in, g_next):
    sv = {}
    x1, sv["ffn1"], h2 = _ffn_fwd(x, G["g_ffn1_pre"], G["g_ffn1_post"], fetch, "1", h_in, G["g_mix_pre"])

    W = fetch("mix", x1)
    Z = _mm([(h2, W["w_in"])], "nt", F32, "mix_in", bm=1024, bn=768)
    o_sb, tot_sb = _sb_fwd(Z)
    cq, ckv = Z[:, Z_Q:Z_Q + 384], Z[:, Z_KV:Z_KV + 256]
    nq = _norm_fwd(cq, G["g_mla_q"], "norm_fwd_q")
    Qf = _mm([(nq, W["w_mla_uq"])], "nt", F32, "mla_uq", bm=1024, bn=1024)
    nkv = _norm_fwd(ckv, G["g_mla_kv"], "norm_fwd_kv")
    KV = _mm([(nkv, W["w_mla_ukv"])], "nn", F32, "mla_ukv", bm=1024, bn=768)
    qcat, kcat = _mla_prep(Qf, KV, Z, cp, sp)
    o_mla, lse = _mla_fwd(qcat, kcat, KV)
    y_cv = _conv_fwd(Z, w_conv)
    C = jnp.concatenate([o_sb, o_mla, y_cv], axis=1).astype(BF16)
    Mx = _mm([(C, W["w_out"])], "nn", F32, "mix_out", bm=1024, bn=1024)
    x2, h3 = _resid_norm(x1, Mx, G["g_mix_post"], 1.0, "resid_norm", G["g_ffn2_pre"])
    sv["mix"] = dict(x=x1, h=h2, Z=Z, tot_sb=tot_sb, cq=cq, ckv=ckv, nq=nq, nkv=nkv, qcat=qcat, kcat=kcat, KV=KV,
                     o_mla=o_mla, lse=lse, C=C, Mx=Mx)

    x3, sv["ffn2"], h4 = _ffn_fwd(x2, G["g_ffn2_pre"], G["g_ffn2_post"], fetch, "2", h3, G["g_ple_pre"])
    W = fetch("ffn2b", x3)

    p16 = p_l.astype(BF16)
    Qg = _mm([(h4, W["w_ple_gate"])], "nn", F32, "ple_gate", bm=1024, bn=1024)
    Pp = _ple_proj(p16, W["w_ple_proj"])
    e = _ew(lambda q, pp: _sigmoid(q) * pp, [Qg, Pp], [(D_MODEL, F32)], "ple_mul")
    x4, h_next = _resid_norm(x3, e, G["g_ple_post"], 1.0, "resid_norm", g_next)
    sv["ple"] = dict(x=x3, h=h4, p16=p16, Qg=Qg, Pp=Pp, e=e)
    sv["W"] = W
    return x4, sv, h_next


def _layer_bwd(dx4, sv, G, w_conv, cp, sp, emit, token):
    gg, gw, W = {}, {}, sv["W"]

    s = sv["ple"]
    de, gg["g_ple_post"] = _norm_bwd(s["e"], _ordered_after(G["g_ple_post"], token), dx4, 1.0, None, F32,
                                     "norm_bwd_e")

    def ple_bwd(dev, q, pp):
        sg = _sigmoid(q)
        return dev * pp * sg * (1.0 - sg), dev * sg

    dQg, dPp = _ew(ple_bwd, [de, s["Qg"], s["Pp"]], [(D_MODEL, BF16)] * 2, "ple_mul_bwd")
    cut = lambda t: t.reshape(N_CHIPS, -1, t.shape[1])
    gw["w_ple_proj"] = _dw_col_shards(s["p16"], dPp, "ple_dw_proj")
    gw["w_ple_gate"] = cut(_mm([(s["h"], dQg)], "tn", BF16, "ple_dw_gate"))
    dh4 = _mm([(dQg, W["w_ple_gate"])], "nt", F32, "ple_dh", bm=1024, bn=1024)
    dx3, gg["g_ple_pre"] = _norm_bwd(s["x"], G["g_ple_pre"], dh4, 1.0, dx4, F32, "norm_bwd_pre")

    def ready2(dws):
        gw["w2_gate"], gw["w2_up"], gw["w2_down"] = dws
        return emit("ffn2", gw)

    dx2, gg["g_ffn2_pre"], gg["g_ffn2_post"], token = _ffn_bwd(
        dx3, sv["ffn2"], G["g_ffn2_pre"], G["g_ffn2_post"], W["w2_gate"], W["w2_up"], W["w2_down"], ready2)

    s = sv["mix"]
    dM, gg["g_mix_post"] = _norm_bwd(s["Mx"], _ordered_after(G["g_mix_post"], token), dx2, 1.0, None, BF16,
                                     "norm_bwd_post")
    dC = _mm([(dM, W["w_out"])], "nt", F32, "mix_out_dx", bm=1024, bn=1024)
    gw["w_out"] = cut(_mm([(s["C"], dM)], "tn", BF16, "mix_out_dw"))

    db, dc, dhh, gg["w_conv"] = _conv_bwd(s["Z"], w_conv, dC)

    dqc, dkc, dv = _mla_bwd(s["qcat"], s["kcat"], s["KV"], s["o_mla"], s["lse"], dC)
    dQf, dkr = _mla_prep_bwd(dqc, dkc, cp, sp)
    gw["w_mla_uq"] = _ungroup_w_uq_t(_mm([(dQf, s["nq"])], "tn", BF16, "mla_uq_dw"))
    dnq = _mm([(dQf, W["w_mla_uq"])], "nn", F32, "mla_uq_dx", bm=1024)
    dcq, gg["g_mla_q"] = _norm_bwd(s["cq"], G["g_mla_q"], dnq, 1.0, None, F32, "norm_bwd_q")
    dkv = jnp.concatenate([dkc, dv], axis=1).astype(BF16)
    gw["w_mla_ukv"] = _cut_cols(_ungroup_w_ukv(_mm([(s["nkv"], dkv)], "tn", BF16, "mla_ukv_dw")))
    dnkv = _mm([(dkv, W["w_mla_ukv"])], "nt", F32, "mla_ukv_dx", bm=1024)
    dckv, gg["g_mla_kv"] = _norm_bwd(s["ckv"], G["g_mla_kv"], dnkv, 1.0, None, F32, "norm_bwd_kv")

    dsq, dsk, dsv = _sb_bwd(s["Z"], s["tot_sb"], dC)
    dZ = jnp.concatenate([dsq, dsk, dsv, dckv, dkr, dcq, db, dc, dhh], axis=1).astype(BF16)
    gw["w_in"] = _ungroup_w_in_t(_mm([(dZ, s["h"])], "tn", BF16, "mix_in_dw"))
    dh2 = _mm([(dZ, W["w_in"])], "nn", F32, "mix_in_dx", bm=1024, bn=1024)
    dx1, gg["g_mix_pre"] = _norm_bwd(s["x"], G["g_mix_pre"], dh2, 1.0, dx2, F32, "norm_bwd_pre")
    started_mix = emit("mix", gw)
    token = token if started_mix is None else started_mix

    def ready1(dws):
        gw["w1_gate"], gw["w1_up"], gw["w1_down"] = dws
        return emit("ffn1", gw)

    dx0, gg["g_ffn1_pre"], gg["g_ffn1_post"], started_ffn1 = _ffn_bwd(
        dx1, sv["ffn1"], G["g_ffn1_pre"], _ordered_after(G["g_ffn1_post"], token), W["w1_gate"], W["w1_up"],
        W["w1_down"], ready1)
    return dx0, gg, token if started_ffn1 is None else started_ffn1


def _pack_small(vecs):
    flat = jnp.concatenate([v.reshape(-1) for v in vecs])
    rows = -(-flat.shape[0] // (8 * LANES)) * 8
    return jnp.pad(flat, (0, rows * LANES - flat.shape[0])).reshape(rows, LANES)


def _group_weights(names, landed, shards, chip):
    out = {}
    for name, g, own in zip(names, landed, shards):
        g = lax.dynamic_update_slice(g, own[None], (chip, 0, 0))
        if name == "w_in":
            g = _regroup_w_in_t(g)
        elif name == "w_mla_uq":
            g = _regroup_w_uq_t(g)
        elif name not in ("w_mla_ukv", "w_ple_proj"):
            g = g.reshape(-1, g.shape[2])
        elif name == "w_mla_ukv":
            g = _regroup_w_ukv(_cat_cols(g))
        out[name] = g
    return out


def kernel(x, p, positions, g_ffn1_pre, w1_gate, w1_up, w1_down, g_ffn1_post, g_mix_pre, w_in, g_mla_q, w_mla_uq, g_mla_kv, w_mla_ukv, w_conv, w_out, g_mix_post, g_ffn2_pre, w2_gate, w2_up, w2_down, g_ffn2_post, g_ple_pre, w_ple_gate, w_ple_proj, g_ple_post, loss_target, m_g_ffn1_pre, m_w1_gate, m_w1_up, m_w1_down, m_g_ffn1_post, m_g_mix_pre, m_w_in, m_g_mla_q, m_w_mla_uq, m_g_mla_kv, m_w_mla_ukv, m_w_conv, m_w_out, m_g_mix_post, m_g_ffn2_pre, m_w2_gate, m_w2_up, m_w2_down, m_g_ffn2_post, m_g_ple_pre, m_w_ple_gate, m_w_ple_proj, m_g_ple_post, v_g_ffn1_pre, v_w1_gate, v_w1_up, v_w1_down, v_g_ffn1_post, v_g_mix_pre, v_w_in, v_g_mla_q, v_w_mla_uq, v_g_mla_kv, v_w_mla_ukv, v_w_conv, v_w_out, v_g_mix_post, v_g_ffn2_pre, v_w2_gate, v_w2_up, v_w2_down, v_g_ffn2_post, v_g_ple_pre, v_w_ple_gate, v_w_ple_proj, v_g_ple_post):
    w = dict(g_ffn1_pre=g_ffn1_pre, w1_gate=w1_gate, w1_up=w1_up, w1_down=w1_down, g_ffn1_post=g_ffn1_post,
             g_mix_pre=g_mix_pre, w_in=w_in, g_mla_q=g_mla_q, w_mla_uq=w_mla_uq, g_mla_kv=g_mla_kv,
             w_mla_ukv=w_mla_ukv, w_conv=w_conv, w_out=w_out, g_mix_post=g_mix_post, g_ffn2_pre=g_ffn2_pre,
             w2_gate=w2_gate, w2_up=w2_up, w2_down=w2_down, g_ffn2_post=g_ffn2_post, g_ple_pre=g_ple_pre,
             w_ple_gate=w_ple_gate, w_ple_proj=w_ple_proj, g_ple_post=g_ple_post)
    m = dict(g_ffn1_pre=m_g_ffn1_pre, w1_gate=m_w1_gate, w1_up=m_w1_up, w1_down=m_w1_down, g_ffn1_post=m_g_ffn1_post,
             g_mix_pre=m_g_mix_pre, w_in=m_w_in, g_mla_q=m_g_mla_q, w_mla_uq=m_w_mla_uq, g_mla_kv=m_g_mla_kv,
             w_mla_ukv=m_w_mla_ukv, w_conv=m_w_conv, w_out=m_w_out, g_mix_post=m_g_mix_post, g_ffn2_pre=m_g_ffn2_pre,
             w2_gate=m_w2_gate, w2_up=m_w2_up, w2_down=m_w2_down, g_ffn2_post=m_g_ffn2_post, g_ple_pre=m_g_ple_pre,
             w_ple_gate=m_w_ple_gate, w_ple_proj=m_w_ple_proj, g_ple_post=m_g_ple_post)
    v = dict(g_ffn1_pre=v_g_ffn1_pre, w1_gate=v_w1_gate, w1_up=v_w1_up, w1_down=v_w1_down, g_ffn1_post=v_g_ffn1_post,
             g_mix_pre=v_g_mix_pre, w_in=v_w_in, g_mla_q=v_g_mla_q, w_mla_uq=v_w_mla_uq, g_mla_kv=v_g_mla_kv,
             w_mla_ukv=v_w_mla_ukv, w_conv=v_w_conv, w_out=v_w_out, g_mix_post=v_g_mix_post, g_ffn2_pre=v_g_ffn2_pre,
             w2_gate=v_w2_gate, w2_up=v_w2_up, w2_down=v_w2_down, g_ffn2_post=v_g_ffn2_post, g_ple_pre=v_g_ple_pre,
             w_ple_gate=v_w_ple_gate, w_ple_proj=v_w_ple_proj, g_ple_post=v_g_ple_post)

    for name in TRANSPOSED:
        w[name], m[name], v[name] = (jnp.swapaxes(t[name], 1, 2) for t in (w, m, v))

    depth = g_ffn1_pre.shape[0]
    assert depth == 2, "the reduced gradients are assembled in [2, K, N] buffers"
    S = x.shape[1]
    cx, cy, cc = lax.axis_index("x"), lax.axis_index("y"), lax.axis_index("c")
    chip = 2 * cx + cy
    c_idx = cc.reshape(1).astype(jnp.int32)
    idx2 = jnp.stack([chip, cc]).astype(jnp.int32)

    conv_slot = lax.dynamic_update_slice(jnp.zeros((depth, 3, 256), F32),
                                         w_conv * (cc == 0).astype(F32), (0, 0, 64 * chip))
    conv_sum = _allreduce_small(_pack_small([conv_slot]), positions)
    w_conv_full = conv_sum[:depth * 3 * 256 // LANES].reshape(depth, 3, 256)

    def padded(name, i):
        (rows, _), rows_p = BIG[name]
        return jnp.pad(w[name][i].astype(BF16), ((0, rows_p - rows), (0, 0)))

    def groups_of(i, groups):
        return groups if i == 0 else (("all", BIG_NAMES),)

    started, after = {}, conv_sum
    for i in range(depth):
        for gname, names in groups_of(i, GATHER_GROUPS):
            shards = [padded(name, i) for name in names]
            lands = [lax.empty((N_CHIPS,) + s.shape, BF16) for s in shards]
            started[i, gname] = _exchange_start(_gather_copies, shards, lands, after, "gather_start_%d_%s" % (i, gname))
            after = started[i, gname][4]
    all_started = after

    inv = ROPE_BASE ** (-jnp.arange(ROPE_HALF, dtype=F32) / ROPE_HALF)
    zeros = lambda n: jnp.zeros((n,), F32)
    ones = jnp.ones((ROPE_HALF,), F32)
    inv_pat = jnp.concatenate([zeros(64), inv, inv, zeros(32)]).reshape(1, LANES)
    sign_pat = jnp.concatenate([zeros(64), -ones, ones, zeros(32)]).reshape(1, LANES)
    cp, sp = _rope_tables(positions.reshape(S, 1), inv_pat, sign_pat)

    def fetcher(i):
        table, have = dict(groups_of(i, GATHER_GROUPS)), {}

        def fetch(group, after):
            key = group if group in table else "all"
            if (i, key) in started:
                ssem, rsem, srcs, lands, _ = started.pop((i, key))
                if i == 0 and group == "ffn1a":
                    after = all_started
                srcs, landed = _exchange_wait(_gather_copies, ssem, rsem, srcs, lands, after,
                                              "gather_wait_%d_%s" % (i, key))
                have.update(_group_weights(table[key], _forward_rows(landed), srcs, chip))
            return have

        return fetch

    xs, saved, h_in = x[0], [], None
    gains = [{name: w[name][i].reshape(1, n) for name, n in GAINS} for i in range(depth)]
    for i in range(depth):
        g_next = gains[i + 1]["g_ffn1_pre"] if i + 1 < depth else None
        xs, sv, h_in = _layer_fwd(xs, p[i, 0], fetcher(i), gains[i], w_conv_full[i], cp, sp, h_in, g_next)
        saved.append(sv)

    loss_part, dx = _loss_and_grad(xs, loss_target[0])
    loss = lax.psum(loss_part[0, 0], AXES)

    scattering = []

    def emitter(i):
        table = dict(groups_of(i, SCATTER_GROUPS))

        def emit(group, gw):
            if group in table:
                names = table[group]
            elif group == "ffn1":
                names = table["all"]
            else:
                return None
            full = [gw[name] for name in names]
            pair_sums = _add_rows(full, _swap_rows(full), c_idx)
            lands = [lax.empty((3,) + s.shape[1:], BF16) for s in pair_sums]
            st = _exchange_start(_scatter_copies, pair_sums, lands, c_idx, "scatter_start_%d_%s" % (i, group))
            scattering.append((i, group, names, st))
            return st[4]

        return emit

    ggs, token = [None] * depth, None
    for i in reversed(range(depth)):
        dx, ggs[i], token = _layer_bwd(dx, saved[i], gains[i], w_conv_full[i], cp, sp, emitter(i), token)

    fins, after = {name: None for name in BIG_NAMES}, dx
    for i, group, names, (ssem, rsem, srcs, lands, _) in scattering:
        srcs, arrived = _exchange_wait(_scatter_copies, ssem, rsem, srcs, lands, after,
                                       "scatter_wait_%d_%s" % (i, group))
        sums = _add_chip_sums(srcs, arrived, idx2, i, [fins[name] for name in names], after)
        fins.update(zip(names, sums))
        after = sums[0]
    joined = _join_rows([fins[name] for name in BIG_NAMES])
    reduced = dict(zip(BIG_NAMES, joined))

    small_names = [name for name, _ in GAINS] + ["w_conv"]
    small = _allreduce_small(_pack_small([jnp.stack([ggs[i][name] for i in range(depth)]) for name in small_names]),
                             joined[0])
    flat, off = small.reshape(-1), 0
    for name, n in GAINS:
        reduced[name] = flat[off:off + depth * n].reshape(depth, n)
        off += depth * n
    conv_full = flat[off:off + depth * 3 * 256].reshape(depth, 3, 256)
    reduced["w_conv"] = lax.dynamic_slice(conv_full, (0, 0, 64 * chip), (depth, 3, 64))

    grads, deltas, new_m, new_v = {}, {}, {}, {}
    for name in WEIGHT_ORDER:
        shape = w[name].shape
        three_d = shape if len(shape) == 3 else (1,) + shape
        g_ = reduced[name] if len(shape) == 3 else reduced[name].reshape(three_d)
        outs = _adamw(w[name].reshape(three_d), g_, m[name].reshape(three_d), v[name].reshape(three_d), "adamw")
        if name in TRANSPOSED:
            outs = [jnp.swapaxes(o, 1, 2) for o in outs]
        grads[name], deltas[name], new_m[name], new_v[name] = (o.reshape(o.shape[-len(shape):]) for o in outs)

    return (loss, dx[None], *[grads[n] for n in WEIGHT_ORDER], *[deltas[n] for n in WEIGHT_ORDER],
            *[new_m[n] for n in WEIGHT_ORDER], *[new_v[n] for n in WEIGHT_ORDER])
```
